```python
import jax, jax.numpy as jnp
from jax import lax
import numpy as np

D_MODEL = 1024
BATCH = 8
SEQ = 4096
DEPTH = 2

CHUNK = 64
LN_EPS = 1e-5
GMLP_HEADS = 4
GMLP_WIDTH = D_MODEL
GMLP_HEAD_DIM = GMLP_WIDTH // GMLP_HEADS
GMLP_BLOCK = 128
POOL_WINDOWS = (2, 4, 8, 16)
POOL_GROUPS = len(POOL_WINDOWS)
POOL_WIDTH = D_MODEL
POOL_GROUP_DIM = POOL_WIDTH // POOL_GROUPS
EVEN_IN = 3 * GMLP_WIDTH + 2 * POOL_WIDTH
EVEN_MIX = GMLP_WIDTH + POOL_WIDTH
MLA_HEADS = 16
MLA_NOPE = 128
MLA_ROPE = 64
MLA_V = 128
MLA_Q_RANK = 256
MLA_KV_RANK = 128
MLA_WIDTH = MLA_HEADS * MLA_V
ODD_IN = MLA_Q_RANK + MLA_KV_RANK + MLA_ROPE + MLA_WIDTH
ROPE_THETA = 10000.0
Q_BLOCK = 128
ATTN_SCALE = (MLA_NOPE + MLA_ROPE) ** -0.5
DEEPNORM_ALPHA = (2.0 * DEPTH) ** 0.25
DEEPNORM_BETA = (8.0 * DEPTH) ** -0.25
N_EVEN = (DEPTH + 1) // 2
N_ODD = DEPTH // 2

kernel_name = "hybrid_gmlp_pool_mla_deepnorm_adaln"


def layer_norm(x, g, b):
    xf = x.astype(jnp.float32)
    mu = jnp.mean(xf, axis=-1, keepdims=True)
    var = jnp.mean(jnp.square(xf - mu), axis=-1, keepdims=True)
    return ((xf - mu) * lax.rsqrt(var + LN_EPS) * g + b).astype(x.dtype)


def rms_norm(x, g):
    xf = x.astype(jnp.float32)
    ms = jnp.mean(jnp.square(xf), axis=-1, keepdims=True)
    return (xf * lax.rsqrt(ms + LN_EPS) * g).astype(x.dtype)


def rope_cos_sin(positions):
    inv = 1.0 / (ROPE_THETA ** (jnp.arange(0, MLA_ROPE, 2, dtype=jnp.float32) / MLA_ROPE))
    ang = positions.astype(jnp.float32)[..., None] * inv
    return jnp.cos(ang), jnp.sin(ang)


def apply_rope(x, cos, sin):
    half = x.shape[-1] // 2
    x1 = x[..., :half].astype(jnp.float32)
    x2 = x[..., half:].astype(jnp.float32)
    return jnp.concatenate([x1 * cos - x2 * sin, x2 * cos + x1 * sin], axis=-1).astype(x.dtype)


def gmlp_spatial_unit(u, v, norm_g, norm_b, ws, bs):
    B, S, _ = u.shape
    nb = S // GMLP_BLOCK
    v = layer_norm(v.reshape(B, S, GMLP_HEADS, GMLP_HEAD_DIM), norm_g, norm_b)
    v = v.reshape(B, nb, GMLP_BLOCK, GMLP_HEADS, GMLP_HEAD_DIM)
    pos_chunk = jnp.arange(GMLP_BLOCK) // CHUNK
    mask = pos_chunk[None, :] <= pos_chunk[:, None]
    w = jnp.where(mask[None], ws, jnp.zeros_like(ws))
    sv = jnp.einsum('hts,bnshd->bnthd', w, v) + bs.T[:, :, None]
    return u * sv.reshape(B, S, GMLP_WIDTH)


def multiscale_pool(xb, pool_w, pool_b, pool_scale):
    B, S, _ = xb.shape
    xg = xb.reshape(B, S, POOL_GROUPS, POOL_GROUP_DIM).astype(jnp.float32)
    cs = jnp.cumsum(xg, axis=1)
    t = jnp.arange(S)
    means = []
    for g, win in enumerate(POOL_WINDOWS):
        csg = cs[:, :, g]
        lagged = jnp.concatenate([jnp.zeros((B, win, POOL_GROUP_DIM), csg.dtype), csg[:, :S - win]], axis=1)
        cnt = jnp.minimum(t + 1, win).astype(jnp.float32)
        means.append((csg - lagged) / cnt[None, :, None])
    pooled = jnp.stack(means, axis=2) - xg
    y = jnp.einsum('bsgd,gde->bsge', pooled.astype(xb.dtype), pool_w).reshape(B, S, POOL_WIDTH)
    return (y + pool_b) * pool_scale


def even_mixer(h, w_in, gmlp_norm_g, gmlp_norm_b, gmlp_ws, gmlp_bs, pool_w, pool_b, pool_scale, w_out):
    proj = h @ w_in
    u, v, z_a, x_b, z_b = jnp.split(proj, [GMLP_WIDTH, 2 * GMLP_WIDTH, 3 * GMLP_WIDTH,
                                           3 * GMLP_WIDTH + POOL_WIDTH], axis=-1)
    a = gmlp_spatial_unit(u, v, gmlp_norm_g, gmlp_norm_b, gmlp_ws, gmlp_bs) * jax.nn.silu(z_a)
    b = multiscale_pool(x_b, pool_w, pool_b, pool_scale) * jax.nn.silu(z_b)
    return jnp.concatenate([a, b], axis=-1) @ w_out


def mla_mixer(h, positions, w_in, q_norm_g, kv_norm_g, w_uq, w_uk, w_uv, w_out):
    B, S, _ = h.shape
    proj = h @ w_in
    q_c, kv_c, k_r, z = jnp.split(proj, [MLA_Q_RANK, MLA_Q_RANK + MLA_KV_RANK,
                                         MLA_Q_RANK + MLA_KV_RANK + MLA_ROPE], axis=-1)
    q_c = rms_norm(q_c, q_norm_g)
    kv_c = rms_norm(kv_c, kv_norm_g)
    q = jnp.einsum('bsr,rhd->bshd', q_c, w_uq)
    q_nope, q_rope = q[..., :MLA_NOPE], q[..., MLA_NOPE:]
    cos, sin = rope_cos_sin(positions)
    q_rope = apply_rope(q_rope, cos[:, :, None, :], sin[:, :, None, :])
    k_rope = apply_rope(k_r, cos, sin)
    q_lat = jnp.einsum('bshd,rhd->bshr', q_nope, w_uk)
    nb = S // Q_BLOCK
    q_lat_b = q_lat.reshape(B, nb, Q_BLOCK, MLA_HEADS, MLA_KV_RANK).transpose(1, 0, 2, 3, 4)
    q_rope_b = q_rope.reshape(B, nb, Q_BLOCK, MLA_HEADS, MLA_ROPE).transpose(1, 0, 2, 3, 4)
    key_chunk = jnp.arange(S) // CHUNK

    def attend_block(args):
        ql, qr, i = args
        s = jnp.einsum('bqhr,bkr->bhqk', ql, kv_c) + jnp.einsum('bqhd,bkd->bhqk', qr, k_rope)
        s = s.astype(jnp.float32) * ATTN_SCALE
        q_chunk = (i * Q_BLOCK + jnp.arange(Q_BLOCK)) // CHUNK
        mask = key_chunk[None, :] <= q_chunk[:, None]
        p = jax.nn.softmax(jnp.where(mask, s, -jnp.inf), axis=-1).astype(kv_c.dtype)
        return jnp.einsum('bhqk,bkr->bqhr', p, kv_c)

    o_lat = lax.map(attend_block, (q_lat_b, q_rope_b, jnp.arange(nb)))
    o_lat = o_lat.transpose(1, 0, 2, 3, 4).reshape(B, S, MLA_HEADS, MLA_KV_RANK)
    o = jnp.einsum('bshr,rhd->bshd', o_lat, w_uv).reshape(B, S, MLA_WIDTH)
    return (o * jax.nn.silu(z)) @ w_out


def _fwd_setup_inputs(seed: int = 0) -> dict:
    key = jax.random.key(seed)
    ks = jax.random.split(key, 24)
    f32 = jnp.float32

    def nrm(k, shape, s):
        return s * jax.random.normal(k, shape, f32)

    x = nrm(ks[0], (BATCH, SEQ, D_MODEL), 1.0)
    c = nrm(ks[1], (BATCH, D_MODEL), 1.0)
    offs = jax.random.randint(ks[2], (BATCH, 1), 0, 4096, dtype=jnp.int32)
    positions = offs + jnp.arange(SEQ, dtype=jnp.int32)[None, :]
    ada_w = nrm(ks[3], (DEPTH, D_MODEL, 3 * D_MODEL), 0.1 * D_MODEL ** -0.5)
    ada_b = nrm(ks[4], (DEPTH, 3 * D_MODEL), 0.01)
    ln_g = 1.0 + nrm(ks[5], (DEPTH, D_MODEL), 0.02)
    ln_b = nrm(ks[6], (DEPTH, D_MODEL), 0.02)
    e_w_in = nrm(ks[7], (N_EVEN, D_MODEL, EVEN_IN), D_MODEL ** -0.5)
    gmlp_norm_g = 1.0 + nrm(ks[8], (N_EVEN, GMLP_HEAD_DIM), 0.02)
    gmlp_norm_b = nrm(ks[9], (N_EVEN, GMLP_HEAD_DIM), 0.02)
    gmlp_ws = nrm(ks[10], (N_EVEN, GMLP_HEADS, GMLP_BLOCK, GMLP_BLOCK), 0.5 * GMLP_BLOCK ** -0.5)
    gmlp_bs = 1.0 + nrm(ks[11], (N_EVEN, GMLP_HEADS, GMLP_BLOCK), 0.02)
    pool_w = nrm(ks[12], (N_EVEN, POOL_GROUPS, POOL_GROUP_DIM, POOL_GROUP_DIM), POOL_GROUP_DIM ** -0.5)
    pool_b = nrm(ks[13], (N_EVEN, POOL_WIDTH), 0.01)
    pool_scale = 1.0 + nrm(ks[14], (N_EVEN, POOL_WIDTH), 0.1)
    e_w_out = nrm(ks[15], (N_EVEN, EVEN_MIX, D_MODEL), DEEPNORM_BETA * EVEN_MIX ** -0.5)
    o_w_in = nrm(ks[16], (N_ODD, D_MODEL, ODD_IN), D_MODEL ** -0.5)
    mla_q_norm_g = 1.0 + nrm(ks[17], (N_ODD, MLA_Q_RANK), 0.02)
    mla_kv_norm_g = 1.0 + nrm(ks[18], (N_ODD, MLA_KV_RANK), 0.02)
    mla_w_uq = nrm(ks[19], (N_ODD, MLA_Q_RANK, MLA_HEADS, MLA_NOPE + MLA_ROPE), MLA_Q_RANK ** -0.5)
    mla_w_uk = nrm(ks[20], (N_ODD, MLA_KV_RANK, MLA_HEADS, MLA_NOPE), MLA_KV_RANK ** -0.5)
    mla_w_uv = nrm(ks[21], (N_ODD, MLA_KV_RANK, MLA_HEADS, MLA_V), MLA_KV_RANK ** -0.5)
    o_w_out = nrm(ks[22], (N_ODD, MLA_WIDTH, D_MODEL), DEEPNORM_BETA * MLA_WIDTH ** -0.5)
    return {"x": x, "c": c, "positions": positions, "ada_w": ada_w, "ada_b": ada_b,
            "ln_g": ln_g, "ln_b": ln_b, "e_w_in": e_w_in, "gmlp_norm_g": gmlp_norm_g,
            "gmlp_norm_b": gmlp_norm_b, "gmlp_ws": gmlp_ws, "gmlp_bs": gmlp_bs,
            "pool_w": pool_w, "pool_b": pool_b, "pool_scale": pool_scale, "e_w_out": e_w_out,
            "o_w_in": o_w_in, "mla_q_norm_g": mla_q_norm_g, "mla_kv_norm_g": mla_kv_norm_g,
            "mla_w_uq": mla_w_uq, "mla_w_uk": mla_w_uk, "mla_w_uv": mla_w_uv, "o_w_out": o_w_out}


def _fwd_reference(x, c, positions, ada_w, ada_b, ln_g, ln_b, e_w_in, gmlp_norm_g, gmlp_norm_b,
              gmlp_ws, gmlp_bs, pool_w, pool_b, pool_scale, e_w_out, o_w_in, mla_q_norm_g,
              mla_kv_norm_g, mla_w_uq, mla_w_uk, mla_w_uv, o_w_out):
    cond = jax.nn.silu(c)
    for l in range(DEPTH):
        mod = cond @ ada_w[l] + ada_b[l]
        shift, scale, gate = jnp.split(mod, 3, axis=-1)
        h = x * (1.0 + scale[:, None, :]) + shift[:, None, :]
        if l % 2 == 0:
            e = l // 2
            y = even_mixer(h, e_w_in[e], gmlp_norm_g[e], gmlp_norm_b[e], gmlp_ws[e], gmlp_bs[e],
                           pool_w[e], pool_b[e], pool_scale[e], e_w_out[e])
        else:
            o = l // 2
            y = mla_mixer(h, positions, o_w_in[o], mla_q_norm_g[o], mla_kv_norm_g[o],
                          mla_w_uq[o], mla_w_uk[o], mla_w_uv[o], o_w_out[o])
        x = layer_norm(DEEPNORM_ALPHA * x + (1.0 + gate[:, None, :]) * y, ln_g[l], ln_b[l])
    return x


import jax as _jax
import jax.numpy as _jnp

TWIN_FORMAT = 'train_step'
FWD_PARAMS = ['x', 'c', 'positions', 'ada_w', 'ada_b', 'ln_g', 'ln_b', 'e_w_in', 'gmlp_norm_g', 'gmlp_norm_b', 'gmlp_ws', 'gmlp_bs', 'pool_w', 'pool_b', 'pool_scale', 'e_w_out', 'o_w_in', 'mla_q_norm_g', 'mla_kv_norm_g', 'mla_w_uq', 'mla_w_uk', 'mla_w_uv', 'o_w_out']
TWIN_WEIGHTS = ['ada_w', 'ada_b', 'ln_g', 'ln_b', 'e_w_in', 'gmlp_norm_g', 'gmlp_norm_b', 'gmlp_ws', 'gmlp_bs', 'pool_w', 'pool_b', 'pool_scale', 'e_w_out', 'o_w_in', 'mla_q_norm_g', 'mla_kv_norm_g', 'mla_w_uq', 'mla_w_uk', 'mla_w_uv', 'o_w_out']
TWIN_DIFF_INPUT = 'x'
TWIN_INPUTS = ['x', 'c', 'positions', 'ada_w', 'ada_b', 'ln_g', 'ln_b', 'e_w_in', 'gmlp_norm_g', 'gmlp_norm_b', 'gmlp_ws', 'gmlp_bs', 'pool_w', 'pool_b', 'pool_scale', 'e_w_out', 'o_w_in', 'mla_q_norm_g', 'mla_kv_norm_g', 'mla_w_uq', 'mla_w_uk', 'mla_w_uv', 'o_w_out', 'loss_target', 'm_ada_w', 'm_ada_b', 'm_ln_g', 'm_ln_b', 'm_e_w_in', 'm_gmlp_norm_g', 'm_gmlp_norm_b', 'm_gmlp_ws', 'm_gmlp_bs', 'm_pool_w', 'm_pool_b', 'm_pool_scale', 'm_e_w_out', 'm_o_w_in', 'm_mla_q_norm_g', 'm_mla_kv_norm_g', 'm_mla_w_uq', 'm_mla_w_uk', 'm_mla_w_uv', 'm_o_w_out', 'v_ada_w', 'v_ada_b', 'v_ln_g', 'v_ln_b', 'v_e_w_in', 'v_gmlp_norm_g', 'v_gmlp_norm_b', 'v_gmlp_ws', 'v_gmlp_bs', 'v_pool_w', 'v_pool_b', 'v_pool_scale', 'v_e_w_out', 'v_o_w_in', 'v_mla_q_norm_g', 'v_mla_kv_norm_g', 'v_mla_w_uq', 'v_mla_w_uk', 'v_mla_w_uv', 'v_o_w_out']
TWIN_OUTPUTS = ['loss', 'grad_x', 'grad_ada_w', 'grad_ada_b', 'grad_ln_g', 'grad_ln_b', 'grad_e_w_in', 'grad_gmlp_norm_g', 'grad_gmlp_norm_b', 'grad_gmlp_ws', 'grad_gmlp_bs', 'grad_pool_w', 'grad_pool_b', 'grad_pool_scale', 'grad_e_w_out', 'grad_o_w_in', 'grad_mla_q_norm_g', 'grad_mla_kv_norm_g', 'grad_mla_w_uq', 'grad_mla_w_uk', 'grad_mla_w_uv', 'grad_o_w_out', 'delta_ada_w', 'delta_ada_b', 'delta_ln_g', 'delta_ln_b', 'delta_e_w_in', 'delta_gmlp_norm_g', 'delta_gmlp_norm_b', 'delta_gmlp_ws', 'delta_gmlp_bs', 'delta_pool_w', 'delta_pool_b', 'delta_pool_scale', 'delta_e_w_out', 'delta_o_w_in', 'delta_mla_q_norm_g', 'delta_mla_kv_norm_g', 'delta_mla_w_uq', 'delta_mla_w_uk', 'delta_mla_w_uv', 'delta_o_w_out', 'new_m_ada_w', 'new_m_ada_b', 'new_m_ln_g', 'new_m_ln_b', 'new_m_e_w_in', 'new_m_gmlp_norm_g', 'new_m_gmlp_norm_b', 'new_m_gmlp_ws', 'new_m_gmlp_bs', 'new_m_pool_w', 'new_m_pool_b', 'new_m_pool_scale', 'new_m_e_w_out', 'new_m_o_w_in', 'new_m_mla_q_norm_g', 'new_m_mla_kv_norm_g', 'new_m_mla_w_uq', 'new_m_mla_w_uk', 'new_m_mla_w_uv', 'new_m_o_w_out', 'new_v_ada_w', 'new_v_ada_b', 'new_v_ln_g', 'new_v_ln_b', 'new_v_e_w_in', 'new_v_gmlp_norm_g', 'new_v_gmlp_norm_b', 'new_v_gmlp_ws', 'new_v_gmlp_bs', 'new_v_pool_w', 'new_v_pool_b', 'new_v_pool_scale', 'new_v_e_w_out', 'new_v_o_w_in', 'new_v_mla_q_norm_g', 'new_v_mla_kv_norm_g', 'new_v_mla_w_uq', 'new_v_mla_w_uk', 'new_v_mla_w_uv', 'new_v_o_w_out']
TWIN_LEAF_KINDS = {'loss': 'loss', 'grad_x': 'grad_x', 'grad_ada_w': 'grad_w', 'grad_ada_b': 'grad_w', 'grad_ln_g': 'grad_w', 'grad_ln_b': 'grad_w', 'grad_e_w_in': 'grad_w', 'grad_gmlp_norm_g': 'grad_w', 'grad_gmlp_norm_b': 'grad_w', 'grad_gmlp_ws': 'grad_w', 'grad_gmlp_bs': 'grad_w', 'grad_pool_w': 'grad_w', 'grad_pool_b': 'grad_w', 'grad_pool_scale': 'grad_w', 'grad_e_w_out': 'grad_w', 'grad_o_w_in': 'grad_w', 'grad_mla_q_norm_g': 'grad_w', 'grad_mla_kv_norm_g': 'grad_w', 'grad_mla_w_uq': 'grad_w', 'grad_mla_w_uk': 'grad_w', 'grad_mla_w_uv': 'grad_w', 'grad_o_w_out': 'grad_w', 'delta_ada_w': 'delta_w', 'delta_ada_b': 'delta_w', 'delta_ln_g': 'delta_w', 'delta_ln_b': 'delta_w', 'delta_e_w_in': 'delta_w', 'delta_gmlp_norm_g': 'delta_w', 'delta_gmlp_norm_b': 'delta_w', 'delta_gmlp_ws': 'delta_w', 'delta_gmlp_bs': 'delta_w', 'delta_pool_w': 'delta_w', 'delta_pool_b': 'delta_w', 'delta_pool_scale': 'delta_w', 'delta_e_w_out': 'delta_w', 'delta_o_w_in': 'delta_w', 'delta_mla_q_norm_g': 'delta_w', 'delta_mla_kv_norm_g': 'delta_w', 'delta_mla_w_uq': 'delta_w', 'delta_mla_w_uk': 'delta_w', 'delta_mla_w_uv': 'delta_w', 'delta_o_w_out': 'delta_w', 'new_m_ada_w': 'new_m', 'new_m_ada_b': 'new_m', 'new_m_ln_g': 'new_m', 'new_m_ln_b': 'new_m', 'new_m_e_w_in': 'new_m', 'new_m_gmlp_norm_g': 'new_m', 'new_m_gmlp_norm_b': 'new_m', 'new_m_gmlp_ws': 'new_m', 'new_m_gmlp_bs': 'new_m', 'new_m_pool_w': 'new_m', 'new_m_pool_b': 'new_m', 'new_m_pool_scale': 'new_m', 'new_m_e_w_out': 'new_m', 'new_m_o_w_in': 'new_m', 'new_m_mla_q_norm_g': 'new_m', 'new_m_mla_kv_norm_g': 'new_m', 'new_m_mla_w_uq': 'new_m', 'new_m_mla_w_uk': 'new_m', 'new_m_mla_w_uv': 'new_m', 'new_m_o_w_out': 'new_m', 'new_v_ada_w': 'new_v', 'new_v_ada_b': 'new_v', 'new_v_ln_g': 'new_v', 'new_v_ln_b': 'new_v', 'new_v_e_w_in': 'new_v', 'new_v_gmlp_norm_g': 'new_v', 'new_v_gmlp_norm_b': 'new_v', 'new_v_gmlp_ws': 'new_v', 'new_v_gmlp_bs': 'new_v', 'new_v_pool_w': 'new_v', 'new_v_pool_b': 'new_v', 'new_v_pool_scale': 'new_v', 'new_v_e_w_out': 'new_v', 'new_v_o_w_in': 'new_v', 'new_v_mla_q_norm_g': 'new_v', 'new_v_mla_kv_norm_g': 'new_v', 'new_v_mla_w_uq': 'new_v', 'new_v_mla_w_uk': 'new_v', 'new_v_mla_w_uv': 'new_v', 'new_v_o_w_out': 'new_v'}


def _forward(args):
    return _fwd_reference(*[args[k] for k in FWD_PARAMS])


def _output_shape():
    out = _jax.eval_shape(lambda: _forward(_fwd_setup_inputs(0)))
    return out.shape, out.dtype

N_MICROBATCH = 1
ADAM_LR = 0.001
ADAM_B1 = 0.9
ADAM_B2 = 0.999
ADAM_EPS = 1e-08
ADAM_WD = 0.01
ADAM_STEP = 10
PER_EXAMPLE_BATCH_AXIS = {'x': 0, 'c': 0, 'positions': 0, 'loss_target': 0}
SHARED_INPUTS = []
_WEIGHT_DTYPES = {'ada_w': _jnp.float32, 'ada_b': _jnp.float32, 'ln_g': _jnp.float32, 'ln_b': _jnp.float32, 'e_w_in': _jnp.float32, 'gmlp_norm_g': _jnp.float32, 'gmlp_norm_b': _jnp.float32, 'gmlp_ws': _jnp.float32, 'gmlp_bs': _jnp.float32, 'pool_w': _jnp.float32, 'pool_b': _jnp.float32, 'pool_scale': _jnp.float32, 'e_w_out': _jnp.float32, 'o_w_in': _jnp.float32, 'mla_q_norm_g': _jnp.float32, 'mla_kv_norm_g': _jnp.float32, 'mla_w_uq': _jnp.float32, 'mla_w_uk': _jnp.float32, 'mla_w_uv': _jnp.float32, 'o_w_out': _jnp.float32}
MOMENT_SCALE = {'ada_w': 2.311381e-02, 'ada_b': 3.879304e-02, 'ln_g': 2.263326e+01, 'ln_b': 5.658812e-01, 'e_w_in': 2.417253e-02, 'gmlp_norm_g': 2.209332e-02, 'gmlp_norm_b': 2.368640e-02, 'gmlp_ws': 3.219711e-02, 'gmlp_bs': 3.770264e-02, 'pool_w': 2.317198e-02, 'pool_b': 3.200564e-02, 'pool_scale': 2.347387e-02, 'e_w_out': 7.346128e-02, 'o_w_in': 7.689986e-03, 'mla_q_norm_g': 1.113173e-02, 'mla_kv_norm_g': 2.540193e-02, 'mla_w_uq': 3.360774e-03, 'mla_w_uk': 3.527925e-03, 'mla_w_uv': 4.596099e-03, 'o_w_out': 1.295443e-02}


def _to_microbatches(a, axis):
    t = _jnp.moveaxis(a, axis, 0)
    t = t.reshape((N_MICROBATCH, t.shape[0] // N_MICROBATCH) + t.shape[1:])
    return _jnp.moveaxis(t, 1, axis + 1)


def setup_inputs(seed: int = 0) -> dict:
    inp = _fwd_setup_inputs(seed)
    key = _jax.random.fold_in(_jax.random.key(seed), 7919)
    shape, _ = _output_shape()
    out = dict(inp)
    out["loss_target"] = _jax.random.normal(_jax.random.fold_in(key, 0), shape, _jnp.float32)
    for i, name in enumerate(TWIN_WEIGHTS):
        w = inp[name].astype(_jnp.float32)
        if MOMENT_SCALE is None:
            s = _jnp.sqrt(_jnp.mean(_jnp.square(w)) + 1e-30)
        else:
            s = MOMENT_SCALE[name]
        km, kv = _jax.random.split(_jax.random.fold_in(key, i + 1))
        out[name] = w
        out["m_" + name] = s * _jax.random.normal(km, w.shape, _jnp.float32)
        out["v_" + name] = (s * s) * _jax.random.uniform(kv, w.shape, _jnp.float32, 0.5, 1.5)
    if N_MICROBATCH > 1:
        for name, axis in PER_EXAMPLE_BATCH_AXIS.items():
            out[name] = _to_microbatches(out[name], axis)
    return {'x': out['x'], 'c': out['c'], 'positions': out['positions'], 'ada_w': out['ada_w'], 'ada_b': out['ada_b'], 'ln_g': out['ln_g'], 'ln_b': out['ln_b'], 'e_w_in': out['e_w_in'], 'gmlp_norm_g': out['gmlp_norm_g'], 'gmlp_norm_b': out['gmlp_norm_b'], 'gmlp_ws': out['gmlp_ws'], 'gmlp_bs': out['gmlp_bs'], 'pool_w': out['pool_w'], 'pool_b': out['pool_b'], 'pool_scale': out['pool_scale'], 'e_w_out': out['e_w_out'], 'o_w_in': out['o_w_in'], 'mla_q_norm_g': out['mla_q_norm_g'], 'mla_kv_norm_g': out['mla_kv_norm_g'], 'mla_w_uq': out['mla_w_uq'], 'mla_w_uk': out['mla_w_uk'], 'mla_w_uv': out['mla_w_uv'], 'o_w_out': out['o_w_out'], 'loss_target': out['loss_target'], 'm_ada_w': out['m_ada_w'], 'm_ada_b': out['m_ada_b'], 'm_ln_g': out['m_ln_g'], 'm_ln_b': out['m_ln_b'], 'm_e_w_in': out['m_e_w_in'], 'm_gmlp_norm_g': out['m_gmlp_norm_g'], 'm_gmlp_norm_b': out['m_gmlp_norm_b'], 'm_gmlp_ws': out['m_gmlp_ws'], 'm_gmlp_bs': out['m_gmlp_bs'], 'm_pool_w': out['m_pool_w'], 'm_pool_b': out['m_pool_b'], 'm_pool_scale': out['m_pool_scale'], 'm_e_w_out': out['m_e_w_out'], 'm_o_w_in': out['m_o_w_in'], 'm_mla_q_norm_g': out['m_mla_q_norm_g'], 'm_mla_kv_norm_g': out['m_mla_kv_norm_g'], 'm_mla_w_uq': out['m_mla_w_uq'], 'm_mla_w_uk': out['m_mla_w_uk'], 'm_mla_w_uv': out['m_mla_w_uv'], 'm_o_w_out': out['m_o_w_out'], 'v_ada_w': out['v_ada_w'], 'v_ada_b': out['v_ada_b'], 'v_ln_g': out['v_ln_g'], 'v_ln_b': out['v_ln_b'], 'v_e_w_in': out['v_e_w_in'], 'v_gmlp_norm_g': out['v_gmlp_norm_g'], 'v_gmlp_norm_b': out['v_gmlp_norm_b'], 'v_gmlp_ws': out['v_gmlp_ws'], 'v_gmlp_bs': out['v_gmlp_bs'], 'v_pool_w': out['v_pool_w'], 'v_pool_b': out['v_pool_b'], 'v_pool_scale': out['v_pool_scale'], 'v_e_w_out': out['v_e_w_out'], 'v_o_w_in': out['v_o_w_in'], 'v_mla_q_norm_g': out['v_mla_q_norm_g'], 'v_mla_kv_norm_g': out['v_mla_kv_norm_g'], 'v_mla_w_uq': out['v_mla_w_uq'], 'v_mla_w_uk': out['v_mla_w_uk'], 'v_mla_w_uv': out['v_mla_w_uv'], 'v_o_w_out': out['v_o_w_out']}


def _loss(weights, diff, rest, loss_target):
    with _jax.named_scope("forward"):
        args = {**rest, TWIN_DIFF_INPUT: diff, **{k: w.astype(_WEIGHT_DTYPES[k]) for k, w in weights.items()}}
        y = _forward(args)
    with _jax.named_scope("loss_head"):
        err = _jnp.square(y.astype(_jnp.float32) - loss_target)
        return 0.5 * _jnp.sum(_jnp.mean(err, axis=-1)) if err.ndim else 0.5 * err


def _adamw(w, g, m, v):
    m = ADAM_B1 * m + (1.0 - ADAM_B1) * g
    v = ADAM_B2 * v + (1.0 - ADAM_B2) * _jnp.square(g)
    m_hat = m / (1.0 - ADAM_B1 ** ADAM_STEP)
    v_hat = v / (1.0 - ADAM_B2 ** ADAM_STEP)
    delta = -ADAM_LR * (m_hat / (_jnp.sqrt(v_hat) + ADAM_EPS) + ADAM_WD * w)
    return delta, m, v


def reference(x, c, positions, ada_w, ada_b, ln_g, ln_b, e_w_in, gmlp_norm_g, gmlp_norm_b, gmlp_ws, gmlp_bs, pool_w, pool_b, pool_scale, e_w_out, o_w_in, mla_q_norm_g, mla_kv_norm_g, mla_w_uq, mla_w_uk, mla_w_uv, o_w_out, loss_target, m_ada_w, m_ada_b, m_ln_g, m_ln_b, m_e_w_in, m_gmlp_norm_g, m_gmlp_norm_b, m_gmlp_ws, m_gmlp_bs, m_pool_w, m_pool_b, m_pool_scale, m_e_w_out, m_o_w_in, m_mla_q_norm_g, m_mla_kv_norm_g, m_mla_w_uq, m_mla_w_uk, m_mla_w_uv, m_o_w_out, v_ada_w, v_ada_b, v_ln_g, v_ln_b, v_e_w_in, v_gmlp_norm_g, v_gmlp_norm_b, v_gmlp_ws, v_gmlp_bs, v_pool_w, v_pool_b, v_pool_scale, v_e_w_out, v_o_w_in, v_mla_q_norm_g, v_mla_kv_norm_g, v_mla_w_uq, v_mla_w_uk, v_mla_w_uv, v_o_w_out):
    given = dict(x=x, c=c, positions=positions, ada_w=ada_w, ada_b=ada_b, ln_g=ln_g, ln_b=ln_b, e_w_in=e_w_in, gmlp_norm_g=gmlp_norm_g, gmlp_norm_b=gmlp_norm_b, gmlp_ws=gmlp_ws, gmlp_bs=gmlp_bs, pool_w=pool_w, pool_b=pool_b, pool_scale=pool_scale, e_w_out=e_w_out, o_w_in=o_w_in, mla_q_norm_g=mla_q_norm_g, mla_kv_norm_g=mla_kv_norm_g, mla_w_uq=mla_w_uq, mla_w_uk=mla_w_uk, mla_w_uv=mla_w_uv, o_w_out=o_w_out, loss_target=loss_target, m_ada_w=m_ada_w, m_ada_b=m_ada_b, m_ln_g=m_ln_g, m_ln_b=m_ln_b, m_e_w_in=m_e_w_in, m_gmlp_norm_g=m_gmlp_norm_g, m_gmlp_norm_b=m_gmlp_norm_b, m_gmlp_ws=m_gmlp_ws, m_gmlp_bs=m_gmlp_bs, m_pool_w=m_pool_w, m_pool_b=m_pool_b, m_pool_scale=m_pool_scale, m_e_w_out=m_e_w_out, m_o_w_in=m_o_w_in, m_mla_q_norm_g=m_mla_q_norm_g, m_mla_kv_norm_g=m_mla_kv_norm_g, m_mla_w_uq=m_mla_w_uq, m_mla_w_uk=m_mla_w_uk, m_mla_w_uv=m_mla_w_uv, m_o_w_out=m_o_w_out, v_ada_w=v_ada_w, v_ada_b=v_ada_b, v_ln_g=v_ln_g, v_ln_b=v_ln_b, v_e_w_in=v_e_w_in, v_gmlp_norm_g=v_gmlp_norm_g, v_gmlp_norm_b=v_gmlp_norm_b, v_gmlp_ws=v_gmlp_ws, v_gmlp_bs=v_gmlp_bs, v_pool_w=v_pool_w, v_pool_b=v_pool_b, v_pool_scale=v_pool_scale, v_e_w_out=v_e_w_out, v_o_w_in=v_o_w_in, v_mla_q_norm_g=v_mla_q_norm_g, v_mla_kv_norm_g=v_mla_kv_norm_g, v_mla_w_uq=v_mla_w_uq, v_mla_w_uk=v_mla_w_uk, v_mla_w_uv=v_mla_w_uv, v_o_w_out=v_o_w_out)
    weights = {n: given[n] for n in TWIN_WEIGHTS}
    shared = {n: given[n] for n in SHARED_INPUTS}
    per_example = {n: given[n] for n in ['x', 'c', 'positions']}
    grad_fn = _jax.value_and_grad(_loss, argnums=(0, 1))

    def one_microbatch(ex, loss_target):
        ex = dict(ex)
        diff = ex.pop(TWIN_DIFF_INPUT)
        return grad_fn(weights, diff, {**shared, **ex}, loss_target)

    if N_MICROBATCH == 1:
        loss, (grad_w, grad_x) = one_microbatch(per_example, given["loss_target"])
    else:
        def body(carry, xs):
            loss_sum, grad_sum = carry
            l_k, (gw_k, gx_k) = one_microbatch(xs[0], xs[1])
            with _jax.named_scope("update"):
                return (loss_sum + l_k, _jax.tree.map(_jnp.add, grad_sum, gw_k)), gx_k

        init = (_jnp.zeros((), _jnp.float32), _jax.tree.map(_jnp.zeros_like, weights))
        (loss, grad_w), grad_x = _jax.lax.scan(body, init, (per_example, given["loss_target"]))
    with _jax.named_scope("update"):
        delta_w, new_m, new_v = {}, {}, {}
        for n in TWIN_WEIGHTS:
            delta_w[n], new_m[n], new_v[n] = _adamw(weights[n], grad_w[n], given["m_" + n], given["v_" + n])
    return (loss, grad_x, *[grad_w[n] for n in TWIN_WEIGHTS], *[delta_w[n] for n in TWIN_WEIGHTS],
            *[new_m[n] for n in TWIN_WEIGHTS], *[new_v[n] for n in TWIN_WEIGHTS])
```

```python
import functools

import jax
import jax.numpy as jnp
from jax import lax
from jax.experimental import pallas as pl
from jax.experimental.pallas import tpu as pltpu

F32 = jnp.float32
BF16 = jnp.bfloat16

D_MODEL = 1024
CHUNK = 64
LN_EPS = 1e-5
GMLP_HEADS = 4
GMLP_HEAD_DIM = 256
GMLP_BLOCK = 128
POOL_WINDOWS = (2, 4, 8, 16)
POOL_GROUP_DIM = 256
POOL_HALO = 16
MLA_HEADS = 16
MLA_NOPE = 128
MLA_ROPE = 64
MLA_Q_RANK = 256
MLA_KV_RANK = 128
MLA_WIDTH = 2048
ODD_IN = 2496
ODD_IN_PAD = 2560
ROPE_THETA = 10000.0
ATTN_SCALE = (MLA_NOPE + MLA_ROPE) ** -0.5
DEEPNORM_ALPHA = 4.0 ** 0.25
ADAM_LR, ADAM_B1, ADAM_B2, ADAM_EPS, ADAM_WD, ADAM_STEP = 0.001, 0.9, 0.999, 1e-8, 0.01, 10
N_DEV = 8
LANES = 1024
VMEM_LIMIT = 56 * 1024 * 1024
MESH = pl.DeviceIdType.MESH

NT = (((1,), (1,)), ((), ()))
NN = (((1,), (0,)), ((), ()))
TN = (((0,), (0,)), ((), ()))


def _params(n_axes):
    return pltpu.CompilerParams(dimension_semantics=("arbitrary",) * n_axes, vmem_limit_bytes=VMEM_LIMIT)


def _dot(a, b, dn):
    return lax.dot_general(a.astype(BF16), b.astype(BF16), dn, preferred_element_type=F32)


def _sigmoid(z):
    return 1.0 / (1.0 + jnp.exp(-z))


def _colsum(t):
    return jnp.sum(t, axis=0, keepdims=True)


def _matmul(pairs, mode, out_dtype, m, n, tm, tn, name):
    dn = {"nn": NN, "nt": NT, "tn": TN}[mode]
    tm, tn = min(tm, m), min(tn, n)
    n_pairs = len(pairs)

    def body(*refs):
        o_ref = refs[-1]
        acc = None
        for p in range(n_pairs):
            t = _dot(refs[2 * p][...], refs[2 * p + 1][...], dn)
            acc = t if acc is None else acc + t
        o_ref[...] = acc.astype(o_ref.dtype)

    in_specs, args = [], []
    for a, b in pairs:
        if mode == "nn":
            k = a.shape[1]
            in_specs += [pl.BlockSpec((tm, k), lambda i, j: (i, 0)), pl.BlockSpec((k, tn), lambda i, j: (0, j))]
        elif mode == "nt":
            k = a.shape[1]
            in_specs += [pl.BlockSpec((tm, k), lambda i, j: (i, 0)), pl.BlockSpec((tn, k), lambda i, j: (j, 0))]
        else:
            k = a.shape[0]
            in_specs += [pl.BlockSpec((k, tm), lambda i, j: (0, i)), pl.BlockSpec((k, tn), lambda i, j: (0, j))]
        args += [a, b]
    return pl.pallas_call(
        body, name=name, grid=(m // tm, n // tn), in_specs=in_specs,
        out_specs=pl.BlockSpec((tm, tn), lambda i, j: (i, j)),
        out_shape=jax.ShapeDtypeStruct((m, n), out_dtype), compiler_params=_params(2),
    )(*args)


def _modulate(x, mod, name):
    s, d = x.shape
    tm = min(s, 512)

    def body(x_ref, m_ref, h_ref):
        shift, scale = m_ref[0:1, 0:d], m_ref[0:1, d:2 * d]
        h_ref[...] = (x_ref[...] * (1.0 + scale) + shift).astype(BF16)

    return pl.pallas_call(
        body, name=name, grid=(s // tm,),
        in_specs=[pl.BlockSpec((tm, d), lambda i: (i, 0)), pl.BlockSpec((8, 3 * d), lambda i: (0, 0))],
        out_specs=pl.BlockSpec((tm, d), lambda i: (i, 0)),
        out_shape=jax.ShapeDtypeStruct((s, d), BF16), compiler_params=_params(1),
    )(x, mod)


def _ln_stats(r):
    mu = jnp.mean(r, axis=-1, keepdims=True)
    rc = r - mu
    var = jnp.mean(rc * rc, axis=-1, keepdims=True)
    rstd = lax.rsqrt(var + LN_EPS)
    return rc * rstd, rstd


def _ln_bwd(dxhat, xhat, rstd):
    return rstd * (dxhat - jnp.mean(dxhat, axis=-1, keepdims=True)
                   - xhat * jnp.mean(dxhat * xhat, axis=-1, keepdims=True))


def _resid_ln(x, y, mod, g, b, name):
    s, d = x.shape
    tm = min(s, 512)

    def body(x_ref, y_ref, m_ref, g_ref, b_ref, o_ref):
        gate = m_ref[0:1, 2 * d:3 * d]
        xhat, _ = _ln_stats(DEEPNORM_ALPHA * x_ref[...] + (1.0 + gate) * y_ref[...])
        o_ref[...] = xhat * g_ref[...] + b_ref[...]

    row = pl.BlockSpec((tm, d), lambda i: (i, 0))
    vec = pl.BlockSpec((1, d), lambda i: (0, 0))
    return pl.pallas_call(
        body, name=name, grid=(s // tm,),
        in_specs=[row, row, pl.BlockSpec((8, 3 * d), lambda i: (0, 0)), vec, vec],
        out_specs=row, out_shape=jax.ShapeDtypeStruct((s, d), F32), compiler_params=_params(1),
    )(x, y, mod, g, b)


def _final_ln_loss_bwd(x, y, mod, g, b, target, name):
    s, d = x.shape
    tm = min(s, 256)

    def body(x_ref, y_ref, m_ref, g_ref, b_ref, t_ref, dy_ref, dx_ref, red_ref):
        @pl.when(pl.program_id(0) == 0)
        def _():
            red_ref[...] = jnp.zeros_like(red_ref)

        gate = m_ref[0:1, 2 * d:3 * d]
        yv = y_ref[...]
        xhat, rstd = _ln_stats(DEEPNORM_ALPHA * x_ref[...] + (1.0 + gate) * yv)
        err = xhat * g_ref[...] + b_ref[...] - t_ref[...]
        dout = err * (1.0 / d)
        dr = _ln_bwd(dout * g_ref[...], xhat, rstd)
        dy_ref[...] = ((1.0 + gate) * dr).astype(BF16)
        dx_ref[...] = DEEPNORM_ALPHA * dr
        red_ref[0:1, :] += _colsum(dout * xhat)
        red_ref[1:2, :] += _colsum(dout)
        red_ref[2:3, :] += _colsum(dr * yv)
        red_ref[3:4, :] += _colsum(err * err)

    row = pl.BlockSpec((tm, d), lambda i: (i, 0))
    vec = pl.BlockSpec((1, d), lambda i: (0, 0))
    return pl.pallas_call(
        body, name=name, grid=(s // tm,),
        in_specs=[row, row, pl.BlockSpec((8, 3 * d), lambda i: (0, 0)), vec, vec, row],
        out_specs=[row, row, pl.BlockSpec((8, d), lambda i: (0, 0))],
        out_shape=[jax.ShapeDtypeStruct((s, d), BF16), jax.ShapeDtypeStruct((s, d), F32),
                   jax.ShapeDtypeStruct((8, d), F32)],
        compiler_params=_params(1),
    )(x, y, mod, g, b, target)


def _mid_bwd(dh, dxres, x, y, mod_lo, mod_hi, g, b, name):
    s, d = x.shape
    tm = min(s, 256)

    def body(dh_ref, dxr_ref, x_ref, y_ref, ml_ref, mh_ref, g_ref, b_ref, dy_ref, dx_ref, red_ref):
        @pl.when(pl.program_id(0) == 0)
        def _():
            red_ref[...] = jnp.zeros_like(red_ref)

        gate = ml_ref[0:1, 2 * d:3 * d]
        scale_hi = mh_ref[0:1, d:2 * d]
        yv, dhv = y_ref[...], dh_ref[...]
        xhat, rstd = _ln_stats(DEEPNORM_ALPHA * x_ref[...] + (1.0 + gate) * yv)
        x_mid = xhat * g_ref[...] + b_ref[...]
        dx_mid = dxr_ref[...] + dhv * (1.0 + scale_hi)
        dr = _ln_bwd(dx_mid * g_ref[...], xhat, rstd)
        dy_ref[...] = ((1.0 + gate) * dr).astype(BF16)
        dx_ref[...] = DEEPNORM_ALPHA * dr
        red_ref[0:1, :] += _colsum(dhv * x_mid)
        red_ref[1:2, :] += _colsum(dhv)
        red_ref[2:3, :] += _colsum(dx_mid * xhat)
        red_ref[3:4, :] += _colsum(dx_mid)
        red_ref[4:5, :] += _colsum(dr * yv)

    row = pl.BlockSpec((tm, d), lambda i: (i, 0))
    vec = pl.BlockSpec((1, d), lambda i: (0, 0))
    modspec = pl.BlockSpec((8, 3 * d), lambda i: (0, 0))
    return pl.pallas_call(
        body, name=name, grid=(s // tm,),
        in_specs=[row, row, row, row, modspec, modspec, vec, vec],
        out_specs=[row, row, pl.BlockSpec((8, d), lambda i: (0, 0))],
        out_shape=[jax.ShapeDtypeStruct((s, d), BF16), jax.ShapeDtypeStruct((s, d), F32),
                   jax.ShapeDtypeStruct((8, d), F32)],
        compiler_params=_params(1),
    )(dh, dxres, x, y, mod_lo, mod_hi, g, b)


def _first_bwd(dh, dxres, x, mod, name):
    s, d = x.shape
    tm = min(s, 512)

    def body(dh_ref, dxr_ref, x_ref, m_ref, gx_ref, red_ref):
        @pl.when(pl.program_id(0) == 0)
        def _():
            red_ref[...] = jnp.zeros_like(red_ref)

        scale = m_ref[0:1, d:2 * d]
        dhv = dh_ref[...]
        gx_ref[...] = dxr_ref[...] + dhv * (1.0 + scale)
        red_ref[0:1, :] += _colsum(dhv * x_ref[...])
        red_ref[1:2, :] += _colsum(dhv)

    row = pl.BlockSpec((tm, d), lambda i: (i, 0))
    return pl.pallas_call(
        body, name=name, grid=(s // tm,),
        in_specs=[row, row, row, pl.BlockSpec((8, 3 * d), lambda i: (0, 0))],
        out_specs=[row, pl.BlockSpec((8, d), lambda i: (0, 0))],
        out_shape=[jax.ShapeDtypeStruct((s, d), F32), jax.ShapeDtypeStruct((8, d), F32)],
        compiler_params=_params(1),
    )(dh, dxres, x, mod)


EVEN_TM = 256


def _gmlp_mask():
    t = lax.broadcasted_iota(jnp.int32, (GMLP_BLOCK, GMLP_BLOCK), 0) // CHUNK
    s = lax.broadcasted_iota(jnp.int32, (GMLP_BLOCK, GMLP_BLOCK), 1) // CHUNK
    return s <= t


def _window_sum(ext, win, back):
    n = ext.shape[0]
    k = 1
    while k < win:
        ext = ext + pltpu.roll(ext, k if back else n - k, 0)
        k *= 2
    return ext


def _inv_count(row0, rows, win):
    t = row0 + lax.broadcasted_iota(jnp.int32, (rows, 1), 0)
    return t, 1.0 / jnp.minimum(t + 1, win).astype(F32)


def _pooled(xb, halo, row0, win):
    tm = xb.shape[0]
    sums = _window_sum(jnp.concatenate([halo, xb], axis=0), win, True)[POOL_HALO:]
    _, inv = _inv_count(row0, tm, win)
    return sums * inv - xb


def _even_fwd(proj, ws, bs_col, ng, nb, pw, pb, ps, name):
    s = proj.shape[0]
    tm = min(s, EVEN_TM)
    hd, gd = GMLP_HEAD_DIM, POOL_GROUP_DIM

    def body(p_ref, halo_ref, ws_ref, bs_ref, ng_ref, nb_ref, pw_ref, pb_ref, ps_ref, m_ref):
        i = pl.program_id(0)
        mask = _gmlp_mask()
        for h in range(GMLP_HEADS):
            wm = jnp.where(mask, ws_ref[h], 0.0).astype(BF16)
            for blk in range(tm // GMLP_BLOCK):
                rows = slice(blk * GMLP_BLOCK, (blk + 1) * GMLP_BLOCK)
                cu, cv, cz = h * hd, 1024 + h * hd, 2048 + h * hd
                vhat, _ = _ln_stats(p_ref[rows, cv:cv + hd])
                vn = vhat * ng_ref[...] + nb_ref[...]
                sv = _dot(wm, vn, NN) + bs_ref[h]
                za = p_ref[rows, cz:cz + hd]
                m_ref[rows, cu:cu + hd] = (p_ref[rows, cu:cu + hd] * sv * (za * _sigmoid(za))).astype(BF16)
        for g, win in enumerate(POOL_WINDOWS):
            cx, cz = 3072 + g * gd, 4096 + g * gd
            halo = jnp.where(i > 0, halo_ref[:, g * gd:(g + 1) * gd], 0.0)
            pooled = _pooled(p_ref[:, cx:cx + gd], halo, i * tm, win)
            yb = _dot(pooled, pw_ref[g], NN) + pb_ref[:, g * gd:(g + 1) * gd]
            zb = p_ref[:, cz:cz + gd]
            m_ref[:, 1024 + g * gd:1024 + (g + 1) * gd] = (
                yb * ps_ref[:, g * gd:(g + 1) * gd] * (zb * _sigmoid(zb))).astype(BF16)

    hb = tm // POOL_HALO
    return pl.pallas_call(
        body, name=name, grid=(s // tm,),
        in_specs=[
            pl.BlockSpec((tm, 5120), lambda i: (i, 0)),
            pl.BlockSpec((POOL_HALO, 1024), lambda i: (jnp.maximum(i * hb - 1, 0), 3)),
            pl.BlockSpec((GMLP_HEADS, GMLP_BLOCK, GMLP_BLOCK), lambda i: (0, 0, 0)),
            pl.BlockSpec((GMLP_HEADS, GMLP_BLOCK, 1), lambda i: (0, 0, 0)),
            pl.BlockSpec((1, hd), lambda i: (0, 0)), pl.BlockSpec((1, hd), lambda i: (0, 0)),
            pl.BlockSpec((4, gd, gd), lambda i: (0, 0, 0)),
            pl.BlockSpec((1, 1024), lambda i: (0, 0)), pl.BlockSpec((1, 1024), lambda i: (0, 0)),
        ],
        out_specs=pl.BlockSpec((tm, 2048), lambda i: (i, 0)),
        out_shape=jax.ShapeDtypeStruct((s, 2048), BF16), compiler_params=_params(1),
    )(proj, proj, ws, bs_col, ng, nb, pw, pb, ps)


def _even_bwd(proj, dm, ws, bs_col, ng, nb, pw, pb, ps, name):
    s = proj.shape[0]
    tm = min(s, EVEN_TM)
    hd, gd = GMLP_HEAD_DIM, POOL_GROUP_DIM
    n_tiles = s // tm

    def body(p_ref, halo_ref, zbn_ref, dm_ref, dbn_ref, ws_ref, bs_ref, ng_ref, nb_ref, pw_ref, pb_ref, ps_ref,
             dp_ref, dws_ref, dbs_ref, dng_ref, dnb_ref, dpw_ref, dpb_ref, dps_ref):
        i = pl.program_id(0)

        @pl.when(i == 0)
        def _():
            for r in (dws_ref, dbs_ref, dng_ref, dnb_ref, dpw_ref, dpb_ref, dps_ref):
                r[...] = jnp.zeros_like(r)

        mask = _gmlp_mask()
        for h in range(GMLP_HEADS):
            wm = jnp.where(mask, ws_ref[h], 0.0).astype(BF16)
            for blk in range(tm // GMLP_BLOCK):
                rows = slice(blk * GMLP_BLOCK, (blk + 1) * GMLP_BLOCK)
                cu, cv, cz = h * hd, 1024 + h * hd, 2048 + h * hd
                vhat, rstd = _ln_stats(p_ref[rows, cv:cv + hd])
                vn = (vhat * ng_ref[...] + nb_ref[...]).astype(BF16)
                sv = _dot(wm, vn, NN) + bs_ref[h]
                u, za, da = p_ref[rows, cu:cu + hd], p_ref[rows, cz:cz + hd], dm_ref[rows, cu:cu + hd]
                sig = _sigmoid(za)
                sa = za * sig
                dau = da * u
                dsv = dau * sa
                dp_ref[rows, cu:cu + hd] = (da * sv * sa).astype(BF16)
                dp_ref[rows, cz:cz + hd] = (dau * sv * (sig * (1.0 + za * (1.0 - sig)))).astype(BF16)
                dsv_b = dsv.astype(BF16)
                dbs_ref[h] += jnp.sum(dsv, axis=1, keepdims=True)
                dws_ref[h] += _dot(dsv_b, vn, NT)
                dvn = _dot(wm, dsv_b, TN)
                dng_ref[...] += _colsum(dvn * vhat)
                dnb_ref[...] += _colsum(dvn)
                dp_ref[rows, cv:cv + hd] = _ln_bwd(dvn * ng_ref[...], vhat, rstd).astype(BF16)

        row0 = i * tm
        for g, win in enumerate(POOL_WINDOWS):
            cx, cz, cd = 3072 + g * gd, 4096 + g * gd, 1024 + g * gd
            gs = slice(g * gd, (g + 1) * gd)
            halo = jnp.where(i > 0, halo_ref[:, gs], 0.0)
            xb = p_ref[:, cx:cx + gd]
            pooled = _pooled(xb, halo, row0, win).astype(BF16)
            scale_g = ps_ref[:, gs]
            yb = _dot(pooled, pw_ref[g], NN) + pb_ref[:, gs]
            zb, db = p_ref[:, cz:cz + gd], dm_ref[:, cd:cd + gd]
            sig = _sigmoid(zb)
            dyp = db * (zb * sig)
            dp_ref[:, cz:cz + gd] = (db * yb * scale_g * (sig * (1.0 + zb * (1.0 - sig)))).astype(BF16)
            dps_ref[:, gs] += _colsum(dyp * yb)
            dpb_ref[:, gs] += _colsum(dyp * scale_g)
            zb_ext = jnp.concatenate([zb, zbn_ref[:, gs]], axis=0)
            db_ext = jnp.concatenate([db, dbn_ref[:, gs]], axis=0)
            dy_ext = (db_ext * (zb_ext * _sigmoid(zb_ext)) * scale_g).astype(BF16)
            dpw_ref[g] += _dot(pooled, dy_ext[:tm], TN)
            dpooled = _dot(dy_ext, pw_ref[g], NT)
            t, inv = _inv_count(row0, tm + POOL_HALO, win)
            w_ext = jnp.where(t < s, dpooled * inv, 0.0)
            dp_ref[:, cx:cx + gd] = (_window_sum(w_ext, win, False)[:tm] - dpooled[:tm]).astype(BF16)

    hb = tm // POOL_HALO
    last = s // POOL_HALO - 1
    small = lambda shape: pl.BlockSpec(shape, lambda i: (0,) * len(shape))
    return pl.pallas_call(
        body, name=name, grid=(n_tiles,),
        in_specs=[
            pl.BlockSpec((tm, 5120), lambda i: (i, 0)),
            pl.BlockSpec((POOL_HALO, 1024), lambda i: (jnp.maximum(i * hb - 1, 0), 3)),
            pl.BlockSpec((POOL_HALO, 1024), lambda i: (jnp.minimum((i + 1) * hb, last), 4)),
            pl.BlockSpec((tm, 2048), lambda i: (i, 0)),
            pl.BlockSpec((POOL_HALO, 1024), lambda i: (jnp.minimum((i + 1) * hb, last), 1)),
            small((GMLP_HEADS, GMLP_BLOCK, GMLP_BLOCK)), small((GMLP_HEADS, GMLP_BLOCK, 1)),
            small((1, hd)), small((1, hd)), small((4, gd, gd)), small((1, 1024)), small((1, 1024)),
        ],
        out_specs=[
            pl.BlockSpec((tm, 5120), lambda i: (i, 0)),
            small((GMLP_HEADS, GMLP_BLOCK, GMLP_BLOCK)), small((GMLP_HEADS, GMLP_BLOCK, 1)),
            small((1, hd)), small((1, hd)), small((4, gd, gd)), small((1, 1024)), small((1, 1024)),
        ],
        out_shape=[
            jax.ShapeDtypeStruct((s, 5120), BF16),
            jax.ShapeDtypeStruct((GMLP_HEADS, GMLP_BLOCK, GMLP_BLOCK), F32),
            jax.ShapeDtypeStruct((GMLP_HEADS, GMLP_BLOCK, 1), F32),
            jax.ShapeDtypeStruct((1, hd), F32), jax.ShapeDtypeStruct((1, hd), F32),
            jax.ShapeDtypeStruct((4, gd, gd), F32),
            jax.ShapeDtypeStruct((1, 1024), F32), jax.ShapeDtypeStruct((1, 1024), F32),
        ],
        compiler_params=_params(1),
    )(proj, proj, proj, dm, dm, ws, bs_col, ng, nb, pw, pb, ps)


def _rope_pair_swap(t):
    lane = lax.broadcasted_iota(jnp.int32, t.shape, 1)
    return jnp.where(lane % 64 < 32, pltpu.roll(t, 96, 1), pltpu.roll(t, 32, 1))


def _rms(x, g):
    r = lax.rsqrt(jnp.mean(x * x, axis=-1, keepdims=True) + LN_EPS)
    return x * r, r


def _rms_bwd(dy, g, xhat, r):
    dyg = dy * g
    return r * (dyg - xhat * jnp.mean(dyg * xhat, axis=-1, keepdims=True))


def _lane_lt(shape, n):
    return lax.broadcasted_iota(jnp.int32, shape, 1) < n


def _mla_prep(proj, cosp, sinp, gq, gkv, name):
    s = proj.shape[0]
    tm = min(s, 512)

    def body(qc_ref, kv_ref, c_ref, s_ref, gq_ref, gkv_ref, qn_ref, kp_ref):
        qhat, _ = _rms(qc_ref[...], None)
        qn_ref[...] = (qhat * gq_ref[...]).astype(BF16)
        khat, _ = _rms(kv_ref[:, 0:128], None)
        kp_ref[:, 0:128] = (khat * gkv_ref[...]).astype(BF16)
        kr = kv_ref[:, 128:256]
        kp_ref[:, 128:256] = (kr * c_ref[...] + _rope_pair_swap(kr) * s_ref[...]).astype(BF16)

    return pl.pallas_call(
        body, name=name, grid=(s // tm,),
        in_specs=[pl.BlockSpec((tm, 256), lambda i: (i, 0)), pl.BlockSpec((tm, 256), lambda i: (i, 1)),
                  pl.BlockSpec((tm, 128), lambda i: (i, 0)), pl.BlockSpec((tm, 128), lambda i: (i, 0)),
                  pl.BlockSpec((1, 256), lambda i: (0, 0)), pl.BlockSpec((1, 128), lambda i: (0, 0))],
        out_specs=[pl.BlockSpec((tm, 256), lambda i: (i, 0)), pl.BlockSpec((tm, 256), lambda i: (i, 0))],
        out_shape=[jax.ShapeDtypeStruct((s, 256), BF16), jax.ShapeDtypeStruct((s, 256), BF16)],
        compiler_params=_params(1),
    )(proj, proj, cosp, sinp, gq, gkv)


def _mla_prep_bwd(proj, dqn, dkp, dv, cosp, sinp, gq, gkv, name):
    s = proj.shape[0]
    tm = min(s, 512)

    def body(qc_ref, kv_ref, dqn_ref, dkp_ref, dv_ref, c_ref, s_ref, gq_ref, gkv_ref, o_ref, red_ref):
        @pl.when(pl.program_id(0) == 0)
        def _():
            red_ref[...] = jnp.zeros_like(red_ref)

        qhat, qr = _rms(qc_ref[...], None)
        dq = dqn_ref[...]
        o_ref[:, 0:256] = _rms_bwd(dq, gq_ref[...], qhat, qr).astype(BF16)
        red_ref[0:1, :] += _colsum(dq * qhat)
        khat, kr = _rms(kv_ref[:, 0:128], None)
        dk = dkp_ref[:, 0:128] + dv_ref[...]
        o_ref[:, 256:384] = _rms_bwd(dk, gkv_ref[...], khat, kr).astype(BF16)
        red_ref[1:2, 0:128] += _colsum(dk * khat)
        dr = dkp_ref[:, 128:256]
        o_ref[:, 384:512] = (dr * c_ref[...] - _rope_pair_swap(dr) * s_ref[...]).astype(BF16)

    return pl.pallas_call(
        body, name=name, grid=(s // tm,),
        in_specs=[pl.BlockSpec((tm, 256), lambda i: (i, 0)), pl.BlockSpec((tm, 256), lambda i: (i, 1)),
                  pl.BlockSpec((tm, 256), lambda i: (i, 0)), pl.BlockSpec((tm, 256), lambda i: (i, 0)),
                  pl.BlockSpec((tm, 128), lambda i: (i, 0)),
                  pl.BlockSpec((tm, 128), lambda i: (i, 0)), pl.BlockSpec((tm, 128), lambda i: (i, 0)),
                  pl.BlockSpec((1, 256), lambda i: (0, 0)), pl.BlockSpec((1, 128), lambda i: (0, 0))],
        out_specs=[pl.BlockSpec((tm, 512), lambda i: (i, 0)), pl.BlockSpec((8, 256), lambda i: (0, 0))],
        out_shape=[jax.ShapeDtypeStruct((s, 512), BF16), jax.ShapeDtypeStruct((8, 256), F32)],
        compiler_params=_params(1),
    )(proj, proj, dqn, dkp, dv, cosp, sinp, gq, gkv)


HEAD_PAIRS = MLA_HEADS // 2
PAIR_TM = 512


def _q_heads(q_up, cosp, sinp, wuk, name):
    s = q_up.shape[0]
    tm = min(s, PAIR_TM)

    def body(qn_ref, qr_ref, c_ref, s_ref, w_ref, o_ref):
        raw = qr_ref[...]
        rot = raw * c_ref[...] + _rope_pair_swap(raw) * s_ref[...]
        low = _lane_lt(rot.shape, 64)
        o_ref[:, 0:128] = _dot(qn_ref[:, 0:128], w_ref[:, 0:128], NT).astype(BF16)
        o_ref[:, 128:256] = jnp.where(low, rot, 0.0).astype(BF16)
        o_ref[:, 256:384] = _dot(qn_ref[:, 128:256], w_ref[:, 128:256], NT).astype(BF16)
        o_ref[:, 384:512] = jnp.where(low, pltpu.roll(rot, 64, 1), 0.0).astype(BF16)

    return pl.pallas_call(
        body, name=name, grid=(s // tm, HEAD_PAIRS),
        in_specs=[pl.BlockSpec((tm, 256), lambda i, p: (i, p)),
                  pl.BlockSpec((tm, 128), lambda i, p: (i, 16 + p)),
                  pl.BlockSpec((tm, 128), lambda i, p: (i, 0)), pl.BlockSpec((tm, 128), lambda i, p: (i, 0)),
                  pl.BlockSpec((128, 256), lambda i, p: (0, p))],
        out_specs=pl.BlockSpec((tm, 512), lambda i, p: (i, p)),
        out_shape=jax.ShapeDtypeStruct((s, MLA_HEADS * 256), BF16), compiler_params=_params(2),
    )(q_up, q_up, cosp, sinp, wuk)


def _q_heads_bwd(dqp, q_up, cosp, sinp, wuk, name):
    s = q_up.shape[0]
    tm = min(s, PAIR_TM)

    def body(dq_ref, qn_ref, c_ref, s_ref, w_ref, dn_ref, dr_ref, dw_ref):
        @pl.when(pl.program_id(1) == 0)
        def _():
            dw_ref[...] = jnp.zeros_like(dw_ref)

        for a in range(2):
            dlat = dq_ref[:, 256 * a:256 * a + 128]
            cols = slice(128 * a, 128 * a + 128)
            dn_ref[:, cols] = _dot(dlat, w_ref[:, cols], NN).astype(BF16)
            dw_ref[:, cols] += _dot(dlat, qn_ref[:, cols], TN)
        drot = dq_ref[:, 128:256].astype(F32) + pltpu.roll(dq_ref[:, 384:512].astype(F32), 64, 1)
        dr_ref[...] = (drot * c_ref[...] - _rope_pair_swap(drot) * s_ref[...]).astype(BF16)

    return pl.pallas_call(
        body, name=name, grid=(HEAD_PAIRS, s // tm),
        in_specs=[pl.BlockSpec((tm, 512), lambda p, i: (i, p)),
                  pl.BlockSpec((tm, 256), lambda p, i: (i, p)),
                  pl.BlockSpec((tm, 128), lambda p, i: (i, 0)), pl.BlockSpec((tm, 128), lambda p, i: (i, 0)),
                  pl.BlockSpec((128, 256), lambda p, i: (0, p))],
        out_specs=[pl.BlockSpec((tm, 256), lambda p, i: (i, p)),
                   pl.BlockSpec((tm, 128), lambda p, i: (i, p)),
                   pl.BlockSpec((128, 256), lambda p, i: (0, p))],
        out_shape=[jax.ShapeDtypeStruct((s, MLA_WIDTH), BF16), jax.ShapeDtypeStruct((s, 1024), BF16),
                   jax.ShapeDtypeStruct((128, MLA_WIDTH), F32)],
        compiler_params=_params(2),
    )(dqp, q_up, cosp, sinp, wuk)


def _o_gate(o_lat, proj, wuv, name):
    s = o_lat.shape[0]
    tm = min(s, PAIR_TM)

    def body(ol_ref, z_ref, w_ref, g_ref):
        for a in range(2):
            cols = slice(128 * a, 128 * a + 128)
            z = z_ref[:, cols]
            g_ref[:, cols] = (_dot(ol_ref[:, cols], w_ref[:, cols], NN) * (z * _sigmoid(z))).astype(BF16)

    return pl.pallas_call(
        body, name=name, grid=(s // tm, HEAD_PAIRS),
        in_specs=[pl.BlockSpec((tm, 256), lambda i, p: (i, p)),
                  pl.BlockSpec((tm, 256), lambda i, p: (i, 2 + p)),
                  pl.BlockSpec((128, 256), lambda i, p: (0, p))],
        out_specs=pl.BlockSpec((tm, 256), lambda i, p: (i, p)),
        out_shape=jax.ShapeDtypeStruct((s, MLA_WIDTH), BF16), compiler_params=_params(2),
    )(o_lat, proj, wuv)


def _o_gate_bwd(dg, o_lat, proj, wuv, name):
    s = o_lat.shape[0]
    tm = min(s, PAIR_TM)

    def body(dg_ref, ol_ref, z_ref, w_ref, dz_ref, dol_ref, dw_ref):
        @pl.when(pl.program_id(1) == 0)
        def _():
            dw_ref[...] = jnp.zeros_like(dw_ref)

        for a in range(2):
            cols = slice(128 * a, 128 * a + 128)
            z, dgv, ol = z_ref[:, cols], dg_ref[:, cols], ol_ref[:, cols]
            sig = _sigmoid(z)
            o = _dot(ol, w_ref[:, cols], NN)
            dz_ref[:, cols] = (dgv * o * (sig * (1.0 + z * (1.0 - sig)))).astype(BF16)
            do = (dgv * (z * sig)).astype(BF16)
            dol_ref[:, cols] = _dot(do, w_ref[:, cols], NT).astype(BF16)
            dw_ref[:, cols] += _dot(ol, do, TN)

    return pl.pallas_call(
        body, name=name, grid=(HEAD_PAIRS, s // tm),
        in_specs=[pl.BlockSpec((tm, 256), lambda p, i: (i, p)),
                  pl.BlockSpec((tm, 256), lambda p, i: (i, p)),
                  pl.BlockSpec((tm, 256), lambda p, i: (i, 2 + p)),
                  pl.BlockSpec((128, 256), lambda p, i: (0, p))],
        out_specs=[pl.BlockSpec((tm, 256), lambda p, i: (i, p)),
                   pl.BlockSpec((tm, 256), lambda p, i: (i, p)),
                   pl.BlockSpec((128, 256), lambda p, i: (0, p))],
        out_shape=[jax.ShapeDtypeStruct((s, MLA_WIDTH), BF16), jax.ShapeDtypeStruct((s, MLA_WIDTH), BF16),
                   jax.ShapeDtypeStruct((128, MLA_WIDTH), F32)],
        compiler_params=_params(2),
    )(dg, o_lat, proj, wuv)


ATT_TQ = CHUNK
ATT_ROWS = ATT_TQ * MLA_HEADS
ATT_TK = 256


def _visible(k0, q_chunk, tk):
    kpos = k0 + lax.broadcasted_iota(jnp.int32, (1, tk), 1)
    return kpos // CHUNK <= q_chunk


def _attn_fwd(qp, kp, name):
    s = kp.shape[0]
    tk = min(ATT_TK, s)

    def body(q_ref, k_ref, o_ref, lse_ref, m_sc, l_sc, acc_sc):
        i = pl.program_id(0)
        q = q_ref[...]
        m_sc[...] = jnp.full_like(m_sc, -jnp.inf)
        l_sc[...] = jnp.zeros_like(l_sc)
        acc_sc[...] = jnp.zeros_like(acc_sc)

        def block(j, carry):
            k0 = pl.multiple_of(j * tk, tk)
            k = k_ref[pl.ds(k0, tk), :]
            sc = jnp.where(_visible(k0, i, tk), _dot(q, k, NT) * ATTN_SCALE, -jnp.inf)
            m_new = jnp.maximum(m_sc[...], jnp.max(sc, axis=1, keepdims=True))
            alpha = jnp.exp(m_sc[...] - m_new)
            p = jnp.exp(sc - m_new)
            l_sc[...] = alpha * l_sc[...] + jnp.sum(p, axis=1, keepdims=True)
            acc_sc[...] = alpha * acc_sc[...] + _dot(p, k[:, 0:128], NN)
            m_sc[...] = m_new
            return carry

        lax.fori_loop(0, (i * ATT_TQ + ATT_TQ + tk - 1) // tk, block, 0)
        o_ref[...] = (acc_sc[...] / l_sc[...]).astype(BF16)
        lse_ref[...] = jnp.broadcast_to(m_sc[...] + jnp.log(l_sc[...]), lse_ref.shape)

    return pl.pallas_call(
        body, name=name, grid=(s // ATT_TQ,),
        in_specs=[pl.BlockSpec((ATT_ROWS, 256), lambda i: (i, 0)), pl.BlockSpec((s, 256), lambda i: (0, 0))],
        out_specs=[pl.BlockSpec((ATT_ROWS, 128), lambda i: (i, 0)), pl.BlockSpec((ATT_ROWS, 128), lambda i: (i, 0))],
        out_shape=[jax.ShapeDtypeStruct((s * MLA_HEADS, 128), BF16), jax.ShapeDtypeStruct((s * MLA_HEADS, 128), F32)],
        scratch_shapes=[pltpu.VMEM((ATT_ROWS, 1), F32), pltpu.VMEM((ATT_ROWS, 1), F32),
                        pltpu.VMEM((ATT_ROWS, 128), F32)],
        compiler_params=_params(1),
    )(qp, kp)


def _attn_bwd(qp, kp, o, do, lse, name):
    s = kp.shape[0]
    tk = min(ATT_TK, s)

    def body(q_ref, k_ref, o_ref, do_ref, lse_ref, dq_ref, dk_ref, dv_ref, dq_sc):
        i = pl.program_id(0)

        @pl.when(i == 0)
        def _():
            dk_ref[...] = jnp.zeros_like(dk_ref)
            dv_ref[...] = jnp.zeros_like(dv_ref)

        q, dov = q_ref[...], do_ref[...]
        delta = jnp.sum(dov.astype(F32) * o_ref[...].astype(F32), axis=1, keepdims=True)
        lse_col = jnp.max(lse_ref[...], axis=1, keepdims=True)
        dq_sc[...] = jnp.zeros_like(dq_sc)

        def block(j, carry):
            k0 = pl.multiple_of(j * tk, tk)
            k = k_ref[pl.ds(k0, tk), :]
            p = jnp.where(_visible(k0, i, tk), jnp.exp(_dot(q, k, NT) * ATTN_SCALE - lse_col), 0.0)
            p_b = p.astype(BF16)
            dv_ref[pl.ds(k0, tk), :] += _dot(p_b, dov, TN)
            ds = (p * (_dot(dov, k[:, 0:128], NT) - delta) * ATTN_SCALE).astype(BF16)
            dq_sc[...] += _dot(ds, k, NN)
            dk_ref[pl.ds(k0, tk), :] += _dot(ds, q, TN)
            return carry

        lax.fori_loop(0, (i * ATT_TQ + ATT_TQ + tk - 1) // tk, block, 0)
        dq_ref[...] = dq_sc[...].astype(BF16)

    tile128 = pl.BlockSpec((ATT_ROWS, 128), lambda i: (i, 0))
    return pl.pallas_call(
        body, name=name, grid=(s // ATT_TQ,),
        in_specs=[pl.BlockSpec((ATT_ROWS, 256), lambda i: (i, 0)), pl.BlockSpec((s, 256), lambda i: (0, 0)),
                  tile128, tile128, tile128],
        out_specs=[pl.BlockSpec((ATT_ROWS, 256), lambda i: (i, 0)),
                   pl.BlockSpec((s, 256), lambda i: (0, 0)), pl.BlockSpec((s, 128), lambda i: (0, 0))],
        out_shape=[jax.ShapeDtypeStruct((s * MLA_HEADS, 256), BF16),
                   jax.ShapeDtypeStruct((s, 256), F32), jax.ShapeDtypeStruct((s, 128), F32)],
        scratch_shapes=[pltpu.VMEM((ATT_ROWS, 256), F32)],
        compiler_params=_params(1),
    )(qp, kp, o, do, lse)


def _place():
    x, y, c = lax.axis_index("x"), lax.axis_index("y"), lax.axis_index("c")
    return x, y, c, 4 * x + 2 * y + c


def _flip(x, y, c, r):
    px = 1 - x if r & 4 else x
    py = 1 - y if r & 2 else y
    pc = 1 - c if r & 1 else c
    return (px, py, pc), 4 * px + 2 * py + pc


def _adaln_exchange(c8, ada_w, ada_b_cols, name):
    d = c8.shape[1]
    w_cols = ada_w.shape[2]

    def body(c_ref, w_ref, b_ref, call_ref, mod_ref, sbuf, rbuf, s1, r1, s2, r2):
        x, y, c, me = _place()
        call_ref[pl.ds(pl.multiple_of(me * 8, 8), 8), :] = c_ref[...]
        peers = [_flip(x, y, c, r) for r in range(1, N_DEV)]

        def c_copy(k, src_lin, to):
            rows = call_ref.at[pl.ds(pl.multiple_of(src_lin * 8, 8), 8), :]
            return pltpu.make_async_remote_copy(src_ref=rows, dst_ref=rows, send_sem=s1.at[k], recv_sem=r1.at[k],
                                                device_id=to, device_id_type=MESH)

        first = [c_copy(k, me, peer) for k, (peer, _) in enumerate(peers)]
        for cp in first:
            cp.start()
        for k, (_, lin) in enumerate(peers):
            c_copy(k, lin, (x, y, c)).wait_recv()
        for cp in first:
            cp.wait_send()

        for j in range(N_DEV):
            cj = call_ref[8 * j:8 * j + 8, :]
            cond = cj * _sigmoid(cj)
            for l in range(2):
                sbuf[j, l] = lax.dot_general(cond, w_ref[l], NN, precision=lax.Precision.HIGHEST,
                                             preferred_element_type=F32) + b_ref[l]

        def m_copy(k, src_slot, dst_slot, to):
            return pltpu.make_async_remote_copy(src_ref=sbuf.at[src_slot], dst_ref=rbuf.at[dst_slot],
                                                send_sem=s2.at[k], recv_sem=r2.at[k], device_id=to,
                                                device_id_type=MESH)

        rbuf[me] = sbuf[me]
        second = [m_copy(k, lin, me, peer) for k, (peer, lin) in enumerate(peers)]
        for cp in second:
            cp.start()
        for k, (_, lin) in enumerate(peers):
            m_copy(k, lin, lin, (x, y, c)).wait_recv()
        for cp in second:
            cp.wait_send()
        for j in range(N_DEV):
            for l in range(2):
                mod_ref[l, :, w_cols * j:w_cols * (j + 1)] = rbuf[j, l]

    vmem = pl.BlockSpec(memory_space=pltpu.VMEM)
    return pl.pallas_call(
        body, name=name, in_specs=[vmem, vmem, vmem], out_specs=[vmem, vmem],
        out_shape=[jax.ShapeDtypeStruct((8 * N_DEV, d), F32), jax.ShapeDtypeStruct((2, 8, 3 * d), F32)],
        scratch_shapes=[pltpu.VMEM((N_DEV, 2, 8, w_cols), F32), pltpu.VMEM((N_DEV, 2, 8, w_cols), F32),
                        pltpu.SemaphoreType.DMA((N_DEV - 1,)), pltpu.SemaphoreType.DMA((N_DEV - 1,)),
                        pltpu.SemaphoreType.DMA((N_DEV - 1,)), pltpu.SemaphoreType.DMA((N_DEV - 1,))],
        compiler_params=pltpu.CompilerParams(vmem_limit_bytes=VMEM_LIMIT),
    )(c8, ada_w, ada_b_cols)


def _all_gather(block, name):
    def body(x_ref, out_ref, send_sems, recv_sems, local_sem):
        x, y, c, _ = _place()
        me, sibling = (x, y, c), (x, y, 1 - c)
        chips = [(1 - x, y), (x, 1 - y), (1 - x, 1 - y)]

        def slot(px, py, pc):
            return out_ref.at[4 * px + 2 * py + pc]

        def copy(k, blk, to, src=None):
            return pltpu.make_async_remote_copy(src_ref=slot(*blk) if src is None else src, dst_ref=slot(*blk),
                                                send_sem=send_sems.at[k], recv_sem=recv_sems.at[k],
                                                device_id=to, device_id_type=MESH)

        mine = pltpu.make_async_copy(x_ref, slot(*me), local_sem)
        mine.start()
        first = [copy(0, me, sibling, src=x_ref)]
        first += [copy(1 + j, me, (*chip, c), src=x_ref) for j, chip in enumerate(chips)]
        for cp in first:
            cp.start()
        passed = [copy(4 + j, (*chip, c), sibling) for j, chip in enumerate(chips)]
        for j, chip in enumerate(chips):
            copy(1 + j, (*chip, c), me).wait_recv()
            passed[j].start()
        copy(0, sibling, me).wait_recv()
        for j, chip in enumerate(chips):
            copy(4 + j, (*chip, 1 - c), me).wait_recv()
        for cp in first + passed:
            cp.wait_send()
        mine.wait()

    anyspace = pl.BlockSpec(memory_space=pl.ANY)
    return pl.pallas_call(
        body, name=name, in_specs=[anyspace], out_specs=anyspace,
        out_shape=jax.ShapeDtypeStruct((N_DEV,) + block.shape, block.dtype),
        scratch_shapes=[pltpu.SemaphoreType.DMA((7,)), pltpu.SemaphoreType.DMA((7,)), pltpu.SemaphoreType.DMA],
    )(block)


def _scatter_parts(gb, gf, name):
    def body(gb_ref, gf_ref, rb_ref, rf_ref, send_sems, recv_sems, local_sems):
        x, y, c, me = _place()
        own = [pltpu.make_async_copy(gb_ref.at[me], rb_ref.at[me], local_sems.at[0]),
               pltpu.make_async_copy(gf_ref.at[me], rf_ref.at[me], local_sems.at[1])]
        for cp in own:
            cp.start()
        sends, recvs = [], []
        for r in range(1, N_DEV):
            peer, lin = _flip(x, y, c, r)
            for t, (src, dst) in enumerate(((gb_ref, rb_ref), (gf_ref, rf_ref))):
                k = 2 * (r - 1) + t
                sends.append(pltpu.make_async_remote_copy(
                    src_ref=src.at[lin], dst_ref=dst.at[me], send_sem=send_sems.at[k], recv_sem=recv_sems.at[k],
                    device_id=peer, device_id_type=MESH))
                recvs.append(pltpu.make_async_remote_copy(
                    src_ref=src.at[lin], dst_ref=dst.at[lin], send_sem=send_sems.at[k], recv_sem=recv_sems.at[k],
                    device_id=(x, y, c), device_id_type=MESH))
        for cp in sends:
            cp.start()
        for cp in recvs:
            cp.wait_recv()
        for cp in sends:
            cp.wait_send()
        for cp in own:
            cp.wait()

    anyspace = pl.BlockSpec(memory_space=pl.ANY)
    n = 2 * (N_DEV - 1)
    return pl.pallas_call(
        body, name=name, in_specs=[anyspace, anyspace], out_specs=[anyspace, anyspace],
        out_shape=[jax.ShapeDtypeStruct(gb.shape, gb.dtype), jax.ShapeDtypeStruct(gf.shape, gf.dtype)],
        scratch_shapes=[pltpu.SemaphoreType.DMA((n,)), pltpu.SemaphoreType.DMA((n,)), pltpu.SemaphoreType.DMA((2,))],
    )(gb, gf)


def _adamw(w, g, m, v):
    m = ADAM_B1 * m + (1.0 - ADAM_B1) * g
    v = ADAM_B2 * v + (1.0 - ADAM_B2) * (g * g)
    m_hat = m / (1.0 - ADAM_B1 ** ADAM_STEP)
    v_hat = v / (1.0 - ADAM_B2 ** ADAM_STEP)
    return -ADAM_LR * (m_hat / (jnp.sqrt(v_hat) + ADAM_EPS) + ADAM_WD * w), m, v


def _sum_parts_adamw(parts, w, m, v, name):
    _, rows, cols = parts.shape
    tr = max(t for t in range(16, 257, 16) if rows % t == 0)

    def body(p_ref, w_ref, m_ref, v_ref, g_ref, d_ref, mo_ref, vo_ref):
        g = p_ref[0].astype(F32)
        for j in range(1, N_DEV):
            g = g + p_ref[j].astype(F32)
        g_ref[...] = g
        d_ref[...], mo_ref[...], vo_ref[...] = _adamw(w_ref[...], g, m_ref[...], v_ref[...])

    row = pl.BlockSpec((tr, cols), lambda i: (i, 0))
    out = jax.ShapeDtypeStruct((rows, cols), F32)
    return pl.pallas_call(
        body, name=name, grid=(rows // tr,),
        in_specs=[pl.BlockSpec((N_DEV, tr, cols), lambda i: (0, i, 0)), row, row, row],
        out_specs=[row, row, row, row], out_shape=[out, out, out, out], compiler_params=_params(1),
    )(parts, w, m, v)


def _sum_parts(parts, name):
    _, rows, cols = parts.shape

    def body(p_ref, g_ref):
        g = p_ref[0]
        for j in range(1, N_DEV):
            g = g + p_ref[j]
        g_ref[...] = g

    return pl.pallas_call(
        body, name=name, out_shape=jax.ShapeDtypeStruct((rows, cols), F32),
        compiler_params=pltpu.CompilerParams(vmem_limit_bytes=VMEM_LIMIT),
    )(parts)


def _adamw_call(g, w, m, v, name):
    rows, cols = g.shape
    tr = 128 if rows % 128 == 0 else rows

    def body(g_ref, w_ref, m_ref, v_ref, d_ref, mo_ref, vo_ref):
        d_ref[...], mo_ref[...], vo_ref[...] = _adamw(w_ref[...], g_ref[...], m_ref[...], v_ref[...])

    row = pl.BlockSpec((tr, cols), lambda i: (i, 0))
    out = jax.ShapeDtypeStruct((rows, cols), F32)
    return pl.pallas_call(
        body, name=name, grid=(rows // tr,), in_specs=[row] * 4, out_specs=[row] * 3, out_shape=[out] * 3,
        compiler_params=_params(1),
    )(g, w, m, v)


def _ada_w_grad_adamw(c_all, dmod_rows, w, m, v, name):
    def body(c_ref, dm_ref, w_ref, m_ref, v_ref, g_ref, d_ref, mo_ref, vo_ref):
        cv = c_ref[...]
        cond = cv * _sigmoid(cv)
        for l in range(2):
            g = lax.dot_general(cond, dm_ref[l], TN, precision=lax.Precision.HIGHEST, preferred_element_type=F32)
            g_ref[l] = g
            d_ref[l], mo_ref[l], vo_ref[l] = _adamw(w_ref[l], g, m_ref[l], v_ref[l])

    out = jax.ShapeDtypeStruct(w.shape, F32)
    return pl.pallas_call(
        body, name=name, out_shape=[out] * 4, compiler_params=pltpu.CompilerParams(vmem_limit_bytes=VMEM_LIMIT),
    )(c_all, dmod_rows, w, m, v)


SHARDED = ("e_w_in", "pool_w", "e_w_out", "o_w_in", "mla_q_norm_g", "mla_w_uq", "o_w_out")
REPLICATED = ("ln_g", "ln_b", "gmlp_norm_g", "gmlp_norm_b", "gmlp_ws", "gmlp_bs", "pool_b", "pool_scale",
              "mla_kv_norm_g", "mla_w_uk", "mla_w_uv")


def _to_shards(name, g):
    if name in ("e_w_in", "o_w_in"):
        k, n = g.shape
        return g.reshape(k, N_DEV, n // N_DEV).transpose(1, 0, 2).reshape(N_DEV, -1)
    if name == "pool_w":
        return g.reshape(4, N_DEV, 32, 256).transpose(1, 0, 2, 3).reshape(N_DEV, -1)
    return g.reshape(N_DEV, -1)


def _from_shards(name, flat):
    if name == "e_w_in":
        return flat.reshape(N_DEV, 1024, 640).transpose(1, 0, 2).reshape(1024, 5120)
    if name == "o_w_in":
        return flat.reshape(N_DEV, 1024, 312).transpose(1, 0, 2).reshape(1024, ODD_IN)
    if name == "pool_w":
        return flat.reshape(N_DEV, 4, 32, 256).transpose(1, 0, 2, 3).reshape(4, 256, 256)
    if name in ("e_w_out", "o_w_out"):
        return flat.reshape(2048, 1024)
    if name == "mla_w_uq":
        return flat.reshape(MLA_Q_RANK, MLA_HEADS, MLA_NOPE + MLA_ROPE)
    return flat.reshape(-1)


def _pad_rows(flat2d, rows):
    n, k = flat2d.shape
    return jnp.pad(flat2d, ((0, 0), (0, rows * LANES - k))).reshape(n, rows, LANES)


def _ceil_to(a, b):
    return -(-a // b) * b


def kernel(x, c, positions, ada_w, ada_b, ln_g, ln_b, e_w_in, gmlp_norm_g, gmlp_norm_b, gmlp_ws, gmlp_bs, pool_w, pool_b, pool_scale, e_w_out, o_w_in, mla_q_norm_g, mla_kv_norm_g, mla_w_uq, mla_w_uk, mla_w_uv, o_w_out, loss_target, m_ada_w, m_ada_b, m_ln_g, m_ln_b, m_e_w_in, m_gmlp_norm_g, m_gmlp_norm_b, m_gmlp_ws, m_gmlp_bs, m_pool_w, m_pool_b, m_pool_scale, m_e_w_out, m_o_w_in, m_mla_q_norm_g, m_mla_kv_norm_g, m_mla_w_uq, m_mla_w_uk, m_mla_w_uv, m_o_w_out, v_ada_w, v_ada_b, v_ln_g, v_ln_b, v_e_w_in, v_gmlp_norm_g, v_gmlp_norm_b, v_gmlp_ws, v_gmlp_bs, v_pool_w, v_pool_b, v_pool_scale, v_e_w_out, v_o_w_in, v_mla_q_norm_g, v_mla_kv_norm_g, v_mla_w_uq, v_mla_w_uk, v_mla_w_uv, v_o_w_out):
    w_in = dict(ada_w=ada_w, ada_b=ada_b, ln_g=ln_g, ln_b=ln_b, e_w_in=e_w_in, gmlp_norm_g=gmlp_norm_g,
                gmlp_norm_b=gmlp_norm_b, gmlp_ws=gmlp_ws, gmlp_bs=gmlp_bs, pool_w=pool_w, pool_b=pool_b,
                pool_scale=pool_scale, e_w_out=e_w_out, o_w_in=o_w_in, mla_q_norm_g=mla_q_norm_g,
                mla_kv_norm_g=mla_kv_norm_g, mla_w_uq=mla_w_uq, mla_w_uk=mla_w_uk, mla_w_uv=mla_w_uv, o_w_out=o_w_out)
    m_in = dict(ada_w=m_ada_w, ada_b=m_ada_b, ln_g=m_ln_g, ln_b=m_ln_b, e_w_in=m_e_w_in, gmlp_norm_g=m_gmlp_norm_g,
                gmlp_norm_b=m_gmlp_norm_b, gmlp_ws=m_gmlp_ws, gmlp_bs=m_gmlp_bs, pool_w=m_pool_w, pool_b=m_pool_b,
                pool_scale=m_pool_scale, e_w_out=m_e_w_out, o_w_in=m_o_w_in, mla_q_norm_g=m_mla_q_norm_g,
                mla_kv_norm_g=m_mla_kv_norm_g, mla_w_uq=m_mla_w_uq, mla_w_uk=m_mla_w_uk, mla_w_uv=m_mla_w_uv,
                o_w_out=m_o_w_out)
    v_in = dict(ada_w=v_ada_w, ada_b=v_ada_b, ln_g=v_ln_g, ln_b=v_ln_b, e_w_in=v_e_w_in, gmlp_norm_g=v_gmlp_norm_g,
                gmlp_norm_b=v_gmlp_norm_b, gmlp_ws=v_gmlp_ws, gmlp_bs=v_gmlp_bs, pool_w=v_pool_w, pool_b=v_pool_b,
                pool_scale=v_pool_scale, e_w_out=v_e_w_out, o_w_in=v_o_w_in, mla_q_norm_g=v_mla_q_norm_g,
                mla_kv_norm_g=v_mla_kv_norm_g, mla_w_uq=v_mla_w_uq, mla_w_uk=v_mla_w_uk, mla_w_uv=v_mla_w_uv,
                o_w_out=v_o_w_out)
    names = list(w_in)
    seq = x.shape[1]
    d = D_MODEL
    me = 4 * lax.axis_index("x") + 2 * lax.axis_index("y") + lax.axis_index("c")
    ada_cols = ada_w.shape[2]

    ada_b_cols = lax.dynamic_slice_in_dim(ada_b, me * ada_cols, ada_cols, axis=1)
    slab_row = lax.broadcasted_iota(jnp.int32, (8, d), 0)
    slab = jnp.where(slab_row == 0, c, jnp.where(slab_row == 1, jnp.pad(mla_q_norm_g, ((0, 0), (0, d - 32))), 0.0))
    c_all, mod = _adaln_exchange(slab, ada_w,
                                 jnp.broadcast_to(ada_b_cols[:, None, :], (2, 8, ada_cols)), "adaln_exchange")

    shard_len = sum(w_in[n].size for n in SHARDED)
    shard_rows = _ceil_to(-(-shard_len // LANES), 16)

    def pack_local(src, dtype):
        flat = jnp.concatenate([src[n].reshape(-1) for n in SHARDED]).astype(dtype)
        return _pad_rows(flat[None], shard_rows)[0]

    gathered = _all_gather(pack_local(w_in, BF16), "weight_gather").reshape(N_DEV, -1)
    full, off = {}, 0
    for n in SHARDED:
        full[n] = _from_shards(n, gathered[:, off:off + w_in[n].size])
        off += w_in[n].size
    w_in_e, w_out_e, w_out_o, pool_w_full = full["e_w_in"], full["e_w_out"], full["o_w_out"], full["pool_w"]
    w_in_o = jnp.concatenate([full["o_w_in"][:, :448], jnp.zeros((d, 64), BF16), full["o_w_in"][:, 448:]], axis=1)
    w_uq = jnp.concatenate([full["mla_w_uq"][:, :, :MLA_NOPE].reshape(MLA_Q_RANK, -1),
                            full["mla_w_uq"][:, :, MLA_NOPE:].reshape(MLA_Q_RANK, -1)], axis=1)
    g_q = c_all.reshape(N_DEV, 8, d)[:, 1, :32].reshape(1, MLA_Q_RANK)

    x0, target = x[0], loss_target[0]
    ws, bs_col = gmlp_ws[0], gmlp_bs[0].reshape(GMLP_HEADS, GMLP_BLOCK, 1)
    wuk2, wuv2 = mla_w_uk[0].reshape(MLA_KV_RANK, -1), mla_w_uv[0].reshape(MLA_KV_RANK, -1)
    inv = 1.0 / (ROPE_THETA ** (jnp.arange(0, MLA_ROPE, 2, dtype=F32) / MLA_ROPE))
    ang = positions[0].astype(F32)[:, None] * inv
    cosp = jnp.tile(jnp.cos(ang), (1, 4))
    sinp = jnp.tile(jnp.concatenate([-jnp.sin(ang), jnp.sin(ang)], axis=1), (1, 2))

    h0 = _modulate(x0, mod[0], "modulate0")
    proj0 = _matmul([(h0, w_in_e)], "nn", F32, seq, 5120, 512, 640, "even_in")
    mix0 = _even_fwd(proj0, ws, bs_col, gmlp_norm_g, gmlp_norm_b, pool_w_full, pool_b, pool_scale, "even_mix")
    y0 = _matmul([(mix0, w_out_e)], "nn", F32, seq, d, 512, 1024, "even_out")
    x1 = _resid_ln(x0, y0, mod[0], ln_g[0:1], ln_b[0:1], "resid_ln0")

    h1 = _modulate(x1, mod[1], "modulate1")
    proj1 = _matmul([(h1, w_in_o)], "nn", F32, seq, ODD_IN_PAD, 512, 512, "odd_in")
    qn, kp = _mla_prep(proj1, cosp, sinp, g_q, mla_kv_norm_g, "mla_prep")
    q_up = _matmul([(qn, w_uq)], "nn", F32, seq, 3072, 512, 1024, "q_up")
    qp = _q_heads(q_up, cosp, sinp, wuk2, "q_heads")
    o_lat, lse = _attn_fwd(qp.reshape(seq * MLA_HEADS, 256), kp, "attn_fwd")
    o_lat2 = o_lat.reshape(seq, MLA_WIDTH)
    gated = _o_gate(o_lat2, proj1, wuv2, "o_gate")
    y1 = _matmul([(gated, w_out_o)], "nn", F32, seq, d, 512, 1024, "odd_out")

    dy1, dxres1, red2 = _final_ln_loss_bwd(x1, y1, mod[1], ln_g[1:2], ln_b[1:2], target, "final_ln_loss")
    dgated = _matmul([(dy1, w_out_o)], "nt", F32, seq, MLA_WIDTH, 512, 512, "odd_out_dx")
    g_w_out_o = _matmul([(gated, dy1)], "tn", F32, MLA_WIDTH, d, 256, 512, "odd_out_dw")
    dz, do_lat, g_wuv = _o_gate_bwd(dgated, o_lat2, proj1, wuv2, "o_gate_bwd")
    dqp, dkp, dvv = _attn_bwd(qp.reshape(seq * MLA_HEADS, 256), kp, o_lat, do_lat.reshape(seq * MLA_HEADS, 128), lse,
                              "attn_bwd")
    dq_nope, dq_rope, g_wuk = _q_heads_bwd(dqp.reshape(seq, MLA_HEADS * 256), q_up, cosp, sinp, wuk2, "q_heads_bwd")
    dqn = _matmul([(dq_nope, w_uq[:, :MLA_WIDTH]), (dq_rope, w_uq[:, MLA_WIDTH:])], "nt", F32, seq, MLA_Q_RANK,
                  512, 256, "q_up_dx")
    g_wuq_n = _matmul([(qn, dq_nope)], "tn", F32, MLA_Q_RANK, MLA_WIDTH, 256, 512, "q_up_dw_nope")
    g_wuq_r = _matmul([(qn, dq_rope)], "tn", F32, MLA_Q_RANK, 1024, 256, 512, "q_up_dw_rope")
    dqkr, red_mla = _mla_prep_bwd(proj1, dqn, dkp, dvv, cosp, sinp, g_q, mla_kv_norm_g, "mla_prep_bwd")
    dproj1 = jnp.concatenate([dqkr, dz], axis=1)
    dh1 = _matmul([(dproj1, w_in_o)], "nt", F32, seq, d, 512, 512, "odd_in_dx")
    g_w_in_o = _matmul([(h1, dproj1)], "tn", F32, d, ODD_IN_PAD, 256, 512, "odd_in_dw")
    dy0, dxres0, red1 = _mid_bwd(dh1, dxres1, x0, y0, mod[0], mod[1], ln_g[0:1], ln_b[0:1], "mid_bwd")
    dmix = _matmul([(dy0, w_out_e)], "nt", F32, seq, 2048, 512, 512, "even_out_dx")
    g_w_out_e = _matmul([(mix0, dy0)], "tn", F32, 2048, d, 256, 512, "even_out_dw")
    dproj0, g_ws, g_bs, g_ng, g_nb, g_pw, g_pb, g_ps = _even_bwd(
        proj0, dmix, ws, bs_col, gmlp_norm_g, gmlp_norm_b, pool_w_full, pool_b, pool_scale, "even_mix_bwd")
    dh0 = _matmul([(dproj0, w_in_e)], "nt", F32, seq, d, 512, 512, "even_in_dx")
    g_w_in_e = _matmul([(h0, dproj0)], "tn", F32, d, 5120, 256, 640, "even_in_dw")
    grad_x, red0 = _first_bwd(dh0, dxres0, x0, mod[0], "first_bwd")

    loss = lax.psum(0.5 / d * jnp.sum(red2[3]), ("x", "y", "c"))

    t_mask = lax.broadcasted_iota(jnp.int32, (GMLP_BLOCK, GMLP_BLOCK), 0) // CHUNK
    s_mask = lax.broadcasted_iota(jnp.int32, (GMLP_BLOCK, GMLP_BLOCK), 1) // CHUNK
    part = {
        "e_w_in": g_w_in_e, "pool_w": g_pw, "e_w_out": g_w_out_e,
        "o_w_in": jnp.concatenate([g_w_in_o[:, :448], g_w_in_o[:, 512:]], axis=1),
        "mla_q_norm_g": red_mla[0],
        "mla_w_uq": jnp.concatenate([g_wuq_n.reshape(MLA_Q_RANK, MLA_HEADS, MLA_NOPE),
                                     g_wuq_r.reshape(MLA_Q_RANK, MLA_HEADS, MLA_ROPE)], axis=2),
        "o_w_out": g_w_out_o,
        "ln_g": jnp.stack([red1[2], red2[0]]), "ln_b": jnp.stack([red1[3], red2[1]]),
        "gmlp_norm_g": g_ng, "gmlp_norm_b": g_nb,
        "gmlp_ws": jnp.where(s_mask <= t_mask, g_ws, 0.0), "gmlp_bs": g_bs,
        "pool_b": g_pb, "pool_scale": g_ps, "mla_kv_norm_g": red_mla[1, :MLA_KV_RANK],
        "mla_w_uk": g_wuk, "mla_w_uv": g_wuv,
    }
    dmod = jnp.stack([jnp.concatenate([red0[1], red0[0], red1[4]]),
                      jnp.concatenate([red1[1], red1[0], red2[2]])])

    gb = _pad_rows(jnp.concatenate([_to_shards(n, part[n]) for n in SHARDED], axis=1).astype(BF16), shard_rows)
    rep_len = sum(w_in[n].size for n in REPLICATED)
    chunk = rep_len // N_DEV
    chunk_rows = -(-chunk // LANES)
    rep_rows = _ceil_to(chunk_rows + 1, 8)

    def pack_rep(src, extra):
        body = _pad_rows(jnp.concatenate([src[n].reshape(-1) for n in REPLICATED]).reshape(N_DEV, chunk), chunk_rows)
        tail = _pad_rows(extra.reshape(2, N_DEV, ada_cols).transpose(1, 0, 2).reshape(N_DEV, -1),
                         rep_rows - chunk_rows)
        return jnp.concatenate([body, tail], axis=1)

    rb, rf = _scatter_parts(gb, pack_rep(part, dmod), "grad_scatter")

    g_sh, d_sh, m_sh, v_sh = _sum_parts_adamw(rb, pack_local(w_in, F32), pack_local(m_in, F32), pack_local(v_in, F32),
                                              "shard_sum_adamw")
    rep_sum = _all_gather(_sum_parts(rf, "replicated_sum"), "replicated_gather").reshape(N_DEV * rep_rows, LANES)
    rep_w, rep_m, rep_v = (pack_rep(src, src["ada_b"]).reshape(N_DEV * rep_rows, LANES) for src in (w_in, m_in, v_in))
    d_rep, m_rep, v_rep = _adamw_call(rep_sum, rep_w, rep_m, rep_v, "replicated_adamw")
    dmod_all = rf[:, chunk_rows, :2 * ada_cols].reshape(N_DEV, 2, ada_cols).transpose(1, 0, 2)
    dmod_rows = jnp.pad(dmod_all[:, :, None, :], ((0, 0), (0, 0), (0, 7), (0, 0))).reshape(2, 8 * N_DEV, ada_cols)
    g_ada, d_ada, m_ada, v_ada = _ada_w_grad_adamw(c_all, dmod_rows, ada_w, m_ada_w, v_ada_w, "ada_w_adamw")

    def unpack(sharded_flat, rep_flat, ada):
        out = {"ada_w": ada}
        flat, off = sharded_flat.reshape(-1), 0
        for n in SHARDED:
            out[n] = flat[off:off + w_in[n].size].reshape(w_in[n].shape)
            off += w_in[n].size
        rep3 = rep_flat.reshape(N_DEV, rep_rows, LANES)
        flat, off = rep3[:, :chunk_rows].reshape(N_DEV, -1)[:, :chunk].reshape(-1), 0
        for n in REPLICATED:
            out[n] = flat[off:off + w_in[n].size].reshape(w_in[n].shape)
            off += w_in[n].size
        out["ada_b"] = rep3[:, chunk_rows, :2 * ada_cols].reshape(N_DEV, 2, ada_cols).transpose(1, 0, 2).reshape(2, -1)
        return [out[n] for n in names]

    return (loss, grad_x[None], *unpack(g_sh, rep_sum, g_ada), *unpack(d_sh, d_rep, d_ada),
            *unpack(m_sh, m_rep, m_ada), *unpack(v_sh, v_rep, v_ada))
```

```python
import functools

import jax
import jax.numpy as jnp
from jax import lax
from jax.experimental import pallas as pl
from jax.experimental.pallas import tpu as pltpu

F32 = jnp.float32
BF16 = jnp.bfloat16

D_MODEL = 1024
CHUNK = 64
LN_EPS = 1e-5
GMLP_HEADS = 4
GMLP_HEAD_DIM = 256
GMLP_BLOCK = 128
POOL_WINDOWS = (2, 4, 8, 16)
POOL_GROUP_DIM = 256
POOL_HALO = 16
MLA_HEADS = 16
MLA_NOPE = 128
MLA_ROPE = 64
MLA_Q_RANK = 256
MLA_KV_RANK = 128
MLA_WIDTH = 2048
ODD_IN = 2496
ODD_IN_PAD = 2560
ROPE_THETA = 10000.0
ATTN_SCALE = (MLA_NOPE + MLA_ROPE) ** -0.5
DEEPNORM_ALPHA = 4.0 ** 0.25
ADAM_LR, ADAM_B1, ADAM_B2, ADAM_EPS, ADAM_WD, ADAM_STEP = 0.001, 0.9, 0.999, 1e-8, 0.01, 10
N_DEV = 8
LANES = 1024
VMEM_LIMIT = 56 * 1024 * 1024
MESH = pl.DeviceIdType.MESH

NT = (((1,), (1,)), ((), ()))
NN = (((1,), (0,)), ((), ()))
TN = (((0,), (0,)), ((), ()))


def _params(n_axes):
    return pltpu.CompilerParams(dimension_semantics=("arbitrary",) * n_axes, vmem_limit_bytes=VMEM_LIMIT)


def _dot(a, b, dn):
    return lax.dot_general(a.astype(BF16), b.astype(BF16), dn, preferred_element_type=F32)


def _sigmoid(z):
    return 1.0 / (1.0 + jnp.exp(-z))


def _colsum(t):
    return jnp.sum(t, axis=0, keepdims=True)


def _matmul(pairs, mode, out_dtype, m, n, tm, tn, name):
    dn = {"nn": NN, "nt": NT, "tn": TN}[mode]
    tm, tn = min(tm, m), min(tn, n)
    n_pairs = len(pairs)

    def body(*refs):
        o_ref = refs[-1]
        acc = None
        for p in range(n_pairs):
            t = _dot(refs[2 * p][...], refs[2 * p + 1][...], dn)
            acc = t if acc is None else acc + t
        o_ref[...] = acc.astype(o_ref.dtype)

    in_specs, args = [], []
    for a, b in pairs:
        if mode == "nn":
            k = a.shape[1]
            in_specs += [pl.BlockSpec((tm, k), lambda i, j: (i, 0)), pl.BlockSpec((k, tn), lambda i, j: (0, j))]
        elif mode == "nt":
            k = a.shape[1]
            in_specs += [pl.BlockSpec((tm, k), lambda i, j: (i, 0)), pl.BlockSpec((tn, k), lambda i, j: (j, 0))]
        else:
            k = a.shape[0]
            in_specs += [pl.BlockSpec((k, tm), lambda i, j: (0, i)), pl.BlockSpec((k, tn), lambda i, j: (0, j))]
        args += [a, b]
    return pl.pallas_call(
        body, name=name, grid=(m // tm, n // tn), in_specs=in_specs,
        out_specs=pl.BlockSpec((tm, tn), lambda i, j: (i, j)),
        out_shape=jax.ShapeDtypeStruct((m, n), out_dtype), compiler_params=_params(2),
    )(*args)


def _modulate(x, mod, name):
    s, d = x.shape
    tm = min(s, 512)

    def body(x_ref, m_ref, h_ref):
        shift, scale = m_ref[0:1, 0:d], m_ref[0:1, d:2 * d]
        h_ref[...] = (x_ref[...] * (1.0 + scale) + shift).astype(BF16)

    return pl.pallas_call(
        body, name=name, grid=(s // tm,),
        in_specs=[pl.BlockSpec((tm, d), lambda i: (i, 0)), pl.BlockSpec((8, 3 * d), lambda i: (0, 0))],
        out_specs=pl.BlockSpec((tm, d), lambda i: (i, 0)),
        out_shape=jax.ShapeDtypeStruct((s, d), BF16), compiler_params=_params(1),
    )(x, mod)


def _ln_stats(r):
    mu = jnp.mean(r, axis=-1, keepdims=True)
    rc = r - mu
    var = jnp.mean(rc * rc, axis=-1, keepdims=True)
    rstd = lax.rsqrt(var + LN_EPS)
    return rc * rstd, rstd


def _ln_bwd(dxhat, xhat, rstd):
    return rstd * (dxhat - jnp.mean(dxhat, axis=-1, keepdims=True)
                   - xhat * jnp.mean(dxhat * xhat, axis=-1, keepdims=True))


def _resid_ln(x, y, mod, g, b, name):
    s, d = x.shape
    tm = min(s, 512)

    def body(x_ref, y_ref, m_ref, g_ref, b_ref, o_ref):
        gate = m_ref[0:1, 2 * d:3 * d]
        xhat, _ = _ln_stats(DEEPNORM_ALPHA * x_ref[...] + (1.0 + gate) * y_ref[...])
        o_ref[...] = xhat * g_ref[...] + b_ref[...]

    row = pl.BlockSpec((tm, d), lambda i: (i, 0))
    vec = pl.BlockSpec((1, d), lambda i: (0, 0))
    return pl.pallas_call(
        body, name=name, grid=(s // tm,),
        in_specs=[row, row, pl.BlockSpec((8, 3 * d), lambda i: (0, 0)), vec, vec],
        out_specs=row, out_shape=jax.ShapeDtypeStruct((s, d), F32), compiler_params=_params(1),
    )(x, y, mod, g, b)


def _final_ln_loss_bwd(x, y, mod, g, b, target, name):
    s, d = x.shape
    tm = min(s, 256)

    def body(x_ref, y_ref, m_ref, g_ref, b_ref, t_ref, dy_ref, dx_ref, red_ref):
        @pl.when(pl.program_id(0) == 0)
        def _():
            red_ref[...] = jnp.zeros_like(red_ref)

        gate = m_ref[0:1, 2 * d:3 * d]
        yv = y_ref[...]
        xhat, rstd = _ln_stats(DEEPNORM_ALPHA * x_ref[...] + (1.0 + gate) * yv)
        err = xhat * g_ref[...] + b_ref[...] - t_ref[...]
        dout = err * (1.0 / d)
        dr = _ln_bwd(dout * g_ref[...], xhat, rstd)
        dy_ref[...] = ((1.0 + gate) * dr).astype(BF16)
        dx_ref[...] = DEEPNORM_ALPHA * dr
        red_ref[0:1, :] += _colsum(dout * xhat)
        red_ref[1:2, :] += _colsum(dout)
        red_ref[2:3, :] += _colsum(dr * yv)
        red_ref[3:4, :] += _colsum(err * err)

    row = pl.BlockSpec((tm, d), lambda i: (i, 0))
    vec = pl.BlockSpec((1, d), lambda i: (0, 0))
    return pl.pallas_call(
        body, name=name, grid=(s // tm,),
        in_specs=[row, row, pl.BlockSpec((8, 3 * d), lambda i: (0, 0)), vec, vec, row],
        out_specs=[row, row, pl.BlockSpec((8, d), lambda i: (0, 0))],
        out_shape=[jax.ShapeDtypeStruct((s, d), BF16), jax.ShapeDtypeStruct((s, d), F32),
                   jax.ShapeDtypeStruct((8, d), F32)],
        compiler_params=_params(1),
    )(x, y, mod, g, b, target)


def _mid_bwd(dh, dxres, x, y, mod_lo, mod_hi, g, b, name):
    s, d = x.shape
    tm = min(s, 256)

    def body(dh_ref, dxr_ref, x_ref, y_ref, ml_ref, mh_ref, g_ref, b_ref, dy_ref, dx_ref, red_ref):
        @pl.when(pl.program_id(0) == 0)
        def _():
            red_ref[...] = jnp.zeros_like(red_ref)

        gate = ml_ref[0:1, 2 * d:3 * d]
        scale_hi = mh_ref[0:1, d:2 * d]
        yv, dhv = y_ref[...], dh_ref[...]
        xhat, rstd = _ln_stats(DEEPNORM_ALPHA * x_ref[...] + (1.0 + gate) * yv)
        x_mid = xhat * g_ref[...] + b_ref[...]
        dx_mid = dxr_ref[...] + dhv * (1.0 + scale_hi)
        dr = _ln_bwd(dx_mid * g_ref[...], xhat, rstd)
        dy_ref[...] = ((1.0 + gate) * dr).astype(BF16)
        dx_ref[...] = DEEPNORM_ALPHA * dr
        red_ref[0:1, :] += _colsum(dhv * x_mid)
        red_ref[1:2, :] += _colsum(dhv)
        red_ref[2:3, :] += _colsum(dx_mid * xhat)
        red_ref[3:4, :] += _colsum(dx_mid)
        red_ref[4:5, :] += _colsum(dr * yv)

    row = pl.BlockSpec((tm, d), lambda i: (i, 0))
    vec = pl.BlockSpec((1, d), lambda i: (0, 0))
    modspec = pl.BlockSpec((8, 3 * d), lambda i: (0, 0))
    return pl.pallas_call(
        body, name=name, grid=(s // tm,),
        in_specs=[row, row, row, row, modspec, modspec, vec, vec],
        out_specs=[row, row, pl.BlockSpec((8, d), lambda i: (0, 0))],
        out_shape=[jax.ShapeDtypeStruct((s, d), BF16), jax.ShapeDtypeStruct((s, d), F32),
                   jax.ShapeDtypeStruct((8, d), F32)],
        compiler_params=_params(1),
    )(dh, dxres, x, y, mod_lo, mod_hi, g, b)


def _first_bwd(dh, dxres, x, mod, name):
    s, d = x.shape
    tm = min(s, 512)

    def body(dh_ref, dxr_ref, x_ref, m_ref, gx_ref, red_ref):
        @pl.when(pl.program_id(0) == 0)
        def _():
            red_ref[...] = jnp.zeros_like(red_ref)

        scale = m_ref[0:1, d:2 * d]
        dhv = dh_ref[...]
        gx_ref[...] = dxr_ref[...] + dhv * (1.0 + scale)
        red_ref[0:1, :] += _colsum(dhv * x_ref[...])
        red_ref[1:2, :] += _colsum(dhv)

    row = pl.BlockSpec((tm, d), lambda i: (i, 0))
    return pl.pallas_call(
        body, name=name, grid=(s // tm,),
        in_specs=[row, row, row, pl.BlockSpec((8, 3 * d), lambda i: (0, 0))],
        out_specs=[row, pl.BlockSpec((8, d), lambda i: (0, 0))],
        out_shape=[jax.ShapeDtypeStruct((s, d), F32), jax.ShapeDtypeStruct((8, d), F32)],
        compiler_params=_params(1),
    )(dh, dxres, x, mod)


EVEN_TM = 256


def _gmlp_mask():
    t = lax.broadcasted_iota(jnp.int32, (GMLP_BLOCK, GMLP_BLOCK), 0) // CHUNK
    s = lax.broadcasted_iota(jnp.int32, (GMLP_BLOCK, GMLP_BLOCK), 1) // CHUNK
    return s <= t


def _window_sum(ext, win, back):
    n = ext.shape[0]
    k = 1
    while k < win:
        ext = ext + pltpu.roll(ext, k if back else n - k, 0)
        k *= 2
    return ext


def _inv_count(row0, rows, win):
    t = row0 + lax.broadcasted_iota(jnp.int32, (rows, 1), 0)
    return t, 1.0 / jnp.minimum(t + 1, win).astype(F32)


def _pooled(xb, halo, row0, win):
    tm = xb.shape[0]
    sums = _window_sum(jnp.concatenate([halo, xb], axis=0), win, True)[POOL_HALO:]
    _, inv = _inv_count(row0, tm, win)
    return sums * inv - xb


def _even_fwd(proj, ws, bs_col, ng, nb, pw, pb, ps, name):
    s = proj.shape[0]
    tm = min(s, EVEN_TM)
    hd, gd = GMLP_HEAD_DIM, POOL_GROUP_DIM

    def body(p_ref, halo_ref, ws_ref, bs_ref, ng_ref, nb_ref, pw_ref, pb_ref, ps_ref, m_ref):
        i = pl.program_id(0)
        mask = _gmlp_mask()
        for h in range(GMLP_HEADS):
            wm = jnp.where(mask, ws_ref[h], 0.0).astype(BF16)
            for blk in range(tm // GMLP_BLOCK):
                rows = slice(blk * GMLP_BLOCK, (blk + 1) * GMLP_BLOCK)
                cu, cv, cz = h * hd, 1024 + h * hd, 2048 + h * hd
                vhat, _ = _ln_stats(p_ref[rows, cv:cv + hd])
                vn = vhat * ng_ref[...] + nb_ref[...]
                sv = _dot(wm, vn, NN) + bs_ref[h]
                za = p_ref[rows, cz:cz + hd]
                m_ref[rows, cu:cu + hd] = (p_ref[rows, cu:cu + hd] * sv * (za * _sigmoid(za))).astype(BF16)
        for g, win in enumerate(POOL_WINDOWS):
            cx, cz = 3072 + g * gd, 4096 + g * gd
            halo = jnp.where(i > 0, halo_ref[:, g * gd:(g + 1) * gd], 0.0)
            pooled = _pooled(p_ref[:, cx:cx + gd], halo, i * tm, win)
            yb = _dot(pooled, pw_ref[g], NN) + pb_ref[:, g * gd:(g + 1) * gd]
            zb = p_ref[:, cz:cz + gd]
            m_ref[:, 1024 + g * gd:1024 + (g + 1) * gd] = (
                yb * ps_ref[:, g * gd:(g + 1) * gd] * (zb * _sigmoid(zb))).astype(BF16)

    hb = tm // POOL_HALO
    return pl.pallas_call(
        body, name=name, grid=(s // tm,),
        in_specs=[
            pl.BlockSpec((tm, 5120), lambda i: (i, 0)),
            pl.BlockSpec((POOL_HALO, 1024), lambda i: (jnp.maximum(i * hb - 1, 0), 3)),
            pl.BlockSpec((GMLP_HEADS, GMLP_BLOCK, GMLP_BLOCK), lambda i: (0, 0, 0)),
            pl.BlockSpec((GMLP_HEADS, GMLP_BLOCK, 1), lambda i: (0, 0, 0)),
            pl.BlockSpec((1, hd), lambda i: (0, 0)), pl.BlockSpec((1, hd), lambda i: (0, 0)),
            pl.BlockSpec((4, gd, gd), lambda i: (0, 0, 0)),
            pl.BlockSpec((1, 1024), lambda i: (0, 0)), pl.BlockSpec((1, 1024), lambda i: (0, 0)),
        ],
        out_specs=pl.BlockSpec((tm, 2048), lambda i: (i, 0)),
        out_shape=jax.ShapeDtypeStruct((s, 2048), BF16), compiler_params=_params(1),
    )(proj, proj, ws, bs_col, ng, nb, pw, pb, ps)


def _even_bwd(proj, dm, ws, bs_col, ng, nb, pw, pb, ps, name):
    s = proj.shape[0]
    tm = min(s, EVEN_TM)
    hd, gd = GMLP_HEAD_DIM, POOL_GROUP_DIM
    n_tiles = s // tm

    def body(p_ref, halo_ref, zbn_ref, dm_ref, dbn_ref, ws_ref, bs_ref, ng_ref, nb_ref, pw_ref, pb_ref, ps_ref,
             dp_ref, dws_ref, dbs_ref, dng_ref, dnb_ref, dpw_ref, dpb_ref, dps_ref):
        i = pl.program_id(0)

        @pl.when(i == 0)
        def _():
            for r in (dws_ref, dbs_ref, dng_ref, dnb_ref, dpw_ref, dpb_ref, dps_ref):
                r[...] = jnp.zeros_like(r)

        mask = _gmlp_mask()
        for h in range(GMLP_HEADS):
            wm = jnp.where(mask, ws_ref[h], 0.0).astype(BF16)
            for blk in range(tm // GMLP_BLOCK):
                rows = slice(blk * GMLP_BLOCK, (blk + 1) * GMLP_BLOCK)
                cu, cv, cz = h * hd, 1024 + h * hd, 2048 + h * hd
                vhat, rstd = _ln_stats(p_ref[rows, cv:cv + hd])
                vn = (vhat * ng_ref[...] + nb_ref[...]).astype(BF16)
                sv = _dot(wm, vn, NN) + bs_ref[h]
                u, za, da = p_ref[rows, cu:cu + hd], p_ref[rows, cz:cz + hd], dm_ref[rows, cu:cu + hd]
                sig = _sigmoid(za)
                sa = za * sig
                dau = da * u
                dsv = dau * sa
                dp_ref[rows, cu:cu + hd] = (da * sv * sa).astype(BF16)
                dp_ref[rows, cz:cz + hd] = (dau * sv * (sig * (1.0 + za * (1.0 - sig)))).astype(BF16)
                dsv_b = dsv.astype(BF16)
                dbs_ref[h] += jnp.sum(dsv, axis=1, keepdims=True)
                dws_ref[h] += _dot(dsv_b, vn, NT)
                dvn = _dot(wm, dsv_b, TN)
                dng_ref[...] += _colsum(dvn * vhat)
                dnb_ref[...] += _colsum(dvn)
                dp_ref[rows, cv:cv + hd] = _ln_bwd(dvn * ng_ref[...], vhat, rstd).astype(BF16)

        row0 = i * tm
        for g, win in enumerate(POOL_WINDOWS):
            cx, cz, cd = 3072 + g * gd, 4096 + g * gd, 1024 + g * gd
            gs = slice(g * gd, (g + 1) * gd)
            halo = jnp.where(i > 0, halo_ref[:, gs], 0.0)
            xb = p_ref[:, cx:cx + gd]
            pooled = _pooled(xb, halo, row0, win).astype(BF16)
            scale_g = ps_ref[:, gs]
            yb = _dot(pooled, pw_ref[g], NN) + pb_ref[:, gs]
            zb, db = p_ref[:, cz:cz + gd], dm_ref[:, cd:cd + gd]
            sig = _sigmoid(zb)
            dyp = db * (zb * sig)
            dp_ref[:, cz:cz + gd] = (db * yb * scale_g * (sig * (1.0 + zb * (1.0 - sig)))).astype(BF16)
            dps_ref[:, gs] += _colsum(dyp * yb)
            dpb_ref[:, gs] += _colsum(dyp * scale_g)
            zb_ext = jnp.concatenate([zb, zbn_ref[:, gs]], axis=0)
            db_ext = jnp.concatenate([db, dbn_ref[:, gs]], axis=0)
            dy_ext = (db_ext * (zb_ext * _sigmoid(zb_ext)) * scale_g).astype(BF16)
            dpw_ref[g] += _dot(pooled, dy_ext[:tm], TN)
            dpooled = _dot(dy_ext, pw_ref[g], NT)
            t, inv = _inv_count(row0, tm + POOL_HALO, win)
            w_ext = jnp.where(t < s, dpooled * inv, 0.0)
            dp_ref[:, cx:cx + gd] = (_window_sum(w_ext, win, False)[:tm] - dpooled[:tm]).astype(BF16)

    hb = tm // POOL_HALO
    last = s // POOL_HALO - 1
    small = lambda shape: pl.BlockSpec(shape, lambda i: (0,) * len(shape))
    return pl.pallas_call(
        body, name=name, grid=(n_tiles,),
        in_specs=[
            pl.BlockSpec((tm, 5120), lambda i: (i, 0)),
            pl.BlockSpec((POOL_HALO, 1024), lambda i: (jnp.maximum(i * hb - 1, 0), 3)),
            pl.BlockSpec((POOL_HALO, 1024), lambda i: (jnp.minimum((i + 1) * hb, last), 4)),
            pl.BlockSpec((tm, 2048), lambda i: (i, 0)),
            pl.BlockSpec((POOL_HALO, 1024), lambda i: (jnp.minimum((i + 1) * hb, last), 1)),
            small((GMLP_HEADS, GMLP_BLOCK, GMLP_BLOCK)), small((GMLP_HEADS, GMLP_BLOCK, 1)),
            small((1, hd)), small((1, hd)), small((4, gd, gd)), small((1, 1024)), small((1, 1024)),
        ],
        out_specs=[
            pl.BlockSpec((tm, 5120), lambda i: (i, 0)),
            small((GMLP_HEADS, GMLP_BLOCK, GMLP_BLOCK)), small((GMLP_HEADS, GMLP_BLOCK, 1)),
            small((1, hd)), small((1, hd)), small((4, gd, gd)), small((1, 1024)), small((1, 1024)),
        ],
        out_shape=[
            jax.ShapeDtypeStruct((s, 5120), BF16),
            jax.ShapeDtypeStruct((GMLP_HEADS, GMLP_BLOCK, GMLP_BLOCK), F32),
            jax.ShapeDtypeStruct((GMLP_HEADS, GMLP_BLOCK, 1), F32),
            jax.ShapeDtypeStruct((1, hd), F32), jax.ShapeDtypeStruct((1, hd), F32),
            jax.ShapeDtypeStruct((4, gd, gd), F32),
            jax.ShapeDtypeStruct((1, 1024), F32), jax.ShapeDtypeStruct((1, 1024), F32),
        ],
        compiler_params=_params(1),
    )(proj, proj, proj, dm, dm, ws, bs_col, ng, nb, pw, pb, ps)


def _rope_pair_swap(t):
    lane = lax.broadcasted_iota(jnp.int32, t.shape, 1)
    return jnp.where(lane % 64 < 32, pltpu.roll(t, 96, 1), pltpu.roll(t, 32, 1))


def _rms(x, g):
    r = lax.rsqrt(jnp.mean(x * x, axis=-1, keepdims=True) + LN_EPS)
    return x * r, r


def _rms_bwd(dy, g, xhat, r):
    dyg = dy * g
    return r * (dyg - xhat * jnp.mean(dyg * xhat, axis=-1, keepdims=True))


def _lane_lt(shape, n):
    return lax.broadcasted_iota(jnp.int32, shape, 1) < n


def _mla_prep(proj, cosp, sinp, gq, gkv, name):
    s = proj.shape[0]
    tm = min(s, 512)

    def body(qc_ref, kv_ref, c_ref, s_ref, gq_ref, gkv_ref, qn_ref, kp_ref):
        qhat, _ = _rms(qc_ref[...], None)
        qn_ref[...] = (qhat * gq_ref[...]).astype(BF16)
        khat, _ = _rms(kv_ref[:, 0:128], None)
        kp_ref[:, 0:128] = (khat * gkv_ref[...]).astype(BF16)
        kr = kv_ref[:, 128:256]
        kp_ref[:, 128:256] = (kr * c_ref[...] + _rope_pair_swap(kr) * s_ref[...]).astype(BF16)

    return pl.pallas_call(
        body, name=name, grid=(s // tm,),
        in_specs=[pl.BlockSpec((tm, 256), lambda i: (i, 0)), pl.BlockSpec((tm, 256), lambda i: (i, 1)),
                  pl.BlockSpec((tm, 128), lambda i: (i, 0)), pl.BlockSpec((tm, 128), lambda i: (i, 0)),
                  pl.BlockSpec((1, 256), lambda i: (0, 0)), pl.BlockSpec((1, 128), lambda i: (0, 0))],
        out_specs=[pl.BlockSpec((tm, 256), lambda i: (i, 0)), pl.BlockSpec((tm, 256), lambda i: (i, 0))],
        out_shape=[jax.ShapeDtypeStruct((s, 256), BF16), jax.ShapeDtypeStruct((s, 256), BF16)],
        compiler_params=_params(1),
    )(proj, proj, cosp, sinp, gq, gkv)


def _mla_prep_bwd(proj, dqn, dkp, dv, cosp, sinp, gq, gkv, name):
    s = proj.shape[0]
    tm = min(s, 512)

    def body(qc_ref, kv_ref, dqn_ref, dkp_ref, dv_ref, c_ref, s_ref, gq_ref, gkv_ref, o_ref, red_ref):
        @pl.when(pl.program_id(0) == 0)
        def _():
            red_ref[...] = jnp.zeros_like(red_ref)

        qhat, qr = _rms(qc_ref[...], None)
        dq = dqn_ref[...]
        o_ref[:, 0:256] = _rms_bwd(dq, gq_ref[...], qhat, qr).astype(BF16)
        red_ref[0:1, :] += _colsum(dq * qhat)
        khat, kr = _rms(kv_ref[:, 0:128], None)
        dk = dkp_ref[:, 0:128] + dv_ref[...]
        o_ref[:, 256:384] = _rms_bwd(dk, gkv_ref[...], khat, kr).astype(BF16)
        red_ref[1:2, 0:128] += _colsum(dk * khat)
        dr = dkp_ref[:, 128:256]
        o_ref[:, 384:512] = (dr * c_ref[...] - _rope_pair_swap(dr) * s_ref[...]).astype(BF16)

    return pl.pallas_call(
        body, name=name, grid=(s // tm,),
        in_specs=[pl.BlockSpec((tm, 256), lambda i: (i, 0)), pl.BlockSpec((tm, 256), lambda i: (i, 1)),
                  pl.BlockSpec((tm, 256), lambda i: (i, 0)), pl.BlockSpec((tm, 256), lambda i: (i, 0)),
                  pl.BlockSpec((tm, 128), lambda i: (i, 0)),
                  pl.BlockSpec((tm, 128), lambda i: (i, 0)), pl.BlockSpec((tm, 128), lambda i: (i, 0)),
                  pl.BlockSpec((1, 256), lambda i: (0, 0)), pl.BlockSpec((1, 128), lambda i: (0, 0))],
        out_specs=[pl.BlockSpec((tm, 512), lambda i: (i, 0)), pl.BlockSpec((8, 256), lambda i: (0, 0))],
        out_shape=[jax.ShapeDtypeStruct((s, 512), BF16), jax.ShapeDtypeStruct((8, 256), F32)],
        compiler_params=_params(1),
    )(proj, proj, dqn, dkp, dv, cosp, sinp, gq, gkv)


HEAD_PAIRS = MLA_HEADS // 2
PAIR_TM = 512


def _q_heads(q_up, cosp, sinp, wuk, name):
    s = q_up.shape[0]
    tm = min(s, PAIR_TM)

    def body(qn_ref, qr_ref, c_ref, s_ref, w_ref, o_ref):
        raw = qr_ref[...]
        rot = raw * c_ref[...] + _rope_pair_swap(raw) * s_ref[...]
        low = _lane_lt(rot.shape, 64)
        o_ref[0, :, 0:128] = _dot(qn_ref[:, 0:128], w_ref[:, 0:128], NT).astype(BF16)
        o_ref[0, :, 128:256] = jnp.where(low, rot, 0.0).astype(BF16)
        o_ref[1, :, 0:128] = _dot(qn_ref[:, 128:256], w_ref[:, 128:256], NT).astype(BF16)
        o_ref[1, :, 128:256] = jnp.where(low, pltpu.roll(rot, 64, 1), 0.0).astype(BF16)

    return pl.pallas_call(
        body, name=name, grid=(s // tm, HEAD_PAIRS),
        in_specs=[pl.BlockSpec((tm, 256), lambda i, p: (i, p)),
                  pl.BlockSpec((tm, 128), lambda i, p: (i, 16 + p)),
                  pl.BlockSpec((tm, 128), lambda i, p: (i, 0)), pl.BlockSpec((tm, 128), lambda i, p: (i, 0)),
                  pl.BlockSpec((128, 256), lambda i, p: (0, p))],
        out_specs=pl.BlockSpec((2, tm, 256), lambda i, p: (p, i, 0)),
        out_shape=jax.ShapeDtypeStruct((MLA_HEADS, s, 256), BF16), compiler_params=_params(2),
    )(q_up, q_up, cosp, sinp, wuk)


def _q_heads_bwd(dqp, q_up, cosp, sinp, wuk, name):
    s = q_up.shape[0]
    tm = min(s, PAIR_TM)

    def body(dq_ref, qn_ref, c_ref, s_ref, w_ref, dn_ref, dr_ref, dw_ref):
        @pl.when(pl.program_id(1) == 0)
        def _():
            dw_ref[...] = jnp.zeros_like(dw_ref)

        for a in range(2):
            dlat = dq_ref[a, :, 0:128]
            cols = slice(128 * a, 128 * a + 128)
            dn_ref[:, cols] = _dot(dlat, w_ref[:, cols], NN).astype(BF16)
            dw_ref[:, cols] += _dot(dlat, qn_ref[:, cols], TN)
        drot = dq_ref[0, :, 128:256].astype(F32) + pltpu.roll(dq_ref[1, :, 128:256].astype(F32), 64, 1)
        dr_ref[...] = (drot * c_ref[...] - _rope_pair_swap(drot) * s_ref[...]).astype(BF16)

    return pl.pallas_call(
        body, name=name, grid=(HEAD_PAIRS, s // tm),
        in_specs=[pl.BlockSpec((2, tm, 256), lambda p, i: (p, i, 0)),
                  pl.BlockSpec((tm, 256), lambda p, i: (i, p)),
                  pl.BlockSpec((tm, 128), lambda p, i: (i, 0)), pl.BlockSpec((tm, 128), lambda p, i: (i, 0)),
                  pl.BlockSpec((128, 256), lambda p, i: (0, p))],
        out_specs=[pl.BlockSpec((tm, 256), lambda p, i: (i, p)),
                   pl.BlockSpec((tm, 128), lambda p, i: (i, p)),
                   pl.BlockSpec((128, 256), lambda p, i: (0, p))],
        out_shape=[jax.ShapeDtypeStruct((s, MLA_WIDTH), BF16), jax.ShapeDtypeStruct((s, 1024), BF16),
                   jax.ShapeDtypeStruct((128, MLA_WIDTH), F32)],
        compiler_params=_params(2),
    )(dqp, q_up, cosp, sinp, wuk)


def _o_gate(o_lat, proj, wuv, name):
    s = o_lat.shape[1]
    tm = min(s, PAIR_TM)

    def body(ol_ref, z_ref, w_ref, g_ref):
        for a in range(2):
            cols = slice(128 * a, 128 * a + 128)
            z = z_ref[:, cols]
            g_ref[:, cols] = (_dot(ol_ref[a], w_ref[:, cols], NN) * (z * _sigmoid(z))).astype(BF16)

    return pl.pallas_call(
        body, name=name, grid=(s // tm, HEAD_PAIRS),
        in_specs=[pl.BlockSpec((2, tm, 128), lambda i, p: (p, i, 0)),
                  pl.BlockSpec((tm, 256), lambda i, p: (i, 2 + p)),
                  pl.BlockSpec((128, 256), lambda i, p: (0, p))],
        out_specs=pl.BlockSpec((tm, 256), lambda i, p: (i, p)),
        out_shape=jax.ShapeDtypeStruct((s, MLA_WIDTH), BF16), compiler_params=_params(2),
    )(o_lat, proj, wuv)


def _o_gate_bwd(dg, o_lat, proj, wuv, name):
    s = o_lat.shape[1]
    tm = min(s, PAIR_TM)

    def body(dg_ref, ol_ref, z_ref, w_ref, dz_ref, dol_ref, dw_ref):
        @pl.when(pl.program_id(1) == 0)
        def _():
            dw_ref[...] = jnp.zeros_like(dw_ref)

        for a in range(2):
            cols = slice(128 * a, 128 * a + 128)
            z, dgv, ol = z_ref[:, cols], dg_ref[:, cols], ol_ref[a]
            sig = _sigmoid(z)
            o = _dot(ol, w_ref[:, cols], NN)
            dz_ref[:, cols] = (dgv * o * (sig * (1.0 + z * (1.0 - sig)))).astype(BF16)
            do = (dgv * (z * sig)).astype(BF16)
            dol_ref[a] = _dot(do, w_ref[:, cols], NT).astype(BF16)
            dw_ref[:, cols] += _dot(ol, do, TN)

    return pl.pallas_call(
        body, name=name, grid=(HEAD_PAIRS, s // tm),
        in_specs=[pl.BlockSpec((tm, 256), lambda p, i: (i, p)),
                  pl.BlockSpec((2, tm, 128), lambda p, i: (p, i, 0)),
                  pl.BlockSpec((tm, 256), lambda p, i: (i, 2 + p)),
                  pl.BlockSpec((128, 256), lambda p, i: (0, p))],
        out_specs=[pl.BlockSpec((tm, 256), lambda p, i: (i, p)),
                   pl.BlockSpec((2, tm, 128), lambda p, i: (p, i, 0)),
                   pl.BlockSpec((128, 256), lambda p, i: (0, p))],
        out_shape=[jax.ShapeDtypeStruct((s, MLA_WIDTH), BF16), jax.ShapeDtypeStruct((MLA_HEADS, s, 128), BF16),
                   jax.ShapeDtypeStruct((128, MLA_WIDTH), F32)],
        compiler_params=_params(2),
    )(dg, o_lat, proj, wuv)


ATT_TQ = CHUNK
ATT_ROWS = ATT_TQ * MLA_HEADS
ATT_TK = 512


def _visible(k0, q_chunk, tk):
    kpos = k0 + lax.broadcasted_iota(jnp.int32, (1, tk), 1)
    return kpos // CHUNK <= q_chunk


def _tile_lanes(t, n):
    return jnp.concatenate([t] * (n // 128), axis=1)


def _key_blocks(i, tk, block):
    n_blocks = (i * ATT_TQ + ATT_TQ + tk - 1) // tk

    def unmasked(j, carry):
        block(j, False)
        return carry

    lax.fori_loop(0, n_blocks - 1, unmasked, 0)
    block(n_blocks - 1, True)


def _attn_fwd(qp, kp, name):
    s = kp.shape[0]
    tk = min(ATT_TK, s)

    def body(q_ref, k_ref, o_ref, lse_ref, m_sc, acc_sc):
        i = pl.program_id(0)
        q = q_ref[...].reshape(ATT_ROWS, 256)
        m_sc[...] = jnp.full_like(m_sc, -jnp.inf)
        acc_sc[...] = jnp.zeros_like(acc_sc)

        def block(j, masked):
            k0 = pl.multiple_of(j * tk, tk)
            k = k_ref[pl.ds(k0, tk), :]
            sc = _dot(q, k, NT) * ATTN_SCALE
            if masked:
                sc = jnp.where(_visible(k0, i, tk), sc, -jnp.inf)
            m_prev = m_sc[...]
            m_new = jnp.maximum(m_prev, jnp.max(sc, axis=1, keepdims=True))
            p = jnp.exp(sc - _tile_lanes(m_new, tk))
            v1 = jnp.where(_lane_lt(k.shape, 128), k, jnp.ones_like(k))
            acc_sc[...] = _tile_lanes(jnp.exp(m_prev - m_new), 256) * acc_sc[...] + _dot(p, v1, NN)
            m_sc[...] = m_new

        _key_blocks(i, tk, block)
        acc = acc_sc[...]
        l = acc[:, 128:256]
        o_ref[...] = (acc[:, 0:128] / l).astype(BF16).reshape(MLA_HEADS, ATT_TQ, 128)
        lse_ref[...] = (m_sc[...] + jnp.log(l)).reshape(MLA_HEADS, ATT_TQ, 128)

    head128 = pl.BlockSpec((MLA_HEADS, ATT_TQ, 128), lambda i: (0, i, 0))
    return pl.pallas_call(
        body, name=name, grid=(s // ATT_TQ,),
        in_specs=[pl.BlockSpec((MLA_HEADS, ATT_TQ, 256), lambda i: (0, i, 0)), pl.BlockSpec((s, 256), lambda i: (0, 0))],
        out_specs=[head128, head128],
        out_shape=[jax.ShapeDtypeStruct((MLA_HEADS, s, 128), BF16), jax.ShapeDtypeStruct((MLA_HEADS, s, 128), F32)],
        scratch_shapes=[pltpu.VMEM((ATT_ROWS, 128), F32), pltpu.VMEM((ATT_ROWS, 256), F32)],
        compiler_params=_params(1),
    )(qp, kp)


def _attn_bwd(qp, kp, o, do, lse, name):
    s = kp.shape[0]
    tk = min(ATT_TK, s)

    def body(q_ref, k_ref, o_ref, do_ref, lse_ref, dq_ref, dk_ref, dv_ref, dq_sc):
        i = pl.program_id(0)

        @pl.when(i == 0)
        def _():
            dk_ref[...] = jnp.zeros_like(dk_ref)
            dv_ref[...] = jnp.zeros_like(dv_ref)

        q = q_ref[...].reshape(ATT_ROWS, 256)
        dov = do_ref[...].reshape(ATT_ROWS, 128)
        delta = jnp.sum(dov.astype(F32) * o_ref[...].reshape(ATT_ROWS, 128).astype(F32), axis=1, keepdims=True)
        delta_t = _tile_lanes(jnp.broadcast_to(delta, (ATT_ROWS, 128)), tk)
        lse_t = _tile_lanes(lse_ref[...].reshape(ATT_ROWS, 128), tk)
        dq_sc[...] = jnp.zeros_like(dq_sc)

        def block(j, masked):
            k0 = pl.multiple_of(j * tk, tk)
            k = k_ref[pl.ds(k0, tk), :]
            p = jnp.exp(_dot(q, k, NT) * ATTN_SCALE - lse_t)
            if masked:
                p = jnp.where(_visible(k0, i, tk), p, 0.0)
            dv_ref[pl.ds(k0, tk), :] += _dot(p, dov, TN)
            ds = (p * (_dot(dov, k[:, 0:128], NT) - delta_t) * ATTN_SCALE).astype(BF16)
            dq_sc[...] += _dot(ds, k, NN)
            dk_ref[pl.ds(k0, tk), :] += _dot(ds, q, TN)

        _key_blocks(i, tk, block)
        dq_ref[...] = dq_sc[...].astype(BF16).reshape(MLA_HEADS, ATT_TQ, 256)

    head128 = pl.BlockSpec((MLA_HEADS, ATT_TQ, 128), lambda i: (0, i, 0))
    head256 = pl.BlockSpec((MLA_HEADS, ATT_TQ, 256), lambda i: (0, i, 0))
    return pl.pallas_call(
        body, name=name, grid=(s // ATT_TQ,),
        in_specs=[head256, pl.BlockSpec((s, 256), lambda i: (0, 0)), head128, head128, head128],
        out_specs=[head256, pl.BlockSpec((s, 256), lambda i: (0, 0)), pl.BlockSpec((s, 128), lambda i: (0, 0))],
        out_shape=[jax.ShapeDtypeStruct((MLA_HEADS, s, 256), BF16),
                   jax.ShapeDtypeStruct((s, 256), F32), jax.ShapeDtypeStruct((s, 128), F32)],
        scratch_shapes=[pltpu.VMEM((ATT_ROWS, 256), F32)],
        compiler_params=_params(1),
    )(qp, kp, o, do, lse)


def _place():
    x, y, c = lax.axis_index("x"), lax.axis_index("y"), lax.axis_index("c")
    return x, y, c, 4 * x + 2 * y + c


def _flip(x, y, c, r):
    px = 1 - x if r & 4 else x
    py = 1 - y if r & 2 else y
    pc = 1 - c if r & 1 else c
    return (px, py, pc), 4 * px + 2 * py + pc


def _adaln_exchange(c8, ada_w, ada_b_cols, name):
    d = c8.shape[1]
    w_cols = ada_w.shape[2]

    def body(c_ref, w_ref, b_ref, call_ref, mod_ref, sbuf, rbuf, s1, r1, s2, r2):
        x, y, c, me = _place()
        call_ref[pl.ds(pl.multiple_of(me * 8, 8), 8), :] = c_ref[...]
        peers = [_flip(x, y, c, r) for r in range(1, N_DEV)]

        def c_copy(k, src_lin, to):
            rows = call_ref.at[pl.ds(pl.multiple_of(src_lin * 8, 8), 8), :]
            return pltpu.make_async_remote_copy(src_ref=rows, dst_ref=rows, send_sem=s1.at[k], recv_sem=r1.at[k],
                                                device_id=to, device_id_type=MESH)

        first = [c_copy(k, me, peer) for k, (peer, _) in enumerate(peers)]
        for cp in first:
            cp.start()
        for k, (_, lin) in enumerate(peers):
            c_copy(k, lin, (x, y, c)).wait_recv()
        for cp in first:
            cp.wait_send()

        for j in range(N_DEV):
            cj = call_ref[8 * j:8 * j + 8, :]
            cond = cj * _sigmoid(cj)
            for l in range(2):
                sbuf[j, l] = lax.dot_general(cond, w_ref[l], NN, precision=lax.Precision.HIGHEST,
                                             preferred_element_type=F32) + b_ref[l]

        def m_copy(k, src_slot, dst_slot, to):
            return pltpu.make_async_remote_copy(src_ref=sbuf.at[src_slot], dst_ref=rbuf.at[dst_slot],
                                                send_sem=s2.at[k], recv_sem=r2.at[k], device_id=to,
                                                device_id_type=MESH)

        rbuf[me] = sbuf[me]
        second = [m_copy(k, lin, me, peer) for k, (peer, lin) in enumerate(peers)]
        for cp in second:
            cp.start()
        for k, (_, lin) in enumerate(peers):
            m_copy(k, lin, lin, (x, y, c)).wait_recv()
        for cp in second:
            cp.wait_send()
        for j in range(N_DEV):
            for l in range(2):
                mod_ref[l, :, w_cols * j:w_cols * (j + 1)] = rbuf[j, l]

    vmem = pl.BlockSpec(memory_space=pltpu.VMEM)
    return pl.pallas_call(
        body, name=name, in_specs=[vmem, vmem, vmem], out_specs=[vmem, vmem],
        out_shape=[jax.ShapeDtypeStruct((8 * N_DEV, d), F32), jax.ShapeDtypeStruct((2, 8, 3 * d), F32)],
        scratch_shapes=[pltpu.VMEM((N_DEV, 2, 8, w_cols), F32), pltpu.VMEM((N_DEV, 2, 8, w_cols), F32),
                        pltpu.SemaphoreType.DMA((N_DEV - 1,)), pltpu.SemaphoreType.DMA((N_DEV - 1,)),
                        pltpu.SemaphoreType.DMA((N_DEV - 1,)), pltpu.SemaphoreType.DMA((N_DEV - 1,))],
        compiler_params=pltpu.CompilerParams(vmem_limit_bytes=VMEM_LIMIT),
    )(c8, ada_w, ada_b_cols)


def _all_gather(block, name):
    def body(x_ref, out_ref, send_sems, recv_sems, local_sem):
        x, y, c, _ = _place()
        me, sibling = (x, y, c), (x, y, 1 - c)
        chips = [(1 - x, y), (x, 1 - y), (1 - x, 1 - y)]

        def slot(px, py, pc):
            return out_ref.at[4 * px + 2 * py + pc]

        def copy(k, blk, to, src=None):
            return pltpu.make_async_remote_copy(src_ref=slot(*blk) if src is None else src, dst_ref=slot(*blk),
                                                send_sem=send_sems.at[k], recv_sem=recv_sems.at[k],
                                                device_id=to, device_id_type=MESH)

        mine = pltpu.make_async_copy(x_ref, slot(*me), local_sem)
        mine.start()
        first = [copy(0, me, sibling, src=x_ref)]
        first += [copy(1 + j, me, (*chip, c), src=x_ref) for j, chip in enumerate(chips)]
        for cp in first:
            cp.start()
        passed = [copy(4 + j, (*chip, c), sibling) for j, chip in enumerate(chips)]
        for j, chip in enumerate(chips):
            copy(1 + j, (*chip, c), me).wait_recv()
            passed[j].start()
        copy(0, sibling, me).wait_recv()
        for j, chip in enumerate(chips):
            copy(4 + j, (*chip, 1 - c), me).wait_recv()
        for cp in first + passed:
            cp.wait_send()
        mine.wait()

    anyspace = pl.BlockSpec(memory_space=pl.ANY)
    return pl.pallas_call(
        body, name=name, in_specs=[anyspace], out_specs=anyspace,
        out_shape=jax.ShapeDtypeStruct((N_DEV,) + block.shape, block.dtype),
        scratch_shapes=[pltpu.SemaphoreType.DMA((7,)), pltpu.SemaphoreType.DMA((7,)), pltpu.SemaphoreType.DMA],
    )(block)


def _scatter_parts(gb, gf, name):
    def body(gb_ref, gf_ref, rb_ref, rf_ref, send_sems, recv_sems, local_sems):
        x, y, c, me = _place()
        own = [pltpu.make_async_copy(gb_ref.at[me], rb_ref.at[me], local_sems.at[0]),
               pltpu.make_async_copy(gf_ref.at[me], rf_ref.at[me], local_sems.at[1])]
        for cp in own:
            cp.start()
        sends, recvs = [], []
        for r in range(1, N_DEV):
            peer, lin = _flip(x, y, c, r)
            for t, (src, dst) in enumerate(((gb_ref, rb_ref), (gf_ref, rf_ref))):
                k = 2 * (r - 1) + t
                sends.append(pltpu.make_async_remote_copy(
                    src_ref=src.at[lin], dst_ref=dst.at[me], send_sem=send_sems.at[k], recv_sem=recv_sems.at[k],
                    device_id=peer, device_id_type=MESH))
                recvs.append(pltpu.make_async_remote_copy(
                    src_ref=src.at[lin], dst_ref=dst.at[lin], send_sem=send_sems.at[k], recv_sem=recv_sems.at[k],
                    device_id=(x, y, c), device_id_type=MESH))
        for cp in sends:
            cp.start()
        for cp in recvs:
            cp.wait_recv()
        for cp in sends:
            cp.wait_send()
        for cp in own:
            cp.wait()

    anyspace = pl.BlockSpec(memory_space=pl.ANY)
    n = 2 * (N_DEV - 1)
    return pl.pallas_call(
        body, name=name, in_specs=[anyspace, anyspace], out_specs=[anyspace, anyspace],
        out_shape=[jax.ShapeDtypeStruct(gb.shape, gb.dtype), jax.ShapeDtypeStruct(gf.shape, gf.dtype)],
        scratch_shapes=[pltpu.SemaphoreType.DMA((n,)), pltpu.SemaphoreType.DMA((n,)), pltpu.SemaphoreType.DMA((2,))],
    )(gb, gf)


def _adamw(w, g, m, v):
    m = ADAM_B1 * m + (1.0 - ADAM_B1) * g
    v = ADAM_B2 * v + (1.0 - ADAM_B2) * (g * g)
    m_hat = m / (1.0 - ADAM_B1 ** ADAM_STEP)
    v_hat = v / (1.0 - ADAM_B2 ** ADAM_STEP)
    return -ADAM_LR * (m_hat / (jnp.sqrt(v_hat) + ADAM_EPS) + ADAM_WD * w), m, v


def _sum_parts_adamw(parts, w, m, v, name):
    _, rows, cols = parts.shape
    tr = max(t for t in range(16, 257, 16) if rows % t == 0)

    def body(p_ref, w_ref, m_ref, v_ref, g_ref, d_ref, mo_ref, vo_ref):
        g = p_ref[0].astype(F32)
        for j in range(1, N_DEV):
            g = g + p_ref[j].astype(F32)
        g_ref[...] = g
        d_ref[...], mo_ref[...], vo_ref[...] = _adamw(w_ref[...], g, m_ref[...], v_ref[...])

    row = pl.BlockSpec((tr, cols), lambda i: (i, 0))
    out = jax.ShapeDtypeStruct((rows, cols), F32)
    return pl.pallas_call(
        body, name=name, grid=(rows // tr,),
        in_specs=[pl.BlockSpec((N_DEV, tr, cols), lambda i: (0, i, 0)), row, row, row],
        out_specs=[row, row, row, row], out_shape=[out, out, out, out], compiler_params=_params(1),
    )(parts, w, m, v)


def _sum_parts(parts, name):
    _, rows, cols = parts.shape

    def body(p_ref, g_ref):
        g = p_ref[0]
        for j in range(1, N_DEV):
            g = g + p_ref[j]
        g_ref[...] = g

    return pl.pallas_call(
        body, name=name, out_shape=jax.ShapeDtypeStruct((rows, cols), F32),
        compiler_params=pltpu.CompilerParams(vmem_limit_bytes=VMEM_LIMIT),
    )(parts)


def _adamw_call(g, w, m, v, name):
    rows, cols = g.shape
    tr = 128 if rows % 128 == 0 else rows

    def body(g_ref, w_ref, m_ref, v_ref, d_ref, mo_ref, vo_ref):
        d_ref[...], mo_ref[...], vo_ref[...] = _adamw(w_ref[...], g_ref[...], m_ref[...], v_ref[...])

    row = pl.BlockSpec((tr, cols), lambda i: (i, 0))
    out = jax.ShapeDtypeStruct((rows, cols), F32)
    return pl.pallas_call(
        body, name=name, grid=(rows // tr,), in_specs=[row] * 4, out_specs=[row] * 3, out_shape=[out] * 3,
        compiler_params=_params(1),
    )(g, w, m, v)


def _ada_w_grad_adamw(c_all, dmod_rows, w, m, v, name):
    def body(c_ref, dm_ref, w_ref, m_ref, v_ref, g_ref, d_ref, mo_ref, vo_ref):
        cv = c_ref[...]
        cond = cv * _sigmoid(cv)
        for l in range(2):
            g = lax.dot_general(cond, dm_ref[l], TN, precision=lax.Precision.HIGHEST, preferred_element_type=F32)
            g_ref[l] = g
            d_ref[l], mo_ref[l], vo_ref[l] = _adamw(w_ref[l], g, m_ref[l], v_ref[l])

    out = jax.ShapeDtypeStruct(w.shape, F32)
    return pl.pallas_call(
        body, name=name, out_shape=[out] * 4, compiler_params=pltpu.CompilerParams(vmem_limit_bytes=VMEM_LIMIT),
    )(c_all, dmod_rows, w, m, v)


SHARDED = ("e_w_in", "pool_w", "e_w_out", "o_w_in", "mla_q_norm_g", "mla_w_uq", "o_w_out")
REPLICATED = ("ln_g", "ln_b", "gmlp_norm_g", "gmlp_norm_b", "gmlp_ws", "gmlp_bs", "pool_b", "pool_scale",
              "mla_kv_norm_g", "mla_w_uk", "mla_w_uv")


def _to_shards(name, g):
    if name in ("e_w_in", "o_w_in"):
        k, n = g.shape
        return g.reshape(k, N_DEV, n // N_DEV).transpose(1, 0, 2).reshape(N_DEV, -1)
    if name == "pool_w":
        return g.reshape(4, N_DEV, 32, 256).transpose(1, 0, 2, 3).reshape(N_DEV, -1)
    return g.reshape(N_DEV, -1)


def _from_shards(name, flat):
    if name == "e_w_in":
        return flat.reshape(N_DEV, 1024, 640).transpose(1, 0, 2).reshape(1024, 5120)
    if name == "o_w_in":
        return flat.reshape(N_DEV, 1024, 312).transpose(1, 0, 2).reshape(1024, ODD_IN)
    if name == "pool_w":
        return flat.reshape(N_DEV, 4, 32, 256).transpose(1, 0, 2, 3).reshape(4, 256, 256)
    if name in ("e_w_out", "o_w_out"):
        return flat.reshape(2048, 1024)
    if name == "mla_w_uq":
        return flat.reshape(MLA_Q_RANK, MLA_HEADS, MLA_NOPE + MLA_ROPE)
    return flat.reshape(-1)


def _pad_rows(flat2d, rows):
    n, k = flat2d.shape
    return jnp.pad(flat2d, ((0, 0), (0, rows * LANES - k))).reshape(n, rows, LANES)


def _ceil_to(a, b):
    return -(-a // b) * b


def kernel(x, c, positions, ada_w, ada_b, ln_g, ln_b, e_w_in, gmlp_norm_g, gmlp_norm_b, gmlp_ws, gmlp_bs, pool_w, pool_b, pool_scale, e_w_out, o_w_in, mla_q_norm_g, mla_kv_norm_g, mla_w_uq, mla_w_uk, mla_w_uv, o_w_out, loss_target, m_ada_w, m_ada_b, m_ln_g, m_ln_b, m_e_w_in, m_gmlp_norm_g, m_gmlp_norm_b, m_gmlp_ws, m_gmlp_bs, m_pool_w, m_pool_b, m_pool_scale, m_e_w_out, m_o_w_in, m_mla_q_norm_g, m_mla_kv_norm_g, m_mla_w_uq, m_mla_w_uk, m_mla_w_uv, m_o_w_out, v_ada_w, v_ada_b, v_ln_g, v_ln_b, v_e_w_in, v_gmlp_norm_g, v_gmlp_norm_b, v_gmlp_ws, v_gmlp_bs, v_pool_w, v_pool_b, v_pool_scale, v_e_w_out, v_o_w_in, v_mla_q_norm_g, v_mla_kv_norm_g, v_mla_w_uq, v_mla_w_uk, v_mla_w_uv, v_o_w_out):
    w_in = dict(ada_w=ada_w, ada_b=ada_b, ln_g=ln_g, ln_b=ln_b, e_w_in=e_w_in, gmlp_norm_g=gmlp_norm_g,
                gmlp_norm_b=gmlp_norm_b, gmlp_ws=gmlp_ws, gmlp_bs=gmlp_bs, pool_w=pool_w, pool_b=pool_b,
                pool_scale=pool_scale, e_w_out=e_w_out, o_w_in=o_w_in, mla_q_norm_g=mla_q_norm_g,
                mla_kv_norm_g=mla_kv_norm_g, mla_w_uq=mla_w_uq, mla_w_uk=mla_w_uk, mla_w_uv=mla_w_uv, o_w_out=o_w_out)
    m_in = dict(ada_w=m_ada_w, ada_b=m_ada_b, ln_g=m_ln_g, ln_b=m_ln_b, e_w_in=m_e_w_in, gmlp_norm_g=m_gmlp_norm_g,
                gmlp_norm_b=m_gmlp_norm_b, gmlp_ws=m_gmlp_ws, gmlp_bs=m_gmlp_bs, pool_w=m_pool_w, pool_b=m_pool_b,
                pool_scale=m_pool_scale, e_w_out=m_e_w_out, o_w_in=m_o_w_in, mla_q_norm_g=m_mla_q_norm_g,
                mla_kv_norm_g=m_mla_kv_norm_g, mla_w_uq=m_mla_w_uq, mla_w_uk=m_mla_w_uk, mla_w_uv=m_mla_w_uv,
                o_w_out=m_o_w_out)
    v_in = dict(ada_w=v_ada_w, ada_b=v_ada_b, ln_g=v_ln_g, ln_b=v_ln_b, e_w_in=v_e_w_in, gmlp_norm_g=v_gmlp_norm_g,
                gmlp_norm_b=v_gmlp_norm_b, gmlp_ws=v_gmlp_ws, gmlp_bs=v_gmlp_bs, pool_w=v_pool_w, pool_b=v_pool_b,
                pool_scale=v_pool_scale, e_w_out=v_e_w_out, o_w_in=v_o_w_in, mla_q_norm_g=v_mla_q_norm_g,
                mla_kv_norm_g=v_mla_kv_norm_g, mla_w_uq=v_mla_w_uq, mla_w_uk=v_mla_w_uk, mla_w_uv=v_mla_w_uv,
                o_w_out=v_o_w_out)
    names = list(w_in)
    seq = x.shape[1]
    d = D_MODEL
    me = 4 * lax.axis_index("x") + 2 * lax.axis_index("y") + lax.axis_index("c")
    ada_cols = ada_w.shape[2]

    ada_b_cols = lax.dynamic_slice_in_dim(ada_b, me * ada_cols, ada_cols, axis=1)
    slab_row = lax.broadcasted_iota(jnp.int32, (8, d), 0)
    slab = jnp.where(slab_row == 0, c, jnp.where(slab_row == 1, jnp.pad(mla_q_norm_g, ((0, 0), (0, d - 32))), 0.0))
    c_all, mod = _adaln_exchange(slab, ada_w,
                                 jnp.broadcast_to(ada_b_cols[:, None, :], (2, 8, ada_cols)), "adaln_exchange")

    shard_len = sum(w_in[n].size for n in SHARDED)
    shard_rows = _ceil_to(-(-shard_len // LANES), 16)

    def pack_local(src, dtype):
        flat = jnp.concatenate([src[n].reshape(-1) for n in SHARDED]).astype(dtype)
        return _pad_rows(flat[None], shard_rows)[0]

    gathered = _all_gather(pack_local(w_in, BF16), "weight_gather").reshape(N_DEV, -1)
    full, off = {}, 0
    for n in SHARDED:
        full[n] = _from_shards(n, gathered[:, off:off + w_in[n].size])
        off += w_in[n].size
    w_in_e, w_out_e, w_out_o, pool_w_full = full["e_w_in"], full["e_w_out"], full["o_w_out"], full["pool_w"]
    w_in_o = jnp.concatenate([full["o_w_in"][:, :448], jnp.zeros((d, 64), BF16), full["o_w_in"][:, 448:]], axis=1)
    w_uq = jnp.concatenate([full["mla_w_uq"][:, :, :MLA_NOPE].reshape(MLA_Q_RANK, -1),
                            full["mla_w_uq"][:, :, MLA_NOPE:].reshape(MLA_Q_RANK, -1)], axis=1)
    g_q = c_all.reshape(N_DEV, 8, d)[:, 1, :32].reshape(1, MLA_Q_RANK)

    x0, target = x[0], loss_target[0]
    ws, bs_col = gmlp_ws[0], gmlp_bs[0].reshape(GMLP_HEADS, GMLP_BLOCK, 1)
    wuk2, wuv2 = mla_w_uk[0].reshape(MLA_KV_RANK, -1), mla_w_uv[0].reshape(MLA_KV_RANK, -1)
    inv = 1.0 / (ROPE_THETA ** (jnp.arange(0, MLA_ROPE, 2, dtype=F32) / MLA_ROPE))
    ang = positions[0].astype(F32)[:, None] * inv
    cosp = jnp.tile(jnp.cos(ang), (1, 4))
    sinp = jnp.tile(jnp.concatenate([-jnp.sin(ang), jnp.sin(ang)], axis=1), (1, 2))

    h0 = _modulate(x0, mod[0], "modulate0")
    proj0 = _matmul([(h0, w_in_e)], "nn", F32, seq, 5120, 512, 640, "even_in")
    mix0 = _even_fwd(proj0, ws, bs_col, gmlp_norm_g, gmlp_norm_b, pool_w_full, pool_b, pool_scale, "even_mix")
    y0 = _matmul([(mix0, w_out_e)], "nn", F32, seq, d, 512, 1024, "even_out")
    x1 = _resid_ln(x0, y0, mod[0], ln_g[0:1], ln_b[0:1], "resid_ln0")

    h1 = _modulate(x1, mod[1], "modulate1")
    proj1 = _matmul([(h1, w_in_o)], "nn", F32, seq, ODD_IN_PAD, 512, 512, "odd_in")
    qn, kp = _mla_prep(proj1, cosp, sinp, g_q, mla_kv_norm_g, "mla_prep")
    q_up = _matmul([(qn, w_uq)], "nn", F32, seq, 3072, 512, 1024, "q_up")
    qp = _q_heads(q_up, cosp, sinp, wuk2, "q_heads")
    o_lat, lse = _attn_fwd(qp, kp, "attn_fwd")
    gated = _o_gate(o_lat, proj1, wuv2, "o_gate")
    y1 = _matmul([(gated, w_out_o)], "nn", F32, seq, d, 512, 1024, "odd_out")

    dy1, dxres1, red2 = _final_ln_loss_bwd(x1, y1, mod[1], ln_g[1:2], ln_b[1:2], target, "final_ln_loss")
    dgated = _matmul([(dy1, w_out_o)], "nt", F32, seq, MLA_WIDTH, 512, 512, "odd_out_dx")
    g_w_out_o = _matmul([(gated, dy1)], "tn", F32, MLA_WIDTH, d, 256, 512, "odd_out_dw")
    dz, do_lat, g_wuv = _o_gate_bwd(dgated, o_lat, proj1, wuv2, "o_gate_bwd")
    dqp, dkp, dvv = _attn_bwd(qp, kp, o_lat, do_lat, lse, "attn_bwd")
    dq_nope, dq_rope, g_wuk = _q_heads_bwd(dqp, q_up, cosp, sinp, wuk2, "q_heads_bwd")
    dqn = _matmul([(dq_nope, w_uq[:, :MLA_WIDTH]), (dq_rope, w_uq[:, MLA_WIDTH:])], "nt", F32, seq, MLA_Q_RANK,
                  512, 256, "q_up_dx")
    g_wuq_n = _matmul([(qn, dq_nope)], "tn", F32, MLA_Q_RANK, MLA_WIDTH, 256, 512, "q_up_dw_nope")
    g_wuq_r = _matmul([(qn, dq_rope)], "tn", F32, MLA_Q_RANK, 1024, 256, 512, "q_up_dw_rope")
    dqkr, red_mla = _mla_prep_bwd(proj1, dqn, dkp, dvv, cosp, sinp, g_q, mla_kv_norm_g, "mla_prep_bwd")
    dproj1 = jnp.concatenate([dqkr, dz], axis=1)
    dh1 = _matmul([(dproj1, w_in_o)], "nt", F32, seq, d, 512, 512, "odd_in_dx")
    g_w_in_o = _matmul([(h1, dproj1)], "tn", F32, d, ODD_IN_PAD, 256, 512, "odd_in_dw")
    dy0, dxres0, red1 = _mid_bwd(dh1, dxres1, x0, y0, mod[0], mod[1], ln_g[0:1], ln_b[0:1], "mid_bwd")
    dmix = _matmul([(dy0, w_out_e)], "nt", F32, seq, 2048, 512, 512, "even_out_dx")
    g_w_out_e = _matmul([(mix0, dy0)], "tn", F32, 2048, d, 256, 512, "even_out_dw")
    dproj0, g_ws, g_bs, g_ng, g_nb, g_pw, g_pb, g_ps = _even_bwd(
        proj0, dmix, ws, bs_col, gmlp_norm_g, gmlp_norm_b, pool_w_full, pool_b, pool_scale, "even_mix_bwd")
    dh0 = _matmul([(dproj0, w_in_e)], "nt", F32, seq, d, 512, 512, "even_in_dx")
    g_w_in_e = _matmul([(h0, dproj0)], "tn", F32, d, 5120, 256, 640, "even_in_dw")
    grad_x, red0 = _first_bwd(dh0, dxres0, x0, mod[0], "first_bwd")

    loss = lax.psum(0.5 / d * jnp.sum(red2[3]), ("x", "y", "c"))

    t_mask = lax.broadcasted_iota(jnp.int32, (GMLP_BLOCK, GMLP_BLOCK), 0) // CHUNK
    s_mask = lax.broadcasted_iota(jnp.int32, (GMLP_BLOCK, GMLP_BLOCK), 1) // CHUNK
    part = {
        "e_w_in": g_w_in_e, "pool_w": g_pw, "e_w_out": g_w_out_e,
        "o_w_in": jnp.concatenate([g_w_in_o[:, :448], g_w_in_o[:, 512:]], axis=1),
        "mla_q_norm_g": red_mla[0],
        "mla_w_uq": jnp.concatenate([g_wuq_n.reshape(MLA_Q_RANK, MLA_HEADS, MLA_NOPE),
                                     g_wuq_r.reshape(MLA_Q_RANK, MLA_HEADS, MLA_ROPE)], axis=2),
        "o_w_out": g_w_out_o,
        "ln_g": jnp.stack([red1[2], red2[0]]), "ln_b": jnp.stack([red1[3], red2[1]]),
        "gmlp_norm_g": g_ng, "gmlp_norm_b": g_nb,
        "gmlp_ws": jnp.where(s_mask <= t_mask, g_ws, 0.0), "gmlp_bs": g_bs,
        "pool_b": g_pb, "pool_scale": g_ps, "mla_kv_norm_g": red_mla[1, :MLA_KV_RANK],
        "mla_w_uk": g_wuk, "mla_w_uv": g_wuv,
    }
    dmod = jnp.stack([jnp.concatenate([red0[1], red0[0], red1[4]]),
                      jnp.concatenate([red1[1], red1[0], red2[2]])])

    gb = _pad_rows(jnp.concatenate([_to_shards(n, part[n]) for n in SHARDED], axis=1).astype(BF16), shard_rows)
    rep_len = sum(w_in[n].size for n in REPLICATED)
    chunk = rep_len // N_DEV
    chunk_rows = -(-chunk // LANES)
    rep_rows = _ceil_to(chunk_rows + 1, 8)

    def pack_rep(src, extra):
        body = _pad_rows(jnp.concatenate([src[n].reshape(-1) for n in REPLICATED]).reshape(N_DEV, chunk), chunk_rows)
        tail = _pad_rows(extra.reshape(2, N_DEV, ada_cols).transpose(1, 0, 2).reshape(N_DEV, -1),
                         rep_rows - chunk_rows)
        return jnp.concatenate([body, tail], axis=1)

    rb, rf = _scatter_parts(gb, pack_rep(part, dmod), "grad_scatter")

    g_sh, d_sh, m_sh, v_sh = _sum_parts_adamw(rb, pack_local(w_in, F32), pack_local(m_in, F32), pack_local(v_in, F32),
                                              "shard_sum_adamw")
    rep_sum = _all_gather(_sum_parts(rf, "replicated_sum"), "replicated_gather").reshape(N_DEV * rep_rows, LANES)
    rep_w, rep_m, rep_v = (pack_rep(src, src["ada_b"]).reshape(N_DEV * rep_rows, LANES) for src in (w_in, m_in, v_in))
    d_rep, m_rep, v_rep = _adamw_call(rep_sum, rep_w, rep_m, rep_v, "replicated_adamw")
    dmod_all = rf[:, chunk_rows, :2 * ada_cols].reshape(N_DEV, 2, ada_cols).transpose(1, 0, 2)
    dmod_rows = jnp.pad(dmod_all[:, :, None, :], ((0, 0), (0, 0), (0, 7), (0, 0))).reshape(2, 8 * N_DEV, ada_cols)
    g_ada, d_ada, m_ada, v_ada = _ada_w_grad_adamw(c_all, dmod_rows, ada_w, m_ada_w, v_ada_w, "ada_w_adamw")

    def unpack(sharded_flat, rep_flat, ada):
        out = {"ada_w": ada}
        flat, off = sharded_flat.reshape(-1), 0
        for n in SHARDED:
            out[n] = flat[off:off + w_in[n].size].reshape(w_in[n].shape)
            off += w_in[n].size
        rep3 = rep_flat.reshape(N_DEV, rep_rows, LANES)
        flat, off = rep3[:, :chunk_rows].reshape(N_DEV, -1)[:, :chunk].reshape(-1), 0
        for n in REPLICATED:
            out[n] = flat[off:off + w_in[n].size].reshape(w_in[n].shape)
            off += w_in[n].size
        out["ada_b"] = rep3[:, chunk_rows, :2 * ada_cols].reshape(N_DEV, 2, ada_cols).transpose(1, 0, 2).reshape(2, -1)
        return [out[n] for n in names]

    return (loss, grad_x[None], *unpack(g_sh, rep_sum, g_ada), *unpack(d_sh, d_rep, d_ada),
            *unpack(m_sh, m_rep, m_ada), *unpack(v_sh, v_rep, v_ada))
```

```python
import functools

import jax
import jax.numpy as jnp
from jax import lax
from jax.experimental import pallas as pl
from jax.experimental.pallas import tpu as pltpu

F32 = jnp.float32
BF16 = jnp.bfloat16

D_MODEL = 1024
CHUNK = 64
LN_EPS = 1e-5
GMLP_HEADS = 4
GMLP_HEAD_DIM = 256
GMLP_BLOCK = 128
POOL_WINDOWS = (2, 4, 8, 16)
POOL_GROUP_DIM = 256
POOL_HALO = 16
MLA_HEADS = 16
MLA_NOPE = 128
MLA_ROPE = 64
MLA_Q_RANK = 256
MLA_KV_RANK = 128
MLA_WIDTH = 2048
ODD_IN = 2496
ODD_IN_PAD = 2560
ROPE_THETA = 10000.0
ATTN_SCALE = (MLA_NOPE + MLA_ROPE) ** -0.5
DEEPNORM_ALPHA = 4.0 ** 0.25
ADAM_LR, ADAM_B1, ADAM_B2, ADAM_EPS, ADAM_WD, ADAM_STEP = 0.001, 0.9, 0.999, 1e-8, 0.01, 10
N_DEV = 8
LANES = 1024
VMEM_LIMIT = 56 * 1024 * 1024
MESH = pl.DeviceIdType.MESH

NT = (((1,), (1,)), ((), ()))
NN = (((1,), (0,)), ((), ()))
TN = (((0,), (0,)), ((), ()))


def _params(n_axes):
    return pltpu.CompilerParams(dimension_semantics=("arbitrary",) * n_axes, vmem_limit_bytes=VMEM_LIMIT)


def _dot(a, b, dn):
    return lax.dot_general(a.astype(BF16), b.astype(BF16), dn, preferred_element_type=F32)


def _sigmoid(z):
    return 1.0 / (1.0 + jnp.exp(-z))


def _colsum(t):
    return jnp.sum(t, axis=0, keepdims=True)


def _matmul(pairs, mode, out_dtype, m, n, tm, tn, name):
    dn = {"nn": NN, "nt": NT, "tn": TN}[mode]
    tm, tn = min(tm, m), min(tn, n)
    n_pairs = len(pairs)

    def body(*refs):
        o_ref = refs[-1]
        acc = None
        for p in range(n_pairs):
            t = _dot(refs[2 * p][...], refs[2 * p + 1][...], dn)
            acc = t if acc is None else acc + t
        o_ref[...] = acc.astype(o_ref.dtype)

    in_specs, args = [], []
    for a, b in pairs:
        if mode == "nn":
            k = a.shape[1]
            in_specs += [pl.BlockSpec((tm, k), lambda i, j: (i, 0)), pl.BlockSpec((k, tn), lambda i, j: (0, j))]
        elif mode == "nt":
            k = a.shape[1]
            in_specs += [pl.BlockSpec((tm, k), lambda i, j: (i, 0)), pl.BlockSpec((tn, k), lambda i, j: (j, 0))]
        else:
            k = a.shape[0]
            in_specs += [pl.BlockSpec((k, tm), lambda i, j: (0, i)), pl.BlockSpec((k, tn), lambda i, j: (0, j))]
        args += [a, b]
    return pl.pallas_call(
        body, name=name, grid=(m // tm, n // tn), in_specs=in_specs,
        out_specs=pl.BlockSpec((tm, tn), lambda i, j: (i, j)),
        out_shape=jax.ShapeDtypeStruct((m, n), out_dtype), compiler_params=_params(2),
    )(*args)


def _matmul_cols_nn(a, w3, out_dtype, tm, name):
    m, k = a.shape
    _, _, n = w3.shape
    tm = min(tm, m)

    def body(a_ref, w_ref, o_ref):
        o_ref[...] = _dot(a_ref[...], w_ref[...], NN).astype(o_ref.dtype)

    return pl.pallas_call(
        body, name=name, grid=(m // tm, N_DEV),
        in_specs=[pl.BlockSpec((tm, k), lambda i, j: (i, 0)), pl.BlockSpec((None, k, n), lambda i, j: (j, 0, 0))],
        out_specs=pl.BlockSpec((tm, n), lambda i, j: (i, j)),
        out_shape=jax.ShapeDtypeStruct((m, N_DEV * n), out_dtype), compiler_params=_params(2),
    )(a, w3)


def _matmul_cols_nt(a, w3, out_dtype, tm, name):
    m = a.shape[0]
    _, k, n = w3.shape
    tm = min(tm, m)

    def body(a_ref, w_ref, o_ref, acc):
        j = pl.program_id(1)

        @pl.when(j == 0)
        def _():
            acc[...] = jnp.zeros_like(acc)

        acc[...] += _dot(a_ref[...], w_ref[...], NT)

        @pl.when(j == N_DEV - 1)
        def _():
            o_ref[...] = acc[...].astype(o_ref.dtype)

    return pl.pallas_call(
        body, name=name, grid=(m // tm, N_DEV),
        in_specs=[pl.BlockSpec((tm, n), lambda i, j: (i, j)), pl.BlockSpec((None, k, n), lambda i, j: (j, 0, 0))],
        out_specs=pl.BlockSpec((tm, k), lambda i, j: (i, 0)),
        out_shape=jax.ShapeDtypeStruct((m, k), out_dtype),
        scratch_shapes=[pltpu.VMEM((tm, k), F32)], compiler_params=_params(2),
    )(a, w3)


def _matmul_cols_tn(a, b, n, out_dtype, tk, name):
    m, k = a.shape
    tk = min(tk, k)

    def body(a_ref, b_ref, o_ref):
        o_ref[...] = _dot(a_ref[...], b_ref[...], TN).astype(o_ref.dtype)

    return pl.pallas_call(
        body, name=name, grid=(N_DEV, k // tk),
        in_specs=[pl.BlockSpec((m, tk), lambda j, i: (0, i)), pl.BlockSpec((m, n), lambda j, i: (0, j))],
        out_specs=pl.BlockSpec((None, tk, n), lambda j, i: (j, i, 0)),
        out_shape=jax.ShapeDtypeStruct((N_DEV, k, n), out_dtype), compiler_params=_params(2),
    )(a, b)


def _rows3(tm, d):
    return pl.BlockSpec((None, tm, d), lambda i: (0, i, 0))


def _modulate(x, mod, name):
    _, s, d = x.shape
    tm = min(s, 512)

    def body(x_ref, m_ref, h_ref):
        shift, scale = m_ref[0:1, 0:d], m_ref[0:1, d:2 * d]
        h_ref[...] = (x_ref[...] * (1.0 + scale) + shift).astype(BF16)

    return pl.pallas_call(
        body, name=name, grid=(s // tm,),
        in_specs=[_rows3(tm, d), pl.BlockSpec((8, 3 * d), lambda i: (0, 0))],
        out_specs=pl.BlockSpec((tm, d), lambda i: (i, 0)),
        out_shape=jax.ShapeDtypeStruct((s, d), BF16), compiler_params=_params(1),
    )(x, mod)


def _ln_stats(r):
    mu = jnp.mean(r, axis=-1, keepdims=True)
    rc = r - mu
    var = jnp.mean(rc * rc, axis=-1, keepdims=True)
    rstd = lax.rsqrt(var + LN_EPS)
    return rc * rstd, rstd


def _ln_bwd(dxhat, xhat, rstd):
    return rstd * (dxhat - jnp.mean(dxhat, axis=-1, keepdims=True)
                   - xhat * jnp.mean(dxhat * xhat, axis=-1, keepdims=True))


def _resid_ln(x, y, mod, g, b, name):
    _, s, d = x.shape
    tm = min(s, 512)

    def body(x_ref, y_ref, m_ref, g_ref, b_ref, o_ref):
        gate = m_ref[0:1, 2 * d:3 * d]
        xhat, _ = _ln_stats(DEEPNORM_ALPHA * x_ref[...] + (1.0 + gate) * y_ref[...])
        o_ref[...] = xhat * g_ref[...] + b_ref[...]

    row = pl.BlockSpec((tm, d), lambda i: (i, 0))
    vec = pl.BlockSpec((1, d), lambda i: (0, 0))
    return pl.pallas_call(
        body, name=name, grid=(s // tm,),
        in_specs=[_rows3(tm, d), row, pl.BlockSpec((8, 3 * d), lambda i: (0, 0)), vec, vec],
        out_specs=_rows3(tm, d), out_shape=jax.ShapeDtypeStruct((1, s, d), F32), compiler_params=_params(1),
    )(x, y, mod, g, b)


def _final_ln_loss_bwd(x, y, mod, g, b, target, name):
    _, s, d = x.shape
    tm = min(s, 256)

    def body(x_ref, y_ref, m_ref, g_ref, b_ref, t_ref, dy_ref, dx_ref, red_ref):
        @pl.when(pl.program_id(0) == 0)
        def _():
            red_ref[...] = jnp.zeros_like(red_ref)

        gate = m_ref[0:1, 2 * d:3 * d]
        yv = y_ref[...]
        xhat, rstd = _ln_stats(DEEPNORM_ALPHA * x_ref[...] + (1.0 + gate) * yv)
        err = xhat * g_ref[...] + b_ref[...] - t_ref[...]
        dout = err * (1.0 / d)
        dr = _ln_bwd(dout * g_ref[...], xhat, rstd)
        dy_ref[...] = ((1.0 + gate) * dr).astype(BF16)
        dx_ref[...] = DEEPNORM_ALPHA * dr
        red_ref[0:1, :] += _colsum(dout * xhat)
        red_ref[1:2, :] += _colsum(dout)
        red_ref[2:3, :] += _colsum(dr * yv)
        red_ref[3:4, :] += _colsum(err * err)

    row = pl.BlockSpec((tm, d), lambda i: (i, 0))
    vec = pl.BlockSpec((1, d), lambda i: (0, 0))
    return pl.pallas_call(
        body, name=name, grid=(s // tm,),
        in_specs=[_rows3(tm, d), row, pl.BlockSpec((8, 3 * d), lambda i: (0, 0)), vec, vec, _rows3(tm, d)],
        out_specs=[row, row, pl.BlockSpec((8, d), lambda i: (0, 0))],
        out_shape=[jax.ShapeDtypeStruct((s, d), BF16), jax.ShapeDtypeStruct((s, d), F32),
                   jax.ShapeDtypeStruct((8, d), F32)],
        compiler_params=_params(1),
    )(x, y, mod, g, b, target)


def _mid_bwd(dh, dxres, x, y, mod_lo, mod_hi, g, b, name):
    _, s, d = x.shape
    tm = min(s, 256)

    def body(dh_ref, dxr_ref, x_ref, y_ref, ml_ref, mh_ref, g_ref, b_ref, dy_ref, dx_ref, red_ref):
        @pl.when(pl.program_id(0) == 0)
        def _():
            red_ref[...] = jnp.zeros_like(red_ref)

        gate = ml_ref[0:1, 2 * d:3 * d]
        scale_hi = mh_ref[0:1, d:2 * d]
        yv, dhv = y_ref[...], dh_ref[...]
        xhat, rstd = _ln_stats(DEEPNORM_ALPHA * x_ref[...] + (1.0 + gate) * yv)
        x_mid = xhat * g_ref[...] + b_ref[...]
        dx_mid = dxr_ref[...] + dhv * (1.0 + scale_hi)
        dr = _ln_bwd(dx_mid * g_ref[...], xhat, rstd)
        dy_ref[...] = ((1.0 + gate) * dr).astype(BF16)
        dx_ref[...] = DEEPNORM_ALPHA * dr
        red_ref[0:1, :] += _colsum(dhv * x_mid)
        red_ref[1:2, :] += _colsum(dhv)
        red_ref[2:3, :] += _colsum(dx_mid * xhat)
        red_ref[3:4, :] += _colsum(dx_mid)
        red_ref[4:5, :] += _colsum(dr * yv)

    row = pl.BlockSpec((tm, d), lambda i: (i, 0))
    vec = pl.BlockSpec((1, d), lambda i: (0, 0))
    modspec = pl.BlockSpec((8, 3 * d), lambda i: (0, 0))
    return pl.pallas_call(
        body, name=name, grid=(s // tm,),
        in_specs=[row, row, _rows3(tm, d), row, modspec, modspec, vec, vec],
        out_specs=[row, row, pl.BlockSpec((8, d), lambda i: (0, 0))],
        out_shape=[jax.ShapeDtypeStruct((s, d), BF16), jax.ShapeDtypeStruct((s, d), F32),
                   jax.ShapeDtypeStruct((8, d), F32)],
        compiler_params=_params(1),
    )(dh, dxres, x, y, mod_lo, mod_hi, g, b)


def _first_bwd(dh, dxres, x, mod, name):
    _, s, d = x.shape
    tm = min(s, 512)

    def body(dh_ref, dxr_ref, x_ref, m_ref, gx_ref, red_ref):
        @pl.when(pl.program_id(0) == 0)
        def _():
            red_ref[...] = jnp.zeros_like(red_ref)

        scale = m_ref[0:1, d:2 * d]
        dhv = dh_ref[...]
        gx_ref[...] = dxr_ref[...] + dhv * (1.0 + scale)
        red_ref[0:1, :] += _colsum(dhv * x_ref[...])
        red_ref[1:2, :] += _colsum(dhv)

    row = pl.BlockSpec((tm, d), lambda i: (i, 0))
    return pl.pallas_call(
        body, name=name, grid=(s // tm,),
        in_specs=[row, row, _rows3(tm, d), pl.BlockSpec((8, 3 * d), lambda i: (0, 0))],
        out_specs=[_rows3(tm, d), pl.BlockSpec((8, d), lambda i: (0, 0))],
        out_shape=[jax.ShapeDtypeStruct((1, s, d), F32), jax.ShapeDtypeStruct((8, d), F32)],
        compiler_params=_params(1),
    )(dh, dxres, x, mod)


EVEN_TM = 256


def _gmlp_mask():
    t = lax.broadcasted_iota(jnp.int32, (GMLP_BLOCK, GMLP_BLOCK), 0) // CHUNK
    s = lax.broadcasted_iota(jnp.int32, (GMLP_BLOCK, GMLP_BLOCK), 1) // CHUNK
    return s <= t


def _window_sum(ext, win, back):
    n = ext.shape[0]
    k = 1
    while k < win:
        ext = ext + pltpu.roll(ext, k if back else n - k, 0)
        k *= 2
    return ext


def _inv_count(row0, rows, win):
    t = row0 + lax.broadcasted_iota(jnp.int32, (rows, 1), 0)
    return t, 1.0 / jnp.minimum(t + 1, win).astype(F32)


def _pooled(xb, halo, row0, win):
    tm = xb.shape[0]
    sums = _window_sum(jnp.concatenate([halo, xb], axis=0), win, True)[POOL_HALO:]
    _, inv = _inv_count(row0, tm, win)
    return sums * inv - xb


def _even_fwd(proj, ws, bs_col, ng, nb, pw, pb, ps, name):
    s = proj.shape[0]
    tm = min(s, EVEN_TM)
    hd, gd = GMLP_HEAD_DIM, POOL_GROUP_DIM

    def body(p_ref, halo_ref, ws_ref, bs_ref, ng_ref, nb_ref, pw_ref, pb_ref, ps_ref, m_ref):
        i = pl.program_id(0)
        mask = _gmlp_mask()
        for h in range(GMLP_HEADS):
            wm = jnp.where(mask, ws_ref[h], 0.0).astype(BF16)
            for blk in range(tm // GMLP_BLOCK):
                rows = slice(blk * GMLP_BLOCK, (blk + 1) * GMLP_BLOCK)
                cu, cv, cz = h * hd, 1024 + h * hd, 2048 + h * hd
                vhat, _ = _ln_stats(p_ref[rows, cv:cv + hd])
                vn = vhat * ng_ref[...] + nb_ref[...]
                sv = _dot(wm, vn, NN) + bs_ref[h]
                za = p_ref[rows, cz:cz + hd]
                m_ref[rows, cu:cu + hd] = (p_ref[rows, cu:cu + hd] * sv * (za * _sigmoid(za))).astype(BF16)
        for g, win in enumerate(POOL_WINDOWS):
            cx, cz = 3072 + g * gd, 4096 + g * gd
            halo = jnp.where(i > 0, halo_ref[:, g * gd:(g + 1) * gd], 0.0)
            pooled = _pooled(p_ref[:, cx:cx + gd], halo, i * tm, win)
            yb = _dot(pooled, pw_ref[g], NN) + pb_ref[:, g * gd:(g + 1) * gd]
            zb = p_ref[:, cz:cz + gd]
            m_ref[:, 1024 + g * gd:1024 + (g + 1) * gd] = (
                yb * ps_ref[:, g * gd:(g + 1) * gd] * (zb * _sigmoid(zb))).astype(BF16)

    hb = tm // POOL_HALO
    return pl.pallas_call(
        body, name=name, grid=(s // tm,),
        in_specs=[
            pl.BlockSpec((tm, 5120), lambda i: (i, 0)),
            pl.BlockSpec((POOL_HALO, 1024), lambda i: (jnp.maximum(i * hb - 1, 0), 3)),
            pl.BlockSpec((GMLP_HEADS, GMLP_BLOCK, GMLP_BLOCK), lambda i: (0, 0, 0)),
            pl.BlockSpec((GMLP_HEADS, GMLP_BLOCK, 1), lambda i: (0, 0, 0)),
            pl.BlockSpec((1, hd), lambda i: (0, 0)), pl.BlockSpec((1, hd), lambda i: (0, 0)),
            pl.BlockSpec((4, gd, gd), lambda i: (0, 0, 0)),
            pl.BlockSpec((1, 1024), lambda i: (0, 0)), pl.BlockSpec((1, 1024), lambda i: (0, 0)),
        ],
        out_specs=pl.BlockSpec((tm, 2048), lambda i: (i, 0)),
        out_shape=jax.ShapeDtypeStruct((s, 2048), BF16), compiler_params=_params(1),
    )(proj, proj, ws, bs_col, ng, nb, pw, pb, ps)


def _even_bwd(proj, dm, ws, bs_col, ng, nb, pw, pb, ps, name):
    s = proj.shape[0]
    tm = min(s, EVEN_TM)
    hd, gd = GMLP_HEAD_DIM, POOL_GROUP_DIM
    n_tiles = s // tm

    def body(p_ref, halo_ref, zbn_ref, dm_ref, dbn_ref, ws_ref, bs_ref, ng_ref, nb_ref, pw_ref, pb_ref, ps_ref,
             dp_ref, dws_ref, dbs_ref, dng_ref, dnb_ref, dpw_ref, dpb_ref, dps_ref):
        i = pl.program_id(0)

        @pl.when(i == 0)
        def _():
            for r in (dws_ref, dbs_ref, dng_ref, dnb_ref, dpw_ref, dpb_ref, dps_ref):
                r[...] = jnp.zeros_like(r)

        mask = _gmlp_mask()
        for h in range(GMLP_HEADS):
            wm = jnp.where(mask, ws_ref[h], 0.0).astype(BF16)
            for blk in range(tm // GMLP_BLOCK):
                rows = slice(blk * GMLP_BLOCK, (blk + 1) * GMLP_BLOCK)
                cu, cv, cz = h * hd, 1024 + h * hd, 2048 + h * hd
                vhat, rstd = _ln_stats(p_ref[rows, cv:cv + hd])
                vn = (vhat * ng_ref[...] + nb_ref[...]).astype(BF16)
                sv = _dot(wm, vn, NN) + bs_ref[h]
                u, za, da = p_ref[rows, cu:cu + hd], p_ref[rows, cz:cz + hd], dm_ref[rows, cu:cu + hd]
                sig = _sigmoid(za)
                sa = za * sig
                dau = da * u
                dsv = dau * sa
                dp_ref[rows, cu:cu + hd] = (da * sv * sa).astype(BF16)
                dp_ref[rows, cz:cz + hd] = (dau * sv * (sig * (1.0 + za * (1.0 - sig)))).astype(BF16)
                dsv_b = dsv.astype(BF16)
                dbs_ref[h] += jnp.sum(dsv, axis=1, keepdims=True)
                dws_ref[h] += _dot(dsv_b, vn, NT)
                dvn = _dot(wm, dsv_b, TN)
                dng_ref[...] += _colsum(dvn * vhat)
                dnb_ref[...] += _colsum(dvn)
                dp_ref[rows, cv:cv + hd] = _ln_bwd(dvn * ng_ref[...], vhat, rstd).astype(BF16)

        row0 = i * tm
        for g, win in enumerate(POOL_WINDOWS):
            cx, cz, cd = 3072 + g * gd, 4096 + g * gd, 1024 + g * gd
            gs = slice(g * gd, (g + 1) * gd)
            halo = jnp.where(i > 0, halo_ref[:, gs], 0.0)
            xb = p_ref[:, cx:cx + gd]
            pooled = _pooled(xb, halo, row0, win).astype(BF16)
            scale_g = ps_ref[:, gs]
            yb = _dot(pooled, pw_ref[g], NN) + pb_ref[:, gs]
            zb, db = p_ref[:, cz:cz + gd], dm_ref[:, cd:cd + gd]
            sig = _sigmoid(zb)
            dyp = db * (zb * sig)
            dp_ref[:, cz:cz + gd] = (db * yb * scale_g * (sig * (1.0 + zb * (1.0 - sig)))).astype(BF16)
            dps_ref[:, gs] += _colsum(dyp * yb)
            dpb_ref[:, gs] += _colsum(dyp * scale_g)
            zb_ext = jnp.concatenate([zb, zbn_ref[:, gs]], axis=0)
            db_ext = jnp.concatenate([db, dbn_ref[:, gs]], axis=0)
            dy_ext = (db_ext * (zb_ext * _sigmoid(zb_ext)) * scale_g).astype(BF16)
            dpw_ref[g] += _dot(pooled, dy_ext[:tm], TN)
            dpooled = _dot(dy_ext, pw_ref[g], NT)
            t, inv = _inv_count(row0, tm + POOL_HALO, win)
            w_ext = jnp.where(t < s, dpooled * inv, 0.0)
            dp_ref[:, cx:cx + gd] = (_window_sum(w_ext, win, False)[:tm] - dpooled[:tm]).astype(BF16)

    hb = tm // POOL_HALO
    last = s // POOL_HALO - 1
    small = lambda shape: pl.BlockSpec(shape, lambda i: (0,) * len(shape))
    return pl.pallas_call(
        body, name=name, grid=(n_tiles,),
        in_specs=[
            pl.BlockSpec((tm, 5120), lambda i: (i, 0)),
            pl.BlockSpec((POOL_HALO, 1024), lambda i: (jnp.maximum(i * hb - 1, 0), 3)),
            pl.BlockSpec((POOL_HALO, 1024), lambda i: (jnp.minimum((i + 1) * hb, last), 4)),
            pl.BlockSpec((tm, 2048), lambda i: (i, 0)),
            pl.BlockSpec((POOL_HALO, 1024), lambda i: (jnp.minimum((i + 1) * hb, last), 1)),
            small((GMLP_HEADS, GMLP_BLOCK, GMLP_BLOCK)), small((GMLP_HEADS, GMLP_BLOCK, 1)),
            small((1, hd)), small((1, hd)), small((4, gd, gd)), small((1, 1024)), small((1, 1024)),
        ],
        out_specs=[
            pl.BlockSpec((tm, 5120), lambda i: (i, 0)),
            small((GMLP_HEADS, GMLP_BLOCK, GMLP_BLOCK)), small((GMLP_HEADS, GMLP_BLOCK, 1)),
            small((1, hd)), small((1, hd)), small((4, gd, gd)), small((1, 1024)), small((1, 1024)),
        ],
        out_shape=[
            jax.ShapeDtypeStruct((s, 5120), BF16),
            jax.ShapeDtypeStruct((GMLP_HEADS, GMLP_BLOCK, GMLP_BLOCK), F32),
            jax.ShapeDtypeStruct((GMLP_HEADS, GMLP_BLOCK, 1), F32),
            jax.ShapeDtypeStruct((1, hd), F32), jax.ShapeDtypeStruct((1, hd), F32),
            jax.ShapeDtypeStruct((4, gd, gd), F32),
            jax.ShapeDtypeStruct((1, 1024), F32), jax.ShapeDtypeStruct((1, 1024), F32),
        ],
        compiler_params=_params(1),
    )(proj, proj, proj, dm, dm, ws, bs_col, ng, nb, pw, pb, ps)


def _rope_pair_swap(t):
    lane = lax.broadcasted_iota(jnp.int32, t.shape, 1)
    return jnp.where(lane % 64 < 32, pltpu.roll(t, 96, 1), pltpu.roll(t, 32, 1))


def _rms(x, g):
    r = lax.rsqrt(jnp.mean(x * x, axis=-1, keepdims=True) + LN_EPS)
    return x * r, r


def _rms_bwd(dy, g, xhat, r):
    dyg = dy * g
    return r * (dyg - xhat * jnp.mean(dyg * xhat, axis=-1, keepdims=True))


def _lane_lt(shape, n):
    return lax.broadcasted_iota(jnp.int32, shape, 1) < n


def _mla_prep(proj, cosp, sinp, gq, gkv, name):
    s = proj.shape[0]
    tm = min(s, 512)

    def body(qc_ref, kv_ref, c_ref, s_ref, gq_ref, gkv_ref, qn_ref, kp_ref):
        qhat, _ = _rms(qc_ref[...], None)
        qn_ref[...] = (qhat * gq_ref[...]).astype(BF16)
        khat, _ = _rms(kv_ref[:, 0:128], None)
        kp_ref[:, 0:128] = (khat * gkv_ref[...]).astype(BF16)
        kr = kv_ref[:, 128:256]
        kp_ref[:, 128:256] = (kr * c_ref[...] + _rope_pair_swap(kr) * s_ref[...]).astype(BF16)

    return pl.pallas_call(
        body, name=name, grid=(s // tm,),
        in_specs=[pl.BlockSpec((tm, 256), lambda i: (i, 0)), pl.BlockSpec((tm, 256), lambda i: (i, 1)),
                  pl.BlockSpec((tm, 128), lambda i: (i, 0)), pl.BlockSpec((tm, 128), lambda i: (i, 0)),
                  pl.BlockSpec((1, 256), lambda i: (0, 0)), pl.BlockSpec((1, 128), lambda i: (0, 0))],
        out_specs=[pl.BlockSpec((tm, 256), lambda i: (i, 0)), pl.BlockSpec((tm, 256), lambda i: (i, 0))],
        out_shape=[jax.ShapeDtypeStruct((s, 256), BF16), jax.ShapeDtypeStruct((s, 256), BF16)],
        compiler_params=_params(1),
    )(proj, proj, cosp, sinp, gq, gkv)


def _mla_prep_bwd(proj, dqn, dkp, dv, cosp, sinp, gq, gkv, dproj, name):
    s = proj.shape[0]
    tm = min(s, 512)

    def body(qc_ref, kv_ref, dqn_ref, dkp_ref, dv_ref, c_ref, s_ref, gq_ref, gkv_ref, dproj_ref, o_ref, red_ref):
        @pl.when(pl.program_id(0) == 0)
        def _():
            red_ref[...] = jnp.zeros_like(red_ref)

        qhat, qr = _rms(qc_ref[...], None)
        dq = dqn_ref[...]
        o_ref[:, 0:256] = _rms_bwd(dq, gq_ref[...], qhat, qr).astype(BF16)
        red_ref[0:1, :] += _colsum(dq * qhat)
        khat, kr = _rms(kv_ref[:, 0:128], None)
        dk = dkp_ref[:, 0:128] + dv_ref[...]
        o_ref[:, 256:384] = _rms_bwd(dk, gkv_ref[...], khat, kr).astype(BF16)
        red_ref[1:2, 0:128] += _colsum(dk * khat)
        dr = dkp_ref[:, 128:256]
        o_ref[:, 384:512] = (dr * c_ref[...] - _rope_pair_swap(dr) * s_ref[...]).astype(BF16)

    return pl.pallas_call(
        body, name=name, grid=(s // tm,),
        in_specs=[pl.BlockSpec((tm, 256), lambda i: (i, 0)), pl.BlockSpec((tm, 256), lambda i: (i, 1)),
                  pl.BlockSpec((tm, 256), lambda i: (i, 0)), pl.BlockSpec((tm, 256), lambda i: (i, 0)),
                  pl.BlockSpec((tm, 128), lambda i: (i, 0)),
                  pl.BlockSpec((tm, 128), lambda i: (i, 0)), pl.BlockSpec((tm, 128), lambda i: (i, 0)),
                  pl.BlockSpec((1, 256), lambda i: (0, 0)), pl.BlockSpec((1, 128), lambda i: (0, 0)),
                  pl.BlockSpec(memory_space=pl.ANY)],
        out_specs=[pl.BlockSpec((tm, 512), lambda i: (i, 0)), pl.BlockSpec((8, 256), lambda i: (0, 0))],
        out_shape=[jax.ShapeDtypeStruct(dproj.shape, BF16), jax.ShapeDtypeStruct((8, 256), F32)],
        input_output_aliases={9: 0}, compiler_params=_params(1),
    )(proj, proj, dqn, dkp, dv, cosp, sinp, gq, gkv, dproj)


HEAD_PAIRS = MLA_HEADS // 2
PAIR_TM = 512


def _q_heads(q_up, cosp, sinp, wuk, name):
    s = q_up.shape[0]
    tm = min(s, PAIR_TM)

    def body(qn_ref, qr_ref, c_ref, s_ref, w_ref, o_ref):
        raw = qr_ref[...]
        rot = raw * c_ref[...] + _rope_pair_swap(raw) * s_ref[...]
        low = _lane_lt(rot.shape, 64)
        o_ref[0, :, 0:128] = _dot(qn_ref[:, 0:128], w_ref[:, 0:128], NT).astype(BF16)
        o_ref[0, :, 128:256] = jnp.where(low, rot, 0.0).astype(BF16)
        o_ref[1, :, 0:128] = _dot(qn_ref[:, 128:256], w_ref[:, 128:256], NT).astype(BF16)
        o_ref[1, :, 128:256] = jnp.where(low, pltpu.roll(rot, 64, 1), 0.0).astype(BF16)

    return pl.pallas_call(
        body, name=name, grid=(s // tm, HEAD_PAIRS),
        in_specs=[pl.BlockSpec((tm, 256), lambda i, p: (i, p)),
                  pl.BlockSpec((tm, 128), lambda i, p: (i, 16 + p)),
                  pl.BlockSpec((tm, 128), lambda i, p: (i, 0)), pl.BlockSpec((tm, 128), lambda i, p: (i, 0)),
                  pl.BlockSpec((128, 256), lambda i, p: (0, p))],
        out_specs=pl.BlockSpec((2, tm, 256), lambda i, p: (p, i, 0)),
        out_shape=jax.ShapeDtypeStruct((MLA_HEADS, s, 256), BF16), compiler_params=_params(2),
    )(q_up, q_up, cosp, sinp, wuk)


def _q_heads_bwd(dqp, q_up, cosp, sinp, wuk, name):
    s = q_up.shape[0]
    tm = min(s, PAIR_TM)

    def body(dq_ref, qn_ref, c_ref, s_ref, w_ref, dn_ref, dr_ref, dw_ref):
        @pl.when(pl.program_id(1) == 0)
        def _():
            dw_ref[...] = jnp.zeros_like(dw_ref)

        for a in range(2):
            dlat = dq_ref[a, :, 0:128]
            cols = slice(128 * a, 128 * a + 128)
            dn_ref[:, cols] = _dot(dlat, w_ref[:, cols], NN).astype(BF16)
            dw_ref[:, cols] += _dot(dlat, qn_ref[:, cols], TN)
        drot = dq_ref[0, :, 128:256].astype(F32) + pltpu.roll(dq_ref[1, :, 128:256].astype(F32), 64, 1)
        dr_ref[...] = (drot * c_ref[...] - _rope_pair_swap(drot) * s_ref[...]).astype(BF16)

    return pl.pallas_call(
        body, name=name, grid=(HEAD_PAIRS, s // tm),
        in_specs=[pl.BlockSpec((2, tm, 256), lambda p, i: (p, i, 0)),
                  pl.BlockSpec((tm, 256), lambda p, i: (i, p)),
                  pl.BlockSpec((tm, 128), lambda p, i: (i, 0)), pl.BlockSpec((tm, 128), lambda p, i: (i, 0)),
                  pl.BlockSpec((128, 256), lambda p, i: (0, p))],
        out_specs=[pl.BlockSpec((tm, 256), lambda p, i: (i, p)),
                   pl.BlockSpec((tm, 128), lambda p, i: (i, p)),
                   pl.BlockSpec((128, 256), lambda p, i: (0, p))],
        out_shape=[jax.ShapeDtypeStruct((s, MLA_WIDTH), BF16), jax.ShapeDtypeStruct((s, 1024), BF16),
                   jax.ShapeDtypeStruct((128, MLA_WIDTH), F32)],
        compiler_params=_params(2),
    )(dqp, q_up, cosp, sinp, wuk)


def _o_gate(o_lat, proj, wuv, name):
    s = o_lat.shape[1]
    tm = min(s, PAIR_TM)

    def body(ol_ref, z_ref, w_ref, g_ref):
        for a in range(2):
            cols = slice(128 * a, 128 * a + 128)
            z = z_ref[:, cols]
            g_ref[:, cols] = (_dot(ol_ref[a], w_ref[:, cols], NN) * (z * _sigmoid(z))).astype(BF16)

    return pl.pallas_call(
        body, name=name, grid=(s // tm, HEAD_PAIRS),
        in_specs=[pl.BlockSpec((2, tm, 128), lambda i, p: (p, i, 0)),
                  pl.BlockSpec((tm, 256), lambda i, p: (i, 2 + p)),
                  pl.BlockSpec((128, 256), lambda i, p: (0, p))],
        out_specs=pl.BlockSpec((tm, 256), lambda i, p: (i, p)),
        out_shape=jax.ShapeDtypeStruct((s, MLA_WIDTH), BF16), compiler_params=_params(2),
    )(o_lat, proj, wuv)


def _o_gate_bwd(dg, o_lat, proj, wuv, name):
    s = o_lat.shape[1]
    tm = min(s, PAIR_TM)

    def body(dg_ref, ol_ref, z_ref, w_ref, dz_ref, dol_ref, dw_ref):
        @pl.when(pl.program_id(1) == 0)
        def _():
            dw_ref[...] = jnp.zeros_like(dw_ref)

        for a in range(2):
            cols = slice(128 * a, 128 * a + 128)
            z, dgv, ol = z_ref[:, cols], dg_ref[:, cols], ol_ref[a]
            sig = _sigmoid(z)
            o = _dot(ol, w_ref[:, cols], NN)
            dz_ref[:, cols] = (dgv * o * (sig * (1.0 + z * (1.0 - sig)))).astype(BF16)
            do = (dgv * (z * sig)).astype(BF16)
            dol_ref[a] = _dot(do, w_ref[:, cols], NT).astype(BF16)
            dw_ref[:, cols] += _dot(ol, do, TN)

    return pl.pallas_call(
        body, name=name, grid=(HEAD_PAIRS, s // tm),
        in_specs=[pl.BlockSpec((tm, 256), lambda p, i: (i, p)),
                  pl.BlockSpec((2, tm, 128), lambda p, i: (p, i, 0)),
                  pl.BlockSpec((tm, 256), lambda p, i: (i, 2 + p)),
                  pl.BlockSpec((128, 256), lambda p, i: (0, p))],
        out_specs=[pl.BlockSpec((tm, 256), lambda p, i: (i, 2 + p)),
                   pl.BlockSpec((2, tm, 128), lambda p, i: (p, i, 0)),
                   pl.BlockSpec((128, 256), lambda p, i: (0, p))],
        out_shape=[jax.ShapeDtypeStruct((s, ODD_IN_PAD), BF16), jax.ShapeDtypeStruct((MLA_HEADS, s, 128), BF16),
                   jax.ShapeDtypeStruct((128, MLA_WIDTH), F32)],
        compiler_params=_params(2),
    )(dg, o_lat, proj, wuv)


ATT_TQ = CHUNK
ATT_ROWS = ATT_TQ * MLA_HEADS
ATT_TK = 512


def _visible(k0, q_chunk, tk):
    kpos = k0 + lax.broadcasted_iota(jnp.int32, (1, tk), 1)
    return kpos // CHUNK <= q_chunk


def _tile_lanes(t, n):
    return jnp.concatenate([t] * (n // 128), axis=1)


def _key_blocks(i, tk, block):
    n_blocks = (i * ATT_TQ + ATT_TQ + tk - 1) // tk

    def unmasked(j, carry):
        block(j, False)
        return carry

    lax.fori_loop(0, n_blocks - 1, unmasked, 0)
    block(n_blocks - 1, True)


def _attn_fwd(qp, kp, name):
    s = kp.shape[0]
    tk = min(ATT_TK, s)

    def body(q_ref, k_ref, o_ref, lse_ref, m_sc, acc_sc):
        i = pl.program_id(0)
        q = q_ref[...].reshape(ATT_ROWS, 256)
        m_sc[...] = jnp.full_like(m_sc, -jnp.inf)
        acc_sc[...] = jnp.zeros_like(acc_sc)

        def block(j, masked):
            k0 = pl.multiple_of(j * tk, tk)
            k = k_ref[pl.ds(k0, tk), :]
            sc = _dot(q, k, NT) * ATTN_SCALE
            if masked:
                sc = jnp.where(_visible(k0, i, tk), sc, -jnp.inf)
            m_prev = m_sc[...]
            m_new = jnp.maximum(m_prev, jnp.max(sc, axis=1, keepdims=True))
            p = jnp.exp(sc - _tile_lanes(m_new, tk))
            v1 = jnp.where(_lane_lt(k.shape, 128), k, jnp.ones_like(k))
            acc_sc[...] = _tile_lanes(jnp.exp(m_prev - m_new), 256) * acc_sc[...] + _dot(p, v1, NN)
            m_sc[...] = m_new

        _key_blocks(i, tk, block)
        acc = acc_sc[...]
        l = acc[:, 128:256]
        o_ref[...] = (acc[:, 0:128] / l).astype(BF16).reshape(MLA_HEADS, ATT_TQ, 128)
        lse_ref[...] = (m_sc[...] + jnp.log(l)).reshape(MLA_HEADS, ATT_TQ, 128)

    head128 = pl.BlockSpec((MLA_HEADS, ATT_TQ, 128), lambda i: (0, i, 0))
    return pl.pallas_call(
        body, name=name, grid=(s // ATT_TQ,),
        in_specs=[pl.BlockSpec((MLA_HEADS, ATT_TQ, 256), lambda i: (0, i, 0)), pl.BlockSpec((s, 256), lambda i: (0, 0))],
        out_specs=[head128, head128],
        out_shape=[jax.ShapeDtypeStruct((MLA_HEADS, s, 128), BF16), jax.ShapeDtypeStruct((MLA_HEADS, s, 128), F32)],
        scratch_shapes=[pltpu.VMEM((ATT_ROWS, 128), F32), pltpu.VMEM((ATT_ROWS, 256), F32)],
        compiler_params=_params(1),
    )(qp, kp)


def _attn_bwd(qp, kp, o, do, lse, name):
    s = kp.shape[0]
    tk = min(ATT_TK, s)

    def body(q_ref, k_ref, o_ref, do_ref, lse_ref, dq_ref, dk_ref, dv_ref, dq_sc):
        i = pl.program_id(0)

        @pl.when(i == 0)
        def _():
            dk_ref[...] = jnp.zeros_like(dk_ref)
            dv_ref[...] = jnp.zeros_like(dv_ref)

        q = q_ref[...].reshape(ATT_ROWS, 256)
        dov = do_ref[...].reshape(ATT_ROWS, 128)
        delta = jnp.sum(dov.astype(F32) * o_ref[...].reshape(ATT_ROWS, 128).astype(F32), axis=1, keepdims=True)
        delta_t = _tile_lanes(jnp.broadcast_to(delta, (ATT_ROWS, 128)), tk)
        lse_t = _tile_lanes(lse_ref[...].reshape(ATT_ROWS, 128), tk)
        dq_sc[...] = jnp.zeros_like(dq_sc)

        def block(j, masked):
            k0 = pl.multiple_of(j * tk, tk)
            k = k_ref[pl.ds(k0, tk), :]
            p = jnp.exp(_dot(q, k, NT) * ATTN_SCALE - lse_t)
            if masked:
                p = jnp.where(_visible(k0, i, tk), p, 0.0)
            dv_ref[pl.ds(k0, tk), :] += _dot(p, dov, TN)
            ds = (p * (_dot(dov, k[:, 0:128], NT) - delta_t) * ATTN_SCALE).astype(BF16)
            dq_sc[...] += _dot(ds, k, NN)
            dk_ref[pl.ds(k0, tk), :] += _dot(ds, q, TN)

        _key_blocks(i, tk, block)
        dq_ref[...] = dq_sc[...].astype(BF16).reshape(MLA_HEADS, ATT_TQ, 256)

    head128 = pl.BlockSpec((MLA_HEADS, ATT_TQ, 128), lambda i: (0, i, 0))
    head256 = pl.BlockSpec((MLA_HEADS, ATT_TQ, 256), lambda i: (0, i, 0))
    return pl.pallas_call(
        body, name=name, grid=(s // ATT_TQ,),
        in_specs=[head256, pl.BlockSpec((s, 256), lambda i: (0, 0)), head128, head128, head128],
        out_specs=[head256, pl.BlockSpec((s, 256), lambda i: (0, 0)), pl.BlockSpec((s, 128), lambda i: (0, 0))],
        out_shape=[jax.ShapeDtypeStruct((MLA_HEADS, s, 256), BF16),
                   jax.ShapeDtypeStruct((s, 256), F32), jax.ShapeDtypeStruct((s, 128), F32)],
        scratch_shapes=[pltpu.VMEM((ATT_ROWS, 256), F32)],
        compiler_params=_params(1),
    )(qp, kp, o, do, lse)


def _place():
    x, y, c = lax.axis_index("x"), lax.axis_index("y"), lax.axis_index("c")
    return x, y, c, 4 * x + 2 * y + c


def _flip(x, y, c, r):
    px = 1 - x if r & 4 else x
    py = 1 - y if r & 2 else y
    pc = 1 - c if r & 1 else c
    return (px, py, pc), 4 * px + 2 * py + pc


def _adaln_exchange(c8, ada_w, ada_b_cols, name):
    d = c8.shape[1]
    w_cols = ada_w.shape[2]

    def body(c_ref, w_ref, b_ref, call_ref, mod_ref, sbuf, rbuf, s1, r1, s2, r2):
        x, y, c, me = _place()
        call_ref[pl.ds(pl.multiple_of(me * 8, 8), 8), :] = c_ref[...]
        peers = [_flip(x, y, c, r) for r in range(1, N_DEV)]

        def c_copy(k, src_lin, to):
            rows = call_ref.at[pl.ds(pl.multiple_of(src_lin * 8, 8), 8), :]
            return pltpu.make_async_remote_copy(src_ref=rows, dst_ref=rows, send_sem=s1.at[k], recv_sem=r1.at[k],
                                                device_id=to, device_id_type=MESH)

        first = [c_copy(k, me, peer) for k, (peer, _) in enumerate(peers)]
        for cp in first:
            cp.start()
        for k, (_, lin) in enumerate(peers):
            c_copy(k, lin, (x, y, c)).wait_recv()
        for cp in first:
            cp.wait_send()

        for j in range(N_DEV):
            cj = call_ref[8 * j:8 * j + 8, :]
            cond = cj * _sigmoid(cj)
            for l in range(2):
                sbuf[j, l] = lax.dot_general(cond, w_ref[l], NN, precision=lax.Precision.HIGHEST,
                                             preferred_element_type=F32) + b_ref[l]

        def m_copy(k, src_slot, dst_slot, to):
            return pltpu.make_async_remote_copy(src_ref=sbuf.at[src_slot], dst_ref=rbuf.at[dst_slot],
                                                send_sem=s2.at[k], recv_sem=r2.at[k], device_id=to,
                                                device_id_type=MESH)

        rbuf[me] = sbuf[me]
        second = [m_copy(k, lin, me, peer) for k, (peer, lin) in enumerate(peers)]
        for cp in second:
            cp.start()
        for k, (_, lin) in enumerate(peers):
            m_copy(k, lin, lin, (x, y, c)).wait_recv()
        for cp in second:
            cp.wait_send()
        for j in range(N_DEV):
            for l in range(2):
                mod_ref[l, :, w_cols * j:w_cols * (j + 1)] = rbuf[j, l]

    vmem = pl.BlockSpec(memory_space=pltpu.VMEM)
    return pl.pallas_call(
        body, name=name, in_specs=[vmem, vmem, vmem], out_specs=[vmem, vmem],
        out_shape=[jax.ShapeDtypeStruct((8 * N_DEV, d), F32), jax.ShapeDtypeStruct((2, 8, 3 * d), F32)],
        scratch_shapes=[pltpu.VMEM((N_DEV, 2, 8, w_cols), F32), pltpu.VMEM((N_DEV, 2, 8, w_cols), F32),
                        pltpu.SemaphoreType.DMA((N_DEV - 1,)), pltpu.SemaphoreType.DMA((N_DEV - 1,)),
                        pltpu.SemaphoreType.DMA((N_DEV - 1,)), pltpu.SemaphoreType.DMA((N_DEV - 1,))],
        compiler_params=pltpu.CompilerParams(vmem_limit_bytes=VMEM_LIMIT),
    )(c8, ada_w, ada_b_cols)


def _all_gather(blocks, name):
    n_arr = len(blocks)

    def body(*refs):
        x_refs, out_refs = refs[:n_arr], refs[n_arr:2 * n_arr]
        send_sems, recv_sems, local_sems = refs[2 * n_arr:]
        x, y, c, _ = _place()
        me, sibling = (x, y, c), (x, y, 1 - c)
        chips = [(1 - x, y), (x, 1 - y), (1 - x, 1 - y)]

        def copy(t, k, blk, to, src=None):
            slot = out_refs[t].at[4 * blk[0] + 2 * blk[1] + blk[2]]
            return pltpu.make_async_remote_copy(src_ref=slot if src is None else src, dst_ref=slot,
                                                send_sem=send_sems.at[7 * t + k], recv_sem=recv_sems.at[7 * t + k],
                                                device_id=to, device_id_type=MESH)

        mine = [pltpu.make_async_copy(x_refs[t], out_refs[t].at[4 * x + 2 * y + c], local_sems.at[t])
                for t in range(n_arr)]
        for cp in mine:
            cp.start()
        first = []
        for t in range(n_arr):
            first.append(copy(t, 0, me, sibling, src=x_refs[t]))
            first += [copy(t, 1 + j, me, (*chip, c), src=x_refs[t]) for j, chip in enumerate(chips)]
        for cp in first:
            cp.start()
        passed = []
        for t in range(n_arr):
            for j, chip in enumerate(chips):
                copy(t, 1 + j, (*chip, c), me).wait_recv()
                passed.append(copy(t, 4 + j, (*chip, c), sibling))
                passed[-1].start()
        for t in range(n_arr):
            copy(t, 0, sibling, me).wait_recv()
            for j, chip in enumerate(chips):
                copy(t, 4 + j, (*chip, 1 - c), me).wait_recv()
        for cp in first + passed:
            cp.wait_send()
        for cp in mine:
            cp.wait()

    anyspace = pl.BlockSpec(memory_space=pl.ANY)
    return pl.pallas_call(
        body, name=name, in_specs=[anyspace] * n_arr, out_specs=[anyspace] * n_arr,
        out_shape=[jax.ShapeDtypeStruct((N_DEV,) + b.shape, b.dtype) for b in blocks],
        scratch_shapes=[pltpu.SemaphoreType.DMA((7 * n_arr,)), pltpu.SemaphoreType.DMA((7 * n_arr,)),
                        pltpu.SemaphoreType.DMA((n_arr,))],
    )(*blocks)


def _scatter_parts(parts, name):
    n_arr = len(parts)

    def body(*refs):
        g_refs, r_refs = refs[:n_arr], refs[n_arr:2 * n_arr]
        send_sems, recv_sems, local_sems = refs[2 * n_arr:]
        x, y, c, me = _place()
        own = [pltpu.make_async_copy(g_refs[t].at[me], r_refs[t].at[me], local_sems.at[t]) for t in range(n_arr)]
        for cp in own:
            cp.start()
        sends, recvs = [], []
        for r in range(1, N_DEV):
            peer, lin = _flip(x, y, c, r)
            for t in range(n_arr):
                k = n_arr * (r - 1) + t
                sends.append(pltpu.make_async_remote_copy(
                    src_ref=g_refs[t].at[lin], dst_ref=r_refs[t].at[me], send_sem=send_sems.at[k],
                    recv_sem=recv_sems.at[k], device_id=peer, device_id_type=MESH))
                recvs.append(pltpu.make_async_remote_copy(
                    src_ref=g_refs[t].at[lin], dst_ref=r_refs[t].at[lin], send_sem=send_sems.at[k],
                    recv_sem=recv_sems.at[k], device_id=(x, y, c), device_id_type=MESH))
        for cp in sends:
            cp.start()
        for cp in recvs:
            cp.wait_recv()
        for cp in sends:
            cp.wait_send()
        for cp in own:
            cp.wait()

    anyspace = pl.BlockSpec(memory_space=pl.ANY)
    n = n_arr * (N_DEV - 1)
    return pl.pallas_call(
        body, name=name, in_specs=[anyspace] * n_arr, out_specs=[anyspace] * n_arr,
        out_shape=[jax.ShapeDtypeStruct(p.shape, p.dtype) for p in parts],
        scratch_shapes=[pltpu.SemaphoreType.DMA((n,)), pltpu.SemaphoreType.DMA((n,)),
                        pltpu.SemaphoreType.DMA((n_arr,))],
    )(*parts)


def _adamw(w, g, m, v):
    m = ADAM_B1 * m + (1.0 - ADAM_B1) * g
    v = ADAM_B2 * v + (1.0 - ADAM_B2) * (g * g)
    m_hat = m / (1.0 - ADAM_B1 ** ADAM_STEP)
    v_hat = v / (1.0 - ADAM_B2 ** ADAM_STEP)
    return -ADAM_LR * (m_hat / (jnp.sqrt(v_hat) + ADAM_EPS) + ADAM_WD * w), m, v


def _sum_parts_adamw(parts, w, m, v, name):
    _, rows, cols = parts.shape
    tr = max(t for t in range(16, 129, 16) if rows % t == 0)

    def body(p_ref, w_ref, m_ref, v_ref, g_ref, d_ref, mo_ref, vo_ref):
        g = p_ref[0].astype(F32)
        for j in range(1, N_DEV):
            g = g + p_ref[j].astype(F32)
        g_ref[...] = g
        d_ref[...], mo_ref[...], vo_ref[...] = _adamw(w_ref[...], g, m_ref[...], v_ref[...])

    row = _rows3(tr, cols)
    out = jax.ShapeDtypeStruct((1, rows, cols), F32)
    return pl.pallas_call(
        body, name=name, grid=(rows // tr,),
        in_specs=[pl.BlockSpec((N_DEV, tr, cols), lambda i: (0, i, 0)), row, row, row],
        out_specs=[row, row, row, row], out_shape=[out, out, out, out], compiler_params=_params(1),
    )(parts, w, m, v)


def _adamw_call(g, w, m, v, name):
    rows, cols = g.shape
    tr = 128 if rows % 128 == 0 else rows

    def body(g_ref, w_ref, m_ref, v_ref, d_ref, mo_ref, vo_ref):
        d_ref[...], mo_ref[...], vo_ref[...] = _adamw(w_ref[...], g_ref[...], m_ref[...], v_ref[...])

    row = pl.BlockSpec((tr, cols), lambda i: (i, 0))
    out = jax.ShapeDtypeStruct((rows, cols), F32)
    return pl.pallas_call(
        body, name=name, grid=(rows // tr,), in_specs=[row] * 4, out_specs=[row] * 3, out_shape=[out] * 3,
        compiler_params=_params(1),
    )(g, w, m, v)


def _ada_w_grad_adamw(c_all, dmod_rows, w, m, v, name):
    def body(c_ref, dm_ref, w_ref, m_ref, v_ref, g_ref, d_ref, mo_ref, vo_ref):
        cv = c_ref[...]
        cond = cv * _sigmoid(cv)
        for l in range(2):
            g = lax.dot_general(cond, dm_ref[l], TN, precision=lax.Precision.HIGHEST, preferred_element_type=F32)
            g_ref[l] = g
            d_ref[l], mo_ref[l], vo_ref[l] = _adamw(w_ref[l], g, m_ref[l], v_ref[l])

    out = jax.ShapeDtypeStruct(w.shape, F32)
    return pl.pallas_call(
        body, name=name, out_shape=[out] * 4, compiler_params=pltpu.CompilerParams(vmem_limit_bytes=VMEM_LIMIT),
    )(c_all, dmod_rows, w, m, v)


REPLICATED = ("ln_g", "ln_b", "gmlp_norm_g", "gmlp_norm_b", "gmlp_ws", "gmlp_bs", "pool_b", "pool_scale",
              "mla_kv_norm_g", "mla_w_uk", "mla_w_uv")
CHUNK_ROWS, ADA_ROW, QNORM_ROW, REP_ROWS = 73, 73, 74, 80
UQ_ROWS, POOLW_ROWS = 96, 32
SMALL_ROWS = REP_ROWS + UQ_ROWS + POOLW_ROWS


def _pad_rows(flat2d, rows):
    n, k = flat2d.shape
    return jnp.pad(flat2d, ((0, 0), (0, rows * LANES - k))).reshape(n, rows, LANES)


def _ada_cols_rows(vec):
    return _pad_rows(vec.reshape(2, N_DEV, -1).transpose(1, 0, 2).reshape(N_DEV, -1), 1)


def _pack_replicated(src, ada_vec):
    flat = jnp.concatenate([src[n].reshape(-1) for n in REPLICATED])
    body = _pad_rows(flat.reshape(N_DEV, -1), CHUNK_ROWS)
    return jnp.concatenate([body, jnp.pad(_ada_cols_rows(ada_vec), ((0, 0), (0, REP_ROWS - CHUNK_ROWS - 1), (0, 0)))],
                           axis=1)


def _unpack_replicated(rep, shapes):
    chunk = sum(s[1] for s in shapes) // N_DEV
    flat, off, out = rep[:, :CHUNK_ROWS].reshape(N_DEV, -1)[:, :chunk].reshape(-1), 0, {}
    for n, size, shape in shapes:
        out[n] = flat[off:off + size].reshape(shape)
        off += size
    cols = 3 * D_MODEL // N_DEV
    out["ada_b"] = rep[:, ADA_ROW, :2 * cols].reshape(N_DEV, 2, cols).transpose(1, 0, 2).reshape(2, -1)
    return out


def kernel(x, c, positions, ada_w, ada_b, ln_g, ln_b, e_w_in, gmlp_norm_g, gmlp_norm_b, gmlp_ws, gmlp_bs, pool_w, pool_b, pool_scale, e_w_out, o_w_in, mla_q_norm_g, mla_kv_norm_g, mla_w_uq, mla_w_uk, mla_w_uv, o_w_out, loss_target, m_ada_w, m_ada_b, m_ln_g, m_ln_b, m_e_w_in, m_gmlp_norm_g, m_gmlp_norm_b, m_gmlp_ws, m_gmlp_bs, m_pool_w, m_pool_b, m_pool_scale, m_e_w_out, m_o_w_in, m_mla_q_norm_g, m_mla_kv_norm_g, m_mla_w_uq, m_mla_w_uk, m_mla_w_uv, m_o_w_out, v_ada_w, v_ada_b, v_ln_g, v_ln_b, v_e_w_in, v_gmlp_norm_g, v_gmlp_norm_b, v_gmlp_ws, v_gmlp_bs, v_pool_w, v_pool_b, v_pool_scale, v_e_w_out, v_o_w_in, v_mla_q_norm_g, v_mla_kv_norm_g, v_mla_w_uq, v_mla_w_uk, v_mla_w_uv, v_o_w_out):
    w_in = dict(ada_w=ada_w, ada_b=ada_b, ln_g=ln_g, ln_b=ln_b, e_w_in=e_w_in, gmlp_norm_g=gmlp_norm_g,
                gmlp_norm_b=gmlp_norm_b, gmlp_ws=gmlp_ws, gmlp_bs=gmlp_bs, pool_w=pool_w, pool_b=pool_b,
                pool_scale=pool_scale, e_w_out=e_w_out, o_w_in=o_w_in, mla_q_norm_g=mla_q_norm_g,
                mla_kv_norm_g=mla_kv_norm_g, mla_w_uq=mla_w_uq, mla_w_uk=mla_w_uk, mla_w_uv=mla_w_uv, o_w_out=o_w_out)
    m_in = dict(ada_w=m_ada_w, ada_b=m_ada_b, ln_g=m_ln_g, ln_b=m_ln_b, e_w_in=m_e_w_in, gmlp_norm_g=m_gmlp_norm_g,
                gmlp_norm_b=m_gmlp_norm_b, gmlp_ws=m_gmlp_ws, gmlp_bs=m_gmlp_bs, pool_w=m_pool_w, pool_b=m_pool_b,
                pool_scale=m_pool_scale, e_w_out=m_e_w_out, o_w_in=m_o_w_in, mla_q_norm_g=m_mla_q_norm_g,
                mla_kv_norm_g=m_mla_kv_norm_g, mla_w_uq=m_mla_w_uq, mla_w_uk=m_mla_w_uk, mla_w_uv=m_mla_w_uv,
                o_w_out=m_o_w_out)
    v_in = dict(ada_w=v_ada_w, ada_b=v_ada_b, ln_g=v_ln_g, ln_b=v_ln_b, e_w_in=v_e_w_in, gmlp_norm_g=v_gmlp_norm_g,
                gmlp_norm_b=v_gmlp_norm_b, gmlp_ws=v_gmlp_ws, gmlp_bs=v_gmlp_bs, pool_w=v_pool_w, pool_b=v_pool_b,
                pool_scale=v_pool_scale, e_w_out=v_e_w_out, o_w_in=v_o_w_in, mla_q_norm_g=v_mla_q_norm_g,
                mla_kv_norm_g=v_mla_kv_norm_g, mla_w_uq=v_mla_w_uq, mla_w_uk=v_mla_w_uk, mla_w_uv=v_mla_w_uv,
                o_w_out=v_o_w_out)
    names = list(w_in)
    seq = x.shape[1]
    d = D_MODEL
    me = 4 * lax.axis_index("x") + 2 * lax.axis_index("y") + lax.axis_index("c")
    ada_cols = ada_w.shape[2]

    ada_b_cols = lax.dynamic_slice_in_dim(ada_b, me * ada_cols, ada_cols, axis=1)
    slab_row = lax.broadcasted_iota(jnp.int32, (8, d), 0)
    slab = jnp.where(slab_row == 0, c, jnp.where(slab_row == 1, jnp.pad(mla_q_norm_g, ((0, 0), (0, d - 32))), 0.0))
    c_all, mod = _adaln_exchange(slab, ada_w,
                                 jnp.broadcast_to(ada_b_cols[:, None, :], (2, 8, ada_cols)), "adaln_exchange")

    uq_len = mla_w_uq.size
    small_b = jnp.concatenate([mla_w_uq.reshape(-1), pool_w.reshape(-1)]).astype(BF16).reshape(-1, LANES)
    w_in_e3, o_in3, w_out_e3, w_out_o3, small3 = _all_gather(
        [e_w_in[0].astype(BF16), o_w_in[0].astype(BF16), e_w_out[0].astype(BF16), o_w_out[0].astype(BF16), small_b],
        "weight_gather")
    w_out_e, w_out_o = w_out_e3.reshape(-1, d), w_out_o3.reshape(-1, d)
    o_in_full = o_in3.transpose(1, 0, 2).reshape(d, ODD_IN)
    w_in_o = jnp.concatenate([o_in_full[:, :448], jnp.zeros((d, 64), BF16), o_in_full[:, 448:]], axis=1)
    small_flat = small3.reshape(N_DEV, -1)
    uq_full = small_flat[:, :uq_len].reshape(MLA_Q_RANK, MLA_HEADS, MLA_NOPE + MLA_ROPE)
    pool_w_full = small_flat[:, uq_len:].reshape(N_DEV, 4, 32, 256).transpose(1, 0, 2, 3).reshape(4, 256, 256)
    w_uq_n = uq_full[:, :, :MLA_NOPE].reshape(MLA_Q_RANK, -1)
    w_uq_r = uq_full[:, :, MLA_NOPE:].reshape(MLA_Q_RANK, -1)
    w_uq = jnp.concatenate([w_uq_n, w_uq_r], axis=1)
    g_q = c_all.reshape(N_DEV, 8, d)[:, 1, :32].reshape(1, MLA_Q_RANK)

    ws, bs_col = gmlp_ws[0], gmlp_bs[0].reshape(GMLP_HEADS, GMLP_BLOCK, 1)
    wuk2, wuv2 = mla_w_uk[0].reshape(MLA_KV_RANK, -1), mla_w_uv[0].reshape(MLA_KV_RANK, -1)
    inv = 1.0 / (ROPE_THETA ** (jnp.arange(0, MLA_ROPE, 2, dtype=F32) / MLA_ROPE))
    ang = positions[0].astype(F32)[:, None] * inv
    cosp = jnp.tile(jnp.cos(ang), (1, 4))
    sinp = jnp.tile(jnp.concatenate([-jnp.sin(ang), jnp.sin(ang)], axis=1), (1, 2))

    h0 = _modulate(x, mod[0], "modulate0")
    proj0 = _matmul_cols_nn(h0, w_in_e3, F32, 512, "even_in")
    mix0 = _even_fwd(proj0, ws, bs_col, gmlp_norm_g, gmlp_norm_b, pool_w_full, pool_b, pool_scale, "even_mix")
    y0 = _matmul([(mix0, w_out_e)], "nn", F32, seq, d, 512, 1024, "even_out")
    x1 = _resid_ln(x, y0, mod[0], ln_g[0:1], ln_b[0:1], "resid_ln0")

    h1 = _modulate(x1, mod[1], "modulate1")
    proj1 = _matmul([(h1, w_in_o)], "nn", F32, seq, ODD_IN_PAD, 512, 512, "odd_in")
    qn, kp = _mla_prep(proj1, cosp, sinp, g_q, mla_kv_norm_g, "mla_prep")
    q_up = _matmul([(qn, w_uq)], "nn", F32, seq, 3072, 512, 1024, "q_up")
    qp = _q_heads(q_up, cosp, sinp, wuk2, "q_heads")
    o_lat, lse = _attn_fwd(qp, kp, "attn_fwd")
    gated = _o_gate(o_lat, proj1, wuv2, "o_gate")
    y1 = _matmul([(gated, w_out_o)], "nn", F32, seq, d, 512, 1024, "odd_out")

    dy1, dxres1, red2 = _final_ln_loss_bwd(x1, y1, mod[1], ln_g[1:2], ln_b[1:2], loss_target, "final_ln_loss")
    dgated = _matmul([(dy1, w_out_o)], "nt", F32, seq, MLA_WIDTH, 512, 512, "odd_out_dx")
    g_w_out_o = _matmul([(gated, dy1)], "tn", BF16, MLA_WIDTH, d, 256, 512, "odd_out_dw")
    dproj1_z, do_lat, g_wuv = _o_gate_bwd(dgated, o_lat, proj1, wuv2, "o_gate_bwd")
    dqp, dkp, dvv = _attn_bwd(qp, kp, o_lat, do_lat, lse, "attn_bwd")
    dq_nope, dq_rope, g_wuk = _q_heads_bwd(dqp, q_up, cosp, sinp, wuk2, "q_heads_bwd")
    dqn = _matmul([(dq_nope, w_uq_n), (dq_rope, w_uq_r)], "nt", F32, seq, MLA_Q_RANK, 512, 256, "q_up_dx")
    g_wuq_n = _matmul([(qn, dq_nope)], "tn", F32, MLA_Q_RANK, MLA_WIDTH, 256, 512, "q_up_dw_nope")
    g_wuq_r = _matmul([(qn, dq_rope)], "tn", F32, MLA_Q_RANK, 1024, 256, 512, "q_up_dw_rope")
    dproj1, red_mla = _mla_prep_bwd(proj1, dqn, dkp, dvv, cosp, sinp, g_q, mla_kv_norm_g, dproj1_z, "mla_prep_bwd")
    dh1 = _matmul([(dproj1, w_in_o)], "nt", F32, seq, d, 512, 512, "odd_in_dx")
    g_w_in_o = _matmul([(h1, dproj1)], "tn", BF16, d, ODD_IN_PAD, 256, 512, "odd_in_dw")
    dy0, dxres0, red1 = _mid_bwd(dh1, dxres1, x, y0, mod[0], mod[1], ln_g[0:1], ln_b[0:1], "mid_bwd")
    dmix = _matmul([(dy0, w_out_e)], "nt", F32, seq, 2048, 512, 512, "even_out_dx")
    g_w_out_e = _matmul([(mix0, dy0)], "tn", BF16, 2048, d, 256, 512, "even_out_dw")
    dproj0, g_ws, g_bs, g_ng, g_nb, g_pw, g_pb, g_ps = _even_bwd(
        proj0, dmix, ws, bs_col, gmlp_norm_g, gmlp_norm_b, pool_w_full, pool_b, pool_scale, "even_mix_bwd")
    dh0 = _matmul_cols_nt(dproj0, w_in_e3, F32, 512, "even_in_dx")
    part_e_in = _matmul_cols_tn(h0, dproj0, w_in_e3.shape[2], BF16, 256, "even_in_dw")
    grad_x, red0 = _first_bwd(dh0, dxres0, x, mod[0], "first_bwd")

    loss = lax.psum(0.5 / d * jnp.sum(red2[3]), ("x", "y", "c"))

    t_mask = lax.broadcasted_iota(jnp.int32, (GMLP_BLOCK, GMLP_BLOCK), 0) // CHUNK
    s_mask = lax.broadcasted_iota(jnp.int32, (GMLP_BLOCK, GMLP_BLOCK), 1) // CHUNK
    part = {
        "ln_g": jnp.stack([red1[2], red2[0]]), "ln_b": jnp.stack([red1[3], red2[1]]),
        "gmlp_norm_g": g_ng, "gmlp_norm_b": g_nb,
        "gmlp_ws": jnp.where(s_mask <= t_mask, g_ws, 0.0), "gmlp_bs": g_bs,
        "pool_b": g_pb, "pool_scale": g_ps, "mla_kv_norm_g": red_mla[1, :MLA_KV_RANK],
        "mla_w_uk": g_wuk, "mla_w_uv": g_wuv,
    }
    dmod = jnp.stack([jnp.concatenate([red0[1], red0[0], red1[4]]),
                      jnp.concatenate([red1[1], red1[0], red2[2]])])

    part_o_in = jnp.concatenate([g_w_in_o[:, :448], g_w_in_o[:, 512:]], axis=1).reshape(d, N_DEV, -1).transpose(1, 0, 2)
    g_uq = jnp.concatenate([g_wuq_n.reshape(MLA_Q_RANK, MLA_HEADS, MLA_NOPE),
                            g_wuq_r.reshape(MLA_Q_RANK, MLA_HEADS, MLA_ROPE)], axis=2)
    part_small = jnp.concatenate([
        _pad_rows(jnp.concatenate([part[n].reshape(-1) for n in REPLICATED]).reshape(N_DEV, -1), CHUNK_ROWS),
        jnp.pad(jnp.concatenate([_ada_cols_rows(dmod), _pad_rows(red_mla[0].reshape(N_DEV, -1), 1)], axis=1),
                ((0, 0), (0, REP_ROWS - QNORM_ROW - 1), (0, 0))),
        g_uq.reshape(N_DEV, UQ_ROWS, LANES),
        g_pw.reshape(4, N_DEV, 32, 256).transpose(1, 0, 2, 3).reshape(N_DEV, POOLW_ROWS, LANES)], axis=1)
    r_e_in, r_o_in, r_e_out, r_o_out, r_small = _scatter_parts(
        [part_e_in, part_o_in, g_w_out_e.reshape(N_DEV, -1, d), g_w_out_o.reshape(N_DEV, -1, d), part_small],
        "grad_scatter")

    def small_local(src):
        return jnp.concatenate([jnp.pad(src["mla_q_norm_g"], ((QNORM_ROW, REP_ROWS - QNORM_ROW - 1), (0, LANES - 32))),
                                src["mla_w_uq"].reshape(UQ_ROWS, LANES), src["pool_w"].reshape(POOLW_ROWS, LANES)])[None]

    res = {"e_w_in": _sum_parts_adamw(r_e_in, e_w_in, m_e_w_in, v_e_w_in, "adamw_e_w_in"),
           "o_w_in": _sum_parts_adamw(r_o_in, o_w_in, m_o_w_in, v_o_w_in, "adamw_o_w_in"),
           "e_w_out": _sum_parts_adamw(r_e_out, e_w_out, m_e_w_out, v_e_w_out, "adamw_e_w_out"),
           "o_w_out": _sum_parts_adamw(r_o_out, o_w_out, m_o_w_out, v_o_w_out, "adamw_o_w_out")}
    small = _sum_parts_adamw(r_small, small_local(w_in), small_local(m_in), small_local(v_in), "adamw_small")
    for n, r0, r1 in (("mla_w_uq", REP_ROWS, REP_ROWS + UQ_ROWS), ("pool_w", REP_ROWS + UQ_ROWS, SMALL_ROWS)):
        res[n] = [t[0, r0:r1].reshape(w_in[n].shape) for t in small]
    res["mla_q_norm_g"] = [t[0, QNORM_ROW:QNORM_ROW + 1, :32] for t in small]
    (rep_sum,) = _all_gather([small[0][0, :REP_ROWS]], "replicated_gather")
    rep_w, rep_m, rep_v = (_pack_replicated(src, src["ada_b"]).reshape(-1, LANES) for src in (w_in, m_in, v_in))
    rep_res = (rep_sum,) + tuple(t.reshape(N_DEV, REP_ROWS, LANES)
                                 for t in _adamw_call(rep_sum.reshape(-1, LANES), rep_w, rep_m, rep_v, "replicated_adamw"))
    shapes = [(n, w_in[n].size, w_in[n].shape) for n in REPLICATED]
    for k, t in enumerate(rep_res):
        for n, val in _unpack_replicated(t, shapes).items():
            res.setdefault(n, [None] * 4)[k] = val
    dmod_all = r_small[:, ADA_ROW, :2 * ada_cols].reshape(N_DEV, 2, ada_cols).transpose(1, 0, 2)
    dmod_rows = jnp.pad(dmod_all[:, :, None, :], ((0, 0), (0, 0), (0, 7), (0, 0))).reshape(2, 8 * N_DEV, ada_cols)
    res["ada_w"] = _ada_w_grad_adamw(c_all, dmod_rows, ada_w, m_ada_w, v_ada_w, "ada_w_adamw")

    return (loss, grad_x, *[res[n][0] for n in names], *[res[n][1] for n in names],
            *[res[n][2] for n in names], *[res[n][3] for n in names])
```

```python
import functools

import jax
import jax.numpy as jnp
from jax import lax
from jax.experimental import pallas as pl
from jax.experimental.pallas import tpu as pltpu

F32 = jnp.float32
BF16 = jnp.bfloat16

D_MODEL = 1024
CHUNK = 64
LN_EPS = 1e-5
GMLP_HEADS = 4
GMLP_HEAD_DIM = 256
GMLP_BLOCK = 128
POOL_WINDOWS = (2, 4, 8, 16)
POOL_GROUP_DIM = 256
POOL_HALO = 16
MLA_HEADS = 16
MLA_NOPE = 128
MLA_ROPE = 64
MLA_Q_RANK = 256
MLA_KV_RANK = 128
MLA_WIDTH = 2048
ODD_IN = 2496
ODD_IN_PAD = 2560
ROPE_THETA = 10000.0
ATTN_SCALE = (MLA_NOPE + MLA_ROPE) ** -0.5
ATTN_SCALE_LOG2 = ATTN_SCALE * 1.4426950408889634
DEEPNORM_ALPHA = 4.0 ** 0.25
ADAM_LR, ADAM_B1, ADAM_B2, ADAM_EPS, ADAM_WD, ADAM_STEP = 0.001, 0.9, 0.999, 1e-8, 0.01, 10
N_DEV = 8
LANES = 1024
VMEM_LIMIT = 56 * 1024 * 1024
MESH = pl.DeviceIdType.MESH

NT = (((1,), (1,)), ((), ()))
NN = (((1,), (0,)), ((), ()))
TN = (((0,), (0,)), ((), ()))


def _params(n_axes):
    return pltpu.CompilerParams(dimension_semantics=("arbitrary",) * n_axes, vmem_limit_bytes=VMEM_LIMIT)


def _dot(a, b, dn):
    return lax.dot_general(a.astype(BF16), b.astype(BF16), dn, preferred_element_type=F32)


def _sigmoid(z):
    return 1.0 / (1.0 + jnp.exp(-z))


def _colsum(t):
    return jnp.sum(t, axis=0, keepdims=True)


def _matmul(pairs, mode, out_dtype, m, n, tm, tn, name):
    dn = {"nn": NN, "nt": NT, "tn": TN}[mode]
    tm, tn = min(tm, m), min(tn, n)
    n_pairs = len(pairs)

    def body(*refs):
        o_ref = refs[-1]
        acc = None
        for p in range(n_pairs):
            t = _dot(refs[2 * p][...], refs[2 * p + 1][...], dn)
            acc = t if acc is None else acc + t
        o_ref[...] = acc.astype(o_ref.dtype)

    in_specs, args = [], []
    for a, b in pairs:
        if mode == "nn":
            k = a.shape[1]
            in_specs += [pl.BlockSpec((tm, k), lambda i, j: (i, 0)), pl.BlockSpec((k, tn), lambda i, j: (0, j))]
        elif mode == "nt":
            k = a.shape[1]
            in_specs += [pl.BlockSpec((tm, k), lambda i, j: (i, 0)), pl.BlockSpec((tn, k), lambda i, j: (j, 0))]
        else:
            k = a.shape[0]
            in_specs += [pl.BlockSpec((k, tm), lambda i, j: (0, i)), pl.BlockSpec((k, tn), lambda i, j: (0, j))]
        args += [a, b]
    return pl.pallas_call(
        body, name=name, grid=(m // tm, n // tn), in_specs=in_specs,
        out_specs=pl.BlockSpec((tm, tn), lambda i, j: (i, j)),
        out_shape=jax.ShapeDtypeStruct((m, n), out_dtype), compiler_params=_params(2),
    )(*args)


def _matmul_cols_nn(a, w3, out_dtype, tm, name):
    m, k = a.shape
    _, _, n = w3.shape
    tm = min(tm, m)

    def body(a_ref, w_ref, o_ref):
        o_ref[...] = _dot(a_ref[...], w_ref[...], NN).astype(o_ref.dtype)

    return pl.pallas_call(
        body, name=name, grid=(m // tm, N_DEV),
        in_specs=[pl.BlockSpec((tm, k), lambda i, j: (i, 0)), pl.BlockSpec((None, k, n), lambda i, j: (j, 0, 0))],
        out_specs=pl.BlockSpec((tm, n), lambda i, j: (i, j)),
        out_shape=jax.ShapeDtypeStruct((m, N_DEV * n), out_dtype), compiler_params=_params(2),
    )(a, w3)


def _matmul_cols_nt(a, w3, out_dtype, tm, name):
    m = a.shape[0]
    _, k, n = w3.shape
    tm = min(tm, m)

    def body(a_ref, w_ref, o_ref, acc):
        j = pl.program_id(1)

        @pl.when(j == 0)
        def _():
            acc[...] = jnp.zeros_like(acc)

        acc[...] += _dot(a_ref[...], w_ref[...], NT)

        @pl.when(j == N_DEV - 1)
        def _():
            o_ref[...] = acc[...].astype(o_ref.dtype)

    return pl.pallas_call(
        body, name=name, grid=(m // tm, N_DEV),
        in_specs=[pl.BlockSpec((tm, n), lambda i, j: (i, j)), pl.BlockSpec((None, k, n), lambda i, j: (j, 0, 0))],
        out_specs=pl.BlockSpec((tm, k), lambda i, j: (i, 0)),
        out_shape=jax.ShapeDtypeStruct((m, k), out_dtype),
        scratch_shapes=[pltpu.VMEM((tm, k), F32)], compiler_params=_params(2),
    )(a, w3)


def _matmul_cols_tn(a, b, n, out_dtype, tk, name):
    m, k = a.shape
    tk = min(tk, k)

    def body(a_ref, b_ref, o_ref):
        o_ref[...] = _dot(a_ref[...], b_ref[...], TN).astype(o_ref.dtype)

    return pl.pallas_call(
        body, name=name, grid=(N_DEV, k // tk),
        in_specs=[pl.BlockSpec((m, tk), lambda j, i: (0, i)), pl.BlockSpec((m, n), lambda j, i: (0, j))],
        out_specs=pl.BlockSpec((None, tk, n), lambda j, i: (j, i, 0)),
        out_shape=jax.ShapeDtypeStruct((N_DEV, k, n), out_dtype), compiler_params=_params(2),
    )(a, b)


def _rows3(tm, d):
    return pl.BlockSpec((None, tm, d), lambda i: (0, i, 0))


def _modulate(x, mod, name):
    _, s, d = x.shape
    tm = min(s, 512)

    def body(x_ref, m_ref, h_ref):
        shift, scale = m_ref[0:1, 0:d], m_ref[0:1, d:2 * d]
        h_ref[...] = (x_ref[...] * (1.0 + scale) + shift).astype(BF16)

    return pl.pallas_call(
        body, name=name, grid=(s // tm,),
        in_specs=[_rows3(tm, d), pl.BlockSpec((8, 3 * d), lambda i: (0, 0))],
        out_specs=pl.BlockSpec((tm, d), lambda i: (i, 0)),
        out_shape=jax.ShapeDtypeStruct((s, d), BF16), compiler_params=_params(1),
    )(x, mod)


def _ln_stats(r):
    mu = jnp.mean(r, axis=-1, keepdims=True)
    rc = r - mu
    var = jnp.mean(rc * rc, axis=-1, keepdims=True)
    rstd = lax.rsqrt(var + LN_EPS)
    return rc * rstd, rstd


def _ln_bwd(dxhat, xhat, rstd):
    return rstd * (dxhat - jnp.mean(dxhat, axis=-1, keepdims=True)
                   - xhat * jnp.mean(dxhat * xhat, axis=-1, keepdims=True))


def _resid_ln(x, y, mod, g, b, name):
    _, s, d = x.shape
    tm = min(s, 512)

    def body(x_ref, y_ref, m_ref, g_ref, b_ref, o_ref):
        gate = m_ref[0:1, 2 * d:3 * d]
        xhat, _ = _ln_stats(DEEPNORM_ALPHA * x_ref[...] + (1.0 + gate) * y_ref[...])
        o_ref[...] = xhat * g_ref[...] + b_ref[...]

    row = pl.BlockSpec((tm, d), lambda i: (i, 0))
    vec = pl.BlockSpec((1, d), lambda i: (0, 0))
    return pl.pallas_call(
        body, name=name, grid=(s // tm,),
        in_specs=[_rows3(tm, d), row, pl.BlockSpec((8, 3 * d), lambda i: (0, 0)), vec, vec],
        out_specs=_rows3(tm, d), out_shape=jax.ShapeDtypeStruct((1, s, d), F32), compiler_params=_params(1),
    )(x, y, mod, g, b)


def _final_ln_loss_bwd(x, y, mod, g, b, target, name):
    _, s, d = x.shape
    tm = min(s, 256)

    def body(x_ref, y_ref, m_ref, g_ref, b_ref, t_ref, dy_ref, dx_ref, red_ref):
        @pl.when(pl.program_id(0) == 0)
        def _():
            red_ref[...] = jnp.zeros_like(red_ref)

        gate = m_ref[0:1, 2 * d:3 * d]
        yv = y_ref[...]
        xhat, rstd = _ln_stats(DEEPNORM_ALPHA * x_ref[...] + (1.0 + gate) * yv)
        err = xhat * g_ref[...] + b_ref[...] - t_ref[...]
        dout = err * (1.0 / d)
        dr = _ln_bwd(dout * g_ref[...], xhat, rstd)
        dy_ref[...] = ((1.0 + gate) * dr).astype(BF16)
        dx_ref[...] = DEEPNORM_ALPHA * dr
        red_ref[0:1, :] += _colsum(dout * xhat)
        red_ref[1:2, :] += _colsum(dout)
        red_ref[2:3, :] += _colsum(dr * yv)
        red_ref[3:4, :] += _colsum(err * err)

    row = pl.BlockSpec((tm, d), lambda i: (i, 0))
    vec = pl.BlockSpec((1, d), lambda i: (0, 0))
    return pl.pallas_call(
        body, name=name, grid=(s // tm,),
        in_specs=[_rows3(tm, d), row, pl.BlockSpec((8, 3 * d), lambda i: (0, 0)), vec, vec, _rows3(tm, d)],
        out_specs=[row, row, pl.BlockSpec((8, d), lambda i: (0, 0))],
        out_shape=[jax.ShapeDtypeStruct((s, d), BF16), jax.ShapeDtypeStruct((s, d), F32),
                   jax.ShapeDtypeStruct((8, d), F32)],
        compiler_params=_params(1),
    )(x, y, mod, g, b, target)


def _mid_bwd(dh, dxres, x, y, mod_lo, mod_hi, g, b, name):
    _, s, d = x.shape
    tm = min(s, 256)

    def body(dh_ref, dxr_ref, x_ref, y_ref, ml_ref, mh_ref, g_ref, b_ref, dy_ref, dx_ref, red_ref):
        @pl.when(pl.program_id(0) == 0)
        def _():
            red_ref[...] = jnp.zeros_like(red_ref)

        gate = ml_ref[0:1, 2 * d:3 * d]
        scale_hi = mh_ref[0:1, d:2 * d]
        yv, dhv = y_ref[...], dh_ref[...]
        xhat, rstd = _ln_stats(DEEPNORM_ALPHA * x_ref[...] + (1.0 + gate) * yv)
        x_mid = xhat * g_ref[...] + b_ref[...]
        dx_mid = dxr_ref[...] + dhv * (1.0 + scale_hi)
        dr = _ln_bwd(dx_mid * g_ref[...], xhat, rstd)
        dy_ref[...] = ((1.0 + gate) * dr).astype(BF16)
        dx_ref[...] = DEEPNORM_ALPHA * dr
        red_ref[0:1, :] += _colsum(dhv * x_mid)
        red_ref[1:2, :] += _colsum(dhv)
        red_ref[2:3, :] += _colsum(dx_mid * xhat)
        red_ref[3:4, :] += _colsum(dx_mid)
        red_ref[4:5, :] += _colsum(dr * yv)

    row = pl.BlockSpec((tm, d), lambda i: (i, 0))
    vec = pl.BlockSpec((1, d), lambda i: (0, 0))
    modspec = pl.BlockSpec((8, 3 * d), lambda i: (0, 0))
    return pl.pallas_call(
        body, name=name, grid=(s // tm,),
        in_specs=[row, row, _rows3(tm, d), row, modspec, modspec, vec, vec],
        out_specs=[row, row, pl.BlockSpec((8, d), lambda i: (0, 0))],
        out_shape=[jax.ShapeDtypeStruct((s, d), BF16), jax.ShapeDtypeStruct((s, d), F32),
                   jax.ShapeDtypeStruct((8, d), F32)],
        compiler_params=_params(1),
    )(dh, dxres, x, y, mod_lo, mod_hi, g, b)


def _first_bwd(dh, dxres, x, mod, name):
    _, s, d = x.shape
    tm = min(s, 512)

    def body(dh_ref, dxr_ref, x_ref, m_ref, gx_ref, red_ref):
        @pl.when(pl.program_id(0) == 0)
        def _():
            red_ref[...] = jnp.zeros_like(red_ref)

        scale = m_ref[0:1, d:2 * d]
        dhv = dh_ref[...]
        gx_ref[...] = dxr_ref[...] + dhv * (1.0 + scale)
        red_ref[0:1, :] += _colsum(dhv * x_ref[...])
        red_ref[1:2, :] += _colsum(dhv)

    row = pl.BlockSpec((tm, d), lambda i: (i, 0))
    return pl.pallas_call(
        body, name=name, grid=(s // tm,),
        in_specs=[row, row, _rows3(tm, d), pl.BlockSpec((8, 3 * d), lambda i: (0, 0))],
        out_specs=[_rows3(tm, d), pl.BlockSpec((8, d), lambda i: (0, 0))],
        out_shape=[jax.ShapeDtypeStruct((1, s, d), F32), jax.ShapeDtypeStruct((8, d), F32)],
        compiler_params=_params(1),
    )(dh, dxres, x, mod)


EVEN_TM = 256


def _gmlp_mask():
    t = lax.broadcasted_iota(jnp.int32, (GMLP_BLOCK, GMLP_BLOCK), 0) // CHUNK
    s = lax.broadcasted_iota(jnp.int32, (GMLP_BLOCK, GMLP_BLOCK), 1) // CHUNK
    return s <= t


def _window_sum(ext, win, back):
    n = ext.shape[0]
    k = 1
    while k < win:
        ext = ext + pltpu.roll(ext, k if back else n - k, 0)
        k *= 2
    return ext


def _inv_count(row0, rows, win):
    t = row0 + lax.broadcasted_iota(jnp.int32, (rows, 1), 0)
    return t, 1.0 / jnp.minimum(t + 1, win).astype(F32)


def _pooled(xb, halo, row0, win):
    tm = xb.shape[0]
    sums = _window_sum(jnp.concatenate([halo, xb], axis=0), win, True)[POOL_HALO:]
    _, inv = _inv_count(row0, tm, win)
    return sums * inv - xb


def _even_fwd(proj, ws, bs_col, ng, nb, pw, pb, ps, name):
    s = proj.shape[0]
    tm = min(s, EVEN_TM)
    hd, gd = GMLP_HEAD_DIM, POOL_GROUP_DIM

    def body(p_ref, halo_ref, ws_ref, bs_ref, ng_ref, nb_ref, pw_ref, pb_ref, ps_ref, m_ref):
        i = pl.program_id(0)
        mask = _gmlp_mask()
        for h in range(GMLP_HEADS):
            wm = jnp.where(mask, ws_ref[h], 0.0).astype(BF16)
            for blk in range(tm // GMLP_BLOCK):
                rows = slice(blk * GMLP_BLOCK, (blk + 1) * GMLP_BLOCK)
                cu, cv, cz = h * hd, 1024 + h * hd, 2048 + h * hd
                vhat, _ = _ln_stats(p_ref[rows, cv:cv + hd])
                vn = vhat * ng_ref[...] + nb_ref[...]
                sv = _dot(wm, vn, NN) + bs_ref[h]
                za = p_ref[rows, cz:cz + hd]
                m_ref[rows, cu:cu + hd] = (p_ref[rows, cu:cu + hd] * sv * (za * _sigmoid(za))).astype(BF16)
        for g, win in enumerate(POOL_WINDOWS):
            cx, cz = 3072 + g * gd, 4096 + g * gd
            halo = jnp.where(i > 0, halo_ref[:, g * gd:(g + 1) * gd], 0.0)
            pooled = _pooled(p_ref[:, cx:cx + gd], halo, i * tm, win)
            yb = _dot(pooled, pw_ref[g], NN) + pb_ref[:, g * gd:(g + 1) * gd]
            zb = p_ref[:, cz:cz + gd]
            m_ref[:, 1024 + g * gd:1024 + (g + 1) * gd] = (
                yb * ps_ref[:, g * gd:(g + 1) * gd] * (zb * _sigmoid(zb))).astype(BF16)

    hb = tm // POOL_HALO
    return pl.pallas_call(
        body, name=name, grid=(s // tm,),
        in_specs=[
            pl.BlockSpec((tm, 5120), lambda i: (i, 0)),
            pl.BlockSpec((POOL_HALO, 1024), lambda i: (jnp.maximum(i * hb - 1, 0), 3)),
            pl.BlockSpec((GMLP_HEADS, GMLP_BLOCK, GMLP_BLOCK), lambda i: (0, 0, 0)),
            pl.BlockSpec((GMLP_HEADS, GMLP_BLOCK, 1), lambda i: (0, 0, 0)),
            pl.BlockSpec((1, hd), lambda i: (0, 0)), pl.BlockSpec((1, hd), lambda i: (0, 0)),
            pl.BlockSpec((4, gd, gd), lambda i: (0, 0, 0)),
            pl.BlockSpec((1, 1024), lambda i: (0, 0)), pl.BlockSpec((1, 1024), lambda i: (0, 0)),
        ],
        out_specs=pl.BlockSpec((tm, 2048), lambda i: (i, 0)),
        out_shape=jax.ShapeDtypeStruct((s, 2048), BF16), compiler_params=_params(1),
    )(proj, proj, ws, bs_col, ng, nb, pw, pb, ps)


def _even_bwd(proj, dm, ws, bs_col, ng, nb, pw, pb, ps, name):
    s = proj.shape[0]
    tm = min(s, EVEN_TM)
    hd, gd = GMLP_HEAD_DIM, POOL_GROUP_DIM
    n_tiles = s // tm

    def body(p_ref, halo_ref, zbn_ref, dm_ref, dbn_ref, ws_ref, bs_ref, ng_ref, nb_ref, pw_ref, pb_ref, ps_ref,
             dp_ref, dws_ref, dbs_ref, dng_ref, dnb_ref, dpw_ref, dpb_ref, dps_ref):
        i = pl.program_id(0)

        @pl.when(i == 0)
        def _():
            for r in (dws_ref, dbs_ref, dng_ref, dnb_ref, dpw_ref, dpb_ref, dps_ref):
                r[...] = jnp.zeros_like(r)

        mask = _gmlp_mask()
        for h in range(GMLP_HEADS):
            wm = jnp.where(mask, ws_ref[h], 0.0).astype(BF16)
            for blk in range(tm // GMLP_BLOCK):
                rows = slice(blk * GMLP_BLOCK, (blk + 1) * GMLP_BLOCK)
                cu, cv, cz = h * hd, 1024 + h * hd, 2048 + h * hd
                vhat, rstd = _ln_stats(p_ref[rows, cv:cv + hd])
                vn = (vhat * ng_ref[...] + nb_ref[...]).astype(BF16)
                sv = _dot(wm, vn, NN) + bs_ref[h]
                u, za, da = p_ref[rows, cu:cu + hd], p_ref[rows, cz:cz + hd], dm_ref[rows, cu:cu + hd]
                sig = _sigmoid(za)
                sa = za * sig
                dau = da * u
                dsv = dau * sa
                dp_ref[rows, cu:cu + hd] = (da * sv * sa).astype(BF16)
                dp_ref[rows, cz:cz + hd] = (dau * sv * (sig * (1.0 + za * (1.0 - sig)))).astype(BF16)
                dsv_b = dsv.astype(BF16)
                dbs_ref[h] += jnp.sum(dsv, axis=1, keepdims=True)
                dws_ref[h] += _dot(dsv_b, vn, NT)
                dvn = _dot(wm, dsv_b, TN)
                dng_ref[...] += _colsum(dvn * vhat)
                dnb_ref[...] += _colsum(dvn)
                dp_ref[rows, cv:cv + hd] = _ln_bwd(dvn * ng_ref[...], vhat, rstd).astype(BF16)

        row0 = i * tm
        for g, win in enumerate(POOL_WINDOWS):
            cx, cz, cd = 3072 + g * gd, 4096 + g * gd, 1024 + g * gd
            gs = slice(g * gd, (g + 1) * gd)
            halo = jnp.where(i > 0, halo_ref[:, gs], 0.0)
            xb = p_ref[:, cx:cx + gd]
            pooled = _pooled(xb, halo, row0, win).astype(BF16)
            scale_g = ps_ref[:, gs]
            yb = _dot(pooled, pw_ref[g], NN) + pb_ref[:, gs]
            zb, db = p_ref[:, cz:cz + gd], dm_ref[:, cd:cd + gd]
            sig = _sigmoid(zb)
            dyp = db * (zb * sig)
            dp_ref[:, cz:cz + gd] = (db * yb * scale_g * (sig * (1.0 + zb * (1.0 - sig)))).astype(BF16)
            dps_ref[:, gs] += _colsum(dyp * yb)
            dpb_ref[:, gs] += _colsum(dyp * scale_g)
            zb_ext = jnp.concatenate([zb, zbn_ref[:, gs]], axis=0)
            db_ext = jnp.concatenate([db, dbn_ref[:, gs]], axis=0)
            dy_ext = (db_ext * (zb_ext * _sigmoid(zb_ext)) * scale_g).astype(BF16)
            dpw_ref[g] += _dot(pooled, dy_ext[:tm], TN)
            dpooled = _dot(dy_ext, pw_ref[g], NT)
            t, inv = _inv_count(row0, tm + POOL_HALO, win)
            w_ext = jnp.where(t < s, dpooled * inv, 0.0)
            dp_ref[:, cx:cx + gd] = (_window_sum(w_ext, win, False)[:tm] - dpooled[:tm]).astype(BF16)

    hb = tm // POOL_HALO
    last = s // POOL_HALO - 1
    small = lambda shape: pl.BlockSpec(shape, lambda i: (0,) * len(shape))
    return pl.pallas_call(
        body, name=name, grid=(n_tiles,),
        in_specs=[
            pl.BlockSpec((tm, 5120), lambda i: (i, 0)),
            pl.BlockSpec((POOL_HALO, 1024), lambda i: (jnp.maximum(i * hb - 1, 0), 3)),
            pl.BlockSpec((POOL_HALO, 1024), lambda i: (jnp.minimum((i + 1) * hb, last), 4)),
            pl.BlockSpec((tm, 2048), lambda i: (i, 0)),
            pl.BlockSpec((POOL_HALO, 1024), lambda i: (jnp.minimum((i + 1) * hb, last), 1)),
            small((GMLP_HEADS, GMLP_BLOCK, GMLP_BLOCK)), small((GMLP_HEADS, GMLP_BLOCK, 1)),
            small((1, hd)), small((1, hd)), small((4, gd, gd)), small((1, 1024)), small((1, 1024)),
        ],
        out_specs=[
            pl.BlockSpec((tm, 5120), lambda i: (i, 0)),
            small((GMLP_HEADS, GMLP_BLOCK, GMLP_BLOCK)), small((GMLP_HEADS, GMLP_BLOCK, 1)),
            small((1, hd)), small((1, hd)), small((4, gd, gd)), small((1, 1024)), small((1, 1024)),
        ],
        out_shape=[
            jax.ShapeDtypeStruct((s, 5120), BF16),
            jax.ShapeDtypeStruct((GMLP_HEADS, GMLP_BLOCK, GMLP_BLOCK), F32),
            jax.ShapeDtypeStruct((GMLP_HEADS, GMLP_BLOCK, 1), F32),
            jax.ShapeDtypeStruct((1, hd), F32), jax.ShapeDtypeStruct((1, hd), F32),
            jax.ShapeDtypeStruct((4, gd, gd), F32),
            jax.ShapeDtypeStruct((1, 1024), F32), jax.ShapeDtypeStruct((1, 1024), F32),
        ],
        compiler_params=_params(1),
    )(proj, proj, proj, dm, dm, ws, bs_col, ng, nb, pw, pb, ps)


def _rope_pair_swap(t):
    lane = lax.broadcasted_iota(jnp.int32, t.shape, 1)
    return jnp.where(lane % 64 < 32, pltpu.roll(t, 96, 1), pltpu.roll(t, 32, 1))


def _rms(x, g):
    r = lax.rsqrt(jnp.mean(x * x, axis=-1, keepdims=True) + LN_EPS)
    return x * r, r


def _rms_bwd(dy, g, xhat, r):
    dyg = dy * g
    return r * (dyg - xhat * jnp.mean(dyg * xhat, axis=-1, keepdims=True))


def _lane_lt(shape, n):
    return lax.broadcasted_iota(jnp.int32, shape, 1) < n


def _mla_prep(proj, cosp, sinp, gq, gkv, name):
    s = proj.shape[0]
    tm = min(s, 512)

    def body(qc_ref, kv_ref, c_ref, s_ref, gq_ref, gkv_ref, qn_ref, kp_ref):
        qhat, _ = _rms(qc_ref[...], None)
        qn_ref[...] = (qhat * gq_ref[...]).astype(BF16)
        khat, _ = _rms(kv_ref[:, 0:128], None)
        kp_ref[:, 0:128] = (khat * gkv_ref[...]).astype(BF16)
        kr = kv_ref[:, 128:256]
        kp_ref[:, 128:256] = (kr * c_ref[...] + _rope_pair_swap(kr) * s_ref[...]).astype(BF16)

    return pl.pallas_call(
        body, name=name, grid=(s // tm,),
        in_specs=[pl.BlockSpec((tm, 256), lambda i: (i, 0)), pl.BlockSpec((tm, 256), lambda i: (i, 1)),
                  pl.BlockSpec((tm, 128), lambda i: (i, 0)), pl.BlockSpec((tm, 128), lambda i: (i, 0)),
                  pl.BlockSpec((1, 256), lambda i: (0, 0)), pl.BlockSpec((1, 128), lambda i: (0, 0))],
        out_specs=[pl.BlockSpec((tm, 256), lambda i: (i, 0)), pl.BlockSpec((tm, 256), lambda i: (i, 0))],
        out_shape=[jax.ShapeDtypeStruct((s, 256), BF16), jax.ShapeDtypeStruct((s, 256), BF16)],
        compiler_params=_params(1),
    )(proj, proj, cosp, sinp, gq, gkv)


def _mla_prep_bwd(proj, dqn, dkp, dv, cosp, sinp, gq, gkv, dproj, name):
    s = proj.shape[0]
    tm = min(s, 512)

    def body(qc_ref, kv_ref, dqn_ref, dkp_ref, dv_ref, c_ref, s_ref, gq_ref, gkv_ref, dproj_ref, o_ref, red_ref):
        @pl.when(pl.program_id(0) == 0)
        def _():
            red_ref[...] = jnp.zeros_like(red_ref)

        qhat, qr = _rms(qc_ref[...], None)
        dq = dqn_ref[...]
        o_ref[:, 0:256] = _rms_bwd(dq, gq_ref[...], qhat, qr).astype(BF16)
        red_ref[0:1, :] += _colsum(dq * qhat)
        khat, kr = _rms(kv_ref[:, 0:128], None)
        dk = dkp_ref[:, 0:128] + dv_ref[...]
        o_ref[:, 256:384] = _rms_bwd(dk, gkv_ref[...], khat, kr).astype(BF16)
        red_ref[1:2, 0:128] += _colsum(dk * khat)
        dr = dkp_ref[:, 128:256]
        o_ref[:, 384:512] = (dr * c_ref[...] - _rope_pair_swap(dr) * s_ref[...]).astype(BF16)

    return pl.pallas_call(
        body, name=name, grid=(s // tm,),
        in_specs=[pl.BlockSpec((tm, 256), lambda i: (i, 0)), pl.BlockSpec((tm, 256), lambda i: (i, 1)),
                  pl.BlockSpec((tm, 256), lambda i: (i, 0)), pl.BlockSpec((tm, 256), lambda i: (i, 0)),
                  pl.BlockSpec((tm, 128), lambda i: (i, 0)),
                  pl.BlockSpec((tm, 128), lambda i: (i, 0)), pl.BlockSpec((tm, 128), lambda i: (i, 0)),
                  pl.BlockSpec((1, 256), lambda i: (0, 0)), pl.BlockSpec((1, 128), lambda i: (0, 0)),
                  pl.BlockSpec(memory_space=pl.ANY)],
        out_specs=[pl.BlockSpec((tm, 512), lambda i: (i, 0)), pl.BlockSpec((8, 256), lambda i: (0, 0))],
        out_shape=[jax.ShapeDtypeStruct(dproj.shape, BF16), jax.ShapeDtypeStruct((8, 256), F32)],
        input_output_aliases={9: 0}, compiler_params=_params(1),
    )(proj, proj, dqn, dkp, dv, cosp, sinp, gq, gkv, dproj)


HEAD_PAIRS = MLA_HEADS // 2
PAIR_TM = 512


def _q_heads(q_up, cosp, sinp, wuk, name):
    s = q_up.shape[0]
    tm = min(s, PAIR_TM)

    def body(qn_ref, qr_ref, c_ref, s_ref, w_ref, o_ref):
        raw = qr_ref[...]
        rot = raw * c_ref[...] + _rope_pair_swap(raw) * s_ref[...]
        low = _lane_lt(rot.shape, 64)
        o_ref[0, :, 0:128] = _dot(qn_ref[:, 0:128], w_ref[:, 0:128], NT).astype(BF16)
        o_ref[0, :, 128:256] = jnp.where(low, rot, 0.0).astype(BF16)
        o_ref[1, :, 0:128] = _dot(qn_ref[:, 128:256], w_ref[:, 128:256], NT).astype(BF16)
        o_ref[1, :, 128:256] = jnp.where(low, pltpu.roll(rot, 64, 1), 0.0).astype(BF16)

    return pl.pallas_call(
        body, name=name, grid=(s // tm, HEAD_PAIRS),
        in_specs=[pl.BlockSpec((tm, 256), lambda i, p: (i, p)),
                  pl.BlockSpec((tm, 128), lambda i, p: (i, 16 + p)),
                  pl.BlockSpec((tm, 128), lambda i, p: (i, 0)), pl.BlockSpec((tm, 128), lambda i, p: (i, 0)),
                  pl.BlockSpec((128, 256), lambda i, p: (0, p))],
        out_specs=pl.BlockSpec((2, tm, 256), lambda i, p: (p, i, 0)),
        out_shape=jax.ShapeDtypeStruct((MLA_HEADS, s, 256), BF16), compiler_params=_params(2),
    )(q_up, q_up, cosp, sinp, wuk)


def _q_heads_bwd(dqp, q_up, cosp, sinp, wuk, name):
    s = q_up.shape[0]
    tm = min(s, PAIR_TM)

    def body(dq_ref, qn_ref, c_ref, s_ref, w_ref, dn_ref, dr_ref, dw_ref):
        @pl.when(pl.program_id(1) == 0)
        def _():
            dw_ref[...] = jnp.zeros_like(dw_ref)

        for a in range(2):
            dlat = dq_ref[a, :, 0:128]
            cols = slice(128 * a, 128 * a + 128)
            dn_ref[:, cols] = _dot(dlat, w_ref[:, cols], NN).astype(BF16)
            dw_ref[:, cols] += _dot(dlat, qn_ref[:, cols], TN)
        drot = dq_ref[0, :, 128:256].astype(F32) + pltpu.roll(dq_ref[1, :, 128:256].astype(F32), 64, 1)
        dr_ref[...] = (drot * c_ref[...] - _rope_pair_swap(drot) * s_ref[...]).astype(BF16)

    return pl.pallas_call(
        body, name=name, grid=(HEAD_PAIRS, s // tm),
        in_specs=[pl.BlockSpec((2, tm, 256), lambda p, i: (p, i, 0)),
                  pl.BlockSpec((tm, 256), lambda p, i: (i, p)),
                  pl.BlockSpec((tm, 128), lambda p, i: (i, 0)), pl.BlockSpec((tm, 128), lambda p, i: (i, 0)),
                  pl.BlockSpec((128, 256), lambda p, i: (0, p))],
        out_specs=[pl.BlockSpec((tm, 256), lambda p, i: (i, p)),
                   pl.BlockSpec((tm, 128), lambda p, i: (i, p)),
                   pl.BlockSpec((128, 256), lambda p, i: (0, p))],
        out_shape=[jax.ShapeDtypeStruct((s, MLA_WIDTH), BF16), jax.ShapeDtypeStruct((s, 1024), BF16),
                   jax.ShapeDtypeStruct((128, MLA_WIDTH), F32)],
        compiler_params=_params(2),
    )(dqp, q_up, cosp, sinp, wuk)


def _o_gate(o_lat, proj, wuv, name):
    s = o_lat.shape[1]
    tm = min(s, PAIR_TM)

    def body(ol_ref, z_ref, w_ref, g_ref):
        for a in range(2):
            cols = slice(128 * a, 128 * a + 128)
            z = z_ref[:, cols]
            g_ref[:, cols] = (_dot(ol_ref[a], w_ref[:, cols], NN) * (z * _sigmoid(z))).astype(BF16)

    return pl.pallas_call(
        body, name=name, grid=(s // tm, HEAD_PAIRS),
        in_specs=[pl.BlockSpec((2, tm, 128), lambda i, p: (p, i, 0)),
                  pl.BlockSpec((tm, 256), lambda i, p: (i, 2 + p)),
                  pl.BlockSpec((128, 256), lambda i, p: (0, p))],
        out_specs=pl.BlockSpec((tm, 256), lambda i, p: (i, p)),
        out_shape=jax.ShapeDtypeStruct((s, MLA_WIDTH), BF16), compiler_params=_params(2),
    )(o_lat, proj, wuv)


def _o_gate_bwd(dg, o_lat, proj, wuv, name):
    s = o_lat.shape[1]
    tm = min(s, PAIR_TM)

    def body(dg_ref, ol_ref, z_ref, w_ref, dz_ref, dol_ref, dw_ref):
        @pl.when(pl.program_id(1) == 0)
        def _():
            dw_ref[...] = jnp.zeros_like(dw_ref)

        for a in range(2):
            cols = slice(128 * a, 128 * a + 128)
            z, dgv, ol = z_ref[:, cols], dg_ref[:, cols], ol_ref[a]
            sig = _sigmoid(z)
            o = _dot(ol, w_ref[:, cols], NN)
            dz_ref[:, cols] = (dgv * o * (sig * (1.0 + z * (1.0 - sig)))).astype(BF16)
            do = (dgv * (z * sig)).astype(BF16)
            dol_ref[a] = _dot(do, w_ref[:, cols], NT).astype(BF16)
            dw_ref[:, cols] += _dot(ol, do, TN)

    return pl.pallas_call(
        body, name=name, grid=(HEAD_PAIRS, s // tm),
        in_specs=[pl.BlockSpec((tm, 256), lambda p, i: (i, p)),
                  pl.BlockSpec((2, tm, 128), lambda p, i: (p, i, 0)),
                  pl.BlockSpec((tm, 256), lambda p, i: (i, 2 + p)),
                  pl.BlockSpec((128, 256), lambda p, i: (0, p))],
        out_specs=[pl.BlockSpec((tm, 256), lambda p, i: (i, 2 + p)),
                   pl.BlockSpec((2, tm, 128), lambda p, i: (p, i, 0)),
                   pl.BlockSpec((128, 256), lambda p, i: (0, p))],
        out_shape=[jax.ShapeDtypeStruct((s, ODD_IN_PAD), BF16), jax.ShapeDtypeStruct((MLA_HEADS, s, 128), BF16),
                   jax.ShapeDtypeStruct((128, MLA_WIDTH), F32)],
        compiler_params=_params(2),
    )(dg, o_lat, proj, wuv)


ATT_TQ = CHUNK
ATT_ROWS = ATT_TQ * MLA_HEADS
ATT_TK = 512
ATT_HEAD_GROUP = 4


def _visible(k0, q_chunk, tk):
    kpos = k0 + lax.broadcasted_iota(jnp.int32, (1, tk), 1)
    return kpos // CHUNK <= q_chunk


def _tile_lanes(t, n):
    return jnp.concatenate([t] * (n // 128), axis=1)


def _key_blocks(i, tk, block, pairs=False):
    n_full = (i * ATT_TQ + ATT_TQ + tk - 1) // tk - 1
    if pairs:
        def two(jj, carry):
            block(2 * jj, False)
            block(2 * jj + 1, False)
            return carry

        lax.fori_loop(0, n_full // 2, two, 0)

        @pl.when(n_full % 2 == 1)
        def _():
            block(n_full - 1, False)
    else:
        def one(j, carry):
            block(j, False)
            return carry

        lax.fori_loop(0, n_full, one, 0)
    block(n_full, True)


def _attn_fwd(qp, kp, name):
    s = kp.shape[0]
    tk = min(ATT_TK, s)

    def body(q_ref, k_ref, o_ref, lse_ref, m_sc, acc_sc):
        i = pl.program_id(0)
        m_sc[...] = jnp.full_like(m_sc, -jnp.inf)
        acc_sc[...] = jnp.zeros_like(acc_sc)

        def block(j, masked):
            k0 = pl.multiple_of(j * tk, tk)
            k = k_ref[pl.ds(k0, tk), :]
            v1 = jnp.where(_lane_lt(k.shape, 128), k, jnp.ones_like(k))
            for h0 in range(0, MLA_HEADS, ATT_HEAD_GROUP):
                rows = slice(h0 * ATT_TQ, (h0 + ATT_HEAD_GROUP) * ATT_TQ)
                q = q_ref[h0:h0 + ATT_HEAD_GROUP].reshape(ATT_HEAD_GROUP * ATT_TQ, 256)
                sc = _dot(q, k, NT) * ATTN_SCALE_LOG2
                if masked:
                    sc = jnp.where(_visible(k0, i, tk), sc, -jnp.inf)
                m_prev = m_sc[rows]
                m_new = jnp.maximum(m_prev, jnp.max(sc, axis=1, keepdims=True))
                p = jnp.exp2(sc - _tile_lanes(m_new, tk))
                acc_sc[rows] = _tile_lanes(jnp.exp2(m_prev - m_new), 256) * acc_sc[rows] + _dot(p, v1, NN)
                m_sc[rows] = m_new

        _key_blocks(i, tk, block, pairs=True)
        acc = acc_sc[...]
        l = acc[:, 128:256]
        o_ref[...] = (acc[:, 0:128] / l).astype(BF16).reshape(MLA_HEADS, ATT_TQ, 128)
        lse_ref[...] = (m_sc[...] + jnp.log2(l)).reshape(MLA_HEADS, ATT_TQ, 128)

    head128 = pl.BlockSpec((MLA_HEADS, ATT_TQ, 128), lambda i: (0, i, 0))
    return pl.pallas_call(
        body, name=name, grid=(s // ATT_TQ,),
        in_specs=[pl.BlockSpec((MLA_HEADS, ATT_TQ, 256), lambda i: (0, i, 0)), pl.BlockSpec((s, 256), lambda i: (0, 0))],
        out_specs=[head128, head128],
        out_shape=[jax.ShapeDtypeStruct((MLA_HEADS, s, 128), BF16), jax.ShapeDtypeStruct((MLA_HEADS, s, 128), F32)],
        scratch_shapes=[pltpu.VMEM((ATT_ROWS, 128), F32), pltpu.VMEM((ATT_ROWS, 256), F32)],
        compiler_params=_params(1),
    )(qp, kp)


def _attn_bwd(qp, kp, o, do, lse, name):
    s = kp.shape[0]
    tk = min(ATT_TK, s)

    def body(q_ref, k_ref, o_ref, do_ref, lse_ref, dq_ref, dk_ref, dv_ref, dq_sc):
        i = pl.program_id(0)

        @pl.when(i == 0)
        def _():
            dk_ref[...] = jnp.zeros_like(dk_ref)
            dv_ref[...] = jnp.zeros_like(dv_ref)

        q = q_ref[...].reshape(ATT_ROWS, 256)
        dov = do_ref[...].reshape(ATT_ROWS, 128)
        delta = jnp.sum(dov.astype(F32) * o_ref[...].reshape(ATT_ROWS, 128).astype(F32), axis=1, keepdims=True)
        delta_t = _tile_lanes(jnp.broadcast_to(delta, (ATT_ROWS, 128)), tk)
        lse_t = _tile_lanes(lse_ref[...].reshape(ATT_ROWS, 128), tk)
        dq_sc[...] = jnp.zeros_like(dq_sc)

        def block(j, masked):
            k0 = pl.multiple_of(j * tk, tk)
            k = k_ref[pl.ds(k0, tk), :]
            p = jnp.exp2(_dot(q, k, NT) * ATTN_SCALE_LOG2 - lse_t)
            if masked:
                p = jnp.where(_visible(k0, i, tk), p, 0.0)
            dv_ref[pl.ds(k0, tk), :] += _dot(p, dov, TN)
            ds = (p * (_dot(dov, k[:, 0:128], NT) - delta_t) * ATTN_SCALE).astype(BF16)
            dq_sc[...] += _dot(ds, k, NN)
            dk_ref[pl.ds(k0, tk), :] += _dot(ds, q, TN)

        _key_blocks(i, tk, block)
        dq_ref[...] = dq_sc[...].astype(BF16).reshape(MLA_HEADS, ATT_TQ, 256)

    head128 = pl.BlockSpec((MLA_HEADS, ATT_TQ, 128), lambda i: (0, i, 0))
    head256 = pl.BlockSpec((MLA_HEADS, ATT_TQ, 256), lambda i: (0, i, 0))
    return pl.pallas_call(
        body, name=name, grid=(s // ATT_TQ,),
        in_specs=[head256, pl.BlockSpec((s, 256), lambda i: (0, 0)), head128, head128, head128],
        out_specs=[head256, pl.BlockSpec((s, 256), lambda i: (0, 0)), pl.BlockSpec((s, 128), lambda i: (0, 0))],
        out_shape=[jax.ShapeDtypeStruct((MLA_HEADS, s, 256), BF16),
                   jax.ShapeDtypeStruct((s, 256), F32), jax.ShapeDtypeStruct((s, 128), F32)],
        scratch_shapes=[pltpu.VMEM((ATT_ROWS, 256), F32)],
        compiler_params=_params(1),
    )(qp, kp, o, do, lse)


def _place():
    x, y, c = lax.axis_index("x"), lax.axis_index("y"), lax.axis_index("c")
    return x, y, c, 4 * x + 2 * y + c


def _flip(x, y, c, r):
    px = 1 - x if r & 4 else x
    py = 1 - y if r & 2 else y
    pc = 1 - c if r & 1 else c
    return (px, py, pc), 4 * px + 2 * py + pc


def _adaln_exchange(c8, ada_w, ada_b_cols, name):
    d = c8.shape[1]
    w_cols = ada_w.shape[2]

    def body(c_ref, w_ref, b_ref, call_ref, mod_ref, sbuf, rbuf, s1, r1, s2, r2):
        x, y, c, me = _place()
        call_ref[pl.ds(pl.multiple_of(me * 8, 8), 8), :] = c_ref[...]
        peers = [_flip(x, y, c, r) for r in range(1, N_DEV)]

        def c_copy(k, src_lin, to):
            rows = call_ref.at[pl.ds(pl.multiple_of(src_lin * 8, 8), 8), :]
            return pltpu.make_async_remote_copy(src_ref=rows, dst_ref=rows, send_sem=s1.at[k], recv_sem=r1.at[k],
                                                device_id=to, device_id_type=MESH)

        first = [c_copy(k, me, peer) for k, (peer, _) in enumerate(peers)]
        for cp in first:
            cp.start()
        for k, (_, lin) in enumerate(peers):
            c_copy(k, lin, (x, y, c)).wait_recv()
        for cp in first:
            cp.wait_send()

        for j in range(N_DEV):
            cj = call_ref[8 * j:8 * j + 8, :]
            cond = cj * _sigmoid(cj)
            for l in range(2):
                sbuf[j, l] = lax.dot_general(cond, w_ref[l], NN, precision=lax.Precision.HIGHEST,
                                             preferred_element_type=F32) + b_ref[l]

        def m_copy(k, src_slot, dst_slot, to):
            return pltpu.make_async_remote_copy(src_ref=sbuf.at[src_slot], dst_ref=rbuf.at[dst_slot],
                                                send_sem=s2.at[k], recv_sem=r2.at[k], device_id=to,
                                                device_id_type=MESH)

        rbuf[me] = sbuf[me]
        second = [m_copy(k, lin, me, peer) for k, (peer, lin) in enumerate(peers)]
        for cp in second:
            cp.start()
        for k, (_, lin) in enumerate(peers):
            m_copy(k, lin, lin, (x, y, c)).wait_recv()
        for cp in second:
            cp.wait_send()
        for j in range(N_DEV):
            for l in range(2):
                mod_ref[l, :, w_cols * j:w_cols * (j + 1)] = rbuf[j, l]

    vmem = pl.BlockSpec(memory_space=pltpu.VMEM)
    return pl.pallas_call(
        body, name=name, in_specs=[vmem, vmem, vmem], out_specs=[vmem, vmem],
        out_shape=[jax.ShapeDtypeStruct((8 * N_DEV, d), F32), jax.ShapeDtypeStruct((2, 8, 3 * d), F32)],
        scratch_shapes=[pltpu.VMEM((N_DEV, 2, 8, w_cols), F32), pltpu.VMEM((N_DEV, 2, 8, w_cols), F32),
                        pltpu.SemaphoreType.DMA((N_DEV - 1,)), pltpu.SemaphoreType.DMA((N_DEV - 1,)),
                        pltpu.SemaphoreType.DMA((N_DEV - 1,)), pltpu.SemaphoreType.DMA((N_DEV - 1,))],
        compiler_params=pltpu.CompilerParams(vmem_limit_bytes=VMEM_LIMIT),
    )(c8, ada_w, ada_b_cols)


def _all_gather(blocks, name):
    n_arr = len(blocks)

    def body(*refs):
        x_refs, out_refs = refs[:n_arr], refs[n_arr:2 * n_arr]
        send_sems, recv_sems, local_sems = refs[2 * n_arr:]
        x, y, c, _ = _place()
        me, sibling = (x, y, c), (x, y, 1 - c)
        chips = [(1 - x, y), (x, 1 - y), (1 - x, 1 - y)]

        def copy(t, k, blk, to, src=None):
            slot = out_refs[t].at[4 * blk[0] + 2 * blk[1] + blk[2]]
            return pltpu.make_async_remote_copy(src_ref=slot if src is None else src, dst_ref=slot,
                                                send_sem=send_sems.at[7 * t + k], recv_sem=recv_sems.at[7 * t + k],
                                                device_id=to, device_id_type=MESH)

        mine = [pltpu.make_async_copy(x_refs[t], out_refs[t].at[4 * x + 2 * y + c], local_sems.at[t])
                for t in range(n_arr)]
        for cp in mine:
            cp.start()
        first = []
        for t in range(n_arr):
            first.append(copy(t, 0, me, sibling, src=x_refs[t]))
            first += [copy(t, 1 + j, me, (*chip, c), src=x_refs[t]) for j, chip in enumerate(chips)]
        for cp in first:
            cp.start()
        passed = []
        for t in range(n_arr):
            for j, chip in enumerate(chips):
                copy(t, 1 + j, (*chip, c), me).wait_recv()
                passed.append(copy(t, 4 + j, (*chip, c), sibling))
                passed[-1].start()
        for t in range(n_arr):
            copy(t, 0, sibling, me).wait_recv()
            for j, chip in enumerate(chips):
                copy(t, 4 + j, (*chip, 1 - c), me).wait_recv()
        for cp in first + passed:
            cp.wait_send()
        for cp in mine:
            cp.wait()

    anyspace = pl.BlockSpec(memory_space=pl.ANY)
    return pl.pallas_call(
        body, name=name, in_specs=[anyspace] * n_arr, out_specs=[anyspace] * n_arr,
        out_shape=[jax.ShapeDtypeStruct((N_DEV,) + b.shape, b.dtype) for b in blocks],
        scratch_shapes=[pltpu.SemaphoreType.DMA((7 * n_arr,)), pltpu.SemaphoreType.DMA((7 * n_arr,)),
                        pltpu.SemaphoreType.DMA((n_arr,))],
    )(*blocks)


def _scatter_parts(parts, name):
    n_arr = len(parts)

    def body(*refs):
        g_refs, r_refs = refs[:n_arr], refs[n_arr:2 * n_arr]
        send_sems, recv_sems, local_sems = refs[2 * n_arr:]
        x, y, c, me = _place()
        own = [pltpu.make_async_copy(g_refs[t].at[me], r_refs[t].at[me], local_sems.at[t]) for t in range(n_arr)]
        for cp in own:
            cp.start()
        sends, recvs = [], []
        for r in range(1, N_DEV):
            peer, lin = _flip(x, y, c, r)
            for t in range(n_arr):
                k = n_arr * (r - 1) + t
                sends.append(pltpu.make_async_remote_copy(
                    src_ref=g_refs[t].at[lin], dst_ref=r_refs[t].at[me], send_sem=send_sems.at[k],
                    recv_sem=recv_sems.at[k], device_id=peer, device_id_type=MESH))
                recvs.append(pltpu.make_async_remote_copy(
                    src_ref=g_refs[t].at[lin], dst_ref=r_refs[t].at[lin], send_sem=send_sems.at[k],
                    recv_sem=recv_sems.at[k], device_id=(x, y, c), device_id_type=MESH))
        for cp in sends:
            cp.start()
        for cp in recvs:
            cp.wait_recv()
        for cp in sends:
            cp.wait_send()
        for cp in own:
            cp.wait()

    anyspace = pl.BlockSpec(memory_space=pl.ANY)
    n = n_arr * (N_DEV - 1)
    return pl.pallas_call(
        body, name=name, in_specs=[anyspace] * n_arr, out_specs=[anyspace] * n_arr,
        out_shape=[jax.ShapeDtypeStruct(p.shape, p.dtype) for p in parts],
        scratch_shapes=[pltpu.SemaphoreType.DMA((n,)), pltpu.SemaphoreType.DMA((n,)),
                        pltpu.SemaphoreType.DMA((n_arr,))],
    )(*parts)


def _adamw(w, g, m, v):
    m = ADAM_B1 * m + (1.0 - ADAM_B1) * g
    v = ADAM_B2 * v + (1.0 - ADAM_B2) * (g * g)
    m_hat = m / (1.0 - ADAM_B1 ** ADAM_STEP)
    v_hat = v / (1.0 - ADAM_B2 ** ADAM_STEP)
    return -ADAM_LR * (m_hat / (jnp.sqrt(v_hat) + ADAM_EPS) + ADAM_WD * w), m, v


def _sum_parts_adamw(parts, w, m, v, name):
    _, rows, cols = parts.shape
    tr = max(t for t in range(16, 129, 16) if rows % t == 0)

    def body(p_ref, w_ref, m_ref, v_ref, g_ref, d_ref, mo_ref, vo_ref):
        g = p_ref[0].astype(F32)
        for j in range(1, N_DEV):
            g = g + p_ref[j].astype(F32)
        g_ref[...] = g
        d_ref[...], mo_ref[...], vo_ref[...] = _adamw(w_ref[...], g, m_ref[...], v_ref[...])

    row = _rows3(tr, cols)
    out = jax.ShapeDtypeStruct((1, rows, cols), F32)
    return pl.pallas_call(
        body, name=name, grid=(rows // tr,),
        in_specs=[pl.BlockSpec((N_DEV, tr, cols), lambda i: (0, i, 0)), row, row, row],
        out_specs=[row, row, row, row], out_shape=[out, out, out, out], compiler_params=_params(1),
    )(parts, w, m, v)


def _adamw_call(g, w, m, v, name):
    rows, cols = g.shape
    tr = 128 if rows % 128 == 0 else rows

    def body(g_ref, w_ref, m_ref, v_ref, d_ref, mo_ref, vo_ref):
        d_ref[...], mo_ref[...], vo_ref[...] = _adamw(w_ref[...], g_ref[...], m_ref[...], v_ref[...])

    row = pl.BlockSpec((tr, cols), lambda i: (i, 0))
    out = jax.ShapeDtypeStruct((rows, cols), F32)
    return pl.pallas_call(
        body, name=name, grid=(rows // tr,), in_specs=[row] * 4, out_specs=[row] * 3, out_shape=[out] * 3,
        compiler_params=_params(1),
    )(g, w, m, v)


def _ada_w_grad_adamw(c_all, dmod_rows, w, m, v, name):
    def body(c_ref, dm_ref, w_ref, m_ref, v_ref, g_ref, d_ref, mo_ref, vo_ref):
        cv = c_ref[...]
        cond = cv * _sigmoid(cv)
        for l in range(2):
            g = lax.dot_general(cond, dm_ref[l], TN, precision=lax.Precision.HIGHEST, preferred_element_type=F32)
            g_ref[l] = g
            d_ref[l], mo_ref[l], vo_ref[l] = _adamw(w_ref[l], g, m_ref[l], v_ref[l])

    out = jax.ShapeDtypeStruct(w.shape, F32)
    return pl.pallas_call(
        body, name=name, out_shape=[out] * 4, compiler_params=pltpu.CompilerParams(vmem_limit_bytes=VMEM_LIMIT),
    )(c_all, dmod_rows, w, m, v)


REPLICATED = ("ln_g", "ln_b", "gmlp_norm_g", "gmlp_norm_b", "gmlp_ws", "gmlp_bs", "pool_b", "pool_scale",
              "mla_kv_norm_g", "mla_w_uk", "mla_w_uv")
CHUNK_ROWS, ADA_ROW, QNORM_ROW, REP_ROWS = 73, 73, 74, 80
UQ_ROWS, POOLW_ROWS = 96, 32
SMALL_ROWS = REP_ROWS + UQ_ROWS + POOLW_ROWS


def _pad_rows(flat2d, rows):
    n, k = flat2d.shape
    return jnp.pad(flat2d, ((0, 0), (0, rows * LANES - k))).reshape(n, rows, LANES)


def _ada_cols_rows(vec):
    return _pad_rows(vec.reshape(2, N_DEV, -1).transpose(1, 0, 2).reshape(N_DEV, -1), 1)


def _pack_replicated(src, ada_vec):
    flat = jnp.concatenate([src[n].reshape(-1) for n in REPLICATED])
    body = _pad_rows(flat.reshape(N_DEV, -1), CHUNK_ROWS)
    return jnp.concatenate([body, jnp.pad(_ada_cols_rows(ada_vec), ((0, 0), (0, REP_ROWS - CHUNK_ROWS - 1), (0, 0)))],
                           axis=1)


def _unpack_replicated(rep, shapes):
    chunk = sum(s[1] for s in shapes) // N_DEV
    flat, off, out = rep[:, :CHUNK_ROWS].reshape(N_DEV, -1)[:, :chunk].reshape(-1), 0, {}
    for n, size, shape in shapes:
        out[n] = flat[off:off + size].reshape(shape)
        off += size
    cols = 3 * D_MODEL // N_DEV
    out["ada_b"] = rep[:, ADA_ROW, :2 * cols].reshape(N_DEV, 2, cols).transpose(1, 0, 2).reshape(2, -1)
    return out


def kernel(x, c, positions, ada_w, ada_b, ln_g, ln_b, e_w_in, gmlp_norm_g, gmlp_norm_b, gmlp_ws, gmlp_bs, pool_w, pool_b, pool_scale, e_w_out, o_w_in, mla_q_norm_g, mla_kv_norm_g, mla_w_uq, mla_w_uk, mla_w_uv, o_w_out, loss_target, m_ada_w, m_ada_b, m_ln_g, m_ln_b, m_e_w_in, m_gmlp_norm_g, m_gmlp_norm_b, m_gmlp_ws, m_gmlp_bs, m_pool_w, m_pool_b, m_pool_scale, m_e_w_out, m_o_w_in, m_mla_q_norm_g, m_mla_kv_norm_g, m_mla_w_uq, m_mla_w_uk, m_mla_w_uv, m_o_w_out, v_ada_w, v_ada_b, v_ln_g, v_ln_b, v_e_w_in, v_gmlp_norm_g, v_gmlp_norm_b, v_gmlp_ws, v_gmlp_bs, v_pool_w, v_pool_b, v_pool_scale, v_e_w_out, v_o_w_in, v_mla_q_norm_g, v_mla_kv_norm_g, v_mla_w_uq, v_mla_w_uk, v_mla_w_uv, v_o_w_out):
    w_in = dict(ada_w=ada_w, ada_b=ada_b, ln_g=ln_g, ln_b=ln_b, e_w_in=e_w_in, gmlp_norm_g=gmlp_norm_g,
                gmlp_norm_b=gmlp_norm_b, gmlp_ws=gmlp_ws, gmlp_bs=gmlp_bs, pool_w=pool_w, pool_b=pool_b,
                pool_scale=pool_scale, e_w_out=e_w_out, o_w_in=o_w_in, mla_q_norm_g=mla_q_norm_g,
                mla_kv_norm_g=mla_kv_norm_g, mla_w_uq=mla_w_uq, mla_w_uk=mla_w_uk, mla_w_uv=mla_w_uv, o_w_out=o_w_out)
    m_in = dict(ada_w=m_ada_w, ada_b=m_ada_b, ln_g=m_ln_g, ln_b=m_ln_b, e_w_in=m_e_w_in, gmlp_norm_g=m_gmlp_norm_g,
                gmlp_norm_b=m_gmlp_norm_b, gmlp_ws=m_gmlp_ws, gmlp_bs=m_gmlp_bs, pool_w=m_pool_w, pool_b=m_pool_b,
                pool_scale=m_pool_scale, e_w_out=m_e_w_out, o_w_in=m_o_w_in, mla_q_norm_g=m_mla_q_norm_g,
                mla_kv_norm_g=m_mla_kv_norm_g, mla_w_uq=m_mla_w_uq, mla_w_uk=m_mla_w_uk, mla_w_uv=m_mla_w_uv,
                o_w_out=m_o_w_out)
    v_in = dict(ada_w=v_ada_w, ada_b=v_ada_b, ln_g=v_ln_g, ln_b=v_ln_b, e_w_in=v_e_w_in, gmlp_norm_g=v_gmlp_norm_g,
                gmlp_norm_b=v_gmlp_norm_b, gmlp_ws=v_gmlp_ws, gmlp_bs=v_gmlp_bs, pool_w=v_pool_w, pool_b=v_pool_b,
                pool_scale=v_pool_scale, e_w_out=v_e_w_out, o_w_in=v_o_w_in, mla_q_norm_g=v_mla_q_norm_g,
                mla_kv_norm_g=v_mla_kv_norm_g, mla_w_uq=v_mla_w_uq, mla_w_uk=v_mla_w_uk, mla_w_uv=v_mla_w_uv,
                o_w_out=v_o_w_out)
    names = list(w_in)
    seq = x.shape[1]
    d = D_MODEL
    me = 4 * lax.axis_index("x") + 2 * lax.axis_index("y") + lax.axis_index("c")
    ada_cols = ada_w.shape[2]

    ada_b_cols = lax.dynamic_slice_in_dim(ada_b, me * ada_cols, ada_cols, axis=1)
    slab_row = lax.broadcasted_iota(jnp.int32, (8, d), 0)
    slab = jnp.where(slab_row == 0, c, jnp.where(slab_row == 1, jnp.pad(mla_q_norm_g, ((0, 0), (0, d - 32))), 0.0))
    c_all, mod = _adaln_exchange(slab, ada_w,
                                 jnp.broadcast_to(ada_b_cols[:, None, :], (2, 8, ada_cols)), "adaln_exchange")

    uq_len = mla_w_uq.size
    small_b = jnp.concatenate([mla_w_uq.reshape(-1), pool_w.reshape(-1)]).astype(BF16).reshape(-1, LANES)
    w_in_e3, o_in3, w_out_e3, w_out_o3, small3 = _all_gather(
        [e_w_in[0].astype(BF16), o_w_in[0].astype(BF16), e_w_out[0].astype(BF16), o_w_out[0].astype(BF16), small_b],
        "weight_gather")
    w_out_e, w_out_o = w_out_e3.reshape(-1, d), w_out_o3.reshape(-1, d)
    o_in_full = o_in3.transpose(1, 0, 2).reshape(d, ODD_IN)
    w_in_o = jnp.concatenate([o_in_full[:, :448], jnp.zeros((d, 64), BF16), o_in_full[:, 448:]], axis=1)
    small_flat = small3.reshape(N_DEV, -1)
    uq_full = small_flat[:, :uq_len].reshape(MLA_Q_RANK, MLA_HEADS, MLA_NOPE + MLA_ROPE)
    pool_w_full = small_flat[:, uq_len:].reshape(N_DEV, 4, 32, 256).transpose(1, 0, 2, 3).reshape(4, 256, 256)
    w_uq_n = uq_full[:, :, :MLA_NOPE].reshape(MLA_Q_RANK, -1)
    w_uq_r = uq_full[:, :, MLA_NOPE:].reshape(MLA_Q_RANK, -1)
    w_uq = jnp.concatenate([w_uq_n, w_uq_r], axis=1)
    g_q = c_all.reshape(N_DEV, 8, d)[:, 1, :32].reshape(1, MLA_Q_RANK)

    ws, bs_col = gmlp_ws[0], gmlp_bs[0].reshape(GMLP_HEADS, GMLP_BLOCK, 1)
    wuk2, wuv2 = mla_w_uk[0].reshape(MLA_KV_RANK, -1), mla_w_uv[0].reshape(MLA_KV_RANK, -1)
    inv = 1.0 / (ROPE_THETA ** (jnp.arange(0, MLA_ROPE, 2, dtype=F32) / MLA_ROPE))
    ang = positions[0].astype(F32)[:, None] * inv
    cosp = jnp.tile(jnp.cos(ang), (1, 4))
    sinp = jnp.tile(jnp.concatenate([-jnp.sin(ang), jnp.sin(ang)], axis=1), (1, 2))

    h0 = _modulate(x, mod[0], "modulate0")
    proj0 = _matmul_cols_nn(h0, w_in_e3, F32, 512, "even_in")
    mix0 = _even_fwd(proj0, ws, bs_col, gmlp_norm_g, gmlp_norm_b, pool_w_full, pool_b, pool_scale, "even_mix")
    y0 = _matmul([(mix0, w_out_e)], "nn", F32, seq, d, 512, 1024, "even_out")
    x1 = _resid_ln(x, y0, mod[0], ln_g[0:1], ln_b[0:1], "resid_ln0")

    h1 = _modulate(x1, mod[1], "modulate1")
    proj1 = _matmul([(h1, w_in_o)], "nn", F32, seq, ODD_IN_PAD, 512, 512, "odd_in")
    qn, kp = _mla_prep(proj1, cosp, sinp, g_q, mla_kv_norm_g, "mla_prep")
    q_up = _matmul([(qn, w_uq)], "nn", F32, seq, 3072, 512, 1024, "q_up")
    qp = _q_heads(q_up, cosp, sinp, wuk2, "q_heads")
    o_lat, lse = _attn_fwd(qp, kp, "attn_fwd")
    gated = _o_gate(o_lat, proj1, wuv2, "o_gate")
    y1 = _matmul([(gated, w_out_o)], "nn", F32, seq, d, 512, 1024, "odd_out")

    dy1, dxres1, red2 = _final_ln_loss_bwd(x1, y1, mod[1], ln_g[1:2], ln_b[1:2], loss_target, "final_ln_loss")
    dgated = _matmul([(dy1, w_out_o)], "nt", F32, seq, MLA_WIDTH, 512, 512, "odd_out_dx")
    g_w_out_o = _matmul([(gated, dy1)], "tn", BF16, MLA_WIDTH, d, 256, 512, "odd_out_dw")
    dproj1_z, do_lat, g_wuv = _o_gate_bwd(dgated, o_lat, proj1, wuv2, "o_gate_bwd")
    dqp, dkp, dvv = _attn_bwd(qp, kp, o_lat, do_lat, lse, "attn_bwd")
    dq_nope, dq_rope, g_wuk = _q_heads_bwd(dqp, q_up, cosp, sinp, wuk2, "q_heads_bwd")
    dqn = _matmul([(dq_nope, w_uq_n), (dq_rope, w_uq_r)], "nt", F32, seq, MLA_Q_RANK, 512, 256, "q_up_dx")
    g_wuq_n = _matmul([(qn, dq_nope)], "tn", F32, MLA_Q_RANK, MLA_WIDTH, 256, 512, "q_up_dw_nope")
    g_wuq_r = _matmul([(qn, dq_rope)], "tn", F32, MLA_Q_RANK, 1024, 256, 512, "q_up_dw_rope")
    dproj1, red_mla = _mla_prep_bwd(proj1, dqn, dkp, dvv, cosp, sinp, g_q, mla_kv_norm_g, dproj1_z, "mla_prep_bwd")
    dh1 = _matmul([(dproj1, w_in_o)], "nt", F32, seq, d, 512, 512, "odd_in_dx")
    g_w_in_o = _matmul([(h1, dproj1)], "tn", BF16, d, ODD_IN_PAD, 256, 512, "odd_in_dw")
    dy0, dxres0, red1 = _mid_bwd(dh1, dxres1, x, y0, mod[0], mod[1], ln_g[0:1], ln_b[0:1], "mid_bwd")
    dmix = _matmul([(dy0, w_out_e)], "nt", F32, seq, 2048, 512, 512, "even_out_dx")
    g_w_out_e = _matmul([(mix0, dy0)], "tn", BF16, 2048, d, 256, 512, "even_out_dw")
    dproj0, g_ws, g_bs, g_ng, g_nb, g_pw, g_pb, g_ps = _even_bwd(
        proj0, dmix, ws, bs_col, gmlp_norm_g, gmlp_norm_b, pool_w_full, pool_b, pool_scale, "even_mix_bwd")
    dh0 = _matmul_cols_nt(dproj0, w_in_e3, F32, 512, "even_in_dx")
    part_e_in = _matmul_cols_tn(h0, dproj0, w_in_e3.shape[2], BF16, 256, "even_in_dw")
    grad_x, red0 = _first_bwd(dh0, dxres0, x, mod[0], "first_bwd")

    loss = lax.psum(0.5 / d * jnp.sum(red2[3]), ("x", "y", "c"))

    t_mask = lax.broadcasted_iota(jnp.int32, (GMLP_BLOCK, GMLP_BLOCK), 0) // CHUNK
    s_mask = lax.broadcasted_iota(jnp.int32, (GMLP_BLOCK, GMLP_BLOCK), 1) // CHUNK
    part = {
        "ln_g": jnp.stack([red1[2], red2[0]]), "ln_b": jnp.stack([red1[3], red2[1]]),
        "gmlp_norm_g": g_ng, "gmlp_norm_b": g_nb,
        "gmlp_ws": jnp.where(s_mask <= t_mask, g_ws, 0.0), "gmlp_bs": g_bs,
        "pool_b": g_pb, "pool_scale": g_ps, "mla_kv_norm_g": red_mla[1, :MLA_KV_RANK],
        "mla_w_uk": g_wuk, "mla_w_uv": g_wuv,
    }
    dmod = jnp.stack([jnp.concatenate([red0[1], red0[0], red1[4]]),
                      jnp.concatenate([red1[1], red1[0], red2[2]])])

    part_o_in = jnp.concatenate([g_w_in_o[:, :448], g_w_in_o[:, 512:]], axis=1).reshape(d, N_DEV, -1).transpose(1, 0, 2)
    g_uq = jnp.concatenate([g_wuq_n.reshape(MLA_Q_RANK, MLA_HEADS, MLA_NOPE),
                            g_wuq_r.reshape(MLA_Q_RANK, MLA_HEADS, MLA_ROPE)], axis=2)
    part_small = jnp.concatenate([
        _pad_rows(jnp.concatenate([part[n].reshape(-1) for n in REPLICATED]).reshape(N_DEV, -1), CHUNK_ROWS),
        jnp.pad(jnp.concatenate([_ada_cols_rows(dmod), _pad_rows(red_mla[0].reshape(N_DEV, -1), 1)], axis=1),
                ((0, 0), (0, REP_ROWS - QNORM_ROW - 1), (0, 0))),
        g_uq.reshape(N_DEV, UQ_ROWS, LANES),
        g_pw.reshape(4, N_DEV, 32, 256).transpose(1, 0, 2, 3).reshape(N_DEV, POOLW_ROWS, LANES)], axis=1)
    r_e_in, r_o_in, r_e_out, r_o_out, r_small = _scatter_parts(
        [part_e_in, part_o_in, g_w_out_e.reshape(N_DEV, -1, d), g_w_out_o.reshape(N_DEV, -1, d), part_small],
        "grad_scatter")

    def small_local(src):
        return jnp.concatenate([jnp.pad(src["mla_q_norm_g"], ((QNORM_ROW, REP_ROWS - QNORM_ROW - 1), (0, LANES - 32))),
                                src["mla_w_uq"].reshape(UQ_ROWS, LANES), src["pool_w"].reshape(POOLW_ROWS, LANES)])[None]

    res = {"e_w_in": _sum_parts_adamw(r_e_in, e_w_in, m_e_w_in, v_e_w_in, "adamw_e_w_in"),
           "o_w_in": _sum_parts_adamw(r_o_in, o_w_in, m_o_w_in, v_o_w_in, "adamw_o_w_in"),
           "e_w_out": _sum_parts_adamw(r_e_out, e_w_out, m_e_w_out, v_e_w_out, "adamw_e_w_out"),
           "o_w_out": _sum_parts_adamw(r_o_out, o_w_out, m_o_w_out, v_o_w_out, "adamw_o_w_out")}
    small = _sum_parts_adamw(r_small, small_local(w_in), small_local(m_in), small_local(v_in), "adamw_small")
    for n, r0, r1 in (("mla_w_uq", REP_ROWS, REP_ROWS + UQ_ROWS), ("pool_w", REP_ROWS + UQ_ROWS, SMALL_ROWS)):
        res[n] = [t[0, r0:r1].reshape(w_in[n].shape) for t in small]
    res["mla_q_norm_g"] = [t[0, QNORM_ROW:QNORM_ROW + 1, :32] for t in small]
    (rep_sum,) = _all_gather([small[0][0, :REP_ROWS]], "replicated_gather")
    rep_w, rep_m, rep_v = (_pack_replicated(src, src["ada_b"]).reshape(-1, LANES) for src in (w_in, m_in, v_in))
    rep_res = (rep_sum,) + tuple(t.reshape(N_DEV, REP_ROWS, LANES)
                                 for t in _adamw_call(rep_sum.reshape(-1, LANES), rep_w, rep_m, rep_v, "replicated_adamw"))
    shapes = [(n, w_in[n].size, w_in[n].shape) for n in REPLICATED]
    for k, t in enumerate(rep_res):
        for n, val in _unpack_replicated(t, shapes).items():
            res.setdefault(n, [None] * 4)[k] = val
    dmod_all = r_small[:, ADA_ROW, :2 * ada_cols].reshape(N_DEV, 2, ada_cols).transpose(1, 0, 2)
    dmod_rows = jnp.pad(dmod_all[:, :, None, :], ((0, 0), (0, 0), (0, 7), (0, 0))).reshape(2, 8 * N_DEV, ada_cols)
    res["ada_w"] = _ada_w_grad_adamw(c_all, dmod_rows, ada_w, m_ada_w, v_ada_w, "ada_w_adamw")

    return (loss, grad_x, *[res[n][0] for n in names], *[res[n][1] for n in names],
            *[res[n][2] for n in names], *[res[n][3] for n in names])
```

```python
import functools

import jax
import jax.numpy as jnp
from jax import lax
from jax.experimental import pallas as pl
from jax.experimental.pallas import tpu as pltpu

F32 = jnp.float32
BF16 = jnp.bfloat16

D_MODEL = 1024
CHUNK = 64
LN_EPS = 1e-5
GMLP_HEADS = 4
GMLP_HEAD_DIM = 256
GMLP_BLOCK = 128
POOL_WINDOWS = (2, 4, 8, 16)
POOL_GROUP_DIM = 256
POOL_HALO = 16
MLA_HEADS = 16
MLA_NOPE = 128
MLA_ROPE = 64
MLA_Q_RANK = 256
MLA_KV_RANK = 128
MLA_WIDTH = 2048
ODD_IN = 2496
ODD_IN_PAD = 2560
ROPE_THETA = 10000.0
ATTN_SCALE = (MLA_NOPE + MLA_ROPE) ** -0.5
ATTN_SCALE_LOG2 = ATTN_SCALE * 1.4426950408889634
DEEPNORM_ALPHA = 4.0 ** 0.25
ADAM_LR, ADAM_B1, ADAM_B2, ADAM_EPS, ADAM_WD, ADAM_STEP = 0.001, 0.9, 0.999, 1e-8, 0.01, 10
N_DEV = 8
LANES = 1024
VMEM_LIMIT = 56 * 1024 * 1024
MESH = pl.DeviceIdType.MESH

NT = (((1,), (1,)), ((), ()))
NN = (((1,), (0,)), ((), ()))
TN = (((0,), (0,)), ((), ()))


def _params(n_axes):
    return pltpu.CompilerParams(dimension_semantics=("arbitrary",) * n_axes, vmem_limit_bytes=VMEM_LIMIT)


def _dot(a, b, dn):
    return lax.dot_general(a.astype(BF16), b.astype(BF16), dn, preferred_element_type=F32)


def _sigmoid(z):
    return 1.0 / (1.0 + jnp.exp(-z))


def _colsum(t):
    return jnp.sum(t, axis=0, keepdims=True)


def _exchange_copies(g_refs, r_refs, send_sems, recv_sems, local_sems, gather):
    x, y, c, me = _place()
    n_arr = len(g_refs)

    def src(t, slot):
        return g_refs[t] if gather else g_refs[t].at[slot]

    own = [pltpu.make_async_copy(src(t, me), r_refs[t].at[me], local_sems.at[t]) for t in range(n_arr)]
    sends, recvs = [], []
    for r in range(1, N_DEV):
        peer, lin = _flip(x, y, c, r)
        for t in range(n_arr):
            k = n_arr * (r - 1) + t
            sends.append(pltpu.make_async_remote_copy(
                src_ref=src(t, lin), dst_ref=r_refs[t].at[me], send_sem=send_sems.at[k], recv_sem=recv_sems.at[k],
                device_id=peer, device_id_type=MESH))
            recvs.append(pltpu.make_async_remote_copy(
                src_ref=src(t, lin), dst_ref=r_refs[t].at[lin], send_sem=send_sems.at[k], recv_sem=recv_sems.at[k],
                device_id=(x, y, c), device_id_type=MESH))
    return own, sends, recvs


def _exchange_start(copies):
    own, sends, _ = copies
    for cp in own + sends:
        cp.start()


def _exchange_wait(copies):
    own, sends, recvs = copies
    for cp in recvs:
        cp.wait_recv()
    for cp in sends:
        cp.wait_send()
    for cp in own:
        cp.wait()


def _exchange_extras(parts, gather):
    shapes = [jax.ShapeDtypeStruct(((N_DEV,) + p.shape) if gather else p.shape, p.dtype) for p in parts]
    n = len(parts) * (N_DEV - 1)
    return shapes, [pltpu.SemaphoreType.DMA((n,)), pltpu.SemaphoreType.DMA((n,)), pltpu.SemaphoreType.DMA((len(parts),))]


def _grid_call(body, name, grid, in_specs, out_specs, out_shape, args, scratch=(), side=None):
    if side is None:
        return pl.pallas_call(body, name=name, grid=grid, in_specs=in_specs, out_specs=out_specs,
                              out_shape=out_shape, scratch_shapes=list(scratch),
                              compiler_params=_params(len(grid)))(*args)
    parts, gather = side
    n_in, n_out, n_sc, n_arr = len(args), len(out_shape), len(scratch), len(parts)
    side_shapes, side_sems = _exchange_extras(parts, gather)

    def wrapped(*refs):
        ins, g_refs = refs[:n_in], refs[n_in:n_in + n_arr]
        outs = refs[n_in + n_arr:n_in + n_arr + n_out]
        r_refs = refs[n_in + n_arr + n_out:n_in + 2 * n_arr + n_out]
        sc = refs[n_in + 2 * n_arr + n_out:n_in + 2 * n_arr + n_out + n_sc]
        copies = _exchange_copies(g_refs, r_refs, *refs[-3:], gather)
        ids = [pl.program_id(a) for a in range(len(grid))]
        first = functools.reduce(jnp.logical_and, [i == 0 for i in ids])
        last = functools.reduce(jnp.logical_and, [i == g - 1 for i, g in zip(ids, grid)])

        @pl.when(first)
        def _():
            _exchange_start(copies)

        body(*ins, *outs, *sc)

        @pl.when(last)
        def _():
            _exchange_wait(copies)

    anyspace = pl.BlockSpec(memory_space=pl.ANY)
    return pl.pallas_call(
        wrapped, name=name, grid=grid, in_specs=list(in_specs) + [anyspace] * n_arr,
        out_specs=list(out_specs) + [anyspace] * n_arr, out_shape=list(out_shape) + side_shapes,
        scratch_shapes=list(scratch) + side_sems, compiler_params=_params(len(grid)),
    )(*args, *parts)


def _matmul(pairs, mode, out_dtype, m, n, tm, tn, name, side=None):
    dn = {"nn": NN, "nt": NT, "tn": TN}[mode]
    tm, tn = min(tm, m), min(tn, n)
    n_pairs = len(pairs)

    def body(*refs):
        o_ref = refs[-1]
        acc = None
        for p in range(n_pairs):
            t = _dot(refs[2 * p][...], refs[2 * p + 1][...], dn)
            acc = t if acc is None else acc + t
        o_ref[...] = acc.astype(o_ref.dtype)

    in_specs, args = [], []
    for a, b in pairs:
        if mode == "nn":
            k = a.shape[1]
            in_specs += [pl.BlockSpec((tm, k), lambda i, j: (i, 0)), pl.BlockSpec((k, tn), lambda i, j: (0, j))]
        elif mode == "nt":
            k = a.shape[1]
            in_specs += [pl.BlockSpec((tm, k), lambda i, j: (i, 0)), pl.BlockSpec((tn, k), lambda i, j: (j, 0))]
        else:
            k = a.shape[0]
            in_specs += [pl.BlockSpec((k, tm), lambda i, j: (0, i)), pl.BlockSpec((k, tn), lambda i, j: (0, j))]
        args += [a, b]
    return _grid_call(body, name, (m // tm, n // tn), in_specs, [pl.BlockSpec((tm, tn), lambda i, j: (i, j))],
                      [jax.ShapeDtypeStruct((m, n), out_dtype)], args, side=side)


def _matmul_cols_nn(a, w3, out_dtype, tm, name, side=None):
    m, k = a.shape
    _, _, n = w3.shape
    tm = min(tm, m)

    def body(a_ref, w_ref, o_ref):
        o_ref[...] = _dot(a_ref[...], w_ref[...], NN).astype(o_ref.dtype)

    return _grid_call(
        body, name, (m // tm, N_DEV),
        [pl.BlockSpec((tm, k), lambda i, j: (i, 0)), pl.BlockSpec((None, k, n), lambda i, j: (j, 0, 0))],
        [pl.BlockSpec((tm, n), lambda i, j: (i, j))], [jax.ShapeDtypeStruct((m, N_DEV * n), out_dtype)], [a, w3],
        side=side)


def _matmul_cols_nt(a, w3, out_dtype, tm, name, side=None):
    m = a.shape[0]
    _, k, n = w3.shape
    tm = min(tm, m)

    def body(a_ref, w_ref, o_ref, acc):
        j = pl.program_id(1)

        @pl.when(j == 0)
        def _():
            acc[...] = jnp.zeros_like(acc)

        acc[...] += _dot(a_ref[...], w_ref[...], NT)

        @pl.when(j == N_DEV - 1)
        def _():
            o_ref[...] = acc[...].astype(o_ref.dtype)

    return _grid_call(
        body, name, (m // tm, N_DEV),
        [pl.BlockSpec((tm, n), lambda i, j: (i, j)), pl.BlockSpec((None, k, n), lambda i, j: (j, 0, 0))],
        [pl.BlockSpec((tm, k), lambda i, j: (i, 0))], [jax.ShapeDtypeStruct((m, k), out_dtype)], [a, w3],
        scratch=[pltpu.VMEM((tm, k), F32)], side=side)


def _matmul_cols_tn(a, b, n, out_dtype, tk, name, side=None):
    m, k = a.shape
    tk = min(tk, k)

    def body(a_ref, b_ref, o_ref):
        o_ref[...] = _dot(a_ref[...], b_ref[...], TN).astype(o_ref.dtype)

    return _grid_call(
        body, name, (N_DEV, k // tk),
        [pl.BlockSpec((m, tk), lambda j, i: (0, i)), pl.BlockSpec((m, n), lambda j, i: (0, j))],
        [pl.BlockSpec((None, tk, n), lambda j, i: (j, i, 0))], [jax.ShapeDtypeStruct((N_DEV, k, n), out_dtype)], [a, b],
        side=side)


def _rows3(tm, d):
    return pl.BlockSpec((None, tm, d), lambda i: (0, i, 0))


def _modulate(x, mod, name):
    _, s, d = x.shape
    tm = min(s, 512)

    def body(x_ref, m_ref, h_ref):
        shift, scale = m_ref[0:1, 0:d], m_ref[0:1, d:2 * d]
        h_ref[...] = (x_ref[...] * (1.0 + scale) + shift).astype(BF16)

    return pl.pallas_call(
        body, name=name, grid=(s // tm,),
        in_specs=[_rows3(tm, d), pl.BlockSpec((8, 3 * d), lambda i: (0, 0))],
        out_specs=pl.BlockSpec((tm, d), lambda i: (i, 0)),
        out_shape=jax.ShapeDtypeStruct((s, d), BF16), compiler_params=_params(1),
    )(x, mod)


def _ln_stats(r):
    mu = jnp.mean(r, axis=-1, keepdims=True)
    rc = r - mu
    var = jnp.mean(rc * rc, axis=-1, keepdims=True)
    rstd = lax.rsqrt(var + LN_EPS)
    return rc * rstd, rstd


def _ln_bwd(dxhat, xhat, rstd):
    return rstd * (dxhat - jnp.mean(dxhat, axis=-1, keepdims=True)
                   - xhat * jnp.mean(dxhat * xhat, axis=-1, keepdims=True))


def _resid_ln(x, y, mod, g, b, name):
    _, s, d = x.shape
    tm = min(s, 512)

    def body(x_ref, y_ref, m_ref, g_ref, b_ref, o_ref):
        gate = m_ref[0:1, 2 * d:3 * d]
        xhat, _ = _ln_stats(DEEPNORM_ALPHA * x_ref[...] + (1.0 + gate) * y_ref[...])
        o_ref[...] = xhat * g_ref[...] + b_ref[...]

    row = pl.BlockSpec((tm, d), lambda i: (i, 0))
    vec = pl.BlockSpec((1, d), lambda i: (0, 0))
    return pl.pallas_call(
        body, name=name, grid=(s // tm,),
        in_specs=[_rows3(tm, d), row, pl.BlockSpec((8, 3 * d), lambda i: (0, 0)), vec, vec],
        out_specs=_rows3(tm, d), out_shape=jax.ShapeDtypeStruct((1, s, d), F32), compiler_params=_params(1),
    )(x, y, mod, g, b)


def _final_ln_loss_bwd(x, y, mod, g, b, target, name):
    _, s, d = x.shape
    tm = min(s, 256)

    def body(x_ref, y_ref, m_ref, g_ref, b_ref, t_ref, dy_ref, dx_ref, red_ref):
        @pl.when(pl.program_id(0) == 0)
        def _():
            red_ref[...] = jnp.zeros_like(red_ref)

        gate = m_ref[0:1, 2 * d:3 * d]
        yv = y_ref[...]
        xhat, rstd = _ln_stats(DEEPNORM_ALPHA * x_ref[...] + (1.0 + gate) * yv)
        err = xhat * g_ref[...] + b_ref[...] - t_ref[...]
        dout = err * (1.0 / d)
        dr = _ln_bwd(dout * g_ref[...], xhat, rstd)
        dy_ref[...] = ((1.0 + gate) * dr).astype(BF16)
        dx_ref[...] = DEEPNORM_ALPHA * dr
        red_ref[0:1, :] += _colsum(dout * xhat)
        red_ref[1:2, :] += _colsum(dout)
        red_ref[2:3, :] += _colsum(dr * yv)
        red_ref[3:4, :] += _colsum(err * err)

    row = pl.BlockSpec((tm, d), lambda i: (i, 0))
    vec = pl.BlockSpec((1, d), lambda i: (0, 0))
    return pl.pallas_call(
        body, name=name, grid=(s // tm,),
        in_specs=[_rows3(tm, d), row, pl.BlockSpec((8, 3 * d), lambda i: (0, 0)), vec, vec, _rows3(tm, d)],
        out_specs=[row, row, pl.BlockSpec((8, d), lambda i: (0, 0))],
        out_shape=[jax.ShapeDtypeStruct((s, d), BF16), jax.ShapeDtypeStruct((s, d), F32),
                   jax.ShapeDtypeStruct((8, d), F32)],
        compiler_params=_params(1),
    )(x, y, mod, g, b, target)


def _mid_bwd(dh, dxres, x, y, mod_lo, mod_hi, g, b, name):
    _, s, d = x.shape
    tm = min(s, 256)

    def body(dh_ref, dxr_ref, x_ref, y_ref, ml_ref, mh_ref, g_ref, b_ref, dy_ref, dx_ref, red_ref):
        @pl.when(pl.program_id(0) == 0)
        def _():
            red_ref[...] = jnp.zeros_like(red_ref)

        gate = ml_ref[0:1, 2 * d:3 * d]
        scale_hi = mh_ref[0:1, d:2 * d]
        yv, dhv = y_ref[...], dh_ref[...]
        xhat, rstd = _ln_stats(DEEPNORM_ALPHA * x_ref[...] + (1.0 + gate) * yv)
        x_mid = xhat * g_ref[...] + b_ref[...]
        dx_mid = dxr_ref[...] + dhv * (1.0 + scale_hi)
        dr = _ln_bwd(dx_mid * g_ref[...], xhat, rstd)
        dy_ref[...] = ((1.0 + gate) * dr).astype(BF16)
        dx_ref[...] = DEEPNORM_ALPHA * dr
        red_ref[0:1, :] += _colsum(dhv * x_mid)
        red_ref[1:2, :] += _colsum(dhv)
        red_ref[2:3, :] += _colsum(dx_mid * xhat)
        red_ref[3:4, :] += _colsum(dx_mid)
        red_ref[4:5, :] += _colsum(dr * yv)

    row = pl.BlockSpec((tm, d), lambda i: (i, 0))
    vec = pl.BlockSpec((1, d), lambda i: (0, 0))
    modspec = pl.BlockSpec((8, 3 * d), lambda i: (0, 0))
    return pl.pallas_call(
        body, name=name, grid=(s // tm,),
        in_specs=[row, row, _rows3(tm, d), row, modspec, modspec, vec, vec],
        out_specs=[row, row, pl.BlockSpec((8, d), lambda i: (0, 0))],
        out_shape=[jax.ShapeDtypeStruct((s, d), BF16), jax.ShapeDtypeStruct((s, d), F32),
                   jax.ShapeDtypeStruct((8, d), F32)],
        compiler_params=_params(1),
    )(dh, dxres, x, y, mod_lo, mod_hi, g, b)


def _first_bwd(dh, dxres, x, mod, name):
    _, s, d = x.shape
    tm = min(s, 512)

    def body(dh_ref, dxr_ref, x_ref, m_ref, gx_ref, red_ref):
        @pl.when(pl.program_id(0) == 0)
        def _():
            red_ref[...] = jnp.zeros_like(red_ref)

        scale = m_ref[0:1, d:2 * d]
        dhv = dh_ref[...]
        gx_ref[...] = dxr_ref[...] + dhv * (1.0 + scale)
        red_ref[0:1, :] += _colsum(dhv * x_ref[...])
        red_ref[1:2, :] += _colsum(dhv)

    row = pl.BlockSpec((tm, d), lambda i: (i, 0))
    return pl.pallas_call(
        body, name=name, grid=(s // tm,),
        in_specs=[row, row, _rows3(tm, d), pl.BlockSpec((8, 3 * d), lambda i: (0, 0))],
        out_specs=[_rows3(tm, d), pl.BlockSpec((8, d), lambda i: (0, 0))],
        out_shape=[jax.ShapeDtypeStruct((1, s, d), F32), jax.ShapeDtypeStruct((8, d), F32)],
        compiler_params=_params(1),
    )(dh, dxres, x, mod)


EVEN_TM = 256


def _gmlp_mask():
    t = lax.broadcasted_iota(jnp.int32, (GMLP_BLOCK, GMLP_BLOCK), 0) // CHUNK
    s = lax.broadcasted_iota(jnp.int32, (GMLP_BLOCK, GMLP_BLOCK), 1) // CHUNK
    return s <= t


def _window_sum(ext, win, back):
    n = ext.shape[0]
    k = 1
    while k < win:
        ext = ext + pltpu.roll(ext, k if back else n - k, 0)
        k *= 2
    return ext


def _inv_count(row0, rows, win):
    t = row0 + lax.broadcasted_iota(jnp.int32, (rows, 1), 0)
    return t, 1.0 / jnp.minimum(t + 1, win).astype(F32)


def _pooled(xb, halo, row0, win):
    tm = xb.shape[0]
    sums = _window_sum(jnp.concatenate([halo, xb], axis=0), win, True)[POOL_HALO:]
    _, inv = _inv_count(row0, tm, win)
    return sums * inv - xb


def _even_fwd(proj, ws, bs_col, ng, nb, pw, pb, ps, name):
    s = proj.shape[0]
    tm = min(s, EVEN_TM)
    hd, gd = GMLP_HEAD_DIM, POOL_GROUP_DIM

    def body(p_ref, halo_ref, ws_ref, bs_ref, ng_ref, nb_ref, pw_ref, pb_ref, ps_ref, m_ref):
        i = pl.program_id(0)
        mask = _gmlp_mask()
        for h in range(GMLP_HEADS):
            wm = jnp.where(mask, ws_ref[h], 0.0).astype(BF16)
            for blk in range(tm // GMLP_BLOCK):
                rows = slice(blk * GMLP_BLOCK, (blk + 1) * GMLP_BLOCK)
                cu, cv, cz = h * hd, 1024 + h * hd, 2048 + h * hd
                vhat, _ = _ln_stats(p_ref[rows, cv:cv + hd])
                vn = vhat * ng_ref[...] + nb_ref[...]
                sv = _dot(wm, vn, NN) + bs_ref[h]
                za = p_ref[rows, cz:cz + hd]
                m_ref[rows, cu:cu + hd] = (p_ref[rows, cu:cu + hd] * sv * (za * _sigmoid(za))).astype(BF16)
        for g, win in enumerate(POOL_WINDOWS):
            cx, cz = 3072 + g * gd, 4096 + g * gd
            halo = jnp.where(i > 0, halo_ref[:, g * gd:(g + 1) * gd], 0.0)
            pooled = _pooled(p_ref[:, cx:cx + gd], halo, i * tm, win)
            yb = _dot(pooled, pw_ref[g], NN) + pb_ref[:, g * gd:(g + 1) * gd]
            zb = p_ref[:, cz:cz + gd]
            m_ref[:, 1024 + g * gd:1024 + (g + 1) * gd] = (
                yb * ps_ref[:, g * gd:(g + 1) * gd] * (zb * _sigmoid(zb))).astype(BF16)

    hb = tm // POOL_HALO
    return pl.pallas_call(
        body, name=name, grid=(s // tm,),
        in_specs=[
            pl.BlockSpec((tm, 5120), lambda i: (i, 0)),
            pl.BlockSpec((POOL_HALO, 1024), lambda i: (jnp.maximum(i * hb - 1, 0), 3)),
            pl.BlockSpec((GMLP_HEADS, GMLP_BLOCK, GMLP_BLOCK), lambda i: (0, 0, 0)),
            pl.BlockSpec((GMLP_HEADS, GMLP_BLOCK, 1), lambda i: (0, 0, 0)),
            pl.BlockSpec((1, hd), lambda i: (0, 0)), pl.BlockSpec((1, hd), lambda i: (0, 0)),
            pl.BlockSpec((4, gd, gd), lambda i: (0, 0, 0)),
            pl.BlockSpec((1, 1024), lambda i: (0, 0)), pl.BlockSpec((1, 1024), lambda i: (0, 0)),
        ],
        out_specs=pl.BlockSpec((tm, 2048), lambda i: (i, 0)),
        out_shape=jax.ShapeDtypeStruct((s, 2048), BF16), compiler_params=_params(1),
    )(proj, proj, ws, bs_col, ng, nb, pw, pb, ps)


def _even_bwd(proj, dm, ws, bs_col, ng, nb, pw, pb, ps, name):
    s = proj.shape[0]
    tm = min(s, EVEN_TM)
    hd, gd = GMLP_HEAD_DIM, POOL_GROUP_DIM
    n_tiles = s // tm

    def body(p_ref, halo_ref, zbn_ref, dm_ref, dbn_ref, ws_ref, bs_ref, ng_ref, nb_ref, pw_ref, pb_ref, ps_ref,
             dp_ref, dws_ref, dbs_ref, dng_ref, dnb_ref, dpw_ref, dpb_ref, dps_ref):
        i = pl.program_id(0)

        @pl.when(i == 0)
        def _():
            for r in (dws_ref, dbs_ref, dng_ref, dnb_ref, dpw_ref, dpb_ref, dps_ref):
                r[...] = jnp.zeros_like(r)

        mask = _gmlp_mask()
        for h in range(GMLP_HEADS):
            wm = jnp.where(mask, ws_ref[h], 0.0).astype(BF16)
            for blk in range(tm // GMLP_BLOCK):
                rows = slice(blk * GMLP_BLOCK, (blk + 1) * GMLP_BLOCK)
                cu, cv, cz = h * hd, 1024 + h * hd, 2048 + h * hd
                vhat, rstd = _ln_stats(p_ref[rows, cv:cv + hd])
                vn = (vhat * ng_ref[...] + nb_ref[...]).astype(BF16)
                sv = _dot(wm, vn, NN) + bs_ref[h]
                u, za, da = p_ref[rows, cu:cu + hd], p_ref[rows, cz:cz + hd], dm_ref[rows, cu:cu + hd]
                sig = _sigmoid(za)
                sa = za * sig
                dau = da * u
                dsv = dau * sa
                dp_ref[rows, cu:cu + hd] = (da * sv * sa).astype(BF16)
                dp_ref[rows, cz:cz + hd] = (dau * sv * (sig * (1.0 + za * (1.0 - sig)))).astype(BF16)
                dsv_b = dsv.astype(BF16)
                dbs_ref[h] += jnp.sum(dsv, axis=1, keepdims=True)
                dws_ref[h] += _dot(dsv_b, vn, NT)
                dvn = _dot(wm, dsv_b, TN)
                dng_ref[...] += _colsum(dvn * vhat)
                dnb_ref[...] += _colsum(dvn)
                dp_ref[rows, cv:cv + hd] = _ln_bwd(dvn * ng_ref[...], vhat, rstd).astype(BF16)

        row0 = i * tm
        for g, win in enumerate(POOL_WINDOWS):
            cx, cz, cd = 3072 + g * gd, 4096 + g * gd, 1024 + g * gd
            gs = slice(g * gd, (g + 1) * gd)
            halo = jnp.where(i > 0, halo_ref[:, gs], 0.0)
            xb = p_ref[:, cx:cx + gd]
            pooled = _pooled(xb, halo, row0, win).astype(BF16)
            scale_g = ps_ref[:, gs]
            yb = _dot(pooled, pw_ref[g], NN) + pb_ref[:, gs]
            zb, db = p_ref[:, cz:cz + gd], dm_ref[:, cd:cd + gd]
            sig = _sigmoid(zb)
            dyp = db * (zb * sig)
            dp_ref[:, cz:cz + gd] = (db * yb * scale_g * (sig * (1.0 + zb * (1.0 - sig)))).astype(BF16)
            dps_ref[:, gs] += _colsum(dyp * yb)
            dpb_ref[:, gs] += _colsum(dyp * scale_g)
            zb_ext = jnp.concatenate([zb, zbn_ref[:, gs]], axis=0)
            db_ext = jnp.concatenate([db, dbn_ref[:, gs]], axis=0)
            dy_ext = (db_ext * (zb_ext * _sigmoid(zb_ext)) * scale_g).astype(BF16)
            dpw_ref[g] += _dot(pooled, dy_ext[:tm], TN)
            dpooled = _dot(dy_ext, pw_ref[g], NT)
            t, inv = _inv_count(row0, tm + POOL_HALO, win)
            w_ext = jnp.where(t < s, dpooled * inv, 0.0)
            dp_ref[:, cx:cx + gd] = (_window_sum(w_ext, win, False)[:tm] - dpooled[:tm]).astype(BF16)

    hb = tm // POOL_HALO
    last = s // POOL_HALO - 1
    small = lambda shape: pl.BlockSpec(shape, lambda i: (0,) * len(shape))
    return pl.pallas_call(
        body, name=name, grid=(n_tiles,),
        in_specs=[
            pl.BlockSpec((tm, 5120), lambda i: (i, 0)),
            pl.BlockSpec((POOL_HALO, 1024), lambda i: (jnp.maximum(i * hb - 1, 0), 3)),
            pl.BlockSpec((POOL_HALO, 1024), lambda i: (jnp.minimum((i + 1) * hb, last), 4)),
            pl.BlockSpec((tm, 2048), lambda i: (i, 0)),
            pl.BlockSpec((POOL_HALO, 1024), lambda i: (jnp.minimum((i + 1) * hb, last), 1)),
            small((GMLP_HEADS, GMLP_BLOCK, GMLP_BLOCK)), small((GMLP_HEADS, GMLP_BLOCK, 1)),
            small((1, hd)), small((1, hd)), small((4, gd, gd)), small((1, 1024)), small((1, 1024)),
        ],
        out_specs=[
            pl.BlockSpec((tm, 5120), lambda i: (i, 0)),
            small((GMLP_HEADS, GMLP_BLOCK, GMLP_BLOCK)), small((GMLP_HEADS, GMLP_BLOCK, 1)),
            small((1, hd)), small((1, hd)), small((4, gd, gd)), small((1, 1024)), small((1, 1024)),
        ],
        out_shape=[
            jax.ShapeDtypeStruct((s, 5120), BF16),
            jax.ShapeDtypeStruct((GMLP_HEADS, GMLP_BLOCK, GMLP_BLOCK), F32),
            jax.ShapeDtypeStruct((GMLP_HEADS, GMLP_BLOCK, 1), F32),
            jax.ShapeDtypeStruct((1, hd), F32), jax.ShapeDtypeStruct((1, hd), F32),
            jax.ShapeDtypeStruct((4, gd, gd), F32),
            jax.ShapeDtypeStruct((1, 1024), F32), jax.ShapeDtypeStruct((1, 1024), F32),
        ],
        compiler_params=_params(1),
    )(proj, proj, proj, dm, dm, ws, bs_col, ng, nb, pw, pb, ps)


def _rope_pair_swap(t):
    lane = lax.broadcasted_iota(jnp.int32, t.shape, 1)
    return jnp.where(lane % 64 < 32, pltpu.roll(t, 96, 1), pltpu.roll(t, 32, 1))


def _rms(x, g):
    r = lax.rsqrt(jnp.mean(x * x, axis=-1, keepdims=True) + LN_EPS)
    return x * r, r


def _rms_bwd(dy, g, xhat, r):
    dyg = dy * g
    return r * (dyg - xhat * jnp.mean(dyg * xhat, axis=-1, keepdims=True))


def _lane_lt(shape, n):
    return lax.broadcasted_iota(jnp.int32, shape, 1) < n


def _mla_prep(proj, cosp, sinp, gq, gkv, name):
    s = proj.shape[0]
    tm = min(s, 512)

    def body(qc_ref, kv_ref, c_ref, s_ref, gq_ref, gkv_ref, qn_ref, kp_ref):
        qhat, _ = _rms(qc_ref[...], None)
        qn_ref[...] = (qhat * gq_ref[...]).astype(BF16)
        khat, _ = _rms(kv_ref[:, 0:128], None)
        kp_ref[:, 0:128] = (khat * gkv_ref[...]).astype(BF16)
        kr = kv_ref[:, 128:256]
        kp_ref[:, 128:256] = (kr * c_ref[...] + _rope_pair_swap(kr) * s_ref[...]).astype(BF16)

    return pl.pallas_call(
        body, name=name, grid=(s // tm,),
        in_specs=[pl.BlockSpec((tm, 256), lambda i: (i, 0)), pl.BlockSpec((tm, 256), lambda i: (i, 1)),
                  pl.BlockSpec((tm, 128), lambda i: (i, 0)), pl.BlockSpec((tm, 128), lambda i: (i, 0)),
                  pl.BlockSpec((1, 256), lambda i: (0, 0)), pl.BlockSpec((1, 128), lambda i: (0, 0))],
        out_specs=[pl.BlockSpec((tm, 256), lambda i: (i, 0)), pl.BlockSpec((tm, 256), lambda i: (i, 0))],
        out_shape=[jax.ShapeDtypeStruct((s, 256), BF16), jax.ShapeDtypeStruct((s, 256), BF16)],
        compiler_params=_params(1),
    )(proj, proj, cosp, sinp, gq, gkv)


def _mla_prep_bwd(proj, dqn, dkp, dv, cosp, sinp, gq, gkv, dproj, name):
    s = proj.shape[0]
    tm = min(s, 512)

    def body(qc_ref, kv_ref, dqn_ref, dkp_ref, dv_ref, c_ref, s_ref, gq_ref, gkv_ref, dproj_ref, o_ref, red_ref):
        @pl.when(pl.program_id(0) == 0)
        def _():
            red_ref[...] = jnp.zeros_like(red_ref)

        qhat, qr = _rms(qc_ref[...], None)
        dq = dqn_ref[...]
        o_ref[:, 0:256] = _rms_bwd(dq, gq_ref[...], qhat, qr).astype(BF16)
        red_ref[0:1, :] += _colsum(dq * qhat)
        khat, kr = _rms(kv_ref[:, 0:128], None)
        dk = dkp_ref[:, 0:128] + dv_ref[...]
        o_ref[:, 256:384] = _rms_bwd(dk, gkv_ref[...], khat, kr).astype(BF16)
        red_ref[1:2, 0:128] += _colsum(dk * khat)
        dr = dkp_ref[:, 128:256]
        o_ref[:, 384:512] = (dr * c_ref[...] - _rope_pair_swap(dr) * s_ref[...]).astype(BF16)

    return pl.pallas_call(
        body, name=name, grid=(s // tm,),
        in_specs=[pl.BlockSpec((tm, 256), lambda i: (i, 0)), pl.BlockSpec((tm, 256), lambda i: (i, 1)),
                  pl.BlockSpec((tm, 256), lambda i: (i, 0)), pl.BlockSpec((tm, 256), lambda i: (i, 0)),
                  pl.BlockSpec((tm, 128), lambda i: (i, 0)),
                  pl.BlockSpec((tm, 128), lambda i: (i, 0)), pl.BlockSpec((tm, 128), lambda i: (i, 0)),
                  pl.BlockSpec((1, 256), lambda i: (0, 0)), pl.BlockSpec((1, 128), lambda i: (0, 0)),
                  pl.BlockSpec(memory_space=pl.ANY)],
        out_specs=[pl.BlockSpec((tm, 512), lambda i: (i, 0)), pl.BlockSpec((8, 256), lambda i: (0, 0))],
        out_shape=[jax.ShapeDtypeStruct(dproj.shape, BF16), jax.ShapeDtypeStruct((8, 256), F32)],
        input_output_aliases={9: 0}, compiler_params=_params(1),
    )(proj, proj, dqn, dkp, dv, cosp, sinp, gq, gkv, dproj)


HEAD_PAIRS = MLA_HEADS // 2
PAIR_TM = 512


def _q_heads(q_up, cosp, sinp, wuk, name):
    s = q_up.shape[0]
    tm = min(s, PAIR_TM)

    def body(qn_ref, qr_ref, c_ref, s_ref, w_ref, o_ref):
        raw = qr_ref[...]
        rot = raw * c_ref[...] + _rope_pair_swap(raw) * s_ref[...]
        low = _lane_lt(rot.shape, 64)
        o_ref[0, :, 0:128] = _dot(qn_ref[:, 0:128], w_ref[:, 0:128], NT).astype(BF16)
        o_ref[0, :, 128:256] = jnp.where(low, rot, 0.0).astype(BF16)
        o_ref[1, :, 0:128] = _dot(qn_ref[:, 128:256], w_ref[:, 128:256], NT).astype(BF16)
        o_ref[1, :, 128:256] = jnp.where(low, pltpu.roll(rot, 64, 1), 0.0).astype(BF16)

    return pl.pallas_call(
        body, name=name, grid=(s // tm, HEAD_PAIRS),
        in_specs=[pl.BlockSpec((tm, 256), lambda i, p: (i, p)),
                  pl.BlockSpec((tm, 128), lambda i, p: (i, 16 + p)),
                  pl.BlockSpec((tm, 128), lambda i, p: (i, 0)), pl.BlockSpec((tm, 128), lambda i, p: (i, 0)),
                  pl.BlockSpec((128, 256), lambda i, p: (0, p))],
        out_specs=pl.BlockSpec((2, tm, 256), lambda i, p: (p, i, 0)),
        out_shape=jax.ShapeDtypeStruct((MLA_HEADS, s, 256), BF16), compiler_params=_params(2),
    )(q_up, q_up, cosp, sinp, wuk)


def _q_heads_bwd(dqp, q_up, cosp, sinp, wuk, name):
    s = q_up.shape[0]
    tm = min(s, PAIR_TM)

    def body(dq_ref, qn_ref, c_ref, s_ref, w_ref, dn_ref, dr_ref, dw_ref):
        @pl.when(pl.program_id(1) == 0)
        def _():
            dw_ref[...] = jnp.zeros_like(dw_ref)

        for a in range(2):
            dlat = dq_ref[a, :, 0:128]
            cols = slice(128 * a, 128 * a + 128)
            dn_ref[:, cols] = _dot(dlat, w_ref[:, cols], NN).astype(BF16)
            dw_ref[:, cols] += _dot(dlat, qn_ref[:, cols], TN)
        drot = dq_ref[0, :, 128:256].astype(F32) + pltpu.roll(dq_ref[1, :, 128:256].astype(F32), 64, 1)
        dr_ref[...] = (drot * c_ref[...] - _rope_pair_swap(drot) * s_ref[...]).astype(BF16)

    return pl.pallas_call(
        body, name=name, grid=(HEAD_PAIRS, s // tm),
        in_specs=[pl.BlockSpec((2, tm, 256), lambda p, i: (p, i, 0)),
                  pl.BlockSpec((tm, 256), lambda p, i: (i, p)),
                  pl.BlockSpec((tm, 128), lambda p, i: (i, 0)), pl.BlockSpec((tm, 128), lambda p, i: (i, 0)),
                  pl.BlockSpec((128, 256), lambda p, i: (0, p))],
        out_specs=[pl.BlockSpec((tm, 256), lambda p, i: (i, p)),
                   pl.BlockSpec((tm, 128), lambda p, i: (i, p)),
                   pl.BlockSpec((128, 256), lambda p, i: (0, p))],
        out_shape=[jax.ShapeDtypeStruct((s, MLA_WIDTH), BF16), jax.ShapeDtypeStruct((s, 1024), BF16),
                   jax.ShapeDtypeStruct((128, MLA_WIDTH), F32)],
        compiler_params=_params(2),
    )(dqp, q_up, cosp, sinp, wuk)


def _o_gate(o_lat, proj, wuv, name):
    s = o_lat.shape[1]
    tm = min(s, PAIR_TM)

    def body(ol_ref, z_ref, w_ref, g_ref):
        for a in range(2):
            cols = slice(128 * a, 128 * a + 128)
            z = z_ref[:, cols]
            g_ref[:, cols] = (_dot(ol_ref[a], w_ref[:, cols], NN) * (z * _sigmoid(z))).astype(BF16)

    return pl.pallas_call(
        body, name=name, grid=(s // tm, HEAD_PAIRS),
        in_specs=[pl.BlockSpec((2, tm, 128), lambda i, p: (p, i, 0)),
                  pl.BlockSpec((tm, 256), lambda i, p: (i, 2 + p)),
                  pl.BlockSpec((128, 256), lambda i, p: (0, p))],
        out_specs=pl.BlockSpec((tm, 256), lambda i, p: (i, p)),
        out_shape=jax.ShapeDtypeStruct((s, MLA_WIDTH), BF16), compiler_params=_params(2),
    )(o_lat, proj, wuv)


def _o_gate_bwd(dg, o_lat, proj, wuv, name):
    s = o_lat.shape[1]
    tm = min(s, PAIR_TM)

    def body(dg_ref, ol_ref, z_ref, w_ref, dz_ref, dol_ref, dw_ref):
        @pl.when(pl.program_id(1) == 0)
        def _():
            dw_ref[...] = jnp.zeros_like(dw_ref)

        for a in range(2):
            cols = slice(128 * a, 128 * a + 128)
            z, dgv, ol = z_ref[:, cols], dg_ref[:, cols], ol_ref[a]
            sig = _sigmoid(z)
            o = _dot(ol, w_ref[:, cols], NN)
            dz_ref[:, cols] = (dgv * o * (sig * (1.0 + z * (1.0 - sig)))).astype(BF16)
            do = (dgv * (z * sig)).astype(BF16)
            dol_ref[a] = _dot(do, w_ref[:, cols], NT).astype(BF16)
            dw_ref[:, cols] += _dot(ol, do, TN)

    return pl.pallas_call(
        body, name=name, grid=(HEAD_PAIRS, s // tm),
        in_specs=[pl.BlockSpec((tm, 256), lambda p, i: (i, p)),
                  pl.BlockSpec((2, tm, 128), lambda p, i: (p, i, 0)),
                  pl.BlockSpec((tm, 256), lambda p, i: (i, 2 + p)),
                  pl.BlockSpec((128, 256), lambda p, i: (0, p))],
        out_specs=[pl.BlockSpec((tm, 256), lambda p, i: (i, 2 + p)),
                   pl.BlockSpec((2, tm, 128), lambda p, i: (p, i, 0)),
                   pl.BlockSpec((128, 256), lambda p, i: (0, p))],
        out_shape=[jax.ShapeDtypeStruct((s, ODD_IN_PAD), BF16), jax.ShapeDtypeStruct((MLA_HEADS, s, 128), BF16),
                   jax.ShapeDtypeStruct((128, MLA_WIDTH), F32)],
        compiler_params=_params(2),
    )(dg, o_lat, proj, wuv)


ATT_TQ = CHUNK
ATT_ROWS = ATT_TQ * MLA_HEADS
ATT_TK = 512
ATT_HEAD_GROUP = 4


def _visible(k0, q_chunk, tk):
    kpos = k0 + lax.broadcasted_iota(jnp.int32, (1, tk), 1)
    return kpos // CHUNK <= q_chunk


def _tile_lanes(t, n):
    return jnp.concatenate([t] * (n // 128), axis=1)


def _key_blocks(i, tk, block, pairs=False):
    n_full = (i * ATT_TQ + ATT_TQ + tk - 1) // tk - 1
    if pairs:
        def two(jj, carry):
            block(2 * jj, False)
            block(2 * jj + 1, False)
            return carry

        lax.fori_loop(0, n_full // 2, two, 0)

        @pl.when(n_full % 2 == 1)
        def _():
            block(n_full - 1, False)
    else:
        def one(j, carry):
            block(j, False)
            return carry

        lax.fori_loop(0, n_full, one, 0)
    block(n_full, True)


def _attn_fwd(qp, kp, name):
    s = kp.shape[0]
    tk = min(ATT_TK, s)

    def body(q_ref, k_ref, o_ref, lse_ref, m_sc, acc_sc):
        i = pl.program_id(0)
        m_sc[...] = jnp.full_like(m_sc, -jnp.inf)
        acc_sc[...] = jnp.zeros_like(acc_sc)

        def block(j, masked):
            k0 = pl.multiple_of(j * tk, tk)
            k = k_ref[pl.ds(k0, tk), :]
            v1 = jnp.where(_lane_lt(k.shape, 128), k, jnp.ones_like(k))
            for h0 in range(0, MLA_HEADS, ATT_HEAD_GROUP):
                rows = slice(h0 * ATT_TQ, (h0 + ATT_HEAD_GROUP) * ATT_TQ)
                q = q_ref[h0:h0 + ATT_HEAD_GROUP].reshape(ATT_HEAD_GROUP * ATT_TQ, 256)
                sc = _dot(q, k, NT) * ATTN_SCALE_LOG2
                if masked:
                    sc = jnp.where(_visible(k0, i, tk), sc, -jnp.inf)
                m_prev = m_sc[rows]
                m_new = jnp.maximum(m_prev, jnp.max(sc, axis=1, keepdims=True))
                p = jnp.exp2(sc - _tile_lanes(m_new, tk))
                acc_sc[rows] = _tile_lanes(jnp.exp2(m_prev - m_new), 256) * acc_sc[rows] + _dot(p, v1, NN)
                m_sc[rows] = m_new

        _key_blocks(i, tk, block, pairs=True)
        acc = acc_sc[...]
        l = acc[:, 128:256]
        o_ref[...] = (acc[:, 0:128] / l).astype(BF16).reshape(MLA_HEADS, ATT_TQ, 128)
        lse_ref[...] = (m_sc[...] + jnp.log2(l)).reshape(MLA_HEADS, ATT_TQ, 128)

    head128 = pl.BlockSpec((MLA_HEADS, ATT_TQ, 128), lambda i: (0, i, 0))
    return pl.pallas_call(
        body, name=name, grid=(s // ATT_TQ,),
        in_specs=[pl.BlockSpec((MLA_HEADS, ATT_TQ, 256), lambda i: (0, i, 0)), pl.BlockSpec((s, 256), lambda i: (0, 0))],
        out_specs=[head128, head128],
        out_shape=[jax.ShapeDtypeStruct((MLA_HEADS, s, 128), BF16), jax.ShapeDtypeStruct((MLA_HEADS, s, 128), F32)],
        scratch_shapes=[pltpu.VMEM((ATT_ROWS, 128), F32), pltpu.VMEM((ATT_ROWS, 256), F32)],
        compiler_params=_params(1),
    )(qp, kp)


def _attn_bwd(qp, kp, o, do, lse, name):
    s = kp.shape[0]
    tk = min(ATT_TK, s)

    def body(q_ref, k_ref, o_ref, do_ref, lse_ref, dq_ref, dk_ref, dv_ref, dq_sc):
        i = pl.program_id(0)

        @pl.when(i == 0)
        def _():
            dk_ref[...] = jnp.zeros_like(dk_ref)
            dv_ref[...] = jnp.zeros_like(dv_ref)

        q = q_ref[...].reshape(ATT_ROWS, 256)
        dov = do_ref[...].reshape(ATT_ROWS, 128)
        delta = jnp.sum(dov.astype(F32) * o_ref[...].reshape(ATT_ROWS, 128).astype(F32), axis=1, keepdims=True)
        delta_t = _tile_lanes(jnp.broadcast_to(delta, (ATT_ROWS, 128)), tk)
        lse_t = _tile_lanes(lse_ref[...].reshape(ATT_ROWS, 128), tk)
        dq_sc[...] = jnp.zeros_like(dq_sc)

        def block(j, masked):
            k0 = pl.multiple_of(j * tk, tk)
            k = k_ref[pl.ds(k0, tk), :]
            p = jnp.exp2(_dot(q, k, NT) * ATTN_SCALE_LOG2 - lse_t)
            if masked:
                p = jnp.where(_visible(k0, i, tk), p, 0.0)
            dv_ref[pl.ds(k0, tk), :] += _dot(p, dov, TN)
            ds = (p * (_dot(dov, k[:, 0:128], NT) - delta_t) * ATTN_SCALE).astype(BF16)
            dq_sc[...] += _dot(ds, k, NN)
            dk_ref[pl.ds(k0, tk), :] += _dot(ds, q, TN)

        _key_blocks(i, tk, block)
        dq_ref[...] = dq_sc[...].astype(BF16).reshape(MLA_HEADS, ATT_TQ, 256)

    head128 = pl.BlockSpec((MLA_HEADS, ATT_TQ, 128), lambda i: (0, i, 0))
    head256 = pl.BlockSpec((MLA_HEADS, ATT_TQ, 256), lambda i: (0, i, 0))
    return pl.pallas_call(
        body, name=name, grid=(s // ATT_TQ,),
        in_specs=[head256, pl.BlockSpec((s, 256), lambda i: (0, 0)), head128, head128, head128],
        out_specs=[head256, pl.BlockSpec((s, 256), lambda i: (0, 0)), pl.BlockSpec((s, 128), lambda i: (0, 0))],
        out_shape=[jax.ShapeDtypeStruct((MLA_HEADS, s, 256), BF16),
                   jax.ShapeDtypeStruct((s, 256), F32), jax.ShapeDtypeStruct((s, 128), F32)],
        scratch_shapes=[pltpu.VMEM((ATT_ROWS, 256), F32)],
        compiler_params=_params(1),
    )(qp, kp, o, do, lse)


def _place():
    x, y, c = lax.axis_index("x"), lax.axis_index("y"), lax.axis_index("c")
    return x, y, c, 4 * x + 2 * y + c


def _flip(x, y, c, r):
    px = 1 - x if r & 4 else x
    py = 1 - y if r & 2 else y
    pc = 1 - c if r & 1 else c
    return (px, py, pc), 4 * px + 2 * py + pc


def _adaln_exchange(c8, ada_w, ada_b_cols, name):
    d = c8.shape[1]
    w_cols = ada_w.shape[2]

    def body(c_ref, w_ref, b_ref, call_ref, mod_ref, sbuf, rbuf, s1, r1, s2, r2):
        x, y, c, me = _place()
        call_ref[pl.ds(pl.multiple_of(me * 8, 8), 8), :] = c_ref[...]
        peers = [_flip(x, y, c, r) for r in range(1, N_DEV)]

        def c_copy(k, src_lin, to):
            rows = call_ref.at[pl.ds(pl.multiple_of(src_lin * 8, 8), 8), :]
            return pltpu.make_async_remote_copy(src_ref=rows, dst_ref=rows, send_sem=s1.at[k], recv_sem=r1.at[k],
                                                device_id=to, device_id_type=MESH)

        first = [c_copy(k, me, peer) for k, (peer, _) in enumerate(peers)]
        for cp in first:
            cp.start()
        for k, (_, lin) in enumerate(peers):
            c_copy(k, lin, (x, y, c)).wait_recv()
        for cp in first:
            cp.wait_send()

        for j in range(N_DEV):
            cj = call_ref[8 * j:8 * j + 8, :]
            cond = cj * _sigmoid(cj)
            for l in range(2):
                sbuf[j, l] = lax.dot_general(cond, w_ref[l], NN, precision=lax.Precision.HIGHEST,
                                             preferred_element_type=F32) + b_ref[l]

        def m_copy(k, src_slot, dst_slot, to):
            return pltpu.make_async_remote_copy(src_ref=sbuf.at[src_slot], dst_ref=rbuf.at[dst_slot],
                                                send_sem=s2.at[k], recv_sem=r2.at[k], device_id=to,
                                                device_id_type=MESH)

        rbuf[me] = sbuf[me]
        second = [m_copy(k, lin, me, peer) for k, (peer, lin) in enumerate(peers)]
        for cp in second:
            cp.start()
        for k, (_, lin) in enumerate(peers):
            m_copy(k, lin, lin, (x, y, c)).wait_recv()
        for cp in second:
            cp.wait_send()
        for j in range(N_DEV):
            for l in range(2):
                mod_ref[l, :, w_cols * j:w_cols * (j + 1)] = rbuf[j, l]

    vmem = pl.BlockSpec(memory_space=pltpu.VMEM)
    return pl.pallas_call(
        body, name=name, in_specs=[vmem, vmem, vmem], out_specs=[vmem, vmem],
        out_shape=[jax.ShapeDtypeStruct((8 * N_DEV, d), F32), jax.ShapeDtypeStruct((2, 8, 3 * d), F32)],
        scratch_shapes=[pltpu.VMEM((N_DEV, 2, 8, w_cols), F32), pltpu.VMEM((N_DEV, 2, 8, w_cols), F32),
                        pltpu.SemaphoreType.DMA((N_DEV - 1,)), pltpu.SemaphoreType.DMA((N_DEV - 1,)),
                        pltpu.SemaphoreType.DMA((N_DEV - 1,)), pltpu.SemaphoreType.DMA((N_DEV - 1,))],
        compiler_params=pltpu.CompilerParams(vmem_limit_bytes=VMEM_LIMIT),
    )(c8, ada_w, ada_b_cols)


def _all_gather(blocks, name):
    n_arr = len(blocks)

    def body(*refs):
        x_refs, out_refs = refs[:n_arr], refs[n_arr:2 * n_arr]
        send_sems, recv_sems, local_sems = refs[2 * n_arr:]
        x, y, c, _ = _place()
        me, sibling = (x, y, c), (x, y, 1 - c)
        chips = [(1 - x, y), (x, 1 - y), (1 - x, 1 - y)]

        def copy(t, k, blk, to, src=None):
            slot = out_refs[t].at[4 * blk[0] + 2 * blk[1] + blk[2]]
            return pltpu.make_async_remote_copy(src_ref=slot if src is None else src, dst_ref=slot,
                                                send_sem=send_sems.at[7 * t + k], recv_sem=recv_sems.at[7 * t + k],
                                                device_id=to, device_id_type=MESH)

        mine = [pltpu.make_async_copy(x_refs[t], out_refs[t].at[4 * x + 2 * y + c], local_sems.at[t])
                for t in range(n_arr)]
        for cp in mine:
            cp.start()
        first = []
        for t in range(n_arr):
            first.append(copy(t, 0, me, sibling, src=x_refs[t]))
            first += [copy(t, 1 + j, me, (*chip, c), src=x_refs[t]) for j, chip in enumerate(chips)]
        for cp in first:
            cp.start()
        passed = []
        for t in range(n_arr):
            for j, chip in enumerate(chips):
                copy(t, 1 + j, (*chip, c), me).wait_recv()
                passed.append(copy(t, 4 + j, (*chip, c), sibling))
                passed[-1].start()
        for t in range(n_arr):
            copy(t, 0, sibling, me).wait_recv()
            for j, chip in enumerate(chips):
                copy(t, 4 + j, (*chip, 1 - c), me).wait_recv()
        for cp in first + passed:
            cp.wait_send()
        for cp in mine:
            cp.wait()

    anyspace = pl.BlockSpec(memory_space=pl.ANY)
    return pl.pallas_call(
        body, name=name, in_specs=[anyspace] * n_arr, out_specs=[anyspace] * n_arr,
        out_shape=[jax.ShapeDtypeStruct((N_DEV,) + b.shape, b.dtype) for b in blocks],
        scratch_shapes=[pltpu.SemaphoreType.DMA((7 * n_arr,)), pltpu.SemaphoreType.DMA((7 * n_arr,)),
                        pltpu.SemaphoreType.DMA((n_arr,))],
    )(*blocks)


def _scatter_parts(parts, name):
    n_arr = len(parts)

    def body(*refs):
        copies = _exchange_copies(refs[:n_arr], refs[n_arr:2 * n_arr], *refs[2 * n_arr:], False)
        _exchange_start(copies)
        _exchange_wait(copies)

    anyspace = pl.BlockSpec(memory_space=pl.ANY)
    shapes, sems = _exchange_extras(parts, False)
    return pl.pallas_call(body, name=name, in_specs=[anyspace] * n_arr, out_specs=[anyspace] * n_arr,
                          out_shape=shapes, scratch_shapes=sems)(*parts)


def _adamw(w, g, m, v):
    m = ADAM_B1 * m + (1.0 - ADAM_B1) * g
    v = ADAM_B2 * v + (1.0 - ADAM_B2) * (g * g)
    m_hat = m / (1.0 - ADAM_B1 ** ADAM_STEP)
    v_hat = v / (1.0 - ADAM_B2 ** ADAM_STEP)
    return -ADAM_LR * (m_hat / (jnp.sqrt(v_hat) + ADAM_EPS) + ADAM_WD * w), m, v


def _sum_parts_adamw(parts, w, m, v, name):
    _, rows, cols = parts.shape
    tr = max(t for t in range(16, 129, 16) if rows % t == 0)

    def body(p_ref, w_ref, m_ref, v_ref, g_ref, d_ref, mo_ref, vo_ref):
        g = p_ref[0].astype(F32)
        for j in range(1, N_DEV):
            g = g + p_ref[j].astype(F32)
        g_ref[...] = g
        d_ref[...], mo_ref[...], vo_ref[...] = _adamw(w_ref[...], g, m_ref[...], v_ref[...])

    row = _rows3(tr, cols)
    out = jax.ShapeDtypeStruct((1, rows, cols), F32)
    return pl.pallas_call(
        body, name=name, grid=(rows // tr,),
        in_specs=[pl.BlockSpec((N_DEV, tr, cols), lambda i: (0, i, 0)), row, row, row],
        out_specs=[row, row, row, row], out_shape=[out, out, out, out], compiler_params=_params(1),
    )(parts, w, m, v)


def _adamw_call(g, w, m, v, name):
    rows, cols = g.shape
    tr = 128 if rows % 128 == 0 else rows

    def body(g_ref, w_ref, m_ref, v_ref, d_ref, mo_ref, vo_ref):
        d_ref[...], mo_ref[...], vo_ref[...] = _adamw(w_ref[...], g_ref[...], m_ref[...], v_ref[...])

    row = pl.BlockSpec((tr, cols), lambda i: (i, 0))
    out = jax.ShapeDtypeStruct((rows, cols), F32)
    return pl.pallas_call(
        body, name=name, grid=(rows // tr,), in_specs=[row] * 4, out_specs=[row] * 3, out_shape=[out] * 3,
        compiler_params=_params(1),
    )(g, w, m, v)


def _ada_w_grad_adamw(c_all, dmod_rows, w, m, v, name):
    def body(c_ref, dm_ref, w_ref, m_ref, v_ref, g_ref, d_ref, mo_ref, vo_ref):
        cv = c_ref[...]
        cond = cv * _sigmoid(cv)
        for l in range(2):
            g = lax.dot_general(cond, dm_ref[l], TN, precision=lax.Precision.HIGHEST, preferred_element_type=F32)
            g_ref[l] = g
            d_ref[l], mo_ref[l], vo_ref[l] = _adamw(w_ref[l], g, m_ref[l], v_ref[l])

    out = jax.ShapeDtypeStruct(w.shape, F32)
    return pl.pallas_call(
        body, name=name, out_shape=[out] * 4, compiler_params=pltpu.CompilerParams(vmem_limit_bytes=VMEM_LIMIT),
    )(c_all, dmod_rows, w, m, v)


REPLICATED = ("ln_g", "ln_b", "gmlp_norm_g", "gmlp_norm_b", "gmlp_ws", "gmlp_bs", "pool_b", "pool_scale",
              "mla_kv_norm_g", "mla_w_uk", "mla_w_uv")
CHUNK_ROWS, ADA_ROW, QNORM_ROW, REP_ROWS = 73, 73, 74, 80
UQ_ROWS, POOLW_ROWS = 96, 32
SMALL_ROWS = REP_ROWS + UQ_ROWS + POOLW_ROWS


def _pad_rows(flat2d, rows):
    n, k = flat2d.shape
    return jnp.pad(flat2d, ((0, 0), (0, rows * LANES - k))).reshape(n, rows, LANES)


def _ada_cols_rows(vec):
    return _pad_rows(vec.reshape(2, N_DEV, -1).transpose(1, 0, 2).reshape(N_DEV, -1), 1)


def _pack_replicated(src, ada_vec):
    flat = jnp.concatenate([src[n].reshape(-1) for n in REPLICATED])
    body = _pad_rows(flat.reshape(N_DEV, -1), CHUNK_ROWS)
    return jnp.concatenate([body, jnp.pad(_ada_cols_rows(ada_vec), ((0, 0), (0, REP_ROWS - CHUNK_ROWS - 1), (0, 0)))],
                           axis=1)


def _unpack_replicated(rep, shapes):
    chunk = sum(s[1] for s in shapes) // N_DEV
    flat, off, out = rep[:, :CHUNK_ROWS].reshape(N_DEV, -1)[:, :chunk].reshape(-1), 0, {}
    for n, size, shape in shapes:
        out[n] = flat[off:off + size].reshape(shape)
        off += size
    cols = 3 * D_MODEL // N_DEV
    out["ada_b"] = rep[:, ADA_ROW, :2 * cols].reshape(N_DEV, 2, cols).transpose(1, 0, 2).reshape(2, -1)
    return out


def kernel(x, c, positions, ada_w, ada_b, ln_g, ln_b, e_w_in, gmlp_norm_g, gmlp_norm_b, gmlp_ws, gmlp_bs, pool_w, pool_b, pool_scale, e_w_out, o_w_in, mla_q_norm_g, mla_kv_norm_g, mla_w_uq, mla_w_uk, mla_w_uv, o_w_out, loss_target, m_ada_w, m_ada_b, m_ln_g, m_ln_b, m_e_w_in, m_gmlp_norm_g, m_gmlp_norm_b, m_gmlp_ws, m_gmlp_bs, m_pool_w, m_pool_b, m_pool_scale, m_e_w_out, m_o_w_in, m_mla_q_norm_g, m_mla_kv_norm_g, m_mla_w_uq, m_mla_w_uk, m_mla_w_uv, m_o_w_out, v_ada_w, v_ada_b, v_ln_g, v_ln_b, v_e_w_in, v_gmlp_norm_g, v_gmlp_norm_b, v_gmlp_ws, v_gmlp_bs, v_pool_w, v_pool_b, v_pool_scale, v_e_w_out, v_o_w_in, v_mla_q_norm_g, v_mla_kv_norm_g, v_mla_w_uq, v_mla_w_uk, v_mla_w_uv, v_o_w_out):
    w_in = dict(ada_w=ada_w, ada_b=ada_b, ln_g=ln_g, ln_b=ln_b, e_w_in=e_w_in, gmlp_norm_g=gmlp_norm_g,
                gmlp_norm_b=gmlp_norm_b, gmlp_ws=gmlp_ws, gmlp_bs=gmlp_bs, pool_w=pool_w, pool_b=pool_b,
                pool_scale=pool_scale, e_w_out=e_w_out, o_w_in=o_w_in, mla_q_norm_g=mla_q_norm_g,
                mla_kv_norm_g=mla_kv_norm_g, mla_w_uq=mla_w_uq, mla_w_uk=mla_w_uk, mla_w_uv=mla_w_uv, o_w_out=o_w_out)
    m_in = dict(ada_w=m_ada_w, ada_b=m_ada_b, ln_g=m_ln_g, ln_b=m_ln_b, e_w_in=m_e_w_in, gmlp_norm_g=m_gmlp_norm_g,
                gmlp_norm_b=m_gmlp_norm_b, gmlp_ws=m_gmlp_ws, gmlp_bs=m_gmlp_bs, pool_w=m_pool_w, pool_b=m_pool_b,
                pool_scale=m_pool_scale, e_w_out=m_e_w_out, o_w_in=m_o_w_in, mla_q_norm_g=m_mla_q_norm_g,
                mla_kv_norm_g=m_mla_kv_norm_g, mla_w_uq=m_mla_w_uq, mla_w_uk=m_mla_w_uk, mla_w_uv=m_mla_w_uv,
                o_w_out=m_o_w_out)
    v_in = dict(ada_w=v_ada_w, ada_b=v_ada_b, ln_g=v_ln_g, ln_b=v_ln_b, e_w_in=v_e_w_in, gmlp_norm_g=v_gmlp_norm_g,
                gmlp_norm_b=v_gmlp_norm_b, gmlp_ws=v_gmlp_ws, gmlp_bs=v_gmlp_bs, pool_w=v_pool_w, pool_b=v_pool_b,
                pool_scale=v_pool_scale, e_w_out=v_e_w_out, o_w_in=v_o_w_in, mla_q_norm_g=v_mla_q_norm_g,
                mla_kv_norm_g=v_mla_kv_norm_g, mla_w_uq=v_mla_w_uq, mla_w_uk=v_mla_w_uk, mla_w_uv=v_mla_w_uv,
                o_w_out=v_o_w_out)
    names = list(w_in)
    seq = x.shape[1]
    d = D_MODEL
    me = 4 * lax.axis_index("x") + 2 * lax.axis_index("y") + lax.axis_index("c")
    ada_cols = ada_w.shape[2]

    ada_b_cols = lax.dynamic_slice_in_dim(ada_b, me * ada_cols, ada_cols, axis=1)
    slab_row = lax.broadcasted_iota(jnp.int32, (8, d), 0)
    slab = jnp.where(slab_row == 0, c, jnp.where(slab_row == 1, jnp.pad(mla_q_norm_g, ((0, 0), (0, d - 32))), 0.0))
    c_all, mod = _adaln_exchange(slab, ada_w,
                                 jnp.broadcast_to(ada_b_cols[:, None, :], (2, 8, ada_cols)), "adaln_exchange")

    uq_len = mla_w_uq.size
    small_b = jnp.concatenate([mla_w_uq.reshape(-1), pool_w.reshape(-1)]).astype(BF16).reshape(-1, LANES)
    w_in_e3, w_out_e3 = _all_gather([e_w_in[0].astype(BF16), e_w_out[0].astype(BF16)], "weight_gather")
    h0 = _modulate(x, mod[0], "modulate0")
    proj0, o_in3, w_out_o3, small3 = _matmul_cols_nn(
        h0, w_in_e3, F32, 512, "even_in", side=([o_w_in[0].astype(BF16), o_w_out[0].astype(BF16), small_b], True))
    w_out_e, w_out_o = w_out_e3.reshape(-1, d), w_out_o3.reshape(-1, d)
    o_in_full = o_in3.transpose(1, 0, 2).reshape(d, ODD_IN)
    w_in_o = jnp.concatenate([o_in_full[:, :448], jnp.zeros((d, 64), BF16), o_in_full[:, 448:]], axis=1)
    small_flat = small3.reshape(N_DEV, -1)
    uq_full = small_flat[:, :uq_len].reshape(MLA_Q_RANK, MLA_HEADS, MLA_NOPE + MLA_ROPE)
    pool_w_full = small_flat[:, uq_len:].reshape(N_DEV, 4, 32, 256).transpose(1, 0, 2, 3).reshape(4, 256, 256)
    w_uq_n = uq_full[:, :, :MLA_NOPE].reshape(MLA_Q_RANK, -1)
    w_uq_r = uq_full[:, :, MLA_NOPE:].reshape(MLA_Q_RANK, -1)
    w_uq = jnp.concatenate([w_uq_n, w_uq_r], axis=1)
    g_q = c_all.reshape(N_DEV, 8, d)[:, 1, :32].reshape(1, MLA_Q_RANK)

    ws, bs_col = gmlp_ws[0], gmlp_bs[0].reshape(GMLP_HEADS, GMLP_BLOCK, 1)
    wuk2, wuv2 = mla_w_uk[0].reshape(MLA_KV_RANK, -1), mla_w_uv[0].reshape(MLA_KV_RANK, -1)
    inv = 1.0 / (ROPE_THETA ** (jnp.arange(0, MLA_ROPE, 2, dtype=F32) / MLA_ROPE))
    ang = positions[0].astype(F32)[:, None] * inv
    cosp = jnp.tile(jnp.cos(ang), (1, 4))
    sinp = jnp.tile(jnp.concatenate([-jnp.sin(ang), jnp.sin(ang)], axis=1), (1, 2))

    mix0 = _even_fwd(proj0, ws, bs_col, gmlp_norm_g, gmlp_norm_b, pool_w_full, pool_b, pool_scale, "even_mix")
    (y0,) = _matmul([(mix0, w_out_e)], "nn", F32, seq, d, 512, 1024, "even_out")
    x1 = _resid_ln(x, y0, mod[0], ln_g[0:1], ln_b[0:1], "resid_ln0")

    h1 = _modulate(x1, mod[1], "modulate1")
    (proj1,) = _matmul([(h1, w_in_o)], "nn", F32, seq, ODD_IN_PAD, 512, 512, "odd_in")
    qn, kp = _mla_prep(proj1, cosp, sinp, g_q, mla_kv_norm_g, "mla_prep")
    (q_up,) = _matmul([(qn, w_uq)], "nn", F32, seq, 3072, 512, 1024, "q_up")
    qp = _q_heads(q_up, cosp, sinp, wuk2, "q_heads")
    o_lat, lse = _attn_fwd(qp, kp, "attn_fwd")
    gated = _o_gate(o_lat, proj1, wuv2, "o_gate")
    (y1,) = _matmul([(gated, w_out_o)], "nn", F32, seq, d, 512, 1024, "odd_out")

    dy1, dxres1, red2 = _final_ln_loss_bwd(x1, y1, mod[1], ln_g[1:2], ln_b[1:2], loss_target, "final_ln_loss")
    (dgated,) = _matmul([(dy1, w_out_o)], "nt", F32, seq, MLA_WIDTH, 512, 512, "odd_out_dx")
    (g_w_out_o,) = _matmul([(gated, dy1)], "tn", BF16, MLA_WIDTH, d, 256, 512, "odd_out_dw")
    dproj1_z, do_lat, g_wuv = _o_gate_bwd(dgated, o_lat, proj1, wuv2, "o_gate_bwd")
    dqp, dkp, dvv = _attn_bwd(qp, kp, o_lat, do_lat, lse, "attn_bwd")
    dq_nope, dq_rope, g_wuk = _q_heads_bwd(dqp, q_up, cosp, sinp, wuk2, "q_heads_bwd")
    (dqn,) = _matmul([(dq_nope, w_uq_n), (dq_rope, w_uq_r)], "nt", F32, seq, MLA_Q_RANK, 512, 256, "q_up_dx")
    (g_wuq_n,) = _matmul([(qn, dq_nope)], "tn", F32, MLA_Q_RANK, MLA_WIDTH, 256, 512, "q_up_dw_nope")
    (g_wuq_r,) = _matmul([(qn, dq_rope)], "tn", F32, MLA_Q_RANK, 1024, 256, 512, "q_up_dw_rope")
    dproj1, red_mla = _mla_prep_bwd(proj1, dqn, dkp, dvv, cosp, sinp, g_q, mla_kv_norm_g, dproj1_z, "mla_prep_bwd")
    (dh1,) = _matmul([(dproj1, w_in_o)], "nt", F32, seq, d, 512, 512, "odd_in_dx")
    (g_w_in_o,) = _matmul([(h1, dproj1)], "tn", BF16, d, ODD_IN_PAD, 256, 512, "odd_in_dw")
    part_o_in = jnp.concatenate([g_w_in_o[:, :448], g_w_in_o[:, 512:]], axis=1).reshape(d, N_DEV, -1).transpose(1, 0, 2)
    dy0, dxres0, red1 = _mid_bwd(dh1, dxres1, x, y0, mod[0], mod[1], ln_g[0:1], ln_b[0:1], "mid_bwd")
    dmix, r_o_out = _matmul([(dy0, w_out_e)], "nt", F32, seq, 2048, 512, 512, "even_out_dx",
                            side=([g_w_out_o.reshape(N_DEV, -1, d)], False))
    g_w_out_e, r_o_in = _matmul([(mix0, dy0)], "tn", BF16, 2048, d, 256, 512, "even_out_dw", side=([part_o_in], False))
    dproj0, g_ws, g_bs, g_ng, g_nb, g_pw, g_pb, g_ps = _even_bwd(
        proj0, dmix, ws, bs_col, gmlp_norm_g, gmlp_norm_b, pool_w_full, pool_b, pool_scale, "even_mix_bwd")
    part_e_in, r_e_out = _matmul_cols_tn(h0, dproj0, w_in_e3.shape[2], BF16, 256, "even_in_dw",
                                         side=([g_w_out_e.reshape(N_DEV, -1, d)], False))
    dh0, r_e_in = _matmul_cols_nt(dproj0, w_in_e3, F32, 512, "even_in_dx", side=([part_e_in], False))
    grad_x, red0 = _first_bwd(dh0, dxres0, x, mod[0], "first_bwd")

    loss = lax.psum(0.5 / d * jnp.sum(red2[3]), ("x", "y", "c"))

    t_mask = lax.broadcasted_iota(jnp.int32, (GMLP_BLOCK, GMLP_BLOCK), 0) // CHUNK
    s_mask = lax.broadcasted_iota(jnp.int32, (GMLP_BLOCK, GMLP_BLOCK), 1) // CHUNK
    part = {
        "ln_g": jnp.stack([red1[2], red2[0]]), "ln_b": jnp.stack([red1[3], red2[1]]),
        "gmlp_norm_g": g_ng, "gmlp_norm_b": g_nb,
        "gmlp_ws": jnp.where(s_mask <= t_mask, g_ws, 0.0), "gmlp_bs": g_bs,
        "pool_b": g_pb, "pool_scale": g_ps, "mla_kv_norm_g": red_mla[1, :MLA_KV_RANK],
        "mla_w_uk": g_wuk, "mla_w_uv": g_wuv,
    }
    dmod = jnp.stack([jnp.concatenate([red0[1], red0[0], red1[4]]),
                      jnp.concatenate([red1[1], red1[0], red2[2]])])

    g_uq = jnp.concatenate([g_wuq_n.reshape(MLA_Q_RANK, MLA_HEADS, MLA_NOPE),
                            g_wuq_r.reshape(MLA_Q_RANK, MLA_HEADS, MLA_ROPE)], axis=2)
    part_small = jnp.concatenate([
        _pad_rows(jnp.concatenate([part[n].reshape(-1) for n in REPLICATED]).reshape(N_DEV, -1), CHUNK_ROWS),
        jnp.pad(jnp.concatenate([_ada_cols_rows(dmod), _pad_rows(red_mla[0].reshape(N_DEV, -1), 1)], axis=1),
                ((0, 0), (0, REP_ROWS - QNORM_ROW - 1), (0, 0))),
        g_uq.reshape(N_DEV, UQ_ROWS, LANES),
        g_pw.reshape(4, N_DEV, 32, 256).transpose(1, 0, 2, 3).reshape(N_DEV, POOLW_ROWS, LANES)], axis=1)
    (r_small,) = _scatter_parts([part_small], "grad_scatter")

    def small_local(src):
        return jnp.concatenate([jnp.pad(src["mla_q_norm_g"], ((QNORM_ROW, REP_ROWS - QNORM_ROW - 1), (0, LANES - 32))),
                                src["mla_w_uq"].reshape(UQ_ROWS, LANES), src["pool_w"].reshape(POOLW_ROWS, LANES)])[None]

    res = {"e_w_in": _sum_parts_adamw(r_e_in, e_w_in, m_e_w_in, v_e_w_in, "adamw_e_w_in"),
           "o_w_in": _sum_parts_adamw(r_o_in, o_w_in, m_o_w_in, v_o_w_in, "adamw_o_w_in"),
           "e_w_out": _sum_parts_adamw(r_e_out, e_w_out, m_e_w_out, v_e_w_out, "adamw_e_w_out"),
           "o_w_out": _sum_parts_adamw(r_o_out, o_w_out, m_o_w_out, v_o_w_out, "adamw_o_w_out")}
    small = _sum_parts_adamw(r_small, small_local(w_in), small_local(m_in), small_local(v_in), "adamw_small")
    for n, r0, r1 in (("mla_w_uq", REP_ROWS, REP_ROWS + UQ_ROWS), ("pool_w", REP_ROWS + UQ_ROWS, SMALL_ROWS)):
        res[n] = [t[0, r0:r1].reshape(w_in[n].shape) for t in small]
    res["mla_q_norm_g"] = [t[0, QNORM_ROW:QNORM_ROW + 1, :32] for t in small]
    (rep_sum,) = _all_gather([small[0][0, :REP_ROWS]], "replicated_gather")
    rep_w, rep_m, rep_v = (_pack_replicated(src, src["ada_b"]).reshape(-1, LANES) for src in (w_in, m_in, v_in))
    rep_res = (rep_sum,) + tuple(t.reshape(N_DEV, REP_ROWS, LANES)
                                 for t in _adamw_call(rep_sum.reshape(-1, LANES), rep_w, rep_m, rep_v, "replicated_adamw"))
    shapes = [(n, w_in[n].size, w_in[n].shape) for n in REPLICATED]
    for k, t in enumerate(rep_res):
        for n, val in _unpack_replicated(t, shapes).items():
            res.setdefault(n, [None] * 4)[k] = val
    dmod_all = r_small[:, ADA_ROW, :2 * ada_cols].reshape(N_DEV, 2, ada_cols).transpose(1, 0, 2)
    dmod_rows = jnp.pad(dmod_all[:, :, None, :], ((0, 0), (0, 0), (0, 7), (0, 0))).reshape(2, 8 * N_DEV, ada_cols)
    res["ada_w"] = _ada_w_grad_adamw(c_all, dmod_rows, ada_w, m_ada_w, v_ada_w, "ada_w_adamw")

    return (loss, grad_x, *[res[n][0] for n in names], *[res[n][1] for n in names],
            *[res[n][2] for n in names], *[res[n][3] for n in names])
```

```python
import functools

import jax
import jax.numpy as jnp
from jax import lax
from jax.experimental import pallas as pl
from jax.experimental.pallas import tpu as pltpu

F32 = jnp.float32
BF16 = jnp.bfloat16

D_MODEL = 1024
CHUNK = 64
LN_EPS = 1e-5
GMLP_HEADS = 4
GMLP_HEAD_DIM = 256
GMLP_BLOCK = 128
POOL_WINDOWS = (2, 4, 8, 16)
POOL_GROUP_DIM = 256
POOL_HALO = 16
MLA_HEADS = 16
MLA_NOPE = 128
MLA_ROPE = 64
MLA_Q_RANK = 256
MLA_KV_RANK = 128
MLA_WIDTH = 2048
ODD_IN = 2496
ODD_IN_PAD = 2560
ROPE_THETA = 10000.0
ATTN_SCALE = (MLA_NOPE + MLA_ROPE) ** -0.5
ATTN_SCALE_LOG2 = ATTN_SCALE * 1.4426950408889634
DEEPNORM_ALPHA = 4.0 ** 0.25
ADAM_LR, ADAM_B1, ADAM_B2, ADAM_EPS, ADAM_WD, ADAM_STEP = 0.001, 0.9, 0.999, 1e-8, 0.01, 10
N_DEV = 8
LANES = 1024
VMEM_LIMIT = 56 * 1024 * 1024
MESH = pl.DeviceIdType.MESH

NT = (((1,), (1,)), ((), ()))
NN = (((1,), (0,)), ((), ()))
TN = (((0,), (0,)), ((), ()))


def _params(n_axes):
    return pltpu.CompilerParams(dimension_semantics=("arbitrary",) * n_axes, vmem_limit_bytes=VMEM_LIMIT)


def _dot(a, b, dn):
    return lax.dot_general(a.astype(BF16), b.astype(BF16), dn, preferred_element_type=F32)


def _sigmoid(z):
    return 1.0 / (1.0 + jnp.exp(-z))


def _colsum(t):
    return jnp.sum(t, axis=0, keepdims=True)


def _exchange_copies(g_refs, r_refs, send_sems, recv_sems, local_sems, gather):
    x, y, c, me = _place()
    n_arr = len(g_refs)

    def src(t, slot):
        return g_refs[t] if gather else g_refs[t].at[slot]

    own = [pltpu.make_async_copy(src(t, me), r_refs[t].at[me], local_sems.at[t]) for t in range(n_arr)]
    sends, recvs = [], []
    for r in range(1, N_DEV):
        peer, lin = _flip(x, y, c, r)
        for t in range(n_arr):
            k = n_arr * (r - 1) + t
            sends.append(pltpu.make_async_remote_copy(
                src_ref=src(t, lin), dst_ref=r_refs[t].at[me], send_sem=send_sems.at[k], recv_sem=recv_sems.at[k],
                device_id=peer, device_id_type=MESH))
            recvs.append(pltpu.make_async_remote_copy(
                src_ref=src(t, lin), dst_ref=r_refs[t].at[lin], send_sem=send_sems.at[k], recv_sem=recv_sems.at[k],
                device_id=(x, y, c), device_id_type=MESH))
    return own, sends, recvs


def _exchange_start(copies):
    own, sends, _ = copies
    for cp in own + sends:
        cp.start()


def _exchange_wait(copies):
    own, sends, recvs = copies
    for cp in recvs:
        cp.wait_recv()
    for cp in sends:
        cp.wait_send()
    for cp in own:
        cp.wait()


def _exchange_extras(parts, gather):
    shapes = [jax.ShapeDtypeStruct(((N_DEV,) + p.shape) if gather else p.shape, p.dtype) for p in parts]
    n = len(parts) * (N_DEV - 1)
    return shapes, [pltpu.SemaphoreType.DMA((n,)), pltpu.SemaphoreType.DMA((n,)), pltpu.SemaphoreType.DMA((len(parts),))]


def _grid_call(body, name, grid, in_specs, out_specs, out_shape, args, scratch=(), side=None):
    if side is None:
        return pl.pallas_call(body, name=name, grid=grid, in_specs=in_specs, out_specs=out_specs,
                              out_shape=out_shape, scratch_shapes=list(scratch),
                              compiler_params=_params(len(grid)))(*args)
    parts, gather = side
    n_in, n_out, n_sc, n_arr = len(args), len(out_shape), len(scratch), len(parts)
    side_shapes, side_sems = _exchange_extras(parts, gather)

    def wrapped(*refs):
        ins, g_refs = refs[:n_in], refs[n_in:n_in + n_arr]
        outs = refs[n_in + n_arr:n_in + n_arr + n_out]
        r_refs = refs[n_in + n_arr + n_out:n_in + 2 * n_arr + n_out]
        sc = refs[n_in + 2 * n_arr + n_out:n_in + 2 * n_arr + n_out + n_sc]
        copies = _exchange_copies(g_refs, r_refs, *refs[-3:], gather)
        ids = [pl.program_id(a) for a in range(len(grid))]
        first = functools.reduce(jnp.logical_and, [i == 0 for i in ids])
        last = functools.reduce(jnp.logical_and, [i == g - 1 for i, g in zip(ids, grid)])

        @pl.when(first)
        def _():
            _exchange_start(copies)

        body(*ins, *outs, *sc)

        @pl.when(last)
        def _():
            _exchange_wait(copies)

    anyspace = pl.BlockSpec(memory_space=pl.ANY)
    return pl.pallas_call(
        wrapped, name=name, grid=grid, in_specs=list(in_specs) + [anyspace] * n_arr,
        out_specs=list(out_specs) + [anyspace] * n_arr, out_shape=list(out_shape) + side_shapes,
        scratch_shapes=list(scratch) + side_sems, compiler_params=_params(len(grid)),
    )(*args, *parts)


def _matmul(pairs, mode, out_dtype, m, n, tm, tn, name, side=None, n_outer=False):
    dn = {"nn": NN, "nt": NT, "tn": TN}[mode]
    tm, tn = min(tm, m), min(tn, n)
    n_pairs = len(pairs)
    grid = (n // tn, m // tm) if n_outer else (m // tm, n // tn)

    def ij(f):
        return (lambda j, i: f(i, j)) if n_outer else f

    def body(*refs):
        o_ref = refs[-1]
        acc = None
        for p in range(n_pairs):
            t = _dot(refs[2 * p][...], refs[2 * p + 1][...], dn)
            acc = t if acc is None else acc + t
        o_ref[...] = acc.astype(o_ref.dtype)

    in_specs, args = [], []
    for a, b in pairs:
        if mode == "nn":
            k = a.shape[1]
            in_specs += [pl.BlockSpec((tm, k), ij(lambda i, j: (i, 0))), pl.BlockSpec((k, tn), ij(lambda i, j: (0, j)))]
        elif mode == "nt":
            k = a.shape[1]
            in_specs += [pl.BlockSpec((tm, k), ij(lambda i, j: (i, 0))), pl.BlockSpec((tn, k), ij(lambda i, j: (j, 0)))]
        else:
            k = a.shape[0]
            in_specs += [pl.BlockSpec((k, tm), ij(lambda i, j: (0, i))), pl.BlockSpec((k, tn), ij(lambda i, j: (0, j)))]
        args += [a, b]
    return _grid_call(body, name, grid, in_specs, [pl.BlockSpec((tm, tn), ij(lambda i, j: (i, j)))],
                      [jax.ShapeDtypeStruct((m, n), out_dtype)], args, side=side)


def _matmul_cols_nn(a, w3, out_dtype, tm, name, side=None):
    m, k = a.shape
    _, _, n = w3.shape
    tm = min(tm, m)

    def body(a_ref, w_ref, o_ref):
        av = a_ref[...]
        for j in range(N_DEV):
            o_ref[:, n * j:n * (j + 1)] = _dot(av, w_ref[j], NN).astype(o_ref.dtype)

    return _grid_call(
        body, name, (m // tm,),
        [pl.BlockSpec((tm, k), lambda i: (i, 0)), pl.BlockSpec((N_DEV, k, n), lambda i: (0, 0, 0))],
        [pl.BlockSpec((tm, N_DEV * n), lambda i: (i, 0))], [jax.ShapeDtypeStruct((m, N_DEV * n), out_dtype)], [a, w3],
        side=side)


def _matmul_cols_nt(a, w3, out_dtype, tm, name, side=None):
    m = a.shape[0]
    _, k, n = w3.shape
    tm = min(tm, m)

    def body(a_ref, w_ref, o_ref):
        acc = _dot(a_ref[:, 0:n], w_ref[0], NT)
        for j in range(1, N_DEV):
            acc = acc + _dot(a_ref[:, n * j:n * (j + 1)], w_ref[j], NT)
        o_ref[...] = acc.astype(o_ref.dtype)

    return _grid_call(
        body, name, (m // tm,),
        [pl.BlockSpec((tm, N_DEV * n), lambda i: (i, 0)), pl.BlockSpec((N_DEV, k, n), lambda i: (0, 0, 0))],
        [pl.BlockSpec((tm, k), lambda i: (i, 0))], [jax.ShapeDtypeStruct((m, k), out_dtype)], [a, w3], side=side)


def _matmul_cols_tn(a, b, n, out_dtype, tk, name, side=None):
    m, k = a.shape
    tk = min(tk, k)

    def body(a_ref, b_ref, o_ref):
        o_ref[...] = _dot(a_ref[...], b_ref[...], TN).astype(o_ref.dtype)

    return _grid_call(
        body, name, (N_DEV, k // tk),
        [pl.BlockSpec((m, tk), lambda j, i: (0, i)), pl.BlockSpec((m, n), lambda j, i: (0, j))],
        [pl.BlockSpec((None, tk, n), lambda j, i: (j, i, 0))], [jax.ShapeDtypeStruct((N_DEV, k, n), out_dtype)], [a, b],
        side=side)


def _rows3(tm, d):
    return pl.BlockSpec((None, tm, d), lambda i: (0, i, 0))


def _modulate(x, mod, name):
    _, s, d = x.shape
    tm = min(s, 512)

    def body(x_ref, m_ref, h_ref):
        shift, scale = m_ref[0:1, 0:d], m_ref[0:1, d:2 * d]
        h_ref[...] = (x_ref[...] * (1.0 + scale) + shift).astype(BF16)

    return pl.pallas_call(
        body, name=name, grid=(s // tm,),
        in_specs=[_rows3(tm, d), pl.BlockSpec((8, 3 * d), lambda i: (0, 0))],
        out_specs=pl.BlockSpec((tm, d), lambda i: (i, 0)),
        out_shape=jax.ShapeDtypeStruct((s, d), BF16), compiler_params=_params(1),
    )(x, mod)


def _ln_stats(r):
    mu = jnp.mean(r, axis=-1, keepdims=True)
    rc = r - mu
    var = jnp.mean(rc * rc, axis=-1, keepdims=True)
    rstd = lax.rsqrt(var + LN_EPS)
    return rc * rstd, rstd


def _ln_bwd(dxhat, xhat, rstd):
    return rstd * (dxhat - jnp.mean(dxhat, axis=-1, keepdims=True)
                   - xhat * jnp.mean(dxhat * xhat, axis=-1, keepdims=True))


def _resid_ln(x, y, mod, g, b, name):
    _, s, d = x.shape
    tm = min(s, 512)

    def body(x_ref, y_ref, m_ref, g_ref, b_ref, o_ref):
        gate = m_ref[0:1, 2 * d:3 * d]
        xhat, _ = _ln_stats(DEEPNORM_ALPHA * x_ref[...] + (1.0 + gate) * y_ref[...])
        o_ref[...] = xhat * g_ref[...] + b_ref[...]

    row = pl.BlockSpec((tm, d), lambda i: (i, 0))
    vec = pl.BlockSpec((1, d), lambda i: (0, 0))
    return pl.pallas_call(
        body, name=name, grid=(s // tm,),
        in_specs=[_rows3(tm, d), row, pl.BlockSpec((8, 3 * d), lambda i: (0, 0)), vec, vec],
        out_specs=_rows3(tm, d), out_shape=jax.ShapeDtypeStruct((1, s, d), F32), compiler_params=_params(1),
    )(x, y, mod, g, b)


def _final_ln_loss_bwd(x, y, mod, g, b, target, name):
    _, s, d = x.shape
    tm = min(s, 256)

    def body(x_ref, y_ref, m_ref, g_ref, b_ref, t_ref, dy_ref, dx_ref, red_ref):
        @pl.when(pl.program_id(0) == 0)
        def _():
            red_ref[...] = jnp.zeros_like(red_ref)

        gate = m_ref[0:1, 2 * d:3 * d]
        yv = y_ref[...]
        xhat, rstd = _ln_stats(DEEPNORM_ALPHA * x_ref[...] + (1.0 + gate) * yv)
        err = xhat * g_ref[...] + b_ref[...] - t_ref[...]
        dout = err * (1.0 / d)
        dr = _ln_bwd(dout * g_ref[...], xhat, rstd)
        dy_ref[...] = ((1.0 + gate) * dr).astype(BF16)
        dx_ref[...] = DEEPNORM_ALPHA * dr
        red_ref[0:1, :] += _colsum(dout * xhat)
        red_ref[1:2, :] += _colsum(dout)
        red_ref[2:3, :] += _colsum(dr * yv)
        red_ref[3:4, :] += _colsum(err * err)

    row = pl.BlockSpec((tm, d), lambda i: (i, 0))
    vec = pl.BlockSpec((1, d), lambda i: (0, 0))
    return pl.pallas_call(
        body, name=name, grid=(s // tm,),
        in_specs=[_rows3(tm, d), row, pl.BlockSpec((8, 3 * d), lambda i: (0, 0)), vec, vec, _rows3(tm, d)],
        out_specs=[row, row, pl.BlockSpec((8, d), lambda i: (0, 0))],
        out_shape=[jax.ShapeDtypeStruct((s, d), BF16), jax.ShapeDtypeStruct((s, d), F32),
                   jax.ShapeDtypeStruct((8, d), F32)],
        compiler_params=_params(1),
    )(x, y, mod, g, b, target)


def _mid_bwd(dh, dxres, x, y, mod_lo, mod_hi, g, b, name):
    _, s, d = x.shape
    tm = min(s, 256)

    def body(dh_ref, dxr_ref, x_ref, y_ref, ml_ref, mh_ref, g_ref, b_ref, dy_ref, dx_ref, red_ref):
        @pl.when(pl.program_id(0) == 0)
        def _():
            red_ref[...] = jnp.zeros_like(red_ref)

        gate = ml_ref[0:1, 2 * d:3 * d]
        scale_hi = mh_ref[0:1, d:2 * d]
        yv, dhv = y_ref[...], dh_ref[...]
        xhat, rstd = _ln_stats(DEEPNORM_ALPHA * x_ref[...] + (1.0 + gate) * yv)
        x_mid = xhat * g_ref[...] + b_ref[...]
        dx_mid = dxr_ref[...] + dhv * (1.0 + scale_hi)
        dr = _ln_bwd(dx_mid * g_ref[...], xhat, rstd)
        dy_ref[...] = ((1.0 + gate) * dr).astype(BF16)
        dx_ref[...] = DEEPNORM_ALPHA * dr
        red_ref[0:1, :] += _colsum(dhv * x_mid)
        red_ref[1:2, :] += _colsum(dhv)
        red_ref[2:3, :] += _colsum(dx_mid * xhat)
        red_ref[3:4, :] += _colsum(dx_mid)
        red_ref[4:5, :] += _colsum(dr * yv)

    row = pl.BlockSpec((tm, d), lambda i: (i, 0))
    vec = pl.BlockSpec((1, d), lambda i: (0, 0))
    modspec = pl.BlockSpec((8, 3 * d), lambda i: (0, 0))
    return pl.pallas_call(
        body, name=name, grid=(s // tm,),
        in_specs=[row, row, _rows3(tm, d), row, modspec, modspec, vec, vec],
        out_specs=[row, row, pl.BlockSpec((8, d), lambda i: (0, 0))],
        out_shape=[jax.ShapeDtypeStruct((s, d), BF16), jax.ShapeDtypeStruct((s, d), F32),
                   jax.ShapeDtypeStruct((8, d), F32)],
        compiler_params=_params(1),
    )(dh, dxres, x, y, mod_lo, mod_hi, g, b)


def _first_bwd(dh, dxres, x, mod, name):
    _, s, d = x.shape
    tm = min(s, 512)

    def body(dh_ref, dxr_ref, x_ref, m_ref, gx_ref, red_ref):
        @pl.when(pl.program_id(0) == 0)
        def _():
            red_ref[...] = jnp.zeros_like(red_ref)

        scale = m_ref[0:1, d:2 * d]
        dhv = dh_ref[...]
        gx_ref[...] = dxr_ref[...] + dhv * (1.0 + scale)
        red_ref[0:1, :] += _colsum(dhv * x_ref[...])
        red_ref[1:2, :] += _colsum(dhv)

    row = pl.BlockSpec((tm, d), lambda i: (i, 0))
    return pl.pallas_call(
        body, name=name, grid=(s // tm,),
        in_specs=[row, row, _rows3(tm, d), pl.BlockSpec((8, 3 * d), lambda i: (0, 0))],
        out_specs=[_rows3(tm, d), pl.BlockSpec((8, d), lambda i: (0, 0))],
        out_shape=[jax.ShapeDtypeStruct((1, s, d), F32), jax.ShapeDtypeStruct((8, d), F32)],
        compiler_params=_params(1),
    )(dh, dxres, x, mod)


EVEN_TM = 256


def _gmlp_mask():
    t = lax.broadcasted_iota(jnp.int32, (GMLP_BLOCK, GMLP_BLOCK), 0) // CHUNK
    s = lax.broadcasted_iota(jnp.int32, (GMLP_BLOCK, GMLP_BLOCK), 1) // CHUNK
    return s <= t


def _window_sum(ext, win, back):
    n = ext.shape[0]
    k = 1
    while k < win:
        ext = ext + pltpu.roll(ext, k if back else n - k, 0)
        k *= 2
    return ext


def _inv_count(row0, rows, win):
    t = row0 + lax.broadcasted_iota(jnp.int32, (rows, 1), 0)
    return t, 1.0 / jnp.minimum(t + 1, win).astype(F32)


def _pooled(xb, halo, row0, win):
    tm = xb.shape[0]
    sums = _window_sum(jnp.concatenate([halo, xb], axis=0), win, True)[POOL_HALO:]
    _, inv = _inv_count(row0, tm, win)
    return sums * inv - xb


def _even_fwd(proj, ws, bs_col, ng, nb, pw, pb, ps, name):
    s = proj.shape[0]
    tm = min(s, EVEN_TM)
    hd, gd = GMLP_HEAD_DIM, POOL_GROUP_DIM

    def body(p_ref, halo_ref, ws_ref, bs_ref, ng_ref, nb_ref, pw_ref, pb_ref, ps_ref, m_ref):
        i = pl.program_id(0)
        mask = _gmlp_mask()
        for h in range(GMLP_HEADS):
            wm = jnp.where(mask, ws_ref[h], 0.0).astype(BF16)
            for blk in range(tm // GMLP_BLOCK):
                rows = slice(blk * GMLP_BLOCK, (blk + 1) * GMLP_BLOCK)
                cu, cv, cz = h * hd, 1024 + h * hd, 2048 + h * hd
                vhat, _ = _ln_stats(p_ref[rows, cv:cv + hd].astype(F32))
                vn = vhat * ng_ref[...] + nb_ref[...]
                sv = _dot(wm, vn, NN) + bs_ref[h]
                za = p_ref[rows, cz:cz + hd].astype(F32)
                m_ref[rows, cu:cu + hd] = (p_ref[rows, cu:cu + hd].astype(F32) * sv * (za * _sigmoid(za))).astype(BF16)
        for g, win in enumerate(POOL_WINDOWS):
            cx, cz = 3072 + g * gd, 4096 + g * gd
            halo = jnp.where(i > 0, halo_ref[:, g * gd:(g + 1) * gd].astype(F32), 0.0)
            pooled = _pooled(p_ref[:, cx:cx + gd].astype(F32), halo, i * tm, win)
            yb = _dot(pooled, pw_ref[g], NN) + pb_ref[:, g * gd:(g + 1) * gd]
            zb = p_ref[:, cz:cz + gd].astype(F32)
            m_ref[:, 1024 + g * gd:1024 + (g + 1) * gd] = (
                yb * ps_ref[:, g * gd:(g + 1) * gd] * (zb * _sigmoid(zb))).astype(BF16)

    hb = tm // POOL_HALO
    return pl.pallas_call(
        body, name=name, grid=(s // tm,),
        in_specs=[
            pl.BlockSpec((tm, 5120), lambda i: (i, 0)),
            pl.BlockSpec((POOL_HALO, 1024), lambda i: (jnp.maximum(i * hb - 1, 0), 3)),
            pl.BlockSpec((GMLP_HEADS, GMLP_BLOCK, GMLP_BLOCK), lambda i: (0, 0, 0)),
            pl.BlockSpec((GMLP_HEADS, GMLP_BLOCK, 1), lambda i: (0, 0, 0)),
            pl.BlockSpec((1, hd), lambda i: (0, 0)), pl.BlockSpec((1, hd), lambda i: (0, 0)),
            pl.BlockSpec((4, gd, gd), lambda i: (0, 0, 0)),
            pl.BlockSpec((1, 1024), lambda i: (0, 0)), pl.BlockSpec((1, 1024), lambda i: (0, 0)),
        ],
        out_specs=pl.BlockSpec((tm, 2048), lambda i: (i, 0)),
        out_shape=jax.ShapeDtypeStruct((s, 2048), BF16), compiler_params=_params(1),
    )(proj, proj, ws, bs_col, ng, nb, pw, pb, ps)


def _even_bwd(proj, dm, ws, bs_col, ng, nb, pw, pb, ps, name, side=None):
    s = proj.shape[0]
    tm = min(s, EVEN_TM)
    hd, gd = GMLP_HEAD_DIM, POOL_GROUP_DIM
    n_tiles = s // tm

    def body(p_ref, halo_ref, zbn_ref, dm_ref, dbn_ref, ws_ref, bs_ref, ng_ref, nb_ref, pw_ref, pb_ref, ps_ref,
             dp_ref, dws_ref, dbs_ref, dng_ref, dnb_ref, dpw_ref, dpb_ref, dps_ref):
        i = pl.program_id(0)

        @pl.when(i == 0)
        def _():
            for r in (dws_ref, dbs_ref, dng_ref, dnb_ref, dpw_ref, dpb_ref, dps_ref):
                r[...] = jnp.zeros_like(r)

        mask = _gmlp_mask()
        for h in range(GMLP_HEADS):
            wm = jnp.where(mask, ws_ref[h], 0.0).astype(BF16)
            for blk in range(tm // GMLP_BLOCK):
                rows = slice(blk * GMLP_BLOCK, (blk + 1) * GMLP_BLOCK)
                cu, cv, cz = h * hd, 1024 + h * hd, 2048 + h * hd
                vhat, rstd = _ln_stats(p_ref[rows, cv:cv + hd].astype(F32))
                vn = (vhat * ng_ref[...] + nb_ref[...]).astype(BF16)
                sv = _dot(wm, vn, NN) + bs_ref[h]
                u, za = p_ref[rows, cu:cu + hd].astype(F32), p_ref[rows, cz:cz + hd].astype(F32)
                da = dm_ref[rows, cu:cu + hd].astype(F32)
                sig = _sigmoid(za)
                sa = za * sig
                dau = da * u
                dsv = dau * sa
                dp_ref[rows, cu:cu + hd] = (da * sv * sa).astype(BF16)
                dp_ref[rows, cz:cz + hd] = (dau * sv * (sig * (1.0 + za * (1.0 - sig)))).astype(BF16)
                dsv_b = dsv.astype(BF16)
                dbs_ref[h] += jnp.sum(dsv, axis=1, keepdims=True)
                dws_ref[h] += _dot(dsv_b, vn, NT)
                dvn = _dot(wm, dsv_b, TN)
                dng_ref[...] += _colsum(dvn * vhat)
                dnb_ref[...] += _colsum(dvn)
                dp_ref[rows, cv:cv + hd] = _ln_bwd(dvn * ng_ref[...], vhat, rstd).astype(BF16)

        row0 = i * tm
        for g, win in enumerate(POOL_WINDOWS):
            cx, cz, cd = 3072 + g * gd, 4096 + g * gd, 1024 + g * gd
            gs = slice(g * gd, (g + 1) * gd)
            halo = jnp.where(i > 0, halo_ref[:, gs].astype(F32), 0.0)
            xb = p_ref[:, cx:cx + gd].astype(F32)
            pooled = _pooled(xb, halo, row0, win).astype(BF16)
            scale_g = ps_ref[:, gs]
            yb = _dot(pooled, pw_ref[g], NN) + pb_ref[:, gs]
            zb, db = p_ref[:, cz:cz + gd].astype(F32), dm_ref[:, cd:cd + gd].astype(F32)
            sig = _sigmoid(zb)
            dyp = db * (zb * sig)
            dp_ref[:, cz:cz + gd] = (db * yb * scale_g * (sig * (1.0 + zb * (1.0 - sig)))).astype(BF16)
            dps_ref[:, gs] += _colsum(dyp * yb)
            dpb_ref[:, gs] += _colsum(dyp * scale_g)
            zb_ext = jnp.concatenate([zb, zbn_ref[:, gs].astype(F32)], axis=0)
            db_ext = jnp.concatenate([db, dbn_ref[:, gs].astype(F32)], axis=0)
            dy_ext = (db_ext * (zb_ext * _sigmoid(zb_ext)) * scale_g).astype(BF16)
            dpw_ref[g] += _dot(pooled, dy_ext[:tm], TN)
            dpooled = _dot(dy_ext, pw_ref[g], NT)
            t, inv = _inv_count(row0, tm + POOL_HALO, win)
            w_ext = jnp.where(t < s, dpooled * inv, 0.0)
            dp_ref[:, cx:cx + gd] = (_window_sum(w_ext, win, False)[:tm] - dpooled[:tm]).astype(BF16)

    hb = tm // POOL_HALO
    last = s // POOL_HALO - 1
    small = lambda shape: pl.BlockSpec(shape, lambda i: (0,) * len(shape))
    return _grid_call(
        body, name, (n_tiles,),
        [
            pl.BlockSpec((tm, 5120), lambda i: (i, 0)),
            pl.BlockSpec((POOL_HALO, 1024), lambda i: (jnp.maximum(i * hb - 1, 0), 3)),
            pl.BlockSpec((POOL_HALO, 1024), lambda i: (jnp.minimum((i + 1) * hb, last), 4)),
            pl.BlockSpec((tm, 2048), lambda i: (i, 0)),
            pl.BlockSpec((POOL_HALO, 1024), lambda i: (jnp.minimum((i + 1) * hb, last), 1)),
            small((GMLP_HEADS, GMLP_BLOCK, GMLP_BLOCK)), small((GMLP_HEADS, GMLP_BLOCK, 1)),
            small((1, hd)), small((1, hd)), small((4, gd, gd)), small((1, 1024)), small((1, 1024)),
        ],
        [
            pl.BlockSpec((tm, 5120), lambda i: (i, 0)),
            small((GMLP_HEADS, GMLP_BLOCK, GMLP_BLOCK)), small((GMLP_HEADS, GMLP_BLOCK, 1)),
            small((1, hd)), small((1, hd)), small((4, gd, gd)), small((1, 1024)), small((1, 1024)),
        ],
        [
            jax.ShapeDtypeStruct((s, 5120), BF16),
            jax.ShapeDtypeStruct((GMLP_HEADS, GMLP_BLOCK, GMLP_BLOCK), F32),
            jax.ShapeDtypeStruct((GMLP_HEADS, GMLP_BLOCK, 1), F32),
            jax.ShapeDtypeStruct((1, hd), F32), jax.ShapeDtypeStruct((1, hd), F32),
            jax.ShapeDtypeStruct((4, gd, gd), F32),
            jax.ShapeDtypeStruct((1, 1024), F32), jax.ShapeDtypeStruct((1, 1024), F32),
        ],
        [proj, proj, proj, dm, dm, ws, bs_col, ng, nb, pw, pb, ps], side=side)


def _rope_pair_swap(t):
    lane = lax.broadcasted_iota(jnp.int32, t.shape, 1)
    return jnp.where(lane % 64 < 32, pltpu.roll(t, 96, 1), pltpu.roll(t, 32, 1))


def _rms(x, g):
    r = lax.rsqrt(jnp.mean(x * x, axis=-1, keepdims=True) + LN_EPS)
    return x * r, r


def _rms_bwd(dy, g, xhat, r):
    dyg = dy * g
    return r * (dyg - xhat * jnp.mean(dyg * xhat, axis=-1, keepdims=True))


def _lane_lt(shape, n):
    return lax.broadcasted_iota(jnp.int32, shape, 1) < n


def _mla_prep(proj, cosp, sinp, gq, gkv, name):
    s = proj.shape[0]
    tm = min(s, 512)

    def body(qc_ref, kv_ref, c_ref, s_ref, gq_ref, gkv_ref, qn_ref, kp_ref):
        qhat, _ = _rms(qc_ref[...].astype(F32), None)
        qn_ref[...] = (qhat * gq_ref[...]).astype(BF16)
        khat, _ = _rms(kv_ref[:, 0:128].astype(F32), None)
        kp_ref[:, 0:128] = (khat * gkv_ref[...]).astype(BF16)
        kr = kv_ref[:, 128:256].astype(F32)
        kp_ref[:, 128:256] = (kr * c_ref[...] + _rope_pair_swap(kr) * s_ref[...]).astype(BF16)

    return pl.pallas_call(
        body, name=name, grid=(s // tm,),
        in_specs=[pl.BlockSpec((tm, 256), lambda i: (i, 0)), pl.BlockSpec((tm, 256), lambda i: (i, 1)),
                  pl.BlockSpec((tm, 128), lambda i: (i, 0)), pl.BlockSpec((tm, 128), lambda i: (i, 0)),
                  pl.BlockSpec((1, 256), lambda i: (0, 0)), pl.BlockSpec((1, 128), lambda i: (0, 0))],
        out_specs=[pl.BlockSpec((tm, 256), lambda i: (i, 0)), pl.BlockSpec((tm, 256), lambda i: (i, 0))],
        out_shape=[jax.ShapeDtypeStruct((s, 256), BF16), jax.ShapeDtypeStruct((s, 256), BF16)],
        compiler_params=_params(1),
    )(proj, proj, cosp, sinp, gq, gkv)


def _mla_prep_bwd(proj, dqn, dkp, dv, cosp, sinp, gq, gkv, dproj, name):
    s = proj.shape[0]
    tm = min(s, 512)

    def body(qc_ref, kv_ref, dqn_ref, dkp_ref, dv_ref, c_ref, s_ref, gq_ref, gkv_ref, dproj_ref, o_ref, red_ref):
        @pl.when(pl.program_id(0) == 0)
        def _():
            red_ref[...] = jnp.zeros_like(red_ref)

        qhat, qr = _rms(qc_ref[...].astype(F32), None)
        dq = dqn_ref[...]
        o_ref[:, 0:256] = _rms_bwd(dq, gq_ref[...], qhat, qr).astype(BF16)
        red_ref[0:1, :] += _colsum(dq * qhat)
        khat, kr = _rms(kv_ref[:, 0:128].astype(F32), None)
        dk = dkp_ref[:, 0:128] + dv_ref[...]
        o_ref[:, 256:384] = _rms_bwd(dk, gkv_ref[...], khat, kr).astype(BF16)
        red_ref[1:2, 0:128] += _colsum(dk * khat)
        dr = dkp_ref[:, 128:256]
        o_ref[:, 384:512] = (dr * c_ref[...] - _rope_pair_swap(dr) * s_ref[...]).astype(BF16)

    return pl.pallas_call(
        body, name=name, grid=(s // tm,),
        in_specs=[pl.BlockSpec((tm, 256), lambda i: (i, 0)), pl.BlockSpec((tm, 256), lambda i: (i, 1)),
                  pl.BlockSpec((tm, 256), lambda i: (i, 0)), pl.BlockSpec((tm, 256), lambda i: (i, 0)),
                  pl.BlockSpec((tm, 128), lambda i: (i, 0)),
                  pl.BlockSpec((tm, 128), lambda i: (i, 0)), pl.BlockSpec((tm, 128), lambda i: (i, 0)),
                  pl.BlockSpec((1, 256), lambda i: (0, 0)), pl.BlockSpec((1, 128), lambda i: (0, 0)),
                  pl.BlockSpec(memory_space=pl.ANY)],
        out_specs=[pl.BlockSpec((tm, 512), lambda i: (i, 0)), pl.BlockSpec((8, 256), lambda i: (0, 0))],
        out_shape=[jax.ShapeDtypeStruct(dproj.shape, BF16), jax.ShapeDtypeStruct((8, 256), F32)],
        input_output_aliases={9: 0}, compiler_params=_params(1),
    )(proj, proj, dqn, dkp, dv, cosp, sinp, gq, gkv, dproj)


HEAD_PAIRS = MLA_HEADS // 2
PAIR_TM = 512


def _q_heads(q_up, cosp, sinp, wuk, name):
    s = q_up.shape[0]
    tm = min(s, PAIR_TM)

    def body(qn_ref, qr_ref, c_ref, s_ref, w_ref, o_ref):
        raw = qr_ref[...].astype(F32)
        rot = raw * c_ref[...] + _rope_pair_swap(raw) * s_ref[...]
        low = _lane_lt(rot.shape, 64)
        o_ref[0, :, 0:128] = _dot(qn_ref[:, 0:128], w_ref[:, 0:128], NT).astype(BF16)
        o_ref[0, :, 128:256] = jnp.where(low, rot, 0.0).astype(BF16)
        o_ref[1, :, 0:128] = _dot(qn_ref[:, 128:256], w_ref[:, 128:256], NT).astype(BF16)
        o_ref[1, :, 128:256] = jnp.where(low, pltpu.roll(rot, 64, 1), 0.0).astype(BF16)

    return pl.pallas_call(
        body, name=name, grid=(s // tm, HEAD_PAIRS),
        in_specs=[pl.BlockSpec((tm, 256), lambda i, p: (i, p)),
                  pl.BlockSpec((tm, 128), lambda i, p: (i, 16 + p)),
                  pl.BlockSpec((tm, 128), lambda i, p: (i, 0)), pl.BlockSpec((tm, 128), lambda i, p: (i, 0)),
                  pl.BlockSpec((128, 256), lambda i, p: (0, p))],
        out_specs=pl.BlockSpec((2, tm, 256), lambda i, p: (p, i, 0)),
        out_shape=jax.ShapeDtypeStruct((MLA_HEADS, s, 256), BF16), compiler_params=_params(2),
    )(q_up, q_up, cosp, sinp, wuk)


def _q_heads_bwd(dqp, q_up, cosp, sinp, wuk, name):
    s = q_up.shape[0]
    tm = min(s, PAIR_TM)

    def body(dq_ref, qn_ref, c_ref, s_ref, w_ref, dn_ref, dr_ref, dw_ref):
        @pl.when(pl.program_id(1) == 0)
        def _():
            dw_ref[...] = jnp.zeros_like(dw_ref)

        for a in range(2):
            dlat = dq_ref[a, :, 0:128]
            cols = slice(128 * a, 128 * a + 128)
            dn_ref[:, cols] = _dot(dlat, w_ref[:, cols], NN).astype(BF16)
            dw_ref[:, cols] += _dot(dlat, qn_ref[:, cols], TN)
        drot = dq_ref[0, :, 128:256].astype(F32) + pltpu.roll(dq_ref[1, :, 128:256].astype(F32), 64, 1)
        dr_ref[...] = (drot * c_ref[...] - _rope_pair_swap(drot) * s_ref[...]).astype(BF16)

    return pl.pallas_call(
        body, name=name, grid=(HEAD_PAIRS, s // tm),
        in_specs=[pl.BlockSpec((2, tm, 256), lambda p, i: (p, i, 0)),
                  pl.BlockSpec((tm, 256), lambda p, i: (i, p)),
                  pl.BlockSpec((tm, 128), lambda p, i: (i, 0)), pl.BlockSpec((tm, 128), lambda p, i: (i, 0)),
                  pl.BlockSpec((128, 256), lambda p, i: (0, p))],
        out_specs=[pl.BlockSpec((tm, 256), lambda p, i: (i, p)),
                   pl.BlockSpec((tm, 128), lambda p, i: (i, p)),
                   pl.BlockSpec((128, 256), lambda p, i: (0, p))],
        out_shape=[jax.ShapeDtypeStruct((s, MLA_WIDTH), BF16), jax.ShapeDtypeStruct((s, 1024), BF16),
                   jax.ShapeDtypeStruct((128, MLA_WIDTH), F32)],
        compiler_params=_params(2),
    )(dqp, q_up, cosp, sinp, wuk)


def _o_gate(o_lat, proj, wuv, name):
    s = o_lat.shape[1]
    tm = min(s, PAIR_TM)

    def body(ol_ref, z_ref, w_ref, g_ref):
        for a in range(2):
            cols = slice(128 * a, 128 * a + 128)
            z = z_ref[:, cols].astype(F32)
            g_ref[:, cols] = (_dot(ol_ref[a], w_ref[:, cols], NN) * (z * _sigmoid(z))).astype(BF16)

    return pl.pallas_call(
        body, name=name, grid=(s // tm, HEAD_PAIRS),
        in_specs=[pl.BlockSpec((2, tm, 128), lambda i, p: (p, i, 0)),
                  pl.BlockSpec((tm, 256), lambda i, p: (i, 2 + p)),
                  pl.BlockSpec((128, 256), lambda i, p: (0, p))],
        out_specs=pl.BlockSpec((tm, 256), lambda i, p: (i, p)),
        out_shape=jax.ShapeDtypeStruct((s, MLA_WIDTH), BF16), compiler_params=_params(2),
    )(o_lat, proj, wuv)


def _o_gate_bwd(dg, o_lat, proj, wuv, name):
    s = o_lat.shape[1]
    tm = min(s, PAIR_TM)

    def body(dg_ref, ol_ref, z_ref, w_ref, dz_ref, dol_ref, dw_ref):
        @pl.when(pl.program_id(1) == 0)
        def _():
            dw_ref[...] = jnp.zeros_like(dw_ref)

        for a in range(2):
            cols = slice(128 * a, 128 * a + 128)
            z, dgv, ol = z_ref[:, cols].astype(F32), dg_ref[:, cols].astype(F32), ol_ref[a]
            sig = _sigmoid(z)
            o = _dot(ol, w_ref[:, cols], NN)
            dz_ref[:, cols] = (dgv * o * (sig * (1.0 + z * (1.0 - sig)))).astype(BF16)
            do = (dgv * (z * sig)).astype(BF16)
            dol_ref[a] = _dot(do, w_ref[:, cols], NT).astype(BF16)
            dw_ref[:, cols] += _dot(ol, do, TN)

    return pl.pallas_call(
        body, name=name, grid=(HEAD_PAIRS, s // tm),
        in_specs=[pl.BlockSpec((tm, 256), lambda p, i: (i, p)),
                  pl.BlockSpec((2, tm, 128), lambda p, i: (p, i, 0)),
                  pl.BlockSpec((tm, 256), lambda p, i: (i, 2 + p)),
                  pl.BlockSpec((128, 256), lambda p, i: (0, p))],
        out_specs=[pl.BlockSpec((tm, 256), lambda p, i: (i, 2 + p)),
                   pl.BlockSpec((2, tm, 128), lambda p, i: (p, i, 0)),
                   pl.BlockSpec((128, 256), lambda p, i: (0, p))],
        out_shape=[jax.ShapeDtypeStruct((s, ODD_IN_PAD), BF16), jax.ShapeDtypeStruct((MLA_HEADS, s, 128), BF16),
                   jax.ShapeDtypeStruct((128, MLA_WIDTH), F32)],
        compiler_params=_params(2),
    )(dg, o_lat, proj, wuv)


ATT_TQ = CHUNK
ATT_ROWS = ATT_TQ * MLA_HEADS
ATT_TK = 512
ATT_HEAD_GROUP = 4


def _visible(k0, q_chunk, tk):
    kpos = k0 + lax.broadcasted_iota(jnp.int32, (1, tk), 1)
    return kpos // CHUNK <= q_chunk


def _tile_lanes(t, n):
    return jnp.concatenate([t] * (n // 128), axis=1)


def _key_blocks(i, tk, block, pairs=False):
    n_full = (i * ATT_TQ + ATT_TQ + tk - 1) // tk - 1
    if pairs:
        def two(jj, carry):
            block(2 * jj, False)
            block(2 * jj + 1, False)
            return carry

        lax.fori_loop(0, n_full // 2, two, 0)

        @pl.when(n_full % 2 == 1)
        def _():
            block(n_full - 1, False)
    else:
        def one(j, carry):
            block(j, False)
            return carry

        lax.fori_loop(0, n_full, one, 0)
    block(n_full, True)


def _attn_fwd(qp, kp, name, side=None):
    s = kp.shape[0]
    tk = min(ATT_TK, s)

    def body(q_ref, k_ref, o_ref, lse_ref, m_sc, acc_sc):
        i = pl.program_id(0)
        m_sc[...] = jnp.full_like(m_sc, -jnp.inf)
        acc_sc[...] = jnp.zeros_like(acc_sc)

        def block(j, masked):
            k0 = pl.multiple_of(j * tk, tk)
            k = k_ref[pl.ds(k0, tk), :]
            v1 = jnp.where(_lane_lt(k.shape, 128), k, jnp.ones_like(k))
            for h0 in range(0, MLA_HEADS, ATT_HEAD_GROUP):
                rows = slice(h0 * ATT_TQ, (h0 + ATT_HEAD_GROUP) * ATT_TQ)
                q = q_ref[h0:h0 + ATT_HEAD_GROUP].reshape(ATT_HEAD_GROUP * ATT_TQ, 256)
                sc = _dot(q, k, NT) * ATTN_SCALE_LOG2
                if masked:
                    sc = jnp.where(_visible(k0, i, tk), sc, -jnp.inf)
                m_prev = m_sc[rows]
                m_new = jnp.maximum(m_prev, jnp.max(sc, axis=1, keepdims=True))
                p = jnp.exp2(sc - _tile_lanes(m_new, tk))
                acc_sc[rows] = _tile_lanes(jnp.exp2(m_prev - m_new), 256) * acc_sc[rows] + _dot(p, v1, NN)
                m_sc[rows] = m_new

        _key_blocks(i, tk, block, pairs=True)
        acc = acc_sc[...]
        l = acc[:, 128:256]
        o_ref[...] = (acc[:, 0:128] / l).astype(BF16).reshape(MLA_HEADS, ATT_TQ, 128)
        lse_ref[...] = (m_sc[...] + jnp.log2(l)).reshape(MLA_HEADS, ATT_TQ, 128)

    head128 = pl.BlockSpec((MLA_HEADS, ATT_TQ, 128), lambda i: (0, i, 0))
    return _grid_call(
        body, name, (s // ATT_TQ,),
        [pl.BlockSpec((MLA_HEADS, ATT_TQ, 256), lambda i: (0, i, 0)), pl.BlockSpec((s, 256), lambda i: (0, 0))],
        [head128, head128],
        [jax.ShapeDtypeStruct((MLA_HEADS, s, 128), BF16), jax.ShapeDtypeStruct((MLA_HEADS, s, 128), F32)],
        [qp, kp], scratch=[pltpu.VMEM((ATT_ROWS, 128), F32), pltpu.VMEM((ATT_ROWS, 256), F32)], side=side)


def _attn_bwd(qp, kp, o, do, lse, name, side=None):
    s = kp.shape[0]
    tk = min(ATT_TK, s)

    def body(q_ref, k_ref, o_ref, do_ref, lse_ref, dq_ref, dk_ref, dv_ref, dq_sc):
        i = pl.program_id(0)

        @pl.when(i == 0)
        def _():
            dk_ref[...] = jnp.zeros_like(dk_ref)
            dv_ref[...] = jnp.zeros_like(dv_ref)

        q = q_ref[...].reshape(ATT_ROWS, 256)
        dov = do_ref[...].reshape(ATT_ROWS, 128)
        delta = jnp.sum(dov.astype(F32) * o_ref[...].reshape(ATT_ROWS, 128).astype(F32), axis=1, keepdims=True)
        delta_t = _tile_lanes(jnp.broadcast_to(delta, (ATT_ROWS, 128)), tk)
        lse_t = _tile_lanes(lse_ref[...].reshape(ATT_ROWS, 128), tk)
        dq_sc[...] = jnp.zeros_like(dq_sc)

        def block(j, masked):
            k0 = pl.multiple_of(j * tk, tk)
            k = k_ref[pl.ds(k0, tk), :]
            p = jnp.exp2(_dot(q, k, NT) * ATTN_SCALE_LOG2 - lse_t)
            if masked:
                p = jnp.where(_visible(k0, i, tk), p, 0.0)
            dv_ref[pl.ds(k0, tk), :] += _dot(p, dov, TN)
            ds = (p * (_dot(dov, k[:, 0:128], NT) - delta_t) * ATTN_SCALE).astype(BF16)
            dq_sc[...] += _dot(ds, k, NN)
            dk_ref[pl.ds(k0, tk), :] += _dot(ds, q, TN)

        _key_blocks(i, tk, block, pairs=True)
        dq_ref[...] = dq_sc[...].astype(BF16).reshape(MLA_HEADS, ATT_TQ, 256)

    head128 = pl.BlockSpec((MLA_HEADS, ATT_TQ, 128), lambda i: (0, i, 0))
    head256 = pl.BlockSpec((MLA_HEADS, ATT_TQ, 256), lambda i: (0, i, 0))
    return _grid_call(
        body, name, (s // ATT_TQ,),
        [head256, pl.BlockSpec((s, 256), lambda i: (0, 0)), head128, head128, head128],
        [head256, pl.BlockSpec((s, 256), lambda i: (0, 0)), pl.BlockSpec((s, 128), lambda i: (0, 0))],
        [jax.ShapeDtypeStruct((MLA_HEADS, s, 256), BF16),
         jax.ShapeDtypeStruct((s, 256), F32), jax.ShapeDtypeStruct((s, 128), F32)],
        [qp, kp, o, do, lse], scratch=[pltpu.VMEM((ATT_ROWS, 256), F32)], side=side)


def _place():
    x, y, c = lax.axis_index("x"), lax.axis_index("y"), lax.axis_index("c")
    return x, y, c, 4 * x + 2 * y + c


def _flip(x, y, c, r):
    px = 1 - x if r & 4 else x
    py = 1 - y if r & 2 else y
    pc = 1 - c if r & 1 else c
    return (px, py, pc), 4 * px + 2 * py + pc


def _adaln_exchange(c8, ada_w, ada_b_cols, name):
    d = c8.shape[1]
    w_cols = ada_w.shape[2]

    def body(c_ref, w_ref, b_ref, call_ref, mod_ref, sbuf, rbuf, s1, r1, s2, r2):
        x, y, c, me = _place()
        call_ref[pl.ds(pl.multiple_of(me * 8, 8), 8), :] = c_ref[...]
        peers = [_flip(x, y, c, r) for r in range(1, N_DEV)]

        def c_copy(k, src_lin, to):
            rows = call_ref.at[pl.ds(pl.multiple_of(src_lin * 8, 8), 8), :]
            return pltpu.make_async_remote_copy(src_ref=rows, dst_ref=rows, send_sem=s1.at[k], recv_sem=r1.at[k],
                                                device_id=to, device_id_type=MESH)

        first = [c_copy(k, me, peer) for k, (peer, _) in enumerate(peers)]
        for cp in first:
            cp.start()
        for k, (_, lin) in enumerate(peers):
            c_copy(k, lin, (x, y, c)).wait_recv()
        for cp in first:
            cp.wait_send()

        for j in range(N_DEV):
            cj = call_ref[8 * j:8 * j + 8, :]
            cond = cj * _sigmoid(cj)
            for l in range(2):
                sbuf[j, l] = lax.dot_general(cond, w_ref[l], NN, precision=lax.Precision.HIGHEST,
                                             preferred_element_type=F32) + b_ref[l]

        def m_copy(k, src_slot, dst_slot, to):
            return pltpu.make_async_remote_copy(src_ref=sbuf.at[src_slot], dst_ref=rbuf.at[dst_slot],
                                                send_sem=s2.at[k], recv_sem=r2.at[k], device_id=to,
                                                device_id_type=MESH)

        rbuf[me] = sbuf[me]
        second = [m_copy(k, lin, me, peer) for k, (peer, lin) in enumerate(peers)]
        for cp in second:
            cp.start()
        for k, (_, lin) in enumerate(peers):
            m_copy(k, lin, lin, (x, y, c)).wait_recv()
        for cp in second:
            cp.wait_send()
        for j in range(N_DEV):
            for l in range(2):
                mod_ref[l, :, w_cols * j:w_cols * (j + 1)] = rbuf[j, l]

    vmem = pl.BlockSpec(memory_space=pltpu.VMEM)
    return pl.pallas_call(
        body, name=name, in_specs=[vmem, vmem, vmem], out_specs=[vmem, vmem],
        out_shape=[jax.ShapeDtypeStruct((8 * N_DEV, d), F32), jax.ShapeDtypeStruct((2, 8, 3 * d), F32)],
        scratch_shapes=[pltpu.VMEM((N_DEV, 2, 8, w_cols), F32), pltpu.VMEM((N_DEV, 2, 8, w_cols), F32),
                        pltpu.SemaphoreType.DMA((N_DEV - 1,)), pltpu.SemaphoreType.DMA((N_DEV - 1,)),
                        pltpu.SemaphoreType.DMA((N_DEV - 1,)), pltpu.SemaphoreType.DMA((N_DEV - 1,))],
        compiler_params=pltpu.CompilerParams(vmem_limit_bytes=VMEM_LIMIT),
    )(c8, ada_w, ada_b_cols)


def _all_gather(blocks, name):
    n_arr = len(blocks)

    def body(*refs):
        x_refs, out_refs = refs[:n_arr], refs[n_arr:2 * n_arr]
        send_sems, recv_sems, local_sems = refs[2 * n_arr:]
        x, y, c, _ = _place()
        me, sibling = (x, y, c), (x, y, 1 - c)
        chips = [(1 - x, y), (x, 1 - y), (1 - x, 1 - y)]

        def copy(t, k, blk, to, src=None):
            slot = out_refs[t].at[4 * blk[0] + 2 * blk[1] + blk[2]]
            return pltpu.make_async_remote_copy(src_ref=slot if src is None else src, dst_ref=slot,
                                                send_sem=send_sems.at[7 * t + k], recv_sem=recv_sems.at[7 * t + k],
                                                device_id=to, device_id_type=MESH)

        mine = [pltpu.make_async_copy(x_refs[t], out_refs[t].at[4 * x + 2 * y + c], local_sems.at[t])
                for t in range(n_arr)]
        for cp in mine:
            cp.start()
        first = []
        for t in range(n_arr):
            first.append(copy(t, 0, me, sibling, src=x_refs[t]))
            first += [copy(t, 1 + j, me, (*chip, c), src=x_refs[t]) for j, chip in enumerate(chips)]
        for cp in first:
            cp.start()
        passed = []
        for t in range(n_arr):
            for j, chip in enumerate(chips):
                copy(t, 1 + j, (*chip, c), me).wait_recv()
                passed.append(copy(t, 4 + j, (*chip, c), sibling))
                passed[-1].start()
        for t in range(n_arr):
            copy(t, 0, sibling, me).wait_recv()
            for j, chip in enumerate(chips):
                copy(t, 4 + j, (*chip, 1 - c), me).wait_recv()
        for cp in first + passed:
            cp.wait_send()
        for cp in mine:
            cp.wait()

    anyspace = pl.BlockSpec(memory_space=pl.ANY)
    return pl.pallas_call(
        body, name=name, in_specs=[anyspace] * n_arr, out_specs=[anyspace] * n_arr,
        out_shape=[jax.ShapeDtypeStruct((N_DEV,) + b.shape, b.dtype) for b in blocks],
        scratch_shapes=[pltpu.SemaphoreType.DMA((7 * n_arr,)), pltpu.SemaphoreType.DMA((7 * n_arr,)),
                        pltpu.SemaphoreType.DMA((n_arr,))],
    )(*blocks)


def _scatter_parts(parts, name):
    n_arr = len(parts)

    def body(*refs):
        copies = _exchange_copies(refs[:n_arr], refs[n_arr:2 * n_arr], *refs[2 * n_arr:], False)
        _exchange_start(copies)
        _exchange_wait(copies)

    anyspace = pl.BlockSpec(memory_space=pl.ANY)
    shapes, sems = _exchange_extras(parts, False)
    return pl.pallas_call(body, name=name, in_specs=[anyspace] * n_arr, out_specs=[anyspace] * n_arr,
                          out_shape=shapes, scratch_shapes=sems)(*parts)


def _adamw(w, g, m, v):
    m = ADAM_B1 * m + (1.0 - ADAM_B1) * g
    v = ADAM_B2 * v + (1.0 - ADAM_B2) * (g * g)
    m_hat = m / (1.0 - ADAM_B1 ** ADAM_STEP)
    v_hat = v / (1.0 - ADAM_B2 ** ADAM_STEP)
    return -ADAM_LR * (m_hat / (jnp.sqrt(v_hat) + ADAM_EPS) + ADAM_WD * w), m, v


def _sum_parts_adamw(parts, w, m, v, name):
    _, rows, cols = parts.shape
    tr = max(t for t in range(16, 129, 16) if rows % t == 0)

    def body(p_ref, w_ref, m_ref, v_ref, g_ref, d_ref, mo_ref, vo_ref):
        g = p_ref[0].astype(F32)
        for j in range(1, N_DEV):
            g = g + p_ref[j].astype(F32)
        g_ref[...] = g
        d_ref[...], mo_ref[...], vo_ref[...] = _adamw(w_ref[...], g, m_ref[...], v_ref[...])

    row = _rows3(tr, cols)
    out = jax.ShapeDtypeStruct((1, rows, cols), F32)
    return pl.pallas_call(
        body, name=name, grid=(rows // tr,),
        in_specs=[pl.BlockSpec((N_DEV, tr, cols), lambda i: (0, i, 0)), row, row, row],
        out_specs=[row, row, row, row], out_shape=[out, out, out, out], compiler_params=_params(1),
    )(parts, w, m, v)


def _adamw_call(g, w, m, v, name):
    rows, cols = g.shape
    tr = 128 if rows % 128 == 0 else rows

    def body(g_ref, w_ref, m_ref, v_ref, d_ref, mo_ref, vo_ref):
        d_ref[...], mo_ref[...], vo_ref[...] = _adamw(w_ref[...], g_ref[...], m_ref[...], v_ref[...])

    row = pl.BlockSpec((tr, cols), lambda i: (i, 0))
    out = jax.ShapeDtypeStruct((rows, cols), F32)
    return pl.pallas_call(
        body, name=name, grid=(rows // tr,), in_specs=[row] * 4, out_specs=[row] * 3, out_shape=[out] * 3,
        compiler_params=_params(1),
    )(g, w, m, v)


def _ada_w_grad_adamw(c_all, dmod_rows, w, m, v, name):
    def body(c_ref, dm_ref, w_ref, m_ref, v_ref, g_ref, d_ref, mo_ref, vo_ref):
        cv = c_ref[...]
        cond = cv * _sigmoid(cv)
        for l in range(2):
            g = lax.dot_general(cond, dm_ref[l], TN, precision=lax.Precision.HIGHEST, preferred_element_type=F32)
            g_ref[l] = g
            d_ref[l], mo_ref[l], vo_ref[l] = _adamw(w_ref[l], g, m_ref[l], v_ref[l])

    out = jax.ShapeDtypeStruct(w.shape, F32)
    return pl.pallas_call(
        body, name=name, out_shape=[out] * 4, compiler_params=pltpu.CompilerParams(vmem_limit_bytes=VMEM_LIMIT),
    )(c_all, dmod_rows, w, m, v)


REPLICATED = ("ln_g", "ln_b", "gmlp_norm_g", "gmlp_norm_b", "gmlp_ws", "gmlp_bs", "pool_b", "pool_scale",
              "mla_kv_norm_g", "mla_w_uk", "mla_w_uv")
CHUNK_ROWS, ADA_ROW, QNORM_ROW, REP_ROWS = 73, 73, 74, 80
UQ_ROWS, POOLW_ROWS = 96, 32
SMALL_ROWS = REP_ROWS + UQ_ROWS + POOLW_ROWS


def _pad_rows(flat2d, rows):
    n, k = flat2d.shape
    return jnp.pad(flat2d, ((0, 0), (0, rows * LANES - k))).reshape(n, rows, LANES)


def _ada_cols_rows(vec):
    return _pad_rows(vec.reshape(2, N_DEV, -1).transpose(1, 0, 2).reshape(N_DEV, -1), 1)


def _pack_replicated(src, ada_vec):
    flat = jnp.concatenate([src[n].reshape(-1) for n in REPLICATED])
    body = _pad_rows(flat.reshape(N_DEV, -1), CHUNK_ROWS)
    return jnp.concatenate([body, jnp.pad(_ada_cols_rows(ada_vec), ((0, 0), (0, REP_ROWS - CHUNK_ROWS - 1), (0, 0)))],
                           axis=1)


def _unpack_replicated(rep, shapes):
    chunk = sum(s[1] for s in shapes) // N_DEV
    flat, off, out = rep[:, :CHUNK_ROWS].reshape(N_DEV, -1)[:, :chunk].reshape(-1), 0, {}
    for n, size, shape in shapes:
        out[n] = flat[off:off + size].reshape(shape)
        off += size
    cols = 3 * D_MODEL // N_DEV
    out["ada_b"] = rep[:, ADA_ROW, :2 * cols].reshape(N_DEV, 2, cols).transpose(1, 0, 2).reshape(2, -1)
    return out


def kernel(x, c, positions, ada_w, ada_b, ln_g, ln_b, e_w_in, gmlp_norm_g, gmlp_norm_b, gmlp_ws, gmlp_bs, pool_w, pool_b, pool_scale, e_w_out, o_w_in, mla_q_norm_g, mla_kv_norm_g, mla_w_uq, mla_w_uk, mla_w_uv, o_w_out, loss_target, m_ada_w, m_ada_b, m_ln_g, m_ln_b, m_e_w_in, m_gmlp_norm_g, m_gmlp_norm_b, m_gmlp_ws, m_gmlp_bs, m_pool_w, m_pool_b, m_pool_scale, m_e_w_out, m_o_w_in, m_mla_q_norm_g, m_mla_kv_norm_g, m_mla_w_uq, m_mla_w_uk, m_mla_w_uv, m_o_w_out, v_ada_w, v_ada_b, v_ln_g, v_ln_b, v_e_w_in, v_gmlp_norm_g, v_gmlp_norm_b, v_gmlp_ws, v_gmlp_bs, v_pool_w, v_pool_b, v_pool_scale, v_e_w_out, v_o_w_in, v_mla_q_norm_g, v_mla_kv_norm_g, v_mla_w_uq, v_mla_w_uk, v_mla_w_uv, v_o_w_out):
    w_in = dict(ada_w=ada_w, ada_b=ada_b, ln_g=ln_g, ln_b=ln_b, e_w_in=e_w_in, gmlp_norm_g=gmlp_norm_g,
                gmlp_norm_b=gmlp_norm_b, gmlp_ws=gmlp_ws, gmlp_bs=gmlp_bs, pool_w=pool_w, pool_b=pool_b,
                pool_scale=pool_scale, e_w_out=e_w_out, o_w_in=o_w_in, mla_q_norm_g=mla_q_norm_g,
                mla_kv_norm_g=mla_kv_norm_g, mla_w_uq=mla_w_uq, mla_w_uk=mla_w_uk, mla_w_uv=mla_w_uv, o_w_out=o_w_out)
    m_in = dict(ada_w=m_ada_w, ada_b=m_ada_b, ln_g=m_ln_g, ln_b=m_ln_b, e_w_in=m_e_w_in, gmlp_norm_g=m_gmlp_norm_g,
                gmlp_norm_b=m_gmlp_norm_b, gmlp_ws=m_gmlp_ws, gmlp_bs=m_gmlp_bs, pool_w=m_pool_w, pool_b=m_pool_b,
                pool_scale=m_pool_scale, e_w_out=m_e_w_out, o_w_in=m_o_w_in, mla_q_norm_g=m_mla_q_norm_g,
                mla_kv_norm_g=m_mla_kv_norm_g, mla_w_uq=m_mla_w_uq, mla_w_uk=m_mla_w_uk, mla_w_uv=m_mla_w_uv,
                o_w_out=m_o_w_out)
    v_in = dict(ada_w=v_ada_w, ada_b=v_ada_b, ln_g=v_ln_g, ln_b=v_ln_b, e_w_in=v_e_w_in, gmlp_norm_g=v_gmlp_norm_g,
                gmlp_norm_b=v_gmlp_norm_b, gmlp_ws=v_gmlp_ws, gmlp_bs=v_gmlp_bs, pool_w=v_pool_w, pool_b=v_pool_b,
                pool_scale=v_pool_scale, e_w_out=v_e_w_out, o_w_in=v_o_w_in, mla_q_norm_g=v_mla_q_norm_g,
                mla_kv_norm_g=v_mla_kv_norm_g, mla_w_uq=v_mla_w_uq, mla_w_uk=v_mla_w_uk, mla_w_uv=v_mla_w_uv,
                o_w_out=v_o_w_out)
    names = list(w_in)
    seq = x.shape[1]
    d = D_MODEL
    me = 4 * lax.axis_index("x") + 2 * lax.axis_index("y") + lax.axis_index("c")
    ada_cols = ada_w.shape[2]

    ada_b_cols = lax.dynamic_slice_in_dim(ada_b, me * ada_cols, ada_cols, axis=1)
    slab_row = lax.broadcasted_iota(jnp.int32, (8, d), 0)
    slab = jnp.where(slab_row == 0, c, jnp.where(slab_row == 1, jnp.pad(mla_q_norm_g, ((0, 0), (0, d - 32))), 0.0))
    c_all, mod = _adaln_exchange(slab, ada_w,
                                 jnp.broadcast_to(ada_b_cols[:, None, :], (2, 8, ada_cols)), "adaln_exchange")

    w_in_e3, w_out_e3, pool_w3 = _all_gather(
        [e_w_in[0].astype(BF16), e_w_out[0].astype(BF16), pool_w.astype(BF16).reshape(POOLW_ROWS, LANES)], "weight_gather")
    h0 = _modulate(x, mod[0], "modulate0")
    proj0, o_in3, uq3 = _matmul_cols_nn(
        h0, w_in_e3, BF16, 512, "even_in",
        side=([o_w_in[0].astype(BF16), mla_w_uq.astype(BF16).reshape(UQ_ROWS, LANES)], True))
    w_out_e = w_out_e3.reshape(-1, d)
    o_in_full = o_in3.transpose(1, 0, 2).reshape(d, ODD_IN)
    w_in_o = jnp.concatenate([o_in_full[:, :448], jnp.zeros((d, 64), BF16), o_in_full[:, 448:]], axis=1)
    uq_full = uq3.reshape(MLA_Q_RANK, MLA_HEADS, MLA_NOPE + MLA_ROPE)
    pool_w_full = pool_w3.reshape(N_DEV, 4, 32, 256).transpose(1, 0, 2, 3).reshape(4, 256, 256)
    w_uq_n = uq_full[:, :, :MLA_NOPE].reshape(MLA_Q_RANK, -1)
    w_uq_r = uq_full[:, :, MLA_NOPE:].reshape(MLA_Q_RANK, -1)
    w_uq = jnp.concatenate([w_uq_n, w_uq_r], axis=1)
    g_q = c_all.reshape(N_DEV, 8, d)[:, 1, :32].reshape(1, MLA_Q_RANK)

    ws, bs_col = gmlp_ws[0], gmlp_bs[0].reshape(GMLP_HEADS, GMLP_BLOCK, 1)
    wuk2, wuv2 = mla_w_uk[0].reshape(MLA_KV_RANK, -1), mla_w_uv[0].reshape(MLA_KV_RANK, -1)
    inv = 1.0 / (ROPE_THETA ** (jnp.arange(0, MLA_ROPE, 2, dtype=F32) / MLA_ROPE))
    ang = positions[0].astype(F32)[:, None] * inv
    cosp = jnp.tile(jnp.cos(ang), (1, 4))
    sinp = jnp.tile(jnp.concatenate([-jnp.sin(ang), jnp.sin(ang)], axis=1), (1, 2))

    mix0 = _even_fwd(proj0, ws, bs_col, gmlp_norm_g, gmlp_norm_b, pool_w_full, pool_b, pool_scale, "even_mix")
    (y0,) = _matmul([(mix0, w_out_e)], "nn", F32, seq, d, 512, 1024, "even_out")
    x1 = _resid_ln(x, y0, mod[0], ln_g[0:1], ln_b[0:1], "resid_ln0")

    h1 = _modulate(x1, mod[1], "modulate1")
    (proj1,) = _matmul([(h1, w_in_o)], "nn", BF16, seq, ODD_IN_PAD, 512, ODD_IN_PAD, "odd_in")
    qn, kp = _mla_prep(proj1, cosp, sinp, g_q, mla_kv_norm_g, "mla_prep")
    (q_up,) = _matmul([(qn, w_uq)], "nn", BF16, seq, 3072, 512, 3072, "q_up")
    qp = _q_heads(q_up, cosp, sinp, wuk2, "q_heads")
    o_lat, lse, w_out_o3 = _attn_fwd(qp, kp, "attn_fwd", side=([o_w_out[0].astype(BF16)], True))
    w_out_o = w_out_o3.reshape(-1, d)
    gated = _o_gate(o_lat, proj1, wuv2, "o_gate")
    (y1,) = _matmul([(gated, w_out_o)], "nn", F32, seq, d, 512, 1024, "odd_out")

    dy1, dxres1, red2 = _final_ln_loss_bwd(x1, y1, mod[1], ln_g[1:2], ln_b[1:2], loss_target, "final_ln_loss")
    (dgated,) = _matmul([(dy1, w_out_o)], "nt", BF16, seq, MLA_WIDTH, 512, MLA_WIDTH, "odd_out_dx")
    (g_w_out_o,) = _matmul([(gated, dy1)], "tn", BF16, MLA_WIDTH, d, 256, d, "odd_out_dw")
    dproj1_z, do_lat, g_wuv = _o_gate_bwd(dgated, o_lat, proj1, wuv2, "o_gate_bwd")
    dqp, dkp, dvv, r_o_out = _attn_bwd(qp, kp, o_lat, do_lat, lse, "attn_bwd",
                                       side=([g_w_out_o.reshape(N_DEV, -1, d)], False))
    dq_nope, dq_rope, g_wuk = _q_heads_bwd(dqp, q_up, cosp, sinp, wuk2, "q_heads_bwd")
    (dqn,) = _matmul([(dq_nope, w_uq_n), (dq_rope, w_uq_r)], "nt", F32, seq, MLA_Q_RANK, 512, 256, "q_up_dx")
    (g_wuq_n,) = _matmul([(qn, dq_nope)], "tn", F32, MLA_Q_RANK, MLA_WIDTH, 256, MLA_WIDTH, "q_up_dw_nope")
    (g_wuq_r,) = _matmul([(qn, dq_rope)], "tn", F32, MLA_Q_RANK, 1024, 256, 1024, "q_up_dw_rope")
    dproj1, red_mla = _mla_prep_bwd(proj1, dqn, dkp, dvv, cosp, sinp, g_q, mla_kv_norm_g, dproj1_z, "mla_prep_bwd")
    (dh1,) = _matmul([(dproj1, w_in_o)], "nt", F32, seq, d, 512, d, "odd_in_dx")
    (g_w_in_o,) = _matmul([(h1, dproj1)], "tn", BF16, d, ODD_IN_PAD, 256, ODD_IN_PAD // 2, "odd_in_dw", n_outer=True)
    part_o_in = jnp.concatenate([g_w_in_o[:, :448], g_w_in_o[:, 512:]], axis=1).reshape(d, N_DEV, -1).transpose(1, 0, 2)
    dy0, dxres0, red1 = _mid_bwd(dh1, dxres1, x, y0, mod[0], mod[1], ln_g[0:1], ln_b[0:1], "mid_bwd")
    (dmix,) = _matmul([(dy0, w_out_e)], "nt", BF16, seq, 2048, 512, 2048, "even_out_dx")
    (g_w_out_e,) = _matmul([(mix0, dy0)], "tn", BF16, 2048, d, 256, d, "even_out_dw")
    dproj0, g_ws, g_bs, g_ng, g_nb, g_pw, g_pb, g_ps, r_o_in = _even_bwd(
        proj0, dmix, ws, bs_col, gmlp_norm_g, gmlp_norm_b, pool_w_full, pool_b, pool_scale, "even_mix_bwd",
        side=([part_o_in], False))
    part_e_in, r_e_out = _matmul_cols_tn(h0, dproj0, w_in_e3.shape[2], BF16, 512, "even_in_dw",
                                         side=([g_w_out_e.reshape(N_DEV, -1, d)], False))
    dh0, r_e_in = _matmul_cols_nt(dproj0, w_in_e3, F32, 512, "even_in_dx", side=([part_e_in], False))
    grad_x, red0 = _first_bwd(dh0, dxres0, x, mod[0], "first_bwd")

    loss = lax.psum(0.5 / d * jnp.sum(red2[3]), ("x", "y", "c"))

    t_mask = lax.broadcasted_iota(jnp.int32, (GMLP_BLOCK, GMLP_BLOCK), 0) // CHUNK
    s_mask = lax.broadcasted_iota(jnp.int32, (GMLP_BLOCK, GMLP_BLOCK), 1) // CHUNK
    part = {
        "ln_g": jnp.stack([red1[2], red2[0]]), "ln_b": jnp.stack([red1[3], red2[1]]),
        "gmlp_norm_g": g_ng, "gmlp_norm_b": g_nb,
        "gmlp_ws": jnp.where(s_mask <= t_mask, g_ws, 0.0), "gmlp_bs": g_bs,
        "pool_b": g_pb, "pool_scale": g_ps, "mla_kv_norm_g": red_mla[1, :MLA_KV_RANK],
        "mla_w_uk": g_wuk, "mla_w_uv": g_wuv,
    }
    dmod = jnp.stack([jnp.concatenate([red0[1], red0[0], red1[4]]),
                      jnp.concatenate([red1[1], red1[0], red2[2]])])

    g_uq = jnp.concatenate([g_wuq_n.reshape(MLA_Q_RANK, MLA_HEADS, MLA_NOPE),
                            g_wuq_r.reshape(MLA_Q_RANK, MLA_HEADS, MLA_ROPE)], axis=2)
    part_small = jnp.concatenate([
        _pad_rows(jnp.concatenate([part[n].reshape(-1) for n in REPLICATED]).reshape(N_DEV, -1), CHUNK_ROWS),
        jnp.pad(jnp.concatenate([_ada_cols_rows(dmod), _pad_rows(red_mla[0].reshape(N_DEV, -1), 1)], axis=1),
                ((0, 0), (0, REP_ROWS - QNORM_ROW - 1), (0, 0))),
        g_uq.reshape(N_DEV, UQ_ROWS, LANES),
        g_pw.reshape(4, N_DEV, 32, 256).transpose(1, 0, 2, 3).reshape(N_DEV, POOLW_ROWS, LANES)], axis=1)
    (r_small,) = _scatter_parts([part_small], "grad_scatter")

    def small_local(src):
        return jnp.concatenate([jnp.pad(src["mla_q_norm_g"], ((QNORM_ROW, REP_ROWS - QNORM_ROW - 1), (0, LANES - 32))),
                                src["mla_w_uq"].reshape(UQ_ROWS, LANES), src["pool_w"].reshape(POOLW_ROWS, LANES)])[None]

    res = {"e_w_in": _sum_parts_adamw(r_e_in, e_w_in, m_e_w_in, v_e_w_in, "adamw_e_w_in"),
           "o_w_in": _sum_parts_adamw(r_o_in, o_w_in, m_o_w_in, v_o_w_in, "adamw_o_w_in"),
           "e_w_out": _sum_parts_adamw(r_e_out, e_w_out, m_e_w_out, v_e_w_out, "adamw_e_w_out"),
           "o_w_out": _sum_parts_adamw(r_o_out, o_w_out, m_o_w_out, v_o_w_out, "adamw_o_w_out")}
    small = _sum_parts_adamw(r_small, small_local(w_in), small_local(m_in), small_local(v_in), "adamw_small")
    for n, r0, r1 in (("mla_w_uq", REP_ROWS, REP_ROWS + UQ_ROWS), ("pool_w", REP_ROWS + UQ_ROWS, SMALL_ROWS)):
        res[n] = [t[0, r0:r1].reshape(w_in[n].shape) for t in small]
    res["mla_q_norm_g"] = [t[0, QNORM_ROW:QNORM_ROW + 1, :32] for t in small]
    (rep_sum,) = _all_gather([small[0][0, :REP_ROWS]], "replicated_gather")
    rep_w, rep_m, rep_v = (_pack_replicated(src, src["ada_b"]).reshape(-1, LANES) for src in (w_in, m_in, v_in))
    rep_res = (rep_sum,) + tuple(t.reshape(N_DEV, REP_ROWS, LANES)
                                 for t in _adamw_call(rep_sum.reshape(-1, LANES), rep_w, rep_m, rep_v, "replicated_adamw"))
    shapes = [(n, w_in[n].size, w_in[n].shape) for n in REPLICATED]
    for k, t in enumerate(rep_res):
        for n, val in _unpack_replicated(t, shapes).items():
            res.setdefault(n, [None] * 4)[k] = val
    dmod_all = r_small[:, ADA_ROW, :2 * ada_cols].reshape(N_DEV, 2, ada_cols).transpose(1, 0, 2)
    dmod_rows = jnp.pad(dmod_all[:, :, None, :], ((0, 0), (0, 0), (0, 7), (0, 0))).reshape(2, 8 * N_DEV, ada_cols)
    res["ada_w"] = _ada_w_grad_adamw(c_all, dmod_rows, ada_w, m_ada_w, v_ada_w, "ada_w_adamw")

    return (loss, grad_x, *[res[n][0] for n in names], *[res[n][1] for n in names],
            *[res[n][2] for n in names], *[res[n][3] for n in names])
```

```python
import functools

import jax
import jax.numpy as jnp
from jax import lax
from jax.experimental import pallas as pl
from jax.experimental.pallas import tpu as pltpu

F32 = jnp.float32
BF16 = jnp.bfloat16

D_MODEL = 1024
CHUNK = 64
LN_EPS = 1e-5
GMLP_HEADS = 4
GMLP_HEAD_DIM = 256
GMLP_BLOCK = 128
POOL_WINDOWS = (2, 4, 8, 16)
POOL_GROUP_DIM = 256
POOL_HALO = 16
MLA_HEADS = 16
MLA_NOPE = 128
MLA_ROPE = 64
MLA_Q_RANK = 256
MLA_KV_RANK = 128
MLA_WIDTH = 2048
ODD_IN = 2496
ODD_IN_PAD = 2560
ROPE_THETA = 10000.0
ATTN_SCALE = (MLA_NOPE + MLA_ROPE) ** -0.5
ATTN_SCALE_LOG2 = ATTN_SCALE * 1.4426950408889634
DEEPNORM_ALPHA = 4.0 ** 0.25
ADAM_LR, ADAM_B1, ADAM_B2, ADAM_EPS, ADAM_WD, ADAM_STEP = 0.001, 0.9, 0.999, 1e-8, 0.01, 10
N_DEV = 8
LANES = 1024
VMEM_LIMIT = 56 * 1024 * 1024
MESH = pl.DeviceIdType.MESH

NT = (((1,), (1,)), ((), ()))
NN = (((1,), (0,)), ((), ()))
TN = (((0,), (0,)), ((), ()))


def _params(n_axes):
    return pltpu.CompilerParams(dimension_semantics=("arbitrary",) * n_axes, vmem_limit_bytes=VMEM_LIMIT)


def _dot(a, b, dn):
    return lax.dot_general(a.astype(BF16), b.astype(BF16), dn, preferred_element_type=F32)


def _sigmoid(z):
    return 1.0 / (1.0 + jnp.exp(-z))


def _colsum(t):
    return jnp.sum(t, axis=0, keepdims=True)


def _exchange_copies(g_refs, r_refs, send_sems, recv_sems, local_sems, gather):
    x, y, c, me = _place()
    n_arr = len(g_refs)

    def src(t, slot):
        return g_refs[t] if gather else g_refs[t].at[slot]

    own = [pltpu.make_async_copy(src(t, me), r_refs[t].at[me], local_sems.at[t]) for t in range(n_arr)]
    sends, recvs = [], []
    for r in range(1, N_DEV):
        peer, lin = _flip(x, y, c, r)
        for t in range(n_arr):
            k = n_arr * (r - 1) + t
            sends.append(pltpu.make_async_remote_copy(
                src_ref=src(t, lin), dst_ref=r_refs[t].at[me], send_sem=send_sems.at[k], recv_sem=recv_sems.at[k],
                device_id=peer, device_id_type=MESH))
            recvs.append(pltpu.make_async_remote_copy(
                src_ref=src(t, lin), dst_ref=r_refs[t].at[lin], send_sem=send_sems.at[k], recv_sem=recv_sems.at[k],
                device_id=(x, y, c), device_id_type=MESH))
    return own, sends, recvs


def _exchange_start(copies):
    own, sends, _ = copies
    for cp in own + sends:
        cp.start()


def _exchange_wait(copies):
    own, sends, recvs = copies
    for cp in recvs:
        cp.wait_recv()
    for cp in sends:
        cp.wait_send()
    for cp in own:
        cp.wait()


def _exchange_extras(parts, gather):
    shapes = [jax.ShapeDtypeStruct(((N_DEV,) + p.shape) if gather else p.shape, p.dtype) for p in parts]
    n = len(parts) * (N_DEV - 1)
    return shapes, [pltpu.SemaphoreType.DMA((n,)), pltpu.SemaphoreType.DMA((n,)), pltpu.SemaphoreType.DMA((len(parts),))]


def _grid_call(body, name, grid, in_specs, out_specs, out_shape, args, scratch=(), side=None):
    if side is None:
        return pl.pallas_call(body, name=name, grid=grid, in_specs=in_specs, out_specs=out_specs,
                              out_shape=out_shape, scratch_shapes=list(scratch),
                              compiler_params=_params(len(grid)))(*args)
    parts, gather = side
    n_in, n_out, n_sc, n_arr = len(args), len(out_shape), len(scratch), len(parts)
    side_shapes, side_sems = _exchange_extras(parts, gather)

    def wrapped(*refs):
        ins, g_refs = refs[:n_in], refs[n_in:n_in + n_arr]
        outs = refs[n_in + n_arr:n_in + n_arr + n_out]
        r_refs = refs[n_in + n_arr + n_out:n_in + 2 * n_arr + n_out]
        sc = refs[n_in + 2 * n_arr + n_out:n_in + 2 * n_arr + n_out + n_sc]
        copies = _exchange_copies(g_refs, r_refs, *refs[-3:], gather)
        ids = [pl.program_id(a) for a in range(len(grid))]
        first = functools.reduce(jnp.logical_and, [i == 0 for i in ids])
        last = functools.reduce(jnp.logical_and, [i == g - 1 for i, g in zip(ids, grid)])

        @pl.when(first)
        def _():
            _exchange_start(copies)

        body(*ins, *outs, *sc)

        @pl.when(last)
        def _():
            _exchange_wait(copies)

    anyspace = pl.BlockSpec(memory_space=pl.ANY)
    return pl.pallas_call(
        wrapped, name=name, grid=grid, in_specs=list(in_specs) + [anyspace] * n_arr,
        out_specs=list(out_specs) + [anyspace] * n_arr, out_shape=list(out_shape) + side_shapes,
        scratch_shapes=list(scratch) + side_sems, compiler_params=_params(len(grid)),
    )(*args, *parts)


def _matmul(pairs, mode, out_dtype, m, n, tm, tn, name, side=None, n_outer=False):
    dn = {"nn": NN, "nt": NT, "tn": TN}[mode]
    tm, tn = min(tm, m), min(tn, n)
    n_pairs = len(pairs)
    grid = (n // tn, m // tm) if n_outer else (m // tm, n // tn)

    def ij(f):
        return (lambda j, i: f(i, j)) if n_outer else f

    def body(*refs):
        o_ref = refs[-1]
        acc = None
        for p in range(n_pairs):
            t = _dot(refs[2 * p][...], refs[2 * p + 1][...], dn)
            acc = t if acc is None else acc + t
        o_ref[...] = acc.astype(o_ref.dtype)

    in_specs, args = [], []
    for a, b in pairs:
        if mode == "nn":
            k = a.shape[1]
            in_specs += [pl.BlockSpec((tm, k), ij(lambda i, j: (i, 0))), pl.BlockSpec((k, tn), ij(lambda i, j: (0, j)))]
        elif mode == "nt":
            k = a.shape[1]
            in_specs += [pl.BlockSpec((tm, k), ij(lambda i, j: (i, 0))), pl.BlockSpec((tn, k), ij(lambda i, j: (j, 0)))]
        else:
            k = a.shape[0]
            in_specs += [pl.BlockSpec((k, tm), ij(lambda i, j: (0, i))), pl.BlockSpec((k, tn), ij(lambda i, j: (0, j)))]
        args += [a, b]
    return _grid_call(body, name, grid, in_specs, [pl.BlockSpec((tm, tn), ij(lambda i, j: (i, j)))],
                      [jax.ShapeDtypeStruct((m, n), out_dtype)], args, side=side)


def _matmul_cols_nn(a, w3, out_dtype, tm, name, side=None):
    m, k = a.shape
    _, _, n = w3.shape
    tm = min(tm, m)

    def body(a_ref, w_ref, o_ref):
        av = a_ref[...]
        for j in range(N_DEV):
            o_ref[:, n * j:n * (j + 1)] = _dot(av, w_ref[j], NN).astype(o_ref.dtype)

    return _grid_call(
        body, name, (m // tm,),
        [pl.BlockSpec((tm, k), lambda i: (i, 0)), pl.BlockSpec((N_DEV, k, n), lambda i: (0, 0, 0))],
        [pl.BlockSpec((tm, N_DEV * n), lambda i: (i, 0))], [jax.ShapeDtypeStruct((m, N_DEV * n), out_dtype)], [a, w3],
        side=side)


def _matmul_cols_nt(a, w3, out_dtype, tm, name, side=None):
    m = a.shape[0]
    _, k, n = w3.shape
    tm = min(tm, m)

    def body(a_ref, w_ref, o_ref):
        acc = _dot(a_ref[:, 0:n], w_ref[0], NT)
        for j in range(1, N_DEV):
            acc = acc + _dot(a_ref[:, n * j:n * (j + 1)], w_ref[j], NT)
        o_ref[...] = acc.astype(o_ref.dtype)

    return _grid_call(
        body, name, (m // tm,),
        [pl.BlockSpec((tm, N_DEV * n), lambda i: (i, 0)), pl.BlockSpec((N_DEV, k, n), lambda i: (0, 0, 0))],
        [pl.BlockSpec((tm, k), lambda i: (i, 0))], [jax.ShapeDtypeStruct((m, k), out_dtype)], [a, w3], side=side)


def _matmul_cols_tn(a, b, n, out_dtype, tk, name, side=None):
    m, k = a.shape
    tk = min(tk, k)

    def body(a_ref, b_ref, o_ref):
        o_ref[...] = _dot(a_ref[...], b_ref[...], TN).astype(o_ref.dtype)

    return _grid_call(
        body, name, (N_DEV, k // tk),
        [pl.BlockSpec((m, tk), lambda j, i: (0, i)), pl.BlockSpec((m, n), lambda j, i: (0, j))],
        [pl.BlockSpec((None, tk, n), lambda j, i: (j, i, 0))], [jax.ShapeDtypeStruct((N_DEV, k, n), out_dtype)], [a, b],
        side=side)


def _rows3(tm, d):
    return pl.BlockSpec((None, tm, d), lambda i: (0, i, 0))


def _modulate(x, mod, name):
    _, s, d = x.shape
    tm = min(s, 512)

    def body(x_ref, m_ref, h_ref):
        shift, scale = m_ref[0:1, 0:d], m_ref[0:1, d:2 * d]
        h_ref[...] = (x_ref[...] * (1.0 + scale) + shift).astype(BF16)

    return pl.pallas_call(
        body, name=name, grid=(s // tm,),
        in_specs=[_rows3(tm, d), pl.BlockSpec((8, 3 * d), lambda i: (0, 0))],
        out_specs=pl.BlockSpec((tm, d), lambda i: (i, 0)),
        out_shape=jax.ShapeDtypeStruct((s, d), BF16), compiler_params=_params(1),
    )(x, mod)


def _ln_stats(r):
    mu = jnp.mean(r, axis=-1, keepdims=True)
    rc = r - mu
    var = jnp.mean(rc * rc, axis=-1, keepdims=True)
    rstd = lax.rsqrt(var + LN_EPS)
    return rc * rstd, rstd


def _ln_bwd(dxhat, xhat, rstd):
    return rstd * (dxhat - jnp.mean(dxhat, axis=-1, keepdims=True)
                   - xhat * jnp.mean(dxhat * xhat, axis=-1, keepdims=True))


def _resid_ln(x, y, mod, g, b, name):
    _, s, d = x.shape
    tm = min(s, 512)

    def body(x_ref, y_ref, m_ref, g_ref, b_ref, o_ref):
        gate = m_ref[0:1, 2 * d:3 * d]
        xhat, _ = _ln_stats(DEEPNORM_ALPHA * x_ref[...] + (1.0 + gate) * y_ref[...])
        o_ref[...] = xhat * g_ref[...] + b_ref[...]

    row = pl.BlockSpec((tm, d), lambda i: (i, 0))
    vec = pl.BlockSpec((1, d), lambda i: (0, 0))
    return pl.pallas_call(
        body, name=name, grid=(s // tm,),
        in_specs=[_rows3(tm, d), row, pl.BlockSpec((8, 3 * d), lambda i: (0, 0)), vec, vec],
        out_specs=_rows3(tm, d), out_shape=jax.ShapeDtypeStruct((1, s, d), F32), compiler_params=_params(1),
    )(x, y, mod, g, b)


def _final_ln_loss_bwd(x, y, mod, g, b, target, name):
    _, s, d = x.shape
    tm = min(s, 256)

    def body(x_ref, y_ref, m_ref, g_ref, b_ref, t_ref, dy_ref, dx_ref, red_ref):
        @pl.when(pl.program_id(0) == 0)
        def _():
            red_ref[...] = jnp.zeros_like(red_ref)

        gate = m_ref[0:1, 2 * d:3 * d]
        yv = y_ref[...]
        xhat, rstd = _ln_stats(DEEPNORM_ALPHA * x_ref[...] + (1.0 + gate) * yv)
        err = xhat * g_ref[...] + b_ref[...] - t_ref[...]
        dout = err * (1.0 / d)
        dr = _ln_bwd(dout * g_ref[...], xhat, rstd)
        dy_ref[...] = ((1.0 + gate) * dr).astype(BF16)
        dx_ref[...] = DEEPNORM_ALPHA * dr
        red_ref[0:1, :] += _colsum(dout * xhat)
        red_ref[1:2, :] += _colsum(dout)
        red_ref[2:3, :] += _colsum(dr * yv)
        red_ref[3:4, :] += _colsum(err * err)

    row = pl.BlockSpec((tm, d), lambda i: (i, 0))
    vec = pl.BlockSpec((1, d), lambda i: (0, 0))
    return pl.pallas_call(
        body, name=name, grid=(s // tm,),
        in_specs=[_rows3(tm, d), row, pl.BlockSpec((8, 3 * d), lambda i: (0, 0)), vec, vec, _rows3(tm, d)],
        out_specs=[row, row, pl.BlockSpec((8, d), lambda i: (0, 0))],
        out_shape=[jax.ShapeDtypeStruct((s, d), BF16), jax.ShapeDtypeStruct((s, d), F32),
                   jax.ShapeDtypeStruct((8, d), F32)],
        compiler_params=_params(1),
    )(x, y, mod, g, b, target)


def _mid_bwd(dh, dxres, x, y, mod_lo, mod_hi, g, b, name):
    _, s, d = x.shape
    tm = min(s, 256)

    def body(dh_ref, dxr_ref, x_ref, y_ref, ml_ref, mh_ref, g_ref, b_ref, dy_ref, dx_ref, red_ref):
        @pl.when(pl.program_id(0) == 0)
        def _():
            red_ref[...] = jnp.zeros_like(red_ref)

        gate = ml_ref[0:1, 2 * d:3 * d]
        scale_hi = mh_ref[0:1, d:2 * d]
        yv, dhv = y_ref[...], dh_ref[...]
        xhat, rstd = _ln_stats(DEEPNORM_ALPHA * x_ref[...] + (1.0 + gate) * yv)
        x_mid = xhat * g_ref[...] + b_ref[...]
        dx_mid = dxr_ref[...] + dhv * (1.0 + scale_hi)
        dr = _ln_bwd(dx_mid * g_ref[...], xhat, rstd)
        dy_ref[...] = ((1.0 + gate) * dr).astype(BF16)
        dx_ref[...] = DEEPNORM_ALPHA * dr
        red_ref[0:1, :] += _colsum(dhv * x_mid)
        red_ref[1:2, :] += _colsum(dhv)
        red_ref[2:3, :] += _colsum(dx_mid * xhat)
        red_ref[3:4, :] += _colsum(dx_mid)
        red_ref[4:5, :] += _colsum(dr * yv)

    row = pl.BlockSpec((tm, d), lambda i: (i, 0))
    vec = pl.BlockSpec((1, d), lambda i: (0, 0))
    modspec = pl.BlockSpec((8, 3 * d), lambda i: (0, 0))
    return pl.pallas_call(
        body, name=name, grid=(s // tm,),
        in_specs=[row, row, _rows3(tm, d), row, modspec, modspec, vec, vec],
        out_specs=[row, row, pl.BlockSpec((8, d), lambda i: (0, 0))],
        out_shape=[jax.ShapeDtypeStruct((s, d), BF16), jax.ShapeDtypeStruct((s, d), F32),
                   jax.ShapeDtypeStruct((8, d), F32)],
        compiler_params=_params(1),
    )(dh, dxres, x, y, mod_lo, mod_hi, g, b)


def _first_bwd(dh, dxres, x, mod, name):
    _, s, d = x.shape
    tm = min(s, 512)

    def body(dh_ref, dxr_ref, x_ref, m_ref, gx_ref, red_ref):
        @pl.when(pl.program_id(0) == 0)
        def _():
            red_ref[...] = jnp.zeros_like(red_ref)

        scale = m_ref[0:1, d:2 * d]
        dhv = dh_ref[...]
        gx_ref[...] = dxr_ref[...] + dhv * (1.0 + scale)
        red_ref[0:1, :] += _colsum(dhv * x_ref[...])
        red_ref[1:2, :] += _colsum(dhv)

    row = pl.BlockSpec((tm, d), lambda i: (i, 0))
    return pl.pallas_call(
        body, name=name, grid=(s // tm,),
        in_specs=[row, row, _rows3(tm, d), pl.BlockSpec((8, 3 * d), lambda i: (0, 0))],
        out_specs=[_rows3(tm, d), pl.BlockSpec((8, d), lambda i: (0, 0))],
        out_shape=[jax.ShapeDtypeStruct((1, s, d), F32), jax.ShapeDtypeStruct((8, d), F32)],
        compiler_params=_params(1),
    )(dh, dxres, x, mod)


EVEN_TM = 256


def _gmlp_mask():
    t = lax.broadcasted_iota(jnp.int32, (GMLP_BLOCK, GMLP_BLOCK), 0) // CHUNK
    s = lax.broadcasted_iota(jnp.int32, (GMLP_BLOCK, GMLP_BLOCK), 1) // CHUNK
    return s <= t


def _window_sum(ext, win, back):
    n = ext.shape[0]
    k = 1
    while k < win:
        ext = ext + pltpu.roll(ext, k if back else n - k, 0)
        k *= 2
    return ext


def _inv_count(row0, rows, win):
    t = row0 + lax.broadcasted_iota(jnp.int32, (rows, 1), 0)
    return t, 1.0 / jnp.minimum(t + 1, win).astype(F32)


def _pooled(xb, halo, row0, win):
    tm = xb.shape[0]
    sums = _window_sum(jnp.concatenate([halo, xb], axis=0), win, True)[POOL_HALO:]
    _, inv = _inv_count(row0, tm, win)
    return sums * inv - xb


def _even_fwd(proj, ws, bs_col, ng, nb, pw, pb, ps, name, side=None):
    s = proj.shape[0]
    tm = min(s, EVEN_TM)
    hd, gd = GMLP_HEAD_DIM, POOL_GROUP_DIM

    def body(p_ref, halo_ref, ws_ref, bs_ref, ng_ref, nb_ref, pw_ref, pb_ref, ps_ref, m_ref):
        i = pl.program_id(0)
        mask = _gmlp_mask()
        for h in range(GMLP_HEADS):
            wm = jnp.where(mask, ws_ref[h], 0.0).astype(BF16)
            for blk in range(tm // GMLP_BLOCK):
                rows = slice(blk * GMLP_BLOCK, (blk + 1) * GMLP_BLOCK)
                cu, cv, cz = h * hd, 1024 + h * hd, 2048 + h * hd
                vhat, _ = _ln_stats(p_ref[rows, cv:cv + hd].astype(F32))
                vn = vhat * ng_ref[...] + nb_ref[...]
                sv = _dot(wm, vn, NN) + bs_ref[h]
                za = p_ref[rows, cz:cz + hd].astype(F32)
                m_ref[rows, cu:cu + hd] = (p_ref[rows, cu:cu + hd].astype(F32) * sv * (za * _sigmoid(za))).astype(BF16)
        for g, win in enumerate(POOL_WINDOWS):
            cx, cz = 3072 + g * gd, 4096 + g * gd
            halo = jnp.where(i > 0, halo_ref[:, g * gd:(g + 1) * gd].astype(F32), 0.0)
            pooled = _pooled(p_ref[:, cx:cx + gd].astype(F32), halo, i * tm, win)
            yb = _dot(pooled, pw_ref[g], NN) + pb_ref[:, g * gd:(g + 1) * gd]
            zb = p_ref[:, cz:cz + gd].astype(F32)
            m_ref[:, 1024 + g * gd:1024 + (g + 1) * gd] = (
                yb * ps_ref[:, g * gd:(g + 1) * gd] * (zb * _sigmoid(zb))).astype(BF16)

    hb = tm // POOL_HALO
    return _grid_call(
        body, name, (s // tm,),
        [
            pl.BlockSpec((tm, 5120), lambda i: (i, 0)),
            pl.BlockSpec((POOL_HALO, 1024), lambda i: (jnp.maximum(i * hb - 1, 0), 3)),
            pl.BlockSpec((GMLP_HEADS, GMLP_BLOCK, GMLP_BLOCK), lambda i: (0, 0, 0)),
            pl.BlockSpec((GMLP_HEADS, GMLP_BLOCK, 1), lambda i: (0, 0, 0)),
            pl.BlockSpec((1, hd), lambda i: (0, 0)), pl.BlockSpec((1, hd), lambda i: (0, 0)),
            pl.BlockSpec((4, gd, gd), lambda i: (0, 0, 0)),
            pl.BlockSpec((1, 1024), lambda i: (0, 0)), pl.BlockSpec((1, 1024), lambda i: (0, 0)),
        ],
        [pl.BlockSpec((tm, 2048), lambda i: (i, 0))], [jax.ShapeDtypeStruct((s, 2048), BF16)],
        [proj, proj, ws, bs_col, ng, nb, pw, pb, ps], side=side)


def _even_bwd(proj, dm, ws, bs_col, ng, nb, pw, pb, ps, name, side=None):
    s = proj.shape[0]
    tm = min(s, EVEN_TM)
    hd, gd = GMLP_HEAD_DIM, POOL_GROUP_DIM
    n_tiles = s // tm

    def body(p_ref, halo_ref, zbn_ref, dm_ref, dbn_ref, ws_ref, bs_ref, ng_ref, nb_ref, pw_ref, pb_ref, ps_ref,
             dp_ref, dws_ref, dbs_ref, dng_ref, dnb_ref, dpw_ref, dpb_ref, dps_ref):
        i = pl.program_id(0)

        @pl.when(i == 0)
        def _():
            for r in (dws_ref, dbs_ref, dng_ref, dnb_ref, dpw_ref, dpb_ref, dps_ref):
                r[...] = jnp.zeros_like(r)

        mask = _gmlp_mask()
        for h in range(GMLP_HEADS):
            wm = jnp.where(mask, ws_ref[h], 0.0).astype(BF16)
            for blk in range(tm // GMLP_BLOCK):
                rows = slice(blk * GMLP_BLOCK, (blk + 1) * GMLP_BLOCK)
                cu, cv, cz = h * hd, 1024 + h * hd, 2048 + h * hd
                vhat, rstd = _ln_stats(p_ref[rows, cv:cv + hd].astype(F32))
                vn = (vhat * ng_ref[...] + nb_ref[...]).astype(BF16)
                sv = _dot(wm, vn, NN) + bs_ref[h]
                u, za = p_ref[rows, cu:cu + hd].astype(F32), p_ref[rows, cz:cz + hd].astype(F32)
                da = dm_ref[rows, cu:cu + hd].astype(F32)
                sig = _sigmoid(za)
                sa = za * sig
                dau = da * u
                dsv = dau * sa
                dp_ref[rows, cu:cu + hd] = (da * sv * sa).astype(BF16)
                dp_ref[rows, cz:cz + hd] = (dau * sv * (sig * (1.0 + za * (1.0 - sig)))).astype(BF16)
                dsv_b = dsv.astype(BF16)
                dbs_ref[h] += jnp.sum(dsv, axis=1, keepdims=True)
                dws_ref[h] += _dot(dsv_b, vn, NT)
                dvn = _dot(wm, dsv_b, TN)
                dng_ref[...] += _colsum(dvn * vhat)
                dnb_ref[...] += _colsum(dvn)
                dp_ref[rows, cv:cv + hd] = _ln_bwd(dvn * ng_ref[...], vhat, rstd).astype(BF16)

        row0 = i * tm
        for g, win in enumerate(POOL_WINDOWS):
            cx, cz, cd = 3072 + g * gd, 4096 + g * gd, 1024 + g * gd
            gs = slice(g * gd, (g + 1) * gd)
            halo = jnp.where(i > 0, halo_ref[:, gs].astype(F32), 0.0)
            xb = p_ref[:, cx:cx + gd].astype(F32)
            pooled = _pooled(xb, halo, row0, win).astype(BF16)
            scale_g = ps_ref[:, gs]
            yb = _dot(pooled, pw_ref[g], NN) + pb_ref[:, gs]
            zb, db = p_ref[:, cz:cz + gd].astype(F32), dm_ref[:, cd:cd + gd].astype(F32)
            sig = _sigmoid(zb)
            dyp = db * (zb * sig)
            dp_ref[:, cz:cz + gd] = (db * yb * scale_g * (sig * (1.0 + zb * (1.0 - sig)))).astype(BF16)
            dps_ref[:, gs] += _colsum(dyp * yb)
            dpb_ref[:, gs] += _colsum(dyp * scale_g)
            zb_ext = jnp.concatenate([zb, zbn_ref[:, gs].astype(F32)], axis=0)
            db_ext = jnp.concatenate([db, dbn_ref[:, gs].astype(F32)], axis=0)
            dy_ext = (db_ext * (zb_ext * _sigmoid(zb_ext)) * scale_g).astype(BF16)
            dpw_ref[g] += _dot(pooled, dy_ext[:tm], TN)
            dpooled = _dot(dy_ext, pw_ref[g], NT)
            t, inv = _inv_count(row0, tm + POOL_HALO, win)
            w_ext = jnp.where(t < s, dpooled * inv, 0.0)
            dp_ref[:, cx:cx + gd] = (_window_sum(w_ext, win, False)[:tm] - dpooled[:tm]).astype(BF16)

    hb = tm // POOL_HALO
    last = s // POOL_HALO - 1
    small = lambda shape: pl.BlockSpec(shape, lambda i: (0,) * len(shape))
    return _grid_call(
        body, name, (n_tiles,),
        [
            pl.BlockSpec((tm, 5120), lambda i: (i, 0)),
            pl.BlockSpec((POOL_HALO, 1024), lambda i: (jnp.maximum(i * hb - 1, 0), 3)),
            pl.BlockSpec((POOL_HALO, 1024), lambda i: (jnp.minimum((i + 1) * hb, last), 4)),
            pl.BlockSpec((tm, 2048), lambda i: (i, 0)),
            pl.BlockSpec((POOL_HALO, 1024), lambda i: (jnp.minimum((i + 1) * hb, last), 1)),
            small((GMLP_HEADS, GMLP_BLOCK, GMLP_BLOCK)), small((GMLP_HEADS, GMLP_BLOCK, 1)),
            small((1, hd)), small((1, hd)), small((4, gd, gd)), small((1, 1024)), small((1, 1024)),
        ],
        [
            pl.BlockSpec((tm, 5120), lambda i: (i, 0)),
            small((GMLP_HEADS, GMLP_BLOCK, GMLP_BLOCK)), small((GMLP_HEADS, GMLP_BLOCK, 1)),
            small((1, hd)), small((1, hd)), small((4, gd, gd)), small((1, 1024)), small((1, 1024)),
        ],
        [
            jax.ShapeDtypeStruct((s, 5120), BF16),
            jax.ShapeDtypeStruct((GMLP_HEADS, GMLP_BLOCK, GMLP_BLOCK), F32),
            jax.ShapeDtypeStruct((GMLP_HEADS, GMLP_BLOCK, 1), F32),
            jax.ShapeDtypeStruct((1, hd), F32), jax.ShapeDtypeStruct((1, hd), F32),
            jax.ShapeDtypeStruct((4, gd, gd), F32),
            jax.ShapeDtypeStruct((1, 1024), F32), jax.ShapeDtypeStruct((1, 1024), F32),
        ],
        [proj, proj, proj, dm, dm, ws, bs_col, ng, nb, pw, pb, ps], side=side)


def _rope_pair_swap(t):
    lane = lax.broadcasted_iota(jnp.int32, t.shape, 1)
    return jnp.where(lane % 64 < 32, pltpu.roll(t, 96, 1), pltpu.roll(t, 32, 1))


def _rms(x, g):
    r = lax.rsqrt(jnp.mean(x * x, axis=-1, keepdims=True) + LN_EPS)
    return x * r, r


def _rms_bwd(dy, g, xhat, r):
    dyg = dy * g
    return r * (dyg - xhat * jnp.mean(dyg * xhat, axis=-1, keepdims=True))


def _lane_lt(shape, n):
    return lax.broadcasted_iota(jnp.int32, shape, 1) < n


def _mla_prep(proj, cosp, sinp, gq, gkv, name):
    s = proj.shape[0]
    tm = min(s, 512)

    def body(qc_ref, kv_ref, c_ref, s_ref, gq_ref, gkv_ref, qn_ref, kp_ref):
        qhat, _ = _rms(qc_ref[...].astype(F32), None)
        qn_ref[...] = (qhat * gq_ref[...]).astype(BF16)
        khat, _ = _rms(kv_ref[:, 0:128].astype(F32), None)
        kp_ref[:, 0:128] = (khat * gkv_ref[...]).astype(BF16)
        kr = kv_ref[:, 128:256].astype(F32)
        kp_ref[:, 128:256] = (kr * c_ref[...] + _rope_pair_swap(kr) * s_ref[...]).astype(BF16)

    return pl.pallas_call(
        body, name=name, grid=(s // tm,),
        in_specs=[pl.BlockSpec((tm, 256), lambda i: (i, 0)), pl.BlockSpec((tm, 256), lambda i: (i, 1)),
                  pl.BlockSpec((tm, 128), lambda i: (i, 0)), pl.BlockSpec((tm, 128), lambda i: (i, 0)),
                  pl.BlockSpec((1, 256), lambda i: (0, 0)), pl.BlockSpec((1, 128), lambda i: (0, 0))],
        out_specs=[pl.BlockSpec((tm, 256), lambda i: (i, 0)), pl.BlockSpec((tm, 256), lambda i: (i, 0))],
        out_shape=[jax.ShapeDtypeStruct((s, 256), BF16), jax.ShapeDtypeStruct((s, 256), BF16)],
        compiler_params=_params(1),
    )(proj, proj, cosp, sinp, gq, gkv)


def _mla_prep_bwd(proj, dqn, dkp, dv, cosp, sinp, gq, gkv, dproj, name):
    s = proj.shape[0]
    tm = min(s, 512)

    def body(qc_ref, kv_ref, dqn_ref, dkp_ref, dv_ref, c_ref, s_ref, gq_ref, gkv_ref, dproj_ref, o_ref, red_ref):
        @pl.when(pl.program_id(0) == 0)
        def _():
            red_ref[...] = jnp.zeros_like(red_ref)

        qhat, qr = _rms(qc_ref[...].astype(F32), None)
        dq = dqn_ref[...]
        o_ref[:, 0:256] = _rms_bwd(dq, gq_ref[...], qhat, qr).astype(BF16)
        red_ref[0:1, :] += _colsum(dq * qhat)
        khat, kr = _rms(kv_ref[:, 0:128].astype(F32), None)
        dk = dkp_ref[:, 0:128] + dv_ref[...]
        o_ref[:, 256:384] = _rms_bwd(dk, gkv_ref[...], khat, kr).astype(BF16)
        red_ref[1:2, 0:128] += _colsum(dk * khat)
        dr = dkp_ref[:, 128:256]
        o_ref[:, 384:512] = (dr * c_ref[...] - _rope_pair_swap(dr) * s_ref[...]).astype(BF16)

    return pl.pallas_call(
        body, name=name, grid=(s // tm,),
        in_specs=[pl.BlockSpec((tm, 256), lambda i: (i, 0)), pl.BlockSpec((tm, 256), lambda i: (i, 1)),
                  pl.BlockSpec((tm, 256), lambda i: (i, 0)), pl.BlockSpec((tm, 256), lambda i: (i, 0)),
                  pl.BlockSpec((tm, 128), lambda i: (i, 0)),
                  pl.BlockSpec((tm, 128), lambda i: (i, 0)), pl.BlockSpec((tm, 128), lambda i: (i, 0)),
                  pl.BlockSpec((1, 256), lambda i: (0, 0)), pl.BlockSpec((1, 128), lambda i: (0, 0)),
                  pl.BlockSpec(memory_space=pl.ANY)],
        out_specs=[pl.BlockSpec((tm, 512), lambda i: (i, 0)), pl.BlockSpec((8, 256), lambda i: (0, 0))],
        out_shape=[jax.ShapeDtypeStruct(dproj.shape, BF16), jax.ShapeDtypeStruct((8, 256), F32)],
        input_output_aliases={9: 0}, compiler_params=_params(1),
    )(proj, proj, dqn, dkp, dv, cosp, sinp, gq, gkv, dproj)


HEADS_TM = 256
Z_COL0 = ODD_IN_PAD - MLA_WIDTH


def _head_cols(h):
    return slice(128 * h, 128 * h + 128)


def _q_heads(q_up, cosp, sinp, wuk, name):
    s = q_up.shape[0]
    tm = min(s, HEADS_TM)

    def body(q_ref, c_ref, s_ref, w_ref, o_ref):
        for p in range(MLA_HEADS // 2):
            raw = q_ref[:, MLA_WIDTH + 128 * p:MLA_WIDTH + 128 * (p + 1)].astype(F32)
            rot = raw * c_ref[...] + _rope_pair_swap(raw) * s_ref[...]
            low = _lane_lt(rot.shape, 64)
            o_ref[2 * p, :, 128:256] = jnp.where(low, rot, 0.0).astype(BF16)
            o_ref[2 * p + 1, :, 128:256] = jnp.where(low, pltpu.roll(rot, 64, 1), 0.0).astype(BF16)
        for h in range(MLA_HEADS):
            o_ref[h, :, 0:128] = _dot(q_ref[:, _head_cols(h)], w_ref[:, _head_cols(h)], NT).astype(BF16)

    return pl.pallas_call(
        body, name=name, grid=(s // tm,),
        in_specs=[pl.BlockSpec((tm, 3072), lambda i: (i, 0)),
                  pl.BlockSpec((tm, 128), lambda i: (i, 0)), pl.BlockSpec((tm, 128), lambda i: (i, 0)),
                  pl.BlockSpec((128, MLA_WIDTH), lambda i: (0, 0))],
        out_specs=pl.BlockSpec((MLA_HEADS, tm, 256), lambda i: (0, i, 0)),
        out_shape=jax.ShapeDtypeStruct((MLA_HEADS, s, 256), BF16), compiler_params=_params(1),
    )(q_up, cosp, sinp, wuk)


def _q_heads_bwd(dqp, q_up, cosp, sinp, wuk, name):
    s = q_up.shape[0]
    tm = min(s, HEADS_TM)

    def body(dq_ref, qn_ref, c_ref, s_ref, w_ref, dn_ref, dr_ref, dw_ref):
        @pl.when(pl.program_id(0) == 0)
        def _():
            dw_ref[...] = jnp.zeros_like(dw_ref)

        for h in range(MLA_HEADS):
            dlat = dq_ref[h, :, 0:128]
            dn_ref[:, _head_cols(h)] = _dot(dlat, w_ref[:, _head_cols(h)], NN).astype(BF16)
            dw_ref[:, _head_cols(h)] += _dot(dlat, qn_ref[:, _head_cols(h)], TN)
        for p in range(MLA_HEADS // 2):
            drot = dq_ref[2 * p, :, 128:256].astype(F32) + pltpu.roll(dq_ref[2 * p + 1, :, 128:256].astype(F32), 64, 1)
            dr_ref[:, _head_cols(p)] = (drot * c_ref[...] - _rope_pair_swap(drot) * s_ref[...]).astype(BF16)

    return pl.pallas_call(
        body, name=name, grid=(s // tm,),
        in_specs=[pl.BlockSpec((MLA_HEADS, tm, 256), lambda i: (0, i, 0)),
                  pl.BlockSpec((tm, MLA_WIDTH), lambda i: (i, 0)),
                  pl.BlockSpec((tm, 128), lambda i: (i, 0)), pl.BlockSpec((tm, 128), lambda i: (i, 0)),
                  pl.BlockSpec((128, MLA_WIDTH), lambda i: (0, 0))],
        out_specs=[pl.BlockSpec((tm, MLA_WIDTH), lambda i: (i, 0)),
                   pl.BlockSpec((tm, 1024), lambda i: (i, 0)),
                   pl.BlockSpec((128, MLA_WIDTH), lambda i: (0, 0))],
        out_shape=[jax.ShapeDtypeStruct((s, MLA_WIDTH), BF16), jax.ShapeDtypeStruct((s, 1024), BF16),
                   jax.ShapeDtypeStruct((128, MLA_WIDTH), F32)],
        compiler_params=_params(1),
    )(dqp, q_up, cosp, sinp, wuk)


def _o_gate(o_lat, proj, wuv, name):
    s = o_lat.shape[1]
    tm = min(s, HEADS_TM)

    def body(ol_ref, p_ref, w_ref, g_ref):
        for h in range(MLA_HEADS):
            z = p_ref[:, Z_COL0 + 128 * h:Z_COL0 + 128 * (h + 1)].astype(F32)
            g_ref[:, _head_cols(h)] = (_dot(ol_ref[h], w_ref[:, _head_cols(h)], NN) * (z * _sigmoid(z))).astype(BF16)

    return pl.pallas_call(
        body, name=name, grid=(s // tm,),
        in_specs=[pl.BlockSpec((MLA_HEADS, tm, 128), lambda i: (0, i, 0)),
                  pl.BlockSpec((tm, ODD_IN_PAD), lambda i: (i, 0)),
                  pl.BlockSpec((128, MLA_WIDTH), lambda i: (0, 0))],
        out_specs=pl.BlockSpec((tm, MLA_WIDTH), lambda i: (i, 0)),
        out_shape=jax.ShapeDtypeStruct((s, MLA_WIDTH), BF16), compiler_params=_params(1),
    )(o_lat, proj, wuv)


def _o_gate_bwd(dg, o_lat, proj, wuv, name):
    s = o_lat.shape[1]
    tm = min(s, HEADS_TM)

    def body(dg_ref, ol_ref, p_ref, w_ref, dp_ref, dol_ref, dw_ref):
        @pl.when(pl.program_id(0) == 0)
        def _():
            dw_ref[...] = jnp.zeros_like(dw_ref)

        dp_ref[:, 0:Z_COL0] = jnp.zeros((tm, Z_COL0), BF16)
        for h in range(MLA_HEADS):
            zc = slice(Z_COL0 + 128 * h, Z_COL0 + 128 * (h + 1))
            z, dgv, ol = p_ref[:, zc].astype(F32), dg_ref[:, _head_cols(h)].astype(F32), ol_ref[h]
            sig = _sigmoid(z)
            o = _dot(ol, w_ref[:, _head_cols(h)], NN)
            dp_ref[:, zc] = (dgv * o * (sig * (1.0 + z * (1.0 - sig)))).astype(BF16)
            do = (dgv * (z * sig)).astype(BF16)
            dol_ref[h] = _dot(do, w_ref[:, _head_cols(h)], NT).astype(BF16)
            dw_ref[:, _head_cols(h)] += _dot(ol, do, TN)

    return pl.pallas_call(
        body, name=name, grid=(s // tm,),
        in_specs=[pl.BlockSpec((tm, MLA_WIDTH), lambda i: (i, 0)),
                  pl.BlockSpec((MLA_HEADS, tm, 128), lambda i: (0, i, 0)),
                  pl.BlockSpec((tm, ODD_IN_PAD), lambda i: (i, 0)),
                  pl.BlockSpec((128, MLA_WIDTH), lambda i: (0, 0))],
        out_specs=[pl.BlockSpec((tm, ODD_IN_PAD), lambda i: (i, 0)),
                   pl.BlockSpec((MLA_HEADS, tm, 128), lambda i: (0, i, 0)),
                   pl.BlockSpec((128, MLA_WIDTH), lambda i: (0, 0))],
        out_shape=[jax.ShapeDtypeStruct((s, ODD_IN_PAD), BF16), jax.ShapeDtypeStruct((MLA_HEADS, s, 128), BF16),
                   jax.ShapeDtypeStruct((128, MLA_WIDTH), F32)],
        compiler_params=_params(1),
    )(dg, o_lat, proj, wuv)


ATT_TQ = CHUNK
ATT_ROWS = ATT_TQ * MLA_HEADS
ATT_TK = 512
ATT_HEAD_GROUP = 4


def _visible(k0, q_chunk, tk):
    kpos = k0 + lax.broadcasted_iota(jnp.int32, (1, tk), 1)
    return kpos // CHUNK <= q_chunk


def _tile_lanes(t, n):
    return jnp.concatenate([t] * (n // 128), axis=1)


def _key_blocks(i, tk, block, pairs=False):
    n_full = (i * ATT_TQ + ATT_TQ + tk - 1) // tk - 1
    if pairs:
        def two(jj, carry):
            block(2 * jj, False)
            block(2 * jj + 1, False)
            return carry

        lax.fori_loop(0, n_full // 2, two, 0)

        @pl.when(n_full % 2 == 1)
        def _():
            block(n_full - 1, False)
    else:
        def one(j, carry):
            block(j, False)
            return carry

        lax.fori_loop(0, n_full, one, 0)
    block(n_full, True)


def _attn_fwd(qp, kp, name, side=None):
    s = kp.shape[0]
    tk = min(ATT_TK, s)

    def body(q_ref, k_ref, o_ref, lse_ref, m_sc, acc_sc):
        i = pl.program_id(0)
        m_sc[...] = jnp.full_like(m_sc, -jnp.inf)
        acc_sc[...] = jnp.zeros_like(acc_sc)

        def block(j, masked):
            k0 = pl.multiple_of(j * tk, tk)
            k = k_ref[pl.ds(k0, tk), :]
            v1 = jnp.where(_lane_lt(k.shape, 128), k, jnp.ones_like(k))
            for h0 in range(0, MLA_HEADS, ATT_HEAD_GROUP):
                rows = slice(h0 * ATT_TQ, (h0 + ATT_HEAD_GROUP) * ATT_TQ)
                q = q_ref[h0:h0 + ATT_HEAD_GROUP].reshape(ATT_HEAD_GROUP * ATT_TQ, 256)
                sc = _dot(q, k, NT) * ATTN_SCALE_LOG2
                if masked:
                    sc = jnp.where(_visible(k0, i, tk), sc, -jnp.inf)
                m_prev = m_sc[rows]
                m_new = jnp.maximum(m_prev, jnp.max(sc, axis=1, keepdims=True))
                p = jnp.exp2(sc - _tile_lanes(m_new, tk))
                acc_sc[rows] = _tile_lanes(jnp.exp2(m_prev - m_new), 256) * acc_sc[rows] + _dot(p, v1, NN)
                m_sc[rows] = m_new

        _key_blocks(i, tk, block, pairs=True)
        acc = acc_sc[...]
        l = acc[:, 128:256]
        o_ref[...] = (acc[:, 0:128] / l).astype(BF16).reshape(MLA_HEADS, ATT_TQ, 128)
        lse_ref[...] = (m_sc[...] + jnp.log2(l)).reshape(MLA_HEADS, ATT_TQ, 128)

    head128 = pl.BlockSpec((MLA_HEADS, ATT_TQ, 128), lambda i: (0, i, 0))
    return _grid_call(
        body, name, (s // ATT_TQ,),
        [pl.BlockSpec((MLA_HEADS, ATT_TQ, 256), lambda i: (0, i, 0)), pl.BlockSpec((s, 256), lambda i: (0, 0))],
        [head128, head128],
        [jax.ShapeDtypeStruct((MLA_HEADS, s, 128), BF16), jax.ShapeDtypeStruct((MLA_HEADS, s, 128), F32)],
        [qp, kp], scratch=[pltpu.VMEM((ATT_ROWS, 128), F32), pltpu.VMEM((ATT_ROWS, 256), F32)], side=side)


def _attn_bwd(qp, kp, o, do, lse, name, side=None):
    s = kp.shape[0]
    tk = min(ATT_TK, s)

    def body(q_ref, k_ref, o_ref, do_ref, lse_ref, dq_ref, dk_ref, dv_ref, dq_sc):
        i = pl.program_id(0)

        @pl.when(i == 0)
        def _():
            dk_ref[...] = jnp.zeros_like(dk_ref)
            dv_ref[...] = jnp.zeros_like(dv_ref)

        q = q_ref[...].reshape(ATT_ROWS, 256)
        dov = do_ref[...].reshape(ATT_ROWS, 128)
        delta = jnp.sum(dov.astype(F32) * o_ref[...].reshape(ATT_ROWS, 128).astype(F32), axis=1, keepdims=True)
        delta_t = _tile_lanes(jnp.broadcast_to(delta, (ATT_ROWS, 128)), tk)
        lse_t = _tile_lanes(lse_ref[...].reshape(ATT_ROWS, 128), tk)
        dq_sc[...] = jnp.zeros_like(dq_sc)

        def block(j, masked):
            k0 = pl.multiple_of(j * tk, tk)
            k = k_ref[pl.ds(k0, tk), :]
            p = jnp.exp2(_dot(q, k, NT) * ATTN_SCALE_LOG2 - lse_t)
            if masked:
                p = jnp.where(_visible(k0, i, tk), p, 0.0)
            dv_ref[pl.ds(k0, tk), :] += _dot(p, dov, TN)
            ds = (p * (_dot(dov, k[:, 0:128], NT) - delta_t) * ATTN_SCALE).astype(BF16)
            dq_sc[...] += _dot(ds, k, NN)
            dk_ref[pl.ds(k0, tk), :] += _dot(ds, q, TN)

        _key_blocks(i, tk, block, pairs=True)
        dq_ref[...] = dq_sc[...].astype(BF16).reshape(MLA_HEADS, ATT_TQ, 256)

    head128 = pl.BlockSpec((MLA_HEADS, ATT_TQ, 128), lambda i: (0, i, 0))
    head256 = pl.BlockSpec((MLA_HEADS, ATT_TQ, 256), lambda i: (0, i, 0))
    return _grid_call(
        body, name, (s // ATT_TQ,),
        [head256, pl.BlockSpec((s, 256), lambda i: (0, 0)), head128, head128, head128],
        [head256, pl.BlockSpec((s, 256), lambda i: (0, 0)), pl.BlockSpec((s, 128), lambda i: (0, 0))],
        [jax.ShapeDtypeStruct((MLA_HEADS, s, 256), BF16),
         jax.ShapeDtypeStruct((s, 256), F32), jax.ShapeDtypeStruct((s, 128), F32)],
        [qp, kp, o, do, lse], scratch=[pltpu.VMEM((ATT_ROWS, 256), F32)], side=side)


def _place():
    x, y, c = lax.axis_index("x"), lax.axis_index("y"), lax.axis_index("c")
    return x, y, c, 4 * x + 2 * y + c


def _flip(x, y, c, r):
    px = 1 - x if r & 4 else x
    py = 1 - y if r & 2 else y
    pc = 1 - c if r & 1 else c
    return (px, py, pc), 4 * px + 2 * py + pc


def _adaln_exchange(c8, ada_w, ada_b_cols, name):
    d = c8.shape[1]
    w_cols = ada_w.shape[2]

    def body(c_ref, w_ref, b_ref, call_ref, mod_ref, sbuf, rbuf, s1, r1, s2, r2):
        x, y, c, me = _place()
        call_ref[pl.ds(pl.multiple_of(me * 8, 8), 8), :] = c_ref[...]
        peers = [_flip(x, y, c, r) for r in range(1, N_DEV)]

        def c_copy(k, src_lin, to):
            rows = call_ref.at[pl.ds(pl.multiple_of(src_lin * 8, 8), 8), :]
            return pltpu.make_async_remote_copy(src_ref=rows, dst_ref=rows, send_sem=s1.at[k], recv_sem=r1.at[k],
                                                device_id=to, device_id_type=MESH)

        first = [c_copy(k, me, peer) for k, (peer, _) in enumerate(peers)]
        for cp in first:
            cp.start()
        for k, (_, lin) in enumerate(peers):
            c_copy(k, lin, (x, y, c)).wait_recv()
        for cp in first:
            cp.wait_send()

        for j in range(N_DEV):
            cj = call_ref[8 * j:8 * j + 8, :]
            cond = cj * _sigmoid(cj)
            for l in range(2):
                sbuf[j, l] = lax.dot_general(cond, w_ref[l], NN, precision=lax.Precision.HIGHEST,
                                             preferred_element_type=F32) + b_ref[l]

        def m_copy(k, src_slot, dst_slot, to):
            return pltpu.make_async_remote_copy(src_ref=sbuf.at[src_slot], dst_ref=rbuf.at[dst_slot],
                                                send_sem=s2.at[k], recv_sem=r2.at[k], device_id=to,
                                                device_id_type=MESH)

        rbuf[me] = sbuf[me]
        second = [m_copy(k, lin, me, peer) for k, (peer, lin) in enumerate(peers)]
        for cp in second:
            cp.start()
        for k, (_, lin) in enumerate(peers):
            m_copy(k, lin, lin, (x, y, c)).wait_recv()
        for cp in second:
            cp.wait_send()
        for j in range(N_DEV):
            for l in range(2):
                mod_ref[l, :, w_cols * j:w_cols * (j + 1)] = rbuf[j, l]

    vmem = pl.BlockSpec(memory_space=pltpu.VMEM)
    return pl.pallas_call(
        body, name=name, in_specs=[vmem, vmem, vmem], out_specs=[vmem, vmem],
        out_shape=[jax.ShapeDtypeStruct((8 * N_DEV, d), F32), jax.ShapeDtypeStruct((2, 8, 3 * d), F32)],
        scratch_shapes=[pltpu.VMEM((N_DEV, 2, 8, w_cols), F32), pltpu.VMEM((N_DEV, 2, 8, w_cols), F32),
                        pltpu.SemaphoreType.DMA((N_DEV - 1,)), pltpu.SemaphoreType.DMA((N_DEV - 1,)),
                        pltpu.SemaphoreType.DMA((N_DEV - 1,)), pltpu.SemaphoreType.DMA((N_DEV - 1,))],
        compiler_params=pltpu.CompilerParams(vmem_limit_bytes=VMEM_LIMIT),
    )(c8, ada_w, ada_b_cols)


def _all_gather(blocks, name):
    n_arr = len(blocks)

    def body(*refs):
        x_refs, out_refs = refs[:n_arr], refs[n_arr:2 * n_arr]
        send_sems, recv_sems, local_sems = refs[2 * n_arr:]
        x, y, c, _ = _place()
        me, sibling = (x, y, c), (x, y, 1 - c)
        chips = [(1 - x, y), (x, 1 - y), (1 - x, 1 - y)]

        def copy(t, k, blk, to, src=None):
            slot = out_refs[t].at[4 * blk[0] + 2 * blk[1] + blk[2]]
            return pltpu.make_async_remote_copy(src_ref=slot if src is None else src, dst_ref=slot,
                                                send_sem=send_sems.at[7 * t + k], recv_sem=recv_sems.at[7 * t + k],
                                                device_id=to, device_id_type=MESH)

        mine = [pltpu.make_async_copy(x_refs[t], out_refs[t].at[4 * x + 2 * y + c], local_sems.at[t])
                for t in range(n_arr)]
        for cp in mine:
            cp.start()
        first = []
        for t in range(n_arr):
            first.append(copy(t, 0, me, sibling, src=x_refs[t]))
            first += [copy(t, 1 + j, me, (*chip, c), src=x_refs[t]) for j, chip in enumerate(chips)]
        for cp in first:
            cp.start()
        passed = []
        for t in range(n_arr):
            for j, chip in enumerate(chips):
                copy(t, 1 + j, (*chip, c), me).wait_recv()
                passed.append(copy(t, 4 + j, (*chip, c), sibling))
                passed[-1].start()
        for t in range(n_arr):
            copy(t, 0, sibling, me).wait_recv()
            for j, chip in enumerate(chips):
                copy(t, 4 + j, (*chip, 1 - c), me).wait_recv()
        for cp in first + passed:
            cp.wait_send()
        for cp in mine:
            cp.wait()

    anyspace = pl.BlockSpec(memory_space=pl.ANY)
    return pl.pallas_call(
        body, name=name, in_specs=[anyspace] * n_arr, out_specs=[anyspace] * n_arr,
        out_shape=[jax.ShapeDtypeStruct((N_DEV,) + b.shape, b.dtype) for b in blocks],
        scratch_shapes=[pltpu.SemaphoreType.DMA((7 * n_arr,)), pltpu.SemaphoreType.DMA((7 * n_arr,)),
                        pltpu.SemaphoreType.DMA((n_arr,))],
    )(*blocks)


def _scatter_parts(parts, name):
    n_arr = len(parts)

    def body(*refs):
        copies = _exchange_copies(refs[:n_arr], refs[n_arr:2 * n_arr], *refs[2 * n_arr:], False)
        _exchange_start(copies)
        _exchange_wait(copies)

    anyspace = pl.BlockSpec(memory_space=pl.ANY)
    shapes, sems = _exchange_extras(parts, False)
    return pl.pallas_call(body, name=name, in_specs=[anyspace] * n_arr, out_specs=[anyspace] * n_arr,
                          out_shape=shapes, scratch_shapes=sems)(*parts)


def _adamw(w, g, m, v):
    m = ADAM_B1 * m + (1.0 - ADAM_B1) * g
    v = ADAM_B2 * v + (1.0 - ADAM_B2) * (g * g)
    m_hat = m / (1.0 - ADAM_B1 ** ADAM_STEP)
    v_hat = v / (1.0 - ADAM_B2 ** ADAM_STEP)
    return -ADAM_LR * (m_hat / (jnp.sqrt(v_hat) + ADAM_EPS) + ADAM_WD * w), m, v


def _sum_parts_adamw(parts, w, m, v, name):
    _, rows, cols = parts.shape
    tr = max(t for t in range(16, 129, 16) if rows % t == 0)

    def body(p_ref, w_ref, m_ref, v_ref, g_ref, d_ref, mo_ref, vo_ref):
        g = p_ref[0].astype(F32)
        for j in range(1, N_DEV):
            g = g + p_ref[j].astype(F32)
        g_ref[...] = g
        d_ref[...], mo_ref[...], vo_ref[...] = _adamw(w_ref[...], g, m_ref[...], v_ref[...])

    row = _rows3(tr, cols)
    out = jax.ShapeDtypeStruct((1, rows, cols), F32)
    return pl.pallas_call(
        body, name=name, grid=(rows // tr,),
        in_specs=[pl.BlockSpec((N_DEV, tr, cols), lambda i: (0, i, 0)), row, row, row],
        out_specs=[row, row, row, row], out_shape=[out, out, out, out], compiler_params=_params(1),
    )(parts, w, m, v)


def _adamw_call(g, w, m, v, name):
    rows, cols = g.shape
    tr = 128 if rows % 128 == 0 else rows

    def body(g_ref, w_ref, m_ref, v_ref, d_ref, mo_ref, vo_ref):
        d_ref[...], mo_ref[...], vo_ref[...] = _adamw(w_ref[...], g_ref[...], m_ref[...], v_ref[...])

    row = pl.BlockSpec((tr, cols), lambda i: (i, 0))
    out = jax.ShapeDtypeStruct((rows, cols), F32)
    return pl.pallas_call(
        body, name=name, grid=(rows // tr,), in_specs=[row] * 4, out_specs=[row] * 3, out_shape=[out] * 3,
        compiler_params=_params(1),
    )(g, w, m, v)


def _ada_w_grad_adamw(c_all, dmod_rows, w, m, v, name):
    def body(c_ref, dm_ref, w_ref, m_ref, v_ref, g_ref, d_ref, mo_ref, vo_ref):
        cv = c_ref[...]
        cond = cv * _sigmoid(cv)
        for l in range(2):
            g = lax.dot_general(cond, dm_ref[l], TN, precision=lax.Precision.HIGHEST, preferred_element_type=F32)
            g_ref[l] = g
            d_ref[l], mo_ref[l], vo_ref[l] = _adamw(w_ref[l], g, m_ref[l], v_ref[l])

    out = jax.ShapeDtypeStruct(w.shape, F32)
    return pl.pallas_call(
        body, name=name, out_shape=[out] * 4, compiler_params=pltpu.CompilerParams(vmem_limit_bytes=VMEM_LIMIT),
    )(c_all, dmod_rows, w, m, v)


REPLICATED = ("ln_g", "ln_b", "gmlp_norm_g", "gmlp_norm_b", "gmlp_ws", "gmlp_bs", "pool_b", "pool_scale",
              "mla_kv_norm_g", "mla_w_uk", "mla_w_uv")
CHUNK_ROWS, ADA_ROW, QNORM_ROW, REP_ROWS = 73, 73, 74, 80
UQ_ROWS, POOLW_ROWS = 96, 32
SMALL_ROWS = REP_ROWS + UQ_ROWS + POOLW_ROWS


def _pad_rows(flat2d, rows):
    n, k = flat2d.shape
    return jnp.pad(flat2d, ((0, 0), (0, rows * LANES - k))).reshape(n, rows, LANES)


def _ada_cols_rows(vec):
    return _pad_rows(vec.reshape(2, N_DEV, -1).transpose(1, 0, 2).reshape(N_DEV, -1), 1)


def _pack_replicated(src, ada_vec):
    flat = jnp.concatenate([src[n].reshape(-1) for n in REPLICATED])
    body = _pad_rows(flat.reshape(N_DEV, -1), CHUNK_ROWS)
    return jnp.concatenate([body, jnp.pad(_ada_cols_rows(ada_vec), ((0, 0), (0, REP_ROWS - CHUNK_ROWS - 1), (0, 0)))],
                           axis=1)


def _unpack_replicated(rep, shapes):
    chunk = sum(s[1] for s in shapes) // N_DEV
    flat, off, out = rep[:, :CHUNK_ROWS].reshape(N_DEV, -1)[:, :chunk].reshape(-1), 0, {}
    for n, size, shape in shapes:
        out[n] = flat[off:off + size].reshape(shape)
        off += size
    cols = 3 * D_MODEL // N_DEV
    out["ada_b"] = rep[:, ADA_ROW, :2 * cols].reshape(N_DEV, 2, cols).transpose(1, 0, 2).reshape(2, -1)
    return out


def kernel(x, c, positions, ada_w, ada_b, ln_g, ln_b, e_w_in, gmlp_norm_g, gmlp_norm_b, gmlp_ws, gmlp_bs, pool_w, pool_b, pool_scale, e_w_out, o_w_in, mla_q_norm_g, mla_kv_norm_g, mla_w_uq, mla_w_uk, mla_w_uv, o_w_out, loss_target, m_ada_w, m_ada_b, m_ln_g, m_ln_b, m_e_w_in, m_gmlp_norm_g, m_gmlp_norm_b, m_gmlp_ws, m_gmlp_bs, m_pool_w, m_pool_b, m_pool_scale, m_e_w_out, m_o_w_in, m_mla_q_norm_g, m_mla_kv_norm_g, m_mla_w_uq, m_mla_w_uk, m_mla_w_uv, m_o_w_out, v_ada_w, v_ada_b, v_ln_g, v_ln_b, v_e_w_in, v_gmlp_norm_g, v_gmlp_norm_b, v_gmlp_ws, v_gmlp_bs, v_pool_w, v_pool_b, v_pool_scale, v_e_w_out, v_o_w_in, v_mla_q_norm_g, v_mla_kv_norm_g, v_mla_w_uq, v_mla_w_uk, v_mla_w_uv, v_o_w_out):
    w_in = dict(ada_w=ada_w, ada_b=ada_b, ln_g=ln_g, ln_b=ln_b, e_w_in=e_w_in, gmlp_norm_g=gmlp_norm_g,
                gmlp_norm_b=gmlp_norm_b, gmlp_ws=gmlp_ws, gmlp_bs=gmlp_bs, pool_w=pool_w, pool_b=pool_b,
                pool_scale=pool_scale, e_w_out=e_w_out, o_w_in=o_w_in, mla_q_norm_g=mla_q_norm_g,
                mla_kv_norm_g=mla_kv_norm_g, mla_w_uq=mla_w_uq, mla_w_uk=mla_w_uk, mla_w_uv=mla_w_uv, o_w_out=o_w_out)
    m_in = dict(ada_w=m_ada_w, ada_b=m_ada_b, ln_g=m_ln_g, ln_b=m_ln_b, e_w_in=m_e_w_in, gmlp_norm_g=m_gmlp_norm_g,
                gmlp_norm_b=m_gmlp_norm_b, gmlp_ws=m_gmlp_ws, gmlp_bs=m_gmlp_bs, pool_w=m_pool_w, pool_b=m_pool_b,
                pool_scale=m_pool_scale, e_w_out=m_e_w_out, o_w_in=m_o_w_in, mla_q_norm_g=m_mla_q_norm_g,
                mla_kv_norm_g=m_mla_kv_norm_g, mla_w_uq=m_mla_w_uq, mla_w_uk=m_mla_w_uk, mla_w_uv=m_mla_w_uv,
                o_w_out=m_o_w_out)
    v_in = dict(ada_w=v_ada_w, ada_b=v_ada_b, ln_g=v_ln_g, ln_b=v_ln_b, e_w_in=v_e_w_in, gmlp_norm_g=v_gmlp_norm_g,
                gmlp_norm_b=v_gmlp_norm_b, gmlp_ws=v_gmlp_ws, gmlp_bs=v_gmlp_bs, pool_w=v_pool_w, pool_b=v_pool_b,
                pool_scale=v_pool_scale, e_w_out=v_e_w_out, o_w_in=v_o_w_in, mla_q_norm_g=v_mla_q_norm_g,
                mla_kv_norm_g=v_mla_kv_norm_g, mla_w_uq=v_mla_w_uq, mla_w_uk=v_mla_w_uk, mla_w_uv=v_mla_w_uv,
                o_w_out=v_o_w_out)
    names = list(w_in)
    seq = x.shape[1]
    d = D_MODEL
    me = 4 * lax.axis_index("x") + 2 * lax.axis_index("y") + lax.axis_index("c")
    ada_cols = ada_w.shape[2]

    ada_b_cols = lax.dynamic_slice_in_dim(ada_b, me * ada_cols, ada_cols, axis=1)
    slab_row = lax.broadcasted_iota(jnp.int32, (8, d), 0)
    slab = jnp.where(slab_row == 0, c, jnp.where(slab_row == 1, jnp.pad(mla_q_norm_g, ((0, 0), (0, d - 32))), 0.0))
    c_all, mod = _adaln_exchange(slab, ada_w,
                                 jnp.broadcast_to(ada_b_cols[:, None, :], (2, 8, ada_cols)), "adaln_exchange")

    w_in_e3, pool_w3 = _all_gather(
        [e_w_in[0].astype(BF16), pool_w.astype(BF16).reshape(POOLW_ROWS, LANES)], "weight_gather")
    h0 = _modulate(x, mod[0], "modulate0")
    proj0, o_in3 = _matmul_cols_nn(h0, w_in_e3, BF16, 512, "even_in", side=([o_w_in[0].astype(BF16)], True))
    o_in_full = o_in3.transpose(1, 0, 2).reshape(d, ODD_IN)
    w_in_o = jnp.concatenate([o_in_full[:, :448], jnp.zeros((d, 64), BF16), o_in_full[:, 448:]], axis=1)
    pool_w_full = pool_w3.reshape(N_DEV, 4, 32, 256).transpose(1, 0, 2, 3).reshape(4, 256, 256)
    g_q = c_all.reshape(N_DEV, 8, d)[:, 1, :32].reshape(1, MLA_Q_RANK)

    ws, bs_col = gmlp_ws[0], gmlp_bs[0].reshape(GMLP_HEADS, GMLP_BLOCK, 1)
    wuk2, wuv2 = mla_w_uk[0].reshape(MLA_KV_RANK, -1), mla_w_uv[0].reshape(MLA_KV_RANK, -1)
    inv = 1.0 / (ROPE_THETA ** (jnp.arange(0, MLA_ROPE, 2, dtype=F32) / MLA_ROPE))
    ang = positions[0].astype(F32)[:, None] * inv
    cosp = jnp.tile(jnp.cos(ang), (1, 4))
    sinp = jnp.tile(jnp.concatenate([-jnp.sin(ang), jnp.sin(ang)], axis=1), (1, 2))

    mix0, w_out_e3 = _even_fwd(proj0, ws, bs_col, gmlp_norm_g, gmlp_norm_b, pool_w_full, pool_b, pool_scale, "even_mix",
                               side=([e_w_out[0].astype(BF16)], True))
    w_out_e = w_out_e3.reshape(-1, d)
    y0, uq3 = _matmul([(mix0, w_out_e)], "nn", F32, seq, d, 512, 1024, "even_out",
                      side=([mla_w_uq.astype(BF16).reshape(UQ_ROWS, LANES)], True))
    uq_full = uq3.reshape(MLA_Q_RANK, MLA_HEADS, MLA_NOPE + MLA_ROPE)
    w_uq_n = uq_full[:, :, :MLA_NOPE].reshape(MLA_Q_RANK, -1)
    w_uq_r = uq_full[:, :, MLA_NOPE:].reshape(MLA_Q_RANK, -1)
    w_uq = jnp.concatenate([w_uq_n, w_uq_r], axis=1)
    x1 = _resid_ln(x, y0, mod[0], ln_g[0:1], ln_b[0:1], "resid_ln0")

    h1 = _modulate(x1, mod[1], "modulate1")
    (proj1,) = _matmul([(h1, w_in_o)], "nn", BF16, seq, ODD_IN_PAD, 512, ODD_IN_PAD, "odd_in")
    qn, kp = _mla_prep(proj1, cosp, sinp, g_q, mla_kv_norm_g, "mla_prep")
    (q_up,) = _matmul([(qn, w_uq)], "nn", BF16, seq, 3072, 512, 3072, "q_up")
    qp = _q_heads(q_up, cosp, sinp, wuk2, "q_heads")
    o_lat, lse, w_out_o3 = _attn_fwd(qp, kp, "attn_fwd", side=([o_w_out[0].astype(BF16)], True))
    w_out_o = w_out_o3.reshape(-1, d)
    gated = _o_gate(o_lat, proj1, wuv2, "o_gate")
    (y1,) = _matmul([(gated, w_out_o)], "nn", F32, seq, d, 512, 1024, "odd_out")

    dy1, dxres1, red2 = _final_ln_loss_bwd(x1, y1, mod[1], ln_g[1:2], ln_b[1:2], loss_target, "final_ln_loss")
    (dgated,) = _matmul([(dy1, w_out_o)], "nt", BF16, seq, MLA_WIDTH, 512, MLA_WIDTH, "odd_out_dx")
    (g_w_out_o,) = _matmul([(gated, dy1)], "tn", BF16, MLA_WIDTH, d, 256, d, "odd_out_dw")
    dproj1_z, do_lat, g_wuv = _o_gate_bwd(dgated, o_lat, proj1, wuv2, "o_gate_bwd")
    dqp, dkp, dvv, r_o_out = _attn_bwd(qp, kp, o_lat, do_lat, lse, "attn_bwd",
                                       side=([g_w_out_o.reshape(N_DEV, -1, d)], False))
    dq_nope, dq_rope, g_wuk = _q_heads_bwd(dqp, q_up, cosp, sinp, wuk2, "q_heads_bwd")
    (dqn,) = _matmul([(dq_nope, w_uq_n), (dq_rope, w_uq_r)], "nt", F32, seq, MLA_Q_RANK, 512, 256, "q_up_dx")
    (g_wuq_n,) = _matmul([(qn, dq_nope)], "tn", F32, MLA_Q_RANK, MLA_WIDTH, 256, MLA_WIDTH, "q_up_dw_nope")
    (g_wuq_r,) = _matmul([(qn, dq_rope)], "tn", F32, MLA_Q_RANK, 1024, 256, 1024, "q_up_dw_rope")
    dproj1, red_mla = _mla_prep_bwd(proj1, dqn, dkp, dvv, cosp, sinp, g_q, mla_kv_norm_g, dproj1_z, "mla_prep_bwd")
    (dh1,) = _matmul([(dproj1, w_in_o)], "nt", F32, seq, d, 512, d, "odd_in_dx")
    (g_w_in_o,) = _matmul([(h1, dproj1)], "tn", BF16, d, ODD_IN_PAD, 256, ODD_IN_PAD // 2, "odd_in_dw", n_outer=True)
    part_o_in = jnp.concatenate([g_w_in_o[:, :448], g_w_in_o[:, 512:]], axis=1).reshape(d, N_DEV, -1).transpose(1, 0, 2)
    dy0, dxres0, red1 = _mid_bwd(dh1, dxres1, x, y0, mod[0], mod[1], ln_g[0:1], ln_b[0:1], "mid_bwd")
    (dmix,) = _matmul([(dy0, w_out_e)], "nt", BF16, seq, 2048, 512, 2048, "even_out_dx")
    (g_w_out_e,) = _matmul([(mix0, dy0)], "tn", BF16, 2048, d, 256, d, "even_out_dw")
    dproj0, g_ws, g_bs, g_ng, g_nb, g_pw, g_pb, g_ps, r_o_in = _even_bwd(
        proj0, dmix, ws, bs_col, gmlp_norm_g, gmlp_norm_b, pool_w_full, pool_b, pool_scale, "even_mix_bwd",
        side=([part_o_in], False))
    part_e_in, r_e_out = _matmul_cols_tn(h0, dproj0, w_in_e3.shape[2], BF16, 512, "even_in_dw",
                                         side=([g_w_out_e.reshape(N_DEV, -1, d)], False))
    dh0, r_e_in = _matmul_cols_nt(dproj0, w_in_e3, F32, 512, "even_in_dx", side=([part_e_in], False))
    grad_x, red0 = _first_bwd(dh0, dxres0, x, mod[0], "first_bwd")

    loss = lax.psum(0.5 / d * jnp.sum(red2[3]), ("x", "y", "c"))

    t_mask = lax.broadcasted_iota(jnp.int32, (GMLP_BLOCK, GMLP_BLOCK), 0) // CHUNK
    s_mask = lax.broadcasted_iota(jnp.int32, (GMLP_BLOCK, GMLP_BLOCK), 1) // CHUNK
    part = {
        "ln_g": jnp.stack([red1[2], red2[0]]), "ln_b": jnp.stack([red1[3], red2[1]]),
        "gmlp_norm_g": g_ng, "gmlp_norm_b": g_nb,
        "gmlp_ws": jnp.where(s_mask <= t_mask, g_ws, 0.0), "gmlp_bs": g_bs,
        "pool_b": g_pb, "pool_scale": g_ps, "mla_kv_norm_g": red_mla[1, :MLA_KV_RANK],
        "mla_w_uk": g_wuk, "mla_w_uv": g_wuv,
    }
    dmod = jnp.stack([jnp.concatenate([red0[1], red0[0], red1[4]]),
                      jnp.concatenate([red1[1], red1[0], red2[2]])])

    g_uq = jnp.concatenate([g_wuq_n.reshape(MLA_Q_RANK, MLA_HEADS, MLA_NOPE),
                            g_wuq_r.reshape(MLA_Q_RANK, MLA_HEADS, MLA_ROPE)], axis=2)
    part_small = jnp.concatenate([
        _pad_rows(jnp.concatenate([part[n].reshape(-1) for n in REPLICATED]).reshape(N_DEV, -1), CHUNK_ROWS),
        jnp.pad(jnp.concatenate([_ada_cols_rows(dmod), _pad_rows(red_mla[0].reshape(N_DEV, -1), 1)], axis=1),
                ((0, 0), (0, REP_ROWS - QNORM_ROW - 1), (0, 0))),
        g_uq.reshape(N_DEV, UQ_ROWS, LANES),
        g_pw.reshape(4, N_DEV, 32, 256).transpose(1, 0, 2, 3).reshape(N_DEV, POOLW_ROWS, LANES)], axis=1)
    (r_small,) = _scatter_parts([part_small], "grad_scatter")

    def small_local(src):
        return jnp.concatenate([jnp.pad(src["mla_q_norm_g"], ((QNORM_ROW, REP_ROWS - QNORM_ROW - 1), (0, LANES - 32))),
                                src["mla_w_uq"].reshape(UQ_ROWS, LANES), src["pool_w"].reshape(POOLW_ROWS, LANES)])[None]

    res = {"e_w_in": _sum_parts_adamw(r_e_in, e_w_in, m_e_w_in, v_e_w_in, "adamw_e_w_in"),
           "o_w_in": _sum_parts_adamw(r_o_in, o_w_in, m_o_w_in, v_o_w_in, "adamw_o_w_in"),
           "e_w_out": _sum_parts_adamw(r_e_out, e_w_out, m_e_w_out, v_e_w_out, "adamw_e_w_out"),
           "o_w_out": _sum_parts_adamw(r_o_out, o_w_out, m_o_w_out, v_o_w_out, "adamw_o_w_out")}
    small = _sum_parts_adamw(r_small, small_local(w_in), small_local(m_in), small_local(v_in), "adamw_small")
    for n, r0, r1 in (("mla_w_uq", REP_ROWS, REP_ROWS + UQ_ROWS), ("pool_w", REP_ROWS + UQ_ROWS, SMALL_ROWS)):
        res[n] = [t[0, r0:r1].reshape(w_in[n].shape) for t in small]
    res["mla_q_norm_g"] = [t[0, QNORM_ROW:QNORM_ROW + 1, :32] for t in small]
    (rep_sum,) = _all_gather([small[0][0, :REP_ROWS]], "replicated_gather")
    rep_w, rep_m, rep_v = (_pack_replicated(src, src["ada_b"]).reshape(-1, LANES) for src in (w_in, m_in, v_in))
    rep_res = (rep_sum,) + tuple(t.reshape(N_DEV, REP_ROWS, LANES)
                                 for t in _adamw_call(rep_sum.reshape(-1, LANES), rep_w, rep_m, rep_v, "replicated_adamw"))
    shapes = [(n, w_in[n].size, w_in[n].shape) for n in REPLICATED]
    for k, t in enumerate(rep_res):
        for n, val in _unpack_replicated(t, shapes).items():
            res.setdefault(n, [None] * 4)[k] = val
    dmod_all = r_small[:, ADA_ROW, :2 * ada_cols].reshape(N_DEV, 2, ada_cols).transpose(1, 0, 2)
    dmod_rows = jnp.pad(dmod_all[:, :, None, :], ((0, 0), (0, 0), (0, 7), (0, 0))).reshape(2, 8 * N_DEV, ada_cols)
    res["ada_w"] = _ada_w_grad_adamw(c_all, dmod_rows, ada_w, m_ada_w, v_ada_w, "ada_w_adamw")

    return (loss, grad_x, *[res[n][0] for n in names], *[res[n][1] for n in names],
            *[res[n][2] for n in names], *[res[n][3] for n in names])
```

```python
import functools

import jax
import jax.numpy as jnp
from jax import lax
from jax.experimental import pallas as pl
from jax.experimental.pallas import tpu as pltpu

F32 = jnp.float32
BF16 = jnp.bfloat16

D_MODEL = 1024
CHUNK = 64
LN_EPS = 1e-5
GMLP_HEADS = 4
GMLP_HEAD_DIM = 256
GMLP_BLOCK = 128
POOL_WINDOWS = (2, 4, 8, 16)
POOL_GROUP_DIM = 256
POOL_HALO = 16
MLA_HEADS = 16
MLA_NOPE = 128
MLA_ROPE = 64
MLA_Q_RANK = 256
MLA_KV_RANK = 128
MLA_WIDTH = 2048
ODD_IN = 2496
ODD_IN_PAD = 2560
ROPE_THETA = 10000.0
ATTN_SCALE = (MLA_NOPE + MLA_ROPE) ** -0.5
ATTN_SCALE_LOG2 = ATTN_SCALE * 1.4426950408889634
DEEPNORM_ALPHA = 4.0 ** 0.25
ADAM_LR, ADAM_B1, ADAM_B2, ADAM_EPS, ADAM_WD, ADAM_STEP = 0.001, 0.9, 0.999, 1e-8, 0.01, 10
N_DEV = 8
LANES = 1024
VMEM_LIMIT = 56 * 1024 * 1024
MESH = pl.DeviceIdType.MESH

NT = (((1,), (1,)), ((), ()))
NN = (((1,), (0,)), ((), ()))
TN = (((0,), (0,)), ((), ()))


def _params(n_axes):
    return pltpu.CompilerParams(dimension_semantics=("arbitrary",) * n_axes, vmem_limit_bytes=VMEM_LIMIT)


def _dot(a, b, dn):
    return lax.dot_general(a.astype(BF16), b.astype(BF16), dn, preferred_element_type=F32)


def _sigmoid(z):
    return 1.0 / (1.0 + jnp.exp(-z))


def _colsum(t):
    return jnp.sum(t, axis=0, keepdims=True)


def _exchange_copies(g_refs, r_refs, send_sems, recv_sems, local_sems, gather):
    x, y, c, me = _place()
    n_arr = len(g_refs)

    def src(t, slot):
        return g_refs[t] if gather else g_refs[t].at[slot]

    own = [pltpu.make_async_copy(src(t, me), r_refs[t].at[me], local_sems.at[t]) for t in range(n_arr)]
    sends, recvs = [], []
    for r in range(1, N_DEV):
        peer, lin = _flip(x, y, c, r)
        for t in range(n_arr):
            k = n_arr * (r - 1) + t
            sends.append(pltpu.make_async_remote_copy(
                src_ref=src(t, lin), dst_ref=r_refs[t].at[me], send_sem=send_sems.at[k], recv_sem=recv_sems.at[k],
                device_id=peer, device_id_type=MESH))
            recvs.append(pltpu.make_async_remote_copy(
                src_ref=src(t, lin), dst_ref=r_refs[t].at[lin], send_sem=send_sems.at[k], recv_sem=recv_sems.at[k],
                device_id=(x, y, c), device_id_type=MESH))
    return own, sends, recvs


def _exchange_start(copies):
    own, sends, _ = copies
    for cp in own + sends:
        cp.start()


def _exchange_wait(copies):
    own, sends, recvs = copies
    for cp in recvs:
        cp.wait_recv()
    for cp in sends:
        cp.wait_send()
    for cp in own:
        cp.wait()


def _exchange_extras(parts, gather):
    shapes = [jax.ShapeDtypeStruct(((N_DEV,) + p.shape) if gather else p.shape, p.dtype) for p in parts]
    n = len(parts) * (N_DEV - 1)
    return shapes, [pltpu.SemaphoreType.DMA((n,)), pltpu.SemaphoreType.DMA((n,)), pltpu.SemaphoreType.DMA((len(parts),))]


def _grid_call(body, name, grid, in_specs, out_specs, out_shape, args, scratch=(), side=None):
    if side is None:
        return pl.pallas_call(body, name=name, grid=grid, in_specs=in_specs, out_specs=out_specs,
                              out_shape=out_shape, scratch_shapes=list(scratch),
                              compiler_params=_params(len(grid)))(*args)
    parts, gather = side
    n_in, n_out, n_sc, n_arr = len(args), len(out_shape), len(scratch), len(parts)
    side_shapes, side_sems = _exchange_extras(parts, gather)

    def wrapped(*refs):
        ins, g_refs = refs[:n_in], refs[n_in:n_in + n_arr]
        outs = refs[n_in + n_arr:n_in + n_arr + n_out]
        r_refs = refs[n_in + n_arr + n_out:n_in + 2 * n_arr + n_out]
        sc = refs[n_in + 2 * n_arr + n_out:n_in + 2 * n_arr + n_out + n_sc]
        copies = _exchange_copies(g_refs, r_refs, *refs[-3:], gather)
        ids = [pl.program_id(a) for a in range(len(grid))]
        first = functools.reduce(jnp.logical_and, [i == 0 for i in ids])
        last = functools.reduce(jnp.logical_and, [i == g - 1 for i, g in zip(ids, grid)])

        @pl.when(first)
        def _():
            _exchange_start(copies)

        body(*ins, *outs, *sc)

        @pl.when(last)
        def _():
            _exchange_wait(copies)

    anyspace = pl.BlockSpec(memory_space=pl.ANY)
    return pl.pallas_call(
        wrapped, name=name, grid=grid, in_specs=list(in_specs) + [anyspace] * n_arr,
        out_specs=list(out_specs) + [anyspace] * n_arr, out_shape=list(out_shape) + side_shapes,
        scratch_shapes=list(scratch) + side_sems, compiler_params=_params(len(grid)),
    )(*args, *parts)


def _matmul(pairs, mode, out_dtype, m, n, tm, tn, name, side=None, n_outer=False):
    dn = {"nn": NN, "nt": NT, "tn": TN}[mode]
    tm, tn = min(tm, m), min(tn, n)
    n_pairs = len(pairs)
    grid = (n // tn, m // tm) if n_outer else (m // tm, n // tn)

    def ij(f):
        return (lambda j, i: f(i, j)) if n_outer else f

    def body(*refs):
        o_ref = refs[-1]
        acc = None
        for p in range(n_pairs):
            t = _dot(refs[2 * p][...], refs[2 * p + 1][...], dn)
            acc = t if acc is None else acc + t
        o_ref[...] = acc.astype(o_ref.dtype)

    in_specs, args = [], []
    for a, b in pairs:
        if mode == "nn":
            k = a.shape[1]
            in_specs += [pl.BlockSpec((tm, k), ij(lambda i, j: (i, 0))), pl.BlockSpec((k, tn), ij(lambda i, j: (0, j)))]
        elif mode == "nt":
            k = a.shape[1]
            in_specs += [pl.BlockSpec((tm, k), ij(lambda i, j: (i, 0))), pl.BlockSpec((tn, k), ij(lambda i, j: (j, 0)))]
        else:
            k = a.shape[0]
            in_specs += [pl.BlockSpec((k, tm), ij(lambda i, j: (0, i))), pl.BlockSpec((k, tn), ij(lambda i, j: (0, j)))]
        args += [a, b]
    return _grid_call(body, name, grid, in_specs, [pl.BlockSpec((tm, tn), ij(lambda i, j: (i, j)))],
                      [jax.ShapeDtypeStruct((m, n), out_dtype)], args, side=side)


def _matmul_cols_nn(a, w3, out_dtype, tm, name, side=None):
    m, k = a.shape
    _, _, n = w3.shape
    tm = min(tm, m)

    def body(a_ref, w_ref, o_ref):
        av = a_ref[...]
        for j in range(N_DEV):
            o_ref[:, n * j:n * (j + 1)] = _dot(av, w_ref[j], NN).astype(o_ref.dtype)

    return _grid_call(
        body, name, (m // tm,),
        [pl.BlockSpec((tm, k), lambda i: (i, 0)), pl.BlockSpec((N_DEV, k, n), lambda i: (0, 0, 0))],
        [pl.BlockSpec((tm, N_DEV * n), lambda i: (i, 0))], [jax.ShapeDtypeStruct((m, N_DEV * n), out_dtype)], [a, w3],
        side=side)


def _matmul_cols_nt(a, w3, out_dtype, tm, name, side=None):
    m = a.shape[0]
    _, k, n = w3.shape
    tm = min(tm, m)

    def body(a_ref, w_ref, o_ref):
        acc = _dot(a_ref[:, 0:n], w_ref[0], NT)
        for j in range(1, N_DEV):
            acc = acc + _dot(a_ref[:, n * j:n * (j + 1)], w_ref[j], NT)
        o_ref[...] = acc.astype(o_ref.dtype)

    return _grid_call(
        body, name, (m // tm,),
        [pl.BlockSpec((tm, N_DEV * n), lambda i: (i, 0)), pl.BlockSpec((N_DEV, k, n), lambda i: (0, 0, 0))],
        [pl.BlockSpec((tm, k), lambda i: (i, 0))], [jax.ShapeDtypeStruct((m, k), out_dtype)], [a, w3], side=side)


def _matmul_cols_tn(a, b, n, out_dtype, tk, name, side=None):
    m, k = a.shape
    tk = min(tk, k)

    def body(a_ref, b_ref, o_ref):
        o_ref[...] = _dot(a_ref[...], b_ref[...], TN).astype(o_ref.dtype)

    return _grid_call(
        body, name, (N_DEV, k // tk),
        [pl.BlockSpec((m, tk), lambda j, i: (0, i)), pl.BlockSpec((m, n), lambda j, i: (0, j))],
        [pl.BlockSpec((None, tk, n), lambda j, i: (j, i, 0))], [jax.ShapeDtypeStruct((N_DEV, k, n), out_dtype)], [a, b],
        side=side)


def _rows3(tm, d):
    return pl.BlockSpec((None, tm, d), lambda i: (0, i, 0))


def _modulate(x, mod, name):
    _, s, d = x.shape
    tm = min(s, 512)

    def body(x_ref, m_ref, h_ref):
        shift, scale = m_ref[0:1, 0:d], m_ref[0:1, d:2 * d]
        h_ref[...] = (x_ref[...] * (1.0 + scale) + shift).astype(BF16)

    return pl.pallas_call(
        body, name=name, grid=(s // tm,),
        in_specs=[_rows3(tm, d), pl.BlockSpec((8, 3 * d), lambda i: (0, 0))],
        out_specs=pl.BlockSpec((tm, d), lambda i: (i, 0)),
        out_shape=jax.ShapeDtypeStruct((s, d), BF16), compiler_params=_params(1),
    )(x, mod)


def _ln_stats(r):
    mu = jnp.mean(r, axis=-1, keepdims=True)
    rc = r - mu
    var = jnp.mean(rc * rc, axis=-1, keepdims=True)
    rstd = lax.rsqrt(var + LN_EPS)
    return rc * rstd, rstd


def _ln_bwd(dxhat, xhat, rstd):
    return rstd * (dxhat - jnp.mean(dxhat, axis=-1, keepdims=True)
                   - xhat * jnp.mean(dxhat * xhat, axis=-1, keepdims=True))


def _resid_ln(x, y, mod, g, b, mod_next, name):
    _, s, d = x.shape
    tm = min(s, 512)

    def body(x_ref, y_ref, m_ref, g_ref, b_ref, mn_ref, o_ref, h_ref):
        gate = m_ref[0:1, 2 * d:3 * d]
        xhat, _ = _ln_stats(DEEPNORM_ALPHA * x_ref[...] + (1.0 + gate) * y_ref[...])
        out = xhat * g_ref[...] + b_ref[...]
        o_ref[...] = out
        h_ref[...] = (out * (1.0 + mn_ref[0:1, d:2 * d]) + mn_ref[0:1, 0:d]).astype(BF16)

    row = pl.BlockSpec((tm, d), lambda i: (i, 0))
    vec = pl.BlockSpec((1, d), lambda i: (0, 0))
    modspec = pl.BlockSpec((8, 3 * d), lambda i: (0, 0))
    return pl.pallas_call(
        body, name=name, grid=(s // tm,),
        in_specs=[_rows3(tm, d), row, modspec, vec, vec, modspec],
        out_specs=[_rows3(tm, d), row],
        out_shape=[jax.ShapeDtypeStruct((1, s, d), F32), jax.ShapeDtypeStruct((s, d), BF16)],
        compiler_params=_params(1),
    )(x, y, mod, g, b, mod_next)


def _final_ln_loss_bwd(x, y, mod, g, b, target, name):
    _, s, d = x.shape
    tm = min(s, 256)

    def body(x_ref, y_ref, m_ref, g_ref, b_ref, t_ref, dy_ref, dx_ref, red_ref):
        @pl.when(pl.program_id(0) == 0)
        def _():
            red_ref[...] = jnp.zeros_like(red_ref)

        gate = m_ref[0:1, 2 * d:3 * d]
        yv = y_ref[...]
        xhat, rstd = _ln_stats(DEEPNORM_ALPHA * x_ref[...] + (1.0 + gate) * yv)
        err = xhat * g_ref[...] + b_ref[...] - t_ref[...]
        dout = err * (1.0 / d)
        dr = _ln_bwd(dout * g_ref[...], xhat, rstd)
        dy_ref[...] = ((1.0 + gate) * dr).astype(BF16)
        dx_ref[...] = DEEPNORM_ALPHA * dr
        red_ref[0:1, :] += _colsum(dout * xhat)
        red_ref[1:2, :] += _colsum(dout)
        red_ref[2:3, :] += _colsum(dr * yv)
        red_ref[3:4, :] += _colsum(err * err)

    row = pl.BlockSpec((tm, d), lambda i: (i, 0))
    vec = pl.BlockSpec((1, d), lambda i: (0, 0))
    return pl.pallas_call(
        body, name=name, grid=(s // tm,),
        in_specs=[_rows3(tm, d), row, pl.BlockSpec((8, 3 * d), lambda i: (0, 0)), vec, vec, _rows3(tm, d)],
        out_specs=[row, row, pl.BlockSpec((8, d), lambda i: (0, 0))],
        out_shape=[jax.ShapeDtypeStruct((s, d), BF16), jax.ShapeDtypeStruct((s, d), F32),
                   jax.ShapeDtypeStruct((8, d), F32)],
        compiler_params=_params(1),
    )(x, y, mod, g, b, target)


def _mid_bwd(dh, dxres, x, y, mod_lo, mod_hi, g, b, name):
    _, s, d = x.shape
    tm = min(s, 256)

    def body(dh_ref, dxr_ref, x_ref, y_ref, ml_ref, mh_ref, g_ref, b_ref, dy_ref, dx_ref, red_ref):
        @pl.when(pl.program_id(0) == 0)
        def _():
            red_ref[...] = jnp.zeros_like(red_ref)

        gate = ml_ref[0:1, 2 * d:3 * d]
        scale_hi = mh_ref[0:1, d:2 * d]
        yv, dhv = y_ref[...], dh_ref[...]
        xhat, rstd = _ln_stats(DEEPNORM_ALPHA * x_ref[...] + (1.0 + gate) * yv)
        x_mid = xhat * g_ref[...] + b_ref[...]
        dx_mid = dxr_ref[...] + dhv * (1.0 + scale_hi)
        dr = _ln_bwd(dx_mid * g_ref[...], xhat, rstd)
        dy_ref[...] = ((1.0 + gate) * dr).astype(BF16)
        dx_ref[...] = DEEPNORM_ALPHA * dr
        red_ref[0:1, :] += _colsum(dhv * x_mid)
        red_ref[1:2, :] += _colsum(dhv)
        red_ref[2:3, :] += _colsum(dx_mid * xhat)
        red_ref[3:4, :] += _colsum(dx_mid)
        red_ref[4:5, :] += _colsum(dr * yv)

    row = pl.BlockSpec((tm, d), lambda i: (i, 0))
    vec = pl.BlockSpec((1, d), lambda i: (0, 0))
    modspec = pl.BlockSpec((8, 3 * d), lambda i: (0, 0))
    return pl.pallas_call(
        body, name=name, grid=(s // tm,),
        in_specs=[row, row, _rows3(tm, d), row, modspec, modspec, vec, vec],
        out_specs=[row, row, pl.BlockSpec((8, d), lambda i: (0, 0))],
        out_shape=[jax.ShapeDtypeStruct((s, d), BF16), jax.ShapeDtypeStruct((s, d), F32),
                   jax.ShapeDtypeStruct((8, d), F32)],
        compiler_params=_params(1),
    )(dh, dxres, x, y, mod_lo, mod_hi, g, b)


def _first_bwd(dh, dxres, x, mod, name):
    _, s, d = x.shape
    tm = min(s, 512)

    def body(dh_ref, dxr_ref, x_ref, m_ref, gx_ref, red_ref):
        @pl.when(pl.program_id(0) == 0)
        def _():
            red_ref[...] = jnp.zeros_like(red_ref)

        scale = m_ref[0:1, d:2 * d]
        dhv = dh_ref[...]
        gx_ref[...] = dxr_ref[...] + dhv * (1.0 + scale)
        red_ref[0:1, :] += _colsum(dhv * x_ref[...])
        red_ref[1:2, :] += _colsum(dhv)

    row = pl.BlockSpec((tm, d), lambda i: (i, 0))
    return pl.pallas_call(
        body, name=name, grid=(s // tm,),
        in_specs=[row, row, _rows3(tm, d), pl.BlockSpec((8, 3 * d), lambda i: (0, 0))],
        out_specs=[_rows3(tm, d), pl.BlockSpec((8, d), lambda i: (0, 0))],
        out_shape=[jax.ShapeDtypeStruct((1, s, d), F32), jax.ShapeDtypeStruct((8, d), F32)],
        compiler_params=_params(1),
    )(dh, dxres, x, mod)


EVEN_TM = 256


def _gmlp_mask():
    t = lax.broadcasted_iota(jnp.int32, (GMLP_BLOCK, GMLP_BLOCK), 0) // CHUNK
    s = lax.broadcasted_iota(jnp.int32, (GMLP_BLOCK, GMLP_BLOCK), 1) // CHUNK
    return s <= t


def _window_sum(ext, win, back):
    n = ext.shape[0]
    k = 1
    while k < win:
        ext = ext + pltpu.roll(ext, k if back else n - k, 0)
        k *= 2
    return ext


def _inv_count(row0, rows, win):
    t = row0 + lax.broadcasted_iota(jnp.int32, (rows, 1), 0)
    return t, 1.0 / jnp.minimum(t + 1, win).astype(F32)


def _pooled(xb, halo, row0, win):
    tm = xb.shape[0]
    sums = _window_sum(jnp.concatenate([halo, xb], axis=0), win, True)[POOL_HALO:]
    _, inv = _inv_count(row0, tm, win)
    return sums * inv - xb


def _even_fwd(proj, ws, bs_col, ng, nb, pw, pb, ps, name, side=None):
    s = proj.shape[0]
    tm = min(s, EVEN_TM)
    hd, gd = GMLP_HEAD_DIM, POOL_GROUP_DIM

    def body(p_ref, halo_ref, ws_ref, bs_ref, ng_ref, nb_ref, pw_ref, pb_ref, ps_ref, m_ref):
        i = pl.program_id(0)
        mask = _gmlp_mask()
        for h in range(GMLP_HEADS):
            wm = jnp.where(mask, ws_ref[h], 0.0).astype(BF16)
            for blk in range(tm // GMLP_BLOCK):
                rows = slice(blk * GMLP_BLOCK, (blk + 1) * GMLP_BLOCK)
                cu, cv, cz = h * hd, 1024 + h * hd, 2048 + h * hd
                vhat, _ = _ln_stats(p_ref[rows, cv:cv + hd].astype(F32))
                vn = vhat * ng_ref[...] + nb_ref[...]
                sv = _dot(wm, vn, NN) + bs_ref[h]
                za = p_ref[rows, cz:cz + hd].astype(F32)
                m_ref[rows, cu:cu + hd] = (p_ref[rows, cu:cu + hd].astype(F32) * sv * (za * _sigmoid(za))).astype(BF16)
        for g, win in enumerate(POOL_WINDOWS):
            cx, cz = 3072 + g * gd, 4096 + g * gd
            halo = jnp.where(i > 0, halo_ref[:, g * gd:(g + 1) * gd].astype(F32), 0.0)
            pooled = _pooled(p_ref[:, cx:cx + gd].astype(F32), halo, i * tm, win)
            yb = _dot(pooled, pw_ref[g], NN) + pb_ref[:, g * gd:(g + 1) * gd]
            zb = p_ref[:, cz:cz + gd].astype(F32)
            m_ref[:, 1024 + g * gd:1024 + (g + 1) * gd] = (
                yb * ps_ref[:, g * gd:(g + 1) * gd] * (zb * _sigmoid(zb))).astype(BF16)

    hb = tm // POOL_HALO
    return _grid_call(
        body, name, (s // tm,),
        [
            pl.BlockSpec((tm, 5120), lambda i: (i, 0)),
            pl.BlockSpec((POOL_HALO, 1024), lambda i: (jnp.maximum(i * hb - 1, 0), 3)),
            pl.BlockSpec((GMLP_HEADS, GMLP_BLOCK, GMLP_BLOCK), lambda i: (0, 0, 0)),
            pl.BlockSpec((GMLP_HEADS, GMLP_BLOCK, 1), lambda i: (0, 0, 0)),
            pl.BlockSpec((1, hd), lambda i: (0, 0)), pl.BlockSpec((1, hd), lambda i: (0, 0)),
            pl.BlockSpec((4, gd, gd), lambda i: (0, 0, 0)),
            pl.BlockSpec((1, 1024), lambda i: (0, 0)), pl.BlockSpec((1, 1024), lambda i: (0, 0)),
        ],
        [pl.BlockSpec((tm, 2048), lambda i: (i, 0))], [jax.ShapeDtypeStruct((s, 2048), BF16)],
        [proj, proj, ws, bs_col, ng, nb, pw, pb, ps], side=side)


def _even_bwd(proj, dm, ws, bs_col, ng, nb, pw, pb, ps, name, side=None):
    s = proj.shape[0]
    tm = min(s, EVEN_TM)
    hd, gd = GMLP_HEAD_DIM, POOL_GROUP_DIM
    n_tiles = s // tm

    def body(p_ref, halo_ref, zbn_ref, dm_ref, dbn_ref, ws_ref, bs_ref, ng_ref, nb_ref, pw_ref, pb_ref, ps_ref,
             dp_ref, dws_ref, dbs_ref, dng_ref, dnb_ref, dpw_ref, dpb_ref, dps_ref):
        i = pl.program_id(0)

        @pl.when(i == 0)
        def _():
            for r in (dws_ref, dbs_ref, dng_ref, dnb_ref, dpw_ref, dpb_ref, dps_ref):
                r[...] = jnp.zeros_like(r)

        mask = _gmlp_mask()
        for h in range(GMLP_HEADS):
            wm = jnp.where(mask, ws_ref[h], 0.0).astype(BF16)
            for blk in range(tm // GMLP_BLOCK):
                rows = slice(blk * GMLP_BLOCK, (blk + 1) * GMLP_BLOCK)
                cu, cv, cz = h * hd, 1024 + h * hd, 2048 + h * hd
                vhat, rstd = _ln_stats(p_ref[rows, cv:cv + hd].astype(F32))
                vn = (vhat * ng_ref[...] + nb_ref[...]).astype(BF16)
                sv = _dot(wm, vn, NN) + bs_ref[h]
                u, za = p_ref[rows, cu:cu + hd].astype(F32), p_ref[rows, cz:cz + hd].astype(F32)
                da = dm_ref[rows, cu:cu + hd].astype(F32)
                sig = _sigmoid(za)
                sa = za * sig
                dau = da * u
                dsv = dau * sa
                dp_ref[rows, cu:cu + hd] = (da * sv * sa).astype(BF16)
                dp_ref[rows, cz:cz + hd] = (dau * sv * (sig * (1.0 + za * (1.0 - sig)))).astype(BF16)
                dsv_b = dsv.astype(BF16)
                dbs_ref[h] += jnp.sum(dsv, axis=1, keepdims=True)
                dws_ref[h] += _dot(dsv_b, vn, NT)
                dvn = _dot(wm, dsv_b, TN)
                dng_ref[...] += _colsum(dvn * vhat)
                dnb_ref[...] += _colsum(dvn)
                dp_ref[rows, cv:cv + hd] = _ln_bwd(dvn * ng_ref[...], vhat, rstd).astype(BF16)

        row0 = i * tm
        for g, win in enumerate(POOL_WINDOWS):
            cx, cz, cd = 3072 + g * gd, 4096 + g * gd, 1024 + g * gd
            gs = slice(g * gd, (g + 1) * gd)
            halo = jnp.where(i > 0, halo_ref[:, gs].astype(F32), 0.0)
            xb = p_ref[:, cx:cx + gd].astype(F32)
            pooled = _pooled(xb, halo, row0, win).astype(BF16)
            scale_g = ps_ref[:, gs]
            yb = _dot(pooled, pw_ref[g], NN) + pb_ref[:, gs]
            zb, db = p_ref[:, cz:cz + gd].astype(F32), dm_ref[:, cd:cd + gd].astype(F32)
            sig = _sigmoid(zb)
            dyp = db * (zb * sig)
            dp_ref[:, cz:cz + gd] = (db * yb * scale_g * (sig * (1.0 + zb * (1.0 - sig)))).astype(BF16)
            dps_ref[:, gs] += _colsum(dyp * yb)
            dpb_ref[:, gs] += _colsum(dyp * scale_g)
            zb_ext = jnp.concatenate([zb, zbn_ref[:, gs].astype(F32)], axis=0)
            db_ext = jnp.concatenate([db, dbn_ref[:, gs].astype(F32)], axis=0)
            dy_ext = (db_ext * (zb_ext * _sigmoid(zb_ext)) * scale_g).astype(BF16)
            dpw_ref[g] += _dot(pooled, dy_ext[:tm], TN)
            dpooled = _dot(dy_ext, pw_ref[g], NT)
            t, inv = _inv_count(row0, tm + POOL_HALO, win)
            w_ext = jnp.where(t < s, dpooled * inv, 0.0)
            dp_ref[:, cx:cx + gd] = (_window_sum(w_ext, win, False)[:tm] - dpooled[:tm]).astype(BF16)

    hb = tm // POOL_HALO
    last = s // POOL_HALO - 1
    small = lambda shape: pl.BlockSpec(shape, lambda i: (0,) * len(shape))
    return _grid_call(
        body, name, (n_tiles,),
        [
            pl.BlockSpec((tm, 5120), lambda i: (i, 0)),
            pl.BlockSpec((POOL_HALO, 1024), lambda i: (jnp.maximum(i * hb - 1, 0), 3)),
            pl.BlockSpec((POOL_HALO, 1024), lambda i: (jnp.minimum((i + 1) * hb, last), 4)),
            pl.BlockSpec((tm, 2048), lambda i: (i, 0)),
            pl.BlockSpec((POOL_HALO, 1024), lambda i: (jnp.minimum((i + 1) * hb, last), 1)),
            small((GMLP_HEADS, GMLP_BLOCK, GMLP_BLOCK)), small((GMLP_HEADS, GMLP_BLOCK, 1)),
            small((1, hd)), small((1, hd)), small((4, gd, gd)), small((1, 1024)), small((1, 1024)),
        ],
        [
            pl.BlockSpec((tm, 5120), lambda i: (i, 0)),
            small((GMLP_HEADS, GMLP_BLOCK, GMLP_BLOCK)), small((GMLP_HEADS, GMLP_BLOCK, 1)),
            small((1, hd)), small((1, hd)), small((4, gd, gd)), small((1, 1024)), small((1, 1024)),
        ],
        [
            jax.ShapeDtypeStruct((s, 5120), BF16),
            jax.ShapeDtypeStruct((GMLP_HEADS, GMLP_BLOCK, GMLP_BLOCK), F32),
            jax.ShapeDtypeStruct((GMLP_HEADS, GMLP_BLOCK, 1), F32),
            jax.ShapeDtypeStruct((1, hd), F32), jax.ShapeDtypeStruct((1, hd), F32),
            jax.ShapeDtypeStruct((4, gd, gd), F32),
            jax.ShapeDtypeStruct((1, 1024), F32), jax.ShapeDtypeStruct((1, 1024), F32),
        ],
        [proj, proj, proj, dm, dm, ws, bs_col, ng, nb, pw, pb, ps], side=side)


def _rope_pair_swap(t):
    lane = lax.broadcasted_iota(jnp.int32, t.shape, 1)
    return jnp.where(lane % 64 < 32, pltpu.roll(t, 96, 1), pltpu.roll(t, 32, 1))


def _rms(x, g):
    r = lax.rsqrt(jnp.mean(x * x, axis=-1, keepdims=True) + LN_EPS)
    return x * r, r


def _rms_bwd(dy, g, xhat, r):
    dyg = dy * g
    return r * (dyg - xhat * jnp.mean(dyg * xhat, axis=-1, keepdims=True))


def _lane_lt(shape, n):
    return lax.broadcasted_iota(jnp.int32, shape, 1) < n


def _mla_prep(proj, cosp, sinp, gq, gkv, name):
    s = proj.shape[0]
    tm = min(s, 512)

    def body(qc_ref, kv_ref, c_ref, s_ref, gq_ref, gkv_ref, qn_ref, kp_ref):
        qhat, _ = _rms(qc_ref[...].astype(F32), None)
        qn_ref[...] = (qhat * gq_ref[...]).astype(BF16)
        khat, _ = _rms(kv_ref[:, 0:128].astype(F32), None)
        kp_ref[:, 0:128] = (khat * gkv_ref[...]).astype(BF16)
        kr = kv_ref[:, 128:256].astype(F32)
        kp_ref[:, 128:256] = (kr * c_ref[...] + _rope_pair_swap(kr) * s_ref[...]).astype(BF16)

    return pl.pallas_call(
        body, name=name, grid=(s // tm,),
        in_specs=[pl.BlockSpec((tm, 256), lambda i: (i, 0)), pl.BlockSpec((tm, 256), lambda i: (i, 1)),
                  pl.BlockSpec((tm, 128), lambda i: (i, 0)), pl.BlockSpec((tm, 128), lambda i: (i, 0)),
                  pl.BlockSpec((1, 256), lambda i: (0, 0)), pl.BlockSpec((1, 128), lambda i: (0, 0))],
        out_specs=[pl.BlockSpec((tm, 256), lambda i: (i, 0)), pl.BlockSpec((tm, 256), lambda i: (i, 0))],
        out_shape=[jax.ShapeDtypeStruct((s, 256), BF16), jax.ShapeDtypeStruct((s, 256), BF16)],
        compiler_params=_params(1),
    )(proj, proj, cosp, sinp, gq, gkv)


def _mla_prep_bwd(proj, dqn, dkp, dv, cosp, sinp, gq, gkv, dproj, name):
    s = proj.shape[0]
    tm = min(s, 512)

    def body(qc_ref, kv_ref, dqn_ref, dkp_ref, dv_ref, c_ref, s_ref, gq_ref, gkv_ref, dproj_ref, o_ref, red_ref):
        @pl.when(pl.program_id(0) == 0)
        def _():
            red_ref[...] = jnp.zeros_like(red_ref)

        qhat, qr = _rms(qc_ref[...].astype(F32), None)
        dq = dqn_ref[...]
        o_ref[:, 0:256] = _rms_bwd(dq, gq_ref[...], qhat, qr).astype(BF16)
        red_ref[0:1, :] += _colsum(dq * qhat)
        khat, kr = _rms(kv_ref[:, 0:128].astype(F32), None)
        dk = dkp_ref[:, 0:128] + dv_ref[...]
        o_ref[:, 256:384] = _rms_bwd(dk, gkv_ref[...], khat, kr).astype(BF16)
        red_ref[1:2, 0:128] += _colsum(dk * khat)
        dr = dkp_ref[:, 128:256]
        o_ref[:, 384:512] = (dr * c_ref[...] - _rope_pair_swap(dr) * s_ref[...]).astype(BF16)

    return pl.pallas_call(
        body, name=name, grid=(s // tm,),
        in_specs=[pl.BlockSpec((tm, 256), lambda i: (i, 0)), pl.BlockSpec((tm, 256), lambda i: (i, 1)),
                  pl.BlockSpec((tm, 256), lambda i: (i, 0)), pl.BlockSpec((tm, 256), lambda i: (i, 0)),
                  pl.BlockSpec((tm, 128), lambda i: (i, 0)),
                  pl.BlockSpec((tm, 128), lambda i: (i, 0)), pl.BlockSpec((tm, 128), lambda i: (i, 0)),
                  pl.BlockSpec((1, 256), lambda i: (0, 0)), pl.BlockSpec((1, 128), lambda i: (0, 0)),
                  pl.BlockSpec(memory_space=pl.ANY)],
        out_specs=[pl.BlockSpec((tm, 512), lambda i: (i, 0)), pl.BlockSpec((8, 256), lambda i: (0, 0))],
        out_shape=[jax.ShapeDtypeStruct(dproj.shape, BF16), jax.ShapeDtypeStruct((8, 256), F32)],
        input_output_aliases={9: 0}, compiler_params=_params(1),
    )(proj, proj, dqn, dkp, dv, cosp, sinp, gq, gkv, dproj)


HEADS_TM = 256
Z_COL0 = ODD_IN_PAD - MLA_WIDTH


def _head_cols(h):
    return slice(128 * h, 128 * h + 128)


def _q_heads(q_up, cosp, sinp, wuk, name):
    s = q_up.shape[0]
    tm = min(s, HEADS_TM)

    def body(q_ref, c_ref, s_ref, w_ref, o_ref):
        for p in range(MLA_HEADS // 2):
            raw = q_ref[:, MLA_WIDTH + 128 * p:MLA_WIDTH + 128 * (p + 1)].astype(F32)
            rot = raw * c_ref[...] + _rope_pair_swap(raw) * s_ref[...]
            low = _lane_lt(rot.shape, 64)
            o_ref[2 * p, :, 128:256] = jnp.where(low, rot, 0.0).astype(BF16)
            o_ref[2 * p + 1, :, 128:256] = jnp.where(low, pltpu.roll(rot, 64, 1), 0.0).astype(BF16)
        for h in range(MLA_HEADS):
            o_ref[h, :, 0:128] = _dot(q_ref[:, _head_cols(h)], w_ref[:, _head_cols(h)], NT).astype(BF16)

    return pl.pallas_call(
        body, name=name, grid=(s // tm,),
        in_specs=[pl.BlockSpec((tm, 3072), lambda i: (i, 0)),
                  pl.BlockSpec((tm, 128), lambda i: (i, 0)), pl.BlockSpec((tm, 128), lambda i: (i, 0)),
                  pl.BlockSpec((128, MLA_WIDTH), lambda i: (0, 0))],
        out_specs=pl.BlockSpec((MLA_HEADS, tm, 256), lambda i: (0, i, 0)),
        out_shape=jax.ShapeDtypeStruct((MLA_HEADS, s, 256), BF16), compiler_params=_params(1),
    )(q_up, cosp, sinp, wuk)


def _q_heads_bwd(dqp, q_up, cosp, sinp, wuk, name):
    s = q_up.shape[0]
    tm = min(s, HEADS_TM)

    def body(dq_ref, qn_ref, c_ref, s_ref, w_ref, dn_ref, dr_ref, dw_ref):
        @pl.when(pl.program_id(0) == 0)
        def _():
            dw_ref[...] = jnp.zeros_like(dw_ref)

        for h in range(MLA_HEADS):
            dlat = dq_ref[h, :, 0:128]
            dn_ref[:, _head_cols(h)] = _dot(dlat, w_ref[:, _head_cols(h)], NN).astype(BF16)
            dw_ref[:, _head_cols(h)] += _dot(dlat, qn_ref[:, _head_cols(h)], TN)
        for p in range(MLA_HEADS // 2):
            drot = dq_ref[2 * p, :, 128:256].astype(F32) + pltpu.roll(dq_ref[2 * p + 1, :, 128:256].astype(F32), 64, 1)
            dr_ref[:, _head_cols(p)] = (drot * c_ref[...] - _rope_pair_swap(drot) * s_ref[...]).astype(BF16)

    return pl.pallas_call(
        body, name=name, grid=(s // tm,),
        in_specs=[pl.BlockSpec((MLA_HEADS, tm, 256), lambda i: (0, i, 0)),
                  pl.BlockSpec((tm, MLA_WIDTH), lambda i: (i, 0)),
                  pl.BlockSpec((tm, 128), lambda i: (i, 0)), pl.BlockSpec((tm, 128), lambda i: (i, 0)),
                  pl.BlockSpec((128, MLA_WIDTH), lambda i: (0, 0))],
        out_specs=[pl.BlockSpec((tm, MLA_WIDTH), lambda i: (i, 0)),
                   pl.BlockSpec((tm, 1024), lambda i: (i, 0)),
                   pl.BlockSpec((128, MLA_WIDTH), lambda i: (0, 0))],
        out_shape=[jax.ShapeDtypeStruct((s, MLA_WIDTH), BF16), jax.ShapeDtypeStruct((s, 1024), BF16),
                   jax.ShapeDtypeStruct((128, MLA_WIDTH), F32)],
        compiler_params=_params(1),
    )(dqp, q_up, cosp, sinp, wuk)


def _o_gate(o_lat, proj, wuv, name):
    s = o_lat.shape[1]
    tm = min(s, HEADS_TM)

    def body(ol_ref, p_ref, w_ref, g_ref):
        for h in range(MLA_HEADS):
            z = p_ref[:, Z_COL0 + 128 * h:Z_COL0 + 128 * (h + 1)].astype(F32)
            g_ref[:, _head_cols(h)] = (_dot(ol_ref[h], w_ref[:, _head_cols(h)], NN) * (z * _sigmoid(z))).astype(BF16)

    return pl.pallas_call(
        body, name=name, grid=(s // tm,),
        in_specs=[pl.BlockSpec((MLA_HEADS, tm, 128), lambda i: (0, i, 0)),
                  pl.BlockSpec((tm, ODD_IN_PAD), lambda i: (i, 0)),
                  pl.BlockSpec((128, MLA_WIDTH), lambda i: (0, 0))],
        out_specs=pl.BlockSpec((tm, MLA_WIDTH), lambda i: (i, 0)),
        out_shape=jax.ShapeDtypeStruct((s, MLA_WIDTH), BF16), compiler_params=_params(1),
    )(o_lat, proj, wuv)


def _o_gate_bwd(dg, o_lat, proj, wuv, name):
    s = o_lat.shape[1]
    tm = min(s, HEADS_TM)

    def body(dg_ref, ol_ref, p_ref, w_ref, dp_ref, dol_ref, dw_ref):
        @pl.when(pl.program_id(0) == 0)
        def _():
            dw_ref[...] = jnp.zeros_like(dw_ref)

        dp_ref[:, 0:Z_COL0] = jnp.zeros((tm, Z_COL0), BF16)
        for h in range(MLA_HEADS):
            zc = slice(Z_COL0 + 128 * h, Z_COL0 + 128 * (h + 1))
            z, dgv, ol = p_ref[:, zc].astype(F32), dg_ref[:, _head_cols(h)].astype(F32), ol_ref[h]
            sig = _sigmoid(z)
            o = _dot(ol, w_ref[:, _head_cols(h)], NN)
            dp_ref[:, zc] = (dgv * o * (sig * (1.0 + z * (1.0 - sig)))).astype(BF16)
            do = (dgv * (z * sig)).astype(BF16)
            dol_ref[h] = _dot(do, w_ref[:, _head_cols(h)], NT).astype(BF16)
            dw_ref[:, _head_cols(h)] += _dot(ol, do, TN)

    return pl.pallas_call(
        body, name=name, grid=(s // tm,),
        in_specs=[pl.BlockSpec((tm, MLA_WIDTH), lambda i: (i, 0)),
                  pl.BlockSpec((MLA_HEADS, tm, 128), lambda i: (0, i, 0)),
                  pl.BlockSpec((tm, ODD_IN_PAD), lambda i: (i, 0)),
                  pl.BlockSpec((128, MLA_WIDTH), lambda i: (0, 0))],
        out_specs=[pl.BlockSpec((tm, ODD_IN_PAD), lambda i: (i, 0)),
                   pl.BlockSpec((MLA_HEADS, tm, 128), lambda i: (0, i, 0)),
                   pl.BlockSpec((128, MLA_WIDTH), lambda i: (0, 0))],
        out_shape=[jax.ShapeDtypeStruct((s, ODD_IN_PAD), BF16), jax.ShapeDtypeStruct((MLA_HEADS, s, 128), BF16),
                   jax.ShapeDtypeStruct((128, MLA_WIDTH), F32)],
        compiler_params=_params(1),
    )(dg, o_lat, proj, wuv)


ATT_TQ = CHUNK
ATT_ROWS = ATT_TQ * MLA_HEADS
ATT_TK = 512
ATT_HEAD_GROUP = 4


def _visible(k0, q_chunk, tk):
    kpos = k0 + lax.broadcasted_iota(jnp.int32, (1, tk), 1)
    return kpos // CHUNK <= q_chunk


def _tile_lanes(t, n):
    return jnp.concatenate([t] * (n // 128), axis=1)


def _key_blocks(i, tk, block, pairs=False):
    n_full = (i * ATT_TQ + ATT_TQ + tk - 1) // tk - 1
    if pairs:
        def two(jj, carry):
            block(2 * jj, False)
            block(2 * jj + 1, False)
            return carry

        lax.fori_loop(0, n_full // 2, two, 0)

        @pl.when(n_full % 2 == 1)
        def _():
            block(n_full - 1, False)
    else:
        def one(j, carry):
            block(j, False)
            return carry

        lax.fori_loop(0, n_full, one, 0)
    block(n_full, True)


def _attn_fwd(qp, kp, name, side=None):
    s = kp.shape[0]
    tk = min(ATT_TK, s)

    def body(q_ref, k_ref, o_ref, lse_ref, m_sc, acc_sc):
        i = pl.program_id(0)
        m_sc[...] = jnp.full_like(m_sc, -jnp.inf)
        acc_sc[...] = jnp.zeros_like(acc_sc)

        def block(j, masked):
            k0 = pl.multiple_of(j * tk, tk)
            k = k_ref[pl.ds(k0, tk), :]
            v1 = jnp.where(_lane_lt(k.shape, 128), k, jnp.ones_like(k))
            for h0 in range(0, MLA_HEADS, ATT_HEAD_GROUP):
                rows = slice(h0 * ATT_TQ, (h0 + ATT_HEAD_GROUP) * ATT_TQ)
                q = q_ref[h0:h0 + ATT_HEAD_GROUP].reshape(ATT_HEAD_GROUP * ATT_TQ, 256)
                sc = _dot(q, k, NT) * ATTN_SCALE_LOG2
                if masked:
                    sc = jnp.where(_visible(k0, i, tk), sc, -jnp.inf)
                m_prev = m_sc[rows]
                m_new = jnp.maximum(m_prev, jnp.max(sc, axis=1, keepdims=True))
                p = jnp.exp2(sc - _tile_lanes(m_new, tk))
                acc_sc[rows] = _tile_lanes(jnp.exp2(m_prev - m_new), 256) * acc_sc[rows] + _dot(p, v1, NN)
                m_sc[rows] = m_new

        _key_blocks(i, tk, block, pairs=True)
        acc = acc_sc[...]
        l = acc[:, 128:256]
        o_ref[...] = (acc[:, 0:128] / l).astype(BF16).reshape(MLA_HEADS, ATT_TQ, 128)
        lse_ref[...] = (m_sc[...] + jnp.log2(l)).reshape(MLA_HEADS, ATT_TQ, 128)

    head128 = pl.BlockSpec((MLA_HEADS, ATT_TQ, 128), lambda i: (0, i, 0))
    return _grid_call(
        body, name, (s // ATT_TQ,),
        [pl.BlockSpec((MLA_HEADS, ATT_TQ, 256), lambda i: (0, i, 0)), pl.BlockSpec((s, 256), lambda i: (0, 0))],
        [head128, head128],
        [jax.ShapeDtypeStruct((MLA_HEADS, s, 128), BF16), jax.ShapeDtypeStruct((MLA_HEADS, s, 128), F32)],
        [qp, kp], scratch=[pltpu.VMEM((ATT_ROWS, 128), F32), pltpu.VMEM((ATT_ROWS, 256), F32)], side=side)


def _attn_bwd(qp, kp, o, do, lse, name, side=None):
    s = kp.shape[0]
    tk = min(ATT_TK, s)

    def body(q_ref, k_ref, o_ref, do_ref, lse_ref, dq_ref, dk_ref, dv_ref, dq_sc):
        i = pl.program_id(0)

        @pl.when(i == 0)
        def _():
            dk_ref[...] = jnp.zeros_like(dk_ref)
            dv_ref[...] = jnp.zeros_like(dv_ref)

        q = q_ref[...].reshape(ATT_ROWS, 256)
        dov = do_ref[...].reshape(ATT_ROWS, 128)
        delta = jnp.sum(dov.astype(F32) * o_ref[...].reshape(ATT_ROWS, 128).astype(F32), axis=1, keepdims=True)
        delta_t = _tile_lanes(jnp.broadcast_to(delta, (ATT_ROWS, 128)), tk)
        lse_t = _tile_lanes(lse_ref[...].reshape(ATT_ROWS, 128), tk)
        dq_sc[...] = jnp.zeros_like(dq_sc)

        def block(j, masked):
            k0 = pl.multiple_of(j * tk, tk)
            k = k_ref[pl.ds(k0, tk), :]
            p = jnp.exp2(_dot(q, k, NT) * ATTN_SCALE_LOG2 - lse_t)
            if masked:
                p = jnp.where(_visible(k0, i, tk), p, 0.0)
            dv_ref[pl.ds(k0, tk), :] += _dot(p, dov, TN)
            ds = (p * (_dot(dov, k[:, 0:128], NT) - delta_t) * ATTN_SCALE).astype(BF16)
            dq_sc[...] += _dot(ds, k, NN)
            dk_ref[pl.ds(k0, tk), :] += _dot(ds, q, TN)

        _key_blocks(i, tk, block, pairs=True)
        dq_ref[...] = dq_sc[...].astype(BF16).reshape(MLA_HEADS, ATT_TQ, 256)

    head128 = pl.BlockSpec((MLA_HEADS, ATT_TQ, 128), lambda i: (0, i, 0))
    head256 = pl.BlockSpec((MLA_HEADS, ATT_TQ, 256), lambda i: (0, i, 0))
    return _grid_call(
        body, name, (s // ATT_TQ,),
        [head256, pl.BlockSpec((s, 256), lambda i: (0, 0)), head128, head128, head128],
        [head256, pl.BlockSpec((s, 256), lambda i: (0, 0)), pl.BlockSpec((s, 128), lambda i: (0, 0))],
        [jax.ShapeDtypeStruct((MLA_HEADS, s, 256), BF16),
         jax.ShapeDtypeStruct((s, 256), F32), jax.ShapeDtypeStruct((s, 128), F32)],
        [qp, kp, o, do, lse], scratch=[pltpu.VMEM((ATT_ROWS, 256), F32)], side=side)


def _place():
    x, y, c = lax.axis_index("x"), lax.axis_index("y"), lax.axis_index("c")
    return x, y, c, 4 * x + 2 * y + c


def _flip(x, y, c, r):
    px = 1 - x if r & 4 else x
    py = 1 - y if r & 2 else y
    pc = 1 - c if r & 1 else c
    return (px, py, pc), 4 * px + 2 * py + pc


def _adaln_exchange(c8, ada_w, ada_b_cols, name):
    d = c8.shape[1]
    w_cols = ada_w.shape[2]

    def body(c_ref, w_ref, b_ref, call_ref, mod_ref, sbuf, rbuf, s1, r1, s2, r2):
        x, y, c, me = _place()
        call_ref[pl.ds(pl.multiple_of(me * 8, 8), 8), :] = c_ref[...]
        peers = [_flip(x, y, c, r) for r in range(1, N_DEV)]

        def c_copy(k, src_lin, to):
            rows = call_ref.at[pl.ds(pl.multiple_of(src_lin * 8, 8), 8), :]
            return pltpu.make_async_remote_copy(src_ref=rows, dst_ref=rows, send_sem=s1.at[k], recv_sem=r1.at[k],
                                                device_id=to, device_id_type=MESH)

        first = [c_copy(k, me, peer) for k, (peer, _) in enumerate(peers)]
        for cp in first:
            cp.start()
        for k, (_, lin) in enumerate(peers):
            c_copy(k, lin, (x, y, c)).wait_recv()
        for cp in first:
            cp.wait_send()

        for j in range(N_DEV):
            cj = call_ref[8 * j:8 * j + 8, :]
            cond = cj * _sigmoid(cj)
            for l in range(2):
                sbuf[j, l] = lax.dot_general(cond, w_ref[l], NN, precision=lax.Precision.HIGHEST,
                                             preferred_element_type=F32) + b_ref[l]

        def m_copy(k, src_slot, dst_slot, to):
            return pltpu.make_async_remote_copy(src_ref=sbuf.at[src_slot], dst_ref=rbuf.at[dst_slot],
                                                send_sem=s2.at[k], recv_sem=r2.at[k], device_id=to,
                                                device_id_type=MESH)

        rbuf[me] = sbuf[me]
        second = [m_copy(k, lin, me, peer) for k, (peer, lin) in enumerate(peers)]
        for cp in second:
            cp.start()
        for k, (_, lin) in enumerate(peers):
            m_copy(k, lin, lin, (x, y, c)).wait_recv()
        for cp in second:
            cp.wait_send()
        for j in range(N_DEV):
            for l in range(2):
                mod_ref[l, :, w_cols * j:w_cols * (j + 1)] = rbuf[j, l]

    vmem = pl.BlockSpec(memory_space=pltpu.VMEM)
    return pl.pallas_call(
        body, name=name, in_specs=[vmem, vmem, vmem], out_specs=[vmem, vmem],
        out_shape=[jax.ShapeDtypeStruct((8 * N_DEV, d), F32), jax.ShapeDtypeStruct((2, 8, 3 * d), F32)],
        scratch_shapes=[pltpu.VMEM((N_DEV, 2, 8, w_cols), F32), pltpu.VMEM((N_DEV, 2, 8, w_cols), F32),
                        pltpu.SemaphoreType.DMA((N_DEV - 1,)), pltpu.SemaphoreType.DMA((N_DEV - 1,)),
                        pltpu.SemaphoreType.DMA((N_DEV - 1,)), pltpu.SemaphoreType.DMA((N_DEV - 1,))],
        compiler_params=pltpu.CompilerParams(vmem_limit_bytes=VMEM_LIMIT),
    )(c8, ada_w, ada_b_cols)


def _all_gather(blocks, name):
    n_arr = len(blocks)

    def body(*refs):
        x_refs, out_refs = refs[:n_arr], refs[n_arr:2 * n_arr]
        send_sems, recv_sems, local_sems = refs[2 * n_arr:]
        x, y, c, _ = _place()
        me, sibling = (x, y, c), (x, y, 1 - c)
        chips = [(1 - x, y), (x, 1 - y), (1 - x, 1 - y)]

        def copy(t, k, blk, to, src=None):
            slot = out_refs[t].at[4 * blk[0] + 2 * blk[1] + blk[2]]
            return pltpu.make_async_remote_copy(src_ref=slot if src is None else src, dst_ref=slot,
                                                send_sem=send_sems.at[7 * t + k], recv_sem=recv_sems.at[7 * t + k],
                                                device_id=to, device_id_type=MESH)

        mine = [pltpu.make_async_copy(x_refs[t], out_refs[t].at[4 * x + 2 * y + c], local_sems.at[t])
                for t in range(n_arr)]
        for cp in mine:
            cp.start()
        first = []
        for t in range(n_arr):
            first.append(copy(t, 0, me, sibling, src=x_refs[t]))
            first += [copy(t, 1 + j, me, (*chip, c), src=x_refs[t]) for j, chip in enumerate(chips)]
        for cp in first:
            cp.start()
        passed = []
        for t in range(n_arr):
            for j, chip in enumerate(chips):
                copy(t, 1 + j, (*chip, c), me).wait_recv()
                passed.append(copy(t, 4 + j, (*chip, c), sibling))
                passed[-1].start()
        for t in range(n_arr):
            copy(t, 0, sibling, me).wait_recv()
            for j, chip in enumerate(chips):
                copy(t, 4 + j, (*chip, 1 - c), me).wait_recv()
        for cp in first + passed:
            cp.wait_send()
        for cp in mine:
            cp.wait()

    anyspace = pl.BlockSpec(memory_space=pl.ANY)
    return pl.pallas_call(
        body, name=name, in_specs=[anyspace] * n_arr, out_specs=[anyspace] * n_arr,
        out_shape=[jax.ShapeDtypeStruct((N_DEV,) + b.shape, b.dtype) for b in blocks],
        scratch_shapes=[pltpu.SemaphoreType.DMA((7 * n_arr,)), pltpu.SemaphoreType.DMA((7 * n_arr,)),
                        pltpu.SemaphoreType.DMA((n_arr,))],
    )(*blocks)


def _scatter_parts(parts, name):
    n_arr = len(parts)

    def body(*refs):
        copies = _exchange_copies(refs[:n_arr], refs[n_arr:2 * n_arr], *refs[2 * n_arr:], False)
        _exchange_start(copies)
        _exchange_wait(copies)

    anyspace = pl.BlockSpec(memory_space=pl.ANY)
    shapes, sems = _exchange_extras(parts, False)
    return pl.pallas_call(body, name=name, in_specs=[anyspace] * n_arr, out_specs=[anyspace] * n_arr,
                          out_shape=shapes, scratch_shapes=sems)(*parts)


def _adamw(w, g, m, v):
    m = ADAM_B1 * m + (1.0 - ADAM_B1) * g
    v = ADAM_B2 * v + (1.0 - ADAM_B2) * (g * g)
    m_hat = m / (1.0 - ADAM_B1 ** ADAM_STEP)
    v_hat = v / (1.0 - ADAM_B2 ** ADAM_STEP)
    return -ADAM_LR * (m_hat / (jnp.sqrt(v_hat) + ADAM_EPS) + ADAM_WD * w), m, v


def _sum_parts_adamw(parts, w, m, v, name):
    _, rows, cols = parts.shape
    tr = max(t for t in range(16, 129, 16) if rows % t == 0)

    def body(p_ref, w_ref, m_ref, v_ref, g_ref, d_ref, mo_ref, vo_ref):
        g = p_ref[0].astype(F32)
        for j in range(1, N_DEV):
            g = g + p_ref[j].astype(F32)
        g_ref[...] = g
        d_ref[...], mo_ref[...], vo_ref[...] = _adamw(w_ref[...], g, m_ref[...], v_ref[...])

    row = _rows3(tr, cols)
    out = jax.ShapeDtypeStruct((1, rows, cols), F32)
    return pl.pallas_call(
        body, name=name, grid=(rows // tr,),
        in_specs=[pl.BlockSpec((N_DEV, tr, cols), lambda i: (0, i, 0)), row, row, row],
        out_specs=[row, row, row, row], out_shape=[out, out, out, out], compiler_params=_params(1),
    )(parts, w, m, v)


def _adamw_call(g, w, m, v, name):
    rows, cols = g.shape
    tr = 128 if rows % 128 == 0 else rows

    def body(g_ref, w_ref, m_ref, v_ref, d_ref, mo_ref, vo_ref):
        d_ref[...], mo_ref[...], vo_ref[...] = _adamw(w_ref[...], g_ref[...], m_ref[...], v_ref[...])

    row = pl.BlockSpec((tr, cols), lambda i: (i, 0))
    out = jax.ShapeDtypeStruct((rows, cols), F32)
    return pl.pallas_call(
        body, name=name, grid=(rows // tr,), in_specs=[row] * 4, out_specs=[row] * 3, out_shape=[out] * 3,
        compiler_params=_params(1),
    )(g, w, m, v)


def _ada_w_grad_adamw(c_all, dmod_rows, w, m, v, name):
    def body(c_ref, dm_ref, w_ref, m_ref, v_ref, g_ref, d_ref, mo_ref, vo_ref):
        cv = c_ref[...]
        cond = cv * _sigmoid(cv)
        for l in range(2):
            g = lax.dot_general(cond, dm_ref[l], TN, precision=lax.Precision.HIGHEST, preferred_element_type=F32)
            g_ref[l] = g
            d_ref[l], mo_ref[l], vo_ref[l] = _adamw(w_ref[l], g, m_ref[l], v_ref[l])

    out = jax.ShapeDtypeStruct(w.shape, F32)
    return pl.pallas_call(
        body, name=name, out_shape=[out] * 4, compiler_params=pltpu.CompilerParams(vmem_limit_bytes=VMEM_LIMIT),
    )(c_all, dmod_rows, w, m, v)


REPLICATED = ("ln_g", "ln_b", "gmlp_norm_g", "gmlp_norm_b", "gmlp_ws", "gmlp_bs", "pool_b", "pool_scale",
              "mla_kv_norm_g", "mla_w_uk", "mla_w_uv")
CHUNK_ROWS, ADA_ROW, QNORM_ROW, LOSS_ROW, REP_ROWS = 73, 73, 74, 75, 80
UQ_ROWS, POOLW_ROWS = 96, 32


def _pad_rows(flat2d, rows):
    n, k = flat2d.shape
    return jnp.pad(flat2d, ((0, 0), (0, rows * LANES - k))).reshape(n, rows, LANES)


def _ada_cols_rows(vec):
    return _pad_rows(vec.reshape(2, N_DEV, -1).transpose(1, 0, 2).reshape(N_DEV, -1), 1)


def _pack_replicated(src, ada_vec):
    flat = jnp.concatenate([src[n].reshape(-1) for n in REPLICATED])
    body = _pad_rows(flat.reshape(N_DEV, -1), CHUNK_ROWS)
    return jnp.concatenate([body, jnp.pad(_ada_cols_rows(ada_vec), ((0, 0), (0, REP_ROWS - CHUNK_ROWS - 1), (0, 0)))],
                           axis=1)


def _unpack_replicated(rep, shapes):
    chunk = sum(s[1] for s in shapes) // N_DEV
    flat, off, out = rep[:, :CHUNK_ROWS].reshape(N_DEV, -1)[:, :chunk].reshape(-1), 0, {}
    for n, size, shape in shapes:
        out[n] = flat[off:off + size].reshape(shape)
        off += size
    cols = 3 * D_MODEL // N_DEV
    out["ada_b"] = rep[:, ADA_ROW, :2 * cols].reshape(N_DEV, 2, cols).transpose(1, 0, 2).reshape(2, -1)
    return out


def kernel(x, c, positions, ada_w, ada_b, ln_g, ln_b, e_w_in, gmlp_norm_g, gmlp_norm_b, gmlp_ws, gmlp_bs, pool_w, pool_b, pool_scale, e_w_out, o_w_in, mla_q_norm_g, mla_kv_norm_g, mla_w_uq, mla_w_uk, mla_w_uv, o_w_out, loss_target, m_ada_w, m_ada_b, m_ln_g, m_ln_b, m_e_w_in, m_gmlp_norm_g, m_gmlp_norm_b, m_gmlp_ws, m_gmlp_bs, m_pool_w, m_pool_b, m_pool_scale, m_e_w_out, m_o_w_in, m_mla_q_norm_g, m_mla_kv_norm_g, m_mla_w_uq, m_mla_w_uk, m_mla_w_uv, m_o_w_out, v_ada_w, v_ada_b, v_ln_g, v_ln_b, v_e_w_in, v_gmlp_norm_g, v_gmlp_norm_b, v_gmlp_ws, v_gmlp_bs, v_pool_w, v_pool_b, v_pool_scale, v_e_w_out, v_o_w_in, v_mla_q_norm_g, v_mla_kv_norm_g, v_mla_w_uq, v_mla_w_uk, v_mla_w_uv, v_o_w_out):
    w_in = dict(ada_w=ada_w, ada_b=ada_b, ln_g=ln_g, ln_b=ln_b, e_w_in=e_w_in, gmlp_norm_g=gmlp_norm_g,
                gmlp_norm_b=gmlp_norm_b, gmlp_ws=gmlp_ws, gmlp_bs=gmlp_bs, pool_w=pool_w, pool_b=pool_b,
                pool_scale=pool_scale, e_w_out=e_w_out, o_w_in=o_w_in, mla_q_norm_g=mla_q_norm_g,
                mla_kv_norm_g=mla_kv_norm_g, mla_w_uq=mla_w_uq, mla_w_uk=mla_w_uk, mla_w_uv=mla_w_uv, o_w_out=o_w_out)
    m_in = dict(ada_w=m_ada_w, ada_b=m_ada_b, ln_g=m_ln_g, ln_b=m_ln_b, e_w_in=m_e_w_in, gmlp_norm_g=m_gmlp_norm_g,
                gmlp_norm_b=m_gmlp_norm_b, gmlp_ws=m_gmlp_ws, gmlp_bs=m_gmlp_bs, pool_w=m_pool_w, pool_b=m_pool_b,
                pool_scale=m_pool_scale, e_w_out=m_e_w_out, o_w_in=m_o_w_in, mla_q_norm_g=m_mla_q_norm_g,
                mla_kv_norm_g=m_mla_kv_norm_g, mla_w_uq=m_mla_w_uq, mla_w_uk=m_mla_w_uk, mla_w_uv=m_mla_w_uv,
                o_w_out=m_o_w_out)
    v_in = dict(ada_w=v_ada_w, ada_b=v_ada_b, ln_g=v_ln_g, ln_b=v_ln_b, e_w_in=v_e_w_in, gmlp_norm_g=v_gmlp_norm_g,
                gmlp_norm_b=v_gmlp_norm_b, gmlp_ws=v_gmlp_ws, gmlp_bs=v_gmlp_bs, pool_w=v_pool_w, pool_b=v_pool_b,
                pool_scale=v_pool_scale, e_w_out=v_e_w_out, o_w_in=v_o_w_in, mla_q_norm_g=v_mla_q_norm_g,
                mla_kv_norm_g=v_mla_kv_norm_g, mla_w_uq=v_mla_w_uq, mla_w_uk=v_mla_w_uk, mla_w_uv=v_mla_w_uv,
                o_w_out=v_o_w_out)
    names = list(w_in)
    seq = x.shape[1]
    d = D_MODEL
    me = 4 * lax.axis_index("x") + 2 * lax.axis_index("y") + lax.axis_index("c")
    ada_cols = ada_w.shape[2]

    ada_b_cols = lax.dynamic_slice_in_dim(ada_b, me * ada_cols, ada_cols, axis=1)
    slab_row = lax.broadcasted_iota(jnp.int32, (8, d), 0)
    slab = jnp.where(slab_row == 0, c, jnp.where(slab_row == 1, jnp.pad(mla_q_norm_g, ((0, 0), (0, d - 32))), 0.0))
    c_all, mod = _adaln_exchange(slab, ada_w,
                                 jnp.broadcast_to(ada_b_cols[:, None, :], (2, 8, ada_cols)), "adaln_exchange")

    w_in_e3, pool_w3 = _all_gather(
        [e_w_in[0].astype(BF16), pool_w.astype(BF16).reshape(POOLW_ROWS, LANES)], "weight_gather")
    h0 = _modulate(x, mod[0], "modulate0")
    proj0, o_in3 = _matmul_cols_nn(h0, w_in_e3, BF16, 512, "even_in", side=([o_w_in[0].astype(BF16)], True))
    o_in_full = o_in3.transpose(1, 0, 2).reshape(d, ODD_IN)
    w_in_o = jnp.concatenate([o_in_full[:, :448], jnp.zeros((d, 64), BF16), o_in_full[:, 448:]], axis=1)
    pool_w_full = pool_w3.reshape(N_DEV, 4, 32, 256).transpose(1, 0, 2, 3).reshape(4, 256, 256)
    g_q = c_all.reshape(N_DEV, 8, d)[:, 1, :32].reshape(1, MLA_Q_RANK)

    ws, bs_col = gmlp_ws[0], gmlp_bs[0].reshape(GMLP_HEADS, GMLP_BLOCK, 1)
    wuk2, wuv2 = mla_w_uk[0].reshape(MLA_KV_RANK, -1), mla_w_uv[0].reshape(MLA_KV_RANK, -1)
    inv = 1.0 / (ROPE_THETA ** (jnp.arange(0, MLA_ROPE, 2, dtype=F32) / MLA_ROPE))
    ang = positions[0].astype(F32)[:, None] * inv
    cosp = jnp.tile(jnp.cos(ang), (1, 4))
    sinp = jnp.tile(jnp.concatenate([-jnp.sin(ang), jnp.sin(ang)], axis=1), (1, 2))

    mix0, w_out_e3 = _even_fwd(proj0, ws, bs_col, gmlp_norm_g, gmlp_norm_b, pool_w_full, pool_b, pool_scale, "even_mix",
                               side=([e_w_out[0].astype(BF16)], True))
    w_out_e = w_out_e3.reshape(-1, d)
    y0, uq3 = _matmul([(mix0, w_out_e)], "nn", F32, seq, d, 512, 1024, "even_out",
                      side=([mla_w_uq.astype(BF16).reshape(UQ_ROWS, LANES)], True))
    uq_full = uq3.reshape(MLA_Q_RANK, MLA_HEADS, MLA_NOPE + MLA_ROPE)
    w_uq_n = uq_full[:, :, :MLA_NOPE].reshape(MLA_Q_RANK, -1)
    w_uq_r = uq_full[:, :, MLA_NOPE:].reshape(MLA_Q_RANK, -1)
    w_uq = jnp.concatenate([w_uq_n, w_uq_r], axis=1)
    x1, h1 = _resid_ln(x, y0, mod[0], ln_g[0:1], ln_b[0:1], mod[1], "resid_ln0")

    (proj1,) = _matmul([(h1, w_in_o)], "nn", BF16, seq, ODD_IN_PAD, 512, ODD_IN_PAD, "odd_in")
    qn, kp = _mla_prep(proj1, cosp, sinp, g_q, mla_kv_norm_g, "mla_prep")
    (q_up,) = _matmul([(qn, w_uq)], "nn", BF16, seq, 3072, 512, 3072, "q_up")
    qp = _q_heads(q_up, cosp, sinp, wuk2, "q_heads")
    o_lat, lse, w_out_o3 = _attn_fwd(qp, kp, "attn_fwd", side=([o_w_out[0].astype(BF16)], True))
    w_out_o = w_out_o3.reshape(-1, d)
    gated = _o_gate(o_lat, proj1, wuv2, "o_gate")
    (y1,) = _matmul([(gated, w_out_o)], "nn", F32, seq, d, 512, 1024, "odd_out")

    dy1, dxres1, red2 = _final_ln_loss_bwd(x1, y1, mod[1], ln_g[1:2], ln_b[1:2], loss_target, "final_ln_loss")
    (dgated,) = _matmul([(dy1, w_out_o)], "nt", BF16, seq, MLA_WIDTH, 512, MLA_WIDTH, "odd_out_dx")
    (g_w_out_o,) = _matmul([(gated, dy1)], "tn", BF16, MLA_WIDTH, d, 256, d, "odd_out_dw")
    dproj1_z, do_lat, g_wuv = _o_gate_bwd(dgated, o_lat, proj1, wuv2, "o_gate_bwd")
    dqp, dkp, dvv, r_o_out = _attn_bwd(qp, kp, o_lat, do_lat, lse, "attn_bwd",
                                       side=([g_w_out_o.reshape(N_DEV, -1, d)], False))
    dq_nope, dq_rope, g_wuk = _q_heads_bwd(dqp, q_up, cosp, sinp, wuk2, "q_heads_bwd")
    (dqn,) = _matmul([(dq_nope, w_uq_n), (dq_rope, w_uq_r)], "nt", F32, seq, MLA_Q_RANK, 512, 256, "q_up_dx")
    (g_wuq_n,) = _matmul([(qn, dq_nope)], "tn", F32, MLA_Q_RANK, MLA_WIDTH, 256, MLA_WIDTH, "q_up_dw_nope")
    (g_wuq_r,) = _matmul([(qn, dq_rope)], "tn", F32, MLA_Q_RANK, 1024, 256, 1024, "q_up_dw_rope")
    dproj1, red_mla = _mla_prep_bwd(proj1, dqn, dkp, dvv, cosp, sinp, g_q, mla_kv_norm_g, dproj1_z, "mla_prep_bwd")
    (dh1,) = _matmul([(dproj1, w_in_o)], "nt", F32, seq, d, 512, d, "odd_in_dx")
    part_uq = jnp.concatenate([g_wuq_n.reshape(MLA_Q_RANK, MLA_HEADS, MLA_NOPE),
                               g_wuq_r.reshape(MLA_Q_RANK, MLA_HEADS, MLA_ROPE)], axis=2).reshape(N_DEV, UQ_ROWS, LANES)
    g_w_in_o, r_uq = _matmul([(h1, dproj1)], "tn", BF16, d, ODD_IN_PAD, 256, ODD_IN_PAD // 2, "odd_in_dw", n_outer=True,
                             side=([part_uq], False))
    part_o_in = jnp.concatenate([g_w_in_o[:, :448], g_w_in_o[:, 512:]], axis=1).reshape(d, N_DEV, -1).transpose(1, 0, 2)
    dy0, dxres0, red1 = _mid_bwd(dh1, dxres1, x, y0, mod[0], mod[1], ln_g[0:1], ln_b[0:1], "mid_bwd")
    (dmix,) = _matmul([(dy0, w_out_e)], "nt", BF16, seq, 2048, 512, 2048, "even_out_dx")
    (g_w_out_e,) = _matmul([(mix0, dy0)], "tn", BF16, 2048, d, 256, d, "even_out_dw")
    dproj0, g_ws, g_bs, g_ng, g_nb, g_pw, g_pb, g_ps, r_o_in = _even_bwd(
        proj0, dmix, ws, bs_col, gmlp_norm_g, gmlp_norm_b, pool_w_full, pool_b, pool_scale, "even_mix_bwd",
        side=([part_o_in], False))
    part_pw = g_pw.reshape(4, N_DEV, 32, 256).transpose(1, 0, 2, 3).reshape(N_DEV, POOLW_ROWS, LANES)
    part_e_in, r_e_out, r_pw = _matmul_cols_tn(h0, dproj0, w_in_e3.shape[2], BF16, 512, "even_in_dw",
                                               side=([g_w_out_e.reshape(N_DEV, -1, d), part_pw], False))
    dh0, r_e_in = _matmul_cols_nt(dproj0, w_in_e3, F32, 512, "even_in_dx", side=([part_e_in], False))
    grad_x, red0 = _first_bwd(dh0, dxres0, x, mod[0], "first_bwd")

    t_mask = lax.broadcasted_iota(jnp.int32, (GMLP_BLOCK, GMLP_BLOCK), 0) // CHUNK
    s_mask = lax.broadcasted_iota(jnp.int32, (GMLP_BLOCK, GMLP_BLOCK), 1) // CHUNK
    part = {
        "ln_g": jnp.stack([red1[2], red2[0]]), "ln_b": jnp.stack([red1[3], red2[1]]),
        "gmlp_norm_g": g_ng, "gmlp_norm_b": g_nb,
        "gmlp_ws": jnp.where(s_mask <= t_mask, g_ws, 0.0), "gmlp_bs": g_bs,
        "pool_b": g_pb, "pool_scale": g_ps, "mla_kv_norm_g": red_mla[1, :MLA_KV_RANK],
        "mla_w_uk": g_wuk, "mla_w_uv": g_wuv,
    }
    dmod = jnp.stack([jnp.concatenate([red0[1], red0[0], red1[4]]),
                      jnp.concatenate([red1[1], red1[0], red2[2]])])

    loss_row = jnp.pad(jnp.broadcast_to((0.5 / d * jnp.sum(red2[3])).reshape(1, 1, 1), (N_DEV, 1, 1)),
                       ((0, 0), (0, 0), (0, LANES - 1)))
    part_small = jnp.concatenate([
        _pad_rows(jnp.concatenate([part[n].reshape(-1) for n in REPLICATED]).reshape(N_DEV, -1), CHUNK_ROWS),
        jnp.pad(jnp.concatenate([_ada_cols_rows(dmod), _pad_rows(red_mla[0].reshape(N_DEV, -1), 1), loss_row], axis=1),
                ((0, 0), (0, REP_ROWS - LOSS_ROW - 1), (0, 0)))], axis=1)
    (r_small,) = _scatter_parts([part_small], "grad_scatter")

    def small_local(src):
        return jnp.pad(src["mla_q_norm_g"], ((QNORM_ROW, REP_ROWS - QNORM_ROW - 1), (0, LANES - 32)))[None]

    def flat_rows(a, rows):
        return a.reshape(1, rows, LANES)

    res = {"e_w_in": _sum_parts_adamw(r_e_in, e_w_in, m_e_w_in, v_e_w_in, "adamw_e_w_in"),
           "o_w_in": _sum_parts_adamw(r_o_in, o_w_in, m_o_w_in, v_o_w_in, "adamw_o_w_in"),
           "e_w_out": _sum_parts_adamw(r_e_out, e_w_out, m_e_w_out, v_e_w_out, "adamw_e_w_out"),
           "o_w_out": _sum_parts_adamw(r_o_out, o_w_out, m_o_w_out, v_o_w_out, "adamw_o_w_out")}
    for n, r_n, rows in (("mla_w_uq", r_uq, UQ_ROWS), ("pool_w", r_pw, POOLW_ROWS)):
        res[n] = [t.reshape(w_in[n].shape) for t in _sum_parts_adamw(
            r_n, flat_rows(w_in[n], rows), flat_rows(m_in[n], rows), flat_rows(v_in[n], rows), "adamw_" + n)]
    small = _sum_parts_adamw(r_small, small_local(w_in), small_local(m_in), small_local(v_in), "adamw_small")
    res["mla_q_norm_g"] = [t[0, QNORM_ROW:QNORM_ROW + 1, :32] for t in small]
    loss = small[0][0, LOSS_ROW, 0]
    (rep_sum,) = _all_gather([small[0][0]], "replicated_gather")
    rep_w, rep_m, rep_v = (_pack_replicated(src, src["ada_b"]).reshape(-1, LANES) for src in (w_in, m_in, v_in))
    rep_res = (rep_sum,) + tuple(t.reshape(N_DEV, REP_ROWS, LANES)
                                 for t in _adamw_call(rep_sum.reshape(-1, LANES), rep_w, rep_m, rep_v, "replicated_adamw"))
    shapes = [(n, w_in[n].size, w_in[n].shape) for n in REPLICATED]
    for k, t in enumerate(rep_res):
        for n, val in _unpack_replicated(t, shapes).items():
            res.setdefault(n, [None] * 4)[k] = val
    dmod_all = r_small[:, ADA_ROW, :2 * ada_cols].reshape(N_DEV, 2, ada_cols).transpose(1, 0, 2)
    dmod_rows = jnp.pad(dmod_all[:, :, None, :], ((0, 0), (0, 0), (0, 7), (0, 0))).reshape(2, 8 * N_DEV, ada_cols)
    res["ada_w"] = _ada_w_grad_adamw(c_all, dmod_rows, ada_w, m_ada_w, v_ada_w, "ada_w_adamw")

    return (loss, grad_x, *[res[n][0] for n in names], *[res[n][1] for n in names],
            *[res[n][2] for n in names], *[res[n][3] for n in names])
```

```python
import functools

import jax
import jax.numpy as jnp
from jax import lax
from jax.experimental import pallas as pl
from jax.experimental.pallas import tpu as pltpu

F32 = jnp.float32
BF16 = jnp.bfloat16

D_MODEL = 1024
CHUNK = 64
LN_EPS = 1e-5
GMLP_HEADS = 4
GMLP_HEAD_DIM = 256
GMLP_BLOCK = 128
POOL_WINDOWS = (2, 4, 8, 16)
POOL_GROUP_DIM = 256
POOL_HALO = 16
MLA_HEADS = 16
MLA_NOPE = 128
MLA_ROPE = 64
MLA_Q_RANK = 256
MLA_KV_RANK = 128
MLA_WIDTH = 2048
ODD_IN = 2496
ODD_IN_PAD = 2560
ROPE_THETA = 10000.0
ATTN_SCALE = (MLA_NOPE + MLA_ROPE) ** -0.5
ATTN_SCALE_LOG2 = ATTN_SCALE * 1.4426950408889634
DEEPNORM_ALPHA = 4.0 ** 0.25
ADAM_LR, ADAM_B1, ADAM_B2, ADAM_EPS, ADAM_WD, ADAM_STEP = 0.001, 0.9, 0.999, 1e-8, 0.01, 10
N_DEV = 8
LANES = 1024
VMEM_LIMIT = 56 * 1024 * 1024
MESH = pl.DeviceIdType.MESH

NT = (((1,), (1,)), ((), ()))
NN = (((1,), (0,)), ((), ()))
TN = (((0,), (0,)), ((), ()))


def _params(n_axes):
    return pltpu.CompilerParams(dimension_semantics=("arbitrary",) * n_axes, vmem_limit_bytes=VMEM_LIMIT)


def _dot(a, b, dn):
    return lax.dot_general(a.astype(BF16), b.astype(BF16), dn, preferred_element_type=F32)


def _sigmoid(z):
    return 1.0 / (1.0 + jnp.exp(-z))


def _colsum(t):
    return jnp.sum(t, axis=0, keepdims=True)


def _exchange_copies(g_refs, r_refs, send_sems, recv_sems, local_sems, gather):
    x, y, c, me = _place()
    n_arr = len(g_refs)

    def src(t, slot):
        return g_refs[t] if gather else g_refs[t].at[slot]

    own = [pltpu.make_async_copy(src(t, me), r_refs[t].at[me], local_sems.at[t]) for t in range(n_arr)]
    sends, recvs = [], []
    for r in range(1, N_DEV):
        peer, lin = _flip(x, y, c, r)
        for t in range(n_arr):
            k = n_arr * (r - 1) + t
            sends.append(pltpu.make_async_remote_copy(
                src_ref=src(t, lin), dst_ref=r_refs[t].at[me], send_sem=send_sems.at[k], recv_sem=recv_sems.at[k],
                device_id=peer, device_id_type=MESH))
            recvs.append(pltpu.make_async_remote_copy(
                src_ref=src(t, lin), dst_ref=r_refs[t].at[lin], send_sem=send_sems.at[k], recv_sem=recv_sems.at[k],
                device_id=(x, y, c), device_id_type=MESH))
    return own, sends, recvs


def _exchange_start(copies):
    own, sends, _ = copies
    for cp in own + sends:
        cp.start()


def _exchange_wait(copies):
    own, sends, recvs = copies
    for cp in recvs:
        cp.wait_recv()
    for cp in sends:
        cp.wait_send()
    for cp in own:
        cp.wait()


def _exchange_extras(parts, gather):
    shapes = [jax.ShapeDtypeStruct(((N_DEV,) + p.shape) if gather else p.shape, p.dtype) for p in parts]
    n = len(parts) * (N_DEV - 1)
    return shapes, [pltpu.SemaphoreType.DMA((n,)), pltpu.SemaphoreType.DMA((n,)), pltpu.SemaphoreType.DMA((len(parts),))]


def _grid_call(body, name, grid, in_specs, out_specs, out_shape, args, scratch=(), side=None):
    if side is None:
        return pl.pallas_call(body, name=name, grid=grid, in_specs=in_specs, out_specs=out_specs,
                              out_shape=out_shape, scratch_shapes=list(scratch),
                              compiler_params=_params(len(grid)))(*args)
    parts, gather = side
    n_in, n_out, n_sc, n_arr = len(args), len(out_shape), len(scratch), len(parts)
    side_shapes, side_sems = _exchange_extras(parts, gather)

    def wrapped(*refs):
        ins, g_refs = refs[:n_in], refs[n_in:n_in + n_arr]
        outs = refs[n_in + n_arr:n_in + n_arr + n_out]
        r_refs = refs[n_in + n_arr + n_out:n_in + 2 * n_arr + n_out]
        sc = refs[n_in + 2 * n_arr + n_out:n_in + 2 * n_arr + n_out + n_sc]
        copies = _exchange_copies(g_refs, r_refs, *refs[-3:], gather)
        ids = [pl.program_id(a) for a in range(len(grid))]
        first = functools.reduce(jnp.logical_and, [i == 0 for i in ids])
        last = functools.reduce(jnp.logical_and, [i == g - 1 for i, g in zip(ids, grid)])

        @pl.when(first)
        def _():
            _exchange_start(copies)

        body(*ins, *outs, *sc)

        @pl.when(last)
        def _():
            _exchange_wait(copies)

    anyspace = pl.BlockSpec(memory_space=pl.ANY)
    return pl.pallas_call(
        wrapped, name=name, grid=grid, in_specs=list(in_specs) + [anyspace] * n_arr,
        out_specs=list(out_specs) + [anyspace] * n_arr, out_shape=list(out_shape) + side_shapes,
        scratch_shapes=list(scratch) + side_sems, compiler_params=_params(len(grid)),
    )(*args, *parts)


def _matmul(pairs, mode, out_dtype, m, n, tm, tn, name, side=None, n_outer=False):
    dn = {"nn": NN, "nt": NT, "tn": TN}[mode]
    tm, tn = min(tm, m), min(tn, n)
    n_pairs = len(pairs)
    grid = (n // tn, m // tm) if n_outer else (m // tm, n // tn)

    def ij(f):
        return (lambda j, i: f(i, j)) if n_outer else f

    def body(*refs):
        o_ref = refs[-1]
        acc = None
        for p in range(n_pairs):
            t = _dot(refs[2 * p][...], refs[2 * p + 1][...], dn)
            acc = t if acc is None else acc + t
        o_ref[...] = acc.astype(o_ref.dtype)

    in_specs, args = [], []
    for a, b in pairs:
        if mode == "nn":
            k = a.shape[1]
            in_specs += [pl.BlockSpec((tm, k), ij(lambda i, j: (i, 0))), pl.BlockSpec((k, tn), ij(lambda i, j: (0, j)))]
        elif mode == "nt":
            k = a.shape[1]
            in_specs += [pl.BlockSpec((tm, k), ij(lambda i, j: (i, 0))), pl.BlockSpec((tn, k), ij(lambda i, j: (j, 0)))]
        else:
            k = a.shape[0]
            in_specs += [pl.BlockSpec((k, tm), ij(lambda i, j: (0, i))), pl.BlockSpec((k, tn), ij(lambda i, j: (0, j)))]
        args += [a, b]
    return _grid_call(body, name, grid, in_specs, [pl.BlockSpec((tm, tn), ij(lambda i, j: (i, j)))],
                      [jax.ShapeDtypeStruct((m, n), out_dtype)], args, side=side)


def _matmul_cols_nn(a, w3, out_dtype, tm, name, side=None):
    m, k = a.shape
    _, _, n = w3.shape
    tm = min(tm, m)

    def body(a_ref, w_ref, o_ref):
        av = a_ref[...]
        for j in range(N_DEV):
            o_ref[:, n * j:n * (j + 1)] = _dot(av, w_ref[j], NN).astype(o_ref.dtype)

    return _grid_call(
        body, name, (m // tm,),
        [pl.BlockSpec((tm, k), lambda i: (i, 0)), pl.BlockSpec((N_DEV, k, n), lambda i: (0, 0, 0))],
        [pl.BlockSpec((tm, N_DEV * n), lambda i: (i, 0))], [jax.ShapeDtypeStruct((m, N_DEV * n), out_dtype)], [a, w3],
        side=side)


def _matmul_cols_nt(a, w3, out_dtype, tm, name, side=None):
    m = a.shape[0]
    _, k, n = w3.shape
    tm = min(tm, m)

    def body(a_ref, w_ref, o_ref):
        acc = _dot(a_ref[:, 0:n], w_ref[0], NT)
        for j in range(1, N_DEV):
            acc = acc + _dot(a_ref[:, n * j:n * (j + 1)], w_ref[j], NT)
        o_ref[...] = acc.astype(o_ref.dtype)

    return _grid_call(
        body, name, (m // tm,),
        [pl.BlockSpec((tm, N_DEV * n), lambda i: (i, 0)), pl.BlockSpec((N_DEV, k, n), lambda i: (0, 0, 0))],
        [pl.BlockSpec((tm, k), lambda i: (i, 0))], [jax.ShapeDtypeStruct((m, k), out_dtype)], [a, w3], side=side)


def _matmul_cols_tn(a, b, n, out_dtype, tk, name, side=None):
    m, k = a.shape
    tk = min(tk, k)

    def body(a_ref, b_ref, o_ref):
        o_ref[...] = _dot(a_ref[...], b_ref[...], TN).astype(o_ref.dtype)

    return _grid_call(
        body, name, (N_DEV, k // tk),
        [pl.BlockSpec((m, tk), lambda j, i: (0, i)), pl.BlockSpec((m, n), lambda j, i: (0, j))],
        [pl.BlockSpec((None, tk, n), lambda j, i: (j, i, 0))], [jax.ShapeDtypeStruct((N_DEV, k, n), out_dtype)], [a, b],
        side=side)


def _rows3(tm, d):
    return pl.BlockSpec((None, tm, d), lambda i: (0, i, 0))


def _modulate(x, mod, name):
    _, s, d = x.shape
    tm = min(s, 512)

    def body(x_ref, m_ref, h_ref):
        shift, scale = m_ref[0:1, 0:d], m_ref[0:1, d:2 * d]
        h_ref[...] = (x_ref[...] * (1.0 + scale) + shift).astype(BF16)

    return pl.pallas_call(
        body, name=name, grid=(s // tm,),
        in_specs=[_rows3(tm, d), pl.BlockSpec((8, 3 * d), lambda i: (0, 0))],
        out_specs=pl.BlockSpec((tm, d), lambda i: (i, 0)),
        out_shape=jax.ShapeDtypeStruct((s, d), BF16), compiler_params=_params(1),
    )(x, mod)


def _ln_stats(r):
    mu = jnp.mean(r, axis=-1, keepdims=True)
    rc = r - mu
    var = jnp.mean(rc * rc, axis=-1, keepdims=True)
    rstd = lax.rsqrt(var + LN_EPS)
    return rc * rstd, rstd


def _ln_bwd(dxhat, xhat, rstd):
    return rstd * (dxhat - jnp.mean(dxhat, axis=-1, keepdims=True)
                   - xhat * jnp.mean(dxhat * xhat, axis=-1, keepdims=True))


def _resid_ln(x, y, mod, g, b, mod_next, name):
    _, s, d = x.shape
    tm = min(s, 512)

    def body(x_ref, y_ref, m_ref, g_ref, b_ref, mn_ref, o_ref, h_ref):
        gate = m_ref[0:1, 2 * d:3 * d]
        xhat, _ = _ln_stats(DEEPNORM_ALPHA * x_ref[...] + (1.0 + gate) * y_ref[...])
        out = xhat * g_ref[...] + b_ref[...]
        o_ref[...] = out
        h_ref[...] = (out * (1.0 + mn_ref[0:1, d:2 * d]) + mn_ref[0:1, 0:d]).astype(BF16)

    row = pl.BlockSpec((tm, d), lambda i: (i, 0))
    vec = pl.BlockSpec((1, d), lambda i: (0, 0))
    modspec = pl.BlockSpec((8, 3 * d), lambda i: (0, 0))
    return pl.pallas_call(
        body, name=name, grid=(s // tm,),
        in_specs=[_rows3(tm, d), row, modspec, vec, vec, modspec],
        out_specs=[_rows3(tm, d), row],
        out_shape=[jax.ShapeDtypeStruct((1, s, d), F32), jax.ShapeDtypeStruct((s, d), BF16)],
        compiler_params=_params(1),
    )(x, y, mod, g, b, mod_next)


def _final_ln_loss_bwd(x, y, mod, g, b, target, name):
    _, s, d = x.shape
    tm = min(s, 256)

    def body(x_ref, y_ref, m_ref, g_ref, b_ref, t_ref, dy_ref, dx_ref, red_ref):
        @pl.when(pl.program_id(0) == 0)
        def _():
            red_ref[...] = jnp.zeros_like(red_ref)

        gate = m_ref[0:1, 2 * d:3 * d]
        yv = y_ref[...]
        xhat, rstd = _ln_stats(DEEPNORM_ALPHA * x_ref[...] + (1.0 + gate) * yv)
        err = xhat * g_ref[...] + b_ref[...] - t_ref[...]
        dout = err * (1.0 / d)
        dr = _ln_bwd(dout * g_ref[...], xhat, rstd)
        dy_ref[...] = ((1.0 + gate) * dr).astype(BF16)
        dx_ref[...] = DEEPNORM_ALPHA * dr
        red_ref[0:1, :] += _colsum(dout * xhat)
        red_ref[1:2, :] += _colsum(dout)
        red_ref[2:3, :] += _colsum(dr * yv)
        red_ref[3:4, :] += _colsum(err * err)

    row = pl.BlockSpec((tm, d), lambda i: (i, 0))
    vec = pl.BlockSpec((1, d), lambda i: (0, 0))
    return pl.pallas_call(
        body, name=name, grid=(s // tm,),
        in_specs=[_rows3(tm, d), row, pl.BlockSpec((8, 3 * d), lambda i: (0, 0)), vec, vec, _rows3(tm, d)],
        out_specs=[row, row, pl.BlockSpec((8, d), lambda i: (0, 0))],
        out_shape=[jax.ShapeDtypeStruct((s, d), BF16), jax.ShapeDtypeStruct((s, d), F32),
                   jax.ShapeDtypeStruct((8, d), F32)],
        compiler_params=_params(1),
    )(x, y, mod, g, b, target)


def _mid_bwd(dh, dxres, x, y, mod_lo, mod_hi, g, b, name):
    _, s, d = x.shape
    tm = min(s, 256)

    def body(dh_ref, dxr_ref, x_ref, y_ref, ml_ref, mh_ref, g_ref, b_ref, dy_ref, dx_ref, red_ref):
        @pl.when(pl.program_id(0) == 0)
        def _():
            red_ref[...] = jnp.zeros_like(red_ref)

        gate = ml_ref[0:1, 2 * d:3 * d]
        scale_hi = mh_ref[0:1, d:2 * d]
        yv, dhv = y_ref[...], dh_ref[...]
        xhat, rstd = _ln_stats(DEEPNORM_ALPHA * x_ref[...] + (1.0 + gate) * yv)
        x_mid = xhat * g_ref[...] + b_ref[...]
        dx_mid = dxr_ref[...] + dhv * (1.0 + scale_hi)
        dr = _ln_bwd(dx_mid * g_ref[...], xhat, rstd)
        dy_ref[...] = ((1.0 + gate) * dr).astype(BF16)
        dx_ref[...] = DEEPNORM_ALPHA * dr
        red_ref[0:1, :] += _colsum(dhv * x_mid)
        red_ref[1:2, :] += _colsum(dhv)
        red_ref[2:3, :] += _colsum(dx_mid * xhat)
        red_ref[3:4, :] += _colsum(dx_mid)
        red_ref[4:5, :] += _colsum(dr * yv)

    row = pl.BlockSpec((tm, d), lambda i: (i, 0))
    vec = pl.BlockSpec((1, d), lambda i: (0, 0))
    modspec = pl.BlockSpec((8, 3 * d), lambda i: (0, 0))
    return pl.pallas_call(
        body, name=name, grid=(s // tm,),
        in_specs=[row, row, _rows3(tm, d), row, modspec, modspec, vec, vec],
        out_specs=[row, row, pl.BlockSpec((8, d), lambda i: (0, 0))],
        out_shape=[jax.ShapeDtypeStruct((s, d), BF16), jax.ShapeDtypeStruct((s, d), F32),
                   jax.ShapeDtypeStruct((8, d), F32)],
        compiler_params=_params(1),
    )(dh, dxres, x, y, mod_lo, mod_hi, g, b)


def _first_bwd(dh, dxres, x, mod, name):
    _, s, d = x.shape
    tm = min(s, 512)

    def body(dh_ref, dxr_ref, x_ref, m_ref, gx_ref, red_ref):
        @pl.when(pl.program_id(0) == 0)
        def _():
            red_ref[...] = jnp.zeros_like(red_ref)

        scale = m_ref[0:1, d:2 * d]
        dhv = dh_ref[...]
        gx_ref[...] = dxr_ref[...] + dhv * (1.0 + scale)
        red_ref[0:1, :] += _colsum(dhv * x_ref[...])
        red_ref[1:2, :] += _colsum(dhv)

    row = pl.BlockSpec((tm, d), lambda i: (i, 0))
    return pl.pallas_call(
        body, name=name, grid=(s // tm,),
        in_specs=[row, row, _rows3(tm, d), pl.BlockSpec((8, 3 * d), lambda i: (0, 0))],
        out_specs=[_rows3(tm, d), pl.BlockSpec((8, d), lambda i: (0, 0))],
        out_shape=[jax.ShapeDtypeStruct((1, s, d), F32), jax.ShapeDtypeStruct((8, d), F32)],
        compiler_params=_params(1),
    )(dh, dxres, x, mod)


EVEN_TM = 256


def _gmlp_mask():
    t = lax.broadcasted_iota(jnp.int32, (GMLP_BLOCK, GMLP_BLOCK), 0) // CHUNK
    s = lax.broadcasted_iota(jnp.int32, (GMLP_BLOCK, GMLP_BLOCK), 1) // CHUNK
    return s <= t


def _window_sum(ext, win, back):
    n = ext.shape[0]
    k = 1
    while k < win:
        ext = ext + pltpu.roll(ext, k if back else n - k, 0)
        k *= 2
    return ext


def _inv_count(row0, rows, win):
    t = row0 + lax.broadcasted_iota(jnp.int32, (rows, 1), 0)
    return t, 1.0 / jnp.minimum(t + 1, win).astype(F32)


def _pooled(xb, halo, row0, win):
    tm = xb.shape[0]
    sums = _window_sum(jnp.concatenate([halo, xb], axis=0), win, True)[POOL_HALO:]
    _, inv = _inv_count(row0, tm, win)
    return sums * inv - xb


def _even_fwd(proj, ws, bs_col, ng, nb, pw, pb, ps, name, side=None):
    s = proj.shape[0]
    tm = min(s, EVEN_TM)
    hd, gd = GMLP_HEAD_DIM, POOL_GROUP_DIM

    def body(p_ref, halo_ref, ws_ref, bs_ref, ng_ref, nb_ref, pw_ref, pb_ref, ps_ref, m_ref):
        i = pl.program_id(0)
        mask = _gmlp_mask()
        for h in range(GMLP_HEADS):
            wm = jnp.where(mask, ws_ref[h], 0.0).astype(BF16)
            for blk in range(tm // GMLP_BLOCK):
                rows = slice(blk * GMLP_BLOCK, (blk + 1) * GMLP_BLOCK)
                cu, cv, cz = h * hd, 1024 + h * hd, 2048 + h * hd
                vhat, _ = _ln_stats(p_ref[rows, cv:cv + hd].astype(F32))
                vn = vhat * ng_ref[...] + nb_ref[...]
                sv = _dot(wm, vn, NN) + bs_ref[h]
                za = p_ref[rows, cz:cz + hd].astype(F32)
                m_ref[rows, cu:cu + hd] = (p_ref[rows, cu:cu + hd].astype(F32) * sv * (za * _sigmoid(za))).astype(BF16)
        for g, win in enumerate(POOL_WINDOWS):
            cx, cz = 3072 + g * gd, 4096 + g * gd
            halo = jnp.where(i > 0, halo_ref[:, g * gd:(g + 1) * gd].astype(F32), 0.0)
            pooled = _pooled(p_ref[:, cx:cx + gd].astype(F32), halo, i * tm, win)
            yb = _dot(pooled, pw_ref[g], NN) + pb_ref[:, g * gd:(g + 1) * gd]
            zb = p_ref[:, cz:cz + gd].astype(F32)
            m_ref[:, 1024 + g * gd:1024 + (g + 1) * gd] = (
                yb * ps_ref[:, g * gd:(g + 1) * gd] * (zb * _sigmoid(zb))).astype(BF16)

    hb = tm // POOL_HALO
    return _grid_call(
        body, name, (s // tm,),
        [
            pl.BlockSpec((tm, 5120), lambda i: (i, 0)),
            pl.BlockSpec((POOL_HALO, 1024), lambda i: (jnp.maximum(i * hb - 1, 0), 3)),
            pl.BlockSpec((GMLP_HEADS, GMLP_BLOCK, GMLP_BLOCK), lambda i: (0, 0, 0)),
            pl.BlockSpec((GMLP_HEADS, GMLP_BLOCK, 1), lambda i: (0, 0, 0)),
            pl.BlockSpec((1, hd), lambda i: (0, 0)), pl.BlockSpec((1, hd), lambda i: (0, 0)),
            pl.BlockSpec((4, gd, gd), lambda i: (0, 0, 0)),
            pl.BlockSpec((1, 1024), lambda i: (0, 0)), pl.BlockSpec((1, 1024), lambda i: (0, 0)),
        ],
        [pl.BlockSpec((tm, 2048), lambda i: (i, 0))], [jax.ShapeDtypeStruct((s, 2048), BF16)],
        [proj, proj, ws, bs_col, ng, nb, pw, pb, ps], side=side)


def _even_bwd(proj, dm, ws, bs_col, ng, nb, pw, pb, ps, name, side=None):
    s = proj.shape[0]
    tm = min(s, EVEN_TM)
    hd, gd = GMLP_HEAD_DIM, POOL_GROUP_DIM
    n_tiles = s // tm

    def body(p_ref, halo_ref, zbn_ref, dm_ref, dbn_ref, ws_ref, bs_ref, ng_ref, nb_ref, pw_ref, pb_ref, ps_ref,
             dp_ref, dws_ref, dbs_ref, dng_ref, dnb_ref, dpw_ref, dpb_ref, dps_ref):
        i = pl.program_id(0)

        @pl.when(i == 0)
        def _():
            for r in (dws_ref, dbs_ref, dng_ref, dnb_ref, dpw_ref, dpb_ref, dps_ref):
                r[...] = jnp.zeros_like(r)

        mask = _gmlp_mask()
        for h in range(GMLP_HEADS):
            wm = jnp.where(mask, ws_ref[h], 0.0).astype(BF16)
            for blk in range(tm // GMLP_BLOCK):
                rows = slice(blk * GMLP_BLOCK, (blk + 1) * GMLP_BLOCK)
                cu, cv, cz = h * hd, 1024 + h * hd, 2048 + h * hd
                vhat, rstd = _ln_stats(p_ref[rows, cv:cv + hd].astype(F32))
                vn = (vhat * ng_ref[...] + nb_ref[...]).astype(BF16)
                sv = _dot(wm, vn, NN) + bs_ref[h]
                u, za = p_ref[rows, cu:cu + hd].astype(F32), p_ref[rows, cz:cz + hd].astype(F32)
                da = dm_ref[rows, cu:cu + hd].astype(F32)
                sig = _sigmoid(za)
                sa = za * sig
                dau = da * u
                dsv = dau * sa
                dp_ref[rows, cu:cu + hd] = (da * sv * sa).astype(BF16)
                dp_ref[rows, cz:cz + hd] = (dau * sv * (sig * (1.0 + za * (1.0 - sig)))).astype(BF16)
                dsv_b = dsv.astype(BF16)
                dbs_ref[h] += jnp.sum(dsv, axis=1, keepdims=True)
                dws_ref[h] += _dot(dsv_b, vn, NT)
                dvn = _dot(wm, dsv_b, TN)
                dng_ref[...] += _colsum(dvn * vhat)
                dnb_ref[...] += _colsum(dvn)
                dp_ref[rows, cv:cv + hd] = _ln_bwd(dvn * ng_ref[...], vhat, rstd).astype(BF16)

        row0 = i * tm
        for g, win in enumerate(POOL_WINDOWS):
            cx, cz, cd = 3072 + g * gd, 4096 + g * gd, 1024 + g * gd
            gs = slice(g * gd, (g + 1) * gd)
            halo = jnp.where(i > 0, halo_ref[:, gs].astype(F32), 0.0)
            xb = p_ref[:, cx:cx + gd].astype(F32)
            pooled = _pooled(xb, halo, row0, win).astype(BF16)
            scale_g = ps_ref[:, gs]
            yb = _dot(pooled, pw_ref[g], NN) + pb_ref[:, gs]
            zb, db = p_ref[:, cz:cz + gd].astype(F32), dm_ref[:, cd:cd + gd].astype(F32)
            sig = _sigmoid(zb)
            dyp = db * (zb * sig)
            dp_ref[:, cz:cz + gd] = (db * yb * scale_g * (sig * (1.0 + zb * (1.0 - sig)))).astype(BF16)
            dps_ref[:, gs] += _colsum(dyp * yb)
            dpb_ref[:, gs] += _colsum(dyp * scale_g)
            zb_ext = jnp.concatenate([zb, zbn_ref[:, gs].astype(F32)], axis=0)
            db_ext = jnp.concatenate([db, dbn_ref[:, gs].astype(F32)], axis=0)
            dy_ext = (db_ext * (zb_ext * _sigmoid(zb_ext)) * scale_g).astype(BF16)
            dpw_ref[g] += _dot(pooled, dy_ext[:tm], TN)
            dpooled = _dot(dy_ext, pw_ref[g], NT)
            t, inv = _inv_count(row0, tm + POOL_HALO, win)
            w_ext = jnp.where(t < s, dpooled * inv, 0.0)
            dp_ref[:, cx:cx + gd] = (_window_sum(w_ext, win, False)[:tm] - dpooled[:tm]).astype(BF16)

    hb = tm // POOL_HALO
    last = s // POOL_HALO - 1
    small = lambda shape: pl.BlockSpec(shape, lambda i: (0,) * len(shape))
    return _grid_call(
        body, name, (n_tiles,),
        [
            pl.BlockSpec((tm, 5120), lambda i: (i, 0)),
            pl.BlockSpec((POOL_HALO, 1024), lambda i: (jnp.maximum(i * hb - 1, 0), 3)),
            pl.BlockSpec((POOL_HALO, 1024), lambda i: (jnp.minimum((i + 1) * hb, last), 4)),
            pl.BlockSpec((tm, 2048), lambda i: (i, 0)),
            pl.BlockSpec((POOL_HALO, 1024), lambda i: (jnp.minimum((i + 1) * hb, last), 1)),
            small((GMLP_HEADS, GMLP_BLOCK, GMLP_BLOCK)), small((GMLP_HEADS, GMLP_BLOCK, 1)),
            small((1, hd)), small((1, hd)), small((4, gd, gd)), small((1, 1024)), small((1, 1024)),
        ],
        [
            pl.BlockSpec((tm, 5120), lambda i: (i, 0)),
            small((GMLP_HEADS, GMLP_BLOCK, GMLP_BLOCK)), small((GMLP_HEADS, GMLP_BLOCK, 1)),
            small((1, hd)), small((1, hd)), small((4, gd, gd)), small((1, 1024)), small((1, 1024)),
        ],
        [
            jax.ShapeDtypeStruct((s, 5120), BF16),
            jax.ShapeDtypeStruct((GMLP_HEADS, GMLP_BLOCK, GMLP_BLOCK), F32),
            jax.ShapeDtypeStruct((GMLP_HEADS, GMLP_BLOCK, 1), F32),
            jax.ShapeDtypeStruct((1, hd), F32), jax.ShapeDtypeStruct((1, hd), F32),
            jax.ShapeDtypeStruct((4, gd, gd), F32),
            jax.ShapeDtypeStruct((1, 1024), F32), jax.ShapeDtypeStruct((1, 1024), F32),
        ],
        [proj, proj, proj, dm, dm, ws, bs_col, ng, nb, pw, pb, ps], side=side)


def _rope_pair_swap(t):
    lane = lax.broadcasted_iota(jnp.int32, t.shape, 1)
    return jnp.where(lane % 64 < 32, pltpu.roll(t, 96, 1), pltpu.roll(t, 32, 1))


def _rms(x, g):
    r = lax.rsqrt(jnp.mean(x * x, axis=-1, keepdims=True) + LN_EPS)
    return x * r, r


def _rms_bwd(dy, g, xhat, r):
    dyg = dy * g
    return r * (dyg - xhat * jnp.mean(dyg * xhat, axis=-1, keepdims=True))


def _lane_lt(shape, n):
    return lax.broadcasted_iota(jnp.int32, shape, 1) < n


def _mla_prep(proj, cosp, sinp, gq, gkv, name):
    s = proj.shape[0]
    tm = min(s, 512)

    def body(qc_ref, kv_ref, c_ref, s_ref, gq_ref, gkv_ref, qn_ref, kp_ref):
        qhat, _ = _rms(qc_ref[...].astype(F32), None)
        qn_ref[...] = (qhat * gq_ref[...]).astype(BF16)
        khat, _ = _rms(kv_ref[:, 0:128].astype(F32), None)
        kp_ref[:, 0:128] = (khat * gkv_ref[...]).astype(BF16)
        kr = kv_ref[:, 128:256].astype(F32)
        kp_ref[:, 128:256] = (kr * c_ref[...] + _rope_pair_swap(kr) * s_ref[...]).astype(BF16)

    return pl.pallas_call(
        body, name=name, grid=(s // tm,),
        in_specs=[pl.BlockSpec((tm, 256), lambda i: (i, 0)), pl.BlockSpec((tm, 256), lambda i: (i, 1)),
                  pl.BlockSpec((tm, 128), lambda i: (i, 0)), pl.BlockSpec((tm, 128), lambda i: (i, 0)),
                  pl.BlockSpec((1, 256), lambda i: (0, 0)), pl.BlockSpec((1, 128), lambda i: (0, 0))],
        out_specs=[pl.BlockSpec((tm, 256), lambda i: (i, 0)), pl.BlockSpec((tm, 256), lambda i: (i, 0))],
        out_shape=[jax.ShapeDtypeStruct((s, 256), BF16), jax.ShapeDtypeStruct((s, 256), BF16)],
        compiler_params=_params(1),
    )(proj, proj, cosp, sinp, gq, gkv)


def _mla_prep_bwd(proj, dqn, dkp, dv, cosp, sinp, gq, gkv, dproj, name):
    s = proj.shape[0]
    tm = min(s, 512)

    def body(qc_ref, kv_ref, dqn_ref, dkp_ref, dv_ref, c_ref, s_ref, gq_ref, gkv_ref, dproj_ref, o_ref, red_ref):
        @pl.when(pl.program_id(0) == 0)
        def _():
            red_ref[...] = jnp.zeros_like(red_ref)

        qhat, qr = _rms(qc_ref[...].astype(F32), None)
        dq = dqn_ref[...]
        o_ref[:, 0:256] = _rms_bwd(dq, gq_ref[...], qhat, qr).astype(BF16)
        red_ref[0:1, :] += _colsum(dq * qhat)
        khat, kr = _rms(kv_ref[:, 0:128].astype(F32), None)
        dk = dkp_ref[:, 0:128] + dv_ref[...]
        o_ref[:, 256:384] = _rms_bwd(dk, gkv_ref[...], khat, kr).astype(BF16)
        red_ref[1:2, 0:128] += _colsum(dk * khat)
        dr = dkp_ref[:, 128:256]
        o_ref[:, 384:512] = (dr * c_ref[...] - _rope_pair_swap(dr) * s_ref[...]).astype(BF16)

    return pl.pallas_call(
        body, name=name, grid=(s // tm,),
        in_specs=[pl.BlockSpec((tm, 256), lambda i: (i, 0)), pl.BlockSpec((tm, 256), lambda i: (i, 1)),
                  pl.BlockSpec((tm, 256), lambda i: (i, 0)), pl.BlockSpec((tm, 256), lambda i: (i, 0)),
                  pl.BlockSpec((tm, 128), lambda i: (i, 0)),
                  pl.BlockSpec((tm, 128), lambda i: (i, 0)), pl.BlockSpec((tm, 128), lambda i: (i, 0)),
                  pl.BlockSpec((1, 256), lambda i: (0, 0)), pl.BlockSpec((1, 128), lambda i: (0, 0)),
                  pl.BlockSpec(memory_space=pl.ANY)],
        out_specs=[pl.BlockSpec((tm, 512), lambda i: (i, 0)), pl.BlockSpec((8, 256), lambda i: (0, 0))],
        out_shape=[jax.ShapeDtypeStruct(dproj.shape, BF16), jax.ShapeDtypeStruct((8, 256), F32)],
        input_output_aliases={9: 0}, compiler_params=_params(1),
    )(proj, proj, dqn, dkp, dv, cosp, sinp, gq, gkv, dproj)


HEADS_TM = 256
Z_COL0 = ODD_IN_PAD - MLA_WIDTH


def _head_cols(h):
    return slice(128 * h, 128 * h + 128)


def _q_heads(q_up, cosp, sinp, wuk, name):
    s = q_up.shape[0]
    tm = min(s, HEADS_TM)

    def body(q_ref, c_ref, s_ref, w_ref, o_ref):
        for p in range(MLA_HEADS // 2):
            raw = q_ref[:, MLA_WIDTH + 128 * p:MLA_WIDTH + 128 * (p + 1)].astype(F32)
            rot = raw * c_ref[...] + _rope_pair_swap(raw) * s_ref[...]
            low = _lane_lt(rot.shape, 64)
            o_ref[2 * p, :, 128:256] = jnp.where(low, rot, 0.0).astype(BF16)
            o_ref[2 * p + 1, :, 128:256] = jnp.where(low, pltpu.roll(rot, 64, 1), 0.0).astype(BF16)
        for h in range(MLA_HEADS):
            o_ref[h, :, 0:128] = _dot(q_ref[:, _head_cols(h)], w_ref[:, _head_cols(h)], NT).astype(BF16)

    return pl.pallas_call(
        body, name=name, grid=(s // tm,),
        in_specs=[pl.BlockSpec((tm, 3072), lambda i: (i, 0)),
                  pl.BlockSpec((tm, 128), lambda i: (i, 0)), pl.BlockSpec((tm, 128), lambda i: (i, 0)),
                  pl.BlockSpec((128, MLA_WIDTH), lambda i: (0, 0))],
        out_specs=pl.BlockSpec((MLA_HEADS, tm, 256), lambda i: (0, i, 0)),
        out_shape=jax.ShapeDtypeStruct((MLA_HEADS, s, 256), BF16), compiler_params=_params(1),
    )(q_up, cosp, sinp, wuk)


def _q_heads_bwd(dqp, q_up, cosp, sinp, wuk, name):
    s = q_up.shape[0]
    tm = min(s, HEADS_TM)

    def body(dq_ref, qn_ref, c_ref, s_ref, w_ref, dn_ref, dr_ref, dw_ref):
        @pl.when(pl.program_id(0) == 0)
        def _():
            dw_ref[...] = jnp.zeros_like(dw_ref)

        for h in range(MLA_HEADS):
            dlat = dq_ref[h, :, 0:128]
            dn_ref[:, _head_cols(h)] = _dot(dlat, w_ref[:, _head_cols(h)], NN).astype(BF16)
            dw_ref[:, _head_cols(h)] += _dot(dlat, qn_ref[:, _head_cols(h)], TN)
        for p in range(MLA_HEADS // 2):
            drot = dq_ref[2 * p, :, 128:256].astype(F32) + pltpu.roll(dq_ref[2 * p + 1, :, 128:256].astype(F32), 64, 1)
            dr_ref[:, _head_cols(p)] = (drot * c_ref[...] - _rope_pair_swap(drot) * s_ref[...]).astype(BF16)

    return pl.pallas_call(
        body, name=name, grid=(s // tm,),
        in_specs=[pl.BlockSpec((MLA_HEADS, tm, 256), lambda i: (0, i, 0)),
                  pl.BlockSpec((tm, MLA_WIDTH), lambda i: (i, 0)),
                  pl.BlockSpec((tm, 128), lambda i: (i, 0)), pl.BlockSpec((tm, 128), lambda i: (i, 0)),
                  pl.BlockSpec((128, MLA_WIDTH), lambda i: (0, 0))],
        out_specs=[pl.BlockSpec((tm, MLA_WIDTH), lambda i: (i, 0)),
                   pl.BlockSpec((tm, 1024), lambda i: (i, 0)),
                   pl.BlockSpec((128, MLA_WIDTH), lambda i: (0, 0))],
        out_shape=[jax.ShapeDtypeStruct((s, MLA_WIDTH), BF16), jax.ShapeDtypeStruct((s, 1024), BF16),
                   jax.ShapeDtypeStruct((128, MLA_WIDTH), F32)],
        compiler_params=_params(1),
    )(dqp, q_up, cosp, sinp, wuk)


def _o_gate(o_lat, proj, wuv, name):
    s = o_lat.shape[1]
    tm = min(s, HEADS_TM)

    def body(ol_ref, p_ref, w_ref, g_ref):
        for h in range(MLA_HEADS):
            z = p_ref[:, Z_COL0 + 128 * h:Z_COL0 + 128 * (h + 1)].astype(F32)
            g_ref[:, _head_cols(h)] = (_dot(ol_ref[h], w_ref[:, _head_cols(h)], NN) * (z * _sigmoid(z))).astype(BF16)

    return pl.pallas_call(
        body, name=name, grid=(s // tm,),
        in_specs=[pl.BlockSpec((MLA_HEADS, tm, 128), lambda i: (0, i, 0)),
                  pl.BlockSpec((tm, ODD_IN_PAD), lambda i: (i, 0)),
                  pl.BlockSpec((128, MLA_WIDTH), lambda i: (0, 0))],
        out_specs=pl.BlockSpec((tm, MLA_WIDTH), lambda i: (i, 0)),
        out_shape=jax.ShapeDtypeStruct((s, MLA_WIDTH), BF16), compiler_params=_params(1),
    )(o_lat, proj, wuv)


def _o_gate_bwd(dg, o_lat, proj, wuv, name):
    s = o_lat.shape[1]
    tm = min(s, HEADS_TM)

    def body(dg_ref, ol_ref, p_ref, w_ref, dp_ref, dol_ref, dw_ref):
        @pl.when(pl.program_id(0) == 0)
        def _():
            dw_ref[...] = jnp.zeros_like(dw_ref)

        dp_ref[:, 0:Z_COL0] = jnp.zeros((tm, Z_COL0), BF16)
        for h in range(MLA_HEADS):
            zc = slice(Z_COL0 + 128 * h, Z_COL0 + 128 * (h + 1))
            z, dgv, ol = p_ref[:, zc].astype(F32), dg_ref[:, _head_cols(h)].astype(F32), ol_ref[h]
            sig = _sigmoid(z)
            o = _dot(ol, w_ref[:, _head_cols(h)], NN)
            dp_ref[:, zc] = (dgv * o * (sig * (1.0 + z * (1.0 - sig)))).astype(BF16)
            do = (dgv * (z * sig)).astype(BF16)
            dol_ref[h] = _dot(do, w_ref[:, _head_cols(h)], NT).astype(BF16)
            dw_ref[:, _head_cols(h)] += _dot(ol, do, TN)

    return pl.pallas_call(
        body, name=name, grid=(s // tm,),
        in_specs=[pl.BlockSpec((tm, MLA_WIDTH), lambda i: (i, 0)),
                  pl.BlockSpec((MLA_HEADS, tm, 128), lambda i: (0, i, 0)),
                  pl.BlockSpec((tm, ODD_IN_PAD), lambda i: (i, 0)),
                  pl.BlockSpec((128, MLA_WIDTH), lambda i: (0, 0))],
        out_specs=[pl.BlockSpec((tm, ODD_IN_PAD), lambda i: (i, 0)),
                   pl.BlockSpec((MLA_HEADS, tm, 128), lambda i: (0, i, 0)),
                   pl.BlockSpec((128, MLA_WIDTH), lambda i: (0, 0))],
        out_shape=[jax.ShapeDtypeStruct((s, ODD_IN_PAD), BF16), jax.ShapeDtypeStruct((MLA_HEADS, s, 128), BF16),
                   jax.ShapeDtypeStruct((128, MLA_WIDTH), F32)],
        compiler_params=_params(1),
    )(dg, o_lat, proj, wuv)


ATT_TQ = CHUNK
ATT_ROWS = ATT_TQ * MLA_HEADS
ATT_TK = 512
ATT_HEAD_GROUP = 4


def _visible(k0, q_chunk, tk):
    kpos = k0 + lax.broadcasted_iota(jnp.int32, (1, tk), 1)
    return kpos // CHUNK <= q_chunk


def _tile_lanes(t, n):
    return jnp.concatenate([t] * (n // 128), axis=1)


def _key_blocks(i, tk, block, pairs=False):
    n_full = (i * ATT_TQ + ATT_TQ + tk - 1) // tk - 1
    if pairs:
        def two(jj, carry):
            block(2 * jj, False)
            block(2 * jj + 1, False)
            return carry

        lax.fori_loop(0, n_full // 2, two, 0)

        @pl.when(n_full % 2 == 1)
        def _():
            block(n_full - 1, False)
    else:
        def one(j, carry):
            block(j, False)
            return carry

        lax.fori_loop(0, n_full, one, 0)
    block(n_full, True)


def _attn_fwd(qp, kp, name, side=None):
    s = kp.shape[0]
    tk = min(ATT_TK, s)

    def body(q_ref, k_ref, o_ref, lse_ref, m_sc, acc_sc):
        i = pl.program_id(0)
        m_sc[...] = jnp.full_like(m_sc, -jnp.inf)
        acc_sc[...] = jnp.zeros_like(acc_sc)

        def block(j, masked):
            k0 = pl.multiple_of(j * tk, tk)
            k = k_ref[pl.ds(k0, tk), :]
            v1 = jnp.where(_lane_lt(k.shape, 128), k, jnp.ones_like(k))
            for h0 in range(0, MLA_HEADS, ATT_HEAD_GROUP):
                rows = slice(h0 * ATT_TQ, (h0 + ATT_HEAD_GROUP) * ATT_TQ)
                q = q_ref[h0:h0 + ATT_HEAD_GROUP].reshape(ATT_HEAD_GROUP * ATT_TQ, 256)
                sc = _dot(q, k, NT) * ATTN_SCALE_LOG2
                if masked:
                    sc = jnp.where(_visible(k0, i, tk), sc, -jnp.inf)
                m_prev = m_sc[rows]
                m_new = jnp.maximum(m_prev, jnp.max(sc, axis=1, keepdims=True))
                p = jnp.exp2(sc - _tile_lanes(m_new, tk))
                acc_sc[rows] = _tile_lanes(jnp.exp2(m_prev - m_new), 256) * acc_sc[rows] + _dot(p, v1, NN)
                m_sc[rows] = m_new

        _key_blocks(i, tk, block, pairs=True)
        acc = acc_sc[...]
        l = acc[:, 128:256]
        o_ref[...] = (acc[:, 0:128] / l).astype(BF16).reshape(MLA_HEADS, ATT_TQ, 128)
        lse_ref[...] = (m_sc[...] + jnp.log2(l)).reshape(MLA_HEADS, ATT_TQ, 128)

    head128 = pl.BlockSpec((MLA_HEADS, ATT_TQ, 128), lambda i: (0, i, 0))
    return _grid_call(
        body, name, (s // ATT_TQ,),
        [pl.BlockSpec((MLA_HEADS, ATT_TQ, 256), lambda i: (0, i, 0)), pl.BlockSpec((s, 256), lambda i: (0, 0))],
        [head128, head128],
        [jax.ShapeDtypeStruct((MLA_HEADS, s, 128), BF16), jax.ShapeDtypeStruct((MLA_HEADS, s, 128), F32)],
        [qp, kp], scratch=[pltpu.VMEM((ATT_ROWS, 128), F32), pltpu.VMEM((ATT_ROWS, 256), F32)], side=side)


def _attn_bwd(qp, kp, o, do, lse, name, side=None):
    s = kp.shape[0]
    tk = min(ATT_TK, s)

    def body(q_ref, k_ref, o_ref, do_ref, lse_ref, dq_ref, dk_ref, dv_ref, dq_sc):
        i = pl.program_id(0)

        @pl.when(i == 0)
        def _():
            dk_ref[...] = jnp.zeros_like(dk_ref)
            dv_ref[...] = jnp.zeros_like(dv_ref)

        q = q_ref[...].reshape(ATT_ROWS, 256)
        dov = do_ref[...].reshape(ATT_ROWS, 128)
        delta = jnp.sum(dov.astype(F32) * o_ref[...].reshape(ATT_ROWS, 128).astype(F32), axis=1, keepdims=True)
        delta_t = _tile_lanes(jnp.broadcast_to(delta, (ATT_ROWS, 128)), tk)
        lse_t = _tile_lanes(lse_ref[...].reshape(ATT_ROWS, 128), tk)
        dq_sc[...] = jnp.zeros_like(dq_sc)

        def block(j, masked):
            k0 = pl.multiple_of(j * tk, tk)
            k = k_ref[pl.ds(k0, tk), :]
            p = jnp.exp2(_dot(q, k, NT) * ATTN_SCALE_LOG2 - lse_t)
            if masked:
                p = jnp.where(_visible(k0, i, tk), p, 0.0)
            dv_ref[pl.ds(k0, tk), :] += _dot(p, dov, TN)
            ds = (p * (_dot(dov, k[:, 0:128], NT) - delta_t) * ATTN_SCALE).astype(BF16)
            dq_sc[...] += _dot(ds, k, NN)
            dk_ref[pl.ds(k0, tk), :] += _dot(ds, q, TN)

        _key_blocks(i, tk, block, pairs=True)
        dq_ref[...] = dq_sc[...].astype(BF16).reshape(MLA_HEADS, ATT_TQ, 256)

    head128 = pl.BlockSpec((MLA_HEADS, ATT_TQ, 128), lambda i: (0, i, 0))
    head256 = pl.BlockSpec((MLA_HEADS, ATT_TQ, 256), lambda i: (0, i, 0))
    return _grid_call(
        body, name, (s // ATT_TQ,),
        [head256, pl.BlockSpec((s, 256), lambda i: (0, 0)), head128, head128, head128],
        [head256, pl.BlockSpec((s, 256), lambda i: (0, 0)), pl.BlockSpec((s, 128), lambda i: (0, 0))],
        [jax.ShapeDtypeStruct((MLA_HEADS, s, 256), BF16),
         jax.ShapeDtypeStruct((s, 256), F32), jax.ShapeDtypeStruct((s, 128), F32)],
        [qp, kp, o, do, lse], scratch=[pltpu.VMEM((ATT_ROWS, 256), F32)], side=side)


def _place():
    x, y, c = lax.axis_index("x"), lax.axis_index("y"), lax.axis_index("c")
    return x, y, c, 4 * x + 2 * y + c


def _flip(x, y, c, r):
    px = 1 - x if r & 4 else x
    py = 1 - y if r & 2 else y
    pc = 1 - c if r & 1 else c
    return (px, py, pc), 4 * px + 2 * py + pc


def _adaln_exchange(c8, ada_w, ada_b_cols, name):
    d = c8.shape[1]
    w_cols = ada_w.shape[2]

    def body(c_ref, w_ref, b_ref, call_ref, mod_ref, sbuf, rbuf, s1, r1, s2, r2):
        x, y, c, me = _place()
        call_ref[pl.ds(pl.multiple_of(me * 8, 8), 8), :] = c_ref[...]
        peers = [_flip(x, y, c, r) for r in range(1, N_DEV)]

        def c_copy(k, src_lin, to):
            rows = call_ref.at[pl.ds(pl.multiple_of(src_lin * 8, 8), 8), :]
            return pltpu.make_async_remote_copy(src_ref=rows, dst_ref=rows, send_sem=s1.at[k], recv_sem=r1.at[k],
                                                device_id=to, device_id_type=MESH)

        first = [c_copy(k, me, peer) for k, (peer, _) in enumerate(peers)]
        for cp in first:
            cp.start()
        for k, (_, lin) in enumerate(peers):
            c_copy(k, lin, (x, y, c)).wait_recv()
        for cp in first:
            cp.wait_send()

        for j in range(N_DEV):
            cj = call_ref[8 * j:8 * j + 8, :]
            cond = cj * _sigmoid(cj)
            for l in range(2):
                sbuf[j, l] = lax.dot_general(cond, w_ref[l], NN, precision=lax.Precision.HIGHEST,
                                             preferred_element_type=F32) + b_ref[l]

        def m_copy(k, src_slot, dst_slot, to):
            return pltpu.make_async_remote_copy(src_ref=sbuf.at[src_slot], dst_ref=rbuf.at[dst_slot],
                                                send_sem=s2.at[k], recv_sem=r2.at[k], device_id=to,
                                                device_id_type=MESH)

        rbuf[me] = sbuf[me]
        second = [m_copy(k, lin, me, peer) for k, (peer, lin) in enumerate(peers)]
        for cp in second:
            cp.start()
        for k, (_, lin) in enumerate(peers):
            m_copy(k, lin, lin, (x, y, c)).wait_recv()
        for cp in second:
            cp.wait_send()
        for j in range(N_DEV):
            for l in range(2):
                mod_ref[l, :, w_cols * j:w_cols * (j + 1)] = rbuf[j, l]

    vmem = pl.BlockSpec(memory_space=pltpu.VMEM)
    return pl.pallas_call(
        body, name=name, in_specs=[vmem, vmem, vmem], out_specs=[vmem, vmem],
        out_shape=[jax.ShapeDtypeStruct((8 * N_DEV, d), F32), jax.ShapeDtypeStruct((2, 8, 3 * d), F32)],
        scratch_shapes=[pltpu.VMEM((N_DEV, 2, 8, w_cols), F32), pltpu.VMEM((N_DEV, 2, 8, w_cols), F32),
                        pltpu.SemaphoreType.DMA((N_DEV - 1,)), pltpu.SemaphoreType.DMA((N_DEV - 1,)),
                        pltpu.SemaphoreType.DMA((N_DEV - 1,)), pltpu.SemaphoreType.DMA((N_DEV - 1,))],
        compiler_params=pltpu.CompilerParams(vmem_limit_bytes=VMEM_LIMIT),
    )(c8, ada_w, ada_b_cols)


def _all_gather(blocks, name):
    n_arr = len(blocks)

    def body(*refs):
        x_refs, out_refs = refs[:n_arr], refs[n_arr:2 * n_arr]
        send_sems, recv_sems, local_sems = refs[2 * n_arr:]
        x, y, c, _ = _place()
        me, sibling = (x, y, c), (x, y, 1 - c)
        chips = [(1 - x, y), (x, 1 - y), (1 - x, 1 - y)]

        def copy(t, k, blk, to, src=None):
            slot = out_refs[t].at[4 * blk[0] + 2 * blk[1] + blk[2]]
            return pltpu.make_async_remote_copy(src_ref=slot if src is None else src, dst_ref=slot,
                                                send_sem=send_sems.at[7 * t + k], recv_sem=recv_sems.at[7 * t + k],
                                                device_id=to, device_id_type=MESH)

        mine = [pltpu.make_async_copy(x_refs[t], out_refs[t].at[4 * x + 2 * y + c], local_sems.at[t])
                for t in range(n_arr)]
        for cp in mine:
            cp.start()
        first = []
        for t in range(n_arr):
            first.append(copy(t, 0, me, sibling, src=x_refs[t]))
            first += [copy(t, 1 + j, me, (*chip, c), src=x_refs[t]) for j, chip in enumerate(chips)]
        for cp in first:
            cp.start()
        passed = []
        for t in range(n_arr):
            for j, chip in enumerate(chips):
                copy(t, 1 + j, (*chip, c), me).wait_recv()
                passed.append(copy(t, 4 + j, (*chip, c), sibling))
                passed[-1].start()
        for t in range(n_arr):
            copy(t, 0, sibling, me).wait_recv()
            for j, chip in enumerate(chips):
                copy(t, 4 + j, (*chip, 1 - c), me).wait_recv()
        for cp in first + passed:
            cp.wait_send()
        for cp in mine:
            cp.wait()

    anyspace = pl.BlockSpec(memory_space=pl.ANY)
    return pl.pallas_call(
        body, name=name, in_specs=[anyspace] * n_arr, out_specs=[anyspace] * n_arr,
        out_shape=[jax.ShapeDtypeStruct((N_DEV,) + b.shape, b.dtype) for b in blocks],
        scratch_shapes=[pltpu.SemaphoreType.DMA((7 * n_arr,)), pltpu.SemaphoreType.DMA((7 * n_arr,)),
                        pltpu.SemaphoreType.DMA((n_arr,))],
    )(*blocks)


def _scatter_parts(parts, name):
    n_arr = len(parts)

    def body(*refs):
        copies = _exchange_copies(refs[:n_arr], refs[n_arr:2 * n_arr], *refs[2 * n_arr:], False)
        _exchange_start(copies)
        _exchange_wait(copies)

    anyspace = pl.BlockSpec(memory_space=pl.ANY)
    shapes, sems = _exchange_extras(parts, False)
    return pl.pallas_call(body, name=name, in_specs=[anyspace] * n_arr, out_specs=[anyspace] * n_arr,
                          out_shape=shapes, scratch_shapes=sems)(*parts)


def _adamw(w, g, m, v):
    m = ADAM_B1 * m + (1.0 - ADAM_B1) * g
    v = ADAM_B2 * v + (1.0 - ADAM_B2) * (g * g)
    m_hat = m / (1.0 - ADAM_B1 ** ADAM_STEP)
    v_hat = v / (1.0 - ADAM_B2 ** ADAM_STEP)
    return -ADAM_LR * (m_hat / (jnp.sqrt(v_hat) + ADAM_EPS) + ADAM_WD * w), m, v


def _sum_parts_adamw(parts, w, m, v, name):
    _, rows, cols = parts.shape
    tr = max(t for t in range(16, 129, 16) if rows % t == 0)

    def body(p_ref, w_ref, m_ref, v_ref, g_ref, d_ref, mo_ref, vo_ref):
        g = p_ref[0].astype(F32)
        for j in range(1, N_DEV):
            g = g + p_ref[j].astype(F32)
        g_ref[...] = g
        d_ref[...], mo_ref[...], vo_ref[...] = _adamw(w_ref[...], g, m_ref[...], v_ref[...])

    row = _rows3(tr, cols)
    out = jax.ShapeDtypeStruct((1, rows, cols), F32)
    return pl.pallas_call(
        body, name=name, grid=(rows // tr,),
        in_specs=[pl.BlockSpec((N_DEV, tr, cols), lambda i: (0, i, 0)), row, row, row],
        out_specs=[row, row, row, row], out_shape=[out, out, out, out], compiler_params=_params(1),
    )(parts, w, m, v)


def _sum_parts_adamw_whole(parts, w, m, v, name):
    def body(p_ref, w_ref, m_ref, v_ref, g_ref, d_ref, mo_ref, vo_ref):
        g = p_ref[0:1]
        for j in range(1, N_DEV):
            g = g + p_ref[j:j + 1]
        g_ref[...] = g
        d_ref[...], mo_ref[...], vo_ref[...] = _adamw(w_ref[...], g, m_ref[...], v_ref[...])

    out = jax.ShapeDtypeStruct(w.shape, F32)
    return pl.pallas_call(body, name=name, out_shape=[out] * 4,
                          compiler_params=pltpu.CompilerParams(vmem_limit_bytes=VMEM_LIMIT))(parts, w, m, v)


def _sum_parts(parts, name):
    def body(p_ref, g_ref):
        g = p_ref[0]
        for j in range(1, N_DEV):
            g = g + p_ref[j]
        g_ref[...] = g

    return pl.pallas_call(body, name=name, out_shape=jax.ShapeDtypeStruct(parts.shape[1:], F32),
                          compiler_params=pltpu.CompilerParams(vmem_limit_bytes=VMEM_LIMIT))(parts)


def _adamw_many(gs, ws, ms, vs, name):
    n = len(gs)

    def body(*refs):
        for k in range(n):
            g_ref, w_ref, m_ref, v_ref = (refs[q * n + k] for q in range(4))
            d_ref, mo_ref, vo_ref = (refs[(4 + q) * n + k] for q in range(3))
            d_ref[...], mo_ref[...], vo_ref[...] = _adamw(w_ref[...], g_ref[...], m_ref[...], v_ref[...])

    out = [jax.ShapeDtypeStruct(w.shape, F32) for w in ws]
    res = pl.pallas_call(body, name=name, out_shape=out * 3,
                         compiler_params=pltpu.CompilerParams(vmem_limit_bytes=VMEM_LIMIT))(*gs, *ws, *ms, *vs)
    return res[:n], res[n:2 * n], res[2 * n:]


def _ada_w_grad_adamw(c_all, dmod_rows, w, m, v, name):
    def body(c_ref, dm_ref, w_ref, m_ref, v_ref, g_ref, d_ref, mo_ref, vo_ref):
        cv = c_ref[...]
        cond = cv * _sigmoid(cv)
        for l in range(2):
            g = lax.dot_general(cond, dm_ref[l], TN, precision=lax.Precision.HIGHEST, preferred_element_type=F32)
            g_ref[l] = g
            d_ref[l], mo_ref[l], vo_ref[l] = _adamw(w_ref[l], g, m_ref[l], v_ref[l])

    out = jax.ShapeDtypeStruct(w.shape, F32)
    return pl.pallas_call(
        body, name=name, out_shape=[out] * 4, compiler_params=pltpu.CompilerParams(vmem_limit_bytes=VMEM_LIMIT),
    )(c_all, dmod_rows, w, m, v)


REPLICATED = ("ln_g", "ln_b", "gmlp_norm_g", "gmlp_norm_b", "gmlp_ws", "gmlp_bs", "pool_b", "pool_scale",
              "mla_kv_norm_g", "mla_w_uk", "mla_w_uv")
CHUNK_ROWS, ADA_ROW, QNORM_ROW, LOSS_ROW, REP_ROWS = 73, 73, 74, 75, 80
UQ_ROWS, POOLW_ROWS = 96, 32


def _pad_rows(flat2d, rows):
    n, k = flat2d.shape
    return jnp.pad(flat2d, ((0, 0), (0, rows * LANES - k))).reshape(n, rows, LANES)


def _ada_cols_rows(vec):
    return _pad_rows(vec.reshape(2, N_DEV, -1).transpose(1, 0, 2).reshape(N_DEV, -1), 1)


def _unpack_replicated(rep, shapes):
    chunk = sum(s[1] for s in shapes) // N_DEV
    flat, off, out = rep[:, :CHUNK_ROWS].reshape(N_DEV, -1)[:, :chunk].reshape(-1), 0, {}
    for n, size, shape in shapes:
        out[n] = flat[off:off + size].reshape(shape)
        off += size
    cols = 3 * D_MODEL // N_DEV
    out["ada_b"] = rep[:, ADA_ROW, :2 * cols].reshape(N_DEV, 2, cols).transpose(1, 0, 2).reshape(2, -1)
    return out


def kernel(x, c, positions, ada_w, ada_b, ln_g, ln_b, e_w_in, gmlp_norm_g, gmlp_norm_b, gmlp_ws, gmlp_bs, pool_w, pool_b, pool_scale, e_w_out, o_w_in, mla_q_norm_g, mla_kv_norm_g, mla_w_uq, mla_w_uk, mla_w_uv, o_w_out, loss_target, m_ada_w, m_ada_b, m_ln_g, m_ln_b, m_e_w_in, m_gmlp_norm_g, m_gmlp_norm_b, m_gmlp_ws, m_gmlp_bs, m_pool_w, m_pool_b, m_pool_scale, m_e_w_out, m_o_w_in, m_mla_q_norm_g, m_mla_kv_norm_g, m_mla_w_uq, m_mla_w_uk, m_mla_w_uv, m_o_w_out, v_ada_w, v_ada_b, v_ln_g, v_ln_b, v_e_w_in, v_gmlp_norm_g, v_gmlp_norm_b, v_gmlp_ws, v_gmlp_bs, v_pool_w, v_pool_b, v_pool_scale, v_e_w_out, v_o_w_in, v_mla_q_norm_g, v_mla_kv_norm_g, v_mla_w_uq, v_mla_w_uk, v_mla_w_uv, v_o_w_out):
    w_in = dict(ada_w=ada_w, ada_b=ada_b, ln_g=ln_g, ln_b=ln_b, e_w_in=e_w_in, gmlp_norm_g=gmlp_norm_g,
                gmlp_norm_b=gmlp_norm_b, gmlp_ws=gmlp_ws, gmlp_bs=gmlp_bs, pool_w=pool_w, pool_b=pool_b,
                pool_scale=pool_scale, e_w_out=e_w_out, o_w_in=o_w_in, mla_q_norm_g=mla_q_norm_g,
                mla_kv_norm_g=mla_kv_norm_g, mla_w_uq=mla_w_uq, mla_w_uk=mla_w_uk, mla_w_uv=mla_w_uv, o_w_out=o_w_out)
    m_in = dict(ada_w=m_ada_w, ada_b=m_ada_b, ln_g=m_ln_g, ln_b=m_ln_b, e_w_in=m_e_w_in, gmlp_norm_g=m_gmlp_norm_g,
                gmlp_norm_b=m_gmlp_norm_b, gmlp_ws=m_gmlp_ws, gmlp_bs=m_gmlp_bs, pool_w=m_pool_w, pool_b=m_pool_b,
                pool_scale=m_pool_scale, e_w_out=m_e_w_out, o_w_in=m_o_w_in, mla_q_norm_g=m_mla_q_norm_g,
                mla_kv_norm_g=m_mla_kv_norm_g, mla_w_uq=m_mla_w_uq, mla_w_uk=m_mla_w_uk, mla_w_uv=m_mla_w_uv,
                o_w_out=m_o_w_out)
    v_in = dict(ada_w=v_ada_w, ada_b=v_ada_b, ln_g=v_ln_g, ln_b=v_ln_b, e_w_in=v_e_w_in, gmlp_norm_g=v_gmlp_norm_g,
                gmlp_norm_b=v_gmlp_norm_b, gmlp_ws=v_gmlp_ws, gmlp_bs=v_gmlp_bs, pool_w=v_pool_w, pool_b=v_pool_b,
                pool_scale=v_pool_scale, e_w_out=v_e_w_out, o_w_in=v_o_w_in, mla_q_norm_g=v_mla_q_norm_g,
                mla_kv_norm_g=v_mla_kv_norm_g, mla_w_uq=v_mla_w_uq, mla_w_uk=v_mla_w_uk, mla_w_uv=v_mla_w_uv,
                o_w_out=v_o_w_out)
    names = list(w_in)
    seq = x.shape[1]
    d = D_MODEL
    me = 4 * lax.axis_index("x") + 2 * lax.axis_index("y") + lax.axis_index("c")
    ada_cols = ada_w.shape[2]

    ada_b_cols = lax.dynamic_slice_in_dim(ada_b, me * ada_cols, ada_cols, axis=1)
    slab_row = lax.broadcasted_iota(jnp.int32, (8, d), 0)
    slab = jnp.where(slab_row == 0, c, jnp.where(slab_row == 1, jnp.pad(mla_q_norm_g, ((0, 0), (0, d - 32))), 0.0))
    c_all, mod = _adaln_exchange(slab, ada_w,
                                 jnp.broadcast_to(ada_b_cols[:, None, :], (2, 8, ada_cols)), "adaln_exchange")

    w_in_e3, pool_w3 = _all_gather(
        [e_w_in[0].astype(BF16), pool_w.astype(BF16).reshape(POOLW_ROWS, LANES)], "weight_gather")
    h0 = _modulate(x, mod[0], "modulate0")
    proj0, o_in3 = _matmul_cols_nn(h0, w_in_e3, BF16, 512, "even_in", side=([o_w_in[0].astype(BF16)], True))
    o_in_full = o_in3.transpose(1, 0, 2).reshape(d, ODD_IN)
    w_in_o = jnp.concatenate([o_in_full[:, :448], jnp.zeros((d, 64), BF16), o_in_full[:, 448:]], axis=1)
    pool_w_full = pool_w3.reshape(N_DEV, 4, 32, 256).transpose(1, 0, 2, 3).reshape(4, 256, 256)
    g_q = c_all.reshape(N_DEV, 8, d)[:, 1, :32].reshape(1, MLA_Q_RANK)

    ws, bs_col = gmlp_ws[0], gmlp_bs[0].reshape(GMLP_HEADS, GMLP_BLOCK, 1)
    wuk2, wuv2 = mla_w_uk[0].reshape(MLA_KV_RANK, -1), mla_w_uv[0].reshape(MLA_KV_RANK, -1)
    inv = 1.0 / (ROPE_THETA ** (jnp.arange(0, MLA_ROPE, 2, dtype=F32) / MLA_ROPE))
    ang = positions[0].astype(F32)[:, None] * inv
    cosp = jnp.tile(jnp.cos(ang), (1, 4))
    sinp = jnp.tile(jnp.concatenate([-jnp.sin(ang), jnp.sin(ang)], axis=1), (1, 2))

    mix0, w_out_e3 = _even_fwd(proj0, ws, bs_col, gmlp_norm_g, gmlp_norm_b, pool_w_full, pool_b, pool_scale, "even_mix",
                               side=([e_w_out[0].astype(BF16)], True))
    w_out_e = w_out_e3.reshape(-1, d)
    y0, uq3 = _matmul([(mix0, w_out_e)], "nn", F32, seq, d, 512, 1024, "even_out",
                      side=([mla_w_uq.astype(BF16).reshape(UQ_ROWS, LANES)], True))
    uq_full = uq3.reshape(MLA_Q_RANK, MLA_HEADS, MLA_NOPE + MLA_ROPE)
    w_uq_n = uq_full[:, :, :MLA_NOPE].reshape(MLA_Q_RANK, -1)
    w_uq_r = uq_full[:, :, MLA_NOPE:].reshape(MLA_Q_RANK, -1)
    w_uq = jnp.concatenate([w_uq_n, w_uq_r], axis=1)
    x1, h1 = _resid_ln(x, y0, mod[0], ln_g[0:1], ln_b[0:1], mod[1], "resid_ln0")

    (proj1,) = _matmul([(h1, w_in_o)], "nn", BF16, seq, ODD_IN_PAD, 512, ODD_IN_PAD, "odd_in")
    qn, kp = _mla_prep(proj1, cosp, sinp, g_q, mla_kv_norm_g, "mla_prep")
    (q_up,) = _matmul([(qn, w_uq)], "nn", BF16, seq, 3072, 512, 3072, "q_up")
    qp = _q_heads(q_up, cosp, sinp, wuk2, "q_heads")
    o_lat, lse, w_out_o3 = _attn_fwd(qp, kp, "attn_fwd", side=([o_w_out[0].astype(BF16)], True))
    w_out_o = w_out_o3.reshape(-1, d)
    gated = _o_gate(o_lat, proj1, wuv2, "o_gate")
    (y1,) = _matmul([(gated, w_out_o)], "nn", F32, seq, d, 512, 1024, "odd_out")

    dy1, dxres1, red2 = _final_ln_loss_bwd(x1, y1, mod[1], ln_g[1:2], ln_b[1:2], loss_target, "final_ln_loss")
    (dgated,) = _matmul([(dy1, w_out_o)], "nt", BF16, seq, MLA_WIDTH, 512, MLA_WIDTH, "odd_out_dx")
    (g_w_out_o,) = _matmul([(gated, dy1)], "tn", BF16, MLA_WIDTH, d, 256, d, "odd_out_dw")
    dproj1_z, do_lat, g_wuv = _o_gate_bwd(dgated, o_lat, proj1, wuv2, "o_gate_bwd")
    dqp, dkp, dvv, r_o_out = _attn_bwd(qp, kp, o_lat, do_lat, lse, "attn_bwd",
                                       side=([g_w_out_o.reshape(N_DEV, -1, d)], False))
    dq_nope, dq_rope, g_wuk = _q_heads_bwd(dqp, q_up, cosp, sinp, wuk2, "q_heads_bwd")
    (dqn,) = _matmul([(dq_nope, w_uq_n), (dq_rope, w_uq_r)], "nt", F32, seq, MLA_Q_RANK, 512, 256, "q_up_dx")
    (g_wuq_n,) = _matmul([(qn, dq_nope)], "tn", F32, MLA_Q_RANK, MLA_WIDTH, 256, MLA_WIDTH, "q_up_dw_nope")
    (g_wuq_r,) = _matmul([(qn, dq_rope)], "tn", F32, MLA_Q_RANK, 1024, 256, 1024, "q_up_dw_rope")
    dproj1, red_mla = _mla_prep_bwd(proj1, dqn, dkp, dvv, cosp, sinp, g_q, mla_kv_norm_g, dproj1_z, "mla_prep_bwd")
    (dh1,) = _matmul([(dproj1, w_in_o)], "nt", F32, seq, d, 512, d, "odd_in_dx")
    part_uq = jnp.concatenate([g_wuq_n.reshape(MLA_Q_RANK, MLA_HEADS, MLA_NOPE),
                               g_wuq_r.reshape(MLA_Q_RANK, MLA_HEADS, MLA_ROPE)], axis=2).reshape(
                                   (N_DEV,) + mla_w_uq.shape[1:])
    g_w_in_o, r_uq = _matmul([(h1, dproj1)], "tn", BF16, d, ODD_IN_PAD, 256, ODD_IN_PAD // 2, "odd_in_dw", n_outer=True,
                             side=([part_uq], False))
    part_o_in = jnp.concatenate([g_w_in_o[:, :448], g_w_in_o[:, 512:]], axis=1).reshape(d, N_DEV, -1).transpose(1, 0, 2)
    dy0, dxres0, red1 = _mid_bwd(dh1, dxres1, x, y0, mod[0], mod[1], ln_g[0:1], ln_b[0:1], "mid_bwd")
    (dmix,) = _matmul([(dy0, w_out_e)], "nt", BF16, seq, 2048, 512, 2048, "even_out_dx")
    (g_w_out_e,) = _matmul([(mix0, dy0)], "tn", BF16, 2048, d, 256, d, "even_out_dw")
    dproj0, g_ws, g_bs, g_ng, g_nb, g_pw, g_pb, g_ps, r_o_in = _even_bwd(
        proj0, dmix, ws, bs_col, gmlp_norm_g, gmlp_norm_b, pool_w_full, pool_b, pool_scale, "even_mix_bwd",
        side=([part_o_in], False))
    part_pw = g_pw.reshape(4, N_DEV, 32, 256).transpose(1, 0, 2, 3)
    part_e_in, r_e_out, r_pw = _matmul_cols_tn(h0, dproj0, w_in_e3.shape[2], BF16, 512, "even_in_dw",
                                               side=([g_w_out_e.reshape(N_DEV, -1, d), part_pw], False))
    dh0, r_e_in = _matmul_cols_nt(dproj0, w_in_e3, F32, 512, "even_in_dx", side=([part_e_in], False))
    grad_x, red0 = _first_bwd(dh0, dxres0, x, mod[0], "first_bwd")

    t_mask = lax.broadcasted_iota(jnp.int32, (GMLP_BLOCK, GMLP_BLOCK), 0) // CHUNK
    s_mask = lax.broadcasted_iota(jnp.int32, (GMLP_BLOCK, GMLP_BLOCK), 1) // CHUNK
    part = {
        "ln_g": jnp.stack([red1[2], red2[0]]), "ln_b": jnp.stack([red1[3], red2[1]]),
        "gmlp_norm_g": g_ng, "gmlp_norm_b": g_nb,
        "gmlp_ws": jnp.where(s_mask <= t_mask, g_ws, 0.0), "gmlp_bs": g_bs,
        "pool_b": g_pb, "pool_scale": g_ps, "mla_kv_norm_g": red_mla[1, :MLA_KV_RANK],
        "mla_w_uk": g_wuk, "mla_w_uv": g_wuv,
    }
    dmod = jnp.stack([jnp.concatenate([red0[1], red0[0], red1[4]]),
                      jnp.concatenate([red1[1], red1[0], red2[2]])])

    loss_row = jnp.pad(jnp.broadcast_to((0.5 / d * jnp.sum(red2[3])).reshape(1, 1, 1), (N_DEV, 1, 1)),
                       ((0, 0), (0, 0), (0, LANES - 1)))
    part_small = jnp.concatenate([
        _pad_rows(jnp.concatenate([part[n].reshape(-1) for n in REPLICATED]).reshape(N_DEV, -1), CHUNK_ROWS),
        jnp.pad(jnp.concatenate([_ada_cols_rows(dmod), _pad_rows(red_mla[0].reshape(N_DEV, -1), 1), loss_row], axis=1),
                ((0, 0), (0, REP_ROWS - LOSS_ROW - 1), (0, 0)))], axis=1)
    (r_small,) = _scatter_parts([part_small], "grad_scatter")

    res = {"e_w_in": _sum_parts_adamw(r_e_in, e_w_in, m_e_w_in, v_e_w_in, "adamw_e_w_in"),
           "o_w_in": _sum_parts_adamw(r_o_in, o_w_in, m_o_w_in, v_o_w_in, "adamw_o_w_in"),
           "e_w_out": _sum_parts_adamw(r_e_out, e_w_out, m_e_w_out, v_e_w_out, "adamw_e_w_out"),
           "o_w_out": _sum_parts_adamw(r_o_out, o_w_out, m_o_w_out, v_o_w_out, "adamw_o_w_out"),
           "mla_w_uq": _sum_parts_adamw_whole(r_uq, mla_w_uq, m_mla_w_uq, v_mla_w_uq, "adamw_w_uq"),
           "pool_w": _sum_parts_adamw_whole(r_pw, pool_w, m_pool_w, v_pool_w, "adamw_pool_w")}
    small_sum = _sum_parts(r_small, "small_sum")
    loss = small_sum[LOSS_ROW, 0]
    (rep_sum,) = _all_gather([small_sum], "replicated_gather")
    grads = _unpack_replicated(rep_sum, [(n, w_in[n].size, w_in[n].shape) for n in REPLICATED])
    grads["mla_q_norm_g"] = small_sum[QNORM_ROW:QNORM_ROW + 1, :32]
    small_names = list(grads)
    deltas, new_ms, new_vs = _adamw_many([grads[n] for n in small_names], [w_in[n] for n in small_names],
                                         [m_in[n] for n in small_names], [v_in[n] for n in small_names], "small_adamw")
    for k, n in enumerate(small_names):
        res[n] = [grads[n], deltas[k], new_ms[k], new_vs[k]]
    dmod_all = r_small[:, ADA_ROW, :2 * ada_cols].reshape(N_DEV, 2, ada_cols).transpose(1, 0, 2)
    dmod_rows = jnp.pad(dmod_all[:, :, None, :], ((0, 0), (0, 0), (0, 7), (0, 0))).reshape(2, 8 * N_DEV, ada_cols)
    res["ada_w"] = _ada_w_grad_adamw(c_all, dmod_rows, ada_w, m_ada_w, v_ada_w, "ada_w_adamw")

    return (loss, grad_x, *[res[n][0] for n in names], *[res[n][1] for n in names],
            *[res[n][2] for n in names], *[res[n][3] for n in names])
```

```python
import functools

import jax
import jax.numpy as jnp
from jax import lax
from jax.experimental import pallas as pl
from jax.experimental.pallas import tpu as pltpu

F32 = jnp.float32
BF16 = jnp.bfloat16

D_MODEL = 1024
CHUNK = 64
LN_EPS = 1e-5
GMLP_HEADS = 4
GMLP_HEAD_DIM = 256
GMLP_BLOCK = 128
POOL_WINDOWS = (2, 4, 8, 16)
POOL_GROUP_DIM = 256
POOL_HALO = 16
MLA_HEADS = 16
MLA_NOPE = 128
MLA_ROPE = 64
MLA_Q_RANK = 256
MLA_KV_RANK = 128
MLA_WIDTH = 2048
ODD_IN = 2496
ODD_IN_PAD = 2560
ROPE_THETA = 10000.0
ATTN_SCALE = (MLA_NOPE + MLA_ROPE) ** -0.5
ATTN_SCALE_LOG2 = ATTN_SCALE * 1.4426950408889634
DEEPNORM_ALPHA = 4.0 ** 0.25
ADAM_LR, ADAM_B1, ADAM_B2, ADAM_EPS, ADAM_WD, ADAM_STEP = 0.001, 0.9, 0.999, 1e-8, 0.01, 10
N_DEV = 8
LANES = 1024
VMEM_LIMIT = 56 * 1024 * 1024
MESH = pl.DeviceIdType.MESH

NT = (((1,), (1,)), ((), ()))
NN = (((1,), (0,)), ((), ()))
TN = (((0,), (0,)), ((), ()))


def _params(n_axes):
    return pltpu.CompilerParams(dimension_semantics=("arbitrary",) * n_axes, vmem_limit_bytes=VMEM_LIMIT)


def _dot(a, b, dn):
    return lax.dot_general(a.astype(BF16), b.astype(BF16), dn, preferred_element_type=F32)


def _sigmoid(z):
    return 1.0 / (1.0 + jnp.exp(-z))


def _colsum(t):
    return jnp.sum(t, axis=0, keepdims=True)


def _exchange_copies(g_refs, r_refs, send_sems, recv_sems, local_sems, gather):
    x, y, c, me = _place()
    n_arr = len(g_refs)

    def src(t, slot):
        return g_refs[t] if gather else g_refs[t].at[slot]

    own = [pltpu.make_async_copy(src(t, me), r_refs[t].at[me], local_sems.at[t]) for t in range(n_arr)]
    sends, recvs = [], []
    for r in range(1, N_DEV):
        peer, lin = _flip(x, y, c, r)
        for t in range(n_arr):
            k = n_arr * (r - 1) + t
            sends.append(pltpu.make_async_remote_copy(
                src_ref=src(t, lin), dst_ref=r_refs[t].at[me], send_sem=send_sems.at[k], recv_sem=recv_sems.at[k],
                device_id=peer, device_id_type=MESH))
            recvs.append(pltpu.make_async_remote_copy(
                src_ref=src(t, lin), dst_ref=r_refs[t].at[lin], send_sem=send_sems.at[k], recv_sem=recv_sems.at[k],
                device_id=(x, y, c), device_id_type=MESH))
    return own, sends, recvs


def _exchange_start(copies):
    own, sends, _ = copies
    for cp in own + sends:
        cp.start()


def _exchange_wait(copies):
    own, sends, recvs = copies
    for cp in recvs:
        cp.wait_recv()
    for cp in sends:
        cp.wait_send()
    for cp in own:
        cp.wait()


def _exchange_extras(parts, gather):
    shapes = [jax.ShapeDtypeStruct(((N_DEV,) + p.shape) if gather else p.shape, p.dtype) for p in parts]
    n = len(parts) * (N_DEV - 1)
    return shapes, [pltpu.SemaphoreType.DMA((n,)), pltpu.SemaphoreType.DMA((n,)), pltpu.SemaphoreType.DMA((len(parts),))]


def _grid_call(body, name, grid, in_specs, out_specs, out_shape, args, scratch=(), side=None):
    if side is None:
        return pl.pallas_call(body, name=name, grid=grid, in_specs=in_specs, out_specs=out_specs,
                              out_shape=out_shape, scratch_shapes=list(scratch),
                              compiler_params=_params(len(grid)))(*args)
    parts, gather = side
    n_in, n_out, n_sc, n_arr = len(args), len(out_shape), len(scratch), len(parts)
    side_shapes, side_sems = _exchange_extras(parts, gather)

    def wrapped(*refs):
        ins, g_refs = refs[:n_in], refs[n_in:n_in + n_arr]
        outs = refs[n_in + n_arr:n_in + n_arr + n_out]
        r_refs = refs[n_in + n_arr + n_out:n_in + 2 * n_arr + n_out]
        sc = refs[n_in + 2 * n_arr + n_out:n_in + 2 * n_arr + n_out + n_sc]
        copies = _exchange_copies(g_refs, r_refs, *refs[-3:], gather)
        ids = [pl.program_id(a) for a in range(len(grid))]
        first = functools.reduce(jnp.logical_and, [i == 0 for i in ids])
        last = functools.reduce(jnp.logical_and, [i == g - 1 for i, g in zip(ids, grid)])

        @pl.when(first)
        def _():
            _exchange_start(copies)

        body(*ins, *outs, *sc)

        @pl.when(last)
        def _():
            _exchange_wait(copies)

    anyspace = pl.BlockSpec(memory_space=pl.ANY)
    return pl.pallas_call(
        wrapped, name=name, grid=grid, in_specs=list(in_specs) + [anyspace] * n_arr,
        out_specs=list(out_specs) + [anyspace] * n_arr, out_shape=list(out_shape) + side_shapes,
        scratch_shapes=list(scratch) + side_sems, compiler_params=_params(len(grid)),
    )(*args, *parts)


def _matmul(pairs, mode, out_dtype, m, n, tm, tn, name, side=None, n_outer=False):
    dn = {"nn": NN, "nt": NT, "tn": TN}[mode]
    tm, tn = min(tm, m), min(tn, n)
    n_pairs = len(pairs)
    grid = (n // tn, m // tm) if n_outer else (m // tm, n // tn)

    def ij(f):
        return (lambda j, i: f(i, j)) if n_outer else f

    def body(*refs):
        o_ref = refs[-1]
        acc = None
        for p in range(n_pairs):
            t = _dot(refs[2 * p][...], refs[2 * p + 1][...], dn)
            acc = t if acc is None else acc + t
        o_ref[...] = acc.astype(o_ref.dtype)

    in_specs, args = [], []
    for a, b in pairs:
        if mode == "nn":
            k = a.shape[1]
            in_specs += [pl.BlockSpec((tm, k), ij(lambda i, j: (i, 0))), pl.BlockSpec((k, tn), ij(lambda i, j: (0, j)))]
        elif mode == "nt":
            k = a.shape[1]
            in_specs += [pl.BlockSpec((tm, k), ij(lambda i, j: (i, 0))), pl.BlockSpec((tn, k), ij(lambda i, j: (j, 0)))]
        else:
            k = a.shape[0]
            in_specs += [pl.BlockSpec((k, tm), ij(lambda i, j: (0, i))), pl.BlockSpec((k, tn), ij(lambda i, j: (0, j)))]
        args += [a, b]
    return _grid_call(body, name, grid, in_specs, [pl.BlockSpec((tm, tn), ij(lambda i, j: (i, j)))],
                      [jax.ShapeDtypeStruct((m, n), out_dtype)], args, side=side)


def _matmul_cols_nn(a, w3, out_dtype, tm, name, side=None):
    m, k = a.shape
    _, _, n = w3.shape
    tm = min(tm, m)

    def body(a_ref, w_ref, o_ref):
        av = a_ref[...]
        for j in range(N_DEV):
            o_ref[:, n * j:n * (j + 1)] = _dot(av, w_ref[j], NN).astype(o_ref.dtype)

    return _grid_call(
        body, name, (m // tm,),
        [pl.BlockSpec((tm, k), lambda i: (i, 0)), pl.BlockSpec((N_DEV, k, n), lambda i: (0, 0, 0))],
        [pl.BlockSpec((tm, N_DEV * n), lambda i: (i, 0))], [jax.ShapeDtypeStruct((m, N_DEV * n), out_dtype)], [a, w3],
        side=side)


def _matmul_cols_nt(a, w3, out_dtype, tm, name, side=None):
    m = a.shape[0]
    _, k, n = w3.shape
    tm = min(tm, m)

    def body(a_ref, w_ref, o_ref):
        acc = _dot(a_ref[:, 0:n], w_ref[0], NT)
        for j in range(1, N_DEV):
            acc = acc + _dot(a_ref[:, n * j:n * (j + 1)], w_ref[j], NT)
        o_ref[...] = acc.astype(o_ref.dtype)

    return _grid_call(
        body, name, (m // tm,),
        [pl.BlockSpec((tm, N_DEV * n), lambda i: (i, 0)), pl.BlockSpec((N_DEV, k, n), lambda i: (0, 0, 0))],
        [pl.BlockSpec((tm, k), lambda i: (i, 0))], [jax.ShapeDtypeStruct((m, k), out_dtype)], [a, w3], side=side)


def _matmul_cols_tn(a, b, n, out_dtype, tk, name, side=None):
    m, k = a.shape
    tk = min(tk, k)

    def body(a_ref, b_ref, o_ref):
        o_ref[...] = _dot(a_ref[...], b_ref[...], TN).astype(o_ref.dtype)

    return _grid_call(
        body, name, (N_DEV, k // tk),
        [pl.BlockSpec((m, tk), lambda j, i: (0, i)), pl.BlockSpec((m, n), lambda j, i: (0, j))],
        [pl.BlockSpec((None, tk, n), lambda j, i: (j, i, 0))], [jax.ShapeDtypeStruct((N_DEV, k, n), out_dtype)], [a, b],
        side=side)


def _rows3(tm, d):
    return pl.BlockSpec((None, tm, d), lambda i: (0, i, 0))


def _modulate(x, mod, name):
    _, s, d = x.shape
    tm = min(s, 512)

    def body(x_ref, m_ref, h_ref):
        shift, scale = m_ref[0:1, 0:d], m_ref[0:1, d:2 * d]
        h_ref[...] = (x_ref[...] * (1.0 + scale) + shift).astype(BF16)

    return pl.pallas_call(
        body, name=name, grid=(s // tm,),
        in_specs=[_rows3(tm, d), pl.BlockSpec((8, 3 * d), lambda i: (0, 0))],
        out_specs=pl.BlockSpec((tm, d), lambda i: (i, 0)),
        out_shape=jax.ShapeDtypeStruct((s, d), BF16), compiler_params=_params(1),
    )(x, mod)


def _ln_stats(r):
    mu = jnp.mean(r, axis=-1, keepdims=True)
    rc = r - mu
    var = jnp.mean(rc * rc, axis=-1, keepdims=True)
    rstd = lax.rsqrt(var + LN_EPS)
    return rc * rstd, rstd


def _ln_bwd(dxhat, xhat, rstd):
    return rstd * (dxhat - jnp.mean(dxhat, axis=-1, keepdims=True)
                   - xhat * jnp.mean(dxhat * xhat, axis=-1, keepdims=True))


def _resid_ln(x, y, mod, g, b, mod_next, name):
    _, s, d = x.shape
    tm = min(s, 512)

    def body(x_ref, y_ref, m_ref, g_ref, b_ref, mn_ref, o_ref, h_ref):
        gate = m_ref[0:1, 2 * d:3 * d]
        xhat, _ = _ln_stats(DEEPNORM_ALPHA * x_ref[...] + (1.0 + gate) * y_ref[...])
        out = xhat * g_ref[...] + b_ref[...]
        o_ref[...] = out
        h_ref[...] = (out * (1.0 + mn_ref[0:1, d:2 * d]) + mn_ref[0:1, 0:d]).astype(BF16)

    row = pl.BlockSpec((tm, d), lambda i: (i, 0))
    vec = pl.BlockSpec((1, d), lambda i: (0, 0))
    modspec = pl.BlockSpec((8, 3 * d), lambda i: (0, 0))
    return pl.pallas_call(
        body, name=name, grid=(s // tm,),
        in_specs=[_rows3(tm, d), row, modspec, vec, vec, modspec],
        out_specs=[_rows3(tm, d), row],
        out_shape=[jax.ShapeDtypeStruct((1, s, d), F32), jax.ShapeDtypeStruct((s, d), BF16)],
        compiler_params=_params(1),
    )(x, y, mod, g, b, mod_next)


def _final_ln_loss_bwd(x, y, mod, g, b, target, name):
    _, s, d = x.shape
    tm = min(s, 256)

    def body(x_ref, y_ref, m_ref, g_ref, b_ref, t_ref, dy_ref, dx_ref, red_ref):
        @pl.when(pl.program_id(0) == 0)
        def _():
            red_ref[...] = jnp.zeros_like(red_ref)

        gate = m_ref[0:1, 2 * d:3 * d]
        yv = y_ref[...]
        xhat, rstd = _ln_stats(DEEPNORM_ALPHA * x_ref[...] + (1.0 + gate) * yv)
        err = xhat * g_ref[...] + b_ref[...] - t_ref[...]
        dout = err * (1.0 / d)
        dr = _ln_bwd(dout * g_ref[...], xhat, rstd)
        dy_ref[...] = ((1.0 + gate) * dr).astype(BF16)
        dx_ref[...] = DEEPNORM_ALPHA * dr
        red_ref[0:1, :] += _colsum(dout * xhat)
        red_ref[1:2, :] += _colsum(dout)
        red_ref[2:3, :] += _colsum(dr * yv)
        red_ref[3:4, :] += _colsum(err * err)

    row = pl.BlockSpec((tm, d), lambda i: (i, 0))
    vec = pl.BlockSpec((1, d), lambda i: (0, 0))
    return pl.pallas_call(
        body, name=name, grid=(s // tm,),
        in_specs=[_rows3(tm, d), row, pl.BlockSpec((8, 3 * d), lambda i: (0, 0)), vec, vec, _rows3(tm, d)],
        out_specs=[row, row, pl.BlockSpec((8, d), lambda i: (0, 0))],
        out_shape=[jax.ShapeDtypeStruct((s, d), BF16), jax.ShapeDtypeStruct((s, d), F32),
                   jax.ShapeDtypeStruct((8, d), F32)],
        compiler_params=_params(1),
    )(x, y, mod, g, b, target)


def _mid_bwd(dh, dxres, x, y, mod_lo, mod_hi, g, b, name):
    _, s, d = x.shape
    tm = min(s, 256)

    def body(dh_ref, dxr_ref, x_ref, y_ref, ml_ref, mh_ref, g_ref, b_ref, dy_ref, dx_ref, red_ref):
        @pl.when(pl.program_id(0) == 0)
        def _():
            red_ref[...] = jnp.zeros_like(red_ref)

        gate = ml_ref[0:1, 2 * d:3 * d]
        scale_hi = mh_ref[0:1, d:2 * d]
        yv, dhv = y_ref[...], dh_ref[...]
        xhat, rstd = _ln_stats(DEEPNORM_ALPHA * x_ref[...] + (1.0 + gate) * yv)
        x_mid = xhat * g_ref[...] + b_ref[...]
        dx_mid = dxr_ref[...] + dhv * (1.0 + scale_hi)
        dr = _ln_bwd(dx_mid * g_ref[...], xhat, rstd)
        dy_ref[...] = ((1.0 + gate) * dr).astype(BF16)
        dx_ref[...] = DEEPNORM_ALPHA * dr
        red_ref[0:1, :] += _colsum(dhv * x_mid)
        red_ref[1:2, :] += _colsum(dhv)
        red_ref[2:3, :] += _colsum(dx_mid * xhat)
        red_ref[3:4, :] += _colsum(dx_mid)
        red_ref[4:5, :] += _colsum(dr * yv)

    row = pl.BlockSpec((tm, d), lambda i: (i, 0))
    vec = pl.BlockSpec((1, d), lambda i: (0, 0))
    modspec = pl.BlockSpec((8, 3 * d), lambda i: (0, 0))
    return pl.pallas_call(
        body, name=name, grid=(s // tm,),
        in_specs=[row, row, _rows3(tm, d), row, modspec, modspec, vec, vec],
        out_specs=[row, row, pl.BlockSpec((8, d), lambda i: (0, 0))],
        out_shape=[jax.ShapeDtypeStruct((s, d), BF16), jax.ShapeDtypeStruct((s, d), F32),
                   jax.ShapeDtypeStruct((8, d), F32)],
        compiler_params=_params(1),
    )(dh, dxres, x, y, mod_lo, mod_hi, g, b)


def _first_bwd(dh, dxres, x, mod, name):
    _, s, d = x.shape
    tm = min(s, 512)

    def body(dh_ref, dxr_ref, x_ref, m_ref, gx_ref, red_ref):
        @pl.when(pl.program_id(0) == 0)
        def _():
            red_ref[...] = jnp.zeros_like(red_ref)

        scale = m_ref[0:1, d:2 * d]
        dhv = dh_ref[...]
        gx_ref[...] = dxr_ref[...] + dhv * (1.0 + scale)
        red_ref[0:1, :] += _colsum(dhv * x_ref[...])
        red_ref[1:2, :] += _colsum(dhv)

    row = pl.BlockSpec((tm, d), lambda i: (i, 0))
    return pl.pallas_call(
        body, name=name, grid=(s // tm,),
        in_specs=[row, row, _rows3(tm, d), pl.BlockSpec((8, 3 * d), lambda i: (0, 0))],
        out_specs=[_rows3(tm, d), pl.BlockSpec((8, d), lambda i: (0, 0))],
        out_shape=[jax.ShapeDtypeStruct((1, s, d), F32), jax.ShapeDtypeStruct((8, d), F32)],
        compiler_params=_params(1),
    )(dh, dxres, x, mod)


EVEN_TM = 256


def _gmlp_mask():
    t = lax.broadcasted_iota(jnp.int32, (GMLP_BLOCK, GMLP_BLOCK), 0) // CHUNK
    s = lax.broadcasted_iota(jnp.int32, (GMLP_BLOCK, GMLP_BLOCK), 1) // CHUNK
    return s <= t


def _window_sum(ext, win, back):
    n = ext.shape[0]
    k = 1
    while k < win:
        ext = ext + pltpu.roll(ext, k if back else n - k, 0)
        k *= 2
    return ext


def _inv_count(row0, rows, win):
    t = row0 + lax.broadcasted_iota(jnp.int32, (rows, 1), 0)
    return t, 1.0 / jnp.minimum(t + 1, win).astype(F32)


def _pooled(xb, halo, row0, win):
    tm = xb.shape[0]
    sums = _window_sum(jnp.concatenate([halo, xb], axis=0), win, True)[POOL_HALO:]
    _, inv = _inv_count(row0, tm, win)
    return sums * inv - xb


def _even_fwd(proj, ws, bs_col, ng, nb, pw, pb, ps, name, side=None):
    s = proj.shape[0]
    tm = min(s, EVEN_TM)
    hd, gd = GMLP_HEAD_DIM, POOL_GROUP_DIM

    def body(p_ref, halo_ref, ws_ref, bs_ref, ng_ref, nb_ref, pw_ref, pb_ref, ps_ref, m_ref):
        i = pl.program_id(0)
        mask = _gmlp_mask()
        for h in range(GMLP_HEADS):
            wm = jnp.where(mask, ws_ref[h], 0.0).astype(BF16)
            for blk in range(tm // GMLP_BLOCK):
                rows = slice(blk * GMLP_BLOCK, (blk + 1) * GMLP_BLOCK)
                cu, cv, cz = h * hd, 1024 + h * hd, 2048 + h * hd
                vhat, _ = _ln_stats(p_ref[rows, cv:cv + hd].astype(F32))
                vn = vhat * ng_ref[...] + nb_ref[...]
                sv = _dot(wm, vn, NN) + bs_ref[h]
                za = p_ref[rows, cz:cz + hd].astype(F32)
                m_ref[rows, cu:cu + hd] = (p_ref[rows, cu:cu + hd].astype(F32) * sv * (za * _sigmoid(za))).astype(BF16)
        for g, win in enumerate(POOL_WINDOWS):
            cx, cz = 3072 + g * gd, 4096 + g * gd
            halo = jnp.where(i > 0, halo_ref[:, g * gd:(g + 1) * gd].astype(F32), 0.0)
            pooled = _pooled(p_ref[:, cx:cx + gd].astype(F32), halo, i * tm, win)
            yb = _dot(pooled, pw_ref[g], NN) + pb_ref[:, g * gd:(g + 1) * gd]
            zb = p_ref[:, cz:cz + gd].astype(F32)
            m_ref[:, 1024 + g * gd:1024 + (g + 1) * gd] = (
                yb * ps_ref[:, g * gd:(g + 1) * gd] * (zb * _sigmoid(zb))).astype(BF16)

    hb = tm // POOL_HALO
    return _grid_call(
        body, name, (s // tm,),
        [
            pl.BlockSpec((tm, 5120), lambda i: (i, 0)),
            pl.BlockSpec((POOL_HALO, 1024), lambda i: (jnp.maximum(i * hb - 1, 0), 3)),
            pl.BlockSpec((GMLP_HEADS, GMLP_BLOCK, GMLP_BLOCK), lambda i: (0, 0, 0)),
            pl.BlockSpec((GMLP_HEADS, GMLP_BLOCK, 1), lambda i: (0, 0, 0)),
            pl.BlockSpec((1, hd), lambda i: (0, 0)), pl.BlockSpec((1, hd), lambda i: (0, 0)),
            pl.BlockSpec((4, gd, gd), lambda i: (0, 0, 0)),
            pl.BlockSpec((1, 1024), lambda i: (0, 0)), pl.BlockSpec((1, 1024), lambda i: (0, 0)),
        ],
        [pl.BlockSpec((tm, 2048), lambda i: (i, 0))], [jax.ShapeDtypeStruct((s, 2048), BF16)],
        [proj, proj, ws, bs_col, ng, nb, pw, pb, ps], side=side)


def _even_bwd(proj, dm, ws, bs_col, ng, nb, pw, pb, ps, name, side=None):
    s = proj.shape[0]
    tm = min(s, EVEN_TM)
    hd, gd = GMLP_HEAD_DIM, POOL_GROUP_DIM
    n_tiles = s // tm

    def body(p_ref, halo_ref, zbn_ref, dm_ref, dbn_ref, ws_ref, bs_ref, ng_ref, nb_ref, pw_ref, pb_ref, ps_ref,
             dp_ref, dws_ref, dbs_ref, dng_ref, dnb_ref, dpw_ref, dpb_ref, dps_ref):
        i = pl.program_id(0)

        @pl.when(i == 0)
        def _():
            for r in (dws_ref, dbs_ref, dng_ref, dnb_ref, dpw_ref, dpb_ref, dps_ref):
                r[...] = jnp.zeros_like(r)

        mask = _gmlp_mask()
        for h in range(GMLP_HEADS):
            wm = jnp.where(mask, ws_ref[h], 0.0).astype(BF16)
            for blk in range(tm // GMLP_BLOCK):
                rows = slice(blk * GMLP_BLOCK, (blk + 1) * GMLP_BLOCK)
                cu, cv, cz = h * hd, 1024 + h * hd, 2048 + h * hd
                vhat, rstd = _ln_stats(p_ref[rows, cv:cv + hd].astype(F32))
                vn = (vhat * ng_ref[...] + nb_ref[...]).astype(BF16)
                sv = _dot(wm, vn, NN) + bs_ref[h]
                u, za = p_ref[rows, cu:cu + hd].astype(F32), p_ref[rows, cz:cz + hd].astype(F32)
                da = dm_ref[rows, cu:cu + hd].astype(F32)
                sig = _sigmoid(za)
                sa = za * sig
                dau = da * u
                dsv = dau * sa
                dp_ref[rows, cu:cu + hd] = (da * sv * sa).astype(BF16)
                dp_ref[rows, cz:cz + hd] = (dau * sv * (sig * (1.0 + za * (1.0 - sig)))).astype(BF16)
                dsv_b = dsv.astype(BF16)
                dbs_ref[h] += jnp.sum(dsv, axis=1, keepdims=True)
                dws_ref[h] += _dot(dsv_b, vn, NT)
                dvn = _dot(wm, dsv_b, TN)
                dng_ref[...] += _colsum(dvn * vhat)
                dnb_ref[...] += _colsum(dvn)
                dp_ref[rows, cv:cv + hd] = _ln_bwd(dvn * ng_ref[...], vhat, rstd).astype(BF16)

        row0 = i * tm
        for g, win in enumerate(POOL_WINDOWS):
            cx, cz, cd = 3072 + g * gd, 4096 + g * gd, 1024 + g * gd
            gs = slice(g * gd, (g + 1) * gd)
            halo = jnp.where(i > 0, halo_ref[:, gs].astype(F32), 0.0)
            xb = p_ref[:, cx:cx + gd].astype(F32)
            pooled = _pooled(xb, halo, row0, win).astype(BF16)
            scale_g = ps_ref[:, gs]
            yb = _dot(pooled, pw_ref[g], NN) + pb_ref[:, gs]
            zb, db = p_ref[:, cz:cz + gd].astype(F32), dm_ref[:, cd:cd + gd].astype(F32)
            sig = _sigmoid(zb)
            dyp = db * (zb * sig)
            dp_ref[:, cz:cz + gd] = (db * yb * scale_g * (sig * (1.0 + zb * (1.0 - sig)))).astype(BF16)
            dps_ref[:, gs] += _colsum(dyp * yb)
            dpb_ref[:, gs] += _colsum(dyp * scale_g)
            zb_ext = jnp.concatenate([zb, zbn_ref[:, gs].astype(F32)], axis=0)
            db_ext = jnp.concatenate([db, dbn_ref[:, gs].astype(F32)], axis=0)
            dy_ext = (db_ext * (zb_ext * _sigmoid(zb_ext)) * scale_g).astype(BF16)
            dpw_ref[g] += _dot(pooled, dy_ext[:tm], TN)
            dpooled = _dot(dy_ext, pw_ref[g], NT)
            t, inv = _inv_count(row0, tm + POOL_HALO, win)
            w_ext = jnp.where(t < s, dpooled * inv, 0.0)
            dp_ref[:, cx:cx + gd] = (_window_sum(w_ext, win, False)[:tm] - dpooled[:tm]).astype(BF16)

    hb = tm // POOL_HALO
    last = s // POOL_HALO - 1
    small = lambda shape: pl.BlockSpec(shape, lambda i: (0,) * len(shape))
    return _grid_call(
        body, name, (n_tiles,),
        [
            pl.BlockSpec((tm, 5120), lambda i: (i, 0)),
            pl.BlockSpec((POOL_HALO, 1024), lambda i: (jnp.maximum(i * hb - 1, 0), 3)),
            pl.BlockSpec((POOL_HALO, 1024), lambda i: (jnp.minimum((i + 1) * hb, last), 4)),
            pl.BlockSpec((tm, 2048), lambda i: (i, 0)),
            pl.BlockSpec((POOL_HALO, 1024), lambda i: (jnp.minimum((i + 1) * hb, last), 1)),
            small((GMLP_HEADS, GMLP_BLOCK, GMLP_BLOCK)), small((GMLP_HEADS, GMLP_BLOCK, 1)),
            small((1, hd)), small((1, hd)), small((4, gd, gd)), small((1, 1024)), small((1, 1024)),
        ],
        [
            pl.BlockSpec((tm, 5120), lambda i: (i, 0)),
            small((GMLP_HEADS, GMLP_BLOCK, GMLP_BLOCK)), small((GMLP_HEADS, GMLP_BLOCK, 1)),
            small((1, hd)), small((1, hd)), small((4, gd, gd)), small((1, 1024)), small((1, 1024)),
        ],
        [
            jax.ShapeDtypeStruct((s, 5120), BF16),
            jax.ShapeDtypeStruct((GMLP_HEADS, GMLP_BLOCK, GMLP_BLOCK), F32),
            jax.ShapeDtypeStruct((GMLP_HEADS, GMLP_BLOCK, 1), F32),
            jax.ShapeDtypeStruct((1, hd), F32), jax.ShapeDtypeStruct((1, hd), F32),
            jax.ShapeDtypeStruct((4, gd, gd), F32),
            jax.ShapeDtypeStruct((1, 1024), F32), jax.ShapeDtypeStruct((1, 1024), F32),
        ],
        [proj, proj, proj, dm, dm, ws, bs_col, ng, nb, pw, pb, ps], side=side)


def _rope_pair_swap(t):
    lane = lax.broadcasted_iota(jnp.int32, t.shape, 1)
    return jnp.where(lane % 64 < 32, pltpu.roll(t, 96, 1), pltpu.roll(t, 32, 1))


def _rms(x, g):
    r = lax.rsqrt(jnp.mean(x * x, axis=-1, keepdims=True) + LN_EPS)
    return x * r, r


def _rms_bwd(dy, g, xhat, r):
    dyg = dy * g
    return r * (dyg - xhat * jnp.mean(dyg * xhat, axis=-1, keepdims=True))


def _lane_lt(shape, n):
    return lax.broadcasted_iota(jnp.int32, shape, 1) < n


def _mla_prep(proj, cosp, sinp, gq, gkv, name):
    s = proj.shape[0]
    tm = min(s, 512)

    def body(qc_ref, kv_ref, c_ref, s_ref, gq_ref, gkv_ref, qn_ref, kp_ref):
        qhat, _ = _rms(qc_ref[...].astype(F32), None)
        qn_ref[...] = (qhat * gq_ref[...]).astype(BF16)
        khat, _ = _rms(kv_ref[:, 0:128].astype(F32), None)
        kp_ref[:, 0:128] = (khat * gkv_ref[...]).astype(BF16)
        kr = kv_ref[:, 128:256].astype(F32)
        kp_ref[:, 128:256] = (kr * c_ref[...] + _rope_pair_swap(kr) * s_ref[...]).astype(BF16)

    return pl.pallas_call(
        body, name=name, grid=(s // tm,),
        in_specs=[pl.BlockSpec((tm, 256), lambda i: (i, 0)), pl.BlockSpec((tm, 256), lambda i: (i, 1)),
                  pl.BlockSpec((tm, 128), lambda i: (i, 0)), pl.BlockSpec((tm, 128), lambda i: (i, 0)),
                  pl.BlockSpec((1, 256), lambda i: (0, 0)), pl.BlockSpec((1, 128), lambda i: (0, 0))],
        out_specs=[pl.BlockSpec((tm, 256), lambda i: (i, 0)), pl.BlockSpec((tm, 256), lambda i: (i, 0))],
        out_shape=[jax.ShapeDtypeStruct((s, 256), BF16), jax.ShapeDtypeStruct((s, 256), BF16)],
        compiler_params=_params(1),
    )(proj, proj, cosp, sinp, gq, gkv)


def _mla_prep_bwd(proj, dqn, dkp, dv, cosp, sinp, gq, gkv, dproj, name):
    s = proj.shape[0]
    tm = min(s, 512)

    def body(qc_ref, kv_ref, dqn_ref, dkp_ref, dv_ref, c_ref, s_ref, gq_ref, gkv_ref, dproj_ref, o_ref, red_ref):
        @pl.when(pl.program_id(0) == 0)
        def _():
            red_ref[...] = jnp.zeros_like(red_ref)

        qhat, qr = _rms(qc_ref[...].astype(F32), None)
        dq = dqn_ref[...]
        o_ref[:, 0:256] = _rms_bwd(dq, gq_ref[...], qhat, qr).astype(BF16)
        red_ref[0:1, :] += _colsum(dq * qhat)
        khat, kr = _rms(kv_ref[:, 0:128].astype(F32), None)
        dk = dkp_ref[:, 0:128] + dv_ref[...]
        o_ref[:, 256:384] = _rms_bwd(dk, gkv_ref[...], khat, kr).astype(BF16)
        red_ref[1:2, 0:128] += _colsum(dk * khat)
        dr = dkp_ref[:, 128:256]
        o_ref[:, 384:512] = (dr * c_ref[...] - _rope_pair_swap(dr) * s_ref[...]).astype(BF16)

    return pl.pallas_call(
        body, name=name, grid=(s // tm,),
        in_specs=[pl.BlockSpec((tm, 256), lambda i: (i, 0)), pl.BlockSpec((tm, 256), lambda i: (i, 1)),
                  pl.BlockSpec((tm, 256), lambda i: (i, 0)), pl.BlockSpec((tm, 256), lambda i: (i, 0)),
                  pl.BlockSpec((tm, 128), lambda i: (i, 0)),
                  pl.BlockSpec((tm, 128), lambda i: (i, 0)), pl.BlockSpec((tm, 128), lambda i: (i, 0)),
                  pl.BlockSpec((1, 256), lambda i: (0, 0)), pl.BlockSpec((1, 128), lambda i: (0, 0)),
                  pl.BlockSpec(memory_space=pl.ANY)],
        out_specs=[pl.BlockSpec((tm, 512), lambda i: (i, 0)), pl.BlockSpec((8, 256), lambda i: (0, 0))],
        out_shape=[jax.ShapeDtypeStruct(dproj.shape, BF16), jax.ShapeDtypeStruct((8, 256), F32)],
        input_output_aliases={9: 0}, compiler_params=_params(1),
    )(proj, proj, dqn, dkp, dv, cosp, sinp, gq, gkv, dproj)


HEADS_TM = 256
Z_COL0 = ODD_IN_PAD - MLA_WIDTH


def _head_cols(h):
    return slice(128 * h, 128 * h + 128)


def _q_heads(q_up, cosp, sinp, wuk, name):
    s = q_up.shape[0]
    tm = min(s, HEADS_TM)

    def body(q_ref, c_ref, s_ref, w_ref, o_ref):
        for p in range(MLA_HEADS // 2):
            raw = q_ref[:, MLA_WIDTH + 128 * p:MLA_WIDTH + 128 * (p + 1)].astype(F32)
            rot = raw * c_ref[...] + _rope_pair_swap(raw) * s_ref[...]
            low = _lane_lt(rot.shape, 64)
            o_ref[2 * p, :, 128:256] = jnp.where(low, rot, 0.0).astype(BF16)
            o_ref[2 * p + 1, :, 128:256] = jnp.where(low, pltpu.roll(rot, 64, 1), 0.0).astype(BF16)
        for h in range(MLA_HEADS):
            o_ref[h, :, 0:128] = _dot(q_ref[:, _head_cols(h)], w_ref[:, _head_cols(h)], NT).astype(BF16)

    return pl.pallas_call(
        body, name=name, grid=(s // tm,),
        in_specs=[pl.BlockSpec((tm, 3072), lambda i: (i, 0)),
                  pl.BlockSpec((tm, 128), lambda i: (i, 0)), pl.BlockSpec((tm, 128), lambda i: (i, 0)),
                  pl.BlockSpec((128, MLA_WIDTH), lambda i: (0, 0))],
        out_specs=pl.BlockSpec((MLA_HEADS, tm, 256), lambda i: (0, i, 0)),
        out_shape=jax.ShapeDtypeStruct((MLA_HEADS, s, 256), BF16), compiler_params=_params(1),
    )(q_up, cosp, sinp, wuk)


def _q_heads_bwd(dqp, q_up, cosp, sinp, wuk, name):
    s = q_up.shape[0]
    tm = min(s, HEADS_TM)

    def body(dq_ref, qn_ref, c_ref, s_ref, w_ref, dn_ref, dr_ref, dw_ref):
        @pl.when(pl.program_id(0) == 0)
        def _():
            dw_ref[...] = jnp.zeros_like(dw_ref)

        for h in range(MLA_HEADS):
            dlat = dq_ref[h, :, 0:128]
            dn_ref[:, _head_cols(h)] = _dot(dlat, w_ref[:, _head_cols(h)], NN).astype(BF16)
            dw_ref[:, _head_cols(h)] += _dot(dlat, qn_ref[:, _head_cols(h)], TN)
        for p in range(MLA_HEADS // 2):
            drot = dq_ref[2 * p, :, 128:256].astype(F32) + pltpu.roll(dq_ref[2 * p + 1, :, 128:256].astype(F32), 64, 1)
            dr_ref[:, _head_cols(p)] = (drot * c_ref[...] - _rope_pair_swap(drot) * s_ref[...]).astype(BF16)

    return pl.pallas_call(
        body, name=name, grid=(s // tm,),
        in_specs=[pl.BlockSpec((MLA_HEADS, tm, 256), lambda i: (0, i, 0)),
                  pl.BlockSpec((tm, MLA_WIDTH), lambda i: (i, 0)),
                  pl.BlockSpec((tm, 128), lambda i: (i, 0)), pl.BlockSpec((tm, 128), lambda i: (i, 0)),
                  pl.BlockSpec((128, MLA_WIDTH), lambda i: (0, 0))],
        out_specs=[pl.BlockSpec((tm, MLA_WIDTH), lambda i: (i, 0)),
                   pl.BlockSpec((tm, 1024), lambda i: (i, 0)),
                   pl.BlockSpec((128, MLA_WIDTH), lambda i: (0, 0))],
        out_shape=[jax.ShapeDtypeStruct((s, MLA_WIDTH), BF16), jax.ShapeDtypeStruct((s, 1024), BF16),
                   jax.ShapeDtypeStruct((128, MLA_WIDTH), F32)],
        compiler_params=_params(1),
    )(dqp, q_up, cosp, sinp, wuk)


def _o_gate(o_lat, proj, wuv, name):
    s = o_lat.shape[1]
    tm = min(s, HEADS_TM)

    def body(ol_ref, p_ref, w_ref, g_ref):
        for h in range(MLA_HEADS):
            z = p_ref[:, Z_COL0 + 128 * h:Z_COL0 + 128 * (h + 1)].astype(F32)
            g_ref[:, _head_cols(h)] = (_dot(ol_ref[h], w_ref[:, _head_cols(h)], NN) * (z * _sigmoid(z))).astype(BF16)

    return pl.pallas_call(
        body, name=name, grid=(s // tm,),
        in_specs=[pl.BlockSpec((MLA_HEADS, tm, 128), lambda i: (0, i, 0)),
                  pl.BlockSpec((tm, ODD_IN_PAD), lambda i: (i, 0)),
                  pl.BlockSpec((128, MLA_WIDTH), lambda i: (0, 0))],
        out_specs=pl.BlockSpec((tm, MLA_WIDTH), lambda i: (i, 0)),
        out_shape=jax.ShapeDtypeStruct((s, MLA_WIDTH), BF16), compiler_params=_params(1),
    )(o_lat, proj, wuv)


def _o_gate_bwd(dg, o_lat, proj, wuv, name):
    s = o_lat.shape[1]
    tm = min(s, HEADS_TM)

    def body(dg_ref, ol_ref, p_ref, w_ref, dp_ref, dol_ref, dw_ref):
        @pl.when(pl.program_id(0) == 0)
        def _():
            dw_ref[...] = jnp.zeros_like(dw_ref)

        dp_ref[:, 0:Z_COL0] = jnp.zeros((tm, Z_COL0), BF16)
        for h in range(MLA_HEADS):
            zc = slice(Z_COL0 + 128 * h, Z_COL0 + 128 * (h + 1))
            z, dgv, ol = p_ref[:, zc].astype(F32), dg_ref[:, _head_cols(h)].astype(F32), ol_ref[h]
            sig = _sigmoid(z)
            o = _dot(ol, w_ref[:, _head_cols(h)], NN)
            dp_ref[:, zc] = (dgv * o * (sig * (1.0 + z * (1.0 - sig)))).astype(BF16)
            do = (dgv * (z * sig)).astype(BF16)
            dol_ref[h] = _dot(do, w_ref[:, _head_cols(h)], NT).astype(BF16)
            dw_ref[:, _head_cols(h)] += _dot(ol, do, TN)

    return pl.pallas_call(
        body, name=name, grid=(s // tm,),
        in_specs=[pl.BlockSpec((tm, MLA_WIDTH), lambda i: (i, 0)),
                  pl.BlockSpec((MLA_HEADS, tm, 128), lambda i: (0, i, 0)),
                  pl.BlockSpec((tm, ODD_IN_PAD), lambda i: (i, 0)),
                  pl.BlockSpec((128, MLA_WIDTH), lambda i: (0, 0))],
        out_specs=[pl.BlockSpec((tm, ODD_IN_PAD), lambda i: (i, 0)),
                   pl.BlockSpec((MLA_HEADS, tm, 128), lambda i: (0, i, 0)),
                   pl.BlockSpec((128, MLA_WIDTH), lambda i: (0, 0))],
        out_shape=[jax.ShapeDtypeStruct((s, ODD_IN_PAD), BF16), jax.ShapeDtypeStruct((MLA_HEADS, s, 128), BF16),
                   jax.ShapeDtypeStruct((128, MLA_WIDTH), F32)],
        compiler_params=_params(1),
    )(dg, o_lat, proj, wuv)


ATT_TQ = CHUNK
ATT_ROWS = ATT_TQ * MLA_HEADS
ATT_TK = 512
ATT_HEAD_GROUP = 4


def _visible(k0, q_chunk, tk):
    kpos = k0 + lax.broadcasted_iota(jnp.int32, (1, tk), 1)
    return kpos // CHUNK <= q_chunk


def _tile_lanes(t, n):
    return jnp.concatenate([t] * (n // 128), axis=1)


def _key_blocks(i, tk, block, pairs=False):
    visible = i * ATT_TQ + ATT_TQ
    n_full = (visible + tk - 1) // tk - 1

    def full(j):
        block(pl.multiple_of(j * tk, tk), tk, False)

    if pairs:
        def two(jj, carry):
            full(2 * jj)
            full(2 * jj + 1)
            return carry

        lax.fori_loop(0, n_full // 2, two, 0)

        @pl.when(n_full % 2 == 1)
        def _():
            full(n_full - 1)
    else:
        def one(j, carry):
            full(j)
            return carry

        lax.fori_loop(0, n_full, one, 0)
    last0 = pl.multiple_of(n_full * tk, tk)
    half = tk // 2
    if half % 128 == 0:
        @pl.when(visible - n_full * tk <= half)
        def _():
            block(last0, half, True)

        @pl.when(visible - n_full * tk > half)
        def _():
            block(last0, tk, True)
    else:
        block(last0, tk, True)


def _attn_fwd(qp, kp, name, side=None):
    s = kp.shape[0]
    tk = min(ATT_TK, s)

    def body(q_ref, k_ref, o_ref, lse_ref, m_sc, acc_sc):
        i = pl.program_id(0)
        m_sc[...] = jnp.full_like(m_sc, -jnp.inf)
        acc_sc[...] = jnp.zeros_like(acc_sc)

        def block(k0, width, masked):
            k = k_ref[pl.ds(k0, width), :]
            v1 = jnp.where(_lane_lt(k.shape, 128), k, jnp.ones_like(k))
            for h0 in range(0, MLA_HEADS, ATT_HEAD_GROUP):
                rows = slice(h0 * ATT_TQ, (h0 + ATT_HEAD_GROUP) * ATT_TQ)
                q = q_ref[h0:h0 + ATT_HEAD_GROUP].reshape(ATT_HEAD_GROUP * ATT_TQ, 256)
                sc = _dot(q, k, NT) * ATTN_SCALE_LOG2
                if masked:
                    sc = jnp.where(_visible(k0, i, width), sc, -jnp.inf)
                m_prev = m_sc[rows]
                m_new = jnp.maximum(m_prev, jnp.max(sc, axis=1, keepdims=True))
                p = jnp.exp2(sc - _tile_lanes(m_new, width))
                acc_sc[rows] = _tile_lanes(jnp.exp2(m_prev - m_new), 256) * acc_sc[rows] + _dot(p, v1, NN)
                m_sc[rows] = m_new

        _key_blocks(i, tk, block, pairs=True)
        acc = acc_sc[...]
        l = acc[:, 128:256]
        o_ref[...] = (acc[:, 0:128] / l).astype(BF16).reshape(MLA_HEADS, ATT_TQ, 128)
        lse_ref[...] = (m_sc[...] + jnp.log2(l)).reshape(MLA_HEADS, ATT_TQ, 128)

    head128 = pl.BlockSpec((MLA_HEADS, ATT_TQ, 128), lambda i: (0, i, 0))
    return _grid_call(
        body, name, (s // ATT_TQ,),
        [pl.BlockSpec((MLA_HEADS, ATT_TQ, 256), lambda i: (0, i, 0)), pl.BlockSpec((s, 256), lambda i: (0, 0))],
        [head128, head128],
        [jax.ShapeDtypeStruct((MLA_HEADS, s, 128), BF16), jax.ShapeDtypeStruct((MLA_HEADS, s, 128), F32)],
        [qp, kp], scratch=[pltpu.VMEM((ATT_ROWS, 128), F32), pltpu.VMEM((ATT_ROWS, 256), F32)], side=side)


def _attn_bwd(qp, kp, o, do, lse, name, side=None):
    s = kp.shape[0]
    tk = min(ATT_TK, s)

    def body(q_ref, k_ref, o_ref, do_ref, lse_ref, dq_ref, dk_ref, dv_ref, dq_sc):
        i = pl.program_id(0)

        @pl.when(i == 0)
        def _():
            dk_ref[...] = jnp.zeros_like(dk_ref)
            dv_ref[...] = jnp.zeros_like(dv_ref)

        q = q_ref[...].reshape(ATT_ROWS, 256)
        dov = do_ref[...].reshape(ATT_ROWS, 128)
        delta = jnp.sum(dov.astype(F32) * o_ref[...].reshape(ATT_ROWS, 128).astype(F32), axis=1, keepdims=True)
        delta_t = _tile_lanes(jnp.broadcast_to(delta, (ATT_ROWS, 128)), tk)
        lse_t = _tile_lanes(lse_ref[...].reshape(ATT_ROWS, 128), tk)
        dq_sc[...] = jnp.zeros_like(dq_sc)

        def block(k0, width, masked):
            k = k_ref[pl.ds(k0, width), :]
            p = jnp.exp2(_dot(q, k, NT) * ATTN_SCALE_LOG2 - lse_t[:, 0:width])
            if masked:
                p = jnp.where(_visible(k0, i, width), p, 0.0)
            dv_ref[pl.ds(k0, width), :] += _dot(p, dov, TN)
            ds = (p * (_dot(dov, k[:, 0:128], NT) - delta_t[:, 0:width]) * ATTN_SCALE).astype(BF16)
            dq_sc[...] += _dot(ds, k, NN)
            dk_ref[pl.ds(k0, width), :] += _dot(ds, q, TN)

        _key_blocks(i, tk, block, pairs=True)
        dq_ref[...] = dq_sc[...].astype(BF16).reshape(MLA_HEADS, ATT_TQ, 256)

    head128 = pl.BlockSpec((MLA_HEADS, ATT_TQ, 128), lambda i: (0, i, 0))
    head256 = pl.BlockSpec((MLA_HEADS, ATT_TQ, 256), lambda i: (0, i, 0))
    return _grid_call(
        body, name, (s // ATT_TQ,),
        [head256, pl.BlockSpec((s, 256), lambda i: (0, 0)), head128, head128, head128],
        [head256, pl.BlockSpec((s, 256), lambda i: (0, 0)), pl.BlockSpec((s, 128), lambda i: (0, 0))],
        [jax.ShapeDtypeStruct((MLA_HEADS, s, 256), BF16),
         jax.ShapeDtypeStruct((s, 256), F32), jax.ShapeDtypeStruct((s, 128), F32)],
        [qp, kp, o, do, lse], scratch=[pltpu.VMEM((ATT_ROWS, 256), F32)], side=side)


def _place():
    x, y, c = lax.axis_index("x"), lax.axis_index("y"), lax.axis_index("c")
    return x, y, c, 4 * x + 2 * y + c


def _flip(x, y, c, r):
    px = 1 - x if r & 4 else x
    py = 1 - y if r & 2 else y
    pc = 1 - c if r & 1 else c
    return (px, py, pc), 4 * px + 2 * py + pc


def _adaln_exchange(c8, ada_w, ada_b_cols, blocks, name):
    d = c8.shape[1]
    w_cols = ada_w.shape[2]
    n_arr = len(blocks)

    def body(c_ref, w_ref, b_ref, *refs):
        x_refs, (call_ref, mod_ref), out_refs = refs[:n_arr], refs[n_arr:n_arr + 2], refs[n_arr + 2:2 * n_arr + 2]
        sbuf, rbuf, s1, r1, s2, r2 = refs[2 * n_arr + 2:2 * n_arr + 8]
        gather = _gather_begin(x_refs, out_refs, *refs[2 * n_arr + 8:])
        x, y, c, me = _place()
        call_ref[pl.ds(pl.multiple_of(me * 8, 8), 8), :] = c_ref[...]
        peers = [_flip(x, y, c, r) for r in range(1, N_DEV)]

        def c_copy(k, src_lin, to):
            rows = call_ref.at[pl.ds(pl.multiple_of(src_lin * 8, 8), 8), :]
            return pltpu.make_async_remote_copy(src_ref=rows, dst_ref=rows, send_sem=s1.at[k], recv_sem=r1.at[k],
                                                device_id=to, device_id_type=MESH)

        first = [c_copy(k, me, peer) for k, (peer, _) in enumerate(peers)]
        for cp in first:
            cp.start()
        for k, (_, lin) in enumerate(peers):
            c_copy(k, lin, (x, y, c)).wait_recv()
        for cp in first:
            cp.wait_send()

        for j in range(N_DEV):
            cj = call_ref[8 * j:8 * j + 8, :]
            cond = cj * _sigmoid(cj)
            for l in range(2):
                sbuf[j, l] = lax.dot_general(cond, w_ref[l], NN, precision=lax.Precision.HIGHEST,
                                             preferred_element_type=F32) + b_ref[l]

        def m_copy(k, src_slot, dst_slot, to):
            return pltpu.make_async_remote_copy(src_ref=sbuf.at[src_slot], dst_ref=rbuf.at[dst_slot],
                                                send_sem=s2.at[k], recv_sem=r2.at[k], device_id=to,
                                                device_id_type=MESH)

        rbuf[me] = sbuf[me]
        second = [m_copy(k, lin, me, peer) for k, (peer, lin) in enumerate(peers)]
        for cp in second:
            cp.start()
        for k, (_, lin) in enumerate(peers):
            m_copy(k, lin, lin, (x, y, c)).wait_recv()
        for cp in second:
            cp.wait_send()
        for j in range(N_DEV):
            for l in range(2):
                mod_ref[l, :, w_cols * j:w_cols * (j + 1)] = rbuf[j, l]
        _gather_finish(gather)

    vmem = pl.BlockSpec(memory_space=pltpu.VMEM)
    anyspace = pl.BlockSpec(memory_space=pl.ANY)
    g_shapes, g_sems = _gather_extras(blocks)
    return pl.pallas_call(
        body, name=name, in_specs=[vmem, vmem, vmem] + [anyspace] * n_arr, out_specs=[vmem, vmem] + [anyspace] * n_arr,
        out_shape=[jax.ShapeDtypeStruct((8 * N_DEV, d), F32), jax.ShapeDtypeStruct((2, 8, 3 * d), F32)] + g_shapes,
        scratch_shapes=[pltpu.VMEM((N_DEV, 2, 8, w_cols), F32), pltpu.VMEM((N_DEV, 2, 8, w_cols), F32),
                        pltpu.SemaphoreType.DMA((N_DEV - 1,)), pltpu.SemaphoreType.DMA((N_DEV - 1,)),
                        pltpu.SemaphoreType.DMA((N_DEV - 1,)), pltpu.SemaphoreType.DMA((N_DEV - 1,))] + g_sems,
        compiler_params=pltpu.CompilerParams(vmem_limit_bytes=VMEM_LIMIT),
    )(c8, ada_w, ada_b_cols, *blocks)


def _gather_begin(x_refs, out_refs, send_sems, recv_sems, local_sems):
    x, y, c, _ = _place()
    me, sibling = (x, y, c), (x, y, 1 - c)
    chips = [(1 - x, y), (x, 1 - y), (1 - x, 1 - y)]
    n_arr = len(x_refs)

    def copy(t, k, blk, to, src=None):
        slot = out_refs[t].at[4 * blk[0] + 2 * blk[1] + blk[2]]
        return pltpu.make_async_remote_copy(src_ref=slot if src is None else src, dst_ref=slot,
                                            send_sem=send_sems.at[7 * t + k], recv_sem=recv_sems.at[7 * t + k],
                                            device_id=to, device_id_type=MESH)

    mine = [pltpu.make_async_copy(x_refs[t], out_refs[t].at[4 * x + 2 * y + c], local_sems.at[t]) for t in range(n_arr)]
    first = []
    for t in range(n_arr):
        first.append(copy(t, 0, me, sibling, src=x_refs[t]))
        first += [copy(t, 1 + j, me, (*chip, c), src=x_refs[t]) for j, chip in enumerate(chips)]
    for cp in mine + first:
        cp.start()
    return copy, mine, first, (me, sibling, chips, c, n_arr)


def _gather_finish(state):
    copy, mine, first, (me, sibling, chips, c, n_arr) = state
    passed = []
    for t in range(n_arr):
        for j, chip in enumerate(chips):
            copy(t, 1 + j, (*chip, c), me).wait_recv()
            passed.append(copy(t, 4 + j, (*chip, c), sibling))
            passed[-1].start()
    for t in range(n_arr):
        copy(t, 0, sibling, me).wait_recv()
        for j, chip in enumerate(chips):
            copy(t, 4 + j, (*chip, 1 - c), me).wait_recv()
    for cp in first + passed:
        cp.wait_send()
    for cp in mine:
        cp.wait()


def _gather_extras(blocks):
    n_arr = len(blocks)
    return ([jax.ShapeDtypeStruct((N_DEV,) + b.shape, b.dtype) for b in blocks],
            [pltpu.SemaphoreType.DMA((7 * n_arr,)), pltpu.SemaphoreType.DMA((7 * n_arr,)),
             pltpu.SemaphoreType.DMA((n_arr,))])


def _all_gather(blocks, name):
    n_arr = len(blocks)

    def body(*refs):
        _gather_finish(_gather_begin(refs[:n_arr], refs[n_arr:2 * n_arr], *refs[2 * n_arr:]))

    anyspace = pl.BlockSpec(memory_space=pl.ANY)
    shapes, sems = _gather_extras(blocks)
    return pl.pallas_call(body, name=name, in_specs=[anyspace] * n_arr, out_specs=[anyspace] * n_arr,
                          out_shape=shapes, scratch_shapes=sems)(*blocks)


def _scatter_parts(parts, name):
    n_arr = len(parts)

    def body(*refs):
        copies = _exchange_copies(refs[:n_arr], refs[n_arr:2 * n_arr], *refs[2 * n_arr:], False)
        _exchange_start(copies)
        _exchange_wait(copies)

    anyspace = pl.BlockSpec(memory_space=pl.ANY)
    shapes, sems = _exchange_extras(parts, False)
    return pl.pallas_call(body, name=name, in_specs=[anyspace] * n_arr, out_specs=[anyspace] * n_arr,
                          out_shape=shapes, scratch_shapes=sems)(*parts)


def _adamw(w, g, m, v):
    m = ADAM_B1 * m + (1.0 - ADAM_B1) * g
    v = ADAM_B2 * v + (1.0 - ADAM_B2) * (g * g)
    m_hat = m / (1.0 - ADAM_B1 ** ADAM_STEP)
    v_hat = v / (1.0 - ADAM_B2 ** ADAM_STEP)
    return -ADAM_LR * (m_hat / (jnp.sqrt(v_hat) + ADAM_EPS) + ADAM_WD * w), m, v


def _sum_parts_adamw(parts, w, m, v, name):
    _, rows, cols = parts.shape
    tr = max(t for t in range(16, 129, 16) if rows % t == 0)

    def body(p_ref, w_ref, m_ref, v_ref, g_ref, d_ref, mo_ref, vo_ref):
        g = p_ref[0].astype(F32)
        for j in range(1, N_DEV):
            g = g + p_ref[j].astype(F32)
        g_ref[...] = g
        d_ref[...], mo_ref[...], vo_ref[...] = _adamw(w_ref[...], g, m_ref[...], v_ref[...])

    row = _rows3(tr, cols)
    out = jax.ShapeDtypeStruct((1, rows, cols), F32)
    return pl.pallas_call(
        body, name=name, grid=(rows // tr,),
        in_specs=[pl.BlockSpec((N_DEV, tr, cols), lambda i: (0, i, 0)), row, row, row],
        out_specs=[row, row, row, row], out_shape=[out, out, out, out], compiler_params=_params(1),
    )(parts, w, m, v)


def _sum_parts_adamw_whole(parts, w, m, v, name):
    def body(p_ref, w_ref, m_ref, v_ref, g_ref, d_ref, mo_ref, vo_ref):
        g = p_ref[0:1]
        for j in range(1, N_DEV):
            g = g + p_ref[j:j + 1]
        g_ref[...] = g
        d_ref[...], mo_ref[...], vo_ref[...] = _adamw(w_ref[...], g, m_ref[...], v_ref[...])

    out = jax.ShapeDtypeStruct(w.shape, F32)
    return pl.pallas_call(body, name=name, out_shape=[out] * 4,
                          compiler_params=pltpu.CompilerParams(vmem_limit_bytes=VMEM_LIMIT))(parts, w, m, v)


def _sum_parts(parts, name):
    def body(p_ref, g_ref):
        g = p_ref[0]
        for j in range(1, N_DEV):
            g = g + p_ref[j]
        g_ref[...] = g

    return pl.pallas_call(body, name=name, out_shape=jax.ShapeDtypeStruct(parts.shape[1:], F32),
                          compiler_params=pltpu.CompilerParams(vmem_limit_bytes=VMEM_LIMIT))(parts)


def _adamw_many(gs, ws, ms, vs, name):
    n = len(gs)

    def body(*refs):
        for k in range(n):
            g_ref, w_ref, m_ref, v_ref = (refs[q * n + k] for q in range(4))
            d_ref, mo_ref, vo_ref = (refs[(4 + q) * n + k] for q in range(3))
            d_ref[...], mo_ref[...], vo_ref[...] = _adamw(w_ref[...], g_ref[...], m_ref[...], v_ref[...])

    out = [jax.ShapeDtypeStruct(w.shape, F32) for w in ws]
    res = pl.pallas_call(body, name=name, out_shape=out * 3,
                         compiler_params=pltpu.CompilerParams(vmem_limit_bytes=VMEM_LIMIT))(*gs, *ws, *ms, *vs)
    return res[:n], res[n:2 * n], res[2 * n:]


def _ada_w_grad_adamw(c_all, dmod_rows, w, m, v, name):
    def body(c_ref, dm_ref, w_ref, m_ref, v_ref, g_ref, d_ref, mo_ref, vo_ref):
        cv = c_ref[...]
        cond = cv * _sigmoid(cv)
        for l in range(2):
            g = lax.dot_general(cond, dm_ref[l], TN, precision=lax.Precision.HIGHEST, preferred_element_type=F32)
            g_ref[l] = g
            d_ref[l], mo_ref[l], vo_ref[l] = _adamw(w_ref[l], g, m_ref[l], v_ref[l])

    out = jax.ShapeDtypeStruct(w.shape, F32)
    return pl.pallas_call(
        body, name=name, out_shape=[out] * 4, compiler_params=pltpu.CompilerParams(vmem_limit_bytes=VMEM_LIMIT),
    )(c_all, dmod_rows, w, m, v)


REPLICATED = ("ln_g", "ln_b", "gmlp_norm_g", "gmlp_norm_b", "gmlp_ws", "gmlp_bs", "pool_b", "pool_scale",
              "mla_kv_norm_g", "mla_w_uk", "mla_w_uv")
CHUNK_ROWS, ADA_ROW, QNORM_ROW, LOSS_ROW, REP_ROWS = 73, 73, 74, 75, 80
UQ_ROWS, POOLW_ROWS = 96, 32


def _pad_rows(flat2d, rows):
    n, k = flat2d.shape
    return jnp.pad(flat2d, ((0, 0), (0, rows * LANES - k))).reshape(n, rows, LANES)


def _ada_cols_rows(vec):
    return _pad_rows(vec.reshape(2, N_DEV, -1).transpose(1, 0, 2).reshape(N_DEV, -1), 1)


def _unpack_replicated(rep, shapes):
    chunk = sum(s[1] for s in shapes) // N_DEV
    flat, off, out = rep[:, :CHUNK_ROWS].reshape(N_DEV, -1)[:, :chunk].reshape(-1), 0, {}
    for n, size, shape in shapes:
        out[n] = flat[off:off + size].reshape(shape)
        off += size
    cols = 3 * D_MODEL // N_DEV
    out["ada_b"] = rep[:, ADA_ROW, :2 * cols].reshape(N_DEV, 2, cols).transpose(1, 0, 2).reshape(2, -1)
    return out


def kernel(x, c, positions, ada_w, ada_b, ln_g, ln_b, e_w_in, gmlp_norm_g, gmlp_norm_b, gmlp_ws, gmlp_bs, pool_w, pool_b, pool_scale, e_w_out, o_w_in, mla_q_norm_g, mla_kv_norm_g, mla_w_uq, mla_w_uk, mla_w_uv, o_w_out, loss_target, m_ada_w, m_ada_b, m_ln_g, m_ln_b, m_e_w_in, m_gmlp_norm_g, m_gmlp_norm_b, m_gmlp_ws, m_gmlp_bs, m_pool_w, m_pool_b, m_pool_scale, m_e_w_out, m_o_w_in, m_mla_q_norm_g, m_mla_kv_norm_g, m_mla_w_uq, m_mla_w_uk, m_mla_w_uv, m_o_w_out, v_ada_w, v_ada_b, v_ln_g, v_ln_b, v_e_w_in, v_gmlp_norm_g, v_gmlp_norm_b, v_gmlp_ws, v_gmlp_bs, v_pool_w, v_pool_b, v_pool_scale, v_e_w_out, v_o_w_in, v_mla_q_norm_g, v_mla_kv_norm_g, v_mla_w_uq, v_mla_w_uk, v_mla_w_uv, v_o_w_out):
    w_in = dict(ada_w=ada_w, ada_b=ada_b, ln_g=ln_g, ln_b=ln_b, e_w_in=e_w_in, gmlp_norm_g=gmlp_norm_g,
                gmlp_norm_b=gmlp_norm_b, gmlp_ws=gmlp_ws, gmlp_bs=gmlp_bs, pool_w=pool_w, pool_b=pool_b,
                pool_scale=pool_scale, e_w_out=e_w_out, o_w_in=o_w_in, mla_q_norm_g=mla_q_norm_g,
                mla_kv_norm_g=mla_kv_norm_g, mla_w_uq=mla_w_uq, mla_w_uk=mla_w_uk, mla_w_uv=mla_w_uv, o_w_out=o_w_out)
    m_in = dict(ada_w=m_ada_w, ada_b=m_ada_b, ln_g=m_ln_g, ln_b=m_ln_b, e_w_in=m_e_w_in, gmlp_norm_g=m_gmlp_norm_g,
                gmlp_norm_b=m_gmlp_norm_b, gmlp_ws=m_gmlp_ws, gmlp_bs=m_gmlp_bs, pool_w=m_pool_w, pool_b=m_pool_b,
                pool_scale=m_pool_scale, e_w_out=m_e_w_out, o_w_in=m_o_w_in, mla_q_norm_g=m_mla_q_norm_g,
                mla_kv_norm_g=m_mla_kv_norm_g, mla_w_uq=m_mla_w_uq, mla_w_uk=m_mla_w_uk, mla_w_uv=m_mla_w_uv,
                o_w_out=m_o_w_out)
    v_in = dict(ada_w=v_ada_w, ada_b=v_ada_b, ln_g=v_ln_g, ln_b=v_ln_b, e_w_in=v_e_w_in, gmlp_norm_g=v_gmlp_norm_g,
                gmlp_norm_b=v_gmlp_norm_b, gmlp_ws=v_gmlp_ws, gmlp_bs=v_gmlp_bs, pool_w=v_pool_w, pool_b=v_pool_b,
                pool_scale=v_pool_scale, e_w_out=v_e_w_out, o_w_in=v_o_w_in, mla_q_norm_g=v_mla_q_norm_g,
                mla_kv_norm_g=v_mla_kv_norm_g, mla_w_uq=v_mla_w_uq, mla_w_uk=v_mla_w_uk, mla_w_uv=v_mla_w_uv,
                o_w_out=v_o_w_out)
    names = list(w_in)
    seq = x.shape[1]
    d = D_MODEL
    me = 4 * lax.axis_index("x") + 2 * lax.axis_index("y") + lax.axis_index("c")
    ada_cols = ada_w.shape[2]

    ada_b_cols = lax.dynamic_slice_in_dim(ada_b, me * ada_cols, ada_cols, axis=1)
    slab_row = lax.broadcasted_iota(jnp.int32, (8, d), 0)
    slab = jnp.where(slab_row == 0, c, jnp.where(slab_row == 1, jnp.pad(mla_q_norm_g, ((0, 0), (0, d - 32))), 0.0))
    c_all, mod, w_in_e3, pool_w3 = _adaln_exchange(
        slab, ada_w, jnp.broadcast_to(ada_b_cols[:, None, :], (2, 8, ada_cols)),
        [e_w_in[0].astype(BF16), pool_w.astype(BF16).reshape(POOLW_ROWS, LANES)], "adaln_exchange")
    h0 = _modulate(x, mod[0], "modulate0")
    proj0, o_in3 = _matmul_cols_nn(h0, w_in_e3, BF16, 512, "even_in", side=([o_w_in[0].astype(BF16)], True))
    o_in_full = o_in3.transpose(1, 0, 2).reshape(d, ODD_IN)
    w_in_o = jnp.concatenate([o_in_full[:, :448], jnp.zeros((d, 64), BF16), o_in_full[:, 448:]], axis=1)
    pool_w_full = pool_w3.reshape(N_DEV, 4, 32, 256).transpose(1, 0, 2, 3).reshape(4, 256, 256)
    g_q = c_all.reshape(N_DEV, 8, d)[:, 1, :32].reshape(1, MLA_Q_RANK)

    ws, bs_col = gmlp_ws[0], gmlp_bs[0].reshape(GMLP_HEADS, GMLP_BLOCK, 1)
    wuk2, wuv2 = mla_w_uk[0].reshape(MLA_KV_RANK, -1), mla_w_uv[0].reshape(MLA_KV_RANK, -1)
    inv = 1.0 / (ROPE_THETA ** (jnp.arange(0, MLA_ROPE, 2, dtype=F32) / MLA_ROPE))
    ang = positions[0].astype(F32)[:, None] * inv
    cosp = jnp.tile(jnp.cos(ang), (1, 4))
    sinp = jnp.tile(jnp.concatenate([-jnp.sin(ang), jnp.sin(ang)], axis=1), (1, 2))

    mix0, w_out_e3 = _even_fwd(proj0, ws, bs_col, gmlp_norm_g, gmlp_norm_b, pool_w_full, pool_b, pool_scale, "even_mix",
                               side=([e_w_out[0].astype(BF16)], True))
    w_out_e = w_out_e3.reshape(-1, d)
    y0, uq3 = _matmul([(mix0, w_out_e)], "nn", F32, seq, d, 512, 1024, "even_out",
                      side=([mla_w_uq.astype(BF16).reshape(UQ_ROWS, LANES)], True))
    uq_full = uq3.reshape(MLA_Q_RANK, MLA_HEADS, MLA_NOPE + MLA_ROPE)
    w_uq_n = uq_full[:, :, :MLA_NOPE].reshape(MLA_Q_RANK, -1)
    w_uq_r = uq_full[:, :, MLA_NOPE:].reshape(MLA_Q_RANK, -1)
    w_uq = jnp.concatenate([w_uq_n, w_uq_r], axis=1)
    x1, h1 = _resid_ln(x, y0, mod[0], ln_g[0:1], ln_b[0:1], mod[1], "resid_ln0")

    (proj1,) = _matmul([(h1, w_in_o)], "nn", BF16, seq, ODD_IN_PAD, 512, ODD_IN_PAD, "odd_in")
    qn, kp = _mla_prep(proj1, cosp, sinp, g_q, mla_kv_norm_g, "mla_prep")
    (q_up,) = _matmul([(qn, w_uq)], "nn", BF16, seq, 3072, 512, 3072, "q_up")
    qp = _q_heads(q_up, cosp, sinp, wuk2, "q_heads")
    o_lat, lse, w_out_o3 = _attn_fwd(qp, kp, "attn_fwd", side=([o_w_out[0].astype(BF16)], True))
    w_out_o = w_out_o3.reshape(-1, d)
    gated = _o_gate(o_lat, proj1, wuv2, "o_gate")
    (y1,) = _matmul([(gated, w_out_o)], "nn", F32, seq, d, 512, 1024, "odd_out")

    dy1, dxres1, red2 = _final_ln_loss_bwd(x1, y1, mod[1], ln_g[1:2], ln_b[1:2], loss_target, "final_ln_loss")
    (dgated,) = _matmul([(dy1, w_out_o)], "nt", BF16, seq, MLA_WIDTH, 512, MLA_WIDTH, "odd_out_dx")
    (g_w_out_o,) = _matmul([(gated, dy1)], "tn", BF16, MLA_WIDTH, d, 256, d, "odd_out_dw")
    dproj1_z, do_lat, g_wuv = _o_gate_bwd(dgated, o_lat, proj1, wuv2, "o_gate_bwd")
    dqp, dkp, dvv, r_o_out = _attn_bwd(qp, kp, o_lat, do_lat, lse, "attn_bwd",
                                       side=([g_w_out_o.reshape(N_DEV, -1, d)], False))
    dq_nope, dq_rope, g_wuk = _q_heads_bwd(dqp, q_up, cosp, sinp, wuk2, "q_heads_bwd")
    (dqn,) = _matmul([(dq_nope, w_uq_n), (dq_rope, w_uq_r)], "nt", F32, seq, MLA_Q_RANK, 512, 256, "q_up_dx")
    (g_wuq_n,) = _matmul([(qn, dq_nope)], "tn", F32, MLA_Q_RANK, MLA_WIDTH, 256, MLA_WIDTH, "q_up_dw_nope")
    (g_wuq_r,) = _matmul([(qn, dq_rope)], "tn", F32, MLA_Q_RANK, 1024, 256, 1024, "q_up_dw_rope")
    dproj1, red_mla = _mla_prep_bwd(proj1, dqn, dkp, dvv, cosp, sinp, g_q, mla_kv_norm_g, dproj1_z, "mla_prep_bwd")
    (dh1,) = _matmul([(dproj1, w_in_o)], "nt", F32, seq, d, 512, d, "odd_in_dx")
    part_uq = jnp.concatenate([g_wuq_n.reshape(MLA_Q_RANK, MLA_HEADS, MLA_NOPE),
                               g_wuq_r.reshape(MLA_Q_RANK, MLA_HEADS, MLA_ROPE)], axis=2).reshape(
                                   (N_DEV,) + mla_w_uq.shape[1:])
    g_w_in_o, r_uq = _matmul([(h1, dproj1)], "tn", BF16, d, ODD_IN_PAD, 256, ODD_IN_PAD // 2, "odd_in_dw", n_outer=True,
                             side=([part_uq], False))
    part_o_in = jnp.concatenate([g_w_in_o[:, :448], g_w_in_o[:, 512:]], axis=1).reshape(d, N_DEV, -1).transpose(1, 0, 2)
    dy0, dxres0, red1 = _mid_bwd(dh1, dxres1, x, y0, mod[0], mod[1], ln_g[0:1], ln_b[0:1], "mid_bwd")
    (dmix,) = _matmul([(dy0, w_out_e)], "nt", BF16, seq, 2048, 512, 2048, "even_out_dx")
    (g_w_out_e,) = _matmul([(mix0, dy0)], "tn", BF16, 2048, d, 256, d, "even_out_dw")
    dproj0, g_ws, g_bs, g_ng, g_nb, g_pw, g_pb, g_ps, r_o_in = _even_bwd(
        proj0, dmix, ws, bs_col, gmlp_norm_g, gmlp_norm_b, pool_w_full, pool_b, pool_scale, "even_mix_bwd",
        side=([part_o_in], False))
    part_pw = g_pw.reshape(4, N_DEV, 32, 256).transpose(1, 0, 2, 3)
    part_e_in, r_e_out, r_pw = _matmul_cols_tn(h0, dproj0, w_in_e3.shape[2], BF16, 512, "even_in_dw",
                                               side=([g_w_out_e.reshape(N_DEV, -1, d), part_pw], False))
    dh0, r_e_in = _matmul_cols_nt(dproj0, w_in_e3, F32, 512, "even_in_dx", side=([part_e_in], False))
    grad_x, red0 = _first_bwd(dh0, dxres0, x, mod[0], "first_bwd")

    t_mask = lax.broadcasted_iota(jnp.int32, (GMLP_BLOCK, GMLP_BLOCK), 0) // CHUNK
    s_mask = lax.broadcasted_iota(jnp.int32, (GMLP_BLOCK, GMLP_BLOCK), 1) // CHUNK
    part = {
        "ln_g": jnp.stack([red1[2], red2[0]]), "ln_b": jnp.stack([red1[3], red2[1]]),
        "gmlp_norm_g": g_ng, "gmlp_norm_b": g_nb,
        "gmlp_ws": jnp.where(s_mask <= t_mask, g_ws, 0.0), "gmlp_bs": g_bs,
        "pool_b": g_pb, "pool_scale": g_ps, "mla_kv_norm_g": red_mla[1, :MLA_KV_RANK],
        "mla_w_uk": g_wuk, "mla_w_uv": g_wuv,
    }
    dmod = jnp.stack([jnp.concatenate([red0[1], red0[0], red1[4]]),
                      jnp.concatenate([red1[1], red1[0], red2[2]])])

    loss_row = jnp.pad(jnp.broadcast_to((0.5 / d * jnp.sum(red2[3])).reshape(1, 1, 1), (N_DEV, 1, 1)),
                       ((0, 0), (0, 0), (0, LANES - 1)))
    part_small = jnp.concatenate([
        _pad_rows(jnp.concatenate([part[n].reshape(-1) for n in REPLICATED]).reshape(N_DEV, -1), CHUNK_ROWS),
        jnp.pad(jnp.concatenate([_ada_cols_rows(dmod), _pad_rows(red_mla[0].reshape(N_DEV, -1), 1), loss_row], axis=1),
                ((0, 0), (0, REP_ROWS - LOSS_ROW - 1), (0, 0)))], axis=1)
    (r_small,) = _scatter_parts([part_small], "grad_scatter")

    res = {"e_w_in": _sum_parts_adamw(r_e_in, e_w_in, m_e_w_in, v_e_w_in, "adamw_e_w_in"),
           "o_w_in": _sum_parts_adamw(r_o_in, o_w_in, m_o_w_in, v_o_w_in, "adamw_o_w_in"),
           "e_w_out": _sum_parts_adamw(r_e_out, e_w_out, m_e_w_out, v_e_w_out, "adamw_e_w_out"),
           "o_w_out": _sum_parts_adamw(r_o_out, o_w_out, m_o_w_out, v_o_w_out, "adamw_o_w_out"),
           "mla_w_uq": _sum_parts_adamw_whole(r_uq, mla_w_uq, m_mla_w_uq, v_mla_w_uq, "adamw_w_uq"),
           "pool_w": _sum_parts_adamw_whole(r_pw, pool_w, m_pool_w, v_pool_w, "adamw_pool_w")}
    small_sum = _sum_parts(r_small, "small_sum")
    loss = small_sum[LOSS_ROW, 0]
    (rep_sum,) = _all_gather([small_sum], "replicated_gather")
    grads = _unpack_replicated(rep_sum, [(n, w_in[n].size, w_in[n].shape) for n in REPLICATED])
    grads["mla_q_norm_g"] = small_sum[QNORM_ROW:QNORM_ROW + 1, :32]
    small_names = list(grads)
    deltas, new_ms, new_vs = _adamw_many([grads[n] for n in small_names], [w_in[n] for n in small_names],
                                         [m_in[n] for n in small_names], [v_in[n] for n in small_names], "small_adamw")
    for k, n in enumerate(small_names):
        res[n] = [grads[n], deltas[k], new_ms[k], new_vs[k]]
    dmod_all = r_small[:, ADA_ROW, :2 * ada_cols].reshape(N_DEV, 2, ada_cols).transpose(1, 0, 2)
    dmod_rows = jnp.pad(dmod_all[:, :, None, :], ((0, 0), (0, 0), (0, 7), (0, 0))).reshape(2, 8 * N_DEV, ada_cols)
    res["ada_w"] = _ada_w_grad_adamw(c_all, dmod_rows, ada_w, m_ada_w, v_ada_w, "ada_w_adamw")

    return (loss, grad_x, *[res[n][0] for n in names], *[res[n][1] for n in names],
            *[res[n][2] for n in names], *[res[n][3] for n in names])
```

```python
import functools

import jax
import jax.numpy as jnp
from jax import lax
from jax.experimental import pallas as pl
from jax.experimental.pallas import tpu as pltpu

F32 = jnp.float32
BF16 = jnp.bfloat16

D_MODEL = 1024
CHUNK = 64
LN_EPS = 1e-5
GMLP_HEADS = 4
GMLP_HEAD_DIM = 256
GMLP_BLOCK = 128
POOL_WINDOWS = (2, 4, 8, 16)
POOL_GROUP_DIM = 256
POOL_HALO = 16
MLA_HEADS = 16
MLA_NOPE = 128
MLA_ROPE = 64
MLA_Q_RANK = 256
MLA_KV_RANK = 128
MLA_WIDTH = 2048
ODD_IN = 2496
ODD_IN_PAD = 2560
ROPE_THETA = 10000.0
ATTN_SCALE = (MLA_NOPE + MLA_ROPE) ** -0.5
ATTN_SCALE_LOG2 = ATTN_SCALE * 1.4426950408889634
DEEPNORM_ALPHA = 4.0 ** 0.25
ADAM_LR, ADAM_B1, ADAM_B2, ADAM_EPS, ADAM_WD, ADAM_STEP = 0.001, 0.9, 0.999, 1e-8, 0.01, 10
N_DEV = 8
LANES = 1024
VMEM_LIMIT = 56 * 1024 * 1024
MESH = pl.DeviceIdType.MESH

NT = (((1,), (1,)), ((), ()))
NN = (((1,), (0,)), ((), ()))
TN = (((0,), (0,)), ((), ()))


def _params(n_axes):
    return pltpu.CompilerParams(dimension_semantics=("arbitrary",) * n_axes, vmem_limit_bytes=VMEM_LIMIT)


def _dot(a, b, dn):
    return lax.dot_general(a.astype(BF16), b.astype(BF16), dn, preferred_element_type=F32)


def _sigmoid(z):
    return 1.0 / (1.0 + jnp.exp(-z))


def _colsum(t):
    return jnp.sum(t, axis=0, keepdims=True)


def _exchange_copies(g_refs, r_refs, send_sems, recv_sems, local_sems, gather):
    x, y, c, me = _place()
    n_arr = len(g_refs)

    def src(t, slot):
        return g_refs[t] if gather else g_refs[t].at[slot]

    own = [pltpu.make_async_copy(src(t, me), r_refs[t].at[me], local_sems.at[t]) for t in range(n_arr)]
    sends, recvs = [], []
    for r in range(1, N_DEV):
        peer, lin = _flip(x, y, c, r)
        for t in range(n_arr):
            k = n_arr * (r - 1) + t
            sends.append(pltpu.make_async_remote_copy(
                src_ref=src(t, lin), dst_ref=r_refs[t].at[me], send_sem=send_sems.at[k], recv_sem=recv_sems.at[k],
                device_id=peer, device_id_type=MESH))
            recvs.append(pltpu.make_async_remote_copy(
                src_ref=src(t, lin), dst_ref=r_refs[t].at[lin], send_sem=send_sems.at[k], recv_sem=recv_sems.at[k],
                device_id=(x, y, c), device_id_type=MESH))
    return own, sends, recvs


def _exchange_start(copies):
    own, sends, _ = copies
    for cp in own + sends:
        cp.start()


def _exchange_wait(copies):
    own, sends, recvs = copies
    for cp in recvs:
        cp.wait_recv()
    for cp in sends:
        cp.wait_send()
    for cp in own:
        cp.wait()


def _exchange_extras(parts, gather):
    shapes = [jax.ShapeDtypeStruct(((N_DEV,) + p.shape) if gather else p.shape, p.dtype) for p in parts]
    n = len(parts) * (N_DEV - 1)
    return shapes, [pltpu.SemaphoreType.DMA((n,)), pltpu.SemaphoreType.DMA((n,)), pltpu.SemaphoreType.DMA((len(parts),))]


def _grid_call(body, name, grid, in_specs, out_specs, out_shape, args, scratch=(), side=None):
    if side is None:
        return pl.pallas_call(body, name=name, grid=grid, in_specs=in_specs, out_specs=out_specs,
                              out_shape=out_shape, scratch_shapes=list(scratch),
                              compiler_params=_params(len(grid)))(*args)
    parts, gather = side
    n_in, n_out, n_sc, n_arr = len(args), len(out_shape), len(scratch), len(parts)
    side_shapes, side_sems = _gather_extras(parts) if gather else _exchange_extras(parts, False)
    mid = tuple(g // 2 for g in grid)

    def wrapped(*refs):
        ins, g_refs = refs[:n_in], refs[n_in:n_in + n_arr]
        outs = refs[n_in + n_arr:n_in + n_arr + n_out]
        r_refs = refs[n_in + n_arr + n_out:n_in + 2 * n_arr + n_out]
        sc = refs[n_in + 2 * n_arr + n_out:n_in + 2 * n_arr + n_out + n_sc]
        ids = [pl.program_id(a) for a in range(len(grid))]

        def at(step):
            return functools.reduce(jnp.logical_and, [i == s for i, s in zip(ids, step)])

        first, last = at((0,) * len(grid)), at(tuple(g - 1 for g in grid))
        if gather:
            exchange = _Gather(g_refs, r_refs, *refs[-3:])
            pl.when(first)(exchange.start)
            if mid != (0,) * len(grid):
                pl.when(at(mid))(exchange.forward)
            body(*ins, *outs, *sc)

            @pl.when(last)
            def _():
                if mid == (0,) * len(grid):
                    exchange.forward()
                exchange.end()
        else:
            copies = _exchange_copies(g_refs, r_refs, *refs[-3:], False)
            pl.when(first)(lambda: _exchange_start(copies))
            body(*ins, *outs, *sc)
            pl.when(last)(lambda: _exchange_wait(copies))

    anyspace = pl.BlockSpec(memory_space=pl.ANY)
    return pl.pallas_call(
        wrapped, name=name, grid=grid, in_specs=list(in_specs) + [anyspace] * n_arr,
        out_specs=list(out_specs) + [anyspace] * n_arr, out_shape=list(out_shape) + side_shapes,
        scratch_shapes=list(scratch) + side_sems, compiler_params=_params(len(grid)),
    )(*args, *parts)


def _matmul(pairs, mode, out_dtype, m, n, tm, tn, name, side=None, n_outer=False):
    dn = {"nn": NN, "nt": NT, "tn": TN}[mode]
    tm, tn = min(tm, m), min(tn, n)
    n_pairs = len(pairs)
    grid = (n // tn, m // tm) if n_outer else (m // tm, n // tn)

    def ij(f):
        return (lambda j, i: f(i, j)) if n_outer else f

    def body(*refs):
        o_ref = refs[-1]
        acc = None
        for p in range(n_pairs):
            t = _dot(refs[2 * p][...], refs[2 * p + 1][...], dn)
            acc = t if acc is None else acc + t
        o_ref[...] = acc.astype(o_ref.dtype)

    in_specs, args = [], []
    for a, b in pairs:
        if mode == "nn":
            k = a.shape[1]
            in_specs += [pl.BlockSpec((tm, k), ij(lambda i, j: (i, 0))), pl.BlockSpec((k, tn), ij(lambda i, j: (0, j)))]
        elif mode == "nt":
            k = a.shape[1]
            in_specs += [pl.BlockSpec((tm, k), ij(lambda i, j: (i, 0))), pl.BlockSpec((tn, k), ij(lambda i, j: (j, 0)))]
        else:
            k = a.shape[0]
            in_specs += [pl.BlockSpec((k, tm), ij(lambda i, j: (0, i))), pl.BlockSpec((k, tn), ij(lambda i, j: (0, j)))]
        args += [a, b]
    return _grid_call(body, name, grid, in_specs, [pl.BlockSpec((tm, tn), ij(lambda i, j: (i, j)))],
                      [jax.ShapeDtypeStruct((m, n), out_dtype)], args, side=side)


def _matmul_cols_nn(a, w3, out_dtype, tm, name, side=None):
    m, k = a.shape
    _, _, n = w3.shape
    tm = min(tm, m)

    def body(a_ref, w_ref, o_ref):
        av = a_ref[...]
        for j in range(N_DEV):
            o_ref[:, n * j:n * (j + 1)] = _dot(av, w_ref[j], NN).astype(o_ref.dtype)

    return _grid_call(
        body, name, (m // tm,),
        [pl.BlockSpec((tm, k), lambda i: (i, 0)), pl.BlockSpec((N_DEV, k, n), lambda i: (0, 0, 0))],
        [pl.BlockSpec((tm, N_DEV * n), lambda i: (i, 0))], [jax.ShapeDtypeStruct((m, N_DEV * n), out_dtype)], [a, w3],
        side=side)


def _matmul_cols_nt(a, w3, out_dtype, tm, name, side=None):
    m = a.shape[0]
    _, k, n = w3.shape
    tm = min(tm, m)

    def body(a_ref, w_ref, o_ref):
        acc = _dot(a_ref[:, 0:n], w_ref[0], NT)
        for j in range(1, N_DEV):
            acc = acc + _dot(a_ref[:, n * j:n * (j + 1)], w_ref[j], NT)
        o_ref[...] = acc.astype(o_ref.dtype)

    return _grid_call(
        body, name, (m // tm,),
        [pl.BlockSpec((tm, N_DEV * n), lambda i: (i, 0)), pl.BlockSpec((N_DEV, k, n), lambda i: (0, 0, 0))],
        [pl.BlockSpec((tm, k), lambda i: (i, 0))], [jax.ShapeDtypeStruct((m, k), out_dtype)], [a, w3], side=side)


def _matmul_cols_tn(a, b, n, out_dtype, tk, name, side=None):
    m, k = a.shape
    tk = min(tk, k)

    def body(a_ref, b_ref, o_ref):
        o_ref[...] = _dot(a_ref[...], b_ref[...], TN).astype(o_ref.dtype)

    return _grid_call(
        body, name, (N_DEV, k // tk),
        [pl.BlockSpec((m, tk), lambda j, i: (0, i)), pl.BlockSpec((m, n), lambda j, i: (0, j))],
        [pl.BlockSpec((None, tk, n), lambda j, i: (j, i, 0))], [jax.ShapeDtypeStruct((N_DEV, k, n), out_dtype)], [a, b],
        side=side)


def _rows3(tm, d):
    return pl.BlockSpec((None, tm, d), lambda i: (0, i, 0))


def _modulate(x, mod, name):
    _, s, d = x.shape
    tm = min(s, 512)

    def body(x_ref, m_ref, h_ref):
        shift, scale = m_ref[0:1, 0:d], m_ref[0:1, d:2 * d]
        h_ref[...] = (x_ref[...] * (1.0 + scale) + shift).astype(BF16)

    return pl.pallas_call(
        body, name=name, grid=(s // tm,),
        in_specs=[_rows3(tm, d), pl.BlockSpec((8, 3 * d), lambda i: (0, 0))],
        out_specs=pl.BlockSpec((tm, d), lambda i: (i, 0)),
        out_shape=jax.ShapeDtypeStruct((s, d), BF16), compiler_params=_params(1),
    )(x, mod)


def _ln_stats(r):
    mu = jnp.mean(r, axis=-1, keepdims=True)
    rc = r - mu
    var = jnp.mean(rc * rc, axis=-1, keepdims=True)
    rstd = lax.rsqrt(var + LN_EPS)
    return rc * rstd, rstd


def _ln_bwd(dxhat, xhat, rstd):
    return rstd * (dxhat - jnp.mean(dxhat, axis=-1, keepdims=True)
                   - xhat * jnp.mean(dxhat * xhat, axis=-1, keepdims=True))


def _resid_ln(x, y, mod, g, b, mod_next, name):
    _, s, d = x.shape
    tm = min(s, 512)

    def body(x_ref, y_ref, m_ref, g_ref, b_ref, mn_ref, o_ref, h_ref):
        gate = m_ref[0:1, 2 * d:3 * d]
        xhat, _ = _ln_stats(DEEPNORM_ALPHA * x_ref[...] + (1.0 + gate) * y_ref[...])
        out = xhat * g_ref[...] + b_ref[...]
        o_ref[...] = out
        h_ref[...] = (out * (1.0 + mn_ref[0:1, d:2 * d]) + mn_ref[0:1, 0:d]).astype(BF16)

    row = pl.BlockSpec((tm, d), lambda i: (i, 0))
    vec = pl.BlockSpec((1, d), lambda i: (0, 0))
    modspec = pl.BlockSpec((8, 3 * d), lambda i: (0, 0))
    return pl.pallas_call(
        body, name=name, grid=(s // tm,),
        in_specs=[_rows3(tm, d), row, modspec, vec, vec, modspec],
        out_specs=[_rows3(tm, d), row],
        out_shape=[jax.ShapeDtypeStruct((1, s, d), F32), jax.ShapeDtypeStruct((s, d), BF16)],
        compiler_params=_params(1),
    )(x, y, mod, g, b, mod_next)


def _final_ln_loss_bwd(x, y, mod, g, b, target, name):
    _, s, d = x.shape
    tm = min(s, 256)

    def body(x_ref, y_ref, m_ref, g_ref, b_ref, t_ref, dy_ref, dx_ref, red_ref):
        @pl.when(pl.program_id(0) == 0)
        def _():
            red_ref[...] = jnp.zeros_like(red_ref)

        gate = m_ref[0:1, 2 * d:3 * d]
        yv = y_ref[...]
        xhat, rstd = _ln_stats(DEEPNORM_ALPHA * x_ref[...] + (1.0 + gate) * yv)
        err = xhat * g_ref[...] + b_ref[...] - t_ref[...]
        dout = err * (1.0 / d)
        dr = _ln_bwd(dout * g_ref[...], xhat, rstd)
        dy_ref[...] = ((1.0 + gate) * dr).astype(BF16)
        dx_ref[...] = DEEPNORM_ALPHA * dr
        red_ref[0:1, :] += _colsum(dout * xhat)
        red_ref[1:2, :] += _colsum(dout)
        red_ref[2:3, :] += _colsum(dr * yv)
        red_ref[3:4, :] += _colsum(err * err)

    row = pl.BlockSpec((tm, d), lambda i: (i, 0))
    vec = pl.BlockSpec((1, d), lambda i: (0, 0))
    return pl.pallas_call(
        body, name=name, grid=(s // tm,),
        in_specs=[_rows3(tm, d), row, pl.BlockSpec((8, 3 * d), lambda i: (0, 0)), vec, vec, _rows3(tm, d)],
        out_specs=[row, row, pl.BlockSpec((8, d), lambda i: (0, 0))],
        out_shape=[jax.ShapeDtypeStruct((s, d), BF16), jax.ShapeDtypeStruct((s, d), F32),
                   jax.ShapeDtypeStruct((8, d), F32)],
        compiler_params=_params(1),
    )(x, y, mod, g, b, target)


def _mid_bwd(dh, dxres, x, y, mod_lo, mod_hi, g, b, name):
    _, s, d = x.shape
    tm = min(s, 256)

    def body(dh_ref, dxr_ref, x_ref, y_ref, ml_ref, mh_ref, g_ref, b_ref, dy_ref, dx_ref, red_ref):
        @pl.when(pl.program_id(0) == 0)
        def _():
            red_ref[...] = jnp.zeros_like(red_ref)

        gate = ml_ref[0:1, 2 * d:3 * d]
        scale_hi = mh_ref[0:1, d:2 * d]
        yv, dhv = y_ref[...], dh_ref[...]
        xhat, rstd = _ln_stats(DEEPNORM_ALPHA * x_ref[...] + (1.0 + gate) * yv)
        x_mid = xhat * g_ref[...] + b_ref[...]
        dx_mid = dxr_ref[...] + dhv * (1.0 + scale_hi)
        dr = _ln_bwd(dx_mid * g_ref[...], xhat, rstd)
        dy_ref[...] = ((1.0 + gate) * dr).astype(BF16)
        dx_ref[...] = DEEPNORM_ALPHA * dr
        red_ref[0:1, :] += _colsum(dhv * x_mid)
        red_ref[1:2, :] += _colsum(dhv)
        red_ref[2:3, :] += _colsum(dx_mid * xhat)
        red_ref[3:4, :] += _colsum(dx_mid)
        red_ref[4:5, :] += _colsum(dr * yv)

    row = pl.BlockSpec((tm, d), lambda i: (i, 0))
    vec = pl.BlockSpec((1, d), lambda i: (0, 0))
    modspec = pl.BlockSpec((8, 3 * d), lambda i: (0, 0))
    return pl.pallas_call(
        body, name=name, grid=(s // tm,),
        in_specs=[row, row, _rows3(tm, d), row, modspec, modspec, vec, vec],
        out_specs=[row, row, pl.BlockSpec((8, d), lambda i: (0, 0))],
        out_shape=[jax.ShapeDtypeStruct((s, d), BF16), jax.ShapeDtypeStruct((s, d), F32),
                   jax.ShapeDtypeStruct((8, d), F32)],
        compiler_params=_params(1),
    )(dh, dxres, x, y, mod_lo, mod_hi, g, b)


def _first_bwd(dh, dxres, x, mod, name):
    _, s, d = x.shape
    tm = min(s, 512)

    def body(dh_ref, dxr_ref, x_ref, m_ref, gx_ref, red_ref):
        @pl.when(pl.program_id(0) == 0)
        def _():
            red_ref[...] = jnp.zeros_like(red_ref)

        scale = m_ref[0:1, d:2 * d]
        dhv = dh_ref[...]
        gx_ref[...] = dxr_ref[...] + dhv * (1.0 + scale)
        red_ref[0:1, :] += _colsum(dhv * x_ref[...])
        red_ref[1:2, :] += _colsum(dhv)

    row = pl.BlockSpec((tm, d), lambda i: (i, 0))
    return pl.pallas_call(
        body, name=name, grid=(s // tm,),
        in_specs=[row, row, _rows3(tm, d), pl.BlockSpec((8, 3 * d), lambda i: (0, 0))],
        out_specs=[_rows3(tm, d), pl.BlockSpec((8, d), lambda i: (0, 0))],
        out_shape=[jax.ShapeDtypeStruct((1, s, d), F32), jax.ShapeDtypeStruct((8, d), F32)],
        compiler_params=_params(1),
    )(dh, dxres, x, mod)


EVEN_TM = 256


def _gmlp_mask():
    t = lax.broadcasted_iota(jnp.int32, (GMLP_BLOCK, GMLP_BLOCK), 0) // CHUNK
    s = lax.broadcasted_iota(jnp.int32, (GMLP_BLOCK, GMLP_BLOCK), 1) // CHUNK
    return s <= t


def _window_sum(ext, win, back):
    n = ext.shape[0]
    k = 1
    while k < win:
        ext = ext + pltpu.roll(ext, k if back else n - k, 0)
        k *= 2
    return ext


def _inv_count(row0, rows, win):
    t = row0 + lax.broadcasted_iota(jnp.int32, (rows, 1), 0)
    return t, 1.0 / jnp.minimum(t + 1, win).astype(F32)


def _pooled(xb, halo, row0, win):
    tm = xb.shape[0]
    sums = _window_sum(jnp.concatenate([halo, xb], axis=0), win, True)[POOL_HALO:]
    _, inv = _inv_count(row0, tm, win)
    return sums * inv - xb


def _even_fwd(proj, ws, bs_col, ng, nb, pw, pb, ps, name, side=None):
    s = proj.shape[0]
    tm = min(s, EVEN_TM)
    hd, gd = GMLP_HEAD_DIM, POOL_GROUP_DIM

    def body(p_ref, halo_ref, ws_ref, bs_ref, ng_ref, nb_ref, pw_ref, pb_ref, ps_ref, m_ref):
        i = pl.program_id(0)
        mask = _gmlp_mask()
        for h in range(GMLP_HEADS):
            wm = jnp.where(mask, ws_ref[h], 0.0).astype(BF16)
            for blk in range(tm // GMLP_BLOCK):
                rows = slice(blk * GMLP_BLOCK, (blk + 1) * GMLP_BLOCK)
                cu, cv, cz = h * hd, 1024 + h * hd, 2048 + h * hd
                vhat, _ = _ln_stats(p_ref[rows, cv:cv + hd].astype(F32))
                vn = vhat * ng_ref[...] + nb_ref[...]
                sv = _dot(wm, vn, NN) + bs_ref[h]
                za = p_ref[rows, cz:cz + hd].astype(F32)
                m_ref[rows, cu:cu + hd] = (p_ref[rows, cu:cu + hd].astype(F32) * sv * (za * _sigmoid(za))).astype(BF16)
        for g, win in enumerate(POOL_WINDOWS):
            cx, cz = 3072 + g * gd, 4096 + g * gd
            halo = jnp.where(i > 0, halo_ref[:, g * gd:(g + 1) * gd].astype(F32), 0.0)
            pooled = _pooled(p_ref[:, cx:cx + gd].astype(F32), halo, i * tm, win)
            yb = _dot(pooled, pw_ref[g], NN) + pb_ref[:, g * gd:(g + 1) * gd]
            zb = p_ref[:, cz:cz + gd].astype(F32)
            m_ref[:, 1024 + g * gd:1024 + (g + 1) * gd] = (
                yb * ps_ref[:, g * gd:(g + 1) * gd] * (zb * _sigmoid(zb))).astype(BF16)

    hb = tm // POOL_HALO
    return _grid_call(
        body, name, (s // tm,),
        [
            pl.BlockSpec((tm, 5120), lambda i: (i, 0)),
            pl.BlockSpec((POOL_HALO, 1024), lambda i: (jnp.maximum(i * hb - 1, 0), 3)),
            pl.BlockSpec((GMLP_HEADS, GMLP_BLOCK, GMLP_BLOCK), lambda i: (0, 0, 0)),
            pl.BlockSpec((GMLP_HEADS, GMLP_BLOCK, 1), lambda i: (0, 0, 0)),
            pl.BlockSpec((1, hd), lambda i: (0, 0)), pl.BlockSpec((1, hd), lambda i: (0, 0)),
            pl.BlockSpec((4, gd, gd), lambda i: (0, 0, 0)),
            pl.BlockSpec((1, 1024), lambda i: (0, 0)), pl.BlockSpec((1, 1024), lambda i: (0, 0)),
        ],
        [pl.BlockSpec((tm, 2048), lambda i: (i, 0))], [jax.ShapeDtypeStruct((s, 2048), BF16)],
        [proj, proj, ws, bs_col, ng, nb, pw, pb, ps], side=side)


def _even_bwd(proj, dm, ws, bs_col, ng, nb, pw, pb, ps, name, side=None):
    s = proj.shape[0]
    tm = min(s, EVEN_TM)
    hd, gd = GMLP_HEAD_DIM, POOL_GROUP_DIM
    n_tiles = s // tm

    def body(p_ref, halo_ref, zbn_ref, dm_ref, dbn_ref, ws_ref, bs_ref, ng_ref, nb_ref, pw_ref, pb_ref, ps_ref,
             dp_ref, dws_ref, dbs_ref, dng_ref, dnb_ref, dpw_ref, dpb_ref, dps_ref):
        i = pl.program_id(0)

        @pl.when(i == 0)
        def _():
            for r in (dws_ref, dbs_ref, dng_ref, dnb_ref, dpw_ref, dpb_ref, dps_ref):
                r[...] = jnp.zeros_like(r)

        mask = _gmlp_mask()
        for h in range(GMLP_HEADS):
            wm = jnp.where(mask, ws_ref[h], 0.0).astype(BF16)
            for blk in range(tm // GMLP_BLOCK):
                rows = slice(blk * GMLP_BLOCK, (blk + 1) * GMLP_BLOCK)
                cu, cv, cz = h * hd, 1024 + h * hd, 2048 + h * hd
                vhat, rstd = _ln_stats(p_ref[rows, cv:cv + hd].astype(F32))
                vn = (vhat * ng_ref[...] + nb_ref[...]).astype(BF16)
                sv = _dot(wm, vn, NN) + bs_ref[h]
                u, za = p_ref[rows, cu:cu + hd].astype(F32), p_ref[rows, cz:cz + hd].astype(F32)
                da = dm_ref[rows, cu:cu + hd].astype(F32)
                sig = _sigmoid(za)
                sa = za * sig
                dau = da * u
                dsv = dau * sa
                dp_ref[rows, cu:cu + hd] = (da * sv * sa).astype(BF16)
                dp_ref[rows, cz:cz + hd] = (dau * sv * (sig * (1.0 + za * (1.0 - sig)))).astype(BF16)
                dsv_b = dsv.astype(BF16)
                dbs_ref[h] += jnp.sum(dsv, axis=1, keepdims=True)
                dws_ref[h] += _dot(dsv_b, vn, NT)
                dvn = _dot(wm, dsv_b, TN)
                dng_ref[...] += _colsum(dvn * vhat)
                dnb_ref[...] += _colsum(dvn)
                dp_ref[rows, cv:cv + hd] = _ln_bwd(dvn * ng_ref[...], vhat, rstd).astype(BF16)

        row0 = i * tm
        for g, win in enumerate(POOL_WINDOWS):
            cx, cz, cd = 3072 + g * gd, 4096 + g * gd, 1024 + g * gd
            gs = slice(g * gd, (g + 1) * gd)
            halo = jnp.where(i > 0, halo_ref[:, gs].astype(F32), 0.0)
            xb = p_ref[:, cx:cx + gd].astype(F32)
            pooled = _pooled(xb, halo, row0, win).astype(BF16)
            scale_g = ps_ref[:, gs]
            yb = _dot(pooled, pw_ref[g], NN) + pb_ref[:, gs]
            zb, db = p_ref[:, cz:cz + gd].astype(F32), dm_ref[:, cd:cd + gd].astype(F32)
            sig = _sigmoid(zb)
            dyp = db * (zb * sig)
            dp_ref[:, cz:cz + gd] = (db * yb * scale_g * (sig * (1.0 + zb * (1.0 - sig)))).astype(BF16)
            dps_ref[:, gs] += _colsum(dyp * yb)
            dpb_ref[:, gs] += _colsum(dyp * scale_g)
            zb_ext = jnp.concatenate([zb, zbn_ref[:, gs].astype(F32)], axis=0)
            db_ext = jnp.concatenate([db, dbn_ref[:, gs].astype(F32)], axis=0)
            dy_ext = (db_ext * (zb_ext * _sigmoid(zb_ext)) * scale_g).astype(BF16)
            dpw_ref[g] += _dot(pooled, dy_ext[:tm], TN)
            dpooled = _dot(dy_ext, pw_ref[g], NT)
            t, inv = _inv_count(row0, tm + POOL_HALO, win)
            w_ext = jnp.where(t < s, dpooled * inv, 0.0)
            dp_ref[:, cx:cx + gd] = (_window_sum(w_ext, win, False)[:tm] - dpooled[:tm]).astype(BF16)

    hb = tm // POOL_HALO
    last = s // POOL_HALO - 1
    small = lambda shape: pl.BlockSpec(shape, lambda i: (0,) * len(shape))
    return _grid_call(
        body, name, (n_tiles,),
        [
            pl.BlockSpec((tm, 5120), lambda i: (i, 0)),
            pl.BlockSpec((POOL_HALO, 1024), lambda i: (jnp.maximum(i * hb - 1, 0), 3)),
            pl.BlockSpec((POOL_HALO, 1024), lambda i: (jnp.minimum((i + 1) * hb, last), 4)),
            pl.BlockSpec((tm, 2048), lambda i: (i, 0)),
            pl.BlockSpec((POOL_HALO, 1024), lambda i: (jnp.minimum((i + 1) * hb, last), 1)),
            small((GMLP_HEADS, GMLP_BLOCK, GMLP_BLOCK)), small((GMLP_HEADS, GMLP_BLOCK, 1)),
            small((1, hd)), small((1, hd)), small((4, gd, gd)), small((1, 1024)), small((1, 1024)),
        ],
        [
            pl.BlockSpec((tm, 5120), lambda i: (i, 0)),
            small((GMLP_HEADS, GMLP_BLOCK, GMLP_BLOCK)), small((GMLP_HEADS, GMLP_BLOCK, 1)),
            small((1, hd)), small((1, hd)), small((4, gd, gd)), small((1, 1024)), small((1, 1024)),
        ],
        [
            jax.ShapeDtypeStruct((s, 5120), BF16),
            jax.ShapeDtypeStruct((GMLP_HEADS, GMLP_BLOCK, GMLP_BLOCK), F32),
            jax.ShapeDtypeStruct((GMLP_HEADS, GMLP_BLOCK, 1), F32),
            jax.ShapeDtypeStruct((1, hd), F32), jax.ShapeDtypeStruct((1, hd), F32),
            jax.ShapeDtypeStruct((4, gd, gd), F32),
            jax.ShapeDtypeStruct((1, 1024), F32), jax.ShapeDtypeStruct((1, 1024), F32),
        ],
        [proj, proj, proj, dm, dm, ws, bs_col, ng, nb, pw, pb, ps], side=side)


def _rope_pair_swap(t):
    lane = lax.broadcasted_iota(jnp.int32, t.shape, 1)
    return jnp.where(lane % 64 < 32, pltpu.roll(t, 96, 1), pltpu.roll(t, 32, 1))


def _rms(x, g):
    r = lax.rsqrt(jnp.mean(x * x, axis=-1, keepdims=True) + LN_EPS)
    return x * r, r


def _rms_bwd(dy, g, xhat, r):
    dyg = dy * g
    return r * (dyg - xhat * jnp.mean(dyg * xhat, axis=-1, keepdims=True))


def _lane_lt(shape, n):
    return lax.broadcasted_iota(jnp.int32, shape, 1) < n


def _mla_prep(proj, cosp, sinp, gq, gkv, name):
    s = proj.shape[0]
    tm = min(s, 512)

    def body(qc_ref, kv_ref, c_ref, s_ref, gq_ref, gkv_ref, qn_ref, kp_ref):
        qhat, _ = _rms(qc_ref[...].astype(F32), None)
        qn_ref[...] = (qhat * gq_ref[...]).astype(BF16)
        khat, _ = _rms(kv_ref[:, 0:128].astype(F32), None)
        kp_ref[:, 0:128] = (khat * gkv_ref[...]).astype(BF16)
        kr = kv_ref[:, 128:256].astype(F32)
        kp_ref[:, 128:256] = (kr * c_ref[...] + _rope_pair_swap(kr) * s_ref[...]).astype(BF16)

    return pl.pallas_call(
        body, name=name, grid=(s // tm,),
        in_specs=[pl.BlockSpec((tm, 256), lambda i: (i, 0)), pl.BlockSpec((tm, 256), lambda i: (i, 1)),
                  pl.BlockSpec((tm, 128), lambda i: (i, 0)), pl.BlockSpec((tm, 128), lambda i: (i, 0)),
                  pl.BlockSpec((1, 256), lambda i: (0, 0)), pl.BlockSpec((1, 128), lambda i: (0, 0))],
        out_specs=[pl.BlockSpec((tm, 256), lambda i: (i, 0)), pl.BlockSpec((tm, 256), lambda i: (i, 0))],
        out_shape=[jax.ShapeDtypeStruct((s, 256), BF16), jax.ShapeDtypeStruct((s, 256), BF16)],
        compiler_params=_params(1),
    )(proj, proj, cosp, sinp, gq, gkv)


def _mla_prep_bwd(proj, dqn, dkp, dv, cosp, sinp, gq, gkv, dproj, name):
    s = proj.shape[0]
    tm = min(s, 512)

    def body(qc_ref, kv_ref, dqn_ref, dkp_ref, dv_ref, c_ref, s_ref, gq_ref, gkv_ref, dproj_ref, o_ref, red_ref):
        @pl.when(pl.program_id(0) == 0)
        def _():
            red_ref[...] = jnp.zeros_like(red_ref)

        qhat, qr = _rms(qc_ref[...].astype(F32), None)
        dq = dqn_ref[...]
        o_ref[:, 0:256] = _rms_bwd(dq, gq_ref[...], qhat, qr).astype(BF16)
        red_ref[0:1, :] += _colsum(dq * qhat)
        khat, kr = _rms(kv_ref[:, 0:128].astype(F32), None)
        dk = dkp_ref[:, 0:128] + dv_ref[...]
        o_ref[:, 256:384] = _rms_bwd(dk, gkv_ref[...], khat, kr).astype(BF16)
        red_ref[1:2, 0:128] += _colsum(dk * khat)
        dr = dkp_ref[:, 128:256]
        o_ref[:, 384:512] = (dr * c_ref[...] - _rope_pair_swap(dr) * s_ref[...]).astype(BF16)

    return pl.pallas_call(
        body, name=name, grid=(s // tm,),
        in_specs=[pl.BlockSpec((tm, 256), lambda i: (i, 0)), pl.BlockSpec((tm, 256), lambda i: (i, 1)),
                  pl.BlockSpec((tm, 256), lambda i: (i, 0)), pl.BlockSpec((tm, 256), lambda i: (i, 0)),
                  pl.BlockSpec((tm, 128), lambda i: (i, 0)),
                  pl.BlockSpec((tm, 128), lambda i: (i, 0)), pl.BlockSpec((tm, 128), lambda i: (i, 0)),
                  pl.BlockSpec((1, 256), lambda i: (0, 0)), pl.BlockSpec((1, 128), lambda i: (0, 0)),
                  pl.BlockSpec(memory_space=pl.ANY)],
        out_specs=[pl.BlockSpec((tm, 512), lambda i: (i, 0)), pl.BlockSpec((8, 256), lambda i: (0, 0))],
        out_shape=[jax.ShapeDtypeStruct(dproj.shape, BF16), jax.ShapeDtypeStruct((8, 256), F32)],
        input_output_aliases={9: 0}, compiler_params=_params(1),
    )(proj, proj, dqn, dkp, dv, cosp, sinp, gq, gkv, dproj)


HEADS_TM = 256
Z_COL0 = ODD_IN_PAD - MLA_WIDTH


def _head_cols(h):
    return slice(128 * h, 128 * h + 128)


def _q_heads(q_up, cosp, sinp, wuk, name):
    s = q_up.shape[0]
    tm = min(s, HEADS_TM)

    def body(q_ref, c_ref, s_ref, w_ref, o_ref):
        for p in range(MLA_HEADS // 2):
            raw = q_ref[:, MLA_WIDTH + 128 * p:MLA_WIDTH + 128 * (p + 1)].astype(F32)
            rot = raw * c_ref[...] + _rope_pair_swap(raw) * s_ref[...]
            low = _lane_lt(rot.shape, 64)
            o_ref[2 * p, :, 128:256] = jnp.where(low, rot, 0.0).astype(BF16)
            o_ref[2 * p + 1, :, 128:256] = jnp.where(low, pltpu.roll(rot, 64, 1), 0.0).astype(BF16)
        for h in range(MLA_HEADS):
            o_ref[h, :, 0:128] = _dot(q_ref[:, _head_cols(h)], w_ref[:, _head_cols(h)], NT).astype(BF16)

    return pl.pallas_call(
        body, name=name, grid=(s // tm,),
        in_specs=[pl.BlockSpec((tm, 3072), lambda i: (i, 0)),
                  pl.BlockSpec((tm, 128), lambda i: (i, 0)), pl.BlockSpec((tm, 128), lambda i: (i, 0)),
                  pl.BlockSpec((128, MLA_WIDTH), lambda i: (0, 0))],
        out_specs=pl.BlockSpec((MLA_HEADS, tm, 256), lambda i: (0, i, 0)),
        out_shape=jax.ShapeDtypeStruct((MLA_HEADS, s, 256), BF16), compiler_params=_params(1),
    )(q_up, cosp, sinp, wuk)


def _q_heads_bwd(dqp, q_up, cosp, sinp, wuk, name):
    s = q_up.shape[0]
    tm = min(s, HEADS_TM)

    def body(dq_ref, qn_ref, c_ref, s_ref, w_ref, dn_ref, dr_ref, dw_ref):
        @pl.when(pl.program_id(0) == 0)
        def _():
            dw_ref[...] = jnp.zeros_like(dw_ref)

        for h in range(MLA_HEADS):
            dlat = dq_ref[h, :, 0:128]
            dn_ref[:, _head_cols(h)] = _dot(dlat, w_ref[:, _head_cols(h)], NN).astype(BF16)
            dw_ref[:, _head_cols(h)] += _dot(dlat, qn_ref[:, _head_cols(h)], TN)
        for p in range(MLA_HEADS // 2):
            drot = dq_ref[2 * p, :, 128:256].astype(F32) + pltpu.roll(dq_ref[2 * p + 1, :, 128:256].astype(F32), 64, 1)
            dr_ref[:, _head_cols(p)] = (drot * c_ref[...] - _rope_pair_swap(drot) * s_ref[...]).astype(BF16)

    return pl.pallas_call(
        body, name=name, grid=(s // tm,),
        in_specs=[pl.BlockSpec((MLA_HEADS, tm, 256), lambda i: (0, i, 0)),
                  pl.BlockSpec((tm, MLA_WIDTH), lambda i: (i, 0)),
                  pl.BlockSpec((tm, 128), lambda i: (i, 0)), pl.BlockSpec((tm, 128), lambda i: (i, 0)),
                  pl.BlockSpec((128, MLA_WIDTH), lambda i: (0, 0))],
        out_specs=[pl.BlockSpec((tm, MLA_WIDTH), lambda i: (i, 0)),
                   pl.BlockSpec((tm, 1024), lambda i: (i, 0)),
                   pl.BlockSpec((128, MLA_WIDTH), lambda i: (0, 0))],
        out_shape=[jax.ShapeDtypeStruct((s, MLA_WIDTH), BF16), jax.ShapeDtypeStruct((s, 1024), BF16),
                   jax.ShapeDtypeStruct((128, MLA_WIDTH), F32)],
        compiler_params=_params(1),
    )(dqp, q_up, cosp, sinp, wuk)


def _o_gate(o_lat, proj, wuv, name):
    s = o_lat.shape[1]
    tm = min(s, HEADS_TM)

    def body(ol_ref, p_ref, w_ref, g_ref):
        for h in range(MLA_HEADS):
            z = p_ref[:, Z_COL0 + 128 * h:Z_COL0 + 128 * (h + 1)].astype(F32)
            g_ref[:, _head_cols(h)] = (_dot(ol_ref[h], w_ref[:, _head_cols(h)], NN) * (z * _sigmoid(z))).astype(BF16)

    return pl.pallas_call(
        body, name=name, grid=(s // tm,),
        in_specs=[pl.BlockSpec((MLA_HEADS, tm, 128), lambda i: (0, i, 0)),
                  pl.BlockSpec((tm, ODD_IN_PAD), lambda i: (i, 0)),
                  pl.BlockSpec((128, MLA_WIDTH), lambda i: (0, 0))],
        out_specs=pl.BlockSpec((tm, MLA_WIDTH), lambda i: (i, 0)),
        out_shape=jax.ShapeDtypeStruct((s, MLA_WIDTH), BF16), compiler_params=_params(1),
    )(o_lat, proj, wuv)


def _o_gate_bwd(dg, o_lat, proj, wuv, name):
    s = o_lat.shape[1]
    tm = min(s, HEADS_TM)

    def body(dg_ref, ol_ref, p_ref, w_ref, dp_ref, dol_ref, dw_ref):
        @pl.when(pl.program_id(0) == 0)
        def _():
            dw_ref[...] = jnp.zeros_like(dw_ref)

        dp_ref[:, 0:Z_COL0] = jnp.zeros((tm, Z_COL0), BF16)
        for h in range(MLA_HEADS):
            zc = slice(Z_COL0 + 128 * h, Z_COL0 + 128 * (h + 1))
            z, dgv, ol = p_ref[:, zc].astype(F32), dg_ref[:, _head_cols(h)].astype(F32), ol_ref[h]
            sig = _sigmoid(z)
            o = _dot(ol, w_ref[:, _head_cols(h)], NN)
            dp_ref[:, zc] = (dgv * o * (sig * (1.0 + z * (1.0 - sig)))).astype(BF16)
            do = (dgv * (z * sig)).astype(BF16)
            dol_ref[h] = _dot(do, w_ref[:, _head_cols(h)], NT).astype(BF16)
            dw_ref[:, _head_cols(h)] += _dot(ol, do, TN)

    return pl.pallas_call(
        body, name=name, grid=(s // tm,),
        in_specs=[pl.BlockSpec((tm, MLA_WIDTH), lambda i: (i, 0)),
                  pl.BlockSpec((MLA_HEADS, tm, 128), lambda i: (0, i, 0)),
                  pl.BlockSpec((tm, ODD_IN_PAD), lambda i: (i, 0)),
                  pl.BlockSpec((128, MLA_WIDTH), lambda i: (0, 0))],
        out_specs=[pl.BlockSpec((tm, ODD_IN_PAD), lambda i: (i, 0)),
                   pl.BlockSpec((MLA_HEADS, tm, 128), lambda i: (0, i, 0)),
                   pl.BlockSpec((128, MLA_WIDTH), lambda i: (0, 0))],
        out_shape=[jax.ShapeDtypeStruct((s, ODD_IN_PAD), BF16), jax.ShapeDtypeStruct((MLA_HEADS, s, 128), BF16),
                   jax.ShapeDtypeStruct((128, MLA_WIDTH), F32)],
        compiler_params=_params(1),
    )(dg, o_lat, proj, wuv)


ATT_TQ = CHUNK
ATT_ROWS = ATT_TQ * MLA_HEADS
ATT_TK = 512
ATT_HEAD_GROUP = 4


def _visible(k0, q_chunk, tk):
    kpos = k0 + lax.broadcasted_iota(jnp.int32, (1, tk), 1)
    return kpos // CHUNK <= q_chunk


def _tile_lanes(t, n):
    return jnp.concatenate([t] * (n // 128), axis=1)


def _key_blocks(i, tk, block, pairs=False):
    visible = i * ATT_TQ + ATT_TQ
    n_full = (visible + tk - 1) // tk - 1

    def full(j):
        block(pl.multiple_of(j * tk, tk), tk, False)

    if pairs:
        def two(jj, carry):
            full(2 * jj)
            full(2 * jj + 1)
            return carry

        lax.fori_loop(0, n_full // 2, two, 0)

        @pl.when(n_full % 2 == 1)
        def _():
            full(n_full - 1)
    else:
        def one(j, carry):
            full(j)
            return carry

        lax.fori_loop(0, n_full, one, 0)
    last0 = pl.multiple_of(n_full * tk, tk)
    half = tk // 2
    if half % 128 == 0:
        @pl.when(visible - n_full * tk <= half)
        def _():
            block(last0, half, True)

        @pl.when(visible - n_full * tk > half)
        def _():
            block(last0, tk, True)
    else:
        block(last0, tk, True)


def _attn_fwd(qp, kp, name, side=None):
    s = kp.shape[0]
    tk = min(ATT_TK, s)

    def body(q_ref, k_ref, o_ref, lse_ref, m_sc, acc_sc):
        i = pl.program_id(0)
        m_sc[...] = jnp.full_like(m_sc, -jnp.inf)
        acc_sc[...] = jnp.zeros_like(acc_sc)

        def block(k0, width, masked):
            k = k_ref[pl.ds(k0, width), :]
            v1 = jnp.where(_lane_lt(k.shape, 128), k, jnp.ones_like(k))
            for h0 in range(0, MLA_HEADS, ATT_HEAD_GROUP):
                rows = slice(h0 * ATT_TQ, (h0 + ATT_HEAD_GROUP) * ATT_TQ)
                q = q_ref[h0:h0 + ATT_HEAD_GROUP].reshape(ATT_HEAD_GROUP * ATT_TQ, 256)
                sc = _dot(q, k, NT) * ATTN_SCALE_LOG2
                if masked:
                    sc = jnp.where(_visible(k0, i, width), sc, -jnp.inf)
                m_prev = m_sc[rows]
                m_new = jnp.maximum(m_prev, jnp.max(sc, axis=1, keepdims=True))
                p = jnp.exp2(sc - _tile_lanes(m_new, width))
                acc_sc[rows] = _tile_lanes(jnp.exp2(m_prev - m_new), 256) * acc_sc[rows] + _dot(p, v1, NN)
                m_sc[rows] = m_new

        _key_blocks(i, tk, block, pairs=True)
        acc = acc_sc[...]
        l = acc[:, 128:256]
        o_ref[...] = (acc[:, 0:128] / l).astype(BF16).reshape(MLA_HEADS, ATT_TQ, 128)
        lse_ref[...] = (m_sc[...] + jnp.log2(l)).reshape(MLA_HEADS, ATT_TQ, 128)

    head128 = pl.BlockSpec((MLA_HEADS, ATT_TQ, 128), lambda i: (0, i, 0))
    return _grid_call(
        body, name, (s // ATT_TQ,),
        [pl.BlockSpec((MLA_HEADS, ATT_TQ, 256), lambda i: (0, i, 0)), pl.BlockSpec((s, 256), lambda i: (0, 0))],
        [head128, head128],
        [jax.ShapeDtypeStruct((MLA_HEADS, s, 128), BF16), jax.ShapeDtypeStruct((MLA_HEADS, s, 128), F32)],
        [qp, kp], scratch=[pltpu.VMEM((ATT_ROWS, 128), F32), pltpu.VMEM((ATT_ROWS, 256), F32)], side=side)


def _attn_bwd(qp, kp, o, do, lse, name, side=None):
    s = kp.shape[0]
    tk = min(ATT_TK, s)

    def body(q_ref, k_ref, o_ref, do_ref, lse_ref, dq_ref, dk_ref, dv_ref, dq_sc):
        i = pl.program_id(0)

        @pl.when(i == 0)
        def _():
            dk_ref[...] = jnp.zeros_like(dk_ref)
            dv_ref[...] = jnp.zeros_like(dv_ref)

        q = q_ref[...].reshape(ATT_ROWS, 256)
        dov = do_ref[...].reshape(ATT_ROWS, 128)
        delta = jnp.sum(dov.astype(F32) * o_ref[...].reshape(ATT_ROWS, 128).astype(F32), axis=1, keepdims=True)
        delta_t = _tile_lanes(jnp.broadcast_to(delta, (ATT_ROWS, 128)), tk)
        lse_t = _tile_lanes(lse_ref[...].reshape(ATT_ROWS, 128), tk)
        dq_sc[...] = jnp.zeros_like(dq_sc)

        def block(k0, width, masked):
            k = k_ref[pl.ds(k0, width), :]
            p = jnp.exp2(_dot(q, k, NT) * ATTN_SCALE_LOG2 - lse_t[:, 0:width])
            if masked:
                p = jnp.where(_visible(k0, i, width), p, 0.0)
            dv_ref[pl.ds(k0, width), :] += _dot(p, dov, TN)
            ds = (p * (_dot(dov, k[:, 0:128], NT) - delta_t[:, 0:width]) * ATTN_SCALE).astype(BF16)
            dq_sc[...] += _dot(ds, k, NN)
            dk_ref[pl.ds(k0, width), :] += _dot(ds, q, TN)

        _key_blocks(i, tk, block, pairs=True)
        dq_ref[...] = dq_sc[...].astype(BF16).reshape(MLA_HEADS, ATT_TQ, 256)

    head128 = pl.BlockSpec((MLA_HEADS, ATT_TQ, 128), lambda i: (0, i, 0))
    head256 = pl.BlockSpec((MLA_HEADS, ATT_TQ, 256), lambda i: (0, i, 0))
    return _grid_call(
        body, name, (s // ATT_TQ,),
        [head256, pl.BlockSpec((s, 256), lambda i: (0, 0)), head128, head128, head128],
        [head256, pl.BlockSpec((s, 256), lambda i: (0, 0)), pl.BlockSpec((s, 128), lambda i: (0, 0))],
        [jax.ShapeDtypeStruct((MLA_HEADS, s, 256), BF16),
         jax.ShapeDtypeStruct((s, 256), F32), jax.ShapeDtypeStruct((s, 128), F32)],
        [qp, kp, o, do, lse], scratch=[pltpu.VMEM((ATT_ROWS, 256), F32)], side=side)


def _place():
    x, y, c = lax.axis_index("x"), lax.axis_index("y"), lax.axis_index("c")
    return x, y, c, 4 * x + 2 * y + c


def _flip(x, y, c, r):
    px = 1 - x if r & 4 else x
    py = 1 - y if r & 2 else y
    pc = 1 - c if r & 1 else c
    return (px, py, pc), 4 * px + 2 * py + pc


def _adaln_exchange(c8, ada_w, ada_b_cols, blocks, name):
    d = c8.shape[1]
    w_cols = ada_w.shape[2]
    n_arr = len(blocks)

    def body(c_ref, w_ref, b_ref, *refs):
        x_refs, (call_ref, mod_ref), out_refs = refs[:n_arr], refs[n_arr:n_arr + 2], refs[n_arr + 2:2 * n_arr + 2]
        sbuf, rbuf, s1, r1, s2, r2 = refs[2 * n_arr + 2:2 * n_arr + 8]
        gather = _Gather(x_refs, out_refs, *refs[2 * n_arr + 8:])
        x, y, c, me = _place()
        call_ref[pl.ds(pl.multiple_of(me * 8, 8), 8), :] = c_ref[...]
        peers = [_flip(x, y, c, r) for r in range(1, N_DEV)]

        def c_copy(k, src_lin, to):
            rows = call_ref.at[pl.ds(pl.multiple_of(src_lin * 8, 8), 8), :]
            return pltpu.make_async_remote_copy(src_ref=rows, dst_ref=rows, send_sem=s1.at[k], recv_sem=r1.at[k],
                                                device_id=to, device_id_type=MESH)

        first = [c_copy(k, me, peer) for k, (peer, _) in enumerate(peers)]
        for cp in first:
            cp.start()
        for k, (_, lin) in enumerate(peers):
            c_copy(k, lin, (x, y, c)).wait_recv()
        for cp in first:
            cp.wait_send()

        for j in range(N_DEV):
            cj = call_ref[8 * j:8 * j + 8, :]
            cond = cj * _sigmoid(cj)
            for l in range(2):
                sbuf[j, l] = lax.dot_general(cond, w_ref[l], NN, precision=lax.Precision.HIGHEST,
                                             preferred_element_type=F32) + b_ref[l]

        def m_copy(k, src_slot, dst_slot, to):
            return pltpu.make_async_remote_copy(src_ref=sbuf.at[src_slot], dst_ref=rbuf.at[dst_slot],
                                                send_sem=s2.at[k], recv_sem=r2.at[k], device_id=to,
                                                device_id_type=MESH)

        rbuf[me] = sbuf[me]
        second = [m_copy(k, lin, me, peer) for k, (peer, lin) in enumerate(peers)]
        for cp in second:
            cp.start()
        gather.start()
        for k, (_, lin) in enumerate(peers):
            m_copy(k, lin, lin, (x, y, c)).wait_recv()
        for cp in second:
            cp.wait_send()
        for j in range(N_DEV):
            for l in range(2):
                mod_ref[l, :, w_cols * j:w_cols * (j + 1)] = rbuf[j, l]
        gather.forward()
        gather.end()

    vmem = pl.BlockSpec(memory_space=pltpu.VMEM)
    anyspace = pl.BlockSpec(memory_space=pl.ANY)
    g_shapes, g_sems = _gather_extras(blocks)
    return pl.pallas_call(
        body, name=name, in_specs=[vmem, vmem, vmem] + [anyspace] * n_arr, out_specs=[vmem, vmem] + [anyspace] * n_arr,
        out_shape=[jax.ShapeDtypeStruct((8 * N_DEV, d), F32), jax.ShapeDtypeStruct((2, 8, 3 * d), F32)] + g_shapes,
        scratch_shapes=[pltpu.VMEM((N_DEV, 2, 8, w_cols), F32), pltpu.VMEM((N_DEV, 2, 8, w_cols), F32),
                        pltpu.SemaphoreType.DMA((N_DEV - 1,)), pltpu.SemaphoreType.DMA((N_DEV - 1,)),
                        pltpu.SemaphoreType.DMA((N_DEV - 1,)), pltpu.SemaphoreType.DMA((N_DEV - 1,))] + g_sems,
        compiler_params=pltpu.CompilerParams(vmem_limit_bytes=VMEM_LIMIT),
    )(c8, ada_w, ada_b_cols, *blocks)


class _Gather:
    def __init__(self, x_refs, out_refs, send_sems, recv_sems, local_sems):
        x, y, c, _ = _place()
        self.me, self.sibling, self.c = (x, y, c), (x, y, 1 - c), c
        self.chips = [(1 - x, y), (x, 1 - y), (1 - x, 1 - y)]
        self.n_arr = len(x_refs)
        self.out_refs, self.send_sems, self.recv_sems = out_refs, send_sems, recv_sems
        self.mine = [pltpu.make_async_copy(x_refs[t], out_refs[t].at[4 * x + 2 * y + c], local_sems.at[t])
                     for t in range(self.n_arr)]
        self.first = []
        for t in range(self.n_arr):
            self.first.append(self.copy(t, 0, self.me, self.sibling, src=x_refs[t]))
            self.first += [self.copy(t, 1 + j, self.me, (*chip, c), src=x_refs[t]) for j, chip in enumerate(self.chips)]
        self.passed = [self.copy(t, 4 + j, (*chip, c), self.sibling)
                       for t in range(self.n_arr) for j, chip in enumerate(self.chips)]

    def copy(self, t, k, blk, to, src=None):
        slot = self.out_refs[t].at[4 * blk[0] + 2 * blk[1] + blk[2]]
        return pltpu.make_async_remote_copy(src_ref=slot if src is None else src, dst_ref=slot,
                                            send_sem=self.send_sems.at[7 * t + k], recv_sem=self.recv_sems.at[7 * t + k],
                                            device_id=to, device_id_type=MESH)

    def start(self):
        for cp in self.mine + self.first:
            cp.start()

    def forward(self):
        for t in range(self.n_arr):
            for j, chip in enumerate(self.chips):
                self.copy(t, 1 + j, (*chip, self.c), self.me).wait_recv()
                self.passed[3 * t + j].start()

    def end(self):
        for t in range(self.n_arr):
            self.copy(t, 0, self.sibling, self.me).wait_recv()
            for j, chip in enumerate(self.chips):
                self.copy(t, 4 + j, (*chip, 1 - self.c), self.me).wait_recv()
        for cp in self.first + self.passed:
            cp.wait_send()
        for cp in self.mine:
            cp.wait()


def _gather_extras(blocks):
    n_arr = len(blocks)
    return ([jax.ShapeDtypeStruct((N_DEV,) + b.shape, b.dtype) for b in blocks],
            [pltpu.SemaphoreType.DMA((7 * n_arr,)), pltpu.SemaphoreType.DMA((7 * n_arr,)),
             pltpu.SemaphoreType.DMA((n_arr,))])


def _all_gather(blocks, name):
    n_arr = len(blocks)

    def body(*refs):
        gather = _Gather(refs[:n_arr], refs[n_arr:2 * n_arr], *refs[2 * n_arr:])
        gather.start()
        gather.forward()
        gather.end()

    anyspace = pl.BlockSpec(memory_space=pl.ANY)
    shapes, sems = _gather_extras(blocks)
    return pl.pallas_call(body, name=name, in_specs=[anyspace] * n_arr, out_specs=[anyspace] * n_arr,
                          out_shape=shapes, scratch_shapes=sems)(*blocks)


def _scatter_parts(parts, name):
    n_arr = len(parts)

    def body(*refs):
        copies = _exchange_copies(refs[:n_arr], refs[n_arr:2 * n_arr], *refs[2 * n_arr:], False)
        _exchange_start(copies)
        _exchange_wait(copies)

    anyspace = pl.BlockSpec(memory_space=pl.ANY)
    shapes, sems = _exchange_extras(parts, False)
    return pl.pallas_call(body, name=name, in_specs=[anyspace] * n_arr, out_specs=[anyspace] * n_arr,
                          out_shape=shapes, scratch_shapes=sems)(*parts)


def _adamw(w, g, m, v):
    m = ADAM_B1 * m + (1.0 - ADAM_B1) * g
    v = ADAM_B2 * v + (1.0 - ADAM_B2) * (g * g)
    m_hat = m / (1.0 - ADAM_B1 ** ADAM_STEP)
    v_hat = v / (1.0 - ADAM_B2 ** ADAM_STEP)
    return -ADAM_LR * (m_hat / (jnp.sqrt(v_hat) + ADAM_EPS) + ADAM_WD * w), m, v


def _sum_parts_adamw(parts, w, m, v, name):
    _, rows, cols = parts.shape
    tr = max(t for t in range(16, 129, 16) if rows % t == 0)

    def body(p_ref, w_ref, m_ref, v_ref, g_ref, d_ref, mo_ref, vo_ref):
        g = p_ref[0].astype(F32)
        for j in range(1, N_DEV):
            g = g + p_ref[j].astype(F32)
        g_ref[...] = g
        d_ref[...], mo_ref[...], vo_ref[...] = _adamw(w_ref[...], g, m_ref[...], v_ref[...])

    row = _rows3(tr, cols)
    out = jax.ShapeDtypeStruct((1, rows, cols), F32)
    return pl.pallas_call(
        body, name=name, grid=(rows // tr,),
        in_specs=[pl.BlockSpec((N_DEV, tr, cols), lambda i: (0, i, 0)), row, row, row],
        out_specs=[row, row, row, row], out_shape=[out, out, out, out], compiler_params=_params(1),
    )(parts, w, m, v)


def _sum_parts_adamw_whole(parts, w, m, v, name):
    def body(p_ref, w_ref, m_ref, v_ref, g_ref, d_ref, mo_ref, vo_ref):
        g = p_ref[0:1].astype(F32)
        for j in range(1, N_DEV):
            g = g + p_ref[j:j + 1].astype(F32)
        g_ref[...] = g
        d_ref[...], mo_ref[...], vo_ref[...] = _adamw(w_ref[...], g, m_ref[...], v_ref[...])

    out = jax.ShapeDtypeStruct(w.shape, F32)
    return pl.pallas_call(body, name=name, out_shape=[out] * 4,
                          compiler_params=pltpu.CompilerParams(vmem_limit_bytes=VMEM_LIMIT))(parts, w, m, v)


def _sum_parts(parts, name):
    def body(p_ref, g_ref):
        g = p_ref[0]
        for j in range(1, N_DEV):
            g = g + p_ref[j]
        g_ref[...] = g

    return pl.pallas_call(body, name=name, out_shape=jax.ShapeDtypeStruct(parts.shape[1:], F32),
                          compiler_params=pltpu.CompilerParams(vmem_limit_bytes=VMEM_LIMIT))(parts)


def _adamw_many(gs, ws, ms, vs, name):
    n = len(gs)

    def body(*refs):
        for k in range(n):
            g_ref, w_ref, m_ref, v_ref = (refs[q * n + k] for q in range(4))
            d_ref, mo_ref, vo_ref = (refs[(4 + q) * n + k] for q in range(3))
            d_ref[...], mo_ref[...], vo_ref[...] = _adamw(w_ref[...], g_ref[...], m_ref[...], v_ref[...])

    out = [jax.ShapeDtypeStruct(w.shape, F32) for w in ws]
    res = pl.pallas_call(body, name=name, out_shape=out * 3,
                         compiler_params=pltpu.CompilerParams(vmem_limit_bytes=VMEM_LIMIT))(*gs, *ws, *ms, *vs)
    return res[:n], res[n:2 * n], res[2 * n:]


def _ada_w_grad_adamw(c_all, dmod_rows, w, m, v, name):
    def body(c_ref, dm_ref, w_ref, m_ref, v_ref, g_ref, d_ref, mo_ref, vo_ref):
        cv = c_ref[...]
        cond = cv * _sigmoid(cv)
        for l in range(2):
            g = lax.dot_general(cond, dm_ref[l], TN, precision=lax.Precision.HIGHEST, preferred_element_type=F32)
            g_ref[l] = g
            d_ref[l], mo_ref[l], vo_ref[l] = _adamw(w_ref[l], g, m_ref[l], v_ref[l])

    out = jax.ShapeDtypeStruct(w.shape, F32)
    return pl.pallas_call(
        body, name=name, out_shape=[out] * 4, compiler_params=pltpu.CompilerParams(vmem_limit_bytes=VMEM_LIMIT),
    )(c_all, dmod_rows, w, m, v)


REPLICATED = ("ln_g", "ln_b", "gmlp_norm_g", "gmlp_norm_b", "gmlp_ws", "gmlp_bs", "pool_b", "pool_scale",
              "mla_kv_norm_g", "mla_w_uk", "mla_w_uv")
CHUNK_ROWS, ADA_ROW, QNORM_ROW, LOSS_ROW, REP_ROWS = 73, 73, 74, 75, 80
UQ_ROWS, POOLW_ROWS = 96, 32


def _pad_rows(flat2d, rows):
    n, k = flat2d.shape
    return jnp.pad(flat2d, ((0, 0), (0, rows * LANES - k))).reshape(n, rows, LANES)


def _ada_cols_rows(vec):
    return _pad_rows(vec.reshape(2, N_DEV, -1).transpose(1, 0, 2).reshape(N_DEV, -1), 1)


def _unpack_replicated(rep, shapes):
    chunk = sum(s[1] for s in shapes) // N_DEV
    flat, off, out = rep[:, :CHUNK_ROWS].reshape(N_DEV, -1)[:, :chunk].reshape(-1), 0, {}
    for n, size, shape in shapes:
        out[n] = flat[off:off + size].reshape(shape)
        off += size
    cols = 3 * D_MODEL // N_DEV
    out["ada_b"] = rep[:, ADA_ROW, :2 * cols].reshape(N_DEV, 2, cols).transpose(1, 0, 2).reshape(2, -1)
    return out


def kernel(x, c, positions, ada_w, ada_b, ln_g, ln_b, e_w_in, gmlp_norm_g, gmlp_norm_b, gmlp_ws, gmlp_bs, pool_w, pool_b, pool_scale, e_w_out, o_w_in, mla_q_norm_g, mla_kv_norm_g, mla_w_uq, mla_w_uk, mla_w_uv, o_w_out, loss_target, m_ada_w, m_ada_b, m_ln_g, m_ln_b, m_e_w_in, m_gmlp_norm_g, m_gmlp_norm_b, m_gmlp_ws, m_gmlp_bs, m_pool_w, m_pool_b, m_pool_scale, m_e_w_out, m_o_w_in, m_mla_q_norm_g, m_mla_kv_norm_g, m_mla_w_uq, m_mla_w_uk, m_mla_w_uv, m_o_w_out, v_ada_w, v_ada_b, v_ln_g, v_ln_b, v_e_w_in, v_gmlp_norm_g, v_gmlp_norm_b, v_gmlp_ws, v_gmlp_bs, v_pool_w, v_pool_b, v_pool_scale, v_e_w_out, v_o_w_in, v_mla_q_norm_g, v_mla_kv_norm_g, v_mla_w_uq, v_mla_w_uk, v_mla_w_uv, v_o_w_out):
    w_in = dict(ada_w=ada_w, ada_b=ada_b, ln_g=ln_g, ln_b=ln_b, e_w_in=e_w_in, gmlp_norm_g=gmlp_norm_g,
                gmlp_norm_b=gmlp_norm_b, gmlp_ws=gmlp_ws, gmlp_bs=gmlp_bs, pool_w=pool_w, pool_b=pool_b,
                pool_scale=pool_scale, e_w_out=e_w_out, o_w_in=o_w_in, mla_q_norm_g=mla_q_norm_g,
                mla_kv_norm_g=mla_kv_norm_g, mla_w_uq=mla_w_uq, mla_w_uk=mla_w_uk, mla_w_uv=mla_w_uv, o_w_out=o_w_out)
    m_in = dict(ada_w=m_ada_w, ada_b=m_ada_b, ln_g=m_ln_g, ln_b=m_ln_b, e_w_in=m_e_w_in, gmlp_norm_g=m_gmlp_norm_g,
                gmlp_norm_b=m_gmlp_norm_b, gmlp_ws=m_gmlp_ws, gmlp_bs=m_gmlp_bs, pool_w=m_pool_w, pool_b=m_pool_b,
                pool_scale=m_pool_scale, e_w_out=m_e_w_out, o_w_in=m_o_w_in, mla_q_norm_g=m_mla_q_norm_g,
                mla_kv_norm_g=m_mla_kv_norm_g, mla_w_uq=m_mla_w_uq, mla_w_uk=m_mla_w_uk, mla_w_uv=m_mla_w_uv,
                o_w_out=m_o_w_out)
    v_in = dict(ada_w=v_ada_w, ada_b=v_ada_b, ln_g=v_ln_g, ln_b=v_ln_b, e_w_in=v_e_w_in, gmlp_norm_g=v_gmlp_norm_g,
                gmlp_norm_b=v_gmlp_norm_b, gmlp_ws=v_gmlp_ws, gmlp_bs=v_gmlp_bs, pool_w=v_pool_w, pool_b=v_pool_b,
                pool_scale=v_pool_scale, e_w_out=v_e_w_out, o_w_in=v_o_w_in, mla_q_norm_g=v_mla_q_norm_g,
                mla_kv_norm_g=v_mla_kv_norm_g, mla_w_uq=v_mla_w_uq, mla_w_uk=v_mla_w_uk, mla_w_uv=v_mla_w_uv,
                o_w_out=v_o_w_out)
    names = list(w_in)
    seq = x.shape[1]
    d = D_MODEL
    me = 4 * lax.axis_index("x") + 2 * lax.axis_index("y") + lax.axis_index("c")
    ada_cols = ada_w.shape[2]

    ada_b_cols = lax.dynamic_slice_in_dim(ada_b, me * ada_cols, ada_cols, axis=1)
    slab_row = lax.broadcasted_iota(jnp.int32, (8, d), 0)
    slab = jnp.where(slab_row == 0, c, jnp.where(slab_row == 1, jnp.pad(mla_q_norm_g, ((0, 0), (0, d - 32))), 0.0))
    c_all, mod, w_in_e3, pool_w3 = _adaln_exchange(
        slab, ada_w, jnp.broadcast_to(ada_b_cols[:, None, :], (2, 8, ada_cols)),
        [e_w_in[0].astype(BF16), pool_w.astype(BF16).reshape(POOLW_ROWS, LANES)], "adaln_exchange")
    h0 = _modulate(x, mod[0], "modulate0")
    proj0, o_in3 = _matmul_cols_nn(h0, w_in_e3, BF16, 512, "even_in", side=([o_w_in[0].astype(BF16)], True))
    o_in_full = o_in3.transpose(1, 0, 2).reshape(d, ODD_IN)
    w_in_o = jnp.concatenate([o_in_full[:, :448], jnp.zeros((d, 64), BF16), o_in_full[:, 448:]], axis=1)
    pool_w_full = pool_w3.reshape(N_DEV, 4, 32, 256).transpose(1, 0, 2, 3).reshape(4, 256, 256)
    g_q = c_all.reshape(N_DEV, 8, d)[:, 1, :32].reshape(1, MLA_Q_RANK)

    ws, bs_col = gmlp_ws[0], gmlp_bs[0].reshape(GMLP_HEADS, GMLP_BLOCK, 1)
    wuk2, wuv2 = mla_w_uk[0].reshape(MLA_KV_RANK, -1), mla_w_uv[0].reshape(MLA_KV_RANK, -1)
    inv = 1.0 / (ROPE_THETA ** (jnp.arange(0, MLA_ROPE, 2, dtype=F32) / MLA_ROPE))
    ang = positions[0].astype(F32)[:, None] * inv
    cosp = jnp.tile(jnp.cos(ang), (1, 4))
    sinp = jnp.tile(jnp.concatenate([-jnp.sin(ang), jnp.sin(ang)], axis=1), (1, 2))

    mix0, w_out_e3 = _even_fwd(proj0, ws, bs_col, gmlp_norm_g, gmlp_norm_b, pool_w_full, pool_b, pool_scale, "even_mix",
                               side=([e_w_out[0].astype(BF16)], True))
    w_out_e = w_out_e3.reshape(-1, d)
    y0, uq3 = _matmul([(mix0, w_out_e)], "nn", F32, seq, d, 512, 1024, "even_out",
                      side=([mla_w_uq.astype(BF16).reshape(UQ_ROWS, LANES)], True))
    uq_full = uq3.reshape(MLA_Q_RANK, MLA_HEADS, MLA_NOPE + MLA_ROPE)
    w_uq_n = uq_full[:, :, :MLA_NOPE].reshape(MLA_Q_RANK, -1)
    w_uq_r = uq_full[:, :, MLA_NOPE:].reshape(MLA_Q_RANK, -1)
    w_uq = jnp.concatenate([w_uq_n, w_uq_r], axis=1)
    x1, h1 = _resid_ln(x, y0, mod[0], ln_g[0:1], ln_b[0:1], mod[1], "resid_ln0")

    (proj1,) = _matmul([(h1, w_in_o)], "nn", BF16, seq, ODD_IN_PAD, 512, ODD_IN_PAD, "odd_in")
    qn, kp = _mla_prep(proj1, cosp, sinp, g_q, mla_kv_norm_g, "mla_prep")
    (q_up,) = _matmul([(qn, w_uq)], "nn", BF16, seq, 3072, 512, 3072, "q_up")
    qp = _q_heads(q_up, cosp, sinp, wuk2, "q_heads")
    o_lat, lse, w_out_o3 = _attn_fwd(qp, kp, "attn_fwd", side=([o_w_out[0].astype(BF16)], True))
    w_out_o = w_out_o3.reshape(-1, d)
    gated = _o_gate(o_lat, proj1, wuv2, "o_gate")
    (y1,) = _matmul([(gated, w_out_o)], "nn", F32, seq, d, 512, 1024, "odd_out")

    dy1, dxres1, red2 = _final_ln_loss_bwd(x1, y1, mod[1], ln_g[1:2], ln_b[1:2], loss_target, "final_ln_loss")
    (dgated,) = _matmul([(dy1, w_out_o)], "nt", BF16, seq, MLA_WIDTH, 512, MLA_WIDTH, "odd_out_dx")
    (g_w_out_o,) = _matmul([(gated, dy1)], "tn", BF16, MLA_WIDTH, d, 256, d, "odd_out_dw")
    dproj1_z, do_lat, g_wuv = _o_gate_bwd(dgated, o_lat, proj1, wuv2, "o_gate_bwd")
    dqp, dkp, dvv, r_o_out = _attn_bwd(qp, kp, o_lat, do_lat, lse, "attn_bwd",
                                       side=([g_w_out_o.reshape(N_DEV, -1, d)], False))
    dq_nope, dq_rope, g_wuk = _q_heads_bwd(dqp, q_up, cosp, sinp, wuk2, "q_heads_bwd")
    (dqn,) = _matmul([(dq_nope, w_uq_n), (dq_rope, w_uq_r)], "nt", F32, seq, MLA_Q_RANK, 512, 256, "q_up_dx")
    (g_wuq_n,) = _matmul([(qn, dq_nope)], "tn", F32, MLA_Q_RANK, MLA_WIDTH, 256, MLA_WIDTH, "q_up_dw_nope")
    (g_wuq_r,) = _matmul([(qn, dq_rope)], "tn", F32, MLA_Q_RANK, 1024, 256, 1024, "q_up_dw_rope")
    dproj1, red_mla = _mla_prep_bwd(proj1, dqn, dkp, dvv, cosp, sinp, g_q, mla_kv_norm_g, dproj1_z, "mla_prep_bwd")
    (dh1,) = _matmul([(dproj1, w_in_o)], "nt", F32, seq, d, 512, d, "odd_in_dx")
    part_uq = jnp.concatenate([g_wuq_n.reshape(MLA_Q_RANK, MLA_HEADS, MLA_NOPE),
                               g_wuq_r.reshape(MLA_Q_RANK, MLA_HEADS, MLA_ROPE)], axis=2).astype(BF16).reshape(
                                   (N_DEV,) + mla_w_uq.shape[1:])
    (g_w_in_o,) = _matmul([(h1, dproj1)], "tn", BF16, d, ODD_IN_PAD, 256, ODD_IN_PAD // 2, "odd_in_dw", n_outer=True)
    part_o_in = jnp.concatenate([g_w_in_o[:, :448], g_w_in_o[:, 512:]], axis=1).reshape(d, N_DEV, -1).transpose(1, 0, 2)
    dy0, dxres0, red1 = _mid_bwd(dh1, dxres1, x, y0, mod[0], mod[1], ln_g[0:1], ln_b[0:1], "mid_bwd")
    dmix, r_uq = _matmul([(dy0, w_out_e)], "nt", BF16, seq, 2048, 512, 2048, "even_out_dx", side=([part_uq], False))
    (g_w_out_e,) = _matmul([(mix0, dy0)], "tn", BF16, 2048, d, 256, d, "even_out_dw")
    dproj0, g_ws, g_bs, g_ng, g_nb, g_pw, g_pb, g_ps, r_o_in = _even_bwd(
        proj0, dmix, ws, bs_col, gmlp_norm_g, gmlp_norm_b, pool_w_full, pool_b, pool_scale, "even_mix_bwd",
        side=([part_o_in], False))
    part_pw = g_pw.reshape(4, N_DEV, 32, 256).transpose(1, 0, 2, 3)
    part_e_in, r_e_out, r_pw = _matmul_cols_tn(h0, dproj0, w_in_e3.shape[2], BF16, 512, "even_in_dw",
                                               side=([g_w_out_e.reshape(N_DEV, -1, d), part_pw], False))
    dh0, r_e_in = _matmul_cols_nt(dproj0, w_in_e3, F32, 512, "even_in_dx", side=([part_e_in], False))
    grad_x, red0 = _first_bwd(dh0, dxres0, x, mod[0], "first_bwd")

    t_mask = lax.broadcasted_iota(jnp.int32, (GMLP_BLOCK, GMLP_BLOCK), 0) // CHUNK
    s_mask = lax.broadcasted_iota(jnp.int32, (GMLP_BLOCK, GMLP_BLOCK), 1) // CHUNK
    part = {
        "ln_g": jnp.stack([red1[2], red2[0]]), "ln_b": jnp.stack([red1[3], red2[1]]),
        "gmlp_norm_g": g_ng, "gmlp_norm_b": g_nb,
        "gmlp_ws": jnp.where(s_mask <= t_mask, g_ws, 0.0), "gmlp_bs": g_bs,
        "pool_b": g_pb, "pool_scale": g_ps, "mla_kv_norm_g": red_mla[1, :MLA_KV_RANK],
        "mla_w_uk": g_wuk, "mla_w_uv": g_wuv,
    }
    dmod = jnp.stack([jnp.concatenate([red0[1], red0[0], red1[4]]),
                      jnp.concatenate([red1[1], red1[0], red2[2]])])

    loss_row = jnp.pad(jnp.broadcast_to((0.5 / d * jnp.sum(red2[3])).reshape(1, 1, 1), (N_DEV, 1, 1)),
                       ((0, 0), (0, 0), (0, LANES - 1)))
    part_small = jnp.concatenate([
        _pad_rows(jnp.concatenate([part[n].reshape(-1) for n in REPLICATED]).reshape(N_DEV, -1), CHUNK_ROWS),
        jnp.pad(jnp.concatenate([_ada_cols_rows(dmod), _pad_rows(red_mla[0].reshape(N_DEV, -1), 1), loss_row], axis=1),
                ((0, 0), (0, REP_ROWS - LOSS_ROW - 1), (0, 0)))], axis=1)
    (r_small,) = _scatter_parts([part_small], "grad_scatter")

    res = {"e_w_in": _sum_parts_adamw(r_e_in, e_w_in, m_e_w_in, v_e_w_in, "adamw_e_w_in"),
           "o_w_in": _sum_parts_adamw(r_o_in, o_w_in, m_o_w_in, v_o_w_in, "adamw_o_w_in"),
           "e_w_out": _sum_parts_adamw(r_e_out, e_w_out, m_e_w_out, v_e_w_out, "adamw_e_w_out"),
           "o_w_out": _sum_parts_adamw(r_o_out, o_w_out, m_o_w_out, v_o_w_out, "adamw_o_w_out"),
           "mla_w_uq": _sum_parts_adamw_whole(r_uq, mla_w_uq, m_mla_w_uq, v_mla_w_uq, "adamw_w_uq"),
           "pool_w": _sum_parts_adamw_whole(r_pw, pool_w, m_pool_w, v_pool_w, "adamw_pool_w")}
    small_sum = _sum_parts(r_small, "small_sum")
    loss = small_sum[LOSS_ROW, 0]
    (rep_sum,) = _all_gather([small_sum], "replicated_gather")
    grads = _unpack_replicated(rep_sum, [(n, w_in[n].size, w_in[n].shape) for n in REPLICATED])
    grads["mla_q_norm_g"] = small_sum[QNORM_ROW:QNORM_ROW + 1, :32]
    small_names = list(grads)
    deltas, new_ms, new_vs = _adamw_many([grads[n] for n in small_names], [w_in[n] for n in small_names],
                                         [m_in[n] for n in small_names], [v_in[n] for n in small_names], "small_adamw")
    for k, n in enumerate(small_names):
        res[n] = [grads[n], deltas[k], new_ms[k], new_vs[k]]
    dmod_all = r_small[:, ADA_ROW, :2 * ada_cols].reshape(N_DEV, 2, ada_cols).transpose(1, 0, 2)
    dmod_rows = jnp.pad(dmod_all[:, :, None, :], ((0, 0), (0, 0), (0, 7), (0, 0))).reshape(2, 8 * N_DEV, ada_cols)
    res["ada_w"] = _ada_w_grad_adamw(c_all, dmod_rows, ada_w, m_ada_w, v_ada_w, "ada_w_adamw")

    return (loss, grad_x, *[res[n][0] for n in names], *[res[n][1] for n in names],
            *[res[n][2] for n in names], *[res[n][3] for n in names])
```

```python
import functools

import jax
import jax.numpy as jnp
from jax import lax
from jax.experimental import pallas as pl
from jax.experimental.pallas import tpu as pltpu

F32 = jnp.float32
BF16 = jnp.bfloat16

D_MODEL = 1024
CHUNK = 64
LN_EPS = 1e-5
GMLP_HEADS = 4
GMLP_HEAD_DIM = 256
GMLP_BLOCK = 128
POOL_WINDOWS = (2, 4, 8, 16)
POOL_GROUP_DIM = 256
POOL_HALO = 16
MLA_HEADS = 16
MLA_NOPE = 128
MLA_ROPE = 64
MLA_Q_RANK = 256
MLA_KV_RANK = 128
MLA_WIDTH = 2048
ODD_IN = 2496
ODD_IN_PAD = 2560
ROPE_THETA = 10000.0
ATTN_SCALE = (MLA_NOPE + MLA_ROPE) ** -0.5
ATTN_SCALE_LOG2 = ATTN_SCALE * 1.4426950408889634
DEEPNORM_ALPHA = 4.0 ** 0.25
ADAM_LR, ADAM_B1, ADAM_B2, ADAM_EPS, ADAM_WD, ADAM_STEP = 0.001, 0.9, 0.999, 1e-8, 0.01, 10
N_DEV = 8
LANES = 1024
VMEM_LIMIT = 56 * 1024 * 1024
MESH = pl.DeviceIdType.MESH

NT = (((1,), (1,)), ((), ()))
NN = (((1,), (0,)), ((), ()))
TN = (((0,), (0,)), ((), ()))


def _params(n_axes):
    return pltpu.CompilerParams(dimension_semantics=("arbitrary",) * n_axes, vmem_limit_bytes=VMEM_LIMIT)


def _dot(a, b, dn):
    return lax.dot_general(a.astype(BF16), b.astype(BF16), dn, preferred_element_type=F32)


def _sigmoid(z):
    return 1.0 / (1.0 + jnp.exp(-z))


def _colsum(t):
    return jnp.sum(t, axis=0, keepdims=True)


EXCHANGE_RELATIONS = {"devices": tuple(range(1, N_DEV)), "chips": (2, 4, 6)}


def _exchange_copies(g_refs, r_refs, send_sems, recv_sems, local_sems, kind):
    x, y, c, me = _place()
    n_arr = len(g_refs)

    def slot(lin):
        return lin // 2 if kind == "chips" else lin

    own = [pltpu.make_async_copy(g_refs[t].at[slot(me)], r_refs[t].at[slot(me)], local_sems.at[t]) for t in range(n_arr)]
    sends, recvs = [], []
    for n, r in enumerate(EXCHANGE_RELATIONS[kind]):
        peer, lin = _flip(x, y, c, r)
        for t in range(n_arr):
            k = n_arr * n + t
            sends.append(pltpu.make_async_remote_copy(
                src_ref=g_refs[t].at[slot(lin)], dst_ref=r_refs[t].at[slot(me)], send_sem=send_sems.at[k],
                recv_sem=recv_sems.at[k], device_id=peer, device_id_type=MESH))
            recvs.append(pltpu.make_async_remote_copy(
                src_ref=g_refs[t].at[slot(lin)], dst_ref=r_refs[t].at[slot(lin)], send_sem=send_sems.at[k],
                recv_sem=recv_sems.at[k], device_id=(x, y, c), device_id_type=MESH))
    return own, sends, recvs


def _exchange_start(copies):
    own, sends, _ = copies
    for cp in own + sends:
        cp.start()


def _exchange_wait(copies):
    own, sends, recvs = copies
    for cp in recvs:
        cp.wait_recv()
    for cp in sends:
        cp.wait_send()
    for cp in own:
        cp.wait()


def _exchange_extras(parts, kind):
    shapes = [jax.ShapeDtypeStruct(p.shape, p.dtype) for p in parts]
    n = len(parts) * len(EXCHANGE_RELATIONS[kind])
    return shapes, [pltpu.SemaphoreType.DMA((n,)), pltpu.SemaphoreType.DMA((n,)), pltpu.SemaphoreType.DMA((len(parts),))]


def _grid_call(body, name, grid, in_specs, out_specs, out_shape, args, scratch=(), side=None):
    if side is None:
        return pl.pallas_call(body, name=name, grid=grid, in_specs=in_specs, out_specs=out_specs,
                              out_shape=out_shape, scratch_shapes=list(scratch),
                              compiler_params=_params(len(grid)))(*args)
    parts, kind = side
    gather = kind == "gather"
    n_in, n_out, n_sc, n_arr = len(args), len(out_shape), len(scratch), len(parts)
    side_shapes, side_sems = _gather_extras(parts) if gather else _exchange_extras(parts, kind)
    mid = tuple(g // 2 for g in grid)

    def wrapped(*refs):
        ins, g_refs = refs[:n_in], refs[n_in:n_in + n_arr]
        outs = refs[n_in + n_arr:n_in + n_arr + n_out]
        r_refs = refs[n_in + n_arr + n_out:n_in + 2 * n_arr + n_out]
        sc = refs[n_in + 2 * n_arr + n_out:n_in + 2 * n_arr + n_out + n_sc]
        ids = [pl.program_id(a) for a in range(len(grid))]

        def at(step):
            return functools.reduce(jnp.logical_and, [i == s for i, s in zip(ids, step)])

        first, last = at((0,) * len(grid)), at(tuple(g - 1 for g in grid))
        if gather:
            exchange = _Gather(g_refs, r_refs, *refs[-3:])
            pl.when(first)(exchange.start)
            if mid != (0,) * len(grid):
                pl.when(at(mid))(exchange.forward)
            body(*ins, *outs, *sc)

            @pl.when(last)
            def _():
                if mid == (0,) * len(grid):
                    exchange.forward()
                exchange.end()
        else:
            copies = _exchange_copies(g_refs, r_refs, *refs[-3:], kind)
            pl.when(first)(lambda: _exchange_start(copies))
            body(*ins, *outs, *sc)
            pl.when(last)(lambda: _exchange_wait(copies))

    anyspace = pl.BlockSpec(memory_space=pl.ANY)
    return pl.pallas_call(
        wrapped, name=name, grid=grid, in_specs=list(in_specs) + [anyspace] * n_arr,
        out_specs=list(out_specs) + [anyspace] * n_arr, out_shape=list(out_shape) + side_shapes,
        scratch_shapes=list(scratch) + side_sems, compiler_params=_params(len(grid)),
    )(*args, *parts)


def _matmul(pairs, mode, out_dtype, m, n, tm, tn, name, side=None, n_outer=False):
    dn = {"nn": NN, "nt": NT, "tn": TN}[mode]
    tm, tn = min(tm, m), min(tn, n)
    n_pairs = len(pairs)
    grid = (n // tn, m // tm) if n_outer else (m // tm, n // tn)

    def ij(f):
        return (lambda j, i: f(i, j)) if n_outer else f

    def body(*refs):
        o_ref = refs[-1]
        acc = None
        for p in range(n_pairs):
            t = _dot(refs[2 * p][...], refs[2 * p + 1][...], dn)
            acc = t if acc is None else acc + t
        o_ref[...] = acc.astype(o_ref.dtype)

    in_specs, args = [], []
    for a, b in pairs:
        if mode == "nn":
            k = a.shape[1]
            in_specs += [pl.BlockSpec((tm, k), ij(lambda i, j: (i, 0))), pl.BlockSpec((k, tn), ij(lambda i, j: (0, j)))]
        elif mode == "nt":
            k = a.shape[1]
            in_specs += [pl.BlockSpec((tm, k), ij(lambda i, j: (i, 0))), pl.BlockSpec((tn, k), ij(lambda i, j: (j, 0)))]
        else:
            k = a.shape[0]
            in_specs += [pl.BlockSpec((k, tm), ij(lambda i, j: (0, i))), pl.BlockSpec((k, tn), ij(lambda i, j: (0, j)))]
        args += [a, b]
    return _grid_call(body, name, grid, in_specs, [pl.BlockSpec((tm, tn), ij(lambda i, j: (i, j)))],
                      [jax.ShapeDtypeStruct((m, n), out_dtype)], args, side=side)


def _matmul_cols_nn(a, w3, out_dtype, tm, name, side=None):
    m, k = a.shape
    _, _, n = w3.shape
    tm = min(tm, m)

    def body(a_ref, w_ref, o_ref):
        av = a_ref[...]
        for j in range(N_DEV):
            o_ref[:, n * j:n * (j + 1)] = _dot(av, w_ref[j], NN).astype(o_ref.dtype)

    return _grid_call(
        body, name, (m // tm,),
        [pl.BlockSpec((tm, k), lambda i: (i, 0)), pl.BlockSpec((N_DEV, k, n), lambda i: (0, 0, 0))],
        [pl.BlockSpec((tm, N_DEV * n), lambda i: (i, 0))], [jax.ShapeDtypeStruct((m, N_DEV * n), out_dtype)], [a, w3],
        side=side)


def _matmul_cols_nt(a, w3, out_dtype, tm, name, side=None):
    m = a.shape[0]
    _, k, n = w3.shape
    tm = min(tm, m)

    def body(a_ref, w_ref, o_ref):
        acc = _dot(a_ref[:, 0:n], w_ref[0], NT)
        for j in range(1, N_DEV):
            acc = acc + _dot(a_ref[:, n * j:n * (j + 1)], w_ref[j], NT)
        o_ref[...] = acc.astype(o_ref.dtype)

    return _grid_call(
        body, name, (m // tm,),
        [pl.BlockSpec((tm, N_DEV * n), lambda i: (i, 0)), pl.BlockSpec((N_DEV, k, n), lambda i: (0, 0, 0))],
        [pl.BlockSpec((tm, k), lambda i: (i, 0))], [jax.ShapeDtypeStruct((m, k), out_dtype)], [a, w3], side=side)


def _matmul_cols_tn(a, b, n, out_dtype, tk, name, side=None):
    m, k = a.shape
    tk = min(tk, k)

    def body(a_ref, b_ref, o_ref):
        o_ref[...] = _dot(a_ref[...], b_ref[...], TN).astype(o_ref.dtype)

    return _grid_call(
        body, name, (N_DEV, k // tk),
        [pl.BlockSpec((m, tk), lambda j, i: (0, i)), pl.BlockSpec((m, n), lambda j, i: (0, j))],
        [pl.BlockSpec((None, tk, n), lambda j, i: (j, i, 0))], [jax.ShapeDtypeStruct((N_DEV, k, n), out_dtype)], [a, b],
        side=side)


def _rows3(tm, d):
    return pl.BlockSpec((None, tm, d), lambda i: (0, i, 0))


def _modulate(x, mod, name):
    _, s, d = x.shape
    tm = min(s, 512)

    def body(x_ref, m_ref, h_ref):
        shift, scale = m_ref[0:1, 0:d], m_ref[0:1, d:2 * d]
        h_ref[...] = (x_ref[...] * (1.0 + scale) + shift).astype(BF16)

    return pl.pallas_call(
        body, name=name, grid=(s // tm,),
        in_specs=[_rows3(tm, d), pl.BlockSpec((8, 3 * d), lambda i: (0, 0))],
        out_specs=pl.BlockSpec((tm, d), lambda i: (i, 0)),
        out_shape=jax.ShapeDtypeStruct((s, d), BF16), compiler_params=_params(1),
    )(x, mod)


def _ln_stats(r):
    mu = jnp.mean(r, axis=-1, keepdims=True)
    rc = r - mu
    var = jnp.mean(rc * rc, axis=-1, keepdims=True)
    rstd = lax.rsqrt(var + LN_EPS)
    return rc * rstd, rstd


def _ln_bwd(dxhat, xhat, rstd):
    return rstd * (dxhat - jnp.mean(dxhat, axis=-1, keepdims=True)
                   - xhat * jnp.mean(dxhat * xhat, axis=-1, keepdims=True))


def _resid_ln(x, y, mod, g, b, mod_next, name):
    _, s, d = x.shape
    tm = min(s, 512)

    def body(x_ref, y_ref, m_ref, g_ref, b_ref, mn_ref, o_ref, h_ref):
        gate = m_ref[0:1, 2 * d:3 * d]
        xhat, _ = _ln_stats(DEEPNORM_ALPHA * x_ref[...] + (1.0 + gate) * y_ref[...])
        out = xhat * g_ref[...] + b_ref[...]
        o_ref[...] = out
        h_ref[...] = (out * (1.0 + mn_ref[0:1, d:2 * d]) + mn_ref[0:1, 0:d]).astype(BF16)

    row = pl.BlockSpec((tm, d), lambda i: (i, 0))
    vec = pl.BlockSpec((1, d), lambda i: (0, 0))
    modspec = pl.BlockSpec((8, 3 * d), lambda i: (0, 0))
    return pl.pallas_call(
        body, name=name, grid=(s // tm,),
        in_specs=[_rows3(tm, d), row, modspec, vec, vec, modspec],
        out_specs=[_rows3(tm, d), row],
        out_shape=[jax.ShapeDtypeStruct((1, s, d), F32), jax.ShapeDtypeStruct((s, d), BF16)],
        compiler_params=_params(1),
    )(x, y, mod, g, b, mod_next)


def _final_ln_loss_bwd(x, y, mod, g, b, target, name):
    _, s, d = x.shape
    tm = min(s, 256)

    def body(x_ref, y_ref, m_ref, g_ref, b_ref, t_ref, dy_ref, dx_ref, red_ref):
        @pl.when(pl.program_id(0) == 0)
        def _():
            red_ref[...] = jnp.zeros_like(red_ref)

        gate = m_ref[0:1, 2 * d:3 * d]
        yv = y_ref[...]
        xhat, rstd = _ln_stats(DEEPNORM_ALPHA * x_ref[...] + (1.0 + gate) * yv)
        err = xhat * g_ref[...] + b_ref[...] - t_ref[...]
        dout = err * (1.0 / d)
        dr = _ln_bwd(dout * g_ref[...], xhat, rstd)
        dy_ref[...] = ((1.0 + gate) * dr).astype(BF16)
        dx_ref[...] = DEEPNORM_ALPHA * dr
        red_ref[0:1, :] += _colsum(dout * xhat)
        red_ref[1:2, :] += _colsum(dout)
        red_ref[2:3, :] += _colsum(dr * yv)
        red_ref[3:4, :] += _colsum(err * err)

    row = pl.BlockSpec((tm, d), lambda i: (i, 0))
    vec = pl.BlockSpec((1, d), lambda i: (0, 0))
    return pl.pallas_call(
        body, name=name, grid=(s // tm,),
        in_specs=[_rows3(tm, d), row, pl.BlockSpec((8, 3 * d), lambda i: (0, 0)), vec, vec, _rows3(tm, d)],
        out_specs=[row, row, pl.BlockSpec((8, d), lambda i: (0, 0))],
        out_shape=[jax.ShapeDtypeStruct((s, d), BF16), jax.ShapeDtypeStruct((s, d), F32),
                   jax.ShapeDtypeStruct((8, d), F32)],
        compiler_params=_params(1),
    )(x, y, mod, g, b, target)


def _mid_bwd(dh, dxres, x, y, mod_lo, mod_hi, g, b, name):
    _, s, d = x.shape
    tm = min(s, 256)

    def body(dh_ref, dxr_ref, x_ref, y_ref, ml_ref, mh_ref, g_ref, b_ref, dy_ref, dx_ref, red_ref):
        @pl.when(pl.program_id(0) == 0)
        def _():
            red_ref[...] = jnp.zeros_like(red_ref)

        gate = ml_ref[0:1, 2 * d:3 * d]
        scale_hi = mh_ref[0:1, d:2 * d]
        yv, dhv = y_ref[...], dh_ref[...]
        xhat, rstd = _ln_stats(DEEPNORM_ALPHA * x_ref[...] + (1.0 + gate) * yv)
        x_mid = xhat * g_ref[...] + b_ref[...]
        dx_mid = dxr_ref[...] + dhv * (1.0 + scale_hi)
        dr = _ln_bwd(dx_mid * g_ref[...], xhat, rstd)
        dy_ref[...] = ((1.0 + gate) * dr).astype(BF16)
        dx_ref[...] = DEEPNORM_ALPHA * dr
        red_ref[0:1, :] += _colsum(dhv * x_mid)
        red_ref[1:2, :] += _colsum(dhv)
        red_ref[2:3, :] += _colsum(dx_mid * xhat)
        red_ref[3:4, :] += _colsum(dx_mid)
        red_ref[4:5, :] += _colsum(dr * yv)

    row = pl.BlockSpec((tm, d), lambda i: (i, 0))
    vec = pl.BlockSpec((1, d), lambda i: (0, 0))
    modspec = pl.BlockSpec((8, 3 * d), lambda i: (0, 0))
    return pl.pallas_call(
        body, name=name, grid=(s // tm,),
        in_specs=[row, row, _rows3(tm, d), row, modspec, modspec, vec, vec],
        out_specs=[row, row, pl.BlockSpec((8, d), lambda i: (0, 0))],
        out_shape=[jax.ShapeDtypeStruct((s, d), BF16), jax.ShapeDtypeStruct((s, d), F32),
                   jax.ShapeDtypeStruct((8, d), F32)],
        compiler_params=_params(1),
    )(dh, dxres, x, y, mod_lo, mod_hi, g, b)


def _first_bwd(dh, dxres, x, mod, name):
    _, s, d = x.shape
    tm = min(s, 512)

    def body(dh_ref, dxr_ref, x_ref, m_ref, gx_ref, red_ref):
        @pl.when(pl.program_id(0) == 0)
        def _():
            red_ref[...] = jnp.zeros_like(red_ref)

        scale = m_ref[0:1, d:2 * d]
        dhv = dh_ref[...]
        gx_ref[...] = dxr_ref[...] + dhv * (1.0 + scale)
        red_ref[0:1, :] += _colsum(dhv * x_ref[...])
        red_ref[1:2, :] += _colsum(dhv)

    row = pl.BlockSpec((tm, d), lambda i: (i, 0))
    return pl.pallas_call(
        body, name=name, grid=(s // tm,),
        in_specs=[row, row, _rows3(tm, d), pl.BlockSpec((8, 3 * d), lambda i: (0, 0))],
        out_specs=[_rows3(tm, d), pl.BlockSpec((8, d), lambda i: (0, 0))],
        out_shape=[jax.ShapeDtypeStruct((1, s, d), F32), jax.ShapeDtypeStruct((8, d), F32)],
        compiler_params=_params(1),
    )(dh, dxres, x, mod)


EVEN_TM = 256


def _gmlp_mask():
    t = lax.broadcasted_iota(jnp.int32, (GMLP_BLOCK, GMLP_BLOCK), 0) // CHUNK
    s = lax.broadcasted_iota(jnp.int32, (GMLP_BLOCK, GMLP_BLOCK), 1) // CHUNK
    return s <= t


def _window_sum(ext, win, back):
    n = ext.shape[0]
    k = 1
    while k < win:
        ext = ext + pltpu.roll(ext, k if back else n - k, 0)
        k *= 2
    return ext


def _inv_count(row0, rows, win):
    t = row0 + lax.broadcasted_iota(jnp.int32, (rows, 1), 0)
    return t, 1.0 / jnp.minimum(t + 1, win).astype(F32)


def _pooled(xb, halo, row0, win):
    tm = xb.shape[0]
    sums = _window_sum(jnp.concatenate([halo, xb], axis=0), win, True)[POOL_HALO:]
    _, inv = _inv_count(row0, tm, win)
    return sums * inv - xb


def _even_fwd(proj, ws, bs_col, ng, nb, pw, pb, ps, name, side=None):
    s = proj.shape[0]
    tm = min(s, EVEN_TM)
    hd, gd = GMLP_HEAD_DIM, POOL_GROUP_DIM

    def body(p_ref, halo_ref, ws_ref, bs_ref, ng_ref, nb_ref, pw_ref, pb_ref, ps_ref, m_ref):
        i = pl.program_id(0)
        mask = _gmlp_mask()
        for h in range(GMLP_HEADS):
            wm = jnp.where(mask, ws_ref[h], 0.0).astype(BF16)
            for blk in range(tm // GMLP_BLOCK):
                rows = slice(blk * GMLP_BLOCK, (blk + 1) * GMLP_BLOCK)
                cu, cv, cz = h * hd, 1024 + h * hd, 2048 + h * hd
                vhat, _ = _ln_stats(p_ref[rows, cv:cv + hd].astype(F32))
                vn = vhat * ng_ref[...] + nb_ref[...]
                sv = _dot(wm, vn, NN) + bs_ref[h]
                za = p_ref[rows, cz:cz + hd].astype(F32)
                m_ref[rows, cu:cu + hd] = (p_ref[rows, cu:cu + hd].astype(F32) * sv * (za * _sigmoid(za))).astype(BF16)
        for g, win in enumerate(POOL_WINDOWS):
            cx, cz = 3072 + g * gd, 4096 + g * gd
            halo = jnp.where(i > 0, halo_ref[:, g * gd:(g + 1) * gd].astype(F32), 0.0)
            pooled = _pooled(p_ref[:, cx:cx + gd].astype(F32), halo, i * tm, win)
            yb = _dot(pooled, pw_ref[g], NN) + pb_ref[:, g * gd:(g + 1) * gd]
            zb = p_ref[:, cz:cz + gd].astype(F32)
            m_ref[:, 1024 + g * gd:1024 + (g + 1) * gd] = (
                yb * ps_ref[:, g * gd:(g + 1) * gd] * (zb * _sigmoid(zb))).astype(BF16)

    hb = tm // POOL_HALO
    return _grid_call(
        body, name, (s // tm,),
        [
            pl.BlockSpec((tm, 5120), lambda i: (i, 0)),
            pl.BlockSpec((POOL_HALO, 1024), lambda i: (jnp.maximum(i * hb - 1, 0), 3)),
            pl.BlockSpec((GMLP_HEADS, GMLP_BLOCK, GMLP_BLOCK), lambda i: (0, 0, 0)),
            pl.BlockSpec((GMLP_HEADS, GMLP_BLOCK, 1), lambda i: (0, 0, 0)),
            pl.BlockSpec((1, hd), lambda i: (0, 0)), pl.BlockSpec((1, hd), lambda i: (0, 0)),
            pl.BlockSpec((4, gd, gd), lambda i: (0, 0, 0)),
            pl.BlockSpec((1, 1024), lambda i: (0, 0)), pl.BlockSpec((1, 1024), lambda i: (0, 0)),
        ],
        [pl.BlockSpec((tm, 2048), lambda i: (i, 0))], [jax.ShapeDtypeStruct((s, 2048), BF16)],
        [proj, proj, ws, bs_col, ng, nb, pw, pb, ps], side=side)


def _even_bwd(proj, dm, ws, bs_col, ng, nb, pw, pb, ps, name, side=None):
    s = proj.shape[0]
    tm = min(s, EVEN_TM)
    hd, gd = GMLP_HEAD_DIM, POOL_GROUP_DIM
    n_tiles = s // tm

    def body(p_ref, halo_ref, zbn_ref, dm_ref, dbn_ref, ws_ref, bs_ref, ng_ref, nb_ref, pw_ref, pb_ref, ps_ref,
             dp_ref, dws_ref, dbs_ref, dng_ref, dnb_ref, dpw_ref, dpb_ref, dps_ref):
        i = pl.program_id(0)

        @pl.when(i == 0)
        def _():
            for r in (dws_ref, dbs_ref, dng_ref, dnb_ref, dpw_ref, dpb_ref, dps_ref):
                r[...] = jnp.zeros_like(r)

        mask = _gmlp_mask()
        for h in range(GMLP_HEADS):
            wm = jnp.where(mask, ws_ref[h], 0.0).astype(BF16)
            for blk in range(tm // GMLP_BLOCK):
                rows = slice(blk * GMLP_BLOCK, (blk + 1) * GMLP_BLOCK)
                cu, cv, cz = h * hd, 1024 + h * hd, 2048 + h * hd
                vhat, rstd = _ln_stats(p_ref[rows, cv:cv + hd].astype(F32))
                vn = (vhat * ng_ref[...] + nb_ref[...]).astype(BF16)
                sv = _dot(wm, vn, NN) + bs_ref[h]
                u, za = p_ref[rows, cu:cu + hd].astype(F32), p_ref[rows, cz:cz + hd].astype(F32)
                da = dm_ref[rows, cu:cu + hd].astype(F32)
                sig = _sigmoid(za)
                sa = za * sig
                dau = da * u
                dsv = dau * sa
                dp_ref[rows, cu:cu + hd] = (da * sv * sa).astype(BF16)
                dp_ref[rows, cz:cz + hd] = (dau * sv * (sig * (1.0 + za * (1.0 - sig)))).astype(BF16)
                dsv_b = dsv.astype(BF16)
                dbs_ref[h] += jnp.sum(dsv, axis=1, keepdims=True)
                dws_ref[h] += _dot(dsv_b, vn, NT)
                dvn = _dot(wm, dsv_b, TN)
                dng_ref[...] += _colsum(dvn * vhat)
                dnb_ref[...] += _colsum(dvn)
                dp_ref[rows, cv:cv + hd] = _ln_bwd(dvn * ng_ref[...], vhat, rstd).astype(BF16)

        row0 = i * tm
        for g, win in enumerate(POOL_WINDOWS):
            cx, cz, cd = 3072 + g * gd, 4096 + g * gd, 1024 + g * gd
            gs = slice(g * gd, (g + 1) * gd)
            halo = jnp.where(i > 0, halo_ref[:, gs].astype(F32), 0.0)
            xb = p_ref[:, cx:cx + gd].astype(F32)
            pooled = _pooled(xb, halo, row0, win).astype(BF16)
            scale_g = ps_ref[:, gs]
            yb = _dot(pooled, pw_ref[g], NN) + pb_ref[:, gs]
            zb, db = p_ref[:, cz:cz + gd].astype(F32), dm_ref[:, cd:cd + gd].astype(F32)
            sig = _sigmoid(zb)
            dyp = db * (zb * sig)
            dp_ref[:, cz:cz + gd] = (db * yb * scale_g * (sig * (1.0 + zb * (1.0 - sig)))).astype(BF16)
            dps_ref[:, gs] += _colsum(dyp * yb)
            dpb_ref[:, gs] += _colsum(dyp * scale_g)
            zb_ext = jnp.concatenate([zb, zbn_ref[:, gs].astype(F32)], axis=0)
            db_ext = jnp.concatenate([db, dbn_ref[:, gs].astype(F32)], axis=0)
            dy_ext = (db_ext * (zb_ext * _sigmoid(zb_ext)) * scale_g).astype(BF16)
            dpw_ref[g] += _dot(pooled, dy_ext[:tm], TN)
            dpooled = _dot(dy_ext, pw_ref[g], NT)
            t, inv = _inv_count(row0, tm + POOL_HALO, win)
            w_ext = jnp.where(t < s, dpooled * inv, 0.0)
            dp_ref[:, cx:cx + gd] = (_window_sum(w_ext, win, False)[:tm] - dpooled[:tm]).astype(BF16)

    hb = tm // POOL_HALO
    last = s // POOL_HALO - 1
    small = lambda shape: pl.BlockSpec(shape, lambda i: (0,) * len(shape))
    return _grid_call(
        body, name, (n_tiles,),
        [
            pl.BlockSpec((tm, 5120), lambda i: (i, 0)),
            pl.BlockSpec((POOL_HALO, 1024), lambda i: (jnp.maximum(i * hb - 1, 0), 3)),
            pl.BlockSpec((POOL_HALO, 1024), lambda i: (jnp.minimum((i + 1) * hb, last), 4)),
            pl.BlockSpec((tm, 2048), lambda i: (i, 0)),
            pl.BlockSpec((POOL_HALO, 1024), lambda i: (jnp.minimum((i + 1) * hb, last), 1)),
            small((GMLP_HEADS, GMLP_BLOCK, GMLP_BLOCK)), small((GMLP_HEADS, GMLP_BLOCK, 1)),
            small((1, hd)), small((1, hd)), small((4, gd, gd)), small((1, 1024)), small((1, 1024)),
        ],
        [
            pl.BlockSpec((tm, 5120), lambda i: (i, 0)),
            small((GMLP_HEADS, GMLP_BLOCK, GMLP_BLOCK)), small((GMLP_HEADS, GMLP_BLOCK, 1)),
            small((1, hd)), small((1, hd)), small((4, gd, gd)), small((1, 1024)), small((1, 1024)),
        ],
        [
            jax.ShapeDtypeStruct((s, 5120), BF16),
            jax.ShapeDtypeStruct((GMLP_HEADS, GMLP_BLOCK, GMLP_BLOCK), F32),
            jax.ShapeDtypeStruct((GMLP_HEADS, GMLP_BLOCK, 1), F32),
            jax.ShapeDtypeStruct((1, hd), F32), jax.ShapeDtypeStruct((1, hd), F32),
            jax.ShapeDtypeStruct((4, gd, gd), F32),
            jax.ShapeDtypeStruct((1, 1024), F32), jax.ShapeDtypeStruct((1, 1024), F32),
        ],
        [proj, proj, proj, dm, dm, ws, bs_col, ng, nb, pw, pb, ps], side=side)


def _rope_pair_swap(t):
    lane = lax.broadcasted_iota(jnp.int32, t.shape, 1)
    return jnp.where(lane % 64 < 32, pltpu.roll(t, 96, 1), pltpu.roll(t, 32, 1))


def _rms(x, g):
    r = lax.rsqrt(jnp.mean(x * x, axis=-1, keepdims=True) + LN_EPS)
    return x * r, r


def _rms_bwd(dy, g, xhat, r):
    dyg = dy * g
    return r * (dyg - xhat * jnp.mean(dyg * xhat, axis=-1, keepdims=True))


def _lane_lt(shape, n):
    return lax.broadcasted_iota(jnp.int32, shape, 1) < n


def _mla_prep(proj, cosp, sinp, gq, gkv, name):
    s = proj.shape[0]
    tm = min(s, 512)

    def body(qc_ref, kv_ref, c_ref, s_ref, gq_ref, gkv_ref, qn_ref, kp_ref):
        qhat, _ = _rms(qc_ref[...].astype(F32), None)
        qn_ref[...] = (qhat * gq_ref[...]).astype(BF16)
        khat, _ = _rms(kv_ref[:, 0:128].astype(F32), None)
        kp_ref[:, 0:128] = (khat * gkv_ref[...]).astype(BF16)
        kr = kv_ref[:, 128:256].astype(F32)
        kp_ref[:, 128:256] = (kr * c_ref[...] + _rope_pair_swap(kr) * s_ref[...]).astype(BF16)

    return pl.pallas_call(
        body, name=name, grid=(s // tm,),
        in_specs=[pl.BlockSpec((tm, 256), lambda i: (i, 0)), pl.BlockSpec((tm, 256), lambda i: (i, 1)),
                  pl.BlockSpec((tm, 128), lambda i: (i, 0)), pl.BlockSpec((tm, 128), lambda i: (i, 0)),
                  pl.BlockSpec((1, 256), lambda i: (0, 0)), pl.BlockSpec((1, 128), lambda i: (0, 0))],
        out_specs=[pl.BlockSpec((tm, 256), lambda i: (i, 0)), pl.BlockSpec((tm, 256), lambda i: (i, 0))],
        out_shape=[jax.ShapeDtypeStruct((s, 256), BF16), jax.ShapeDtypeStruct((s, 256), BF16)],
        compiler_params=_params(1),
    )(proj, proj, cosp, sinp, gq, gkv)


def _mla_prep_bwd(proj, dqn, dkp, dv, cosp, sinp, gq, gkv, dproj, name):
    s = proj.shape[0]
    tm = min(s, 512)

    def body(qc_ref, kv_ref, dqn_ref, dkp_ref, dv_ref, c_ref, s_ref, gq_ref, gkv_ref, dproj_ref, o_ref, red_ref):
        @pl.when(pl.program_id(0) == 0)
        def _():
            red_ref[...] = jnp.zeros_like(red_ref)

        qhat, qr = _rms(qc_ref[...].astype(F32), None)
        dq = dqn_ref[...]
        o_ref[:, 0:256] = _rms_bwd(dq, gq_ref[...], qhat, qr).astype(BF16)
        red_ref[0:1, :] += _colsum(dq * qhat)
        khat, kr = _rms(kv_ref[:, 0:128].astype(F32), None)
        dk = dkp_ref[:, 0:128] + dv_ref[...]
        o_ref[:, 256:384] = _rms_bwd(dk, gkv_ref[...], khat, kr).astype(BF16)
        red_ref[1:2, 0:128] += _colsum(dk * khat)
        dr = dkp_ref[:, 128:256]
        o_ref[:, 384:512] = (dr * c_ref[...] - _rope_pair_swap(dr) * s_ref[...]).astype(BF16)

    return pl.pallas_call(
        body, name=name, grid=(s // tm,),
        in_specs=[pl.BlockSpec((tm, 256), lambda i: (i, 0)), pl.BlockSpec((tm, 256), lambda i: (i, 1)),
                  pl.BlockSpec((tm, 256), lambda i: (i, 0)), pl.BlockSpec((tm, 256), lambda i: (i, 0)),
                  pl.BlockSpec((tm, 128), lambda i: (i, 0)),
                  pl.BlockSpec((tm, 128), lambda i: (i, 0)), pl.BlockSpec((tm, 128), lambda i: (i, 0)),
                  pl.BlockSpec((1, 256), lambda i: (0, 0)), pl.BlockSpec((1, 128), lambda i: (0, 0)),
                  pl.BlockSpec(memory_space=pl.ANY)],
        out_specs=[pl.BlockSpec((tm, 512), lambda i: (i, 0)), pl.BlockSpec((8, 256), lambda i: (0, 0))],
        out_shape=[jax.ShapeDtypeStruct(dproj.shape, BF16), jax.ShapeDtypeStruct((8, 256), F32)],
        input_output_aliases={9: 0}, compiler_params=_params(1),
    )(proj, proj, dqn, dkp, dv, cosp, sinp, gq, gkv, dproj)


HEADS_TM = 256
Z_COL0 = ODD_IN_PAD - MLA_WIDTH


def _head_cols(h):
    return slice(128 * h, 128 * h + 128)


def _q_heads(q_up, cosp, sinp, wuk, name):
    s = q_up.shape[0]
    tm = min(s, HEADS_TM)

    def body(q_ref, c_ref, s_ref, w_ref, o_ref):
        for p in range(MLA_HEADS // 2):
            raw = q_ref[:, MLA_WIDTH + 128 * p:MLA_WIDTH + 128 * (p + 1)].astype(F32)
            rot = raw * c_ref[...] + _rope_pair_swap(raw) * s_ref[...]
            low = _lane_lt(rot.shape, 64)
            o_ref[2 * p, :, 128:256] = jnp.where(low, rot, 0.0).astype(BF16)
            o_ref[2 * p + 1, :, 128:256] = jnp.where(low, pltpu.roll(rot, 64, 1), 0.0).astype(BF16)
        for h in range(MLA_HEADS):
            o_ref[h, :, 0:128] = _dot(q_ref[:, _head_cols(h)], w_ref[:, _head_cols(h)], NT).astype(BF16)

    return pl.pallas_call(
        body, name=name, grid=(s // tm,),
        in_specs=[pl.BlockSpec((tm, 3072), lambda i: (i, 0)),
                  pl.BlockSpec((tm, 128), lambda i: (i, 0)), pl.BlockSpec((tm, 128), lambda i: (i, 0)),
                  pl.BlockSpec((128, MLA_WIDTH), lambda i: (0, 0))],
        out_specs=pl.BlockSpec((MLA_HEADS, tm, 256), lambda i: (0, i, 0)),
        out_shape=jax.ShapeDtypeStruct((MLA_HEADS, s, 256), BF16), compiler_params=_params(1),
    )(q_up, cosp, sinp, wuk)


def _q_heads_bwd(dqp, q_up, cosp, sinp, wuk, name):
    s = q_up.shape[0]
    tm = min(s, HEADS_TM)

    def body(dq_ref, qn_ref, c_ref, s_ref, w_ref, dn_ref, dr_ref, dw_ref):
        @pl.when(pl.program_id(0) == 0)
        def _():
            dw_ref[...] = jnp.zeros_like(dw_ref)

        for h in range(MLA_HEADS):
            dlat = dq_ref[h, :, 0:128]
            dn_ref[:, _head_cols(h)] = _dot(dlat, w_ref[:, _head_cols(h)], NN).astype(BF16)
            dw_ref[:, _head_cols(h)] += _dot(dlat, qn_ref[:, _head_cols(h)], TN)
        for p in range(MLA_HEADS // 2):
            drot = dq_ref[2 * p, :, 128:256].astype(F32) + pltpu.roll(dq_ref[2 * p + 1, :, 128:256].astype(F32), 64, 1)
            dr_ref[:, _head_cols(p)] = (drot * c_ref[...] - _rope_pair_swap(drot) * s_ref[...]).astype(BF16)

    return pl.pallas_call(
        body, name=name, grid=(s // tm,),
        in_specs=[pl.BlockSpec((MLA_HEADS, tm, 256), lambda i: (0, i, 0)),
                  pl.BlockSpec((tm, MLA_WIDTH), lambda i: (i, 0)),
                  pl.BlockSpec((tm, 128), lambda i: (i, 0)), pl.BlockSpec((tm, 128), lambda i: (i, 0)),
                  pl.BlockSpec((128, MLA_WIDTH), lambda i: (0, 0))],
        out_specs=[pl.BlockSpec((tm, MLA_WIDTH), lambda i: (i, 0)),
                   pl.BlockSpec((tm, 1024), lambda i: (i, 0)),
                   pl.BlockSpec((128, MLA_WIDTH), lambda i: (0, 0))],
        out_shape=[jax.ShapeDtypeStruct((s, MLA_WIDTH), BF16), jax.ShapeDtypeStruct((s, 1024), BF16),
                   jax.ShapeDtypeStruct((128, MLA_WIDTH), F32)],
        compiler_params=_params(1),
    )(dqp, q_up, cosp, sinp, wuk)


def _o_gate(o_lat, proj, wuv, name):
    s = o_lat.shape[1]
    tm = min(s, HEADS_TM)

    def body(ol_ref, p_ref, w_ref, g_ref):
        for h in range(MLA_HEADS):
            z = p_ref[:, Z_COL0 + 128 * h:Z_COL0 + 128 * (h + 1)].astype(F32)
            g_ref[:, _head_cols(h)] = (_dot(ol_ref[h], w_ref[:, _head_cols(h)], NN) * (z * _sigmoid(z))).astype(BF16)

    return pl.pallas_call(
        body, name=name, grid=(s // tm,),
        in_specs=[pl.BlockSpec((MLA_HEADS, tm, 128), lambda i: (0, i, 0)),
                  pl.BlockSpec((tm, ODD_IN_PAD), lambda i: (i, 0)),
                  pl.BlockSpec((128, MLA_WIDTH), lambda i: (0, 0))],
        out_specs=pl.BlockSpec((tm, MLA_WIDTH), lambda i: (i, 0)),
        out_shape=jax.ShapeDtypeStruct((s, MLA_WIDTH), BF16), compiler_params=_params(1),
    )(o_lat, proj, wuv)


def _o_gate_bwd(dg, o_lat, proj, wuv, name):
    s = o_lat.shape[1]
    tm = min(s, HEADS_TM)

    def body(dg_ref, ol_ref, p_ref, w_ref, dp_ref, dol_ref, dw_ref):
        @pl.when(pl.program_id(0) == 0)
        def _():
            dw_ref[...] = jnp.zeros_like(dw_ref)

        dp_ref[:, 0:Z_COL0] = jnp.zeros((tm, Z_COL0), BF16)
        for h in range(MLA_HEADS):
            zc = slice(Z_COL0 + 128 * h, Z_COL0 + 128 * (h + 1))
            z, dgv, ol = p_ref[:, zc].astype(F32), dg_ref[:, _head_cols(h)].astype(F32), ol_ref[h]
            sig = _sigmoid(z)
            o = _dot(ol, w_ref[:, _head_cols(h)], NN)
            dp_ref[:, zc] = (dgv * o * (sig * (1.0 + z * (1.0 - sig)))).astype(BF16)
            do = (dgv * (z * sig)).astype(BF16)
            dol_ref[h] = _dot(do, w_ref[:, _head_cols(h)], NT).astype(BF16)
            dw_ref[:, _head_cols(h)] += _dot(ol, do, TN)

    return pl.pallas_call(
        body, name=name, grid=(s // tm,),
        in_specs=[pl.BlockSpec((tm, MLA_WIDTH), lambda i: (i, 0)),
                  pl.BlockSpec((MLA_HEADS, tm, 128), lambda i: (0, i, 0)),
                  pl.BlockSpec((tm, ODD_IN_PAD), lambda i: (i, 0)),
                  pl.BlockSpec((128, MLA_WIDTH), lambda i: (0, 0))],
        out_specs=[pl.BlockSpec((tm, ODD_IN_PAD), lambda i: (i, 0)),
                   pl.BlockSpec((MLA_HEADS, tm, 128), lambda i: (0, i, 0)),
                   pl.BlockSpec((128, MLA_WIDTH), lambda i: (0, 0))],
        out_shape=[jax.ShapeDtypeStruct((s, ODD_IN_PAD), BF16), jax.ShapeDtypeStruct((MLA_HEADS, s, 128), BF16),
                   jax.ShapeDtypeStruct((128, MLA_WIDTH), F32)],
        compiler_params=_params(1),
    )(dg, o_lat, proj, wuv)


ATT_TQ = CHUNK
ATT_ROWS = ATT_TQ * MLA_HEADS
ATT_TK = 512
ATT_HEAD_GROUP = 4


def _visible(k0, q_chunk, tk):
    kpos = k0 + lax.broadcasted_iota(jnp.int32, (1, tk), 1)
    return kpos // CHUNK <= q_chunk


def _tile_lanes(t, n):
    return jnp.concatenate([t] * (n // 128), axis=1)


def _key_blocks(i, tk, block, pairs=False):
    visible = i * ATT_TQ + ATT_TQ
    n_full = (visible + tk - 1) // tk - 1

    def full(j):
        block(pl.multiple_of(j * tk, tk), tk, False)

    if pairs:
        def two(jj, carry):
            full(2 * jj)
            full(2 * jj + 1)
            return carry

        lax.fori_loop(0, n_full // 2, two, 0)

        @pl.when(n_full % 2 == 1)
        def _():
            full(n_full - 1)
    else:
        def one(j, carry):
            full(j)
            return carry

        lax.fori_loop(0, n_full, one, 0)
    last0 = pl.multiple_of(n_full * tk, tk)
    half = tk // 2
    if half % 128 == 0:
        @pl.when(visible - n_full * tk <= half)
        def _():
            block(last0, half, True)

        @pl.when(visible - n_full * tk > half)
        def _():
            block(last0, tk, True)
    else:
        block(last0, tk, True)


def _attn_fwd(qp, kp, name, side=None):
    s = kp.shape[0]
    tk = min(ATT_TK, s)

    def body(q_ref, k_ref, o_ref, lse_ref, m_sc, acc_sc):
        i = pl.program_id(0)
        m_sc[...] = jnp.full_like(m_sc, -jnp.inf)
        acc_sc[...] = jnp.zeros_like(acc_sc)

        def block(k0, width, masked):
            k = k_ref[pl.ds(k0, width), :]
            v1 = jnp.where(_lane_lt(k.shape, 128), k, jnp.ones_like(k))
            for h0 in range(0, MLA_HEADS, ATT_HEAD_GROUP):
                rows = slice(h0 * ATT_TQ, (h0 + ATT_HEAD_GROUP) * ATT_TQ)
                q = q_ref[h0:h0 + ATT_HEAD_GROUP].reshape(ATT_HEAD_GROUP * ATT_TQ, 256)
                sc = _dot(q, k, NT) * ATTN_SCALE_LOG2
                if masked:
                    sc = jnp.where(_visible(k0, i, width), sc, -jnp.inf)
                m_prev = m_sc[rows]
                m_new = jnp.maximum(m_prev, jnp.max(sc, axis=1, keepdims=True))
                p = jnp.exp2(sc - _tile_lanes(m_new, width))
                acc_sc[rows] = _tile_lanes(jnp.exp2(m_prev - m_new), 256) * acc_sc[rows] + _dot(p, v1, NN)
                m_sc[rows] = m_new

        _key_blocks(i, tk, block, pairs=True)
        acc = acc_sc[...]
        l = acc[:, 128:256]
        o_ref[...] = (acc[:, 0:128] / l).astype(BF16).reshape(MLA_HEADS, ATT_TQ, 128)
        lse_ref[...] = (m_sc[...] + jnp.log2(l)).reshape(MLA_HEADS, ATT_TQ, 128)

    head128 = pl.BlockSpec((MLA_HEADS, ATT_TQ, 128), lambda i: (0, i, 0))
    return _grid_call(
        body, name, (s // ATT_TQ,),
        [pl.BlockSpec((MLA_HEADS, ATT_TQ, 256), lambda i: (0, i, 0)), pl.BlockSpec((s, 256), lambda i: (0, 0))],
        [head128, head128],
        [jax.ShapeDtypeStruct((MLA_HEADS, s, 128), BF16), jax.ShapeDtypeStruct((MLA_HEADS, s, 128), F32)],
        [qp, kp], scratch=[pltpu.VMEM((ATT_ROWS, 128), F32), pltpu.VMEM((ATT_ROWS, 256), F32)], side=side)


def _attn_bwd(qp, kp, o, do, lse, name, side=None):
    s = kp.shape[0]
    tk = min(ATT_TK, s)

    def body(q_ref, k_ref, o_ref, do_ref, lse_ref, dq_ref, dk_ref, dv_ref, dq_sc):
        i = pl.program_id(0)

        @pl.when(i == 0)
        def _():
            dk_ref[...] = jnp.zeros_like(dk_ref)
            dv_ref[...] = jnp.zeros_like(dv_ref)

        q = q_ref[...].reshape(ATT_ROWS, 256)
        dov = do_ref[...].reshape(ATT_ROWS, 128)
        delta = jnp.sum(dov.astype(F32) * o_ref[...].reshape(ATT_ROWS, 128).astype(F32), axis=1, keepdims=True)
        delta_t = _tile_lanes(jnp.broadcast_to(delta, (ATT_ROWS, 128)), tk)
        lse_t = _tile_lanes(lse_ref[...].reshape(ATT_ROWS, 128), tk)
        dq_sc[...] = jnp.zeros_like(dq_sc)

        def block(k0, width, masked):
            k = k_ref[pl.ds(k0, width), :]
            p = jnp.exp2(_dot(q, k, NT) * ATTN_SCALE_LOG2 - lse_t[:, 0:width])
            if masked:
                p = jnp.where(_visible(k0, i, width), p, 0.0)
            dv_ref[pl.ds(k0, width), :] += _dot(p, dov, TN)
            ds = (p * (_dot(dov, k[:, 0:128], NT) - delta_t[:, 0:width]) * ATTN_SCALE).astype(BF16)
            dq_sc[...] += _dot(ds, k, NN)
            dk_ref[pl.ds(k0, width), :] += _dot(ds, q, TN)

        _key_blocks(i, tk, block, pairs=True)
        dq_ref[...] = dq_sc[...].astype(BF16).reshape(MLA_HEADS, ATT_TQ, 256)

    head128 = pl.BlockSpec((MLA_HEADS, ATT_TQ, 128), lambda i: (0, i, 0))
    head256 = pl.BlockSpec((MLA_HEADS, ATT_TQ, 256), lambda i: (0, i, 0))
    return _grid_call(
        body, name, (s // ATT_TQ,),
        [head256, pl.BlockSpec((s, 256), lambda i: (0, 0)), head128, head128, head128],
        [head256, pl.BlockSpec((s, 256), lambda i: (0, 0)), pl.BlockSpec((s, 128), lambda i: (0, 0))],
        [jax.ShapeDtypeStruct((MLA_HEADS, s, 256), BF16),
         jax.ShapeDtypeStruct((s, 256), F32), jax.ShapeDtypeStruct((s, 128), F32)],
        [qp, kp, o, do, lse], scratch=[pltpu.VMEM((ATT_ROWS, 256), F32)], side=side)


def _place():
    x, y, c = lax.axis_index("x"), lax.axis_index("y"), lax.axis_index("c")
    return x, y, c, 4 * x + 2 * y + c


def _flip(x, y, c, r):
    px = 1 - x if r & 4 else x
    py = 1 - y if r & 2 else y
    pc = 1 - c if r & 1 else c
    return (px, py, pc), 4 * px + 2 * py + pc


def _adaln_exchange(c8, ada_w, ada_b_cols, blocks, name):
    d = c8.shape[1]
    w_cols = ada_w.shape[2]
    n_arr = len(blocks)

    def body(c_ref, w_ref, b_ref, *refs):
        x_refs, (call_ref, mod_ref), out_refs = refs[:n_arr], refs[n_arr:n_arr + 2], refs[n_arr + 2:2 * n_arr + 2]
        sbuf, rbuf, s1, r1, s2, r2 = refs[2 * n_arr + 2:2 * n_arr + 8]
        gather = _Gather(x_refs, out_refs, *refs[2 * n_arr + 8:])
        x, y, c, me = _place()
        call_ref[pl.ds(pl.multiple_of(me * 8, 8), 8), :] = c_ref[...]
        peers = [_flip(x, y, c, r) for r in range(1, N_DEV)]

        def c_copy(k, src_lin, to):
            rows = call_ref.at[pl.ds(pl.multiple_of(src_lin * 8, 8), 8), :]
            return pltpu.make_async_remote_copy(src_ref=rows, dst_ref=rows, send_sem=s1.at[k], recv_sem=r1.at[k],
                                                device_id=to, device_id_type=MESH)

        first = [c_copy(k, me, peer) for k, (peer, _) in enumerate(peers)]
        for cp in first:
            cp.start()
        for k, (_, lin) in enumerate(peers):
            c_copy(k, lin, (x, y, c)).wait_recv()
        for cp in first:
            cp.wait_send()

        for j in range(N_DEV):
            cj = call_ref[8 * j:8 * j + 8, :]
            cond = cj * _sigmoid(cj)
            for l in range(2):
                sbuf[j, l] = lax.dot_general(cond, w_ref[l], NN, precision=lax.Precision.HIGHEST,
                                             preferred_element_type=F32) + b_ref[l]

        def m_copy(k, src_slot, dst_slot, to):
            return pltpu.make_async_remote_copy(src_ref=sbuf.at[src_slot], dst_ref=rbuf.at[dst_slot],
                                                send_sem=s2.at[k], recv_sem=r2.at[k], device_id=to,
                                                device_id_type=MESH)

        rbuf[me] = sbuf[me]
        second = [m_copy(k, lin, me, peer) for k, (peer, lin) in enumerate(peers)]
        for cp in second:
            cp.start()
        gather.start()
        for k, (_, lin) in enumerate(peers):
            m_copy(k, lin, lin, (x, y, c)).wait_recv()
        for cp in second:
            cp.wait_send()
        for j in range(N_DEV):
            for l in range(2):
                mod_ref[l, :, w_cols * j:w_cols * (j + 1)] = rbuf[j, l]
        gather.forward()
        gather.end()

    vmem = pl.BlockSpec(memory_space=pltpu.VMEM)
    anyspace = pl.BlockSpec(memory_space=pl.ANY)
    g_shapes, g_sems = _gather_extras(blocks)
    return pl.pallas_call(
        body, name=name, in_specs=[vmem, vmem, vmem] + [anyspace] * n_arr, out_specs=[vmem, vmem] + [anyspace] * n_arr,
        out_shape=[jax.ShapeDtypeStruct((8 * N_DEV, d), F32), jax.ShapeDtypeStruct((2, 8, 3 * d), F32)] + g_shapes,
        scratch_shapes=[pltpu.VMEM((N_DEV, 2, 8, w_cols), F32), pltpu.VMEM((N_DEV, 2, 8, w_cols), F32),
                        pltpu.SemaphoreType.DMA((N_DEV - 1,)), pltpu.SemaphoreType.DMA((N_DEV - 1,)),
                        pltpu.SemaphoreType.DMA((N_DEV - 1,)), pltpu.SemaphoreType.DMA((N_DEV - 1,))] + g_sems,
        compiler_params=pltpu.CompilerParams(vmem_limit_bytes=VMEM_LIMIT),
    )(c8, ada_w, ada_b_cols, *blocks)


class _Gather:
    def __init__(self, x_refs, out_refs, send_sems, recv_sems, local_sems):
        x, y, c, _ = _place()
        self.me, self.sibling, self.c = (x, y, c), (x, y, 1 - c), c
        self.chips = [(1 - x, y), (x, 1 - y), (1 - x, 1 - y)]
        self.n_arr = len(x_refs)
        self.out_refs, self.send_sems, self.recv_sems = out_refs, send_sems, recv_sems
        self.mine = [pltpu.make_async_copy(x_refs[t], out_refs[t].at[4 * x + 2 * y + c], local_sems.at[t])
                     for t in range(self.n_arr)]
        self.first = []
        for t in range(self.n_arr):
            self.first.append(self.copy(t, 0, self.me, self.sibling, src=x_refs[t]))
            self.first += [self.copy(t, 1 + j, self.me, (*chip, c), src=x_refs[t]) for j, chip in enumerate(self.chips)]
        self.passed = [self.copy(t, 4 + j, (*chip, c), self.sibling)
                       for t in range(self.n_arr) for j, chip in enumerate(self.chips)]

    def copy(self, t, k, blk, to, src=None):
        slot = self.out_refs[t].at[4 * blk[0] + 2 * blk[1] + blk[2]]
        return pltpu.make_async_remote_copy(src_ref=slot if src is None else src, dst_ref=slot,
                                            send_sem=self.send_sems.at[7 * t + k], recv_sem=self.recv_sems.at[7 * t + k],
                                            device_id=to, device_id_type=MESH)

    def start(self):
        for cp in self.mine + self.first:
            cp.start()

    def forward(self):
        for t in range(self.n_arr):
            for j, chip in enumerate(self.chips):
                self.copy(t, 1 + j, (*chip, self.c), self.me).wait_recv()
                self.passed[3 * t + j].start()

    def end(self):
        for t in range(self.n_arr):
            self.copy(t, 0, self.sibling, self.me).wait_recv()
            for j, chip in enumerate(self.chips):
                self.copy(t, 4 + j, (*chip, 1 - self.c), self.me).wait_recv()
        for cp in self.first + self.passed:
            cp.wait_send()
        for cp in self.mine:
            cp.wait()


def _gather_extras(blocks):
    n_arr = len(blocks)
    return ([jax.ShapeDtypeStruct((N_DEV,) + b.shape, b.dtype) for b in blocks],
            [pltpu.SemaphoreType.DMA((7 * n_arr,)), pltpu.SemaphoreType.DMA((7 * n_arr,)),
             pltpu.SemaphoreType.DMA((n_arr,))])


def _all_gather(blocks, name):
    n_arr = len(blocks)

    def body(*refs):
        gather = _Gather(refs[:n_arr], refs[n_arr:2 * n_arr], *refs[2 * n_arr:])
        gather.start()
        gather.forward()
        gather.end()

    anyspace = pl.BlockSpec(memory_space=pl.ANY)
    shapes, sems = _gather_extras(blocks)
    return pl.pallas_call(body, name=name, in_specs=[anyspace] * n_arr, out_specs=[anyspace] * n_arr,
                          out_shape=shapes, scratch_shapes=sems)(*blocks)


def _scatter_parts(parts, name):
    n_arr = len(parts)

    def body(*refs):
        copies = _exchange_copies(refs[:n_arr], refs[n_arr:2 * n_arr], *refs[2 * n_arr:], "devices")
        _exchange_start(copies)
        _exchange_wait(copies)

    anyspace = pl.BlockSpec(memory_space=pl.ANY)
    shapes, sems = _exchange_extras(parts, "devices")
    return pl.pallas_call(body, name=name, in_specs=[anyspace] * n_arr, out_specs=[anyspace] * n_arr,
                          out_shape=shapes, scratch_shapes=sems)(*parts)


N_CHIPS = N_DEV // 2


def _sibling_swap(part, name):
    def body(g_ref, r_ref, send_sems, recv_sems):
        x, y, c, _ = _place()
        sends = [pltpu.make_async_remote_copy(
            src_ref=g_ref.at[2 * q + 1 - c], dst_ref=r_ref.at[q], send_sem=send_sems.at[q], recv_sem=recv_sems.at[q],
            device_id=(x, y, 1 - c), device_id_type=MESH) for q in range(N_CHIPS)]
        recvs = [pltpu.make_async_remote_copy(
            src_ref=g_ref.at[2 * q + c], dst_ref=r_ref.at[q], send_sem=send_sems.at[q], recv_sem=recv_sems.at[q],
            device_id=(x, y, c), device_id_type=MESH) for q in range(N_CHIPS)]
        for cp in sends:
            cp.start()
        for cp in recvs:
            cp.wait_recv()
        for cp in sends:
            cp.wait_send()

    anyspace = pl.BlockSpec(memory_space=pl.ANY)
    return pl.pallas_call(
        body, name=name, in_specs=[anyspace], out_specs=anyspace,
        out_shape=jax.ShapeDtypeStruct((N_CHIPS,) + part.shape[1:], part.dtype),
        scratch_shapes=[pltpu.SemaphoreType.DMA((N_CHIPS,)), pltpu.SemaphoreType.DMA((N_CHIPS,))])(part)


def _pair_sum(a, b, name):
    n, rows, cols = a.shape
    tr = max(t for t in range(16, 513, 16) if rows % t == 0)

    def body(a_ref, b_ref, o_ref):
        o_ref[...] = (a_ref[...].astype(F32) + b_ref[...].astype(F32)).astype(o_ref.dtype)

    blk = pl.BlockSpec((None, tr, cols), lambda q, i: (q, i, 0))
    return pl.pallas_call(body, name=name, grid=(n, rows // tr), in_specs=[blk, blk], out_specs=blk,
                          out_shape=jax.ShapeDtypeStruct(a.shape, a.dtype), compiler_params=_params(2))(a, b)


def _adamw(w, g, m, v):
    m = ADAM_B1 * m + (1.0 - ADAM_B1) * g
    v = ADAM_B2 * v + (1.0 - ADAM_B2) * (g * g)
    m_hat = m / (1.0 - ADAM_B1 ** ADAM_STEP)
    v_hat = v / (1.0 - ADAM_B2 ** ADAM_STEP)
    return -ADAM_LR * (m_hat / (jnp.sqrt(v_hat) + ADAM_EPS) + ADAM_WD * w), m, v


def _sum_parts_adamw(parts, w, m, v, name):
    n_parts, rows, cols = parts.shape
    tr = max(t for t in range(16, 129, 16) if rows % t == 0)

    def body(p_ref, w_ref, m_ref, v_ref, g_ref, d_ref, mo_ref, vo_ref):
        g = p_ref[0].astype(F32)
        for j in range(1, n_parts):
            g = g + p_ref[j].astype(F32)
        g_ref[...] = g
        d_ref[...], mo_ref[...], vo_ref[...] = _adamw(w_ref[...], g, m_ref[...], v_ref[...])

    row = _rows3(tr, cols)
    out = jax.ShapeDtypeStruct((1, rows, cols), F32)
    return pl.pallas_call(
        body, name=name, grid=(rows // tr,),
        in_specs=[pl.BlockSpec((n_parts, tr, cols), lambda i: (0, i, 0)), row, row, row],
        out_specs=[row, row, row, row], out_shape=[out, out, out, out], compiler_params=_params(1),
    )(parts, w, m, v)


def _sum_parts_adamw_whole(parts, w, m, v, name):
    def body(p_ref, w_ref, m_ref, v_ref, g_ref, d_ref, mo_ref, vo_ref):
        g = p_ref[0:1].astype(F32)
        for j in range(1, N_DEV):
            g = g + p_ref[j:j + 1].astype(F32)
        g_ref[...] = g
        d_ref[...], mo_ref[...], vo_ref[...] = _adamw(w_ref[...], g, m_ref[...], v_ref[...])

    out = jax.ShapeDtypeStruct(w.shape, F32)
    return pl.pallas_call(body, name=name, out_shape=[out] * 4,
                          compiler_params=pltpu.CompilerParams(vmem_limit_bytes=VMEM_LIMIT))(parts, w, m, v)


def _sum_parts(parts, name):
    def body(p_ref, g_ref):
        g = p_ref[0]
        for j in range(1, N_DEV):
            g = g + p_ref[j]
        g_ref[...] = g

    return pl.pallas_call(body, name=name, out_shape=jax.ShapeDtypeStruct(parts.shape[1:], F32),
                          compiler_params=pltpu.CompilerParams(vmem_limit_bytes=VMEM_LIMIT))(parts)


def _adamw_many(gs, ws, ms, vs, name):
    n = len(gs)

    def body(*refs):
        for k in range(n):
            g_ref, w_ref, m_ref, v_ref = (refs[q * n + k] for q in range(4))
            d_ref, mo_ref, vo_ref = (refs[(4 + q) * n + k] for q in range(3))
            d_ref[...], mo_ref[...], vo_ref[...] = _adamw(w_ref[...], g_ref[...], m_ref[...], v_ref[...])

    out = [jax.ShapeDtypeStruct(w.shape, F32) for w in ws]
    res = pl.pallas_call(body, name=name, out_shape=out * 3,
                         compiler_params=pltpu.CompilerParams(vmem_limit_bytes=VMEM_LIMIT))(*gs, *ws, *ms, *vs)
    return res[:n], res[n:2 * n], res[2 * n:]


def _ada_w_grad_adamw(c_all, dmod_rows, w, m, v, name):
    def body(c_ref, dm_ref, w_ref, m_ref, v_ref, g_ref, d_ref, mo_ref, vo_ref):
        cv = c_ref[...]
        cond = cv * _sigmoid(cv)
        for l in range(2):
            g = lax.dot_general(cond, dm_ref[l], TN, precision=lax.Precision.HIGHEST, preferred_element_type=F32)
            g_ref[l] = g
            d_ref[l], mo_ref[l], vo_ref[l] = _adamw(w_ref[l], g, m_ref[l], v_ref[l])

    out = jax.ShapeDtypeStruct(w.shape, F32)
    return pl.pallas_call(
        body, name=name, out_shape=[out] * 4, compiler_params=pltpu.CompilerParams(vmem_limit_bytes=VMEM_LIMIT),
    )(c_all, dmod_rows, w, m, v)


REPLICATED = ("ln_g", "ln_b", "gmlp_norm_g", "gmlp_norm_b", "gmlp_ws", "gmlp_bs", "pool_b", "pool_scale",
              "mla_kv_norm_g", "mla_w_uk", "mla_w_uv")
CHUNK_ROWS, ADA_ROW, QNORM_ROW, LOSS_ROW, REP_ROWS = 73, 73, 74, 75, 80
UQ_ROWS, POOLW_ROWS = 96, 32


def _pad_rows(flat2d, rows):
    n, k = flat2d.shape
    return jnp.pad(flat2d, ((0, 0), (0, rows * LANES - k))).reshape(n, rows, LANES)


def _ada_cols_rows(vec):
    return _pad_rows(vec.reshape(2, N_DEV, -1).transpose(1, 0, 2).reshape(N_DEV, -1), 1)


def _unpack_replicated(rep, shapes):
    chunk = sum(s[1] for s in shapes) // N_DEV
    flat, off, out = rep[:, :CHUNK_ROWS].reshape(N_DEV, -1)[:, :chunk].reshape(-1), 0, {}
    for n, size, shape in shapes:
        out[n] = flat[off:off + size].reshape(shape)
        off += size
    cols = 3 * D_MODEL // N_DEV
    out["ada_b"] = rep[:, ADA_ROW, :2 * cols].reshape(N_DEV, 2, cols).transpose(1, 0, 2).reshape(2, -1)
    return out


def kernel(x, c, positions, ada_w, ada_b, ln_g, ln_b, e_w_in, gmlp_norm_g, gmlp_norm_b, gmlp_ws, gmlp_bs, pool_w, pool_b, pool_scale, e_w_out, o_w_in, mla_q_norm_g, mla_kv_norm_g, mla_w_uq, mla_w_uk, mla_w_uv, o_w_out, loss_target, m_ada_w, m_ada_b, m_ln_g, m_ln_b, m_e_w_in, m_gmlp_norm_g, m_gmlp_norm_b, m_gmlp_ws, m_gmlp_bs, m_pool_w, m_pool_b, m_pool_scale, m_e_w_out, m_o_w_in, m_mla_q_norm_g, m_mla_kv_norm_g, m_mla_w_uq, m_mla_w_uk, m_mla_w_uv, m_o_w_out, v_ada_w, v_ada_b, v_ln_g, v_ln_b, v_e_w_in, v_gmlp_norm_g, v_gmlp_norm_b, v_gmlp_ws, v_gmlp_bs, v_pool_w, v_pool_b, v_pool_scale, v_e_w_out, v_o_w_in, v_mla_q_norm_g, v_mla_kv_norm_g, v_mla_w_uq, v_mla_w_uk, v_mla_w_uv, v_o_w_out):
    w_in = dict(ada_w=ada_w, ada_b=ada_b, ln_g=ln_g, ln_b=ln_b, e_w_in=e_w_in, gmlp_norm_g=gmlp_norm_g,
                gmlp_norm_b=gmlp_norm_b, gmlp_ws=gmlp_ws, gmlp_bs=gmlp_bs, pool_w=pool_w, pool_b=pool_b,
                pool_scale=pool_scale, e_w_out=e_w_out, o_w_in=o_w_in, mla_q_norm_g=mla_q_norm_g,
                mla_kv_norm_g=mla_kv_norm_g, mla_w_uq=mla_w_uq, mla_w_uk=mla_w_uk, mla_w_uv=mla_w_uv, o_w_out=o_w_out)
    m_in = dict(ada_w=m_ada_w, ada_b=m_ada_b, ln_g=m_ln_g, ln_b=m_ln_b, e_w_in=m_e_w_in, gmlp_norm_g=m_gmlp_norm_g,
                gmlp_norm_b=m_gmlp_norm_b, gmlp_ws=m_gmlp_ws, gmlp_bs=m_gmlp_bs, pool_w=m_pool_w, pool_b=m_pool_b,
                pool_scale=m_pool_scale, e_w_out=m_e_w_out, o_w_in=m_o_w_in, mla_q_norm_g=m_mla_q_norm_g,
                mla_kv_norm_g=m_mla_kv_norm_g, mla_w_uq=m_mla_w_uq, mla_w_uk=m_mla_w_uk, mla_w_uv=m_mla_w_uv,
                o_w_out=m_o_w_out)
    v_in = dict(ada_w=v_ada_w, ada_b=v_ada_b, ln_g=v_ln_g, ln_b=v_ln_b, e_w_in=v_e_w_in, gmlp_norm_g=v_gmlp_norm_g,
                gmlp_norm_b=v_gmlp_norm_b, gmlp_ws=v_gmlp_ws, gmlp_bs=v_gmlp_bs, pool_w=v_pool_w, pool_b=v_pool_b,
                pool_scale=v_pool_scale, e_w_out=v_e_w_out, o_w_in=v_o_w_in, mla_q_norm_g=v_mla_q_norm_g,
                mla_kv_norm_g=v_mla_kv_norm_g, mla_w_uq=v_mla_w_uq, mla_w_uk=v_mla_w_uk, mla_w_uv=v_mla_w_uv,
                o_w_out=v_o_w_out)
    names = list(w_in)
    seq = x.shape[1]
    d = D_MODEL
    me = 4 * lax.axis_index("x") + 2 * lax.axis_index("y") + lax.axis_index("c")
    ada_cols = ada_w.shape[2]

    ada_b_cols = lax.dynamic_slice_in_dim(ada_b, me * ada_cols, ada_cols, axis=1)
    slab_row = lax.broadcasted_iota(jnp.int32, (8, d), 0)
    slab = jnp.where(slab_row == 0, c, jnp.where(slab_row == 1, jnp.pad(mla_q_norm_g, ((0, 0), (0, d - 32))), 0.0))
    c_all, mod, w_in_e3, pool_w3 = _adaln_exchange(
        slab, ada_w, jnp.broadcast_to(ada_b_cols[:, None, :], (2, 8, ada_cols)),
        [e_w_in[0].astype(BF16), pool_w.astype(BF16).reshape(POOLW_ROWS, LANES)], "adaln_exchange")
    h0 = _modulate(x, mod[0], "modulate0")
    proj0, o_in3 = _matmul_cols_nn(h0, w_in_e3, BF16, 512, "even_in", side=([o_w_in[0].astype(BF16)], "gather"))
    o_in_full = o_in3.transpose(1, 0, 2).reshape(d, ODD_IN)
    w_in_o = jnp.concatenate([o_in_full[:, :448], jnp.zeros((d, 64), BF16), o_in_full[:, 448:]], axis=1)
    pool_w_full = pool_w3.reshape(N_DEV, 4, 32, 256).transpose(1, 0, 2, 3).reshape(4, 256, 256)
    g_q = c_all.reshape(N_DEV, 8, d)[:, 1, :32].reshape(1, MLA_Q_RANK)

    ws, bs_col = gmlp_ws[0], gmlp_bs[0].reshape(GMLP_HEADS, GMLP_BLOCK, 1)
    wuk2, wuv2 = mla_w_uk[0].reshape(MLA_KV_RANK, -1), mla_w_uv[0].reshape(MLA_KV_RANK, -1)
    inv = 1.0 / (ROPE_THETA ** (jnp.arange(0, MLA_ROPE, 2, dtype=F32) / MLA_ROPE))
    ang = positions[0].astype(F32)[:, None] * inv
    cosp = jnp.tile(jnp.cos(ang), (1, 4))
    sinp = jnp.tile(jnp.concatenate([-jnp.sin(ang), jnp.sin(ang)], axis=1), (1, 2))

    mix0, w_out_e3 = _even_fwd(proj0, ws, bs_col, gmlp_norm_g, gmlp_norm_b, pool_w_full, pool_b, pool_scale, "even_mix",
                               side=([e_w_out[0].astype(BF16)], "gather"))
    w_out_e = w_out_e3.reshape(-1, d)
    y0, uq3 = _matmul([(mix0, w_out_e)], "nn", F32, seq, d, 512, 1024, "even_out",
                      side=([mla_w_uq.astype(BF16).reshape(UQ_ROWS, LANES)], "gather"))
    uq_full = uq3.reshape(MLA_Q_RANK, MLA_HEADS, MLA_NOPE + MLA_ROPE)
    w_uq_n = uq_full[:, :, :MLA_NOPE].reshape(MLA_Q_RANK, -1)
    w_uq_r = uq_full[:, :, MLA_NOPE:].reshape(MLA_Q_RANK, -1)
    w_uq = jnp.concatenate([w_uq_n, w_uq_r], axis=1)
    x1, h1 = _resid_ln(x, y0, mod[0], ln_g[0:1], ln_b[0:1], mod[1], "resid_ln0")

    (proj1,) = _matmul([(h1, w_in_o)], "nn", BF16, seq, ODD_IN_PAD, 512, ODD_IN_PAD, "odd_in")
    qn, kp = _mla_prep(proj1, cosp, sinp, g_q, mla_kv_norm_g, "mla_prep")
    (q_up,) = _matmul([(qn, w_uq)], "nn", BF16, seq, 3072, 512, 3072, "q_up")
    qp = _q_heads(q_up, cosp, sinp, wuk2, "q_heads")
    o_lat, lse, w_out_o3 = _attn_fwd(qp, kp, "attn_fwd", side=([o_w_out[0].astype(BF16)], "gather"))
    w_out_o = w_out_o3.reshape(-1, d)
    gated = _o_gate(o_lat, proj1, wuv2, "o_gate")
    (y1,) = _matmul([(gated, w_out_o)], "nn", F32, seq, d, 512, 1024, "odd_out")

    dy1, dxres1, red2 = _final_ln_loss_bwd(x1, y1, mod[1], ln_g[1:2], ln_b[1:2], loss_target, "final_ln_loss")
    (dgated,) = _matmul([(dy1, w_out_o)], "nt", BF16, seq, MLA_WIDTH, 512, MLA_WIDTH, "odd_out_dx")
    (g_w_out_o,) = _matmul([(gated, dy1)], "tn", BF16, MLA_WIDTH, d, 256, d, "odd_out_dw")
    dproj1_z, do_lat, g_wuv = _o_gate_bwd(dgated, o_lat, proj1, wuv2, "o_gate_bwd")
    dqp, dkp, dvv, r_o_out = _attn_bwd(qp, kp, o_lat, do_lat, lse, "attn_bwd",
                                       side=([g_w_out_o.reshape(N_DEV, -1, d)], "devices"))
    dq_nope, dq_rope, g_wuk = _q_heads_bwd(dqp, q_up, cosp, sinp, wuk2, "q_heads_bwd")
    (dqn,) = _matmul([(dq_nope, w_uq_n), (dq_rope, w_uq_r)], "nt", F32, seq, MLA_Q_RANK, 512, 256, "q_up_dx")
    (g_wuq_n,) = _matmul([(qn, dq_nope)], "tn", F32, MLA_Q_RANK, MLA_WIDTH, 256, MLA_WIDTH, "q_up_dw_nope")
    (g_wuq_r,) = _matmul([(qn, dq_rope)], "tn", F32, MLA_Q_RANK, 1024, 256, 1024, "q_up_dw_rope")
    dproj1, red_mla = _mla_prep_bwd(proj1, dqn, dkp, dvv, cosp, sinp, g_q, mla_kv_norm_g, dproj1_z, "mla_prep_bwd")
    (dh1,) = _matmul([(dproj1, w_in_o)], "nt", F32, seq, d, 512, d, "odd_in_dx")
    part_uq = jnp.concatenate([g_wuq_n.reshape(MLA_Q_RANK, MLA_HEADS, MLA_NOPE),
                               g_wuq_r.reshape(MLA_Q_RANK, MLA_HEADS, MLA_ROPE)], axis=2).astype(BF16).reshape(
                                   (N_DEV,) + mla_w_uq.shape[1:])
    (g_w_in_o,) = _matmul([(h1, dproj1)], "tn", BF16, d, ODD_IN_PAD, 256, ODD_IN_PAD // 2, "odd_in_dw", n_outer=True)
    part_o_in = jnp.concatenate([g_w_in_o[:, :448], g_w_in_o[:, 512:]], axis=1).reshape(d, N_DEV, -1).transpose(1, 0, 2)
    dy0, dxres0, red1 = _mid_bwd(dh1, dxres1, x, y0, mod[0], mod[1], ln_g[0:1], ln_b[0:1], "mid_bwd")
    dmix, r_uq = _matmul([(dy0, w_out_e)], "nt", BF16, seq, 2048, 512, 2048, "even_out_dx", side=([part_uq], "devices"))
    (g_w_out_e,) = _matmul([(mix0, dy0)], "tn", BF16, 2048, d, 256, d, "even_out_dw")
    dproj0, g_ws, g_bs, g_ng, g_nb, g_pw, g_pb, g_ps, r_o_in = _even_bwd(
        proj0, dmix, ws, bs_col, gmlp_norm_g, gmlp_norm_b, pool_w_full, pool_b, pool_scale, "even_mix_bwd",
        side=([part_o_in], "devices"))
    part_pw = g_pw.reshape(4, N_DEV, 32, 256).transpose(1, 0, 2, 3)
    part_e_in, r_e_out, r_pw = _matmul_cols_tn(h0, dproj0, w_in_e3.shape[2], BF16, 512, "even_in_dw",
                                               side=([g_w_out_e.reshape(N_DEV, -1, d), part_pw], "devices"))
    mine = lax.dynamic_index_in_dim(part_e_in.reshape((N_CHIPS, 2) + part_e_in.shape[1:]), lax.axis_index("c"), 1, False)
    chip_e_in = _pair_sum(mine, _sibling_swap(part_e_in, "e_in_sibling_swap"), "e_in_pair_sum")
    dh0, r_e_in = _matmul_cols_nt(dproj0, w_in_e3, F32, 512, "even_in_dx", side=([chip_e_in], "chips"))
    grad_x, red0 = _first_bwd(dh0, dxres0, x, mod[0], "first_bwd")

    t_mask = lax.broadcasted_iota(jnp.int32, (GMLP_BLOCK, GMLP_BLOCK), 0) // CHUNK
    s_mask = lax.broadcasted_iota(jnp.int32, (GMLP_BLOCK, GMLP_BLOCK), 1) // CHUNK
    part = {
        "ln_g": jnp.stack([red1[2], red2[0]]), "ln_b": jnp.stack([red1[3], red2[1]]),
        "gmlp_norm_g": g_ng, "gmlp_norm_b": g_nb,
        "gmlp_ws": jnp.where(s_mask <= t_mask, g_ws, 0.0), "gmlp_bs": g_bs,
        "pool_b": g_pb, "pool_scale": g_ps, "mla_kv_norm_g": red_mla[1, :MLA_KV_RANK],
        "mla_w_uk": g_wuk, "mla_w_uv": g_wuv,
    }
    dmod = jnp.stack([jnp.concatenate([red0[1], red0[0], red1[4]]),
                      jnp.concatenate([red1[1], red1[0], red2[2]])])

    loss_row = jnp.pad(jnp.broadcast_to((0.5 / d * jnp.sum(red2[3])).reshape(1, 1, 1), (N_DEV, 1, 1)),
                       ((0, 0), (0, 0), (0, LANES - 1)))
    part_small = jnp.concatenate([
        _pad_rows(jnp.concatenate([part[n].reshape(-1) for n in REPLICATED]).reshape(N_DEV, -1), CHUNK_ROWS),
        jnp.pad(jnp.concatenate([_ada_cols_rows(dmod), _pad_rows(red_mla[0].reshape(N_DEV, -1), 1), loss_row], axis=1),
                ((0, 0), (0, REP_ROWS - LOSS_ROW - 1), (0, 0)))], axis=1)
    (r_small,) = _scatter_parts([part_small], "grad_scatter")

    res = {"e_w_in": _sum_parts_adamw(r_e_in, e_w_in, m_e_w_in, v_e_w_in, "adamw_e_w_in"),
           "o_w_in": _sum_parts_adamw(r_o_in, o_w_in, m_o_w_in, v_o_w_in, "adamw_o_w_in"),
           "e_w_out": _sum_parts_adamw(r_e_out, e_w_out, m_e_w_out, v_e_w_out, "adamw_e_w_out"),
           "o_w_out": _sum_parts_adamw(r_o_out, o_w_out, m_o_w_out, v_o_w_out, "adamw_o_w_out"),
           "mla_w_uq": _sum_parts_adamw_whole(r_uq, mla_w_uq, m_mla_w_uq, v_mla_w_uq, "adamw_w_uq"),
           "pool_w": _sum_parts_adamw_whole(r_pw, pool_w, m_pool_w, v_pool_w, "adamw_pool_w")}
    small_sum = _sum_parts(r_small, "small_sum")
    loss = small_sum[LOSS_ROW, 0]
    (rep_sum,) = _all_gather([small_sum], "replicated_gather")
    grads = _unpack_replicated(rep_sum, [(n, w_in[n].size, w_in[n].shape) for n in REPLICATED])
    grads["mla_q_norm_g"] = small_sum[QNORM_ROW:QNORM_ROW + 1, :32]
    small_names = list(grads)
    deltas, new_ms, new_vs = _adamw_many([grads[n] for n in small_names], [w_in[n] for n in small_names],
                                         [m_in[n] for n in small_names], [v_in[n] for n in small_names], "small_adamw")
    for k, n in enumerate(small_names):
        res[n] = [grads[n], deltas[k], new_ms[k], new_vs[k]]
    dmod_all = r_small[:, ADA_ROW, :2 * ada_cols].reshape(N_DEV, 2, ada_cols).transpose(1, 0, 2)
    dmod_rows = jnp.pad(dmod_all[:, :, None, :], ((0, 0), (0, 0), (0, 7), (0, 0))).reshape(2, 8 * N_DEV, ada_cols)
    res["ada_w"] = _ada_w_grad_adamw(c_all, dmod_rows, ada_w, m_ada_w, v_ada_w, "ada_w_adamw")

    return (loss, grad_x, *[res[n][0] for n in names], *[res[n][1] for n in names],
            *[res[n][2] for n in names], *[res[n][3] for n in names])
```

```python
import functools

import jax
import jax.numpy as jnp
from jax import lax
from jax.experimental import pallas as pl
from jax.experimental.pallas import tpu as pltpu

F32 = jnp.float32
BF16 = jnp.bfloat16

D_MODEL = 1024
CHUNK = 64
LN_EPS = 1e-5
GMLP_HEADS = 4
GMLP_HEAD_DIM = 256
GMLP_BLOCK = 128
POOL_WINDOWS = (2, 4, 8, 16)
POOL_GROUP_DIM = 256
POOL_HALO = 16
MLA_HEADS = 16
MLA_NOPE = 128
MLA_ROPE = 64
MLA_Q_RANK = 256
MLA_KV_RANK = 128
MLA_WIDTH = 2048
ODD_IN = 2496
ODD_IN_PAD = 2560
ROPE_THETA = 10000.0
ATTN_SCALE = (MLA_NOPE + MLA_ROPE) ** -0.5
ATTN_SCALE_LOG2 = ATTN_SCALE * 1.4426950408889634
DEEPNORM_ALPHA = 4.0 ** 0.25
ADAM_LR, ADAM_B1, ADAM_B2, ADAM_EPS, ADAM_WD, ADAM_STEP = 0.001, 0.9, 0.999, 1e-8, 0.01, 10
N_DEV = 8
LANES = 1024
VMEM_LIMIT = 56 * 1024 * 1024
MESH = pl.DeviceIdType.MESH

NT = (((1,), (1,)), ((), ()))
NN = (((1,), (0,)), ((), ()))
TN = (((0,), (0,)), ((), ()))


def _params(n_axes):
    return pltpu.CompilerParams(dimension_semantics=("arbitrary",) * n_axes, vmem_limit_bytes=VMEM_LIMIT)


def _dot(a, b, dn):
    return lax.dot_general(a.astype(BF16), b.astype(BF16), dn, preferred_element_type=F32)


def _sigmoid(z):
    return 1.0 / (1.0 + jnp.exp(-z))


def _colsum(t):
    return jnp.sum(t, axis=0, keepdims=True)


EXCHANGE_RELATIONS = {"devices": tuple(range(1, N_DEV)), "chips": (2, 4, 6)}


def _exchange_copies(g_refs, r_refs, send_sems, recv_sems, local_sems, kind):
    x, y, c, me = _place()
    n_arr = len(g_refs)

    def slot(lin):
        return lin // 2 if kind == "chips" else lin

    own = [pltpu.make_async_copy(g_refs[t].at[slot(me)], r_refs[t].at[slot(me)], local_sems.at[t]) for t in range(n_arr)]
    sends, recvs = [], []
    for n, r in enumerate(EXCHANGE_RELATIONS[kind]):
        peer, lin = _flip(x, y, c, r)
        for t in range(n_arr):
            k = n_arr * n + t
            sends.append(pltpu.make_async_remote_copy(
                src_ref=g_refs[t].at[slot(lin)], dst_ref=r_refs[t].at[slot(me)], send_sem=send_sems.at[k],
                recv_sem=recv_sems.at[k], device_id=peer, device_id_type=MESH))
            recvs.append(pltpu.make_async_remote_copy(
                src_ref=g_refs[t].at[slot(lin)], dst_ref=r_refs[t].at[slot(lin)], send_sem=send_sems.at[k],
                recv_sem=recv_sems.at[k], device_id=(x, y, c), device_id_type=MESH))
    return own, sends, recvs


def _exchange_start(copies):
    own, sends, _ = copies
    for cp in own + sends:
        cp.start()


def _exchange_wait(copies):
    own, sends, recvs = copies
    for cp in recvs:
        cp.wait_recv()
    for cp in sends:
        cp.wait_send()
    for cp in own:
        cp.wait()


def _exchange_extras(parts, kind):
    shapes = [jax.ShapeDtypeStruct(p.shape, p.dtype) for p in parts]
    n = len(parts) * len(EXCHANGE_RELATIONS[kind])
    return shapes, [pltpu.SemaphoreType.DMA((n,)), pltpu.SemaphoreType.DMA((n,)), pltpu.SemaphoreType.DMA((len(parts),))]


def _grid_call(body, name, grid, in_specs, out_specs, out_shape, args, scratch=(), side=None):
    if side is None:
        return pl.pallas_call(body, name=name, grid=grid, in_specs=in_specs, out_specs=out_specs,
                              out_shape=out_shape, scratch_shapes=list(scratch),
                              compiler_params=_params(len(grid)))(*args)
    parts, kind = side
    gather = kind == "gather"
    n_in, n_out, n_sc, n_arr = len(args), len(out_shape), len(scratch), len(parts)
    side_shapes, side_sems = _gather_extras(parts) if gather else _exchange_extras(parts, kind)
    mid = tuple(g // 2 for g in grid)

    def wrapped(*refs):
        ins, g_refs = refs[:n_in], refs[n_in:n_in + n_arr]
        outs = refs[n_in + n_arr:n_in + n_arr + n_out]
        r_refs = refs[n_in + n_arr + n_out:n_in + 2 * n_arr + n_out]
        sc = refs[n_in + 2 * n_arr + n_out:n_in + 2 * n_arr + n_out + n_sc]
        ids = [pl.program_id(a) for a in range(len(grid))]

        def at(step):
            return functools.reduce(jnp.logical_and, [i == s for i, s in zip(ids, step)])

        first, last = at((0,) * len(grid)), at(tuple(g - 1 for g in grid))
        if gather:
            exchange = _Gather(g_refs, r_refs, *refs[-3:])
            pl.when(first)(exchange.start)
            if mid != (0,) * len(grid):
                pl.when(at(mid))(exchange.forward)
            body(*ins, *outs, *sc)

            @pl.when(last)
            def _():
                if mid == (0,) * len(grid):
                    exchange.forward()
                exchange.end()
        else:
            copies = _exchange_copies(g_refs, r_refs, *refs[-3:], kind)
            pl.when(first)(lambda: _exchange_start(copies))
            body(*ins, *outs, *sc)
            pl.when(last)(lambda: _exchange_wait(copies))

    anyspace = pl.BlockSpec(memory_space=pl.ANY)
    return pl.pallas_call(
        wrapped, name=name, grid=grid, in_specs=list(in_specs) + [anyspace] * n_arr,
        out_specs=list(out_specs) + [anyspace] * n_arr, out_shape=list(out_shape) + side_shapes,
        scratch_shapes=list(scratch) + side_sems, compiler_params=_params(len(grid)),
    )(*args, *parts)


def _matmul(pairs, mode, out_dtype, m, n, tm, tn, name, side=None, n_outer=False):
    dn = {"nn": NN, "nt": NT, "tn": TN}[mode]
    tm, tn = min(tm, m), min(tn, n)
    n_pairs = len(pairs)
    grid = (n // tn, m // tm) if n_outer else (m // tm, n // tn)

    def ij(f):
        return (lambda j, i: f(i, j)) if n_outer else f

    def body(*refs):
        o_ref = refs[-1]
        acc = None
        for p in range(n_pairs):
            t = _dot(refs[2 * p][...], refs[2 * p + 1][...], dn)
            acc = t if acc is None else acc + t
        o_ref[...] = acc.astype(o_ref.dtype)

    in_specs, args = [], []
    for a, b in pairs:
        if mode == "nn":
            k = a.shape[1]
            in_specs += [pl.BlockSpec((tm, k), ij(lambda i, j: (i, 0))), pl.BlockSpec((k, tn), ij(lambda i, j: (0, j)))]
        elif mode == "nt":
            k = a.shape[1]
            in_specs += [pl.BlockSpec((tm, k), ij(lambda i, j: (i, 0))), pl.BlockSpec((tn, k), ij(lambda i, j: (j, 0)))]
        else:
            k = a.shape[0]
            in_specs += [pl.BlockSpec((k, tm), ij(lambda i, j: (0, i))), pl.BlockSpec((k, tn), ij(lambda i, j: (0, j)))]
        args += [a, b]
    return _grid_call(body, name, grid, in_specs, [pl.BlockSpec((tm, tn), ij(lambda i, j: (i, j)))],
                      [jax.ShapeDtypeStruct((m, n), out_dtype)], args, side=side)


def _matmul_cols_nn(a, w3, out_dtype, tm, name, side=None):
    m, k = a.shape
    _, _, n = w3.shape
    tm = min(tm, m)

    def body(a_ref, w_ref, o_ref):
        av = a_ref[...]
        for j in range(N_DEV):
            o_ref[:, n * j:n * (j + 1)] = _dot(av, w_ref[j], NN).astype(o_ref.dtype)

    return _grid_call(
        body, name, (m // tm,),
        [pl.BlockSpec((tm, k), lambda i: (i, 0)), pl.BlockSpec((N_DEV, k, n), lambda i: (0, 0, 0))],
        [pl.BlockSpec((tm, N_DEV * n), lambda i: (i, 0))], [jax.ShapeDtypeStruct((m, N_DEV * n), out_dtype)], [a, w3],
        side=side)


def _matmul_cols_tn(a, b, n, out_dtype, tk, name, side=None):
    m, k = a.shape
    tk = min(tk, k)

    def body(a_ref, b_ref, o_ref):
        o_ref[...] = _dot(a_ref[...], b_ref[...], TN).astype(o_ref.dtype)

    return _grid_call(
        body, name, (N_DEV, k // tk),
        [pl.BlockSpec((m, tk), lambda j, i: (0, i)), pl.BlockSpec((m, n), lambda j, i: (0, j))],
        [pl.BlockSpec((None, tk, n), lambda j, i: (j, i, 0))], [jax.ShapeDtypeStruct((N_DEV, k, n), out_dtype)], [a, b],
        side=side)


def _rows3(tm, d):
    return pl.BlockSpec((None, tm, d), lambda i: (0, i, 0))


def _modulate(x, mod, name):
    _, s, d = x.shape
    tm = min(s, 512)

    def body(x_ref, m_ref, h_ref):
        shift, scale = m_ref[0:1, 0:d], m_ref[0:1, d:2 * d]
        h_ref[...] = (x_ref[...] * (1.0 + scale) + shift).astype(BF16)

    return pl.pallas_call(
        body, name=name, grid=(s // tm,),
        in_specs=[_rows3(tm, d), pl.BlockSpec((8, 3 * d), lambda i: (0, 0))],
        out_specs=pl.BlockSpec((tm, d), lambda i: (i, 0)),
        out_shape=jax.ShapeDtypeStruct((s, d), BF16), compiler_params=_params(1),
    )(x, mod)


def _ln_stats(r):
    mu = jnp.mean(r, axis=-1, keepdims=True)
    rc = r - mu
    var = jnp.mean(rc * rc, axis=-1, keepdims=True)
    rstd = lax.rsqrt(var + LN_EPS)
    return rc * rstd, rstd


def _ln_bwd(dxhat, xhat, rstd):
    return rstd * (dxhat - jnp.mean(dxhat, axis=-1, keepdims=True)
                   - xhat * jnp.mean(dxhat * xhat, axis=-1, keepdims=True))


def _resid_ln(x, y, mod, g, b, mod_next, name):
    _, s, d = x.shape
    tm = min(s, 512)

    def body(x_ref, y_ref, m_ref, g_ref, b_ref, mn_ref, o_ref, h_ref):
        gate = m_ref[0:1, 2 * d:3 * d]
        xhat, _ = _ln_stats(DEEPNORM_ALPHA * x_ref[...] + (1.0 + gate) * y_ref[...])
        out = xhat * g_ref[...] + b_ref[...]
        o_ref[...] = out
        h_ref[...] = (out * (1.0 + mn_ref[0:1, d:2 * d]) + mn_ref[0:1, 0:d]).astype(BF16)

    row = pl.BlockSpec((tm, d), lambda i: (i, 0))
    vec = pl.BlockSpec((1, d), lambda i: (0, 0))
    modspec = pl.BlockSpec((8, 3 * d), lambda i: (0, 0))
    return pl.pallas_call(
        body, name=name, grid=(s // tm,),
        in_specs=[_rows3(tm, d), row, modspec, vec, vec, modspec],
        out_specs=[_rows3(tm, d), row],
        out_shape=[jax.ShapeDtypeStruct((1, s, d), F32), jax.ShapeDtypeStruct((s, d), BF16)],
        compiler_params=_params(1),
    )(x, y, mod, g, b, mod_next)


def _final_ln_loss_bwd(x, y, mod, g, b, target, name):
    _, s, d = x.shape
    tm = min(s, 256)

    def body(x_ref, y_ref, m_ref, g_ref, b_ref, t_ref, dy_ref, dx_ref, red_ref):
        @pl.when(pl.program_id(0) == 0)
        def _():
            red_ref[...] = jnp.zeros_like(red_ref)

        gate = m_ref[0:1, 2 * d:3 * d]
        yv = y_ref[...]
        xhat, rstd = _ln_stats(DEEPNORM_ALPHA * x_ref[...] + (1.0 + gate) * yv)
        err = xhat * g_ref[...] + b_ref[...] - t_ref[...]
        dout = err * (1.0 / d)
        dr = _ln_bwd(dout * g_ref[...], xhat, rstd)
        dy_ref[...] = ((1.0 + gate) * dr).astype(BF16)
        dx_ref[...] = DEEPNORM_ALPHA * dr
        red_ref[0:1, :] += _colsum(dout * xhat)
        red_ref[1:2, :] += _colsum(dout)
        red_ref[2:3, :] += _colsum(dr * yv)
        red_ref[3:4, :] += _colsum(err * err)

    row = pl.BlockSpec((tm, d), lambda i: (i, 0))
    vec = pl.BlockSpec((1, d), lambda i: (0, 0))
    return pl.pallas_call(
        body, name=name, grid=(s // tm,),
        in_specs=[_rows3(tm, d), row, pl.BlockSpec((8, 3 * d), lambda i: (0, 0)), vec, vec, _rows3(tm, d)],
        out_specs=[row, row, pl.BlockSpec((8, d), lambda i: (0, 0))],
        out_shape=[jax.ShapeDtypeStruct((s, d), BF16), jax.ShapeDtypeStruct((s, d), F32),
                   jax.ShapeDtypeStruct((8, d), F32)],
        compiler_params=_params(1),
    )(x, y, mod, g, b, target)


def _mid_bwd(dh, dxres, x, y, mod_lo, mod_hi, g, b, name):
    _, s, d = x.shape
    tm = min(s, 256)

    def body(dh_ref, dxr_ref, x_ref, y_ref, ml_ref, mh_ref, g_ref, b_ref, dy_ref, dx_ref, red_ref):
        @pl.when(pl.program_id(0) == 0)
        def _():
            red_ref[...] = jnp.zeros_like(red_ref)

        gate = ml_ref[0:1, 2 * d:3 * d]
        scale_hi = mh_ref[0:1, d:2 * d]
        yv, dhv = y_ref[...], dh_ref[...]
        xhat, rstd = _ln_stats(DEEPNORM_ALPHA * x_ref[...] + (1.0 + gate) * yv)
        x_mid = xhat * g_ref[...] + b_ref[...]
        dx_mid = dxr_ref[...] + dhv * (1.0 + scale_hi)
        dr = _ln_bwd(dx_mid * g_ref[...], xhat, rstd)
        dy_ref[...] = ((1.0 + gate) * dr).astype(BF16)
        dx_ref[...] = DEEPNORM_ALPHA * dr
        red_ref[0:1, :] += _colsum(dhv * x_mid)
        red_ref[1:2, :] += _colsum(dhv)
        red_ref[2:3, :] += _colsum(dx_mid * xhat)
        red_ref[3:4, :] += _colsum(dx_mid)
        red_ref[4:5, :] += _colsum(dr * yv)

    row = pl.BlockSpec((tm, d), lambda i: (i, 0))
    vec = pl.BlockSpec((1, d), lambda i: (0, 0))
    modspec = pl.BlockSpec((8, 3 * d), lambda i: (0, 0))
    return pl.pallas_call(
        body, name=name, grid=(s // tm,),
        in_specs=[row, row, _rows3(tm, d), row, modspec, modspec, vec, vec],
        out_specs=[row, row, pl.BlockSpec((8, d), lambda i: (0, 0))],
        out_shape=[jax.ShapeDtypeStruct((s, d), BF16), jax.ShapeDtypeStruct((s, d), F32),
                   jax.ShapeDtypeStruct((8, d), F32)],
        compiler_params=_params(1),
    )(dh, dxres, x, y, mod_lo, mod_hi, g, b)


def _first_bwd(dproj, w3, dxres, x, mod, name, side=None):
    _, s, d = x.shape
    n = w3.shape[2]
    tm = min(s, 256)

    def body(a_ref, w_ref, dxr_ref, x_ref, m_ref, gx_ref, red_ref):
        @pl.when(pl.program_id(0) == 0)
        def _():
            red_ref[...] = jnp.zeros_like(red_ref)

        dhv = _dot(a_ref[:, 0:n], w_ref[0], NT)
        for j in range(1, N_DEV):
            dhv = dhv + _dot(a_ref[:, n * j:n * (j + 1)], w_ref[j], NT)
        gx_ref[...] = dxr_ref[...] + dhv * (1.0 + m_ref[0:1, d:2 * d])
        red_ref[0:1, :] += _colsum(dhv * x_ref[...])
        red_ref[1:2, :] += _colsum(dhv)

    return _grid_call(
        body, name, (s // tm,),
        [pl.BlockSpec((tm, N_DEV * n), lambda i: (i, 0)), pl.BlockSpec((N_DEV, d, n), lambda i: (0, 0, 0)),
         pl.BlockSpec((tm, d), lambda i: (i, 0)), _rows3(tm, d), pl.BlockSpec((8, 3 * d), lambda i: (0, 0))],
        [_rows3(tm, d), pl.BlockSpec((8, d), lambda i: (0, 0))],
        [jax.ShapeDtypeStruct((1, s, d), F32), jax.ShapeDtypeStruct((8, d), F32)],
        [dproj, w3, dxres, x, mod], side=side)


EVEN_TM = 256


def _gmlp_mask():
    t = lax.broadcasted_iota(jnp.int32, (GMLP_BLOCK, GMLP_BLOCK), 0) // CHUNK
    s = lax.broadcasted_iota(jnp.int32, (GMLP_BLOCK, GMLP_BLOCK), 1) // CHUNK
    return s <= t


def _window_sum(ext, win, back):
    n = ext.shape[0]
    k = 1
    while k < win:
        ext = ext + pltpu.roll(ext, k if back else n - k, 0)
        k *= 2
    return ext


def _inv_count(row0, rows, win):
    t = row0 + lax.broadcasted_iota(jnp.int32, (rows, 1), 0)
    return t, 1.0 / jnp.minimum(t + 1, win).astype(F32)


def _pooled(xb, halo, row0, win):
    tm = xb.shape[0]
    sums = _window_sum(jnp.concatenate([halo, xb], axis=0), win, True)[POOL_HALO:]
    _, inv = _inv_count(row0, tm, win)
    return sums * inv - xb


def _even_fwd(proj, ws, bs_col, ng, nb, pw, pb, ps, name, side=None):
    s = proj.shape[0]
    tm = min(s, EVEN_TM)
    hd, gd = GMLP_HEAD_DIM, POOL_GROUP_DIM

    def body(p_ref, halo_ref, ws_ref, bs_ref, ng_ref, nb_ref, pw_ref, pb_ref, ps_ref, m_ref):
        i = pl.program_id(0)
        mask = _gmlp_mask()
        for h in range(GMLP_HEADS):
            wm = jnp.where(mask, ws_ref[h], 0.0).astype(BF16)
            for blk in range(tm // GMLP_BLOCK):
                rows = slice(blk * GMLP_BLOCK, (blk + 1) * GMLP_BLOCK)
                cu, cv, cz = h * hd, 1024 + h * hd, 2048 + h * hd
                vhat, _ = _ln_stats(p_ref[rows, cv:cv + hd].astype(F32))
                vn = vhat * ng_ref[...] + nb_ref[...]
                sv = _dot(wm, vn, NN) + bs_ref[h]
                za = p_ref[rows, cz:cz + hd].astype(F32)
                m_ref[rows, cu:cu + hd] = (p_ref[rows, cu:cu + hd].astype(F32) * sv * (za * _sigmoid(za))).astype(BF16)
        for g, win in enumerate(POOL_WINDOWS):
            cx, cz = 3072 + g * gd, 4096 + g * gd
            halo = jnp.where(i > 0, halo_ref[:, g * gd:(g + 1) * gd].astype(F32), 0.0)
            pooled = _pooled(p_ref[:, cx:cx + gd].astype(F32), halo, i * tm, win)
            yb = _dot(pooled, pw_ref[g], NN) + pb_ref[:, g * gd:(g + 1) * gd]
            zb = p_ref[:, cz:cz + gd].astype(F32)
            m_ref[:, 1024 + g * gd:1024 + (g + 1) * gd] = (
                yb * ps_ref[:, g * gd:(g + 1) * gd] * (zb * _sigmoid(zb))).astype(BF16)

    hb = tm // POOL_HALO
    return _grid_call(
        body, name, (s // tm,),
        [
            pl.BlockSpec((tm, 5120), lambda i: (i, 0)),
            pl.BlockSpec((POOL_HALO, 1024), lambda i: (jnp.maximum(i * hb - 1, 0), 3)),
            pl.BlockSpec((GMLP_HEADS, GMLP_BLOCK, GMLP_BLOCK), lambda i: (0, 0, 0)),
            pl.BlockSpec((GMLP_HEADS, GMLP_BLOCK, 1), lambda i: (0, 0, 0)),
            pl.BlockSpec((1, hd), lambda i: (0, 0)), pl.BlockSpec((1, hd), lambda i: (0, 0)),
            pl.BlockSpec((4, gd, gd), lambda i: (0, 0, 0)),
            pl.BlockSpec((1, 1024), lambda i: (0, 0)), pl.BlockSpec((1, 1024), lambda i: (0, 0)),
        ],
        [pl.BlockSpec((tm, 2048), lambda i: (i, 0))], [jax.ShapeDtypeStruct((s, 2048), BF16)],
        [proj, proj, ws, bs_col, ng, nb, pw, pb, ps], side=side)


def _even_bwd(proj, dm, ws, bs_col, ng, nb, pw, pb, ps, name, side=None):
    s = proj.shape[0]
    tm = min(s, EVEN_TM)
    hd, gd = GMLP_HEAD_DIM, POOL_GROUP_DIM
    n_tiles = s // tm

    def body(p_ref, halo_ref, zbn_ref, dm_ref, dbn_ref, ws_ref, bs_ref, ng_ref, nb_ref, pw_ref, pb_ref, ps_ref,
             dp_ref, dws_ref, dbs_ref, dng_ref, dnb_ref, dpw_ref, dpb_ref, dps_ref):
        i = pl.program_id(0)

        @pl.when(i == 0)
        def _():
            for r in (dws_ref, dbs_ref, dng_ref, dnb_ref, dpw_ref, dpb_ref, dps_ref):
                r[...] = jnp.zeros_like(r)

        mask = _gmlp_mask()
        for h in range(GMLP_HEADS):
            wm = jnp.where(mask, ws_ref[h], 0.0).astype(BF16)
            for blk in range(tm // GMLP_BLOCK):
                rows = slice(blk * GMLP_BLOCK, (blk + 1) * GMLP_BLOCK)
                cu, cv, cz = h * hd, 1024 + h * hd, 2048 + h * hd
                vhat, rstd = _ln_stats(p_ref[rows, cv:cv + hd].astype(F32))
                vn = (vhat * ng_ref[...] + nb_ref[...]).astype(BF16)
                sv = _dot(wm, vn, NN) + bs_ref[h]
                u, za = p_ref[rows, cu:cu + hd].astype(F32), p_ref[rows, cz:cz + hd].astype(F32)
                da = dm_ref[rows, cu:cu + hd].astype(F32)
                sig = _sigmoid(za)
                sa = za * sig
                dau = da * u
                dsv = dau * sa
                dp_ref[rows, cu:cu + hd] = (da * sv * sa).astype(BF16)
                dp_ref[rows, cz:cz + hd] = (dau * sv * (sig * (1.0 + za * (1.0 - sig)))).astype(BF16)
                dsv_b = dsv.astype(BF16)
                dbs_ref[h] += jnp.sum(dsv, axis=1, keepdims=True)
                dws_ref[h] += _dot(dsv_b, vn, NT)
                dvn = _dot(wm, dsv_b, TN)
                dng_ref[...] += _colsum(dvn * vhat)
                dnb_ref[...] += _colsum(dvn)
                dp_ref[rows, cv:cv + hd] = _ln_bwd(dvn * ng_ref[...], vhat, rstd).astype(BF16)

        row0 = i * tm
        for g, win in enumerate(POOL_WINDOWS):
            cx, cz, cd = 3072 + g * gd, 4096 + g * gd, 1024 + g * gd
            gs = slice(g * gd, (g + 1) * gd)
            halo = jnp.where(i > 0, halo_ref[:, gs].astype(F32), 0.0)
            xb = p_ref[:, cx:cx + gd].astype(F32)
            pooled = _pooled(xb, halo, row0, win).astype(BF16)
            scale_g = ps_ref[:, gs]
            yb = _dot(pooled, pw_ref[g], NN) + pb_ref[:, gs]
            zb, db = p_ref[:, cz:cz + gd].astype(F32), dm_ref[:, cd:cd + gd].astype(F32)
            sig = _sigmoid(zb)
            dyp = db * (zb * sig)
            dp_ref[:, cz:cz + gd] = (db * yb * scale_g * (sig * (1.0 + zb * (1.0 - sig)))).astype(BF16)
            dps_ref[:, gs] += _colsum(dyp * yb)
            dpb_ref[:, gs] += _colsum(dyp * scale_g)
            zb_ext = jnp.concatenate([zb, zbn_ref[:, gs].astype(F32)], axis=0)
            db_ext = jnp.concatenate([db, dbn_ref[:, gs].astype(F32)], axis=0)
            dy_ext = (db_ext * (zb_ext * _sigmoid(zb_ext)) * scale_g).astype(BF16)
            dpw_ref[g] += _dot(pooled, dy_ext[:tm], TN)
            dpooled = _dot(dy_ext, pw_ref[g], NT)
            t, inv = _inv_count(row0, tm + POOL_HALO, win)
            w_ext = jnp.where(t < s, dpooled * inv, 0.0)
            dp_ref[:, cx:cx + gd] = (_window_sum(w_ext, win, False)[:tm] - dpooled[:tm]).astype(BF16)

    hb = tm // POOL_HALO
    last = s // POOL_HALO - 1
    small = lambda shape: pl.BlockSpec(shape, lambda i: (0,) * len(shape))
    return _grid_call(
        body, name, (n_tiles,),
        [
            pl.BlockSpec((tm, 5120), lambda i: (i, 0)),
            pl.BlockSpec((POOL_HALO, 1024), lambda i: (jnp.maximum(i * hb - 1, 0), 3)),
            pl.BlockSpec((POOL_HALO, 1024), lambda i: (jnp.minimum((i + 1) * hb, last), 4)),
            pl.BlockSpec((tm, 2048), lambda i: (i, 0)),
            pl.BlockSpec((POOL_HALO, 1024), lambda i: (jnp.minimum((i + 1) * hb, last), 1)),
            small((GMLP_HEADS, GMLP_BLOCK, GMLP_BLOCK)), small((GMLP_HEADS, GMLP_BLOCK, 1)),
            small((1, hd)), small((1, hd)), small((4, gd, gd)), small((1, 1024)), small((1, 1024)),
        ],
        [
            pl.BlockSpec((tm, 5120), lambda i: (i, 0)),
            small((GMLP_HEADS, GMLP_BLOCK, GMLP_BLOCK)), small((GMLP_HEADS, GMLP_BLOCK, 1)),
            small((1, hd)), small((1, hd)), small((4, gd, gd)), small((1, 1024)), small((1, 1024)),
        ],
        [
            jax.ShapeDtypeStruct((s, 5120), BF16),
            jax.ShapeDtypeStruct((GMLP_HEADS, GMLP_BLOCK, GMLP_BLOCK), F32),
            jax.ShapeDtypeStruct((GMLP_HEADS, GMLP_BLOCK, 1), F32),
            jax.ShapeDtypeStruct((1, hd), F32), jax.ShapeDtypeStruct((1, hd), F32),
            jax.ShapeDtypeStruct((4, gd, gd), F32),
            jax.ShapeDtypeStruct((1, 1024), F32), jax.ShapeDtypeStruct((1, 1024), F32),
        ],
        [proj, proj, proj, dm, dm, ws, bs_col, ng, nb, pw, pb, ps], side=side)


def _rope_pair_swap(t):
    lane = lax.broadcasted_iota(jnp.int32, t.shape, 1)
    return jnp.where(lane % 64 < 32, pltpu.roll(t, 96, 1), pltpu.roll(t, 32, 1))


def _rms(x, g):
    r = lax.rsqrt(jnp.mean(x * x, axis=-1, keepdims=True) + LN_EPS)
    return x * r, r


def _rms_bwd(dy, g, xhat, r):
    dyg = dy * g
    return r * (dyg - xhat * jnp.mean(dyg * xhat, axis=-1, keepdims=True))


def _lane_lt(shape, n):
    return lax.broadcasted_iota(jnp.int32, shape, 1) < n


def _mla_prep(proj, cosp, sinp, gq, gkv, name):
    s = proj.shape[0]
    tm = min(s, 512)

    def body(qc_ref, kv_ref, c_ref, s_ref, gq_ref, gkv_ref, qn_ref, kp_ref):
        qhat, _ = _rms(qc_ref[...].astype(F32), None)
        qn_ref[...] = (qhat * gq_ref[...]).astype(BF16)
        khat, _ = _rms(kv_ref[:, 0:128].astype(F32), None)
        kp_ref[:, 0:128] = (khat * gkv_ref[...]).astype(BF16)
        kr = kv_ref[:, 128:256].astype(F32)
        kp_ref[:, 128:256] = (kr * c_ref[...] + _rope_pair_swap(kr) * s_ref[...]).astype(BF16)

    return pl.pallas_call(
        body, name=name, grid=(s // tm,),
        in_specs=[pl.BlockSpec((tm, 256), lambda i: (i, 0)), pl.BlockSpec((tm, 256), lambda i: (i, 1)),
                  pl.BlockSpec((tm, 128), lambda i: (i, 0)), pl.BlockSpec((tm, 128), lambda i: (i, 0)),
                  pl.BlockSpec((1, 256), lambda i: (0, 0)), pl.BlockSpec((1, 128), lambda i: (0, 0))],
        out_specs=[pl.BlockSpec((tm, 256), lambda i: (i, 0)), pl.BlockSpec((tm, 256), lambda i: (i, 0))],
        out_shape=[jax.ShapeDtypeStruct((s, 256), BF16), jax.ShapeDtypeStruct((s, 256), BF16)],
        compiler_params=_params(1),
    )(proj, proj, cosp, sinp, gq, gkv)


def _mla_prep_bwd(proj, dqn, dkp, dv, cosp, sinp, gq, gkv, dproj, name):
    s = proj.shape[0]
    tm = min(s, 512)

    def body(qc_ref, kv_ref, dqn_ref, dkp_ref, dv_ref, c_ref, s_ref, gq_ref, gkv_ref, dproj_ref, o_ref, red_ref):
        @pl.when(pl.program_id(0) == 0)
        def _():
            red_ref[...] = jnp.zeros_like(red_ref)

        qhat, qr = _rms(qc_ref[...].astype(F32), None)
        dq = dqn_ref[...]
        o_ref[:, 0:256] = _rms_bwd(dq, gq_ref[...], qhat, qr).astype(BF16)
        red_ref[0:1, :] += _colsum(dq * qhat)
        khat, kr = _rms(kv_ref[:, 0:128].astype(F32), None)
        dk = dkp_ref[:, 0:128] + dv_ref[...]
        o_ref[:, 256:384] = _rms_bwd(dk, gkv_ref[...], khat, kr).astype(BF16)
        red_ref[1:2, 0:128] += _colsum(dk * khat)
        dr = dkp_ref[:, 128:256]
        o_ref[:, 384:512] = (dr * c_ref[...] - _rope_pair_swap(dr) * s_ref[...]).astype(BF16)

    return pl.pallas_call(
        body, name=name, grid=(s // tm,),
        in_specs=[pl.BlockSpec((tm, 256), lambda i: (i, 0)), pl.BlockSpec((tm, 256), lambda i: (i, 1)),
                  pl.BlockSpec((tm, 256), lambda i: (i, 0)), pl.BlockSpec((tm, 256), lambda i: (i, 0)),
                  pl.BlockSpec((tm, 128), lambda i: (i, 0)),
                  pl.BlockSpec((tm, 128), lambda i: (i, 0)), pl.BlockSpec((tm, 128), lambda i: (i, 0)),
                  pl.BlockSpec((1, 256), lambda i: (0, 0)), pl.BlockSpec((1, 128), lambda i: (0, 0)),
                  pl.BlockSpec(memory_space=pl.ANY)],
        out_specs=[pl.BlockSpec((tm, 512), lambda i: (i, 0)), pl.BlockSpec((8, 256), lambda i: (0, 0))],
        out_shape=[jax.ShapeDtypeStruct(dproj.shape, BF16), jax.ShapeDtypeStruct((8, 256), F32)],
        input_output_aliases={9: 0}, compiler_params=_params(1),
    )(proj, proj, dqn, dkp, dv, cosp, sinp, gq, gkv, dproj)


HEADS_TM = 256
Z_COL0 = ODD_IN_PAD - MLA_WIDTH


def _head_cols(h):
    return slice(128 * h, 128 * h + 128)


def _q_heads(q_up, cosp, sinp, wuk, name):
    s = q_up.shape[0]
    tm = min(s, HEADS_TM)

    def body(q_ref, c_ref, s_ref, w_ref, o_ref):
        for p in range(MLA_HEADS // 2):
            raw = q_ref[:, MLA_WIDTH + 128 * p:MLA_WIDTH + 128 * (p + 1)].astype(F32)
            rot = raw * c_ref[...] + _rope_pair_swap(raw) * s_ref[...]
            low = _lane_lt(rot.shape, 64)
            o_ref[2 * p, :, 128:256] = jnp.where(low, rot, 0.0).astype(BF16)
            o_ref[2 * p + 1, :, 128:256] = jnp.where(low, pltpu.roll(rot, 64, 1), 0.0).astype(BF16)
        for h in range(MLA_HEADS):
            o_ref[h, :, 0:128] = _dot(q_ref[:, _head_cols(h)], w_ref[:, _head_cols(h)], NT).astype(BF16)

    return pl.pallas_call(
        body, name=name, grid=(s // tm,),
        in_specs=[pl.BlockSpec((tm, 3072), lambda i: (i, 0)),
                  pl.BlockSpec((tm, 128), lambda i: (i, 0)), pl.BlockSpec((tm, 128), lambda i: (i, 0)),
                  pl.BlockSpec((128, MLA_WIDTH), lambda i: (0, 0))],
        out_specs=pl.BlockSpec((MLA_HEADS, tm, 256), lambda i: (0, i, 0)),
        out_shape=jax.ShapeDtypeStruct((MLA_HEADS, s, 256), BF16), compiler_params=_params(1),
    )(q_up, cosp, sinp, wuk)


def _q_heads_bwd(dqp, q_up, cosp, sinp, wuk, name):
    s = q_up.shape[0]
    tm = min(s, HEADS_TM)

    def body(dq_ref, qn_ref, c_ref, s_ref, w_ref, dn_ref, dr_ref, dw_ref):
        @pl.when(pl.program_id(0) == 0)
        def _():
            dw_ref[...] = jnp.zeros_like(dw_ref)

        for h in range(MLA_HEADS):
            dlat = dq_ref[h, :, 0:128]
            dn_ref[:, _head_cols(h)] = _dot(dlat, w_ref[:, _head_cols(h)], NN).astype(BF16)
            dw_ref[:, _head_cols(h)] += _dot(dlat, qn_ref[:, _head_cols(h)], TN)
        for p in range(MLA_HEADS // 2):
            drot = dq_ref[2 * p, :, 128:256].astype(F32) + pltpu.roll(dq_ref[2 * p + 1, :, 128:256].astype(F32), 64, 1)
            dr_ref[:, _head_cols(p)] = (drot * c_ref[...] - _rope_pair_swap(drot) * s_ref[...]).astype(BF16)

    return pl.pallas_call(
        body, name=name, grid=(s // tm,),
        in_specs=[pl.BlockSpec((MLA_HEADS, tm, 256), lambda i: (0, i, 0)),
                  pl.BlockSpec((tm, MLA_WIDTH), lambda i: (i, 0)),
                  pl.BlockSpec((tm, 128), lambda i: (i, 0)), pl.BlockSpec((tm, 128), lambda i: (i, 0)),
                  pl.BlockSpec((128, MLA_WIDTH), lambda i: (0, 0))],
        out_specs=[pl.BlockSpec((tm, MLA_WIDTH), lambda i: (i, 0)),
                   pl.BlockSpec((tm, 1024), lambda i: (i, 0)),
                   pl.BlockSpec((128, MLA_WIDTH), lambda i: (0, 0))],
        out_shape=[jax.ShapeDtypeStruct((s, MLA_WIDTH), BF16), jax.ShapeDtypeStruct((s, 1024), BF16),
                   jax.ShapeDtypeStruct((128, MLA_WIDTH), F32)],
        compiler_params=_params(1),
    )(dqp, q_up, cosp, sinp, wuk)


def _o_gate(o_lat, proj, wuv, name):
    s = o_lat.shape[1]
    tm = min(s, HEADS_TM)

    def body(ol_ref, p_ref, w_ref, g_ref):
        for h in range(MLA_HEADS):
            z = p_ref[:, Z_COL0 + 128 * h:Z_COL0 + 128 * (h + 1)].astype(F32)
            g_ref[:, _head_cols(h)] = (_dot(ol_ref[h], w_ref[:, _head_cols(h)], NN) * (z * _sigmoid(z))).astype(BF16)

    return pl.pallas_call(
        body, name=name, grid=(s // tm,),
        in_specs=[pl.BlockSpec((MLA_HEADS, tm, 128), lambda i: (0, i, 0)),
                  pl.BlockSpec((tm, ODD_IN_PAD), lambda i: (i, 0)),
                  pl.BlockSpec((128, MLA_WIDTH), lambda i: (0, 0))],
        out_specs=pl.BlockSpec((tm, MLA_WIDTH), lambda i: (i, 0)),
        out_shape=jax.ShapeDtypeStruct((s, MLA_WIDTH), BF16), compiler_params=_params(1),
    )(o_lat, proj, wuv)


def _o_gate_bwd(dg, o_lat, proj, wuv, name):
    s = o_lat.shape[1]
    tm = min(s, HEADS_TM)

    def body(dg_ref, ol_ref, p_ref, w_ref, dp_ref, dol_ref, dw_ref):
        @pl.when(pl.program_id(0) == 0)
        def _():
            dw_ref[...] = jnp.zeros_like(dw_ref)

        dp_ref[:, 0:Z_COL0] = jnp.zeros((tm, Z_COL0), BF16)
        for h in range(MLA_HEADS):
            zc = slice(Z_COL0 + 128 * h, Z_COL0 + 128 * (h + 1))
            z, dgv, ol = p_ref[:, zc].astype(F32), dg_ref[:, _head_cols(h)].astype(F32), ol_ref[h]
            sig = _sigmoid(z)
            o = _dot(ol, w_ref[:, _head_cols(h)], NN)
            dp_ref[:, zc] = (dgv * o * (sig * (1.0 + z * (1.0 - sig)))).astype(BF16)
            do = (dgv * (z * sig)).astype(BF16)
            dol_ref[h] = _dot(do, w_ref[:, _head_cols(h)], NT).astype(BF16)
            dw_ref[:, _head_cols(h)] += _dot(ol, do, TN)

    return pl.pallas_call(
        body, name=name, grid=(s // tm,),
        in_specs=[pl.BlockSpec((tm, MLA_WIDTH), lambda i: (i, 0)),
                  pl.BlockSpec((MLA_HEADS, tm, 128), lambda i: (0, i, 0)),
                  pl.BlockSpec((tm, ODD_IN_PAD), lambda i: (i, 0)),
                  pl.BlockSpec((128, MLA_WIDTH), lambda i: (0, 0))],
        out_specs=[pl.BlockSpec((tm, ODD_IN_PAD), lambda i: (i, 0)),
                   pl.BlockSpec((MLA_HEADS, tm, 128), lambda i: (0, i, 0)),
                   pl.BlockSpec((128, MLA_WIDTH), lambda i: (0, 0))],
        out_shape=[jax.ShapeDtypeStruct((s, ODD_IN_PAD), BF16), jax.ShapeDtypeStruct((MLA_HEADS, s, 128), BF16),
                   jax.ShapeDtypeStruct((128, MLA_WIDTH), F32)],
        compiler_params=_params(1),
    )(dg, o_lat, proj, wuv)


ATT_TQ = CHUNK
ATT_ROWS = ATT_TQ * MLA_HEADS
ATT_TK = 512
ATT_HEAD_GROUP = 8


def _visible(k0, q_chunk, tk):
    kpos = k0 + lax.broadcasted_iota(jnp.int32, (1, tk), 1)
    return kpos // CHUNK <= q_chunk


def _tile_lanes(t, n):
    return jnp.concatenate([t] * (n // 128), axis=1)


def _key_blocks(i, tk, block, pairs=False):
    visible = i * ATT_TQ + ATT_TQ
    n_full = (visible + tk - 1) // tk - 1

    def full(j):
        block(pl.multiple_of(j * tk, tk), tk, False)

    if pairs:
        def two(jj, carry):
            full(2 * jj)
            full(2 * jj + 1)
            return carry

        lax.fori_loop(0, n_full // 2, two, 0)

        @pl.when(n_full % 2 == 1)
        def _():
            full(n_full - 1)
    else:
        def one(j, carry):
            full(j)
            return carry

        lax.fori_loop(0, n_full, one, 0)
    last0 = pl.multiple_of(n_full * tk, tk)
    half = tk // 2
    if half % 128 == 0:
        @pl.when(visible - n_full * tk <= half)
        def _():
            block(last0, half, True)

        @pl.when(visible - n_full * tk > half)
        def _():
            block(last0, tk, True)
    else:
        block(last0, tk, True)


def _attn_fwd(qp, kp, name, side=None):
    s = kp.shape[0]
    tk = min(ATT_TK, s)

    def body(q_ref, k_ref, o_ref, lse_ref, m_sc, acc_sc):
        i = pl.program_id(0)
        m_sc[...] = jnp.full_like(m_sc, -jnp.inf)
        acc_sc[...] = jnp.zeros_like(acc_sc)

        def block(k0, width, masked):
            k = k_ref[pl.ds(k0, width), :]
            v1 = jnp.where(_lane_lt(k.shape, 128), k, jnp.ones_like(k))
            for h0 in range(0, MLA_HEADS, ATT_HEAD_GROUP):
                rows = slice(h0 * ATT_TQ, (h0 + ATT_HEAD_GROUP) * ATT_TQ)
                q = q_ref[h0:h0 + ATT_HEAD_GROUP].reshape(ATT_HEAD_GROUP * ATT_TQ, 256)
                sc = _dot(q, k, NT) * ATTN_SCALE_LOG2
                if masked:
                    sc = jnp.where(_visible(k0, i, width), sc, -jnp.inf)
                m_prev = m_sc[rows]
                m_new = jnp.maximum(m_prev, jnp.max(sc, axis=1, keepdims=True))
                p = jnp.exp2(sc - _tile_lanes(m_new, width))
                acc_sc[rows] = _tile_lanes(jnp.exp2(m_prev - m_new), 256) * acc_sc[rows] + _dot(p, v1, NN)
                m_sc[rows] = m_new

        _key_blocks(i, tk, block, pairs=True)
        acc = acc_sc[...]
        l = acc[:, 128:256]
        o_ref[...] = (acc[:, 0:128] / l).astype(BF16).reshape(MLA_HEADS, ATT_TQ, 128)
        lse_ref[...] = (m_sc[...] + jnp.log2(l)).reshape(MLA_HEADS, ATT_TQ, 128)

    head128 = pl.BlockSpec((MLA_HEADS, ATT_TQ, 128), lambda i: (0, i, 0))
    return _grid_call(
        body, name, (s // ATT_TQ,),
        [pl.BlockSpec((MLA_HEADS, ATT_TQ, 256), lambda i: (0, i, 0)), pl.BlockSpec((s, 256), lambda i: (0, 0))],
        [head128, head128],
        [jax.ShapeDtypeStruct((MLA_HEADS, s, 128), BF16), jax.ShapeDtypeStruct((MLA_HEADS, s, 128), F32)],
        [qp, kp], scratch=[pltpu.VMEM((ATT_ROWS, 128), F32), pltpu.VMEM((ATT_ROWS, 256), F32)], side=side)


def _attn_bwd(qp, kp, o, do, lse, name, side=None):
    s = kp.shape[0]
    tk = min(ATT_TK, s)

    def body(q_ref, k_ref, o_ref, do_ref, lse_ref, dq_ref, dk_ref, dv_ref, dq_sc):
        i = pl.program_id(0)

        @pl.when(i == 0)
        def _():
            dk_ref[...] = jnp.zeros_like(dk_ref)
            dv_ref[...] = jnp.zeros_like(dv_ref)

        q = q_ref[...].reshape(ATT_ROWS, 256)
        dov = do_ref[...].reshape(ATT_ROWS, 128)
        delta = jnp.sum(dov.astype(F32) * o_ref[...].reshape(ATT_ROWS, 128).astype(F32), axis=1, keepdims=True)
        delta_t = _tile_lanes(jnp.broadcast_to(delta, (ATT_ROWS, 128)), tk)
        lse_t = _tile_lanes(lse_ref[...].reshape(ATT_ROWS, 128), tk)
        dq_sc[...] = jnp.zeros_like(dq_sc)

        def block(k0, width, masked):
            k = k_ref[pl.ds(k0, width), :]
            p = jnp.exp2(_dot(q, k, NT) * ATTN_SCALE_LOG2 - lse_t[:, 0:width])
            if masked:
                p = jnp.where(_visible(k0, i, width), p, 0.0)
            dv_ref[pl.ds(k0, width), :] += _dot(p, dov, TN)
            ds = (p * (_dot(dov, k[:, 0:128], NT) - delta_t[:, 0:width]) * ATTN_SCALE).astype(BF16)
            dq_sc[...] += _dot(ds, k, NN)
            dk_ref[pl.ds(k0, width), :] += _dot(ds, q, TN)

        _key_blocks(i, tk, block, pairs=True)
        dq_ref[...] = dq_sc[...].astype(BF16).reshape(MLA_HEADS, ATT_TQ, 256)

    head128 = pl.BlockSpec((MLA_HEADS, ATT_TQ, 128), lambda i: (0, i, 0))
    head256 = pl.BlockSpec((MLA_HEADS, ATT_TQ, 256), lambda i: (0, i, 0))
    return _grid_call(
        body, name, (s // ATT_TQ,),
        [head256, pl.BlockSpec((s, 256), lambda i: (0, 0)), head128, head128, head128],
        [head256, pl.BlockSpec((s, 256), lambda i: (0, 0)), pl.BlockSpec((s, 128), lambda i: (0, 0))],
        [jax.ShapeDtypeStruct((MLA_HEADS, s, 256), BF16),
         jax.ShapeDtypeStruct((s, 256), F32), jax.ShapeDtypeStruct((s, 128), F32)],
        [qp, kp, o, do, lse], scratch=[pltpu.VMEM((ATT_ROWS, 256), F32)], side=side)


def _place():
    x, y, c = lax.axis_index("x"), lax.axis_index("y"), lax.axis_index("c")
    return x, y, c, 4 * x + 2 * y + c


def _flip(x, y, c, r):
    px = 1 - x if r & 4 else x
    py = 1 - y if r & 2 else y
    pc = 1 - c if r & 1 else c
    return (px, py, pc), 4 * px + 2 * py + pc


def _adaln_exchange(c8, ada_w, ada_b_cols, blocks, name):
    d = c8.shape[1]
    w_cols = ada_w.shape[2]
    n_arr = len(blocks)

    def body(c_ref, w_ref, b_ref, *refs):
        x_refs, (call_ref, mod_ref), out_refs = refs[:n_arr], refs[n_arr:n_arr + 2], refs[n_arr + 2:2 * n_arr + 2]
        sbuf, rbuf, s1, r1, s2, r2 = refs[2 * n_arr + 2:2 * n_arr + 8]
        gather = _Gather(x_refs, out_refs, *refs[2 * n_arr + 8:])
        x, y, c, me = _place()
        call_ref[pl.ds(pl.multiple_of(me * 8, 8), 8), :] = c_ref[...]
        peers = [_flip(x, y, c, r) for r in range(1, N_DEV)]

        def c_copy(k, src_lin, to):
            rows = call_ref.at[pl.ds(pl.multiple_of(src_lin * 8, 8), 8), :]
            return pltpu.make_async_remote_copy(src_ref=rows, dst_ref=rows, send_sem=s1.at[k], recv_sem=r1.at[k],
                                                device_id=to, device_id_type=MESH)

        first = [c_copy(k, me, peer) for k, (peer, _) in enumerate(peers)]
        for cp in first:
            cp.start()
        for k, (_, lin) in enumerate(peers):
            c_copy(k, lin, (x, y, c)).wait_recv()
        for cp in first:
            cp.wait_send()

        for j in range(N_DEV):
            cj = call_ref[8 * j:8 * j + 8, :]
            cond = cj * _sigmoid(cj)
            for l in range(2):
                sbuf[j, l] = lax.dot_general(cond, w_ref[l], NN, precision=lax.Precision.HIGHEST,
                                             preferred_element_type=F32) + b_ref[l]

        def m_copy(k, src_slot, dst_slot, to):
            return pltpu.make_async_remote_copy(src_ref=sbuf.at[src_slot], dst_ref=rbuf.at[dst_slot],
                                                send_sem=s2.at[k], recv_sem=r2.at[k], device_id=to,
                                                device_id_type=MESH)

        rbuf[me] = sbuf[me]
        second = [m_copy(k, lin, me, peer) for k, (peer, lin) in enumerate(peers)]
        for cp in second:
            cp.start()
        gather.start()
        for k, (_, lin) in enumerate(peers):
            m_copy(k, lin, lin, (x, y, c)).wait_recv()
        for cp in second:
            cp.wait_send()
        for j in range(N_DEV):
            for l in range(2):
                mod_ref[l, :, w_cols * j:w_cols * (j + 1)] = rbuf[j, l]
        gather.forward()
        gather.end()

    vmem = pl.BlockSpec(memory_space=pltpu.VMEM)
    anyspace = pl.BlockSpec(memory_space=pl.ANY)
    g_shapes, g_sems = _gather_extras(blocks)
    return pl.pallas_call(
        body, name=name, in_specs=[vmem, vmem, vmem] + [anyspace] * n_arr, out_specs=[vmem, vmem] + [anyspace] * n_arr,
        out_shape=[jax.ShapeDtypeStruct((8 * N_DEV, d), F32), jax.ShapeDtypeStruct((2, 8, 3 * d), F32)] + g_shapes,
        scratch_shapes=[pltpu.VMEM((N_DEV, 2, 8, w_cols), F32), pltpu.VMEM((N_DEV, 2, 8, w_cols), F32),
                        pltpu.SemaphoreType.DMA((N_DEV - 1,)), pltpu.SemaphoreType.DMA((N_DEV - 1,)),
                        pltpu.SemaphoreType.DMA((N_DEV - 1,)), pltpu.SemaphoreType.DMA((N_DEV - 1,))] + g_sems,
        compiler_params=pltpu.CompilerParams(vmem_limit_bytes=VMEM_LIMIT),
    )(c8, ada_w, ada_b_cols, *blocks)


class _Gather:
    def __init__(self, x_refs, out_refs, send_sems, recv_sems, local_sems):
        x, y, c, _ = _place()
        self.me, self.sibling, self.c = (x, y, c), (x, y, 1 - c), c
        self.chips = [(1 - x, y), (x, 1 - y), (1 - x, 1 - y)]
        self.n_arr = len(x_refs)
        self.out_refs, self.send_sems, self.recv_sems = out_refs, send_sems, recv_sems
        self.mine = [pltpu.make_async_copy(x_refs[t], out_refs[t].at[4 * x + 2 * y + c], local_sems.at[t])
                     for t in range(self.n_arr)]
        self.first = []
        for t in range(self.n_arr):
            self.first.append(self.copy(t, 0, self.me, self.sibling, src=x_refs[t]))
            self.first += [self.copy(t, 1 + j, self.me, (*chip, c), src=x_refs[t]) for j, chip in enumerate(self.chips)]
        self.passed = [self.copy(t, 4 + j, (*chip, c), self.sibling)
                       for t in range(self.n_arr) for j, chip in enumerate(self.chips)]

    def copy(self, t, k, blk, to, src=None):
        slot = self.out_refs[t].at[4 * blk[0] + 2 * blk[1] + blk[2]]
        return pltpu.make_async_remote_copy(src_ref=slot if src is None else src, dst_ref=slot,
                                            send_sem=self.send_sems.at[7 * t + k], recv_sem=self.recv_sems.at[7 * t + k],
                                            device_id=to, device_id_type=MESH)

    def start(self):
        for cp in self.mine + self.first:
            cp.start()

    def forward(self):
        for t in range(self.n_arr):
            for j, chip in enumerate(self.chips):
                self.copy(t, 1 + j, (*chip, self.c), self.me).wait_recv()
                self.passed[3 * t + j].start()

    def end(self):
        for t in range(self.n_arr):
            self.copy(t, 0, self.sibling, self.me).wait_recv()
            for j, chip in enumerate(self.chips):
                self.copy(t, 4 + j, (*chip, 1 - self.c), self.me).wait_recv()
        for cp in self.first + self.passed:
            cp.wait_send()
        for cp in self.mine:
            cp.wait()


def _gather_extras(blocks):
    n_arr = len(blocks)
    return ([jax.ShapeDtypeStruct((N_DEV,) + b.shape, b.dtype) for b in blocks],
            [pltpu.SemaphoreType.DMA((7 * n_arr,)), pltpu.SemaphoreType.DMA((7 * n_arr,)),
             pltpu.SemaphoreType.DMA((n_arr,))])


def _small_allreduce(part, name):
    _, rows, cols = part.shape

    def body(p_ref, recv_ref, tot_ref, all_ref, s1, r1, s2, r2):
        x, y, c, me = _place()
        peers = [_flip(x, y, c, r) for r in range(1, N_DEV)]

        def exchange(src_of, dst_ref, send_sems, recv_sems):
            sends = [pltpu.make_async_remote_copy(src_ref=src_of(lin), dst_ref=dst_ref.at[me], send_sem=send_sems.at[k],
                                                  recv_sem=recv_sems.at[k], device_id=peer, device_id_type=MESH)
                     for k, (peer, lin) in enumerate(peers)]
            for cp in sends:
                cp.start()
            for k, (_, lin) in enumerate(peers):
                pltpu.make_async_remote_copy(src_ref=src_of(lin), dst_ref=dst_ref.at[lin], send_sem=send_sems.at[k],
                                             recv_sem=recv_sems.at[k], device_id=(x, y, c), device_id_type=MESH).wait_recv()
            for cp in sends:
                cp.wait_send()

        recv_ref[me] = p_ref[me]
        exchange(lambda lin: p_ref.at[lin], recv_ref, s1, r1)
        total = recv_ref[0]
        for j in range(1, N_DEV):
            total = total + recv_ref[j]
        tot_ref[...] = total
        all_ref[me] = total
        exchange(lambda lin: tot_ref, all_ref, s2, r2)

    vmem = pl.BlockSpec(memory_space=pltpu.VMEM)
    sems = [pltpu.SemaphoreType.DMA((N_DEV - 1,)) for _ in range(4)]
    return pl.pallas_call(
        body, name=name, in_specs=[vmem], out_specs=[vmem, vmem, vmem],
        out_shape=[jax.ShapeDtypeStruct(part.shape, F32), jax.ShapeDtypeStruct((rows, cols), F32),
                   jax.ShapeDtypeStruct(part.shape, F32)],
        scratch_shapes=sems, compiler_params=pltpu.CompilerParams(vmem_limit_bytes=VMEM_LIMIT))(part)


N_CHIPS = N_DEV // 2


def _sibling_swap(part, name):
    def body(g_ref, r_ref, send_sems, recv_sems):
        x, y, c, _ = _place()
        sends = [pltpu.make_async_remote_copy(
            src_ref=g_ref.at[2 * q + 1 - c], dst_ref=r_ref.at[q], send_sem=send_sems.at[q], recv_sem=recv_sems.at[q],
            device_id=(x, y, 1 - c), device_id_type=MESH) for q in range(N_CHIPS)]
        recvs = [pltpu.make_async_remote_copy(
            src_ref=g_ref.at[2 * q + c], dst_ref=r_ref.at[q], send_sem=send_sems.at[q], recv_sem=recv_sems.at[q],
            device_id=(x, y, c), device_id_type=MESH) for q in range(N_CHIPS)]
        for cp in sends:
            cp.start()
        for cp in recvs:
            cp.wait_recv()
        for cp in sends:
            cp.wait_send()

    anyspace = pl.BlockSpec(memory_space=pl.ANY)
    return pl.pallas_call(
        body, name=name, in_specs=[anyspace], out_specs=anyspace,
        out_shape=jax.ShapeDtypeStruct((N_CHIPS,) + part.shape[1:], part.dtype),
        scratch_shapes=[pltpu.SemaphoreType.DMA((N_CHIPS,)), pltpu.SemaphoreType.DMA((N_CHIPS,))])(part)


def _pair_sum(a, b, name):
    n, rows, cols = a.shape
    tr = max(t for t in range(16, 513, 16) if rows % t == 0)

    def body(a_ref, b_ref, o_ref):
        o_ref[...] = (a_ref[...].astype(F32) + b_ref[...].astype(F32)).astype(o_ref.dtype)

    blk = pl.BlockSpec((None, tr, cols), lambda q, i: (q, i, 0))
    return pl.pallas_call(body, name=name, grid=(n, rows // tr), in_specs=[blk, blk], out_specs=blk,
                          out_shape=jax.ShapeDtypeStruct(a.shape, a.dtype), compiler_params=_params(2))(a, b)


def _adamw(w, g, m, v):
    m = ADAM_B1 * m + (1.0 - ADAM_B1) * g
    v = ADAM_B2 * v + (1.0 - ADAM_B2) * (g * g)
    m_hat = m / (1.0 - ADAM_B1 ** ADAM_STEP)
    v_hat = v / (1.0 - ADAM_B2 ** ADAM_STEP)
    return -ADAM_LR * (m_hat / (jnp.sqrt(v_hat) + ADAM_EPS) + ADAM_WD * w), m, v


def _sum_parts_adamw(parts, w, m, v, name):
    n_parts, rows, cols = parts.shape
    tr = max(t for t in range(16, 129, 16) if rows % t == 0)

    def body(p_ref, w_ref, m_ref, v_ref, g_ref, d_ref, mo_ref, vo_ref):
        g = p_ref[0].astype(F32)
        for j in range(1, n_parts):
            g = g + p_ref[j].astype(F32)
        g_ref[...] = g
        d_ref[...], mo_ref[...], vo_ref[...] = _adamw(w_ref[...], g, m_ref[...], v_ref[...])

    row = _rows3(tr, cols)
    out = jax.ShapeDtypeStruct((1, rows, cols), F32)
    return pl.pallas_call(
        body, name=name, grid=(rows // tr,),
        in_specs=[pl.BlockSpec((n_parts, tr, cols), lambda i: (0, i, 0)), row, row, row],
        out_specs=[row, row, row, row], out_shape=[out, out, out, out], compiler_params=_params(1),
    )(parts, w, m, v)


def _sum_parts_adamw_whole(parts, w, m, v, name):
    def body(p_ref, w_ref, m_ref, v_ref, g_ref, d_ref, mo_ref, vo_ref):
        g = p_ref[0:1].astype(F32)
        for j in range(1, N_DEV):
            g = g + p_ref[j:j + 1].astype(F32)
        g_ref[...] = g
        d_ref[...], mo_ref[...], vo_ref[...] = _adamw(w_ref[...], g, m_ref[...], v_ref[...])

    out = jax.ShapeDtypeStruct(w.shape, F32)
    return pl.pallas_call(body, name=name, out_shape=[out] * 4,
                          compiler_params=pltpu.CompilerParams(vmem_limit_bytes=VMEM_LIMIT))(parts, w, m, v)


def _adamw_many(gs, ws, ms, vs, name):
    n = len(gs)

    def body(*refs):
        for k in range(n):
            g_ref, w_ref, m_ref, v_ref = (refs[q * n + k] for q in range(4))
            d_ref, mo_ref, vo_ref = (refs[(4 + q) * n + k] for q in range(3))
            d_ref[...], mo_ref[...], vo_ref[...] = _adamw(w_ref[...], g_ref[...], m_ref[...], v_ref[...])

    out = [jax.ShapeDtypeStruct(w.shape, F32) for w in ws]
    res = pl.pallas_call(body, name=name, out_shape=out * 3,
                         compiler_params=pltpu.CompilerParams(vmem_limit_bytes=VMEM_LIMIT))(*gs, *ws, *ms, *vs)
    return res[:n], res[n:2 * n], res[2 * n:]


def _ada_w_grad_adamw(c_all, dmod_rows, w, m, v, name):
    def body(c_ref, dm_ref, w_ref, m_ref, v_ref, g_ref, d_ref, mo_ref, vo_ref):
        cv = c_ref[...]
        cond = cv * _sigmoid(cv)
        for l in range(2):
            g = lax.dot_general(cond, dm_ref[l], TN, precision=lax.Precision.HIGHEST, preferred_element_type=F32)
            g_ref[l] = g
            d_ref[l], mo_ref[l], vo_ref[l] = _adamw(w_ref[l], g, m_ref[l], v_ref[l])

    out = jax.ShapeDtypeStruct(w.shape, F32)
    return pl.pallas_call(
        body, name=name, out_shape=[out] * 4, compiler_params=pltpu.CompilerParams(vmem_limit_bytes=VMEM_LIMIT),
    )(c_all, dmod_rows, w, m, v)


REPLICATED = ("ln_g", "ln_b", "gmlp_norm_g", "gmlp_norm_b", "gmlp_ws", "gmlp_bs", "pool_b", "pool_scale",
              "mla_kv_norm_g", "mla_w_uk", "mla_w_uv")
CHUNK_ROWS, ADA_ROW, QNORM_ROW, LOSS_ROW, REP_ROWS = 73, 73, 74, 75, 80
UQ_ROWS, POOLW_ROWS = 96, 32


def _pad_rows(flat2d, rows):
    n, k = flat2d.shape
    return jnp.pad(flat2d, ((0, 0), (0, rows * LANES - k))).reshape(n, rows, LANES)


def _ada_cols_rows(vec):
    return _pad_rows(vec.reshape(2, N_DEV, -1).transpose(1, 0, 2).reshape(N_DEV, -1), 1)


def _unpack_replicated(rep, shapes):
    chunk = sum(s[1] for s in shapes) // N_DEV
    flat, off, out = rep[:, :CHUNK_ROWS].reshape(N_DEV, -1)[:, :chunk].reshape(-1), 0, {}
    for n, size, shape in shapes:
        out[n] = flat[off:off + size].reshape(shape)
        off += size
    cols = 3 * D_MODEL // N_DEV
    out["ada_b"] = rep[:, ADA_ROW, :2 * cols].reshape(N_DEV, 2, cols).transpose(1, 0, 2).reshape(2, -1)
    return out


def kernel(x, c, positions, ada_w, ada_b, ln_g, ln_b, e_w_in, gmlp_norm_g, gmlp_norm_b, gmlp_ws, gmlp_bs, pool_w, pool_b, pool_scale, e_w_out, o_w_in, mla_q_norm_g, mla_kv_norm_g, mla_w_uq, mla_w_uk, mla_w_uv, o_w_out, loss_target, m_ada_w, m_ada_b, m_ln_g, m_ln_b, m_e_w_in, m_gmlp_norm_g, m_gmlp_norm_b, m_gmlp_ws, m_gmlp_bs, m_pool_w, m_pool_b, m_pool_scale, m_e_w_out, m_o_w_in, m_mla_q_norm_g, m_mla_kv_norm_g, m_mla_w_uq, m_mla_w_uk, m_mla_w_uv, m_o_w_out, v_ada_w, v_ada_b, v_ln_g, v_ln_b, v_e_w_in, v_gmlp_norm_g, v_gmlp_norm_b, v_gmlp_ws, v_gmlp_bs, v_pool_w, v_pool_b, v_pool_scale, v_e_w_out, v_o_w_in, v_mla_q_norm_g, v_mla_kv_norm_g, v_mla_w_uq, v_mla_w_uk, v_mla_w_uv, v_o_w_out):
    w_in = dict(ada_w=ada_w, ada_b=ada_b, ln_g=ln_g, ln_b=ln_b, e_w_in=e_w_in, gmlp_norm_g=gmlp_norm_g,
                gmlp_norm_b=gmlp_norm_b, gmlp_ws=gmlp_ws, gmlp_bs=gmlp_bs, pool_w=pool_w, pool_b=pool_b,
                pool_scale=pool_scale, e_w_out=e_w_out, o_w_in=o_w_in, mla_q_norm_g=mla_q_norm_g,
                mla_kv_norm_g=mla_kv_norm_g, mla_w_uq=mla_w_uq, mla_w_uk=mla_w_uk, mla_w_uv=mla_w_uv, o_w_out=o_w_out)
    m_in = dict(ada_w=m_ada_w, ada_b=m_ada_b, ln_g=m_ln_g, ln_b=m_ln_b, e_w_in=m_e_w_in, gmlp_norm_g=m_gmlp_norm_g,
                gmlp_norm_b=m_gmlp_norm_b, gmlp_ws=m_gmlp_ws, gmlp_bs=m_gmlp_bs, pool_w=m_pool_w, pool_b=m_pool_b,
                pool_scale=m_pool_scale, e_w_out=m_e_w_out, o_w_in=m_o_w_in, mla_q_norm_g=m_mla_q_norm_g,
                mla_kv_norm_g=m_mla_kv_norm_g, mla_w_uq=m_mla_w_uq, mla_w_uk=m_mla_w_uk, mla_w_uv=m_mla_w_uv,
                o_w_out=m_o_w_out)
    v_in = dict(ada_w=v_ada_w, ada_b=v_ada_b, ln_g=v_ln_g, ln_b=v_ln_b, e_w_in=v_e_w_in, gmlp_norm_g=v_gmlp_norm_g,
                gmlp_norm_b=v_gmlp_norm_b, gmlp_ws=v_gmlp_ws, gmlp_bs=v_gmlp_bs, pool_w=v_pool_w, pool_b=v_pool_b,
                pool_scale=v_pool_scale, e_w_out=v_e_w_out, o_w_in=v_o_w_in, mla_q_norm_g=v_mla_q_norm_g,
                mla_kv_norm_g=v_mla_kv_norm_g, mla_w_uq=v_mla_w_uq, mla_w_uk=v_mla_w_uk, mla_w_uv=v_mla_w_uv,
                o_w_out=v_o_w_out)
    names = list(w_in)
    seq = x.shape[1]
    d = D_MODEL
    me = 4 * lax.axis_index("x") + 2 * lax.axis_index("y") + lax.axis_index("c")
    ada_cols = ada_w.shape[2]

    ada_b_cols = lax.dynamic_slice_in_dim(ada_b, me * ada_cols, ada_cols, axis=1)
    slab_row = lax.broadcasted_iota(jnp.int32, (8, d), 0)
    slab = jnp.where(slab_row == 0, c, jnp.where(slab_row == 1, jnp.pad(mla_q_norm_g, ((0, 0), (0, d - 32))), 0.0))
    c_all, mod, w_in_e3, pool_w3 = _adaln_exchange(
        slab, ada_w, jnp.broadcast_to(ada_b_cols[:, None, :], (2, 8, ada_cols)),
        [e_w_in[0].astype(BF16), pool_w.astype(BF16).reshape(POOLW_ROWS, LANES)], "adaln_exchange")
    h0 = _modulate(x, mod[0], "modulate0")
    proj0, o_in3 = _matmul_cols_nn(h0, w_in_e3, BF16, 512, "even_in", side=([o_w_in[0].astype(BF16)], "gather"))
    o_in_full = o_in3.transpose(1, 0, 2).reshape(d, ODD_IN)
    w_in_o = jnp.concatenate([o_in_full[:, :448], jnp.zeros((d, 64), BF16), o_in_full[:, 448:]], axis=1)
    pool_w_full = pool_w3.reshape(N_DEV, 4, 32, 256).transpose(1, 0, 2, 3).reshape(4, 256, 256)
    g_q = c_all.reshape(N_DEV, 8, d)[:, 1, :32].reshape(1, MLA_Q_RANK)

    ws, bs_col = gmlp_ws[0], gmlp_bs[0].reshape(GMLP_HEADS, GMLP_BLOCK, 1)
    wuk2, wuv2 = mla_w_uk[0].reshape(MLA_KV_RANK, -1), mla_w_uv[0].reshape(MLA_KV_RANK, -1)
    inv = 1.0 / (ROPE_THETA ** (jnp.arange(0, MLA_ROPE, 2, dtype=F32) / MLA_ROPE))
    ang = positions[0].astype(F32)[:, None] * inv
    cosp = jnp.tile(jnp.cos(ang), (1, 4))
    sinp = jnp.tile(jnp.concatenate([-jnp.sin(ang), jnp.sin(ang)], axis=1), (1, 2))

    mix0, w_out_e3 = _even_fwd(proj0, ws, bs_col, gmlp_norm_g, gmlp_norm_b, pool_w_full, pool_b, pool_scale, "even_mix",
                               side=([e_w_out[0].astype(BF16)], "gather"))
    w_out_e = w_out_e3.reshape(-1, d)
    y0, uq3 = _matmul([(mix0, w_out_e)], "nn", F32, seq, d, 512, 1024, "even_out",
                      side=([mla_w_uq.astype(BF16).reshape(UQ_ROWS, LANES)], "gather"))
    uq_full = uq3.reshape(MLA_Q_RANK, MLA_HEADS, MLA_NOPE + MLA_ROPE)
    w_uq_n = uq_full[:, :, :MLA_NOPE].reshape(MLA_Q_RANK, -1)
    w_uq_r = uq_full[:, :, MLA_NOPE:].reshape(MLA_Q_RANK, -1)
    w_uq = jnp.concatenate([w_uq_n, w_uq_r], axis=1)
    x1, h1 = _resid_ln(x, y0, mod[0], ln_g[0:1], ln_b[0:1], mod[1], "resid_ln0")

    (proj1,) = _matmul([(h1, w_in_o)], "nn", BF16, seq, ODD_IN_PAD, 512, ODD_IN_PAD, "odd_in")
    qn, kp = _mla_prep(proj1, cosp, sinp, g_q, mla_kv_norm_g, "mla_prep")
    (q_up,) = _matmul([(qn, w_uq)], "nn", BF16, seq, 3072, 512, 3072, "q_up")
    qp = _q_heads(q_up, cosp, sinp, wuk2, "q_heads")
    o_lat, lse, w_out_o3 = _attn_fwd(qp, kp, "attn_fwd", side=([o_w_out[0].astype(BF16)], "gather"))
    w_out_o = w_out_o3.reshape(-1, d)
    gated = _o_gate(o_lat, proj1, wuv2, "o_gate")
    (y1,) = _matmul([(gated, w_out_o)], "nn", F32, seq, d, 512, 1024, "odd_out")

    dy1, dxres1, red2 = _final_ln_loss_bwd(x1, y1, mod[1], ln_g[1:2], ln_b[1:2], loss_target, "final_ln_loss")
    (dgated,) = _matmul([(dy1, w_out_o)], "nt", BF16, seq, MLA_WIDTH, 512, MLA_WIDTH, "odd_out_dx")
    (g_w_out_o,) = _matmul([(gated, dy1)], "tn", BF16, MLA_WIDTH, d, 256, d, "odd_out_dw")
    dproj1_z, do_lat, g_wuv = _o_gate_bwd(dgated, o_lat, proj1, wuv2, "o_gate_bwd")
    dqp, dkp, dvv, r_o_out = _attn_bwd(qp, kp, o_lat, do_lat, lse, "attn_bwd",
                                       side=([g_w_out_o.reshape(N_DEV, -1, d)], "devices"))
    dq_nope, dq_rope, g_wuk = _q_heads_bwd(dqp, q_up, cosp, sinp, wuk2, "q_heads_bwd")
    (dqn,) = _matmul([(dq_nope, w_uq_n), (dq_rope, w_uq_r)], "nt", F32, seq, MLA_Q_RANK, 512, 256, "q_up_dx")
    (g_wuq_n,) = _matmul([(qn, dq_nope)], "tn", F32, MLA_Q_RANK, MLA_WIDTH, 256, MLA_WIDTH, "q_up_dw_nope")
    (g_wuq_r,) = _matmul([(qn, dq_rope)], "tn", F32, MLA_Q_RANK, 1024, 256, 1024, "q_up_dw_rope")
    dproj1, red_mla = _mla_prep_bwd(proj1, dqn, dkp, dvv, cosp, sinp, g_q, mla_kv_norm_g, dproj1_z, "mla_prep_bwd")
    (dh1,) = _matmul([(dproj1, w_in_o)], "nt", F32, seq, d, 512, d, "odd_in_dx")
    part_uq = jnp.concatenate([g_wuq_n.reshape(MLA_Q_RANK, MLA_HEADS, MLA_NOPE),
                               g_wuq_r.reshape(MLA_Q_RANK, MLA_HEADS, MLA_ROPE)], axis=2).astype(BF16).reshape(
                                   (N_DEV,) + mla_w_uq.shape[1:])
    (g_w_in_o,) = _matmul([(h1, dproj1)], "tn", BF16, d, ODD_IN_PAD, 256, ODD_IN_PAD // 2, "odd_in_dw", n_outer=True)
    part_o_in = jnp.concatenate([g_w_in_o[:, :448], g_w_in_o[:, 512:]], axis=1).reshape(d, N_DEV, -1).transpose(1, 0, 2)
    dy0, dxres0, red1 = _mid_bwd(dh1, dxres1, x, y0, mod[0], mod[1], ln_g[0:1], ln_b[0:1], "mid_bwd")
    dmix, r_uq = _matmul([(dy0, w_out_e)], "nt", BF16, seq, 2048, 512, 2048, "even_out_dx", side=([part_uq], "devices"))
    (g_w_out_e,) = _matmul([(mix0, dy0)], "tn", BF16, 2048, d, 256, d, "even_out_dw")
    dproj0, g_ws, g_bs, g_ng, g_nb, g_pw, g_pb, g_ps, r_o_in = _even_bwd(
        proj0, dmix, ws, bs_col, gmlp_norm_g, gmlp_norm_b, pool_w_full, pool_b, pool_scale, "even_mix_bwd",
        side=([part_o_in], "devices"))
    part_pw = g_pw.reshape(4, N_DEV, 32, 256).transpose(1, 0, 2, 3)
    part_e_in, r_e_out, r_pw = _matmul_cols_tn(h0, dproj0, w_in_e3.shape[2], BF16, 512, "even_in_dw",
                                               side=([g_w_out_e.reshape(N_DEV, -1, d), part_pw], "devices"))
    mine = lax.dynamic_index_in_dim(part_e_in.reshape((N_CHIPS, 2) + part_e_in.shape[1:]), lax.axis_index("c"), 1, False)
    chip_e_in = _pair_sum(mine, _sibling_swap(part_e_in, "e_in_sibling_swap"), "e_in_pair_sum")
    grad_x, red0, r_e_in = _first_bwd(dproj0, w_in_e3, dxres0, x, mod[0], "even_in_dx", side=([chip_e_in], "chips"))

    t_mask = lax.broadcasted_iota(jnp.int32, (GMLP_BLOCK, GMLP_BLOCK), 0) // CHUNK
    s_mask = lax.broadcasted_iota(jnp.int32, (GMLP_BLOCK, GMLP_BLOCK), 1) // CHUNK
    part = {
        "ln_g": jnp.stack([red1[2], red2[0]]), "ln_b": jnp.stack([red1[3], red2[1]]),
        "gmlp_norm_g": g_ng, "gmlp_norm_b": g_nb,
        "gmlp_ws": jnp.where(s_mask <= t_mask, g_ws, 0.0), "gmlp_bs": g_bs,
        "pool_b": g_pb, "pool_scale": g_ps, "mla_kv_norm_g": red_mla[1, :MLA_KV_RANK],
        "mla_w_uk": g_wuk, "mla_w_uv": g_wuv,
    }
    dmod = jnp.stack([jnp.concatenate([red0[1], red0[0], red1[4]]),
                      jnp.concatenate([red1[1], red1[0], red2[2]])])

    loss_row = jnp.pad(jnp.broadcast_to((0.5 / d * jnp.sum(red2[3])).reshape(1, 1, 1), (N_DEV, 1, 1)),
                       ((0, 0), (0, 0), (0, LANES - 1)))
    part_small = jnp.concatenate([
        _pad_rows(jnp.concatenate([part[n].reshape(-1) for n in REPLICATED]).reshape(N_DEV, -1), CHUNK_ROWS),
        jnp.pad(jnp.concatenate([_ada_cols_rows(dmod), _pad_rows(red_mla[0].reshape(N_DEV, -1), 1), loss_row], axis=1),
                ((0, 0), (0, REP_ROWS - LOSS_ROW - 1), (0, 0)))], axis=1)
    r_small, small_sum, rep_sum = _small_allreduce(part_small, "small_allreduce")
    loss = small_sum[LOSS_ROW, 0]

    res = {"e_w_in": _sum_parts_adamw(r_e_in, e_w_in, m_e_w_in, v_e_w_in, "adamw_e_w_in"),
           "o_w_in": _sum_parts_adamw(r_o_in, o_w_in, m_o_w_in, v_o_w_in, "adamw_o_w_in"),
           "e_w_out": _sum_parts_adamw(r_e_out, e_w_out, m_e_w_out, v_e_w_out, "adamw_e_w_out"),
           "o_w_out": _sum_parts_adamw(r_o_out, o_w_out, m_o_w_out, v_o_w_out, "adamw_o_w_out"),
           "mla_w_uq": _sum_parts_adamw_whole(r_uq, mla_w_uq, m_mla_w_uq, v_mla_w_uq, "adamw_w_uq"),
           "pool_w": _sum_parts_adamw_whole(r_pw, pool_w, m_pool_w, v_pool_w, "adamw_pool_w")}
    grads = _unpack_replicated(rep_sum, [(n, w_in[n].size, w_in[n].shape) for n in REPLICATED])
    grads["mla_q_norm_g"] = small_sum[QNORM_ROW:QNORM_ROW + 1, :32]
    small_names = list(grads)
    deltas, new_ms, new_vs = _adamw_many([grads[n] for n in small_names], [w_in[n] for n in small_names],
                                         [m_in[n] for n in small_names], [v_in[n] for n in small_names], "small_adamw")
    for k, n in enumerate(small_names):
        res[n] = [grads[n], deltas[k], new_ms[k], new_vs[k]]
    dmod_all = r_small[:, ADA_ROW, :2 * ada_cols].reshape(N_DEV, 2, ada_cols).transpose(1, 0, 2)
    dmod_rows = jnp.pad(dmod_all[:, :, None, :], ((0, 0), (0, 0), (0, 7), (0, 0))).reshape(2, 8 * N_DEV, ada_cols)
    res["ada_w"] = _ada_w_grad_adamw(c_all, dmod_rows, ada_w, m_ada_w, v_ada_w, "ada_w_adamw")

    return (loss, grad_x, *[res[n][0] for n in names], *[res[n][1] for n in names],
            *[res[n][2] for n in names], *[res[n][3] for n in names])
```

```python
import functools

import jax
import jax.numpy as jnp
from jax import lax
from jax.experimental import pallas as pl
from jax.experimental.pallas import tpu as pltpu

F32 = jnp.float32
BF16 = jnp.bfloat16

D_MODEL = 1024
CHUNK = 64
LN_EPS = 1e-5
GMLP_HEADS = 4
GMLP_HEAD_DIM = 256
GMLP_BLOCK = 128
POOL_WINDOWS = (2, 4, 8, 16)
POOL_GROUP_DIM = 256
POOL_HALO = 16
MLA_HEADS = 16
MLA_NOPE = 128
MLA_ROPE = 64
MLA_Q_RANK = 256
MLA_KV_RANK = 128
MLA_WIDTH = 2048
ODD_IN = 2496
ODD_IN_PAD = 2560
ROPE_THETA = 10000.0
ATTN_SCALE = (MLA_NOPE + MLA_ROPE) ** -0.5
ATTN_SCALE_LOG2 = ATTN_SCALE * 1.4426950408889634
DEEPNORM_ALPHA = 4.0 ** 0.25
ADAM_LR, ADAM_B1, ADAM_B2, ADAM_EPS, ADAM_WD, ADAM_STEP = 0.001, 0.9, 0.999, 1e-8, 0.01, 10
N_DEV = 8
LANES = 1024
VMEM_LIMIT = 56 * 1024 * 1024
MESH = pl.DeviceIdType.MESH

NT = (((1,), (1,)), ((), ()))
NN = (((1,), (0,)), ((), ()))
TN = (((0,), (0,)), ((), ()))


def _params(n_axes):
    return pltpu.CompilerParams(dimension_semantics=("arbitrary",) * n_axes, vmem_limit_bytes=VMEM_LIMIT)


def _dot(a, b, dn):
    return lax.dot_general(a.astype(BF16), b.astype(BF16), dn, preferred_element_type=F32)


def _sigmoid(z):
    return 1.0 / (1.0 + jnp.exp(-z))


def _colsum(t):
    return jnp.sum(t, axis=0, keepdims=True)


EXCHANGE_RELATIONS = {"devices": tuple(range(1, N_DEV)), "chips": (2, 4, 6)}


def _exchange_copies(g_refs, r_refs, send_sems, recv_sems, local_sems, kind):
    x, y, c, me = _place()
    n_arr = len(g_refs)

    def slot(lin):
        return lin // 2 if kind == "chips" else lin

    own = [pltpu.make_async_copy(g_refs[t].at[slot(me)], r_refs[t].at[slot(me)], local_sems.at[t]) for t in range(n_arr)]
    sends, recvs = [], []
    for n, r in enumerate(EXCHANGE_RELATIONS[kind]):
        peer, lin = _flip(x, y, c, r)
        for t in range(n_arr):
            k = n_arr * n + t
            sends.append(pltpu.make_async_remote_copy(
                src_ref=g_refs[t].at[slot(lin)], dst_ref=r_refs[t].at[slot(me)], send_sem=send_sems.at[k],
                recv_sem=recv_sems.at[k], device_id=peer, device_id_type=MESH))
            recvs.append(pltpu.make_async_remote_copy(
                src_ref=g_refs[t].at[slot(lin)], dst_ref=r_refs[t].at[slot(lin)], send_sem=send_sems.at[k],
                recv_sem=recv_sems.at[k], device_id=(x, y, c), device_id_type=MESH))
    return own, sends, recvs


def _exchange_start(copies):
    own, sends, _ = copies
    for cp in own + sends:
        cp.start()


def _exchange_wait(copies):
    own, sends, recvs = copies
    for cp in recvs:
        cp.wait_recv()
    for cp in sends:
        cp.wait_send()
    for cp in own:
        cp.wait()


def _exchange_extras(parts, kind):
    shapes = [jax.ShapeDtypeStruct(p.shape, p.dtype) for p in parts]
    n = len(parts) * len(EXCHANGE_RELATIONS[kind])
    return shapes, [pltpu.SemaphoreType.DMA((n,)), pltpu.SemaphoreType.DMA((n,)), pltpu.SemaphoreType.DMA((len(parts),))]


def _grid_call(body, name, grid, in_specs, out_specs, out_shape, args, scratch=(), side=None):
    if side is None:
        return pl.pallas_call(body, name=name, grid=grid, in_specs=in_specs, out_specs=out_specs,
                              out_shape=out_shape, scratch_shapes=list(scratch),
                              compiler_params=_params(len(grid)))(*args)
    parts, kind = side
    gather = kind == "gather"
    n_in, n_out, n_sc, n_arr = len(args), len(out_shape), len(scratch), len(parts)
    side_shapes, side_sems = _gather_extras(parts) if gather else _exchange_extras(parts, kind)
    mid = tuple(g // 2 for g in grid)

    def wrapped(*refs):
        ins, g_refs = refs[:n_in], refs[n_in:n_in + n_arr]
        outs = refs[n_in + n_arr:n_in + n_arr + n_out]
        r_refs = refs[n_in + n_arr + n_out:n_in + 2 * n_arr + n_out]
        sc = refs[n_in + 2 * n_arr + n_out:n_in + 2 * n_arr + n_out + n_sc]
        ids = [pl.program_id(a) for a in range(len(grid))]

        def at(step):
            return functools.reduce(jnp.logical_and, [i == s for i, s in zip(ids, step)])

        first, last = at((0,) * len(grid)), at(tuple(g - 1 for g in grid))
        if gather:
            exchange = _Gather(g_refs, r_refs, *refs[-3:])
            pl.when(first)(exchange.start)
            if mid != (0,) * len(grid):
                pl.when(at(mid))(exchange.forward)
            body(*ins, *outs, *sc)

            @pl.when(last)
            def _():
                if mid == (0,) * len(grid):
                    exchange.forward()
                exchange.end()
        else:
            copies = _exchange_copies(g_refs, r_refs, *refs[-3:], kind)
            pl.when(first)(lambda: _exchange_start(copies))
            body(*ins, *outs, *sc)
            pl.when(last)(lambda: _exchange_wait(copies))

    anyspace = pl.BlockSpec(memory_space=pl.ANY)
    return pl.pallas_call(
        wrapped, name=name, grid=grid, in_specs=list(in_specs) + [anyspace] * n_arr,
        out_specs=list(out_specs) + [anyspace] * n_arr, out_shape=list(out_shape) + side_shapes,
        scratch_shapes=list(scratch) + side_sems, compiler_params=_params(len(grid)),
    )(*args, *parts)


def _matmul(pairs, mode, out_dtype, m, n, tm, tn, name, side=None, n_outer=False):
    dn = {"nn": NN, "nt": NT, "tn": TN}[mode]
    tm, tn = min(tm, m), min(tn, n)
    n_pairs = len(pairs)
    grid = (n // tn, m // tm) if n_outer else (m // tm, n // tn)

    def ij(f):
        return (lambda j, i: f(i, j)) if n_outer else f

    def body(*refs):
        o_ref = refs[-1]
        acc = None
        for p in range(n_pairs):
            t = _dot(refs[2 * p][...], refs[2 * p + 1][...], dn)
            acc = t if acc is None else acc + t
        o_ref[...] = acc.astype(o_ref.dtype)

    in_specs, args = [], []
    for a, b in pairs:
        if mode == "nn":
            k = a.shape[1]
            in_specs += [pl.BlockSpec((tm, k), ij(lambda i, j: (i, 0))), pl.BlockSpec((k, tn), ij(lambda i, j: (0, j)))]
        elif mode == "nt":
            k = a.shape[1]
            in_specs += [pl.BlockSpec((tm, k), ij(lambda i, j: (i, 0))), pl.BlockSpec((tn, k), ij(lambda i, j: (j, 0)))]
        else:
            k = a.shape[0]
            in_specs += [pl.BlockSpec((k, tm), ij(lambda i, j: (0, i))), pl.BlockSpec((k, tn), ij(lambda i, j: (0, j)))]
        args += [a, b]
    return _grid_call(body, name, grid, in_specs, [pl.BlockSpec((tm, tn), ij(lambda i, j: (i, j)))],
                      [jax.ShapeDtypeStruct((m, n), out_dtype)], args, side=side)


def _matmul_cols_nn(a, w3, out_dtype, tm, name, side=None):
    m, k = a.shape
    _, _, n = w3.shape
    tm = min(tm, m)

    def body(a_ref, w_ref, o_ref):
        av = a_ref[...]
        for j in range(N_DEV):
            o_ref[:, n * j:n * (j + 1)] = _dot(av, w_ref[j], NN).astype(o_ref.dtype)

    return _grid_call(
        body, name, (m // tm,),
        [pl.BlockSpec((tm, k), lambda i: (i, 0)), pl.BlockSpec((N_DEV, k, n), lambda i: (0, 0, 0))],
        [pl.BlockSpec((tm, N_DEV * n), lambda i: (i, 0))], [jax.ShapeDtypeStruct((m, N_DEV * n), out_dtype)], [a, w3],
        side=side)


def _matmul_cols_tn(a, b, n, out_dtype, tk, name, side=None):
    m, k = a.shape
    tk = min(tk, k)

    def body(a_ref, b_ref, o_ref):
        o_ref[...] = _dot(a_ref[...], b_ref[...], TN).astype(o_ref.dtype)

    return _grid_call(
        body, name, (N_DEV, k // tk),
        [pl.BlockSpec((m, tk), lambda j, i: (0, i)), pl.BlockSpec((m, n), lambda j, i: (0, j))],
        [pl.BlockSpec((None, tk, n), lambda j, i: (j, i, 0))], [jax.ShapeDtypeStruct((N_DEV, k, n), out_dtype)], [a, b],
        side=side)


def _rows3(tm, d):
    return pl.BlockSpec((None, tm, d), lambda i: (0, i, 0))


def _modulate(x, mod, name):
    _, s, d = x.shape
    tm = min(s, 512)

    def body(x_ref, m_ref, h_ref):
        shift, scale = m_ref[0:1, 0:d], m_ref[0:1, d:2 * d]
        h_ref[...] = (x_ref[...] * (1.0 + scale) + shift).astype(BF16)

    return pl.pallas_call(
        body, name=name, grid=(s // tm,),
        in_specs=[_rows3(tm, d), pl.BlockSpec((8, 3 * d), lambda i: (0, 0))],
        out_specs=pl.BlockSpec((tm, d), lambda i: (i, 0)),
        out_shape=jax.ShapeDtypeStruct((s, d), BF16), compiler_params=_params(1),
    )(x, mod)


def _ln_stats(r):
    mu = jnp.mean(r, axis=-1, keepdims=True)
    rc = r - mu
    var = jnp.mean(rc * rc, axis=-1, keepdims=True)
    rstd = lax.rsqrt(var + LN_EPS)
    return rc * rstd, rstd


def _ln_bwd(dxhat, xhat, rstd):
    return rstd * (dxhat - jnp.mean(dxhat, axis=-1, keepdims=True)
                   - xhat * jnp.mean(dxhat * xhat, axis=-1, keepdims=True))


def _resid_ln(x, y, mod, g, b, mod_next, name):
    _, s, d = x.shape
    tm = min(s, 512)

    def body(x_ref, y_ref, m_ref, g_ref, b_ref, mn_ref, o_ref, h_ref):
        gate = m_ref[0:1, 2 * d:3 * d]
        xhat, _ = _ln_stats(DEEPNORM_ALPHA * x_ref[...] + (1.0 + gate) * y_ref[...])
        out = xhat * g_ref[...] + b_ref[...]
        o_ref[...] = out
        h_ref[...] = (out * (1.0 + mn_ref[0:1, d:2 * d]) + mn_ref[0:1, 0:d]).astype(BF16)

    row = pl.BlockSpec((tm, d), lambda i: (i, 0))
    vec = pl.BlockSpec((1, d), lambda i: (0, 0))
    modspec = pl.BlockSpec((8, 3 * d), lambda i: (0, 0))
    return pl.pallas_call(
        body, name=name, grid=(s // tm,),
        in_specs=[_rows3(tm, d), row, modspec, vec, vec, modspec],
        out_specs=[_rows3(tm, d), row],
        out_shape=[jax.ShapeDtypeStruct((1, s, d), F32), jax.ShapeDtypeStruct((s, d), BF16)],
        compiler_params=_params(1),
    )(x, y, mod, g, b, mod_next)


def _final_ln_loss_bwd(x, y, mod, g, b, target, name):
    _, s, d = x.shape
    tm = min(s, 256)

    def body(x_ref, y_ref, m_ref, g_ref, b_ref, t_ref, dy_ref, dx_ref, red_ref):
        @pl.when(pl.program_id(0) == 0)
        def _():
            red_ref[...] = jnp.zeros_like(red_ref)

        gate = m_ref[0:1, 2 * d:3 * d]
        yv = y_ref[...]
        xhat, rstd = _ln_stats(DEEPNORM_ALPHA * x_ref[...] + (1.0 + gate) * yv)
        err = xhat * g_ref[...] + b_ref[...] - t_ref[...]
        dout = err * (1.0 / d)
        dr = _ln_bwd(dout * g_ref[...], xhat, rstd)
        dy_ref[...] = ((1.0 + gate) * dr).astype(BF16)
        dx_ref[...] = DEEPNORM_ALPHA * dr
        red_ref[0:1, :] += _colsum(dout * xhat)
        red_ref[1:2, :] += _colsum(dout)
        red_ref[2:3, :] += _colsum(dr * yv)
        red_ref[3:4, :] += _colsum(err * err)

    row = pl.BlockSpec((tm, d), lambda i: (i, 0))
    vec = pl.BlockSpec((1, d), lambda i: (0, 0))
    return pl.pallas_call(
        body, name=name, grid=(s // tm,),
        in_specs=[_rows3(tm, d), row, pl.BlockSpec((8, 3 * d), lambda i: (0, 0)), vec, vec, _rows3(tm, d)],
        out_specs=[row, row, pl.BlockSpec((8, d), lambda i: (0, 0))],
        out_shape=[jax.ShapeDtypeStruct((s, d), BF16), jax.ShapeDtypeStruct((s, d), F32),
                   jax.ShapeDtypeStruct((8, d), F32)],
        compiler_params=_params(1),
    )(x, y, mod, g, b, target)


def _mid_bwd(dh, dxres, x, y, mod_lo, mod_hi, g, b, name):
    _, s, d = x.shape
    tm = min(s, 256)

    def body(dh_ref, dxr_ref, x_ref, y_ref, ml_ref, mh_ref, g_ref, b_ref, dy_ref, dx_ref, red_ref):
        @pl.when(pl.program_id(0) == 0)
        def _():
            red_ref[...] = jnp.zeros_like(red_ref)

        gate = ml_ref[0:1, 2 * d:3 * d]
        scale_hi = mh_ref[0:1, d:2 * d]
        yv, dhv = y_ref[...], dh_ref[...]
        xhat, rstd = _ln_stats(DEEPNORM_ALPHA * x_ref[...] + (1.0 + gate) * yv)
        x_mid = xhat * g_ref[...] + b_ref[...]
        dx_mid = dxr_ref[...] + dhv * (1.0 + scale_hi)
        dr = _ln_bwd(dx_mid * g_ref[...], xhat, rstd)
        dy_ref[...] = ((1.0 + gate) * dr).astype(BF16)
        dx_ref[...] = DEEPNORM_ALPHA * dr
        red_ref[0:1, :] += _colsum(dhv * x_mid)
        red_ref[1:2, :] += _colsum(dhv)
        red_ref[2:3, :] += _colsum(dx_mid * xhat)
        red_ref[3:4, :] += _colsum(dx_mid)
        red_ref[4:5, :] += _colsum(dr * yv)

    row = pl.BlockSpec((tm, d), lambda i: (i, 0))
    vec = pl.BlockSpec((1, d), lambda i: (0, 0))
    modspec = pl.BlockSpec((8, 3 * d), lambda i: (0, 0))
    return pl.pallas_call(
        body, name=name, grid=(s // tm,),
        in_specs=[row, row, _rows3(tm, d), row, modspec, modspec, vec, vec],
        out_specs=[row, row, pl.BlockSpec((8, d), lambda i: (0, 0))],
        out_shape=[jax.ShapeDtypeStruct((s, d), BF16), jax.ShapeDtypeStruct((s, d), F32),
                   jax.ShapeDtypeStruct((8, d), F32)],
        compiler_params=_params(1),
    )(dh, dxres, x, y, mod_lo, mod_hi, g, b)


def _first_bwd(dproj, w3, dxres, x, mod, name, side=None):
    _, s, d = x.shape
    n = w3.shape[2]
    tm = min(s, 256)

    def body(a_ref, w_ref, dxr_ref, x_ref, m_ref, gx_ref, red_ref):
        @pl.when(pl.program_id(0) == 0)
        def _():
            red_ref[...] = jnp.zeros_like(red_ref)

        dhv = _dot(a_ref[:, 0:n], w_ref[0], NT)
        for j in range(1, N_DEV):
            dhv = dhv + _dot(a_ref[:, n * j:n * (j + 1)], w_ref[j], NT)
        gx_ref[...] = dxr_ref[...] + dhv * (1.0 + m_ref[0:1, d:2 * d])
        red_ref[0:1, :] += _colsum(dhv * x_ref[...])
        red_ref[1:2, :] += _colsum(dhv)

    return _grid_call(
        body, name, (s // tm,),
        [pl.BlockSpec((tm, N_DEV * n), lambda i: (i, 0)), pl.BlockSpec((N_DEV, d, n), lambda i: (0, 0, 0)),
         pl.BlockSpec((tm, d), lambda i: (i, 0)), _rows3(tm, d), pl.BlockSpec((8, 3 * d), lambda i: (0, 0))],
        [_rows3(tm, d), pl.BlockSpec((8, d), lambda i: (0, 0))],
        [jax.ShapeDtypeStruct((1, s, d), F32), jax.ShapeDtypeStruct((8, d), F32)],
        [dproj, w3, dxres, x, mod], side=side)


EVEN_TM = 256


def _gmlp_mask():
    t = lax.broadcasted_iota(jnp.int32, (GMLP_BLOCK, GMLP_BLOCK), 0) // CHUNK
    s = lax.broadcasted_iota(jnp.int32, (GMLP_BLOCK, GMLP_BLOCK), 1) // CHUNK
    return s <= t


def _window_sum(ext, win, back):
    n = ext.shape[0]
    k = 1
    while k < win:
        ext = ext + pltpu.roll(ext, k if back else n - k, 0)
        k *= 2
    return ext


def _inv_count(row0, rows, win):
    t = row0 + lax.broadcasted_iota(jnp.int32, (rows, 1), 0)
    return t, 1.0 / jnp.minimum(t + 1, win).astype(F32)


def _pooled(xb, halo, row0, win):
    tm = xb.shape[0]
    sums = _window_sum(jnp.concatenate([halo, xb], axis=0), win, True)[POOL_HALO:]
    _, inv = _inv_count(row0, tm, win)
    return sums * inv - xb


def _even_fwd(proj, ws, bs_col, ng, nb, pw, pb, ps, name, side=None):
    s = proj.shape[0]
    tm = min(s, EVEN_TM)
    hd, gd = GMLP_HEAD_DIM, POOL_GROUP_DIM

    def body(p_ref, halo_ref, ws_ref, bs_ref, ng_ref, nb_ref, pw_ref, pb_ref, ps_ref, m_ref):
        i = pl.program_id(0)
        mask = _gmlp_mask()
        for h in range(GMLP_HEADS):
            wm = jnp.where(mask, ws_ref[h], 0.0).astype(BF16)
            for blk in range(tm // GMLP_BLOCK):
                rows = slice(blk * GMLP_BLOCK, (blk + 1) * GMLP_BLOCK)
                cu, cv, cz = h * hd, 1024 + h * hd, 2048 + h * hd
                vhat, _ = _ln_stats(p_ref[rows, cv:cv + hd].astype(F32))
                vn = vhat * ng_ref[...] + nb_ref[...]
                sv = _dot(wm, vn, NN) + bs_ref[h]
                za = p_ref[rows, cz:cz + hd].astype(F32)
                m_ref[rows, cu:cu + hd] = (p_ref[rows, cu:cu + hd].astype(F32) * sv * (za * _sigmoid(za))).astype(BF16)
        for g, win in enumerate(POOL_WINDOWS):
            cx, cz = 3072 + g * gd, 4096 + g * gd
            halo = jnp.where(i > 0, halo_ref[:, g * gd:(g + 1) * gd].astype(F32), 0.0)
            pooled = _pooled(p_ref[:, cx:cx + gd].astype(F32), halo, i * tm, win)
            yb = _dot(pooled, pw_ref[g], NN) + pb_ref[:, g * gd:(g + 1) * gd]
            zb = p_ref[:, cz:cz + gd].astype(F32)
            m_ref[:, 1024 + g * gd:1024 + (g + 1) * gd] = (
                yb * ps_ref[:, g * gd:(g + 1) * gd] * (zb * _sigmoid(zb))).astype(BF16)

    hb = tm // POOL_HALO
    return _grid_call(
        body, name, (s // tm,),
        [
            pl.BlockSpec((tm, 5120), lambda i: (i, 0)),
            pl.BlockSpec((POOL_HALO, 1024), lambda i: (jnp.maximum(i * hb - 1, 0), 3)),
            pl.BlockSpec((GMLP_HEADS, GMLP_BLOCK, GMLP_BLOCK), lambda i: (0, 0, 0)),
            pl.BlockSpec((GMLP_HEADS, GMLP_BLOCK, 1), lambda i: (0, 0, 0)),
            pl.BlockSpec((1, hd), lambda i: (0, 0)), pl.BlockSpec((1, hd), lambda i: (0, 0)),
            pl.BlockSpec((4, gd, gd), lambda i: (0, 0, 0)),
            pl.BlockSpec((1, 1024), lambda i: (0, 0)), pl.BlockSpec((1, 1024), lambda i: (0, 0)),
        ],
        [pl.BlockSpec((tm, 2048), lambda i: (i, 0))], [jax.ShapeDtypeStruct((s, 2048), BF16)],
        [proj, proj, ws, bs_col, ng, nb, pw, pb, ps], side=side)


def _even_bwd(proj, dm, ws, bs_col, ng, nb, pw, pb, ps, name, side=None):
    s = proj.shape[0]
    tm = min(s, EVEN_TM)
    hd, gd = GMLP_HEAD_DIM, POOL_GROUP_DIM
    n_tiles = s // tm

    def body(p_ref, halo_ref, zbn_ref, dm_ref, dbn_ref, ws_ref, bs_ref, ng_ref, nb_ref, pw_ref, pb_ref, ps_ref,
             dp_ref, dws_ref, dbs_ref, dng_ref, dnb_ref, dpw_ref, dpb_ref, dps_ref):
        i = pl.program_id(0)

        @pl.when(i == 0)
        def _():
            for r in (dws_ref, dbs_ref, dng_ref, dnb_ref, dpw_ref, dpb_ref, dps_ref):
                r[...] = jnp.zeros_like(r)

        mask = _gmlp_mask()
        for h in range(GMLP_HEADS):
            wm = jnp.where(mask, ws_ref[h], 0.0).astype(BF16)
            for blk in range(tm // GMLP_BLOCK):
                rows = slice(blk * GMLP_BLOCK, (blk + 1) * GMLP_BLOCK)
                cu, cv, cz = h * hd, 1024 + h * hd, 2048 + h * hd
                vhat, rstd = _ln_stats(p_ref[rows, cv:cv + hd].astype(F32))
                vn = (vhat * ng_ref[...] + nb_ref[...]).astype(BF16)
                sv = _dot(wm, vn, NN) + bs_ref[h]
                u, za = p_ref[rows, cu:cu + hd].astype(F32), p_ref[rows, cz:cz + hd].astype(F32)
                da = dm_ref[rows, cu:cu + hd].astype(F32)
                sig = _sigmoid(za)
                sa = za * sig
                dau = da * u
                dsv = dau * sa
                dp_ref[rows, cu:cu + hd] = (da * sv * sa).astype(BF16)
                dp_ref[rows, cz:cz + hd] = (dau * sv * (sig * (1.0 + za * (1.0 - sig)))).astype(BF16)
                dsv_b = dsv.astype(BF16)
                dbs_ref[h] += jnp.sum(dsv, axis=1, keepdims=True)
                dws_ref[h] += _dot(dsv_b, vn, NT)
                dvn = _dot(wm, dsv_b, TN)
                dng_ref[...] += _colsum(dvn * vhat)
                dnb_ref[...] += _colsum(dvn)
                dp_ref[rows, cv:cv + hd] = _ln_bwd(dvn * ng_ref[...], vhat, rstd).astype(BF16)

        row0 = i * tm
        for g, win in enumerate(POOL_WINDOWS):
            cx, cz, cd = 3072 + g * gd, 4096 + g * gd, 1024 + g * gd
            gs = slice(g * gd, (g + 1) * gd)
            halo = jnp.where(i > 0, halo_ref[:, gs].astype(F32), 0.0)
            xb = p_ref[:, cx:cx + gd].astype(F32)
            pooled = _pooled(xb, halo, row0, win).astype(BF16)
            scale_g = ps_ref[:, gs]
            yb = _dot(pooled, pw_ref[g], NN) + pb_ref[:, gs]
            zb, db = p_ref[:, cz:cz + gd].astype(F32), dm_ref[:, cd:cd + gd].astype(F32)
            sig = _sigmoid(zb)
            dyp = db * (zb * sig)
            dp_ref[:, cz:cz + gd] = (db * yb * scale_g * (sig * (1.0 + zb * (1.0 - sig)))).astype(BF16)
            dps_ref[:, gs] += _colsum(dyp * yb)
            dpb_ref[:, gs] += _colsum(dyp * scale_g)
            zb_ext = jnp.concatenate([zb, zbn_ref[:, gs].astype(F32)], axis=0)
            db_ext = jnp.concatenate([db, dbn_ref[:, gs].astype(F32)], axis=0)
            dy_ext = (db_ext * (zb_ext * _sigmoid(zb_ext)) * scale_g).astype(BF16)
            dpw_ref[g] += _dot(pooled, dy_ext[:tm], TN)
            dpooled = _dot(dy_ext, pw_ref[g], NT)
            t, inv = _inv_count(row0, tm + POOL_HALO, win)
            w_ext = jnp.where(t < s, dpooled * inv, 0.0)
            dp_ref[:, cx:cx + gd] = (_window_sum(w_ext, win, False)[:tm] - dpooled[:tm]).astype(BF16)

    hb = tm // POOL_HALO
    last = s // POOL_HALO - 1
    small = lambda shape: pl.BlockSpec(shape, lambda i: (0,) * len(shape))
    return _grid_call(
        body, name, (n_tiles,),
        [
            pl.BlockSpec((tm, 5120), lambda i: (i, 0)),
            pl.BlockSpec((POOL_HALO, 1024), lambda i: (jnp.maximum(i * hb - 1, 0), 3)),
            pl.BlockSpec((POOL_HALO, 1024), lambda i: (jnp.minimum((i + 1) * hb, last), 4)),
            pl.BlockSpec((tm, 2048), lambda i: (i, 0)),
            pl.BlockSpec((POOL_HALO, 1024), lambda i: (jnp.minimum((i + 1) * hb, last), 1)),
            small((GMLP_HEADS, GMLP_BLOCK, GMLP_BLOCK)), small((GMLP_HEADS, GMLP_BLOCK, 1)),
            small((1, hd)), small((1, hd)), small((4, gd, gd)), small((1, 1024)), small((1, 1024)),
        ],
        [
            pl.BlockSpec((tm, 5120), lambda i: (i, 0)),
            small((GMLP_HEADS, GMLP_BLOCK, GMLP_BLOCK)), small((GMLP_HEADS, GMLP_BLOCK, 1)),
            small((1, hd)), small((1, hd)), small((4, gd, gd)), small((1, 1024)), small((1, 1024)),
        ],
        [
            jax.ShapeDtypeStruct((s, 5120), BF16),
            jax.ShapeDtypeStruct((GMLP_HEADS, GMLP_BLOCK, GMLP_BLOCK), F32),
            jax.ShapeDtypeStruct((GMLP_HEADS, GMLP_BLOCK, 1), F32),
            jax.ShapeDtypeStruct((1, hd), F32), jax.ShapeDtypeStruct((1, hd), F32),
            jax.ShapeDtypeStruct((4, gd, gd), F32),
            jax.ShapeDtypeStruct((1, 1024), F32), jax.ShapeDtypeStruct((1, 1024), F32),
        ],
        [proj, proj, proj, dm, dm, ws, bs_col, ng, nb, pw, pb, ps], side=side)


def _rope_pair_swap(t):
    lane = lax.broadcasted_iota(jnp.int32, t.shape, 1)
    return jnp.where(lane % 64 < 32, pltpu.roll(t, 96, 1), pltpu.roll(t, 32, 1))


def _rms(x, g):
    r = lax.rsqrt(jnp.mean(x * x, axis=-1, keepdims=True) + LN_EPS)
    return x * r, r


def _rms_bwd(dy, g, xhat, r):
    dyg = dy * g
    return r * (dyg - xhat * jnp.mean(dyg * xhat, axis=-1, keepdims=True))


def _lane_lt(shape, n):
    return lax.broadcasted_iota(jnp.int32, shape, 1) < n


def _mla_prep(proj, cosp, sinp, gq, gkv, name):
    s = proj.shape[0]
    tm = min(s, 512)

    def body(qc_ref, kv_ref, c_ref, s_ref, gq_ref, gkv_ref, qn_ref, kp_ref):
        qhat, _ = _rms(qc_ref[...].astype(F32), None)
        qn_ref[...] = (qhat * gq_ref[...]).astype(BF16)
        khat, _ = _rms(kv_ref[:, 0:128].astype(F32), None)
        kp_ref[:, 0:128] = (khat * gkv_ref[...]).astype(BF16)
        kr = kv_ref[:, 128:256].astype(F32)
        kp_ref[:, 128:256] = (kr * c_ref[...] + _rope_pair_swap(kr) * s_ref[...]).astype(BF16)

    return pl.pallas_call(
        body, name=name, grid=(s // tm,),
        in_specs=[pl.BlockSpec((tm, 256), lambda i: (i, 0)), pl.BlockSpec((tm, 256), lambda i: (i, 1)),
                  pl.BlockSpec((tm, 128), lambda i: (i, 0)), pl.BlockSpec((tm, 128), lambda i: (i, 0)),
                  pl.BlockSpec((1, 256), lambda i: (0, 0)), pl.BlockSpec((1, 128), lambda i: (0, 0))],
        out_specs=[pl.BlockSpec((tm, 256), lambda i: (i, 0)), pl.BlockSpec((tm, 256), lambda i: (i, 0))],
        out_shape=[jax.ShapeDtypeStruct((s, 256), BF16), jax.ShapeDtypeStruct((s, 256), BF16)],
        compiler_params=_params(1),
    )(proj, proj, cosp, sinp, gq, gkv)


def _mla_prep_bwd(proj, dqn, dkp, dv, cosp, sinp, gq, gkv, dproj, name):
    s = proj.shape[0]
    tm = min(s, 512)

    def body(qc_ref, kv_ref, dqn_ref, dkp_ref, dv_ref, c_ref, s_ref, gq_ref, gkv_ref, dproj_ref, o_ref, red_ref):
        @pl.when(pl.program_id(0) == 0)
        def _():
            red_ref[...] = jnp.zeros_like(red_ref)

        qhat, qr = _rms(qc_ref[...].astype(F32), None)
        dq = dqn_ref[...]
        o_ref[:, 0:256] = _rms_bwd(dq, gq_ref[...], qhat, qr).astype(BF16)
        red_ref[0:1, :] += _colsum(dq * qhat)
        khat, kr = _rms(kv_ref[:, 0:128].astype(F32), None)
        dk = dkp_ref[:, 0:128] + dv_ref[...]
        o_ref[:, 256:384] = _rms_bwd(dk, gkv_ref[...], khat, kr).astype(BF16)
        red_ref[1:2, 0:128] += _colsum(dk * khat)
        dr = dkp_ref[:, 128:256]
        o_ref[:, 384:512] = (dr * c_ref[...] - _rope_pair_swap(dr) * s_ref[...]).astype(BF16)

    return pl.pallas_call(
        body, name=name, grid=(s // tm,),
        in_specs=[pl.BlockSpec((tm, 256), lambda i: (i, 0)), pl.BlockSpec((tm, 256), lambda i: (i, 1)),
                  pl.BlockSpec((tm, 256), lambda i: (i, 0)), pl.BlockSpec((tm, 256), lambda i: (i, 0)),
                  pl.BlockSpec((tm, 128), lambda i: (i, 0)),
                  pl.BlockSpec((tm, 128), lambda i: (i, 0)), pl.BlockSpec((tm, 128), lambda i: (i, 0)),
                  pl.BlockSpec((1, 256), lambda i: (0, 0)), pl.BlockSpec((1, 128), lambda i: (0, 0)),
                  pl.BlockSpec(memory_space=pl.ANY)],
        out_specs=[pl.BlockSpec((tm, 512), lambda i: (i, 0)), pl.BlockSpec((8, 256), lambda i: (0, 0))],
        out_shape=[jax.ShapeDtypeStruct(dproj.shape, BF16), jax.ShapeDtypeStruct((8, 256), F32)],
        input_output_aliases={9: 0}, compiler_params=_params(1),
    )(proj, proj, dqn, dkp, dv, cosp, sinp, gq, gkv, dproj)


HEADS_TM = 256
Z_COL0 = ODD_IN_PAD - MLA_WIDTH


def _head_cols(h):
    return slice(128 * h, 128 * h + 128)


def _q_heads(q_up, cosp, sinp, wuk, name):
    s = q_up.shape[0]
    tm = min(s, HEADS_TM)

    def body(q_ref, c_ref, s_ref, w_ref, o_ref):
        for p in range(MLA_HEADS // 2):
            raw = q_ref[:, MLA_WIDTH + 128 * p:MLA_WIDTH + 128 * (p + 1)].astype(F32)
            rot = raw * c_ref[...] + _rope_pair_swap(raw) * s_ref[...]
            low = _lane_lt(rot.shape, 64)
            o_ref[2 * p, :, 128:256] = jnp.where(low, rot, 0.0).astype(BF16)
            o_ref[2 * p + 1, :, 128:256] = jnp.where(low, pltpu.roll(rot, 64, 1), 0.0).astype(BF16)
        for h in range(MLA_HEADS):
            o_ref[h, :, 0:128] = _dot(q_ref[:, _head_cols(h)], w_ref[:, _head_cols(h)], NT).astype(BF16)

    return pl.pallas_call(
        body, name=name, grid=(s // tm,),
        in_specs=[pl.BlockSpec((tm, 3072), lambda i: (i, 0)),
                  pl.BlockSpec((tm, 128), lambda i: (i, 0)), pl.BlockSpec((tm, 128), lambda i: (i, 0)),
                  pl.BlockSpec((128, MLA_WIDTH), lambda i: (0, 0))],
        out_specs=pl.BlockSpec((MLA_HEADS, tm, 256), lambda i: (0, i, 0)),
        out_shape=jax.ShapeDtypeStruct((MLA_HEADS, s, 256), BF16), compiler_params=_params(1),
    )(q_up, cosp, sinp, wuk)


def _q_heads_bwd(dqp, q_up, cosp, sinp, wuk, name):
    s = q_up.shape[0]
    tm = min(s, HEADS_TM)

    def body(dq_ref, qn_ref, c_ref, s_ref, w_ref, dn_ref, dr_ref, dw_ref):
        @pl.when(pl.program_id(0) == 0)
        def _():
            dw_ref[...] = jnp.zeros_like(dw_ref)

        for h in range(MLA_HEADS):
            dlat = dq_ref[h, :, 0:128]
            dn_ref[:, _head_cols(h)] = _dot(dlat, w_ref[:, _head_cols(h)], NN).astype(BF16)
            dw_ref[:, _head_cols(h)] += _dot(dlat, qn_ref[:, _head_cols(h)], TN)
        for p in range(MLA_HEADS // 2):
            drot = dq_ref[2 * p, :, 128:256].astype(F32) + pltpu.roll(dq_ref[2 * p + 1, :, 128:256].astype(F32), 64, 1)
            dr_ref[:, _head_cols(p)] = (drot * c_ref[...] - _rope_pair_swap(drot) * s_ref[...]).astype(BF16)

    return pl.pallas_call(
        body, name=name, grid=(s // tm,),
        in_specs=[pl.BlockSpec((MLA_HEADS, tm, 256), lambda i: (0, i, 0)),
                  pl.BlockSpec((tm, MLA_WIDTH), lambda i: (i, 0)),
                  pl.BlockSpec((tm, 128), lambda i: (i, 0)), pl.BlockSpec((tm, 128), lambda i: (i, 0)),
                  pl.BlockSpec((128, MLA_WIDTH), lambda i: (0, 0))],
        out_specs=[pl.BlockSpec((tm, MLA_WIDTH), lambda i: (i, 0)),
                   pl.BlockSpec((tm, 1024), lambda i: (i, 0)),
                   pl.BlockSpec((128, MLA_WIDTH), lambda i: (0, 0))],
        out_shape=[jax.ShapeDtypeStruct((s, MLA_WIDTH), BF16), jax.ShapeDtypeStruct((s, 1024), BF16),
                   jax.ShapeDtypeStruct((128, MLA_WIDTH), F32)],
        compiler_params=_params(1),
    )(dqp, q_up, cosp, sinp, wuk)


def _o_gate(o_lat, proj, wuv, name):
    s = o_lat.shape[1]
    tm = min(s, HEADS_TM)

    def body(ol_ref, p_ref, w_ref, g_ref):
        for h in range(MLA_HEADS):
            z = p_ref[:, Z_COL0 + 128 * h:Z_COL0 + 128 * (h + 1)].astype(F32)
            g_ref[:, _head_cols(h)] = (_dot(ol_ref[h], w_ref[:, _head_cols(h)], NN) * (z * _sigmoid(z))).astype(BF16)

    return pl.pallas_call(
        body, name=name, grid=(s // tm,),
        in_specs=[pl.BlockSpec((MLA_HEADS, tm, 128), lambda i: (0, i, 0)),
                  pl.BlockSpec((tm, ODD_IN_PAD), lambda i: (i, 0)),
                  pl.BlockSpec((128, MLA_WIDTH), lambda i: (0, 0))],
        out_specs=pl.BlockSpec((tm, MLA_WIDTH), lambda i: (i, 0)),
        out_shape=jax.ShapeDtypeStruct((s, MLA_WIDTH), BF16), compiler_params=_params(1),
    )(o_lat, proj, wuv)


def _o_gate_bwd(dg, o_lat, proj, wuv, name):
    s = o_lat.shape[1]
    tm = min(s, HEADS_TM)

    def body(dg_ref, ol_ref, p_ref, w_ref, dp_ref, dol_ref, dw_ref):
        @pl.when(pl.program_id(0) == 0)
        def _():
            dw_ref[...] = jnp.zeros_like(dw_ref)

        dp_ref[:, 0:Z_COL0] = jnp.zeros((tm, Z_COL0), BF16)
        for h in range(MLA_HEADS):
            zc = slice(Z_COL0 + 128 * h, Z_COL0 + 128 * (h + 1))
            z, dgv, ol = p_ref[:, zc].astype(F32), dg_ref[:, _head_cols(h)].astype(F32), ol_ref[h]
            sig = _sigmoid(z)
            o = _dot(ol, w_ref[:, _head_cols(h)], NN)
            dp_ref[:, zc] = (dgv * o * (sig * (1.0 + z * (1.0 - sig)))).astype(BF16)
            do = (dgv * (z * sig)).astype(BF16)
            dol_ref[h] = _dot(do, w_ref[:, _head_cols(h)], NT).astype(BF16)
            dw_ref[:, _head_cols(h)] += _dot(ol, do, TN)

    return pl.pallas_call(
        body, name=name, grid=(s // tm,),
        in_specs=[pl.BlockSpec((tm, MLA_WIDTH), lambda i: (i, 0)),
                  pl.BlockSpec((MLA_HEADS, tm, 128), lambda i: (0, i, 0)),
                  pl.BlockSpec((tm, ODD_IN_PAD), lambda i: (i, 0)),
                  pl.BlockSpec((128, MLA_WIDTH), lambda i: (0, 0))],
        out_specs=[pl.BlockSpec((tm, ODD_IN_PAD), lambda i: (i, 0)),
                   pl.BlockSpec((MLA_HEADS, tm, 128), lambda i: (0, i, 0)),
                   pl.BlockSpec((128, MLA_WIDTH), lambda i: (0, 0))],
        out_shape=[jax.ShapeDtypeStruct((s, ODD_IN_PAD), BF16), jax.ShapeDtypeStruct((MLA_HEADS, s, 128), BF16),
                   jax.ShapeDtypeStruct((128, MLA_WIDTH), F32)],
        compiler_params=_params(1),
    )(dg, o_lat, proj, wuv)


ATT_TQ = CHUNK
ATT_ROWS = ATT_TQ * MLA_HEADS
ATT_TK = 512
ATT_HEAD_GROUP = 8


def _visible(k0, q_chunk, tk):
    kpos = k0 + lax.broadcasted_iota(jnp.int32, (1, tk), 1)
    return kpos // CHUNK <= q_chunk


def _tile_lanes(t, n):
    return jnp.concatenate([t] * (n // 128), axis=1)


def _key_blocks(i, tk, block, pairs=False):
    visible = i * ATT_TQ + ATT_TQ
    n_full = (visible + tk - 1) // tk - 1

    def full(j):
        block(pl.multiple_of(j * tk, tk), tk, False)

    if pairs:
        def two(jj, carry):
            full(2 * jj)
            full(2 * jj + 1)
            return carry

        lax.fori_loop(0, n_full // 2, two, 0)

        @pl.when(n_full % 2 == 1)
        def _():
            full(n_full - 1)
    else:
        def one(j, carry):
            full(j)
            return carry

        lax.fori_loop(0, n_full, one, 0)
    last0 = pl.multiple_of(n_full * tk, tk)
    half = tk // 2
    if half % 128 == 0:
        @pl.when(visible - n_full * tk <= half)
        def _():
            block(last0, half, True)

        @pl.when(visible - n_full * tk > half)
        def _():
            block(last0, tk, True)
    else:
        block(last0, tk, True)


def _attn_fwd(qp, kp, name, side=None):
    s = kp.shape[0]
    tk = min(ATT_TK, s)

    def body(q_ref, k_ref, o_ref, lse_ref, m_sc, acc_sc):
        i = pl.program_id(0)
        m_sc[...] = jnp.full_like(m_sc, -jnp.inf)
        acc_sc[...] = jnp.zeros_like(acc_sc)

        def block(k0, width, masked):
            k = k_ref[pl.ds(k0, width), :]
            v1 = jnp.where(_lane_lt(k.shape, 128), k, jnp.ones_like(k))
            for h0 in range(0, MLA_HEADS, ATT_HEAD_GROUP):
                rows = slice(h0 * ATT_TQ, (h0 + ATT_HEAD_GROUP) * ATT_TQ)
                q = q_ref[h0:h0 + ATT_HEAD_GROUP].reshape(ATT_HEAD_GROUP * ATT_TQ, 256)
                sc = _dot(q, k, NT) * ATTN_SCALE_LOG2
                if masked:
                    sc = jnp.where(_visible(k0, i, width), sc, -jnp.inf)
                m_prev = m_sc[rows]
                m_new = jnp.maximum(m_prev, jnp.max(sc, axis=1, keepdims=True))
                p = jnp.exp2(sc - _tile_lanes(m_new, width))
                acc_sc[rows] = _tile_lanes(jnp.exp2(m_prev - m_new), 256) * acc_sc[rows] + _dot(p, v1, NN)
                m_sc[rows] = m_new

        _key_blocks(i, tk, block, pairs=True)
        acc = acc_sc[...]
        l = acc[:, 128:256]
        o_ref[...] = (acc[:, 0:128] / l).astype(BF16).reshape(MLA_HEADS, ATT_TQ, 128)
        lse_ref[...] = (m_sc[...] + jnp.log2(l)).reshape(MLA_HEADS, ATT_TQ, 128)

    head128 = pl.BlockSpec((MLA_HEADS, ATT_TQ, 128), lambda i: (0, i, 0))
    return _grid_call(
        body, name, (s // ATT_TQ,),
        [pl.BlockSpec((MLA_HEADS, ATT_TQ, 256), lambda i: (0, i, 0)), pl.BlockSpec((s, 256), lambda i: (0, 0))],
        [head128, head128],
        [jax.ShapeDtypeStruct((MLA_HEADS, s, 128), BF16), jax.ShapeDtypeStruct((MLA_HEADS, s, 128), F32)],
        [qp, kp], scratch=[pltpu.VMEM((ATT_ROWS, 128), F32), pltpu.VMEM((ATT_ROWS, 256), F32)], side=side)


def _attn_bwd(qp, kp, o, do, lse, name, side=None):
    s = kp.shape[0]
    tk = min(ATT_TK, s)

    def body(q_ref, k_ref, o_ref, do_ref, lse_ref, dq_ref, dk_ref, dv_ref, dq_sc):
        i = pl.program_id(0)

        @pl.when(i == 0)
        def _():
            dk_ref[...] = jnp.zeros_like(dk_ref)
            dv_ref[...] = jnp.zeros_like(dv_ref)

        q = q_ref[...].reshape(ATT_ROWS, 256)
        dov = do_ref[...].reshape(ATT_ROWS, 128)
        delta = jnp.sum(dov.astype(F32) * o_ref[...].reshape(ATT_ROWS, 128).astype(F32), axis=1, keepdims=True)
        delta_t = _tile_lanes(jnp.broadcast_to(delta, (ATT_ROWS, 128)), tk)
        lse_t = _tile_lanes(lse_ref[...].reshape(ATT_ROWS, 128), tk)
        dq_sc[...] = jnp.zeros_like(dq_sc)

        def block(k0, width, masked):
            k = k_ref[pl.ds(k0, width), :]
            p = jnp.exp2(_dot(q, k, NT) * ATTN_SCALE_LOG2 - lse_t[:, 0:width])
            if masked:
                p = jnp.where(_visible(k0, i, width), p, 0.0)
            dv_ref[pl.ds(k0, width), :] += _dot(p, dov, TN)
            ds = (p * (_dot(dov, k[:, 0:128], NT) - delta_t[:, 0:width]) * ATTN_SCALE).astype(BF16)
            dq_sc[...] += _dot(ds, k, NN)
            dk_ref[pl.ds(k0, width), :] += _dot(ds, q, TN)

        _key_blocks(i, tk, block, pairs=True)
        dq_ref[...] = dq_sc[...].astype(BF16).reshape(MLA_HEADS, ATT_TQ, 256)

    head128 = pl.BlockSpec((MLA_HEADS, ATT_TQ, 128), lambda i: (0, i, 0))
    head256 = pl.BlockSpec((MLA_HEADS, ATT_TQ, 256), lambda i: (0, i, 0))
    return _grid_call(
        body, name, (s // ATT_TQ,),
        [head256, pl.BlockSpec((s, 256), lambda i: (0, 0)), head128, head128, head128],
        [head256, pl.BlockSpec((s, 256), lambda i: (0, 0)), pl.BlockSpec((s, 128), lambda i: (0, 0))],
        [jax.ShapeDtypeStruct((MLA_HEADS, s, 256), BF16),
         jax.ShapeDtypeStruct((s, 256), F32), jax.ShapeDtypeStruct((s, 128), F32)],
        [qp, kp, o, do, lse], scratch=[pltpu.VMEM((ATT_ROWS, 256), F32)], side=side)


def _place():
    x, y, c = lax.axis_index("x"), lax.axis_index("y"), lax.axis_index("c")
    return x, y, c, 4 * x + 2 * y + c


def _flip(x, y, c, r):
    px = 1 - x if r & 4 else x
    py = 1 - y if r & 2 else y
    pc = 1 - c if r & 1 else c
    return (px, py, pc), 4 * px + 2 * py + pc


def _adaln_exchange(c8, ada_w, ada_b_cols, blocks, name):
    d = c8.shape[1]
    w_cols = ada_w.shape[2]
    n_arr = len(blocks)

    def body(c_ref, w_ref, b_ref, *refs):
        x_refs, (call_ref, mod_ref), out_refs = refs[:n_arr], refs[n_arr:n_arr + 2], refs[n_arr + 2:2 * n_arr + 2]
        sbuf, rbuf, s1, r1, s2, r2 = refs[2 * n_arr + 2:2 * n_arr + 8]
        gather = _Gather(x_refs, out_refs, *refs[2 * n_arr + 8:])
        x, y, c, me = _place()
        call_ref[pl.ds(pl.multiple_of(me * 8, 8), 8), :] = c_ref[...]
        peers = [_flip(x, y, c, r) for r in range(1, N_DEV)]

        def c_copy(k, src_lin, to):
            rows = call_ref.at[pl.ds(pl.multiple_of(src_lin * 8, 8), 8), :]
            return pltpu.make_async_remote_copy(src_ref=rows, dst_ref=rows, send_sem=s1.at[k], recv_sem=r1.at[k],
                                                device_id=to, device_id_type=MESH)

        first = [c_copy(k, me, peer) for k, (peer, _) in enumerate(peers)]
        for cp in first:
            cp.start()
        for k, (_, lin) in enumerate(peers):
            c_copy(k, lin, (x, y, c)).wait_recv()
        for cp in first:
            cp.wait_send()

        for j in range(N_DEV):
            cj = call_ref[8 * j:8 * j + 8, :]
            cond = cj * _sigmoid(cj)
            for l in range(2):
                sbuf[j, l] = lax.dot_general(cond, w_ref[l], NN, precision=lax.Precision.HIGHEST,
                                             preferred_element_type=F32) + b_ref[l]

        def m_copy(k, src_slot, dst_slot, to):
            return pltpu.make_async_remote_copy(src_ref=sbuf.at[src_slot], dst_ref=rbuf.at[dst_slot],
                                                send_sem=s2.at[k], recv_sem=r2.at[k], device_id=to,
                                                device_id_type=MESH)

        rbuf[me] = sbuf[me]
        second = [m_copy(k, lin, me, peer) for k, (peer, lin) in enumerate(peers)]
        for cp in second:
            cp.start()
        gather.start()
        for k, (_, lin) in enumerate(peers):
            m_copy(k, lin, lin, (x, y, c)).wait_recv()
        for cp in second:
            cp.wait_send()
        for j in range(N_DEV):
            for l in range(2):
                mod_ref[l, :, w_cols * j:w_cols * (j + 1)] = rbuf[j, l]
        gather.forward()
        gather.end()

    vmem = pl.BlockSpec(memory_space=pltpu.VMEM)
    anyspace = pl.BlockSpec(memory_space=pl.ANY)
    g_shapes, g_sems = _gather_extras(blocks)
    return pl.pallas_call(
        body, name=name, in_specs=[vmem, vmem, vmem] + [anyspace] * n_arr, out_specs=[vmem, vmem] + [anyspace] * n_arr,
        out_shape=[jax.ShapeDtypeStruct((8 * N_DEV, d), F32), jax.ShapeDtypeStruct((2, 8, 3 * d), F32)] + g_shapes,
        scratch_shapes=[pltpu.VMEM((N_DEV, 2, 8, w_cols), F32), pltpu.VMEM((N_DEV, 2, 8, w_cols), F32),
                        pltpu.SemaphoreType.DMA((N_DEV - 1,)), pltpu.SemaphoreType.DMA((N_DEV - 1,)),
                        pltpu.SemaphoreType.DMA((N_DEV - 1,)), pltpu.SemaphoreType.DMA((N_DEV - 1,))] + g_sems,
        compiler_params=pltpu.CompilerParams(vmem_limit_bytes=VMEM_LIMIT),
    )(c8, ada_w, ada_b_cols, *blocks)


class _Gather:
    def __init__(self, x_refs, out_refs, send_sems, recv_sems, local_sems):
        x, y, c, _ = _place()
        self.me, self.sibling, self.c = (x, y, c), (x, y, 1 - c), c
        self.chips = [(1 - x, y), (x, 1 - y), (1 - x, 1 - y)]
        self.n_arr = len(x_refs)
        self.out_refs, self.send_sems, self.recv_sems = out_refs, send_sems, recv_sems
        self.mine = [pltpu.make_async_copy(x_refs[t], out_refs[t].at[4 * x + 2 * y + c], local_sems.at[t])
                     for t in range(self.n_arr)]
        self.first = []
        for t in range(self.n_arr):
            self.first.append(self.copy(t, 0, self.me, self.sibling, src=x_refs[t]))
            self.first += [self.copy(t, 1 + j, self.me, (*chip, c), src=x_refs[t]) for j, chip in enumerate(self.chips)]
        self.passed = [self.copy(t, 4 + j, (*chip, c), self.sibling)
                       for t in range(self.n_arr) for j, chip in enumerate(self.chips)]

    def copy(self, t, k, blk, to, src=None):
        slot = self.out_refs[t].at[4 * blk[0] + 2 * blk[1] + blk[2]]
        return pltpu.make_async_remote_copy(src_ref=slot if src is None else src, dst_ref=slot,
                                            send_sem=self.send_sems.at[7 * t + k], recv_sem=self.recv_sems.at[7 * t + k],
                                            device_id=to, device_id_type=MESH)

    def start(self):
        for cp in self.mine + self.first:
            cp.start()

    def forward(self):
        for t in range(self.n_arr):
            for j, chip in enumerate(self.chips):
                self.copy(t, 1 + j, (*chip, self.c), self.me).wait_recv()
                self.passed[3 * t + j].start()

    def end(self):
        for t in range(self.n_arr):
            self.copy(t, 0, self.sibling, self.me).wait_recv()
            for j, chip in enumerate(self.chips):
                self.copy(t, 4 + j, (*chip, 1 - self.c), self.me).wait_recv()
        for cp in self.first + self.passed:
            cp.wait_send()
        for cp in self.mine:
            cp.wait()


def _gather_extras(blocks):
    n_arr = len(blocks)
    return ([jax.ShapeDtypeStruct((N_DEV,) + b.shape, b.dtype) for b in blocks],
            [pltpu.SemaphoreType.DMA((7 * n_arr,)), pltpu.SemaphoreType.DMA((7 * n_arr,)),
             pltpu.SemaphoreType.DMA((n_arr,))])


def _all_gather(blocks, name):
    n_arr = len(blocks)

    def body(*refs):
        gather = _Gather(refs[:n_arr], refs[n_arr:2 * n_arr], *refs[2 * n_arr:])
        gather.start()
        gather.forward()
        gather.end()

    anyspace = pl.BlockSpec(memory_space=pl.ANY)
    shapes, sems = _gather_extras(blocks)
    return pl.pallas_call(body, name=name, in_specs=[anyspace] * n_arr, out_specs=[anyspace] * n_arr,
                          out_shape=shapes, scratch_shapes=sems)(*blocks)


def _scatter_parts(parts, name):
    n_arr = len(parts)

    def body(*refs):
        copies = _exchange_copies(refs[:n_arr], refs[n_arr:2 * n_arr], *refs[2 * n_arr:], "devices")
        _exchange_start(copies)
        _exchange_wait(copies)

    anyspace = pl.BlockSpec(memory_space=pl.ANY)
    shapes, sems = _exchange_extras(parts, "devices")
    return pl.pallas_call(body, name=name, in_specs=[anyspace] * n_arr, out_specs=[anyspace] * n_arr,
                          out_shape=shapes, scratch_shapes=sems)(*parts)


def _sum_parts(parts, name):
    def body(p_ref, g_ref):
        g = p_ref[0]
        for j in range(1, N_DEV):
            g = g + p_ref[j]
        g_ref[...] = g

    return pl.pallas_call(body, name=name, out_shape=jax.ShapeDtypeStruct(parts.shape[1:], F32),
                          compiler_params=pltpu.CompilerParams(vmem_limit_bytes=VMEM_LIMIT))(parts)


N_CHIPS = N_DEV // 2


def _sibling_swap(part, name):
    def body(g_ref, r_ref, send_sems, recv_sems):
        x, y, c, _ = _place()
        sends = [pltpu.make_async_remote_copy(
            src_ref=g_ref.at[2 * q + 1 - c], dst_ref=r_ref.at[q], send_sem=send_sems.at[q], recv_sem=recv_sems.at[q],
            device_id=(x, y, 1 - c), device_id_type=MESH) for q in range(N_CHIPS)]
        recvs = [pltpu.make_async_remote_copy(
            src_ref=g_ref.at[2 * q + c], dst_ref=r_ref.at[q], send_sem=send_sems.at[q], recv_sem=recv_sems.at[q],
            device_id=(x, y, c), device_id_type=MESH) for q in range(N_CHIPS)]
        for cp in sends:
            cp.start()
        for cp in recvs:
            cp.wait_recv()
        for cp in sends:
            cp.wait_send()

    anyspace = pl.BlockSpec(memory_space=pl.ANY)
    return pl.pallas_call(
        body, name=name, in_specs=[anyspace], out_specs=anyspace,
        out_shape=jax.ShapeDtypeStruct((N_CHIPS,) + part.shape[1:], part.dtype),
        scratch_shapes=[pltpu.SemaphoreType.DMA((N_CHIPS,)), pltpu.SemaphoreType.DMA((N_CHIPS,))])(part)


def _pair_sum(a, b, name):
    n, rows, cols = a.shape
    tr = max(t for t in range(16, 513, 16) if rows % t == 0)

    def body(a_ref, b_ref, o_ref):
        o_ref[...] = (a_ref[...].astype(F32) + b_ref[...].astype(F32)).astype(o_ref.dtype)

    blk = pl.BlockSpec((None, tr, cols), lambda q, i: (q, i, 0))
    return pl.pallas_call(body, name=name, grid=(n, rows // tr), in_specs=[blk, blk], out_specs=blk,
                          out_shape=jax.ShapeDtypeStruct(a.shape, a.dtype), compiler_params=_params(2))(a, b)


def _adamw(w, g, m, v):
    m = ADAM_B1 * m + (1.0 - ADAM_B1) * g
    v = ADAM_B2 * v + (1.0 - ADAM_B2) * (g * g)
    m_hat = m / (1.0 - ADAM_B1 ** ADAM_STEP)
    v_hat = v / (1.0 - ADAM_B2 ** ADAM_STEP)
    return -ADAM_LR * (m_hat / (jnp.sqrt(v_hat) + ADAM_EPS) + ADAM_WD * w), m, v


def _sum_parts_adamw(parts, w, m, v, name):
    n_parts, rows, cols = parts.shape
    tr = max(t for t in range(16, 129, 16) if rows % t == 0)

    def body(p_ref, w_ref, m_ref, v_ref, g_ref, d_ref, mo_ref, vo_ref):
        g = p_ref[0].astype(F32)
        for j in range(1, n_parts):
            g = g + p_ref[j].astype(F32)
        g_ref[...] = g
        d_ref[...], mo_ref[...], vo_ref[...] = _adamw(w_ref[...], g, m_ref[...], v_ref[...])

    row = _rows3(tr, cols)
    out = jax.ShapeDtypeStruct((1, rows, cols), F32)
    return pl.pallas_call(
        body, name=name, grid=(rows // tr,),
        in_specs=[pl.BlockSpec((n_parts, tr, cols), lambda i: (0, i, 0)), row, row, row],
        out_specs=[row, row, row, row], out_shape=[out, out, out, out], compiler_params=_params(1),
    )(parts, w, m, v)


def _sum_parts_adamw_whole(parts, w, m, v, name):
    def body(p_ref, w_ref, m_ref, v_ref, g_ref, d_ref, mo_ref, vo_ref):
        g = p_ref[0:1].astype(F32)
        for j in range(1, N_DEV):
            g = g + p_ref[j:j + 1].astype(F32)
        g_ref[...] = g
        d_ref[...], mo_ref[...], vo_ref[...] = _adamw(w_ref[...], g, m_ref[...], v_ref[...])

    out = jax.ShapeDtypeStruct(w.shape, F32)
    return pl.pallas_call(body, name=name, out_shape=[out] * 4,
                          compiler_params=pltpu.CompilerParams(vmem_limit_bytes=VMEM_LIMIT))(parts, w, m, v)


def _adamw_many(gs, ws, ms, vs, name):
    n = len(gs)

    def body(*refs):
        for k in range(n):
            g_ref, w_ref, m_ref, v_ref = (refs[q * n + k] for q in range(4))
            d_ref, mo_ref, vo_ref = (refs[(4 + q) * n + k] for q in range(3))
            d_ref[...], mo_ref[...], vo_ref[...] = _adamw(w_ref[...], g_ref[...], m_ref[...], v_ref[...])

    out = [jax.ShapeDtypeStruct(w.shape, F32) for w in ws]
    res = pl.pallas_call(body, name=name, out_shape=out * 3,
                         compiler_params=pltpu.CompilerParams(vmem_limit_bytes=VMEM_LIMIT))(*gs, *ws, *ms, *vs)
    return res[:n], res[n:2 * n], res[2 * n:]


def _ada_w_grad_adamw(c_all, dmod_rows, w, m, v, name):
    def body(c_ref, dm_ref, w_ref, m_ref, v_ref, g_ref, d_ref, mo_ref, vo_ref):
        cv = c_ref[...]
        cond = cv * _sigmoid(cv)
        for l in range(2):
            g = lax.dot_general(cond, dm_ref[l], TN, precision=lax.Precision.HIGHEST, preferred_element_type=F32)
            g_ref[l] = g
            d_ref[l], mo_ref[l], vo_ref[l] = _adamw(w_ref[l], g, m_ref[l], v_ref[l])

    out = jax.ShapeDtypeStruct(w.shape, F32)
    return pl.pallas_call(
        body, name=name, out_shape=[out] * 4, compiler_params=pltpu.CompilerParams(vmem_limit_bytes=VMEM_LIMIT),
    )(c_all, dmod_rows, w, m, v)


REPLICATED = ("ln_g", "ln_b", "gmlp_norm_g", "gmlp_norm_b", "gmlp_ws", "gmlp_bs", "pool_b", "pool_scale",
              "mla_kv_norm_g", "mla_w_uk", "mla_w_uv")
CHUNK_ROWS, ADA_ROW, QNORM_ROW, LOSS_ROW, REP_ROWS = 73, 73, 74, 75, 80
UQ_ROWS, POOLW_ROWS = 96, 32


def _pad_rows(flat2d, rows):
    n, k = flat2d.shape
    return jnp.pad(flat2d, ((0, 0), (0, rows * LANES - k))).reshape(n, rows, LANES)


def _ada_cols_rows(vec):
    return _pad_rows(vec.reshape(2, N_DEV, -1).transpose(1, 0, 2).reshape(N_DEV, -1), 1)


def _unpack_replicated(rep, shapes):
    chunk = sum(s[1] for s in shapes) // N_DEV
    flat, off, out = rep[:, :CHUNK_ROWS].reshape(N_DEV, -1)[:, :chunk].reshape(-1), 0, {}
    for n, size, shape in shapes:
        out[n] = flat[off:off + size].reshape(shape)
        off += size
    cols = 3 * D_MODEL // N_DEV
    out["ada_b"] = rep[:, ADA_ROW, :2 * cols].reshape(N_DEV, 2, cols).transpose(1, 0, 2).reshape(2, -1)
    return out


def kernel(x, c, positions, ada_w, ada_b, ln_g, ln_b, e_w_in, gmlp_norm_g, gmlp_norm_b, gmlp_ws, gmlp_bs, pool_w, pool_b, pool_scale, e_w_out, o_w_in, mla_q_norm_g, mla_kv_norm_g, mla_w_uq, mla_w_uk, mla_w_uv, o_w_out, loss_target, m_ada_w, m_ada_b, m_ln_g, m_ln_b, m_e_w_in, m_gmlp_norm_g, m_gmlp_norm_b, m_gmlp_ws, m_gmlp_bs, m_pool_w, m_pool_b, m_pool_scale, m_e_w_out, m_o_w_in, m_mla_q_norm_g, m_mla_kv_norm_g, m_mla_w_uq, m_mla_w_uk, m_mla_w_uv, m_o_w_out, v_ada_w, v_ada_b, v_ln_g, v_ln_b, v_e_w_in, v_gmlp_norm_g, v_gmlp_norm_b, v_gmlp_ws, v_gmlp_bs, v_pool_w, v_pool_b, v_pool_scale, v_e_w_out, v_o_w_in, v_mla_q_norm_g, v_mla_kv_norm_g, v_mla_w_uq, v_mla_w_uk, v_mla_w_uv, v_o_w_out):
    w_in = dict(ada_w=ada_w, ada_b=ada_b, ln_g=ln_g, ln_b=ln_b, e_w_in=e_w_in, gmlp_norm_g=gmlp_norm_g,
                gmlp_norm_b=gmlp_norm_b, gmlp_ws=gmlp_ws, gmlp_bs=gmlp_bs, pool_w=pool_w, pool_b=pool_b,
                pool_scale=pool_scale, e_w_out=e_w_out, o_w_in=o_w_in, mla_q_norm_g=mla_q_norm_g,
                mla_kv_norm_g=mla_kv_norm_g, mla_w_uq=mla_w_uq, mla_w_uk=mla_w_uk, mla_w_uv=mla_w_uv, o_w_out=o_w_out)
    m_in = dict(ada_w=m_ada_w, ada_b=m_ada_b, ln_g=m_ln_g, ln_b=m_ln_b, e_w_in=m_e_w_in, gmlp_norm_g=m_gmlp_norm_g,
                gmlp_norm_b=m_gmlp_norm_b, gmlp_ws=m_gmlp_ws, gmlp_bs=m_gmlp_bs, pool_w=m_pool_w, pool_b=m_pool_b,
                pool_scale=m_pool_scale, e_w_out=m_e_w_out, o_w_in=m_o_w_in, mla_q_norm_g=m_mla_q_norm_g,
                mla_kv_norm_g=m_mla_kv_norm_g, mla_w_uq=m_mla_w_uq, mla_w_uk=m_mla_w_uk, mla_w_uv=m_mla_w_uv,
                o_w_out=m_o_w_out)
    v_in = dict(ada_w=v_ada_w, ada_b=v_ada_b, ln_g=v_ln_g, ln_b=v_ln_b, e_w_in=v_e_w_in, gmlp_norm_g=v_gmlp_norm_g,
                gmlp_norm_b=v_gmlp_norm_b, gmlp_ws=v_gmlp_ws, gmlp_bs=v_gmlp_bs, pool_w=v_pool_w, pool_b=v_pool_b,
                pool_scale=v_pool_scale, e_w_out=v_e_w_out, o_w_in=v_o_w_in, mla_q_norm_g=v_mla_q_norm_g,
                mla_kv_norm_g=v_mla_kv_norm_g, mla_w_uq=v_mla_w_uq, mla_w_uk=v_mla_w_uk, mla_w_uv=v_mla_w_uv,
                o_w_out=v_o_w_out)
    names = list(w_in)
    seq = x.shape[1]
    d = D_MODEL
    me = 4 * lax.axis_index("x") + 2 * lax.axis_index("y") + lax.axis_index("c")
    ada_cols = ada_w.shape[2]

    ada_b_cols = lax.dynamic_slice_in_dim(ada_b, me * ada_cols, ada_cols, axis=1)
    slab_row = lax.broadcasted_iota(jnp.int32, (8, d), 0)
    slab = jnp.where(slab_row == 0, c, jnp.where(slab_row == 1, jnp.pad(mla_q_norm_g, ((0, 0), (0, d - 32))), 0.0))
    c_all, mod, w_in_e3, pool_w3 = _adaln_exchange(
        slab, ada_w, jnp.broadcast_to(ada_b_cols[:, None, :], (2, 8, ada_cols)),
        [e_w_in[0].astype(BF16), pool_w.astype(BF16).reshape(POOLW_ROWS, LANES)], "adaln_exchange")
    h0 = _modulate(x, mod[0], "modulate0")
    proj0, o_in3 = _matmul_cols_nn(h0, w_in_e3, BF16, 512, "even_in", side=([o_w_in[0].astype(BF16)], "gather"))
    o_in_full = o_in3.transpose(1, 0, 2).reshape(d, ODD_IN)
    w_in_o = jnp.concatenate([o_in_full[:, :448], jnp.zeros((d, 64), BF16), o_in_full[:, 448:]], axis=1)
    pool_w_full = pool_w3.reshape(N_DEV, 4, 32, 256).transpose(1, 0, 2, 3).reshape(4, 256, 256)
    g_q = c_all.reshape(N_DEV, 8, d)[:, 1, :32].reshape(1, MLA_Q_RANK)

    ws, bs_col = gmlp_ws[0], gmlp_bs[0].reshape(GMLP_HEADS, GMLP_BLOCK, 1)
    wuk2, wuv2 = mla_w_uk[0].reshape(MLA_KV_RANK, -1), mla_w_uv[0].reshape(MLA_KV_RANK, -1)
    inv = 1.0 / (ROPE_THETA ** (jnp.arange(0, MLA_ROPE, 2, dtype=F32) / MLA_ROPE))
    ang = positions[0].astype(F32)[:, None] * inv
    cosp = jnp.tile(jnp.cos(ang), (1, 4))
    sinp = jnp.tile(jnp.concatenate([-jnp.sin(ang), jnp.sin(ang)], axis=1), (1, 2))

    mix0, w_out_e3 = _even_fwd(proj0, ws, bs_col, gmlp_norm_g, gmlp_norm_b, pool_w_full, pool_b, pool_scale, "even_mix",
                               side=([e_w_out[0].astype(BF16)], "gather"))
    w_out_e = w_out_e3.reshape(-1, d)
    y0, uq3 = _matmul([(mix0, w_out_e)], "nn", F32, seq, d, 512, 1024, "even_out",
                      side=([mla_w_uq.astype(BF16).reshape(UQ_ROWS, LANES)], "gather"))
    uq_full = uq3.reshape(MLA_Q_RANK, MLA_HEADS, MLA_NOPE + MLA_ROPE)
    w_uq_n = uq_full[:, :, :MLA_NOPE].reshape(MLA_Q_RANK, -1)
    w_uq_r = uq_full[:, :, MLA_NOPE:].reshape(MLA_Q_RANK, -1)
    w_uq = jnp.concatenate([w_uq_n, w_uq_r], axis=1)
    x1, h1 = _resid_ln(x, y0, mod[0], ln_g[0:1], ln_b[0:1], mod[1], "resid_ln0")

    (proj1,) = _matmul([(h1, w_in_o)], "nn", BF16, seq, ODD_IN_PAD, 512, ODD_IN_PAD, "odd_in")
    qn, kp = _mla_prep(proj1, cosp, sinp, g_q, mla_kv_norm_g, "mla_prep")
    (q_up,) = _matmul([(qn, w_uq)], "nn", BF16, seq, 3072, 512, 3072, "q_up")
    qp = _q_heads(q_up, cosp, sinp, wuk2, "q_heads")
    o_lat, lse, w_out_o3 = _attn_fwd(qp, kp, "attn_fwd", side=([o_w_out[0].astype(BF16)], "gather"))
    w_out_o = w_out_o3.reshape(-1, d)
    gated = _o_gate(o_lat, proj1, wuv2, "o_gate")
    (y1,) = _matmul([(gated, w_out_o)], "nn", F32, seq, d, 512, 1024, "odd_out")

    dy1, dxres1, red2 = _final_ln_loss_bwd(x1, y1, mod[1], ln_g[1:2], ln_b[1:2], loss_target, "final_ln_loss")
    (dgated,) = _matmul([(dy1, w_out_o)], "nt", BF16, seq, MLA_WIDTH, 512, MLA_WIDTH, "odd_out_dx")
    (g_w_out_o,) = _matmul([(gated, dy1)], "tn", BF16, MLA_WIDTH, d, 256, d, "odd_out_dw")
    dproj1_z, do_lat, g_wuv = _o_gate_bwd(dgated, o_lat, proj1, wuv2, "o_gate_bwd")
    dqp, dkp, dvv, r_o_out = _attn_bwd(qp, kp, o_lat, do_lat, lse, "attn_bwd",
                                       side=([g_w_out_o.reshape(N_DEV, -1, d)], "devices"))
    dq_nope, dq_rope, g_wuk = _q_heads_bwd(dqp, q_up, cosp, sinp, wuk2, "q_heads_bwd")
    (dqn,) = _matmul([(dq_nope, w_uq_n), (dq_rope, w_uq_r)], "nt", F32, seq, MLA_Q_RANK, 512, 256, "q_up_dx")
    (g_wuq_n,) = _matmul([(qn, dq_nope)], "tn", F32, MLA_Q_RANK, MLA_WIDTH, 256, MLA_WIDTH, "q_up_dw_nope")
    (g_wuq_r,) = _matmul([(qn, dq_rope)], "tn", F32, MLA_Q_RANK, 1024, 256, 1024, "q_up_dw_rope")
    dproj1, red_mla = _mla_prep_bwd(proj1, dqn, dkp, dvv, cosp, sinp, g_q, mla_kv_norm_g, dproj1_z, "mla_prep_bwd")
    (dh1,) = _matmul([(dproj1, w_in_o)], "nt", F32, seq, d, 512, d, "odd_in_dx")
    part_uq = jnp.concatenate([g_wuq_n.reshape(MLA_Q_RANK, MLA_HEADS, MLA_NOPE),
                               g_wuq_r.reshape(MLA_Q_RANK, MLA_HEADS, MLA_ROPE)], axis=2).astype(BF16).reshape(
                                   (N_DEV,) + mla_w_uq.shape[1:])
    (g_w_in_o,) = _matmul([(h1, dproj1)], "tn", BF16, d, ODD_IN_PAD, 256, ODD_IN_PAD // 2, "odd_in_dw", n_outer=True)
    part_o_in = jnp.concatenate([g_w_in_o[:, :448], g_w_in_o[:, 512:]], axis=1).reshape(d, N_DEV, -1).transpose(1, 0, 2)
    dy0, dxres0, red1 = _mid_bwd(dh1, dxres1, x, y0, mod[0], mod[1], ln_g[0:1], ln_b[0:1], "mid_bwd")
    dmix, r_uq = _matmul([(dy0, w_out_e)], "nt", BF16, seq, 2048, 512, 2048, "even_out_dx", side=([part_uq], "devices"))
    (g_w_out_e,) = _matmul([(mix0, dy0)], "tn", BF16, 2048, d, 256, d, "even_out_dw")
    dproj0, g_ws, g_bs, g_ng, g_nb, g_pw, g_pb, g_ps, r_o_in = _even_bwd(
        proj0, dmix, ws, bs_col, gmlp_norm_g, gmlp_norm_b, pool_w_full, pool_b, pool_scale, "even_mix_bwd",
        side=([part_o_in], "devices"))
    part_pw = g_pw.reshape(4, N_DEV, 32, 256).transpose(1, 0, 2, 3)
    part_e_in, r_e_out, r_pw = _matmul_cols_tn(h0, dproj0, w_in_e3.shape[2], BF16, 512, "even_in_dw",
                                               side=([g_w_out_e.reshape(N_DEV, -1, d), part_pw], "devices"))
    mine = lax.dynamic_index_in_dim(part_e_in.reshape((N_CHIPS, 2) + part_e_in.shape[1:]), lax.axis_index("c"), 1, False)
    chip_e_in = _pair_sum(mine, _sibling_swap(part_e_in, "e_in_sibling_swap"), "e_in_pair_sum")
    grad_x, red0, r_e_in = _first_bwd(dproj0, w_in_e3, dxres0, x, mod[0], "even_in_dx", side=([chip_e_in], "chips"))

    t_mask = lax.broadcasted_iota(jnp.int32, (GMLP_BLOCK, GMLP_BLOCK), 0) // CHUNK
    s_mask = lax.broadcasted_iota(jnp.int32, (GMLP_BLOCK, GMLP_BLOCK), 1) // CHUNK
    part = {
        "ln_g": jnp.stack([red1[2], red2[0]]), "ln_b": jnp.stack([red1[3], red2[1]]),
        "gmlp_norm_g": g_ng, "gmlp_norm_b": g_nb,
        "gmlp_ws": jnp.where(s_mask <= t_mask, g_ws, 0.0), "gmlp_bs": g_bs,
        "pool_b": g_pb, "pool_scale": g_ps, "mla_kv_norm_g": red_mla[1, :MLA_KV_RANK],
        "mla_w_uk": g_wuk, "mla_w_uv": g_wuv,
    }
    dmod = jnp.stack([jnp.concatenate([red0[1], red0[0], red1[4]]),
                      jnp.concatenate([red1[1], red1[0], red2[2]])])

    loss_row = jnp.pad(jnp.broadcast_to((0.5 / d * jnp.sum(red2[3])).reshape(1, 1, 1), (N_DEV, 1, 1)),
                       ((0, 0), (0, 0), (0, LANES - 1)))
    part_small = jnp.concatenate([
        _pad_rows(jnp.concatenate([part[n].reshape(-1) for n in REPLICATED]).reshape(N_DEV, -1), CHUNK_ROWS),
        jnp.pad(jnp.concatenate([_ada_cols_rows(dmod), _pad_rows(red_mla[0].reshape(N_DEV, -1), 1), loss_row], axis=1),
                ((0, 0), (0, REP_ROWS - LOSS_ROW - 1), (0, 0)))], axis=1)
    (r_small,) = _scatter_parts([part_small], "grad_scatter")
    small_sum = _sum_parts(r_small, "small_sum")
    loss = small_sum[LOSS_ROW, 0]
    (rep_sum,) = _all_gather([small_sum], "replicated_gather")

    res = {"e_w_in": _sum_parts_adamw(r_e_in, e_w_in, m_e_w_in, v_e_w_in, "adamw_e_w_in"),
           "o_w_in": _sum_parts_adamw(r_o_in, o_w_in, m_o_w_in, v_o_w_in, "adamw_o_w_in"),
           "e_w_out": _sum_parts_adamw(r_e_out, e_w_out, m_e_w_out, v_e_w_out, "adamw_e_w_out"),
           "o_w_out": _sum_parts_adamw(r_o_out, o_w_out, m_o_w_out, v_o_w_out, "adamw_o_w_out"),
           "mla_w_uq": _sum_parts_adamw_whole(r_uq, mla_w_uq, m_mla_w_uq, v_mla_w_uq, "adamw_w_uq"),
           "pool_w": _sum_parts_adamw_whole(r_pw, pool_w, m_pool_w, v_pool_w, "adamw_pool_w")}
    grads = _unpack_replicated(rep_sum, [(n, w_in[n].size, w_in[n].shape) for n in REPLICATED])
    grads["mla_q_norm_g"] = small_sum[QNORM_ROW:QNORM_ROW + 1, :32]
    small_names = list(grads)
    deltas, new_ms, new_vs = _adamw_many([grads[n] for n in small_names], [w_in[n] for n in small_names],
                                         [m_in[n] for n in small_names], [v_in[n] for n in small_names], "small_adamw")
    for k, n in enumerate(small_names):
        res[n] = [grads[n], deltas[k], new_ms[k], new_vs[k]]
    dmod_all = r_small[:, ADA_ROW, :2 * ada_cols].reshape(N_DEV, 2, ada_cols).transpose(1, 0, 2)
    dmod_rows = jnp.pad(dmod_all[:, :, None, :], ((0, 0), (0, 0), (0, 7), (0, 0))).reshape(2, 8 * N_DEV, ada_cols)
    res["ada_w"] = _ada_w_grad_adamw(c_all, dmod_rows, ada_w, m_ada_w, v_ada_w, "ada_w_adamw")

    return (loss, grad_x, *[res[n][0] for n in names], *[res[n][1] for n in names],
            *[res[n][2] for n in names], *[res[n][3] for n in names])
```

```python
import functools

import jax
import jax.numpy as jnp
from jax import lax
from jax.experimental import pallas as pl
from jax.experimental.pallas import tpu as pltpu

F32 = jnp.float32
BF16 = jnp.bfloat16

D_MODEL = 1024
CHUNK = 64
LN_EPS = 1e-5
GMLP_HEADS = 4
GMLP_HEAD_DIM = 256
GMLP_BLOCK = 128
POOL_WINDOWS = (2, 4, 8, 16)
POOL_GROUP_DIM = 256
POOL_HALO = 16
MLA_HEADS = 16
MLA_NOPE = 128
MLA_ROPE = 64
MLA_Q_RANK = 256
MLA_KV_RANK = 128
MLA_WIDTH = 2048
ODD_IN = 2496
ODD_IN_PAD = 2560
ROPE_THETA = 10000.0
ATTN_SCALE = (MLA_NOPE + MLA_ROPE) ** -0.5
ATTN_SCALE_LOG2 = ATTN_SCALE * 1.4426950408889634
DEEPNORM_ALPHA = 4.0 ** 0.25
ADAM_LR, ADAM_B1, ADAM_B2, ADAM_EPS, ADAM_WD, ADAM_STEP = 0.001, 0.9, 0.999, 1e-8, 0.01, 10
N_DEV = 8
LANES = 1024
VMEM_LIMIT = 56 * 1024 * 1024
MESH = pl.DeviceIdType.MESH

NT = (((1,), (1,)), ((), ()))
NN = (((1,), (0,)), ((), ()))
TN = (((0,), (0,)), ((), ()))


def _params(n_axes):
    return pltpu.CompilerParams(dimension_semantics=("arbitrary",) * n_axes, vmem_limit_bytes=VMEM_LIMIT)


def _dot(a, b, dn):
    return lax.dot_general(a.astype(BF16), b.astype(BF16), dn, preferred_element_type=F32)


def _sigmoid(z):
    return 1.0 / (1.0 + jnp.exp(-z))


def _colsum(t):
    return jnp.sum(t, axis=0, keepdims=True)


EXCHANGE_RELATIONS = {"devices": tuple(range(1, N_DEV)), "chips": (2, 4, 6)}


def _exchange_copies(g_refs, r_refs, send_sems, recv_sems, local_sems, kind):
    x, y, c, me = _place()
    n_arr = len(g_refs)

    def slot(lin):
        return lin // 2 if kind == "chips" else lin

    own = [pltpu.make_async_copy(g_refs[t].at[slot(me)], r_refs[t].at[slot(me)], local_sems.at[t]) for t in range(n_arr)]
    sends, recvs = [], []
    for n, r in enumerate(EXCHANGE_RELATIONS[kind]):
        peer, lin = _flip(x, y, c, r)
        for t in range(n_arr):
            k = n_arr * n + t
            sends.append(pltpu.make_async_remote_copy(
                src_ref=g_refs[t].at[slot(lin)], dst_ref=r_refs[t].at[slot(me)], send_sem=send_sems.at[k],
                recv_sem=recv_sems.at[k], device_id=peer, device_id_type=MESH))
            recvs.append(pltpu.make_async_remote_copy(
                src_ref=g_refs[t].at[slot(lin)], dst_ref=r_refs[t].at[slot(lin)], send_sem=send_sems.at[k],
                recv_sem=recv_sems.at[k], device_id=(x, y, c), device_id_type=MESH))
    return own, sends, recvs


def _exchange_start(copies):
    own, sends, _ = copies
    for cp in own + sends:
        cp.start()


def _exchange_wait(copies):
    own, sends, recvs = copies
    for cp in recvs:
        cp.wait_recv()
    for cp in sends:
        cp.wait_send()
    for cp in own:
        cp.wait()


def _exchange_extras(parts, kind):
    shapes = [jax.ShapeDtypeStruct(p.shape, p.dtype) for p in parts]
    n = len(parts) * len(EXCHANGE_RELATIONS[kind])
    return shapes, [pltpu.SemaphoreType.DMA((n,)), pltpu.SemaphoreType.DMA((n,)), pltpu.SemaphoreType.DMA((len(parts),))]


def _grid_call(body, name, grid, in_specs, out_specs, out_shape, args, scratch=(), side=None):
    if side is None:
        return pl.pallas_call(body, name=name, grid=grid, in_specs=in_specs, out_specs=out_specs,
                              out_shape=out_shape, scratch_shapes=list(scratch),
                              compiler_params=_params(len(grid)))(*args)
    parts, kind = side
    gather = kind == "gather"
    n_in, n_out, n_sc, n_arr = len(args), len(out_shape), len(scratch), len(parts)
    side_shapes, side_sems = _gather_extras(parts) if gather else _exchange_extras(parts, kind)
    mid = tuple(g // 2 for g in grid)

    def wrapped(*refs):
        ins, g_refs = refs[:n_in], refs[n_in:n_in + n_arr]
        outs = refs[n_in + n_arr:n_in + n_arr + n_out]
        r_refs = refs[n_in + n_arr + n_out:n_in + 2 * n_arr + n_out]
        sc = refs[n_in + 2 * n_arr + n_out:n_in + 2 * n_arr + n_out + n_sc]
        ids = [pl.program_id(a) for a in range(len(grid))]

        def at(step):
            return functools.reduce(jnp.logical_and, [i == s for i, s in zip(ids, step)])

        first, last = at((0,) * len(grid)), at(tuple(g - 1 for g in grid))
        if gather:
            exchange = _Gather(g_refs, r_refs, *refs[-3:])
            pl.when(first)(exchange.start)
            if mid != (0,) * len(grid):
                pl.when(at(mid))(exchange.forward)
            body(*ins, *outs, *sc)

            @pl.when(last)
            def _():
                if mid == (0,) * len(grid):
                    exchange.forward()
                exchange.end()
        else:
            copies = _exchange_copies(g_refs, r_refs, *refs[-3:], kind)
            pl.when(first)(lambda: _exchange_start(copies))
            body(*ins, *outs, *sc)
            pl.when(last)(lambda: _exchange_wait(copies))

    anyspace = pl.BlockSpec(memory_space=pl.ANY)
    return pl.pallas_call(
        wrapped, name=name, grid=grid, in_specs=list(in_specs) + [anyspace] * n_arr,
        out_specs=list(out_specs) + [anyspace] * n_arr, out_shape=list(out_shape) + side_shapes,
        scratch_shapes=list(scratch) + side_sems, compiler_params=_params(len(grid)),
    )(*args, *parts)


def _matmul(pairs, mode, out_dtype, m, n, tm, tn, name, side=None, n_outer=False):
    dn = {"nn": NN, "nt": NT, "tn": TN}[mode]
    tm, tn = min(tm, m), min(tn, n)
    n_pairs = len(pairs)
    grid = (n // tn, m // tm) if n_outer else (m // tm, n // tn)

    def ij(f):
        return (lambda j, i: f(i, j)) if n_outer else f

    def body(*refs):
        o_ref = refs[-1]
        acc = None
        for p in range(n_pairs):
            t = _dot(refs[2 * p][...], refs[2 * p + 1][...], dn)
            acc = t if acc is None else acc + t
        o_ref[...] = acc.astype(o_ref.dtype)

    in_specs, args = [], []
    for a, b in pairs:
        if mode == "nn":
            k = a.shape[1]
            in_specs += [pl.BlockSpec((tm, k), ij(lambda i, j: (i, 0))), pl.BlockSpec((k, tn), ij(lambda i, j: (0, j)))]
        elif mode == "nt":
            k = a.shape[1]
            in_specs += [pl.BlockSpec((tm, k), ij(lambda i, j: (i, 0))), pl.BlockSpec((tn, k), ij(lambda i, j: (j, 0)))]
        else:
            k = a.shape[0]
            in_specs += [pl.BlockSpec((k, tm), ij(lambda i, j: (0, i))), pl.BlockSpec((k, tn), ij(lambda i, j: (0, j)))]
        args += [a, b]
    return _grid_call(body, name, grid, in_specs, [pl.BlockSpec((tm, tn), ij(lambda i, j: (i, j)))],
                      [jax.ShapeDtypeStruct((m, n), out_dtype)], args, side=side)


def _matmul_cols_nn(a, w3, out_dtype, tm, name, side=None):
    m, k = a.shape
    _, _, n = w3.shape
    tm = min(tm, m)

    def body(a_ref, w_ref, o_ref):
        av = a_ref[...]
        for j in range(N_DEV):
            o_ref[:, n * j:n * (j + 1)] = _dot(av, w_ref[j], NN).astype(o_ref.dtype)

    return _grid_call(
        body, name, (m // tm,),
        [pl.BlockSpec((tm, k), lambda i: (i, 0)), pl.BlockSpec((N_DEV, k, n), lambda i: (0, 0, 0))],
        [pl.BlockSpec((tm, N_DEV * n), lambda i: (i, 0))], [jax.ShapeDtypeStruct((m, N_DEV * n), out_dtype)], [a, w3],
        side=side)


def _matmul_cols_tn(a, b, n, out_dtype, tk, name, side=None):
    m, k = a.shape
    tk = min(tk, k)

    def body(a_ref, b_ref, o_ref):
        o_ref[...] = _dot(a_ref[...], b_ref[...], TN).astype(o_ref.dtype)

    return _grid_call(
        body, name, (N_DEV, k // tk),
        [pl.BlockSpec((m, tk), lambda j, i: (0, i)), pl.BlockSpec((m, n), lambda j, i: (0, j))],
        [pl.BlockSpec((None, tk, n), lambda j, i: (j, i, 0))], [jax.ShapeDtypeStruct((N_DEV, k, n), out_dtype)], [a, b],
        side=side)


def _rows3(tm, d):
    return pl.BlockSpec((None, tm, d), lambda i: (0, i, 0))


def _modulate(x, mod, name):
    _, s, d = x.shape
    tm = min(s, 512)

    def body(x_ref, m_ref, h_ref):
        shift, scale = m_ref[0:1, 0:d], m_ref[0:1, d:2 * d]
        h_ref[...] = (x_ref[...] * (1.0 + scale) + shift).astype(BF16)

    return pl.pallas_call(
        body, name=name, grid=(s // tm,),
        in_specs=[_rows3(tm, d), pl.BlockSpec((8, 3 * d), lambda i: (0, 0))],
        out_specs=pl.BlockSpec((tm, d), lambda i: (i, 0)),
        out_shape=jax.ShapeDtypeStruct((s, d), BF16), compiler_params=_params(1),
    )(x, mod)


def _ln_stats(r):
    mu = jnp.mean(r, axis=-1, keepdims=True)
    rc = r - mu
    var = jnp.mean(rc * rc, axis=-1, keepdims=True)
    rstd = lax.rsqrt(var + LN_EPS)
    return rc * rstd, rstd


def _ln_bwd(dxhat, xhat, rstd):
    return rstd * (dxhat - jnp.mean(dxhat, axis=-1, keepdims=True)
                   - xhat * jnp.mean(dxhat * xhat, axis=-1, keepdims=True))


def _resid_ln(x, y, mod, g, b, mod_next, name):
    _, s, d = x.shape
    tm = min(s, 512)

    def body(x_ref, y_ref, m_ref, g_ref, b_ref, mn_ref, o_ref, h_ref):
        gate = m_ref[0:1, 2 * d:3 * d]
        xhat, _ = _ln_stats(DEEPNORM_ALPHA * x_ref[...] + (1.0 + gate) * y_ref[...])
        out = xhat * g_ref[...] + b_ref[...]
        o_ref[...] = out
        h_ref[...] = (out * (1.0 + mn_ref[0:1, d:2 * d]) + mn_ref[0:1, 0:d]).astype(BF16)

    row = pl.BlockSpec((tm, d), lambda i: (i, 0))
    vec = pl.BlockSpec((1, d), lambda i: (0, 0))
    modspec = pl.BlockSpec((8, 3 * d), lambda i: (0, 0))
    return pl.pallas_call(
        body, name=name, grid=(s // tm,),
        in_specs=[_rows3(tm, d), row, modspec, vec, vec, modspec],
        out_specs=[_rows3(tm, d), row],
        out_shape=[jax.ShapeDtypeStruct((1, s, d), F32), jax.ShapeDtypeStruct((s, d), BF16)],
        compiler_params=_params(1),
    )(x, y, mod, g, b, mod_next)


def _final_ln_loss_bwd(x, y, mod, g, b, target, name):
    _, s, d = x.shape
    tm = min(s, 256)

    def body(x_ref, y_ref, m_ref, g_ref, b_ref, t_ref, dy_ref, dx_ref, red_ref):
        @pl.when(pl.program_id(0) == 0)
        def _():
            red_ref[...] = jnp.zeros_like(red_ref)

        gate = m_ref[0:1, 2 * d:3 * d]
        yv = y_ref[...]
        xhat, rstd = _ln_stats(DEEPNORM_ALPHA * x_ref[...] + (1.0 + gate) * yv)
        err = xhat * g_ref[...] + b_ref[...] - t_ref[...]
        dout = err * (1.0 / d)
        dr = _ln_bwd(dout * g_ref[...], xhat, rstd)
        dy_ref[...] = ((1.0 + gate) * dr).astype(BF16)
        dx_ref[...] = DEEPNORM_ALPHA * dr
        red_ref[0:1, :] += _colsum(dout * xhat)
        red_ref[1:2, :] += _colsum(dout)
        red_ref[2:3, :] += _colsum(dr * yv)
        red_ref[3:4, :] += _colsum(err * err)

    row = pl.BlockSpec((tm, d), lambda i: (i, 0))
    vec = pl.BlockSpec((1, d), lambda i: (0, 0))
    return pl.pallas_call(
        body, name=name, grid=(s // tm,),
        in_specs=[_rows3(tm, d), row, pl.BlockSpec((8, 3 * d), lambda i: (0, 0)), vec, vec, _rows3(tm, d)],
        out_specs=[row, row, pl.BlockSpec((8, d), lambda i: (0, 0))],
        out_shape=[jax.ShapeDtypeStruct((s, d), BF16), jax.ShapeDtypeStruct((s, d), F32),
                   jax.ShapeDtypeStruct((8, d), F32)],
        compiler_params=_params(1),
    )(x, y, mod, g, b, target)


def _mid_bwd(dh, dxres, x, y, mod_lo, mod_hi, g, b, name):
    _, s, d = x.shape
    tm = min(s, 256)

    def body(dh_ref, dxr_ref, x_ref, y_ref, ml_ref, mh_ref, g_ref, b_ref, dy_ref, dx_ref, red_ref):
        @pl.when(pl.program_id(0) == 0)
        def _():
            red_ref[...] = jnp.zeros_like(red_ref)

        gate = ml_ref[0:1, 2 * d:3 * d]
        scale_hi = mh_ref[0:1, d:2 * d]
        yv, dhv = y_ref[...], dh_ref[...]
        xhat, rstd = _ln_stats(DEEPNORM_ALPHA * x_ref[...] + (1.0 + gate) * yv)
        x_mid = xhat * g_ref[...] + b_ref[...]
        dx_mid = dxr_ref[...] + dhv * (1.0 + scale_hi)
        dr = _ln_bwd(dx_mid * g_ref[...], xhat, rstd)
        dy_ref[...] = ((1.0 + gate) * dr).astype(BF16)
        dx_ref[...] = DEEPNORM_ALPHA * dr
        red_ref[0:1, :] += _colsum(dhv * x_mid)
        red_ref[1:2, :] += _colsum(dhv)
        red_ref[2:3, :] += _colsum(dx_mid * xhat)
        red_ref[3:4, :] += _colsum(dx_mid)
        red_ref[4:5, :] += _colsum(dr * yv)

    row = pl.BlockSpec((tm, d), lambda i: (i, 0))
    vec = pl.BlockSpec((1, d), lambda i: (0, 0))
    modspec = pl.BlockSpec((8, 3 * d), lambda i: (0, 0))
    return pl.pallas_call(
        body, name=name, grid=(s // tm,),
        in_specs=[row, row, _rows3(tm, d), row, modspec, modspec, vec, vec],
        out_specs=[row, row, pl.BlockSpec((8, d), lambda i: (0, 0))],
        out_shape=[jax.ShapeDtypeStruct((s, d), BF16), jax.ShapeDtypeStruct((s, d), F32),
                   jax.ShapeDtypeStruct((8, d), F32)],
        compiler_params=_params(1),
    )(dh, dxres, x, y, mod_lo, mod_hi, g, b)


def _first_bwd(dproj, w3, dxres, x, mod, name, side=None):
    _, s, d = x.shape
    n = w3.shape[2]
    tm = min(s, 256)

    def body(a_ref, w_ref, dxr_ref, x_ref, m_ref, gx_ref, red_ref):
        @pl.when(pl.program_id(0) == 0)
        def _():
            red_ref[...] = jnp.zeros_like(red_ref)

        dhv = _dot(a_ref[:, 0:n], w_ref[0], NT)
        for j in range(1, N_DEV):
            dhv = dhv + _dot(a_ref[:, n * j:n * (j + 1)], w_ref[j], NT)
        gx_ref[...] = dxr_ref[...] + dhv * (1.0 + m_ref[0:1, d:2 * d])
        red_ref[0:1, :] += _colsum(dhv * x_ref[...])
        red_ref[1:2, :] += _colsum(dhv)

    return _grid_call(
        body, name, (s // tm,),
        [pl.BlockSpec((tm, N_DEV * n), lambda i: (i, 0)), pl.BlockSpec((N_DEV, d, n), lambda i: (0, 0, 0)),
         pl.BlockSpec((tm, d), lambda i: (i, 0)), _rows3(tm, d), pl.BlockSpec((8, 3 * d), lambda i: (0, 0))],
        [_rows3(tm, d), pl.BlockSpec((8, d), lambda i: (0, 0))],
        [jax.ShapeDtypeStruct((1, s, d), F32), jax.ShapeDtypeStruct((8, d), F32)],
        [dproj, w3, dxres, x, mod], side=side)


EVEN_TM = 256


def _gmlp_mask():
    t = lax.broadcasted_iota(jnp.int32, (GMLP_BLOCK, GMLP_BLOCK), 0) // CHUNK
    s = lax.broadcasted_iota(jnp.int32, (GMLP_BLOCK, GMLP_BLOCK), 1) // CHUNK
    return s <= t


def _window_sum(ext, win, back):
    n = ext.shape[0]
    k = 1
    while k < win:
        ext = ext + pltpu.roll(ext, k if back else n - k, 0)
        k *= 2
    return ext


def _inv_count(row0, rows, win):
    t = row0 + lax.broadcasted_iota(jnp.int32, (rows, 1), 0)
    return t, 1.0 / jnp.minimum(t + 1, win).astype(F32)


def _pooled(xb, halo, row0, win):
    tm = xb.shape[0]
    sums = _window_sum(jnp.concatenate([halo, xb], axis=0), win, True)[POOL_HALO:]
    _, inv = _inv_count(row0, tm, win)
    return sums * inv - xb


def _even_fwd(proj, ws, bs_col, ng, nb, pw, pb, ps, name, side=None):
    s = proj.shape[0]
    tm = min(s, EVEN_TM)
    hd, gd = GMLP_HEAD_DIM, POOL_GROUP_DIM

    def body(p_ref, halo_ref, ws_ref, bs_ref, ng_ref, nb_ref, pw_ref, pb_ref, ps_ref, m_ref):
        i = pl.program_id(0)
        mask = _gmlp_mask()
        for h in range(GMLP_HEADS):
            wm = jnp.where(mask, ws_ref[h], 0.0).astype(BF16)
            for blk in range(tm // GMLP_BLOCK):
                rows = slice(blk * GMLP_BLOCK, (blk + 1) * GMLP_BLOCK)
                cu, cv, cz = h * hd, 1024 + h * hd, 2048 + h * hd
                vhat, _ = _ln_stats(p_ref[rows, cv:cv + hd].astype(F32))
                vn = vhat * ng_ref[...] + nb_ref[...]
                sv = _dot(wm, vn, NN) + bs_ref[h]
                za = p_ref[rows, cz:cz + hd].astype(F32)
                m_ref[rows, cu:cu + hd] = (p_ref[rows, cu:cu + hd].astype(F32) * sv * (za * _sigmoid(za))).astype(BF16)
        for g, win in enumerate(POOL_WINDOWS):
            cx, cz = 3072 + g * gd, 4096 + g * gd
            halo = jnp.where(i > 0, halo_ref[:, g * gd:(g + 1) * gd].astype(F32), 0.0)
            pooled = _pooled(p_ref[:, cx:cx + gd].astype(F32), halo, i * tm, win)
            yb = _dot(pooled, pw_ref[g], NN) + pb_ref[:, g * gd:(g + 1) * gd]
            zb = p_ref[:, cz:cz + gd].astype(F32)
            m_ref[:, 1024 + g * gd:1024 + (g + 1) * gd] = (
                yb * ps_ref[:, g * gd:(g + 1) * gd] * (zb * _sigmoid(zb))).astype(BF16)

    hb = tm // POOL_HALO
    return _grid_call(
        body, name, (s // tm,),
        [
            pl.BlockSpec((tm, 5120), lambda i: (i, 0)),
            pl.BlockSpec((POOL_HALO, 1024), lambda i: (jnp.maximum(i * hb - 1, 0), 3)),
            pl.BlockSpec((GMLP_HEADS, GMLP_BLOCK, GMLP_BLOCK), lambda i: (0, 0, 0)),
            pl.BlockSpec((GMLP_HEADS, GMLP_BLOCK, 1), lambda i: (0, 0, 0)),
            pl.BlockSpec((1, hd), lambda i: (0, 0)), pl.BlockSpec((1, hd), lambda i: (0, 0)),
            pl.BlockSpec((4, gd, gd), lambda i: (0, 0, 0)),
            pl.BlockSpec((1, 1024), lambda i: (0, 0)), pl.BlockSpec((1, 1024), lambda i: (0, 0)),
        ],
        [pl.BlockSpec((tm, 2048), lambda i: (i, 0))], [jax.ShapeDtypeStruct((s, 2048), BF16)],
        [proj, proj, ws, bs_col, ng, nb, pw, pb, ps], side=side)


def _even_bwd(proj, dm, ws, bs_col, ng, nb, pw, pb, ps, name, side=None):
    s = proj.shape[0]
    tm = min(s, EVEN_TM)
    hd, gd = GMLP_HEAD_DIM, POOL_GROUP_DIM
    n_tiles = s // tm

    def body(p_ref, halo_ref, zbn_ref, dm_ref, dbn_ref, ws_ref, bs_ref, ng_ref, nb_ref, pw_ref, pb_ref, ps_ref,
             dp_ref, dws_ref, dbs_ref, dng_ref, dnb_ref, dpw_ref, dpb_ref, dps_ref):
        i = pl.program_id(0)

        @pl.when(i == 0)
        def _():
            for r in (dws_ref, dbs_ref, dng_ref, dnb_ref, dpw_ref, dpb_ref, dps_ref):
                r[...] = jnp.zeros_like(r)

        mask = _gmlp_mask()
        for h in range(GMLP_HEADS):
            wm = jnp.where(mask, ws_ref[h], 0.0).astype(BF16)
            for blk in range(tm // GMLP_BLOCK):
                rows = slice(blk * GMLP_BLOCK, (blk + 1) * GMLP_BLOCK)
                cu, cv, cz = h * hd, 1024 + h * hd, 2048 + h * hd
                vhat, rstd = _ln_stats(p_ref[rows, cv:cv + hd].astype(F32))
                vn = (vhat * ng_ref[...] + nb_ref[...]).astype(BF16)
                sv = _dot(wm, vn, NN) + bs_ref[h]
                u, za = p_ref[rows, cu:cu + hd].astype(F32), p_ref[rows, cz:cz + hd].astype(F32)
                da = dm_ref[rows, cu:cu + hd].astype(F32)
                sig = _sigmoid(za)
                sa = za * sig
                dau = da * u
                dsv = dau * sa
                dp_ref[rows, cu:cu + hd] = (da * sv * sa).astype(BF16)
                dp_ref[rows, cz:cz + hd] = (dau * sv * (sig * (1.0 + za * (1.0 - sig)))).astype(BF16)
                dsv_b = dsv.astype(BF16)
                dbs_ref[h] += jnp.sum(dsv, axis=1, keepdims=True)
                dws_ref[h] += _dot(dsv_b, vn, NT)
                dvn = _dot(wm, dsv_b, TN)
                dng_ref[...] += _colsum(dvn * vhat)
                dnb_ref[...] += _colsum(dvn)
                dp_ref[rows, cv:cv + hd] = _ln_bwd(dvn * ng_ref[...], vhat, rstd).astype(BF16)

        row0 = i * tm
        for g, win in enumerate(POOL_WINDOWS):
            cx, cz, cd = 3072 + g * gd, 4096 + g * gd, 1024 + g * gd
            gs = slice(g * gd, (g + 1) * gd)
            halo = jnp.where(i > 0, halo_ref[:, gs].astype(F32), 0.0)
            xb = p_ref[:, cx:cx + gd].astype(F32)
            pooled = _pooled(xb, halo, row0, win).astype(BF16)
            scale_g = ps_ref[:, gs]
            yb = _dot(pooled, pw_ref[g], NN) + pb_ref[:, gs]
            zb, db = p_ref[:, cz:cz + gd].astype(F32), dm_ref[:, cd:cd + gd].astype(F32)
            sig = _sigmoid(zb)
            dyp = db * (zb * sig)
            dp_ref[:, cz:cz + gd] = (db * yb * scale_g * (sig * (1.0 + zb * (1.0 - sig)))).astype(BF16)
            dps_ref[:, gs] += _colsum(dyp * yb)
            dpb_ref[:, gs] += _colsum(dyp * scale_g)
            zb_ext = jnp.concatenate([zb, zbn_ref[:, gs].astype(F32)], axis=0)
            db_ext = jnp.concatenate([db, dbn_ref[:, gs].astype(F32)], axis=0)
            dy_ext = (db_ext * (zb_ext * _sigmoid(zb_ext)) * scale_g).astype(BF16)
            dpw_ref[g] += _dot(pooled, dy_ext[:tm], TN)
            dpooled = _dot(dy_ext, pw_ref[g], NT)
            t, inv = _inv_count(row0, tm + POOL_HALO, win)
            w_ext = jnp.where(t < s, dpooled * inv, 0.0)
            dp_ref[:, cx:cx + gd] = (_window_sum(w_ext, win, False)[:tm] - dpooled[:tm]).astype(BF16)

    hb = tm // POOL_HALO
    last = s // POOL_HALO - 1
    small = lambda shape: pl.BlockSpec(shape, lambda i: (0,) * len(shape))
    return _grid_call(
        body, name, (n_tiles,),
        [
            pl.BlockSpec((tm, 5120), lambda i: (i, 0)),
            pl.BlockSpec((POOL_HALO, 1024), lambda i: (jnp.maximum(i * hb - 1, 0), 3)),
            pl.BlockSpec((POOL_HALO, 1024), lambda i: (jnp.minimum((i + 1) * hb, last), 4)),
            pl.BlockSpec((tm, 2048), lambda i: (i, 0)),
            pl.BlockSpec((POOL_HALO, 1024), lambda i: (jnp.minimum((i + 1) * hb, last), 1)),
            small((GMLP_HEADS, GMLP_BLOCK, GMLP_BLOCK)), small((GMLP_HEADS, GMLP_BLOCK, 1)),
            small((1, hd)), small((1, hd)), small((4, gd, gd)), small((1, 1024)), small((1, 1024)),
        ],
        [
            pl.BlockSpec((tm, 5120), lambda i: (i, 0)),
            small((GMLP_HEADS, GMLP_BLOCK, GMLP_BLOCK)), small((GMLP_HEADS, GMLP_BLOCK, 1)),
            small((1, hd)), small((1, hd)), small((4, gd, gd)), small((1, 1024)), small((1, 1024)),
        ],
        [
            jax.ShapeDtypeStruct((s, 5120), BF16),
            jax.ShapeDtypeStruct((GMLP_HEADS, GMLP_BLOCK, GMLP_BLOCK), F32),
            jax.ShapeDtypeStruct((GMLP_HEADS, GMLP_BLOCK, 1), F32),
            jax.ShapeDtypeStruct((1, hd), F32), jax.ShapeDtypeStruct((1, hd), F32),
            jax.ShapeDtypeStruct((4, gd, gd), F32),
            jax.ShapeDtypeStruct((1, 1024), F32), jax.ShapeDtypeStruct((1, 1024), F32),
        ],
        [proj, proj, proj, dm, dm, ws, bs_col, ng, nb, pw, pb, ps], side=side)


def _rope_pair_swap(t):
    lane = lax.broadcasted_iota(jnp.int32, t.shape, 1)
    return jnp.where(lane % 64 < 32, pltpu.roll(t, 96, 1), pltpu.roll(t, 32, 1))


def _rms(x, g):
    r = lax.rsqrt(jnp.mean(x * x, axis=-1, keepdims=True) + LN_EPS)
    return x * r, r


def _rms_bwd(dy, g, xhat, r):
    dyg = dy * g
    return r * (dyg - xhat * jnp.mean(dyg * xhat, axis=-1, keepdims=True))


def _lane_lt(shape, n):
    return lax.broadcasted_iota(jnp.int32, shape, 1) < n


def _mla_prep(proj, cosp, sinp, gq, gkv, name):
    s = proj.shape[0]
    tm = min(s, 512)

    def body(qc_ref, kv_ref, c_ref, s_ref, gq_ref, gkv_ref, qn_ref, kp_ref):
        qhat, _ = _rms(qc_ref[...].astype(F32), None)
        qn_ref[...] = (qhat * gq_ref[...]).astype(BF16)
        khat, _ = _rms(kv_ref[:, 0:128].astype(F32), None)
        kp_ref[:, 0:128] = (khat * gkv_ref[...]).astype(BF16)
        kr = kv_ref[:, 128:256].astype(F32)
        kp_ref[:, 128:256] = (kr * c_ref[...] + _rope_pair_swap(kr) * s_ref[...]).astype(BF16)

    return pl.pallas_call(
        body, name=name, grid=(s // tm,),
        in_specs=[pl.BlockSpec((tm, 256), lambda i: (i, 0)), pl.BlockSpec((tm, 256), lambda i: (i, 1)),
                  pl.BlockSpec((tm, 128), lambda i: (i, 0)), pl.BlockSpec((tm, 128), lambda i: (i, 0)),
                  pl.BlockSpec((1, 256), lambda i: (0, 0)), pl.BlockSpec((1, 128), lambda i: (0, 0))],
        out_specs=[pl.BlockSpec((tm, 256), lambda i: (i, 0)), pl.BlockSpec((tm, 256), lambda i: (i, 0))],
        out_shape=[jax.ShapeDtypeStruct((s, 256), BF16), jax.ShapeDtypeStruct((s, 256), BF16)],
        compiler_params=_params(1),
    )(proj, proj, cosp, sinp, gq, gkv)


def _mla_prep_bwd(proj, dqn, dkp, dv, cosp, sinp, gq, gkv, dproj, name):
    s = proj.shape[0]
    tm = min(s, 512)

    def body(qc_ref, kv_ref, dqn_ref, dkp_ref, dv_ref, c_ref, s_ref, gq_ref, gkv_ref, dproj_ref, o_ref, red_ref):
        @pl.when(pl.program_id(0) == 0)
        def _():
            red_ref[...] = jnp.zeros_like(red_ref)

        qhat, qr = _rms(qc_ref[...].astype(F32), None)
        dq = dqn_ref[...]
        o_ref[:, 0:256] = _rms_bwd(dq, gq_ref[...], qhat, qr).astype(BF16)
        red_ref[0:1, :] += _colsum(dq * qhat)
        khat, kr = _rms(kv_ref[:, 0:128].astype(F32), None)
        dk = dkp_ref[:, 0:128] + dv_ref[...]
        o_ref[:, 256:384] = _rms_bwd(dk, gkv_ref[...], khat, kr).astype(BF16)
        red_ref[1:2, 0:128] += _colsum(dk * khat)
        dr = dkp_ref[:, 128:256]
        o_ref[:, 384:512] = (dr * c_ref[...] - _rope_pair_swap(dr) * s_ref[...]).astype(BF16)

    return pl.pallas_call(
        body, name=name, grid=(s // tm,),
        in_specs=[pl.BlockSpec((tm, 256), lambda i: (i, 0)), pl.BlockSpec((tm, 256), lambda i: (i, 1)),
                  pl.BlockSpec((tm, 256), lambda i: (i, 0)), pl.BlockSpec((tm, 256), lambda i: (i, 0)),
                  pl.BlockSpec((tm, 128), lambda i: (i, 0)),
                  pl.BlockSpec((tm, 128), lambda i: (i, 0)), pl.BlockSpec((tm, 128), lambda i: (i, 0)),
                  pl.BlockSpec((1, 256), lambda i: (0, 0)), pl.BlockSpec((1, 128), lambda i: (0, 0)),
                  pl.BlockSpec(memory_space=pl.ANY)],
        out_specs=[pl.BlockSpec((tm, 512), lambda i: (i, 0)), pl.BlockSpec((8, 256), lambda i: (0, 0))],
        out_shape=[jax.ShapeDtypeStruct(dproj.shape, BF16), jax.ShapeDtypeStruct((8, 256), F32)],
        input_output_aliases={9: 0}, compiler_params=_params(1),
    )(proj, proj, dqn, dkp, dv, cosp, sinp, gq, gkv, dproj)


HEADS_TM = 512
Z_COL0 = ODD_IN_PAD - MLA_WIDTH


def _head_cols(h):
    return slice(128 * h, 128 * h + 128)


def _q_heads(q_up, cosp, sinp, wuk, name):
    s = q_up.shape[0]
    tm = min(s, HEADS_TM)

    def body(q_ref, c_ref, s_ref, w_ref, o_ref):
        for p in range(MLA_HEADS // 2):
            raw = q_ref[:, MLA_WIDTH + 128 * p:MLA_WIDTH + 128 * (p + 1)].astype(F32)
            rot = raw * c_ref[...] + _rope_pair_swap(raw) * s_ref[...]
            low = _lane_lt(rot.shape, 64)
            o_ref[2 * p, :, 128:256] = jnp.where(low, rot, 0.0).astype(BF16)
            o_ref[2 * p + 1, :, 128:256] = jnp.where(low, pltpu.roll(rot, 64, 1), 0.0).astype(BF16)
        for h in range(MLA_HEADS):
            o_ref[h, :, 0:128] = _dot(q_ref[:, _head_cols(h)], w_ref[:, _head_cols(h)], NT).astype(BF16)

    return pl.pallas_call(
        body, name=name, grid=(s // tm,),
        in_specs=[pl.BlockSpec((tm, 3072), lambda i: (i, 0)),
                  pl.BlockSpec((tm, 128), lambda i: (i, 0)), pl.BlockSpec((tm, 128), lambda i: (i, 0)),
                  pl.BlockSpec((128, MLA_WIDTH), lambda i: (0, 0))],
        out_specs=pl.BlockSpec((MLA_HEADS, tm, 256), lambda i: (0, i, 0)),
        out_shape=jax.ShapeDtypeStruct((MLA_HEADS, s, 256), BF16), compiler_params=_params(1),
    )(q_up, cosp, sinp, wuk)


def _q_heads_bwd(dqp, q_up, cosp, sinp, wuk, name):
    s = q_up.shape[0]
    tm = min(s, HEADS_TM)

    def body(dq_ref, qn_ref, c_ref, s_ref, w_ref, dn_ref, dr_ref, dw_ref):
        @pl.when(pl.program_id(0) == 0)
        def _():
            dw_ref[...] = jnp.zeros_like(dw_ref)

        for h in range(MLA_HEADS):
            dlat = dq_ref[h, :, 0:128]
            dn_ref[:, _head_cols(h)] = _dot(dlat, w_ref[:, _head_cols(h)], NN).astype(BF16)
            dw_ref[:, _head_cols(h)] += _dot(dlat, qn_ref[:, _head_cols(h)], TN)
        for p in range(MLA_HEADS // 2):
            drot = dq_ref[2 * p, :, 128:256].astype(F32) + pltpu.roll(dq_ref[2 * p + 1, :, 128:256].astype(F32), 64, 1)
            dr_ref[:, _head_cols(p)] = (drot * c_ref[...] - _rope_pair_swap(drot) * s_ref[...]).astype(BF16)

    return pl.pallas_call(
        body, name=name, grid=(s // tm,),
        in_specs=[pl.BlockSpec((MLA_HEADS, tm, 256), lambda i: (0, i, 0)),
                  pl.BlockSpec((tm, MLA_WIDTH), lambda i: (i, 0)),
                  pl.BlockSpec((tm, 128), lambda i: (i, 0)), pl.BlockSpec((tm, 128), lambda i: (i, 0)),
                  pl.BlockSpec((128, MLA_WIDTH), lambda i: (0, 0))],
        out_specs=[pl.BlockSpec((tm, MLA_WIDTH), lambda i: (i, 0)),
                   pl.BlockSpec((tm, 1024), lambda i: (i, 0)),
                   pl.BlockSpec((128, MLA_WIDTH), lambda i: (0, 0))],
        out_shape=[jax.ShapeDtypeStruct((s, MLA_WIDTH), BF16), jax.ShapeDtypeStruct((s, 1024), BF16),
                   jax.ShapeDtypeStruct((128, MLA_WIDTH), F32)],
        compiler_params=_params(1),
    )(dqp, q_up, cosp, sinp, wuk)


def _o_gate(o_lat, proj, wuv, name):
    s = o_lat.shape[1]
    tm = min(s, HEADS_TM)

    def body(ol_ref, p_ref, w_ref, g_ref):
        for h in range(MLA_HEADS):
            z = p_ref[:, Z_COL0 + 128 * h:Z_COL0 + 128 * (h + 1)].astype(F32)
            g_ref[:, _head_cols(h)] = (_dot(ol_ref[h], w_ref[:, _head_cols(h)], NN) * (z * _sigmoid(z))).astype(BF16)

    return pl.pallas_call(
        body, name=name, grid=(s // tm,),
        in_specs=[pl.BlockSpec((MLA_HEADS, tm, 128), lambda i: (0, i, 0)),
                  pl.BlockSpec((tm, ODD_IN_PAD), lambda i: (i, 0)),
                  pl.BlockSpec((128, MLA_WIDTH), lambda i: (0, 0))],
        out_specs=pl.BlockSpec((tm, MLA_WIDTH), lambda i: (i, 0)),
        out_shape=jax.ShapeDtypeStruct((s, MLA_WIDTH), BF16), compiler_params=_params(1),
    )(o_lat, proj, wuv)


def _o_gate_bwd(dg, o_lat, proj, wuv, name):
    s = o_lat.shape[1]
    tm = min(s, HEADS_TM)

    def body(dg_ref, ol_ref, p_ref, w_ref, dp_ref, dol_ref, dw_ref):
        @pl.when(pl.program_id(0) == 0)
        def _():
            dw_ref[...] = jnp.zeros_like(dw_ref)

        dp_ref[:, 0:Z_COL0] = jnp.zeros((tm, Z_COL0), BF16)
        for h in range(MLA_HEADS):
            zc = slice(Z_COL0 + 128 * h, Z_COL0 + 128 * (h + 1))
            z, dgv, ol = p_ref[:, zc].astype(F32), dg_ref[:, _head_cols(h)].astype(F32), ol_ref[h]
            sig = _sigmoid(z)
            o = _dot(ol, w_ref[:, _head_cols(h)], NN)
            dp_ref[:, zc] = (dgv * o * (sig * (1.0 + z * (1.0 - sig)))).astype(BF16)
            do = (dgv * (z * sig)).astype(BF16)
            dol_ref[h] = _dot(do, w_ref[:, _head_cols(h)], NT).astype(BF16)
            dw_ref[:, _head_cols(h)] += _dot(ol, do, TN)

    return pl.pallas_call(
        body, name=name, grid=(s // tm,),
        in_specs=[pl.BlockSpec((tm, MLA_WIDTH), lambda i: (i, 0)),
                  pl.BlockSpec((MLA_HEADS, tm, 128), lambda i: (0, i, 0)),
                  pl.BlockSpec((tm, ODD_IN_PAD), lambda i: (i, 0)),
                  pl.BlockSpec((128, MLA_WIDTH), lambda i: (0, 0))],
        out_specs=[pl.BlockSpec((tm, ODD_IN_PAD), lambda i: (i, 0)),
                   pl.BlockSpec((MLA_HEADS, tm, 128), lambda i: (0, i, 0)),
                   pl.BlockSpec((128, MLA_WIDTH), lambda i: (0, 0))],
        out_shape=[jax.ShapeDtypeStruct((s, ODD_IN_PAD), BF16), jax.ShapeDtypeStruct((MLA_HEADS, s, 128), BF16),
                   jax.ShapeDtypeStruct((128, MLA_WIDTH), F32)],
        compiler_params=_params(1),
    )(dg, o_lat, proj, wuv)


ATT_TQ = CHUNK
ATT_ROWS = ATT_TQ * MLA_HEADS
ATT_TK = 512
ATT_HEAD_GROUP = 8


def _visible(k0, q_chunk, tk):
    kpos = k0 + lax.broadcasted_iota(jnp.int32, (1, tk), 1)
    return kpos // CHUNK <= q_chunk


def _tile_lanes(t, n):
    return jnp.concatenate([t] * (n // 128), axis=1)


def _key_blocks(i, tk, block, pairs=False):
    visible = i * ATT_TQ + ATT_TQ
    n_full = (visible + tk - 1) // tk - 1

    def full(j):
        block(pl.multiple_of(j * tk, tk), tk, False)

    if pairs:
        def two(jj, carry):
            full(2 * jj)
            full(2 * jj + 1)
            return carry

        lax.fori_loop(0, n_full // 2, two, 0)

        @pl.when(n_full % 2 == 1)
        def _():
            full(n_full - 1)
    else:
        def one(j, carry):
            full(j)
            return carry

        lax.fori_loop(0, n_full, one, 0)
    last0 = pl.multiple_of(n_full * tk, tk)
    half = tk // 2
    if half % 128 == 0:
        @pl.when(visible - n_full * tk <= half)
        def _():
            block(last0, half, True)

        @pl.when(visible - n_full * tk > half)
        def _():
            block(last0, tk, True)
    else:
        block(last0, tk, True)


def _attn_fwd(qp, kp, name, side=None):
    s = kp.shape[0]
    tk = min(ATT_TK, s)

    def body(q_ref, k_ref, o_ref, lse_ref, m_sc, acc_sc):
        i = pl.program_id(0)
        m_sc[...] = jnp.full_like(m_sc, -jnp.inf)
        acc_sc[...] = jnp.zeros_like(acc_sc)

        def block(k0, width, masked):
            k = k_ref[pl.ds(k0, width), :]
            v1 = jnp.where(_lane_lt(k.shape, 128), k, jnp.ones_like(k))
            for h0 in range(0, MLA_HEADS, ATT_HEAD_GROUP):
                rows = slice(h0 * ATT_TQ, (h0 + ATT_HEAD_GROUP) * ATT_TQ)
                q = q_ref[h0:h0 + ATT_HEAD_GROUP].reshape(ATT_HEAD_GROUP * ATT_TQ, 256)
                sc = _dot(q, k, NT) * ATTN_SCALE_LOG2
                if masked:
                    sc = jnp.where(_visible(k0, i, width), sc, -jnp.inf)
                m_prev = m_sc[rows]
                m_new = jnp.maximum(m_prev, jnp.max(sc, axis=1, keepdims=True))
                p = jnp.exp2(sc - _tile_lanes(m_new, width))
                acc_sc[rows] = _tile_lanes(jnp.exp2(m_prev - m_new), 256) * acc_sc[rows] + _dot(p, v1, NN)
                m_sc[rows] = m_new

        _key_blocks(i, tk, block, pairs=True)
        acc = acc_sc[...]
        l = acc[:, 128:256]
        o_ref[...] = (acc[:, 0:128] / l).astype(BF16).reshape(MLA_HEADS, ATT_TQ, 128)
        lse_ref[...] = (m_sc[...] + jnp.log2(l)).reshape(MLA_HEADS, ATT_TQ, 128)

    head128 = pl.BlockSpec((MLA_HEADS, ATT_TQ, 128), lambda i: (0, i, 0))
    return _grid_call(
        body, name, (s // ATT_TQ,),
        [pl.BlockSpec((MLA_HEADS, ATT_TQ, 256), lambda i: (0, i, 0)), pl.BlockSpec((s, 256), lambda i: (0, 0))],
        [head128, head128],
        [jax.ShapeDtypeStruct((MLA_HEADS, s, 128), BF16), jax.ShapeDtypeStruct((MLA_HEADS, s, 128), F32)],
        [qp, kp], scratch=[pltpu.VMEM((ATT_ROWS, 128), F32), pltpu.VMEM((ATT_ROWS, 256), F32)], side=side)


def _attn_bwd(qp, kp, o, do, lse, name, side=None):
    s = kp.shape[0]
    tk = min(ATT_TK, s)

    def body(q_ref, k_ref, o_ref, do_ref, lse_ref, dq_ref, dk_ref, dv_ref, dq_sc):
        i = pl.program_id(0)

        @pl.when(i == 0)
        def _():
            dk_ref[...] = jnp.zeros_like(dk_ref)
            dv_ref[...] = jnp.zeros_like(dv_ref)

        q = q_ref[...].reshape(ATT_ROWS, 256)
        dov = do_ref[...].reshape(ATT_ROWS, 128)
        delta = jnp.sum(dov.astype(F32) * o_ref[...].reshape(ATT_ROWS, 128).astype(F32), axis=1, keepdims=True)
        delta_t = _tile_lanes(jnp.broadcast_to(delta, (ATT_ROWS, 128)), tk)
        lse_t = _tile_lanes(lse_ref[...].reshape(ATT_ROWS, 128), tk)
        dq_sc[...] = jnp.zeros_like(dq_sc)

        def block(k0, width, masked):
            k = k_ref[pl.ds(k0, width), :]
            p = jnp.exp2(_dot(q, k, NT) * ATTN_SCALE_LOG2 - lse_t[:, 0:width])
            if masked:
                p = jnp.where(_visible(k0, i, width), p, 0.0)
            dv_ref[pl.ds(k0, width), :] += _dot(p, dov, TN)
            ds = (p * (_dot(dov, k[:, 0:128], NT) - delta_t[:, 0:width]) * ATTN_SCALE).astype(BF16)
            dq_sc[...] += _dot(ds, k, NN)
            dk_ref[pl.ds(k0, width), :] += _dot(ds, q, TN)

        _key_blocks(i, tk, block, pairs=True)
        dq_ref[...] = dq_sc[...].astype(BF16).reshape(MLA_HEADS, ATT_TQ, 256)

    head128 = pl.BlockSpec((MLA_HEADS, ATT_TQ, 128), lambda i: (0, i, 0))
    head256 = pl.BlockSpec((MLA_HEADS, ATT_TQ, 256), lambda i: (0, i, 0))
    return _grid_call(
        body, name, (s // ATT_TQ,),
        [head256, pl.BlockSpec((s, 256), lambda i: (0, 0)), head128, head128, head128],
        [head256, pl.BlockSpec((s, 256), lambda i: (0, 0)), pl.BlockSpec((s, 128), lambda i: (0, 0))],
        [jax.ShapeDtypeStruct((MLA_HEADS, s, 256), BF16),
         jax.ShapeDtypeStruct((s, 256), F32), jax.ShapeDtypeStruct((s, 128), F32)],
        [qp, kp, o, do, lse], scratch=[pltpu.VMEM((ATT_ROWS, 256), F32)], side=side)


def _place():
    x, y, c = lax.axis_index("x"), lax.axis_index("y"), lax.axis_index("c")
    return x, y, c, 4 * x + 2 * y + c


def _flip(x, y, c, r):
    px = 1 - x if r & 4 else x
    py = 1 - y if r & 2 else y
    pc = 1 - c if r & 1 else c
    return (px, py, pc), 4 * px + 2 * py + pc


def _adaln_exchange(c8, ada_w, ada_b_cols, blocks, name):
    d = c8.shape[1]
    w_cols = ada_w.shape[2]
    n_arr = len(blocks)

    def body(c_ref, w_ref, b_ref, *refs):
        x_refs, (call_ref, mod_ref), out_refs = refs[:n_arr], refs[n_arr:n_arr + 2], refs[n_arr + 2:2 * n_arr + 2]
        sbuf, rbuf, s1, r1, s2, r2 = refs[2 * n_arr + 2:2 * n_arr + 8]
        gather = _Gather(x_refs, out_refs, *refs[2 * n_arr + 8:])
        x, y, c, me = _place()
        call_ref[pl.ds(pl.multiple_of(me * 8, 8), 8), :] = c_ref[...]
        peers = [_flip(x, y, c, r) for r in range(1, N_DEV)]

        def c_copy(k, src_lin, to):
            rows = call_ref.at[pl.ds(pl.multiple_of(src_lin * 8, 8), 8), :]
            return pltpu.make_async_remote_copy(src_ref=rows, dst_ref=rows, send_sem=s1.at[k], recv_sem=r1.at[k],
                                                device_id=to, device_id_type=MESH)

        first = [c_copy(k, me, peer) for k, (peer, _) in enumerate(peers)]
        for cp in first:
            cp.start()
        for k, (_, lin) in enumerate(peers):
            c_copy(k, lin, (x, y, c)).wait_recv()
        for cp in first:
            cp.wait_send()

        for j in range(N_DEV):
            cj = call_ref[8 * j:8 * j + 8, :]
            cond = cj * _sigmoid(cj)
            for l in range(2):
                sbuf[j, l] = lax.dot_general(cond, w_ref[l], NN, precision=lax.Precision.HIGHEST,
                                             preferred_element_type=F32) + b_ref[l]

        def m_copy(k, src_slot, dst_slot, to):
            return pltpu.make_async_remote_copy(src_ref=sbuf.at[src_slot], dst_ref=rbuf.at[dst_slot],
                                                send_sem=s2.at[k], recv_sem=r2.at[k], device_id=to,
                                                device_id_type=MESH)

        rbuf[me] = sbuf[me]
        second = [m_copy(k, lin, me, peer) for k, (peer, lin) in enumerate(peers)]
        for cp in second:
            cp.start()
        gather.start()
        for k, (_, lin) in enumerate(peers):
            m_copy(k, lin, lin, (x, y, c)).wait_recv()
        for cp in second:
            cp.wait_send()
        for j in range(N_DEV):
            for l in range(2):
                mod_ref[l, :, w_cols * j:w_cols * (j + 1)] = rbuf[j, l]
        gather.forward()
        gather.end()

    vmem = pl.BlockSpec(memory_space=pltpu.VMEM)
    anyspace = pl.BlockSpec(memory_space=pl.ANY)
    g_shapes, g_sems = _gather_extras(blocks)
    return pl.pallas_call(
        body, name=name, in_specs=[vmem, vmem, vmem] + [anyspace] * n_arr, out_specs=[vmem, vmem] + [anyspace] * n_arr,
        out_shape=[jax.ShapeDtypeStruct((8 * N_DEV, d), F32), jax.ShapeDtypeStruct((2, 8, 3 * d), F32)] + g_shapes,
        scratch_shapes=[pltpu.VMEM((N_DEV, 2, 8, w_cols), F32), pltpu.VMEM((N_DEV, 2, 8, w_cols), F32),
                        pltpu.SemaphoreType.DMA((N_DEV - 1,)), pltpu.SemaphoreType.DMA((N_DEV - 1,)),
                        pltpu.SemaphoreType.DMA((N_DEV - 1,)), pltpu.SemaphoreType.DMA((N_DEV - 1,))] + g_sems,
        compiler_params=pltpu.CompilerParams(vmem_limit_bytes=VMEM_LIMIT),
    )(c8, ada_w, ada_b_cols, *blocks)


class _Gather:
    def __init__(self, x_refs, out_refs, send_sems, recv_sems, local_sems):
        x, y, c, _ = _place()
        self.me, self.sibling, self.c = (x, y, c), (x, y, 1 - c), c
        self.chips = [(1 - x, y), (x, 1 - y), (1 - x, 1 - y)]
        self.n_arr = len(x_refs)
        self.out_refs, self.send_sems, self.recv_sems = out_refs, send_sems, recv_sems
        self.mine = [pltpu.make_async_copy(x_refs[t], out_refs[t].at[4 * x + 2 * y + c], local_sems.at[t])
                     for t in range(self.n_arr)]
        self.first = []
        for t in range(self.n_arr):
            self.first.append(self.copy(t, 0, self.me, self.sibling, src=x_refs[t]))
            self.first += [self.copy(t, 1 + j, self.me, (*chip, c), src=x_refs[t]) for j, chip in enumerate(self.chips)]
        self.passed = [self.copy(t, 4 + j, (*chip, c), self.sibling)
                       for t in range(self.n_arr) for j, chip in enumerate(self.chips)]

    def copy(self, t, k, blk, to, src=None):
        slot = self.out_refs[t].at[4 * blk[0] + 2 * blk[1] + blk[2]]
        return pltpu.make_async_remote_copy(src_ref=slot if src is None else src, dst_ref=slot,
                                            send_sem=self.send_sems.at[7 * t + k], recv_sem=self.recv_sems.at[7 * t + k],
                                            device_id=to, device_id_type=MESH)

    def start(self):
        for cp in self.mine + self.first:
            cp.start()

    def forward(self):
        for t in range(self.n_arr):
            for j, chip in enumerate(self.chips):
                self.copy(t, 1 + j, (*chip, self.c), self.me).wait_recv()
                self.passed[3 * t + j].start()

    def end(self):
        for t in range(self.n_arr):
            self.copy(t, 0, self.sibling, self.me).wait_recv()
            for j, chip in enumerate(self.chips):
                self.copy(t, 4 + j, (*chip, 1 - self.c), self.me).wait_recv()
        for cp in self.first + self.passed:
            cp.wait_send()
        for cp in self.mine:
            cp.wait()


def _gather_extras(blocks):
    n_arr = len(blocks)
    return ([jax.ShapeDtypeStruct((N_DEV,) + b.shape, b.dtype) for b in blocks],
            [pltpu.SemaphoreType.DMA((7 * n_arr,)), pltpu.SemaphoreType.DMA((7 * n_arr,)),
             pltpu.SemaphoreType.DMA((n_arr,))])


def _all_gather(blocks, name):
    n_arr = len(blocks)

    def body(*refs):
        gather = _Gather(refs[:n_arr], refs[n_arr:2 * n_arr], *refs[2 * n_arr:])
        gather.start()
        gather.forward()
        gather.end()

    anyspace = pl.BlockSpec(memory_space=pl.ANY)
    shapes, sems = _gather_extras(blocks)
    return pl.pallas_call(body, name=name, in_specs=[anyspace] * n_arr, out_specs=[anyspace] * n_arr,
                          out_shape=shapes, scratch_shapes=sems)(*blocks)


def _scatter_parts(parts, name):
    n_arr = len(parts)

    def body(*refs):
        copies = _exchange_copies(refs[:n_arr], refs[n_arr:2 * n_arr], *refs[2 * n_arr:], "devices")
        _exchange_start(copies)
        _exchange_wait(copies)

    anyspace = pl.BlockSpec(memory_space=pl.ANY)
    shapes, sems = _exchange_extras(parts, "devices")
    return pl.pallas_call(body, name=name, in_specs=[anyspace] * n_arr, out_specs=[anyspace] * n_arr,
                          out_shape=shapes, scratch_shapes=sems)(*parts)


def _sum_parts(parts, name):
    def body(p_ref, g_ref):
        g = p_ref[0]
        for j in range(1, N_DEV):
            g = g + p_ref[j]
        g_ref[...] = g

    return pl.pallas_call(body, name=name, out_shape=jax.ShapeDtypeStruct(parts.shape[1:], F32),
                          compiler_params=pltpu.CompilerParams(vmem_limit_bytes=VMEM_LIMIT))(parts)


N_CHIPS = N_DEV // 2


def _sibling_swap(part, name):
    def body(g_ref, r_ref, send_sems, recv_sems):
        x, y, c, _ = _place()
        sends = [pltpu.make_async_remote_copy(
            src_ref=g_ref.at[2 * q + 1 - c], dst_ref=r_ref.at[q], send_sem=send_sems.at[q], recv_sem=recv_sems.at[q],
            device_id=(x, y, 1 - c), device_id_type=MESH) for q in range(N_CHIPS)]
        recvs = [pltpu.make_async_remote_copy(
            src_ref=g_ref.at[2 * q + c], dst_ref=r_ref.at[q], send_sem=send_sems.at[q], recv_sem=recv_sems.at[q],
            device_id=(x, y, c), device_id_type=MESH) for q in range(N_CHIPS)]
        for cp in sends:
            cp.start()
        for cp in recvs:
            cp.wait_recv()
        for cp in sends:
            cp.wait_send()

    anyspace = pl.BlockSpec(memory_space=pl.ANY)
    return pl.pallas_call(
        body, name=name, in_specs=[anyspace], out_specs=anyspace,
        out_shape=jax.ShapeDtypeStruct((N_CHIPS,) + part.shape[1:], part.dtype),
        scratch_shapes=[pltpu.SemaphoreType.DMA((N_CHIPS,)), pltpu.SemaphoreType.DMA((N_CHIPS,))])(part)


def _pair_sum(a, b, name):
    n, rows, cols = a.shape
    tr = max(t for t in range(16, 513, 16) if rows % t == 0)

    def body(a_ref, b_ref, o_ref):
        o_ref[...] = (a_ref[...].astype(F32) + b_ref[...].astype(F32)).astype(o_ref.dtype)

    blk = pl.BlockSpec((None, tr, cols), lambda q, i: (q, i, 0))
    return pl.pallas_call(body, name=name, grid=(n, rows // tr), in_specs=[blk, blk], out_specs=blk,
                          out_shape=jax.ShapeDtypeStruct(a.shape, a.dtype), compiler_params=_params(2))(a, b)


def _adamw(w, g, m, v):
    m = ADAM_B1 * m + (1.0 - ADAM_B1) * g
    v = ADAM_B2 * v + (1.0 - ADAM_B2) * (g * g)
    m_hat = m / (1.0 - ADAM_B1 ** ADAM_STEP)
    v_hat = v / (1.0 - ADAM_B2 ** ADAM_STEP)
    return -ADAM_LR * (m_hat / (jnp.sqrt(v_hat) + ADAM_EPS) + ADAM_WD * w), m, v


def _sum_parts_adamw(parts, w, m, v, name):
    n_parts, rows, cols = parts.shape
    tr = max(t for t in range(16, 257, 16) if rows % t == 0)

    def body(p_ref, w_ref, m_ref, v_ref, g_ref, d_ref, mo_ref, vo_ref):
        g = p_ref[0].astype(F32)
        for j in range(1, n_parts):
            g = g + p_ref[j].astype(F32)
        g_ref[...] = g
        d_ref[...], mo_ref[...], vo_ref[...] = _adamw(w_ref[...], g, m_ref[...], v_ref[...])

    row = _rows3(tr, cols)
    out = jax.ShapeDtypeStruct((1, rows, cols), F32)
    return pl.pallas_call(
        body, name=name, grid=(rows // tr,),
        in_specs=[pl.BlockSpec((n_parts, tr, cols), lambda i: (0, i, 0)), row, row, row],
        out_specs=[row, row, row, row], out_shape=[out, out, out, out], compiler_params=_params(1),
    )(parts, w, m, v)


def _sum_parts_adamw_whole(parts, w, m, v, name):
    def body(p_ref, w_ref, m_ref, v_ref, g_ref, d_ref, mo_ref, vo_ref):
        g = p_ref[0:1].astype(F32)
        for j in range(1, N_DEV):
            g = g + p_ref[j:j + 1].astype(F32)
        g_ref[...] = g
        d_ref[...], mo_ref[...], vo_ref[...] = _adamw(w_ref[...], g, m_ref[...], v_ref[...])

    out = jax.ShapeDtypeStruct(w.shape, F32)
    return pl.pallas_call(body, name=name, out_shape=[out] * 4,
                          compiler_params=pltpu.CompilerParams(vmem_limit_bytes=VMEM_LIMIT))(parts, w, m, v)


def _adamw_many(gs, ws, ms, vs, name):
    n = len(gs)

    def body(*refs):
        for k in range(n):
            g_ref, w_ref, m_ref, v_ref = (refs[q * n + k] for q in range(4))
            d_ref, mo_ref, vo_ref = (refs[(4 + q) * n + k] for q in range(3))
            d_ref[...], mo_ref[...], vo_ref[...] = _adamw(w_ref[...], g_ref[...], m_ref[...], v_ref[...])

    out = [jax.ShapeDtypeStruct(w.shape, F32) for w in ws]
    res = pl.pallas_call(body, name=name, out_shape=out * 3,
                         compiler_params=pltpu.CompilerParams(vmem_limit_bytes=VMEM_LIMIT))(*gs, *ws, *ms, *vs)
    return res[:n], res[n:2 * n], res[2 * n:]


def _ada_w_grad_adamw(c_all, dmod_rows, w, m, v, name):
    def body(c_ref, dm_ref, w_ref, m_ref, v_ref, g_ref, d_ref, mo_ref, vo_ref):
        cv = c_ref[...]
        g = lax.dot_general(cv * _sigmoid(cv), dm_ref[...], TN, precision=lax.Precision.HIGHEST,
                            preferred_element_type=F32)
        g_ref[...] = g
        d_ref[...], mo_ref[...], vo_ref[...] = _adamw(w_ref[...], g, m_ref[...], v_ref[...])

    n_layers, d, cols = w.shape
    layer = pl.BlockSpec((None, d, cols), lambda l: (l, 0, 0))
    out = jax.ShapeDtypeStruct(w.shape, F32)
    return pl.pallas_call(
        body, name=name, grid=(n_layers,),
        in_specs=[pl.BlockSpec(c_all.shape, lambda l: (0, 0)),
                  pl.BlockSpec((None,) + dmod_rows.shape[1:], lambda l: (l, 0, 0)), layer, layer, layer],
        out_specs=[layer] * 4, out_shape=[out] * 4, compiler_params=_params(1),
    )(c_all, dmod_rows, w, m, v)


REPLICATED = ("ln_g", "ln_b", "gmlp_norm_g", "gmlp_norm_b", "gmlp_ws", "gmlp_bs", "pool_b", "pool_scale",
              "mla_kv_norm_g", "mla_w_uk", "mla_w_uv")
CHUNK_ROWS, ADA_ROW, QNORM_ROW, LOSS_ROW, REP_ROWS = 73, 73, 74, 75, 80
UQ_ROWS, POOLW_ROWS = 96, 32


def _pad_rows(flat2d, rows):
    n, k = flat2d.shape
    return jnp.pad(flat2d, ((0, 0), (0, rows * LANES - k))).reshape(n, rows, LANES)


def _ada_cols_rows(vec):
    return _pad_rows(vec.reshape(2, N_DEV, -1).transpose(1, 0, 2).reshape(N_DEV, -1), 1)


def _unpack_replicated(rep, shapes):
    chunk = sum(s[1] for s in shapes) // N_DEV
    flat, off, out = rep[:, :CHUNK_ROWS].reshape(N_DEV, -1)[:, :chunk].reshape(-1), 0, {}
    for n, size, shape in shapes:
        out[n] = flat[off:off + size].reshape(shape)
        off += size
    cols = 3 * D_MODEL // N_DEV
    out["ada_b"] = rep[:, ADA_ROW, :2 * cols].reshape(N_DEV, 2, cols).transpose(1, 0, 2).reshape(2, -1)
    return out


def kernel(x, c, positions, ada_w, ada_b, ln_g, ln_b, e_w_in, gmlp_norm_g, gmlp_norm_b, gmlp_ws, gmlp_bs, pool_w, pool_b, pool_scale, e_w_out, o_w_in, mla_q_norm_g, mla_kv_norm_g, mla_w_uq, mla_w_uk, mla_w_uv, o_w_out, loss_target, m_ada_w, m_ada_b, m_ln_g, m_ln_b, m_e_w_in, m_gmlp_norm_g, m_gmlp_norm_b, m_gmlp_ws, m_gmlp_bs, m_pool_w, m_pool_b, m_pool_scale, m_e_w_out, m_o_w_in, m_mla_q_norm_g, m_mla_kv_norm_g, m_mla_w_uq, m_mla_w_uk, m_mla_w_uv, m_o_w_out, v_ada_w, v_ada_b, v_ln_g, v_ln_b, v_e_w_in, v_gmlp_norm_g, v_gmlp_norm_b, v_gmlp_ws, v_gmlp_bs, v_pool_w, v_pool_b, v_pool_scale, v_e_w_out, v_o_w_in, v_mla_q_norm_g, v_mla_kv_norm_g, v_mla_w_uq, v_mla_w_uk, v_mla_w_uv, v_o_w_out):
    w_in = dict(ada_w=ada_w, ada_b=ada_b, ln_g=ln_g, ln_b=ln_b, e_w_in=e_w_in, gmlp_norm_g=gmlp_norm_g,
                gmlp_norm_b=gmlp_norm_b, gmlp_ws=gmlp_ws, gmlp_bs=gmlp_bs, pool_w=pool_w, pool_b=pool_b,
                pool_scale=pool_scale, e_w_out=e_w_out, o_w_in=o_w_in, mla_q_norm_g=mla_q_norm_g,
                mla_kv_norm_g=mla_kv_norm_g, mla_w_uq=mla_w_uq, mla_w_uk=mla_w_uk, mla_w_uv=mla_w_uv, o_w_out=o_w_out)
    m_in = dict(ada_w=m_ada_w, ada_b=m_ada_b, ln_g=m_ln_g, ln_b=m_ln_b, e_w_in=m_e_w_in, gmlp_norm_g=m_gmlp_norm_g,
                gmlp_norm_b=m_gmlp_norm_b, gmlp_ws=m_gmlp_ws, gmlp_bs=m_gmlp_bs, pool_w=m_pool_w, pool_b=m_pool_b,
                pool_scale=m_pool_scale, e_w_out=m_e_w_out, o_w_in=m_o_w_in, mla_q_norm_g=m_mla_q_norm_g,
                mla_kv_norm_g=m_mla_kv_norm_g, mla_w_uq=m_mla_w_uq, mla_w_uk=m_mla_w_uk, mla_w_uv=m_mla_w_uv,
                o_w_out=m_o_w_out)
    v_in = dict(ada_w=v_ada_w, ada_b=v_ada_b, ln_g=v_ln_g, ln_b=v_ln_b, e_w_in=v_e_w_in, gmlp_norm_g=v_gmlp_norm_g,
                gmlp_norm_b=v_gmlp_norm_b, gmlp_ws=v_gmlp_ws, gmlp_bs=v_gmlp_bs, pool_w=v_pool_w, pool_b=v_pool_b,
                pool_scale=v_pool_scale, e_w_out=v_e_w_out, o_w_in=v_o_w_in, mla_q_norm_g=v_mla_q_norm_g,
                mla_kv_norm_g=v_mla_kv_norm_g, mla_w_uq=v_mla_w_uq, mla_w_uk=v_mla_w_uk, mla_w_uv=v_mla_w_uv,
                o_w_out=v_o_w_out)
    names = list(w_in)
    seq = x.shape[1]
    d = D_MODEL
    me = 4 * lax.axis_index("x") + 2 * lax.axis_index("y") + lax.axis_index("c")
    ada_cols = ada_w.shape[2]

    ada_b_cols = lax.dynamic_slice_in_dim(ada_b, me * ada_cols, ada_cols, axis=1)
    slab_row = lax.broadcasted_iota(jnp.int32, (8, d), 0)
    slab = jnp.where(slab_row == 0, c, jnp.where(slab_row == 1, jnp.pad(mla_q_norm_g, ((0, 0), (0, d - 32))), 0.0))
    c_all, mod, w_in_e3, pool_w3 = _adaln_exchange(
        slab, ada_w, jnp.broadcast_to(ada_b_cols[:, None, :], (2, 8, ada_cols)),
        [e_w_in[0].astype(BF16), pool_w.astype(BF16).reshape(POOLW_ROWS, LANES)], "adaln_exchange")
    h0 = _modulate(x, mod[0], "modulate0")
    proj0, o_in3 = _matmul_cols_nn(h0, w_in_e3, BF16, 512, "even_in", side=([o_w_in[0].astype(BF16)], "gather"))
    o_in_full = o_in3.transpose(1, 0, 2).reshape(d, ODD_IN)
    w_in_o = jnp.concatenate([o_in_full[:, :448], jnp.zeros((d, 64), BF16), o_in_full[:, 448:]], axis=1)
    pool_w_full = pool_w3.reshape(N_DEV, 4, 32, 256).transpose(1, 0, 2, 3).reshape(4, 256, 256)
    g_q = c_all.reshape(N_DEV, 8, d)[:, 1, :32].reshape(1, MLA_Q_RANK)

    ws, bs_col = gmlp_ws[0], gmlp_bs[0].reshape(GMLP_HEADS, GMLP_BLOCK, 1)
    wuk2, wuv2 = mla_w_uk[0].reshape(MLA_KV_RANK, -1), mla_w_uv[0].reshape(MLA_KV_RANK, -1)
    inv = 1.0 / (ROPE_THETA ** (jnp.arange(0, MLA_ROPE, 2, dtype=F32) / MLA_ROPE))
    ang = positions[0].astype(F32)[:, None] * inv
    cosp = jnp.tile(jnp.cos(ang), (1, 4))
    sinp = jnp.tile(jnp.concatenate([-jnp.sin(ang), jnp.sin(ang)], axis=1), (1, 2))

    mix0, w_out_e3 = _even_fwd(proj0, ws, bs_col, gmlp_norm_g, gmlp_norm_b, pool_w_full, pool_b, pool_scale, "even_mix",
                               side=([e_w_out[0].astype(BF16)], "gather"))
    w_out_e = w_out_e3.reshape(-1, d)
    y0, uq3 = _matmul([(mix0, w_out_e)], "nn", F32, seq, d, 512, 1024, "even_out",
                      side=([mla_w_uq.astype(BF16).reshape(UQ_ROWS, LANES)], "gather"))
    uq_full = uq3.reshape(MLA_Q_RANK, MLA_HEADS, MLA_NOPE + MLA_ROPE)
    w_uq_n = uq_full[:, :, :MLA_NOPE].reshape(MLA_Q_RANK, -1)
    w_uq_r = uq_full[:, :, MLA_NOPE:].reshape(MLA_Q_RANK, -1)
    w_uq = jnp.concatenate([w_uq_n, w_uq_r], axis=1)
    x1, h1 = _resid_ln(x, y0, mod[0], ln_g[0:1], ln_b[0:1], mod[1], "resid_ln0")

    (proj1,) = _matmul([(h1, w_in_o)], "nn", BF16, seq, ODD_IN_PAD, 512, ODD_IN_PAD, "odd_in")
    qn, kp = _mla_prep(proj1, cosp, sinp, g_q, mla_kv_norm_g, "mla_prep")
    (q_up,) = _matmul([(qn, w_uq)], "nn", BF16, seq, 3072, 512, 3072, "q_up")
    qp = _q_heads(q_up, cosp, sinp, wuk2, "q_heads")
    o_lat, lse, w_out_o3 = _attn_fwd(qp, kp, "attn_fwd", side=([o_w_out[0].astype(BF16)], "gather"))
    w_out_o = w_out_o3.reshape(-1, d)
    gated = _o_gate(o_lat, proj1, wuv2, "o_gate")
    (y1,) = _matmul([(gated, w_out_o)], "nn", F32, seq, d, 512, 1024, "odd_out")

    dy1, dxres1, red2 = _final_ln_loss_bwd(x1, y1, mod[1], ln_g[1:2], ln_b[1:2], loss_target, "final_ln_loss")
    (dgated,) = _matmul([(dy1, w_out_o)], "nt", BF16, seq, MLA_WIDTH, 512, MLA_WIDTH, "odd_out_dx")
    (g_w_out_o,) = _matmul([(gated, dy1)], "tn", BF16, MLA_WIDTH, d, 256, d, "odd_out_dw")
    dproj1_z, do_lat, g_wuv = _o_gate_bwd(dgated, o_lat, proj1, wuv2, "o_gate_bwd")
    dqp, dkp, dvv, r_o_out = _attn_bwd(qp, kp, o_lat, do_lat, lse, "attn_bwd",
                                       side=([g_w_out_o.reshape(N_DEV, -1, d)], "devices"))
    dq_nope, dq_rope, g_wuk = _q_heads_bwd(dqp, q_up, cosp, sinp, wuk2, "q_heads_bwd")
    (dqn,) = _matmul([(dq_nope, w_uq_n), (dq_rope, w_uq_r)], "nt", F32, seq, MLA_Q_RANK, 512, 256, "q_up_dx")
    (g_wuq_n,) = _matmul([(qn, dq_nope)], "tn", F32, MLA_Q_RANK, MLA_WIDTH, 256, MLA_WIDTH, "q_up_dw_nope")
    (g_wuq_r,) = _matmul([(qn, dq_rope)], "tn", F32, MLA_Q_RANK, 1024, 256, 1024, "q_up_dw_rope")
    dproj1, red_mla = _mla_prep_bwd(proj1, dqn, dkp, dvv, cosp, sinp, g_q, mla_kv_norm_g, dproj1_z, "mla_prep_bwd")
    (dh1,) = _matmul([(dproj1, w_in_o)], "nt", F32, seq, d, 512, d, "odd_in_dx")
    part_uq = jnp.concatenate([g_wuq_n.reshape(MLA_Q_RANK, MLA_HEADS, MLA_NOPE),
                               g_wuq_r.reshape(MLA_Q_RANK, MLA_HEADS, MLA_ROPE)], axis=2).astype(BF16).reshape(
                                   (N_DEV,) + mla_w_uq.shape[1:])
    (g_w_in_o,) = _matmul([(h1, dproj1)], "tn", BF16, d, ODD_IN_PAD, 256, ODD_IN_PAD // 2, "odd_in_dw", n_outer=True)
    part_o_in = jnp.concatenate([g_w_in_o[:, :448], g_w_in_o[:, 512:]], axis=1).reshape(d, N_DEV, -1).transpose(1, 0, 2)
    dy0, dxres0, red1 = _mid_bwd(dh1, dxres1, x, y0, mod[0], mod[1], ln_g[0:1], ln_b[0:1], "mid_bwd")
    dmix, r_uq = _matmul([(dy0, w_out_e)], "nt", BF16, seq, 2048, 512, 2048, "even_out_dx", side=([part_uq], "devices"))
    (g_w_out_e,) = _matmul([(mix0, dy0)], "tn", BF16, 2048, d, 256, d, "even_out_dw")
    dproj0, g_ws, g_bs, g_ng, g_nb, g_pw, g_pb, g_ps, r_o_in = _even_bwd(
        proj0, dmix, ws, bs_col, gmlp_norm_g, gmlp_norm_b, pool_w_full, pool_b, pool_scale, "even_mix_bwd",
        side=([part_o_in], "devices"))
    part_pw = g_pw.reshape(4, N_DEV, 32, 256).transpose(1, 0, 2, 3)
    part_e_in, r_e_out, r_pw = _matmul_cols_tn(h0, dproj0, w_in_e3.shape[2], BF16, 512, "even_in_dw",
                                               side=([g_w_out_e.reshape(N_DEV, -1, d), part_pw], "devices"))
    mine = lax.dynamic_index_in_dim(part_e_in.reshape((N_CHIPS, 2) + part_e_in.shape[1:]), lax.axis_index("c"), 1, False)
    chip_e_in = _pair_sum(mine, _sibling_swap(part_e_in, "e_in_sibling_swap"), "e_in_pair_sum")
    grad_x, red0, r_e_in = _first_bwd(dproj0, w_in_e3, dxres0, x, mod[0], "even_in_dx", side=([chip_e_in], "chips"))

    t_mask = lax.broadcasted_iota(jnp.int32, (GMLP_BLOCK, GMLP_BLOCK), 0) // CHUNK
    s_mask = lax.broadcasted_iota(jnp.int32, (GMLP_BLOCK, GMLP_BLOCK), 1) // CHUNK
    part = {
        "ln_g": jnp.stack([red1[2], red2[0]]), "ln_b": jnp.stack([red1[3], red2[1]]),
        "gmlp_norm_g": g_ng, "gmlp_norm_b": g_nb,
        "gmlp_ws": jnp.where(s_mask <= t_mask, g_ws, 0.0), "gmlp_bs": g_bs,
        "pool_b": g_pb, "pool_scale": g_ps, "mla_kv_norm_g": red_mla[1, :MLA_KV_RANK],
        "mla_w_uk": g_wuk, "mla_w_uv": g_wuv,
    }
    dmod = jnp.stack([jnp.concatenate([red0[1], red0[0], red1[4]]),
                      jnp.concatenate([red1[1], red1[0], red2[2]])])

    loss_row = jnp.pad(jnp.broadcast_to((0.5 / d * jnp.sum(red2[3])).reshape(1, 1, 1), (N_DEV, 1, 1)),
                       ((0, 0), (0, 0), (0, LANES - 1)))
    part_small = jnp.concatenate([
        _pad_rows(jnp.concatenate([part[n].reshape(-1) for n in REPLICATED]).reshape(N_DEV, -1), CHUNK_ROWS),
        jnp.pad(jnp.concatenate([_ada_cols_rows(dmod), _pad_rows(red_mla[0].reshape(N_DEV, -1), 1), loss_row], axis=1),
                ((0, 0), (0, REP_ROWS - LOSS_ROW - 1), (0, 0)))], axis=1)
    (r_small,) = _scatter_parts([part_small], "grad_scatter")
    small_sum = _sum_parts(r_small, "small_sum")
    loss = small_sum[LOSS_ROW, 0]
    (rep_sum,) = _all_gather([small_sum], "replicated_gather")

    res = {"e_w_in": _sum_parts_adamw(r_e_in, e_w_in, m_e_w_in, v_e_w_in, "adamw_e_w_in"),
           "o_w_in": _sum_parts_adamw(r_o_in, o_w_in, m_o_w_in, v_o_w_in, "adamw_o_w_in"),
           "e_w_out": _sum_parts_adamw(r_e_out, e_w_out, m_e_w_out, v_e_w_out, "adamw_e_w_out"),
           "o_w_out": _sum_parts_adamw(r_o_out, o_w_out, m_o_w_out, v_o_w_out, "adamw_o_w_out"),
           "mla_w_uq": _sum_parts_adamw_whole(r_uq, mla_w_uq, m_mla_w_uq, v_mla_w_uq, "adamw_w_uq"),
           "pool_w": _sum_parts_adamw_whole(r_pw, pool_w, m_pool_w, v_pool_w, "adamw_pool_w")}
    grads = _unpack_replicated(rep_sum, [(n, w_in[n].size, w_in[n].shape) for n in REPLICATED])
    grads["mla_q_norm_g"] = small_sum[QNORM_ROW:QNORM_ROW + 1, :32]
    small_names = list(grads)
    deltas, new_ms, new_vs = _adamw_many([grads[n] for n in small_names], [w_in[n] for n in small_names],
                                         [m_in[n] for n in small_names], [v_in[n] for n in small_names], "small_adamw")
    for k, n in enumerate(small_names):
        res[n] = [grads[n], deltas[k], new_ms[k], new_vs[k]]
    dmod_all = r_small[:, ADA_ROW, :2 * ada_cols].reshape(N_DEV, 2, ada_cols).transpose(1, 0, 2)
    dmod_rows = jnp.pad(dmod_all[:, :, None, :], ((0, 0), (0, 0), (0, 7), (0, 0))).reshape(2, 8 * N_DEV, ada_cols)
    res["ada_w"] = _ada_w_grad_adamw(c_all, dmod_rows, ada_w, m_ada_w, v_ada_w, "ada_w_adamw")

    return (loss, grad_x, *[res[n][0] for n in names], *[res[n][1] for n in names],
            *[res[n][2] for n in names], *[res[n][3] for n in names])
```

```python
import functools

import jax
import jax.numpy as jnp
from jax import lax
from jax.experimental import pallas as pl
from jax.experimental.pallas import tpu as pltpu

F32 = jnp.float32
BF16 = jnp.bfloat16

D_MODEL = 1024
CHUNK = 64
LN_EPS = 1e-5
GMLP_HEADS = 4
GMLP_HEAD_DIM = 256
GMLP_BLOCK = 128
POOL_WINDOWS = (2, 4, 8, 16)
POOL_GROUP_DIM = 256
POOL_HALO = 16
MLA_HEADS = 16
MLA_NOPE = 128
MLA_ROPE = 64
MLA_Q_RANK = 256
MLA_KV_RANK = 128
MLA_WIDTH = 2048
ODD_IN = 2496
ODD_IN_PAD = 2560
ROPE_THETA = 10000.0
ATTN_SCALE = (MLA_NOPE + MLA_ROPE) ** -0.5
ATTN_SCALE_LOG2 = ATTN_SCALE * 1.4426950408889634
DEEPNORM_ALPHA = 4.0 ** 0.25
ADAM_LR, ADAM_B1, ADAM_B2, ADAM_EPS, ADAM_WD, ADAM_STEP = 0.001, 0.9, 0.999, 1e-8, 0.01, 10
N_DEV = 8
LANES = 1024
VMEM_LIMIT = 56 * 1024 * 1024
MESH = pl.DeviceIdType.MESH

NT = (((1,), (1,)), ((), ()))
NN = (((1,), (0,)), ((), ()))
TN = (((0,), (0,)), ((), ()))


def _params(n_axes):
    return pltpu.CompilerParams(dimension_semantics=("arbitrary",) * n_axes, vmem_limit_bytes=VMEM_LIMIT)


def _dot(a, b, dn):
    return lax.dot_general(a.astype(BF16), b.astype(BF16), dn, preferred_element_type=F32)


def _sigmoid(z):
    return 1.0 / (1.0 + jnp.exp(-z))


def _colsum(t):
    return jnp.sum(t, axis=0, keepdims=True)


EXCHANGE_RELATIONS = {"devices": tuple(range(1, N_DEV)), "chips": (2, 4, 6)}


def _exchange_copies(g_refs, r_refs, send_sems, recv_sems, local_sems, kind):
    x, y, c, me = _place()
    n_arr = len(g_refs)

    def slot(lin):
        return lin // 2 if kind == "chips" else lin

    own = [pltpu.make_async_copy(g_refs[t].at[slot(me)], r_refs[t].at[slot(me)], local_sems.at[t]) for t in range(n_arr)]
    sends, recvs = [], []
    for n, r in enumerate(EXCHANGE_RELATIONS[kind]):
        peer, lin = _flip(x, y, c, r)
        for t in range(n_arr):
            k = n_arr * n + t
            sends.append(pltpu.make_async_remote_copy(
                src_ref=g_refs[t].at[slot(lin)], dst_ref=r_refs[t].at[slot(me)], send_sem=send_sems.at[k],
                recv_sem=recv_sems.at[k], device_id=peer, device_id_type=MESH))
            recvs.append(pltpu.make_async_remote_copy(
                src_ref=g_refs[t].at[slot(lin)], dst_ref=r_refs[t].at[slot(lin)], send_sem=send_sems.at[k],
                recv_sem=recv_sems.at[k], device_id=(x, y, c), device_id_type=MESH))
    return own, sends, recvs


def _exchange_start(copies):
    own, sends, _ = copies
    for cp in own + sends:
        cp.start()


def _exchange_wait(copies):
    own, sends, recvs = copies
    for cp in recvs:
        cp.wait_recv()
    for cp in sends:
        cp.wait_send()
    for cp in own:
        cp.wait()


def _exchange_extras(parts, kind):
    shapes = [jax.ShapeDtypeStruct(p.shape, p.dtype) for p in parts]
    n = len(parts) * len(EXCHANGE_RELATIONS[kind])
    return shapes, [pltpu.SemaphoreType.DMA((n,)), pltpu.SemaphoreType.DMA((n,)), pltpu.SemaphoreType.DMA((len(parts),))]


def _grid_call(body, name, grid, in_specs, out_specs, out_shape, args, scratch=(), side=None):
    if side is None:
        return pl.pallas_call(body, name=name, grid=grid, in_specs=in_specs, out_specs=out_specs,
                              out_shape=out_shape, scratch_shapes=list(scratch),
                              compiler_params=_params(len(grid)))(*args)
    parts, kind = side
    gather = kind == "gather"
    n_in, n_out, n_sc, n_arr = len(args), len(out_shape), len(scratch), len(parts)
    side_shapes, side_sems = _gather_extras(parts) if gather else _exchange_extras(parts, kind)
    mid = tuple(g // 2 for g in grid)

    def wrapped(*refs):
        ins, g_refs = refs[:n_in], refs[n_in:n_in + n_arr]
        outs = refs[n_in + n_arr:n_in + n_arr + n_out]
        r_refs = refs[n_in + n_arr + n_out:n_in + 2 * n_arr + n_out]
        sc = refs[n_in + 2 * n_arr + n_out:n_in + 2 * n_arr + n_out + n_sc]
        ids = [pl.program_id(a) for a in range(len(grid))]

        def at(step):
            return functools.reduce(jnp.logical_and, [i == s for i, s in zip(ids, step)])

        first, last = at((0,) * len(grid)), at(tuple(g - 1 for g in grid))
        if gather:
            exchange = _Gather(g_refs, r_refs, *refs[-3:])
            pl.when(first)(exchange.start)
            if mid != (0,) * len(grid):
                pl.when(at(mid))(exchange.forward)
            body(*ins, *outs, *sc)

            @pl.when(last)
            def _():
                if mid == (0,) * len(grid):
                    exchange.forward()
                exchange.end()
        else:
            copies = _exchange_copies(g_refs, r_refs, *refs[-3:], kind)
            pl.when(first)(lambda: _exchange_start(copies))
            body(*ins, *outs, *sc)
            pl.when(last)(lambda: _exchange_wait(copies))

    anyspace = pl.BlockSpec(memory_space=pl.ANY)
    return pl.pallas_call(
        wrapped, name=name, grid=grid, in_specs=list(in_specs) + [anyspace] * n_arr,
        out_specs=list(out_specs) + [anyspace] * n_arr, out_shape=list(out_shape) + side_shapes,
        scratch_shapes=list(scratch) + side_sems, compiler_params=_params(len(grid)),
    )(*args, *parts)


def _matmul(pairs, mode, out_dtype, m, n, tm, tn, name, side=None, n_outer=False):
    dn = {"nn": NN, "nt": NT, "tn": TN}[mode]
    tm, tn = min(tm, m), min(tn, n)
    n_pairs = len(pairs)
    grid = (n // tn, m // tm) if n_outer else (m // tm, n // tn)

    def ij(f):
        return (lambda j, i: f(i, j)) if n_outer else f

    def body(*refs):
        o_ref = refs[-1]
        acc = None
        for p in range(n_pairs):
            t = _dot(refs[2 * p][...], refs[2 * p + 1][...], dn)
            acc = t if acc is None else acc + t
        o_ref[...] = acc.astype(o_ref.dtype)

    in_specs, args = [], []
    for a, b in pairs:
        if mode == "nn":
            k = a.shape[1]
            in_specs += [pl.BlockSpec((tm, k), ij(lambda i, j: (i, 0))), pl.BlockSpec((k, tn), ij(lambda i, j: (0, j)))]
        elif mode == "nt":
            k = a.shape[1]
            in_specs += [pl.BlockSpec((tm, k), ij(lambda i, j: (i, 0))), pl.BlockSpec((tn, k), ij(lambda i, j: (j, 0)))]
        else:
            k = a.shape[0]
            in_specs += [pl.BlockSpec((k, tm), ij(lambda i, j: (0, i))), pl.BlockSpec((k, tn), ij(lambda i, j: (0, j)))]
        args += [a, b]
    return _grid_call(body, name, grid, in_specs, [pl.BlockSpec((tm, tn), ij(lambda i, j: (i, j)))],
                      [jax.ShapeDtypeStruct((m, n), out_dtype)], args, side=side)


def _matmul_cols_nn(a, w3, out_dtype, tm, name, side=None):
    m, k = a.shape
    _, _, n = w3.shape
    tm = min(tm, m)

    def body(a_ref, w_ref, o_ref):
        av = a_ref[...]
        for j in range(N_DEV):
            o_ref[:, n * j:n * (j + 1)] = _dot(av, w_ref[j], NN).astype(o_ref.dtype)

    return _grid_call(
        body, name, (m // tm,),
        [pl.BlockSpec((tm, k), lambda i: (i, 0)), pl.BlockSpec((N_DEV, k, n), lambda i: (0, 0, 0))],
        [pl.BlockSpec((tm, N_DEV * n), lambda i: (i, 0))], [jax.ShapeDtypeStruct((m, N_DEV * n), out_dtype)], [a, w3],
        side=side)


def _matmul_cols_tn(a, b, n, out_dtype, tk, name, side=None):
    m, k = a.shape
    tk = min(tk, k)

    def body(a_ref, b_ref, o_ref):
        o_ref[...] = _dot(a_ref[...], b_ref[...], TN).astype(o_ref.dtype)

    return _grid_call(
        body, name, (N_DEV, k // tk),
        [pl.BlockSpec((m, tk), lambda j, i: (0, i)), pl.BlockSpec((m, n), lambda j, i: (0, j))],
        [pl.BlockSpec((None, tk, n), lambda j, i: (j, i, 0))], [jax.ShapeDtypeStruct((N_DEV, k, n), out_dtype)], [a, b],
        side=side)


def _rows3(tm, d):
    return pl.BlockSpec((None, tm, d), lambda i: (0, i, 0))


def _modulate(x, mod, name):
    _, s, d = x.shape
    tm = min(s, 512)

    def body(x_ref, m_ref, h_ref):
        shift, scale = m_ref[0:1, 0:d], m_ref[0:1, d:2 * d]
        h_ref[...] = (x_ref[...] * (1.0 + scale) + shift).astype(BF16)

    return pl.pallas_call(
        body, name=name, grid=(s // tm,),
        in_specs=[_rows3(tm, d), pl.BlockSpec((8, 3 * d), lambda i: (0, 0))],
        out_specs=pl.BlockSpec((tm, d), lambda i: (i, 0)),
        out_shape=jax.ShapeDtypeStruct((s, d), BF16), compiler_params=_params(1),
    )(x, mod)


def _ln_stats(r):
    mu = jnp.mean(r, axis=-1, keepdims=True)
    rc = r - mu
    var = jnp.mean(rc * rc, axis=-1, keepdims=True)
    rstd = lax.rsqrt(var + LN_EPS)
    return rc * rstd, rstd


def _ln_bwd(dxhat, xhat, rstd):
    return rstd * (dxhat - jnp.mean(dxhat, axis=-1, keepdims=True)
                   - xhat * jnp.mean(dxhat * xhat, axis=-1, keepdims=True))


def _resid_ln(x, y, mod, g, b, mod_next, name):
    _, s, d = x.shape
    tm = min(s, 512)

    def body(x_ref, y_ref, m_ref, g_ref, b_ref, mn_ref, o_ref, h_ref):
        gate = m_ref[0:1, 2 * d:3 * d]
        xhat, _ = _ln_stats(DEEPNORM_ALPHA * x_ref[...] + (1.0 + gate) * y_ref[...])
        out = xhat * g_ref[...] + b_ref[...]
        o_ref[...] = out
        h_ref[...] = (out * (1.0 + mn_ref[0:1, d:2 * d]) + mn_ref[0:1, 0:d]).astype(BF16)

    row = pl.BlockSpec((tm, d), lambda i: (i, 0))
    vec = pl.BlockSpec((1, d), lambda i: (0, 0))
    modspec = pl.BlockSpec((8, 3 * d), lambda i: (0, 0))
    return pl.pallas_call(
        body, name=name, grid=(s // tm,),
        in_specs=[_rows3(tm, d), row, modspec, vec, vec, modspec],
        out_specs=[_rows3(tm, d), row],
        out_shape=[jax.ShapeDtypeStruct((1, s, d), F32), jax.ShapeDtypeStruct((s, d), BF16)],
        compiler_params=_params(1),
    )(x, y, mod, g, b, mod_next)


def _final_ln_loss_bwd(x, y, mod, g, b, target, name):
    _, s, d = x.shape
    tm = min(s, 512)

    def body(x_ref, y_ref, m_ref, g_ref, b_ref, t_ref, dy_ref, dx_ref, red_ref):
        @pl.when(pl.program_id(0) == 0)
        def _():
            red_ref[...] = jnp.zeros_like(red_ref)

        gate = m_ref[0:1, 2 * d:3 * d]
        yv = y_ref[...]
        xhat, rstd = _ln_stats(DEEPNORM_ALPHA * x_ref[...] + (1.0 + gate) * yv)
        err = xhat * g_ref[...] + b_ref[...] - t_ref[...]
        dout = err * (1.0 / d)
        dr = _ln_bwd(dout * g_ref[...], xhat, rstd)
        dy_ref[...] = ((1.0 + gate) * dr).astype(BF16)
        dx_ref[...] = DEEPNORM_ALPHA * dr
        red_ref[0:1, :] += _colsum(dout * xhat)
        red_ref[1:2, :] += _colsum(dout)
        red_ref[2:3, :] += _colsum(dr * yv)
        red_ref[3:4, :] += _colsum(err * err)

    row = pl.BlockSpec((tm, d), lambda i: (i, 0))
    vec = pl.BlockSpec((1, d), lambda i: (0, 0))
    return pl.pallas_call(
        body, name=name, grid=(s // tm,),
        in_specs=[_rows3(tm, d), row, pl.BlockSpec((8, 3 * d), lambda i: (0, 0)), vec, vec, _rows3(tm, d)],
        out_specs=[row, row, pl.BlockSpec((8, d), lambda i: (0, 0))],
        out_shape=[jax.ShapeDtypeStruct((s, d), BF16), jax.ShapeDtypeStruct((s, d), F32),
                   jax.ShapeDtypeStruct((8, d), F32)],
        compiler_params=_params(1),
    )(x, y, mod, g, b, target)


def _mid_bwd(dh, dxres, x, y, mod_lo, mod_hi, g, b, name):
    _, s, d = x.shape
    tm = min(s, 512)

    def body(dh_ref, dxr_ref, x_ref, y_ref, ml_ref, mh_ref, g_ref, b_ref, dy_ref, dx_ref, red_ref):
        @pl.when(pl.program_id(0) == 0)
        def _():
            red_ref[...] = jnp.zeros_like(red_ref)

        gate = ml_ref[0:1, 2 * d:3 * d]
        scale_hi = mh_ref[0:1, d:2 * d]
        yv, dhv = y_ref[...], dh_ref[...]
        xhat, rstd = _ln_stats(DEEPNORM_ALPHA * x_ref[...] + (1.0 + gate) * yv)
        x_mid = xhat * g_ref[...] + b_ref[...]
        dx_mid = dxr_ref[...] + dhv * (1.0 + scale_hi)
        dr = _ln_bwd(dx_mid * g_ref[...], xhat, rstd)
        dy_ref[...] = ((1.0 + gate) * dr).astype(BF16)
        dx_ref[...] = DEEPNORM_ALPHA * dr
        red_ref[0:1, :] += _colsum(dhv * x_mid)
        red_ref[1:2, :] += _colsum(dhv)
        red_ref[2:3, :] += _colsum(dx_mid * xhat)
        red_ref[3:4, :] += _colsum(dx_mid)
        red_ref[4:5, :] += _colsum(dr * yv)

    row = pl.BlockSpec((tm, d), lambda i: (i, 0))
    vec = pl.BlockSpec((1, d), lambda i: (0, 0))
    modspec = pl.BlockSpec((8, 3 * d), lambda i: (0, 0))
    return pl.pallas_call(
        body, name=name, grid=(s // tm,),
        in_specs=[row, row, _rows3(tm, d), row, modspec, modspec, vec, vec],
        out_specs=[row, row, pl.BlockSpec((8, d), lambda i: (0, 0))],
        out_shape=[jax.ShapeDtypeStruct((s, d), BF16), jax.ShapeDtypeStruct((s, d), F32),
                   jax.ShapeDtypeStruct((8, d), F32)],
        compiler_params=_params(1),
    )(dh, dxres, x, y, mod_lo, mod_hi, g, b)


def _first_bwd(dproj, w3, dxres, x, mod, name, side=None):
    _, s, d = x.shape
    n = w3.shape[2]
    tm = min(s, 256)

    def body(a_ref, w_ref, dxr_ref, x_ref, m_ref, gx_ref, red_ref):
        @pl.when(pl.program_id(0) == 0)
        def _():
            red_ref[...] = jnp.zeros_like(red_ref)

        dhv = _dot(a_ref[:, 0:n], w_ref[0], NT)
        for j in range(1, N_DEV):
            dhv = dhv + _dot(a_ref[:, n * j:n * (j + 1)], w_ref[j], NT)
        gx_ref[...] = dxr_ref[...] + dhv * (1.0 + m_ref[0:1, d:2 * d])
        red_ref[0:1, :] += _colsum(dhv * x_ref[...])
        red_ref[1:2, :] += _colsum(dhv)

    return _grid_call(
        body, name, (s // tm,),
        [pl.BlockSpec((tm, N_DEV * n), lambda i: (i, 0)), pl.BlockSpec((N_DEV, d, n), lambda i: (0, 0, 0)),
         pl.BlockSpec((tm, d), lambda i: (i, 0)), _rows3(tm, d), pl.BlockSpec((8, 3 * d), lambda i: (0, 0))],
        [_rows3(tm, d), pl.BlockSpec((8, d), lambda i: (0, 0))],
        [jax.ShapeDtypeStruct((1, s, d), F32), jax.ShapeDtypeStruct((8, d), F32)],
        [dproj, w3, dxres, x, mod], side=side)


EVEN_TM = 512


def _gmlp_mask():
    t = lax.broadcasted_iota(jnp.int32, (GMLP_BLOCK, GMLP_BLOCK), 0) // CHUNK
    s = lax.broadcasted_iota(jnp.int32, (GMLP_BLOCK, GMLP_BLOCK), 1) // CHUNK
    return s <= t


def _window_sum(ext, win, back):
    n = ext.shape[0]
    k = 1
    while k < win:
        ext = ext + pltpu.roll(ext, k if back else n - k, 0)
        k *= 2
    return ext


def _inv_count(row0, rows, win):
    t = row0 + lax.broadcasted_iota(jnp.int32, (rows, 1), 0)
    return t, 1.0 / jnp.minimum(t + 1, win).astype(F32)


def _pooled(xb, halo, row0, win):
    tm = xb.shape[0]
    sums = _window_sum(jnp.concatenate([halo, xb], axis=0), win, True)[POOL_HALO:]
    _, inv = _inv_count(row0, tm, win)
    return sums * inv - xb


def _even_fwd(proj, ws, bs_col, ng, nb, pw, pb, ps, name, side=None):
    s = proj.shape[0]
    tm = min(s, EVEN_TM)
    hd, gd = GMLP_HEAD_DIM, POOL_GROUP_DIM

    def body(p_ref, halo_ref, ws_ref, bs_ref, ng_ref, nb_ref, pw_ref, pb_ref, ps_ref, m_ref):
        i = pl.program_id(0)
        mask = _gmlp_mask()
        for h in range(GMLP_HEADS):
            wm = jnp.where(mask, ws_ref[h], 0.0).astype(BF16)
            for blk in range(tm // GMLP_BLOCK):
                rows = slice(blk * GMLP_BLOCK, (blk + 1) * GMLP_BLOCK)
                cu, cv, cz = h * hd, 1024 + h * hd, 2048 + h * hd
                vhat, _ = _ln_stats(p_ref[rows, cv:cv + hd].astype(F32))
                vn = vhat * ng_ref[...] + nb_ref[...]
                sv = _dot(wm, vn, NN) + bs_ref[h]
                za = p_ref[rows, cz:cz + hd].astype(F32)
                m_ref[rows, cu:cu + hd] = (p_ref[rows, cu:cu + hd].astype(F32) * sv * (za * _sigmoid(za))).astype(BF16)
        for g, win in enumerate(POOL_WINDOWS):
            cx, cz = 3072 + g * gd, 4096 + g * gd
            halo = jnp.where(i > 0, halo_ref[:, g * gd:(g + 1) * gd].astype(F32), 0.0)
            pooled = _pooled(p_ref[:, cx:cx + gd].astype(F32), halo, i * tm, win)
            yb = _dot(pooled, pw_ref[g], NN) + pb_ref[:, g * gd:(g + 1) * gd]
            zb = p_ref[:, cz:cz + gd].astype(F32)
            m_ref[:, 1024 + g * gd:1024 + (g + 1) * gd] = (
                yb * ps_ref[:, g * gd:(g + 1) * gd] * (zb * _sigmoid(zb))).astype(BF16)

    hb = tm // POOL_HALO
    return _grid_call(
        body, name, (s // tm,),
        [
            pl.BlockSpec((tm, 5120), lambda i: (i, 0)),
            pl.BlockSpec((POOL_HALO, 1024), lambda i: (jnp.maximum(i * hb - 1, 0), 3)),
            pl.BlockSpec((GMLP_HEADS, GMLP_BLOCK, GMLP_BLOCK), lambda i: (0, 0, 0)),
            pl.BlockSpec((GMLP_HEADS, GMLP_BLOCK, 1), lambda i: (0, 0, 0)),
            pl.BlockSpec((1, hd), lambda i: (0, 0)), pl.BlockSpec((1, hd), lambda i: (0, 0)),
            pl.BlockSpec((4, gd, gd), lambda i: (0, 0, 0)),
            pl.BlockSpec((1, 1024), lambda i: (0, 0)), pl.BlockSpec((1, 1024), lambda i: (0, 0)),
        ],
        [pl.BlockSpec((tm, 2048), lambda i: (i, 0))], [jax.ShapeDtypeStruct((s, 2048), BF16)],
        [proj, proj, ws, bs_col, ng, nb, pw, pb, ps], side=side)


def _even_bwd(proj, dm, ws, bs_col, ng, nb, pw, pb, ps, name, side=None):
    s = proj.shape[0]
    tm = min(s, EVEN_TM)
    hd, gd = GMLP_HEAD_DIM, POOL_GROUP_DIM
    n_tiles = s // tm

    def body(p_ref, halo_ref, zbn_ref, dm_ref, dbn_ref, ws_ref, bs_ref, ng_ref, nb_ref, pw_ref, pb_ref, ps_ref,
             dp_ref, dws_ref, dbs_ref, dng_ref, dnb_ref, dpw_ref, dpb_ref, dps_ref):
        i = pl.program_id(0)

        @pl.when(i == 0)
        def _():
            for r in (dws_ref, dbs_ref, dng_ref, dnb_ref, dpw_ref, dpb_ref, dps_ref):
                r[...] = jnp.zeros_like(r)

        mask = _gmlp_mask()
        for h in range(GMLP_HEADS):
            wm = jnp.where(mask, ws_ref[h], 0.0).astype(BF16)
            for blk in range(tm // GMLP_BLOCK):
                rows = slice(blk * GMLP_BLOCK, (blk + 1) * GMLP_BLOCK)
                cu, cv, cz = h * hd, 1024 + h * hd, 2048 + h * hd
                vhat, rstd = _ln_stats(p_ref[rows, cv:cv + hd].astype(F32))
                vn = (vhat * ng_ref[...] + nb_ref[...]).astype(BF16)
                sv = _dot(wm, vn, NN) + bs_ref[h]
                u, za = p_ref[rows, cu:cu + hd].astype(F32), p_ref[rows, cz:cz + hd].astype(F32)
                da = dm_ref[rows, cu:cu + hd].astype(F32)
                sig = _sigmoid(za)
                sa = za * sig
                dau = da * u
                dsv = dau * sa
                dp_ref[rows, cu:cu + hd] = (da * sv * sa).astype(BF16)
                dp_ref[rows, cz:cz + hd] = (dau * sv * (sig * (1.0 + za * (1.0 - sig)))).astype(BF16)
                dsv_b = dsv.astype(BF16)
                dbs_ref[h] += jnp.sum(dsv, axis=1, keepdims=True)
                dws_ref[h] += _dot(dsv_b, vn, NT)
                dvn = _dot(wm, dsv_b, TN)
                dng_ref[...] += _colsum(dvn * vhat)
                dnb_ref[...] += _colsum(dvn)
                dp_ref[rows, cv:cv + hd] = _ln_bwd(dvn * ng_ref[...], vhat, rstd).astype(BF16)

        row0 = i * tm
        for g, win in enumerate(POOL_WINDOWS):
            cx, cz, cd = 3072 + g * gd, 4096 + g * gd, 1024 + g * gd
            gs = slice(g * gd, (g + 1) * gd)
            halo = jnp.where(i > 0, halo_ref[:, gs].astype(F32), 0.0)
            xb = p_ref[:, cx:cx + gd].astype(F32)
            pooled = _pooled(xb, halo, row0, win).astype(BF16)
            scale_g = ps_ref[:, gs]
            yb = _dot(pooled, pw_ref[g], NN) + pb_ref[:, gs]
            zb, db = p_ref[:, cz:cz + gd].astype(F32), dm_ref[:, cd:cd + gd].astype(F32)
            sig = _sigmoid(zb)
            dyp = db * (zb * sig)
            dp_ref[:, cz:cz + gd] = (db * yb * scale_g * (sig * (1.0 + zb * (1.0 - sig)))).astype(BF16)
            dps_ref[:, gs] += _colsum(dyp * yb)
            dpb_ref[:, gs] += _colsum(dyp * scale_g)
            zb_ext = jnp.concatenate([zb, zbn_ref[:, gs].astype(F32)], axis=0)
            db_ext = jnp.concatenate([db, dbn_ref[:, gs].astype(F32)], axis=0)
            dy_ext = (db_ext * (zb_ext * _sigmoid(zb_ext)) * scale_g).astype(BF16)
            dpw_ref[g] += _dot(pooled, dy_ext[:tm], TN)
            dpooled = _dot(dy_ext, pw_ref[g], NT)
            t, inv = _inv_count(row0, tm + POOL_HALO, win)
            w_ext = jnp.where(t < s, dpooled * inv, 0.0)
            dp_ref[:, cx:cx + gd] = (_window_sum(w_ext, win, False)[:tm] - dpooled[:tm]).astype(BF16)

    hb = tm // POOL_HALO
    last = s // POOL_HALO - 1
    small = lambda shape: pl.BlockSpec(shape, lambda i: (0,) * len(shape))
    return _grid_call(
        body, name, (n_tiles,),
        [
            pl.BlockSpec((tm, 5120), lambda i: (i, 0)),
            pl.BlockSpec((POOL_HALO, 1024), lambda i: (jnp.maximum(i * hb - 1, 0), 3)),
            pl.BlockSpec((POOL_HALO, 1024), lambda i: (jnp.minimum((i + 1) * hb, last), 4)),
            pl.BlockSpec((tm, 2048), lambda i: (i, 0)),
            pl.BlockSpec((POOL_HALO, 1024), lambda i: (jnp.minimum((i + 1) * hb, last), 1)),
            small((GMLP_HEADS, GMLP_BLOCK, GMLP_BLOCK)), small((GMLP_HEADS, GMLP_BLOCK, 1)),
            small((1, hd)), small((1, hd)), small((4, gd, gd)), small((1, 1024)), small((1, 1024)),
        ],
        [
            pl.BlockSpec((tm, 5120), lambda i: (i, 0)),
            small((GMLP_HEADS, GMLP_BLOCK, GMLP_BLOCK)), small((GMLP_HEADS, GMLP_BLOCK, 1)),
            small((1, hd)), small((1, hd)), small((4, gd, gd)), small((1, 1024)), small((1, 1024)),
        ],
        [
            jax.ShapeDtypeStruct((s, 5120), BF16),
            jax.ShapeDtypeStruct((GMLP_HEADS, GMLP_BLOCK, GMLP_BLOCK), F32),
            jax.ShapeDtypeStruct((GMLP_HEADS, GMLP_BLOCK, 1), F32),
            jax.ShapeDtypeStruct((1, hd), F32), jax.ShapeDtypeStruct((1, hd), F32),
            jax.ShapeDtypeStruct((4, gd, gd), F32),
            jax.ShapeDtypeStruct((1, 1024), F32), jax.ShapeDtypeStruct((1, 1024), F32),
        ],
        [proj, proj, proj, dm, dm, ws, bs_col, ng, nb, pw, pb, ps], side=side)


def _rope_pair_swap(t):
    lane = lax.broadcasted_iota(jnp.int32, t.shape, 1)
    return jnp.where(lane % 64 < 32, pltpu.roll(t, 96, 1), pltpu.roll(t, 32, 1))


def _rms(x, g):
    r = lax.rsqrt(jnp.mean(x * x, axis=-1, keepdims=True) + LN_EPS)
    return x * r, r


def _rms_bwd(dy, g, xhat, r):
    dyg = dy * g
    return r * (dyg - xhat * jnp.mean(dyg * xhat, axis=-1, keepdims=True))


def _lane_lt(shape, n):
    return lax.broadcasted_iota(jnp.int32, shape, 1) < n


def _mla_prep(proj, cosp, sinp, gq, gkv, name):
    s = proj.shape[0]
    tm = min(s, 512)

    def body(qc_ref, kv_ref, c_ref, s_ref, gq_ref, gkv_ref, qn_ref, kp_ref):
        qhat, _ = _rms(qc_ref[...].astype(F32), None)
        qn_ref[...] = (qhat * gq_ref[...]).astype(BF16)
        khat, _ = _rms(kv_ref[:, 0:128].astype(F32), None)
        kp_ref[:, 0:128] = (khat * gkv_ref[...]).astype(BF16)
        kr = kv_ref[:, 128:256].astype(F32)
        kp_ref[:, 128:256] = (kr * c_ref[...] + _rope_pair_swap(kr) * s_ref[...]).astype(BF16)

    return pl.pallas_call(
        body, name=name, grid=(s // tm,),
        in_specs=[pl.BlockSpec((tm, 256), lambda i: (i, 0)), pl.BlockSpec((tm, 256), lambda i: (i, 1)),
                  pl.BlockSpec((tm, 128), lambda i: (i, 0)), pl.BlockSpec((tm, 128), lambda i: (i, 0)),
                  pl.BlockSpec((1, 256), lambda i: (0, 0)), pl.BlockSpec((1, 128), lambda i: (0, 0))],
        out_specs=[pl.BlockSpec((tm, 256), lambda i: (i, 0)), pl.BlockSpec((tm, 256), lambda i: (i, 0))],
        out_shape=[jax.ShapeDtypeStruct((s, 256), BF16), jax.ShapeDtypeStruct((s, 256), BF16)],
        compiler_params=_params(1),
    )(proj, proj, cosp, sinp, gq, gkv)


def _mla_prep_bwd(proj, dqn, dkp, dv, cosp, sinp, gq, gkv, dproj, name):
    s = proj.shape[0]
    tm = min(s, 512)

    def body(qc_ref, kv_ref, dqn_ref, dkp_ref, dv_ref, c_ref, s_ref, gq_ref, gkv_ref, dproj_ref, o_ref, red_ref):
        @pl.when(pl.program_id(0) == 0)
        def _():
            red_ref[...] = jnp.zeros_like(red_ref)

        qhat, qr = _rms(qc_ref[...].astype(F32), None)
        dq = dqn_ref[...]
        o_ref[:, 0:256] = _rms_bwd(dq, gq_ref[...], qhat, qr).astype(BF16)
        red_ref[0:1, :] += _colsum(dq * qhat)
        khat, kr = _rms(kv_ref[:, 0:128].astype(F32), None)
        dk = dkp_ref[:, 0:128] + dv_ref[...]
        o_ref[:, 256:384] = _rms_bwd(dk, gkv_ref[...], khat, kr).astype(BF16)
        red_ref[1:2, 0:128] += _colsum(dk * khat)
        dr = dkp_ref[:, 128:256]
        o_ref[:, 384:512] = (dr * c_ref[...] - _rope_pair_swap(dr) * s_ref[...]).astype(BF16)

    return pl.pallas_call(
        body, name=name, grid=(s // tm,),
        in_specs=[pl.BlockSpec((tm, 256), lambda i: (i, 0)), pl.BlockSpec((tm, 256), lambda i: (i, 1)),
                  pl.BlockSpec((tm, 256), lambda i: (i, 0)), pl.BlockSpec((tm, 256), lambda i: (i, 0)),
                  pl.BlockSpec((tm, 128), lambda i: (i, 0)),
                  pl.BlockSpec((tm, 128), lambda i: (i, 0)), pl.BlockSpec((tm, 128), lambda i: (i, 0)),
                  pl.BlockSpec((1, 256), lambda i: (0, 0)), pl.BlockSpec((1, 128), lambda i: (0, 0)),
                  pl.BlockSpec(memory_space=pl.ANY)],
        out_specs=[pl.BlockSpec((tm, 512), lambda i: (i, 0)), pl.BlockSpec((8, 256), lambda i: (0, 0))],
        out_shape=[jax.ShapeDtypeStruct(dproj.shape, BF16), jax.ShapeDtypeStruct((8, 256), F32)],
        input_output_aliases={9: 0}, compiler_params=_params(1),
    )(proj, proj, dqn, dkp, dv, cosp, sinp, gq, gkv, dproj)


HEADS_TM = 512
Z_COL0 = ODD_IN_PAD - MLA_WIDTH


def _head_cols(h):
    return slice(128 * h, 128 * h + 128)


def _q_heads(q_up, cosp, sinp, wuk, name):
    s = q_up.shape[0]
    tm = min(s, HEADS_TM)

    def body(q_ref, c_ref, s_ref, w_ref, o_ref):
        for p in range(MLA_HEADS // 2):
            raw = q_ref[:, MLA_WIDTH + 128 * p:MLA_WIDTH + 128 * (p + 1)].astype(F32)
            rot = raw * c_ref[...] + _rope_pair_swap(raw) * s_ref[...]
            low = _lane_lt(rot.shape, 64)
            o_ref[2 * p, :, 128:256] = jnp.where(low, rot, 0.0).astype(BF16)
            o_ref[2 * p + 1, :, 128:256] = jnp.where(low, pltpu.roll(rot, 64, 1), 0.0).astype(BF16)
        for h in range(MLA_HEADS):
            o_ref[h, :, 0:128] = _dot(q_ref[:, _head_cols(h)], w_ref[:, _head_cols(h)], NT).astype(BF16)

    return pl.pallas_call(
        body, name=name, grid=(s // tm,),
        in_specs=[pl.BlockSpec((tm, 3072), lambda i: (i, 0)),
                  pl.BlockSpec((tm, 128), lambda i: (i, 0)), pl.BlockSpec((tm, 128), lambda i: (i, 0)),
                  pl.BlockSpec((128, MLA_WIDTH), lambda i: (0, 0))],
        out_specs=pl.BlockSpec((MLA_HEADS, tm, 256), lambda i: (0, i, 0)),
        out_shape=jax.ShapeDtypeStruct((MLA_HEADS, s, 256), BF16), compiler_params=_params(1),
    )(q_up, cosp, sinp, wuk)


def _q_heads_bwd(dqp, q_up, cosp, sinp, wuk, name):
    s = q_up.shape[0]
    tm = min(s, HEADS_TM)

    def body(dq_ref, qn_ref, c_ref, s_ref, w_ref, dn_ref, dr_ref, dw_ref):
        @pl.when(pl.program_id(0) == 0)
        def _():
            dw_ref[...] = jnp.zeros_like(dw_ref)

        for h in range(MLA_HEADS):
            dlat = dq_ref[h, :, 0:128]
            dn_ref[:, _head_cols(h)] = _dot(dlat, w_ref[:, _head_cols(h)], NN).astype(BF16)
            dw_ref[:, _head_cols(h)] += _dot(dlat, qn_ref[:, _head_cols(h)], TN)
        for p in range(MLA_HEADS // 2):
            drot = dq_ref[2 * p, :, 128:256].astype(F32) + pltpu.roll(dq_ref[2 * p + 1, :, 128:256].astype(F32), 64, 1)
            dr_ref[:, _head_cols(p)] = (drot * c_ref[...] - _rope_pair_swap(drot) * s_ref[...]).astype(BF16)

    return pl.pallas_call(
        body, name=name, grid=(s // tm,),
        in_specs=[pl.BlockSpec((MLA_HEADS, tm, 256), lambda i: (0, i, 0)),
                  pl.BlockSpec((tm, MLA_WIDTH), lambda i: (i, 0)),
                  pl.BlockSpec((tm, 128), lambda i: (i, 0)), pl.BlockSpec((tm, 128), lambda i: (i, 0)),
                  pl.BlockSpec((128, MLA_WIDTH), lambda i: (0, 0))],
        out_specs=[pl.BlockSpec((tm, MLA_WIDTH), lambda i: (i, 0)),
                   pl.BlockSpec((tm, 1024), lambda i: (i, 0)),
                   pl.BlockSpec((128, MLA_WIDTH), lambda i: (0, 0))],
        out_shape=[jax.ShapeDtypeStruct((s, MLA_WIDTH), BF16), jax.ShapeDtypeStruct((s, 1024), BF16),
                   jax.ShapeDtypeStruct((128, MLA_WIDTH), F32)],
        compiler_params=_params(1),
    )(dqp, q_up, cosp, sinp, wuk)


def _o_gate(o_lat, proj, wuv, name):
    s = o_lat.shape[1]
    tm = min(s, HEADS_TM)

    def body(ol_ref, p_ref, w_ref, g_ref):
        for h in range(MLA_HEADS):
            z = p_ref[:, Z_COL0 + 128 * h:Z_COL0 + 128 * (h + 1)].astype(F32)
            g_ref[:, _head_cols(h)] = (_dot(ol_ref[h], w_ref[:, _head_cols(h)], NN) * (z * _sigmoid(z))).astype(BF16)

    return pl.pallas_call(
        body, name=name, grid=(s // tm,),
        in_specs=[pl.BlockSpec((MLA_HEADS, tm, 128), lambda i: (0, i, 0)),
                  pl.BlockSpec((tm, ODD_IN_PAD), lambda i: (i, 0)),
                  pl.BlockSpec((128, MLA_WIDTH), lambda i: (0, 0))],
        out_specs=pl.BlockSpec((tm, MLA_WIDTH), lambda i: (i, 0)),
        out_shape=jax.ShapeDtypeStruct((s, MLA_WIDTH), BF16), compiler_params=_params(1),
    )(o_lat, proj, wuv)


def _o_gate_bwd(dg, o_lat, proj, wuv, name):
    s = o_lat.shape[1]
    tm = min(s, HEADS_TM)

    def body(dg_ref, ol_ref, p_ref, w_ref, dp_ref, dol_ref, dw_ref):
        @pl.when(pl.program_id(0) == 0)
        def _():
            dw_ref[...] = jnp.zeros_like(dw_ref)

        dp_ref[:, 0:Z_COL0] = jnp.zeros((tm, Z_COL0), BF16)
        for h in range(MLA_HEADS):
            zc = slice(Z_COL0 + 128 * h, Z_COL0 + 128 * (h + 1))
            z, dgv, ol = p_ref[:, zc].astype(F32), dg_ref[:, _head_cols(h)].astype(F32), ol_ref[h]
            sig = _sigmoid(z)
            o = _dot(ol, w_ref[:, _head_cols(h)], NN)
            dp_ref[:, zc] = (dgv * o * (sig * (1.0 + z * (1.0 - sig)))).astype(BF16)
            do = (dgv * (z * sig)).astype(BF16)
            dol_ref[h] = _dot(do, w_ref[:, _head_cols(h)], NT).astype(BF16)
            dw_ref[:, _head_cols(h)] += _dot(ol, do, TN)

    return pl.pallas_call(
        body, name=name, grid=(s // tm,),
        in_specs=[pl.BlockSpec((tm, MLA_WIDTH), lambda i: (i, 0)),
                  pl.BlockSpec((MLA_HEADS, tm, 128), lambda i: (0, i, 0)),
                  pl.BlockSpec((tm, ODD_IN_PAD), lambda i: (i, 0)),
                  pl.BlockSpec((128, MLA_WIDTH), lambda i: (0, 0))],
        out_specs=[pl.BlockSpec((tm, ODD_IN_PAD), lambda i: (i, 0)),
                   pl.BlockSpec((MLA_HEADS, tm, 128), lambda i: (0, i, 0)),
                   pl.BlockSpec((128, MLA_WIDTH), lambda i: (0, 0))],
        out_shape=[jax.ShapeDtypeStruct((s, ODD_IN_PAD), BF16), jax.ShapeDtypeStruct((MLA_HEADS, s, 128), BF16),
                   jax.ShapeDtypeStruct((128, MLA_WIDTH), F32)],
        compiler_params=_params(1),
    )(dg, o_lat, proj, wuv)


ATT_TQ = CHUNK
ATT_ROWS = ATT_TQ * MLA_HEADS
ATT_TK = 512
ATT_HEAD_GROUP = 8


def _visible(k0, q_chunk, tk):
    kpos = k0 + lax.broadcasted_iota(jnp.int32, (1, tk), 1)
    return kpos // CHUNK <= q_chunk


def _tile_lanes(t, n):
    return jnp.concatenate([t] * (n // 128), axis=1)


def _key_blocks(i, tk, block, pairs=False):
    visible = i * ATT_TQ + ATT_TQ
    n_full = (visible + tk - 1) // tk - 1

    def full(j):
        block(pl.multiple_of(j * tk, tk), tk, False)

    if pairs:
        def two(jj, carry):
            full(2 * jj)
            full(2 * jj + 1)
            return carry

        lax.fori_loop(0, n_full // 2, two, 0)

        @pl.when(n_full % 2 == 1)
        def _():
            full(n_full - 1)
    else:
        def one(j, carry):
            full(j)
            return carry

        lax.fori_loop(0, n_full, one, 0)
    last0 = pl.multiple_of(n_full * tk, tk)
    half = tk // 2
    if half % 128 == 0:
        @pl.when(visible - n_full * tk <= half)
        def _():
            block(last0, half, True)

        @pl.when(visible - n_full * tk > half)
        def _():
            block(last0, tk, True)
    else:
        block(last0, tk, True)


def _attn_fwd(qp, kp, name, side=None):
    s = kp.shape[0]
    tk = min(ATT_TK, s)

    def body(q_ref, k_ref, o_ref, lse_ref, m_sc, acc_sc):
        i = pl.program_id(0)
        m_sc[...] = jnp.full_like(m_sc, -jnp.inf)
        acc_sc[...] = jnp.zeros_like(acc_sc)

        def block(k0, width, masked):
            k = k_ref[pl.ds(k0, width), :]
            v1 = jnp.where(_lane_lt(k.shape, 128), k, jnp.ones_like(k))
            for h0 in range(0, MLA_HEADS, ATT_HEAD_GROUP):
                rows = slice(h0 * ATT_TQ, (h0 + ATT_HEAD_GROUP) * ATT_TQ)
                q = q_ref[h0:h0 + ATT_HEAD_GROUP].reshape(ATT_HEAD_GROUP * ATT_TQ, 256)
                sc = _dot(q, k, NT) * ATTN_SCALE_LOG2
                if masked:
                    sc = jnp.where(_visible(k0, i, width), sc, -jnp.inf)
                m_prev = m_sc[rows]
                m_new = jnp.maximum(m_prev, jnp.max(sc, axis=1, keepdims=True))
                p = jnp.exp2(sc - _tile_lanes(m_new, width))
                acc_sc[rows] = _tile_lanes(jnp.exp2(m_prev - m_new), 256) * acc_sc[rows] + _dot(p, v1, NN)
                m_sc[rows] = m_new

        _key_blocks(i, tk, block, pairs=True)
        acc = acc_sc[...]
        l = acc[:, 128:256]
        o_ref[...] = (acc[:, 0:128] / l).astype(BF16).reshape(MLA_HEADS, ATT_TQ, 128)
        lse_ref[...] = (m_sc[...] + jnp.log2(l)).reshape(MLA_HEADS, ATT_TQ, 128)

    head128 = pl.BlockSpec((MLA_HEADS, ATT_TQ, 128), lambda i: (0, i, 0))
    return _grid_call(
        body, name, (s // ATT_TQ,),
        [pl.BlockSpec((MLA_HEADS, ATT_TQ, 256), lambda i: (0, i, 0)), pl.BlockSpec((s, 256), lambda i: (0, 0))],
        [head128, head128],
        [jax.ShapeDtypeStruct((MLA_HEADS, s, 128), BF16), jax.ShapeDtypeStruct((MLA_HEADS, s, 128), F32)],
        [qp, kp], scratch=[pltpu.VMEM((ATT_ROWS, 128), F32), pltpu.VMEM((ATT_ROWS, 256), F32)], side=side)


def _attn_bwd(qp, kp, o, do, lse, name, side=None):
    s = kp.shape[0]
    tk = min(ATT_TK, s)

    def body(q_ref, k_ref, o_ref, do_ref, lse_ref, dq_ref, dk_ref, dv_ref, dq_sc):
        i = pl.program_id(0)

        @pl.when(i == 0)
        def _():
            dk_ref[...] = jnp.zeros_like(dk_ref)
            dv_ref[...] = jnp.zeros_like(dv_ref)

        q = q_ref[...].reshape(ATT_ROWS, 256)
        dov = do_ref[...].reshape(ATT_ROWS, 128)
        delta = jnp.sum(dov.astype(F32) * o_ref[...].reshape(ATT_ROWS, 128).astype(F32), axis=1, keepdims=True)
        delta_t = _tile_lanes(jnp.broadcast_to(delta, (ATT_ROWS, 128)), tk)
        lse_t = _tile_lanes(lse_ref[...].reshape(ATT_ROWS, 128), tk)
        dq_sc[...] = jnp.zeros_like(dq_sc)

        def block(k0, width, masked):
            k = k_ref[pl.ds(k0, width), :]
            p = jnp.exp2(_dot(q, k, NT) * ATTN_SCALE_LOG2 - lse_t[:, 0:width])
            if masked:
                p = jnp.where(_visible(k0, i, width), p, 0.0)
            dv_ref[pl.ds(k0, width), :] += _dot(p, dov, TN)
            ds = (p * (_dot(dov, k[:, 0:128], NT) - delta_t[:, 0:width]) * ATTN_SCALE).astype(BF16)
            dq_sc[...] += _dot(ds, k, NN)
            dk_ref[pl.ds(k0, width), :] += _dot(ds, q, TN)

        _key_blocks(i, tk, block, pairs=True)
        dq_ref[...] = dq_sc[...].astype(BF16).reshape(MLA_HEADS, ATT_TQ, 256)

    head128 = pl.BlockSpec((MLA_HEADS, ATT_TQ, 128), lambda i: (0, i, 0))
    head256 = pl.BlockSpec((MLA_HEADS, ATT_TQ, 256), lambda i: (0, i, 0))
    return _grid_call(
        body, name, (s // ATT_TQ,),
        [head256, pl.BlockSpec((s, 256), lambda i: (0, 0)), head128, head128, head128],
        [head256, pl.BlockSpec((s, 256), lambda i: (0, 0)), pl.BlockSpec((s, 128), lambda i: (0, 0))],
        [jax.ShapeDtypeStruct((MLA_HEADS, s, 256), BF16),
         jax.ShapeDtypeStruct((s, 256), F32), jax.ShapeDtypeStruct((s, 128), F32)],
        [qp, kp, o, do, lse], scratch=[pltpu.VMEM((ATT_ROWS, 256), F32)], side=side)


def _place():
    x, y, c = lax.axis_index("x"), lax.axis_index("y"), lax.axis_index("c")
    return x, y, c, 4 * x + 2 * y + c


def _flip(x, y, c, r):
    px = 1 - x if r & 4 else x
    py = 1 - y if r & 2 else y
    pc = 1 - c if r & 1 else c
    return (px, py, pc), 4 * px + 2 * py + pc


def _adaln_exchange(c8, ada_w, ada_b_cols, blocks, name):
    d = c8.shape[1]
    w_cols = ada_w.shape[2]
    n_arr = len(blocks)

    def body(c_ref, w_ref, b_ref, *refs):
        x_refs, (call_ref, mod_ref), out_refs = refs[:n_arr], refs[n_arr:n_arr + 2], refs[n_arr + 2:2 * n_arr + 2]
        sbuf, rbuf, s1, r1, s2, r2 = refs[2 * n_arr + 2:2 * n_arr + 8]
        gather = _Gather(x_refs, out_refs, *refs[2 * n_arr + 8:])
        x, y, c, me = _place()
        call_ref[pl.ds(pl.multiple_of(me * 8, 8), 8), :] = c_ref[...]
        peers = [_flip(x, y, c, r) for r in range(1, N_DEV)]

        def c_copy(k, src_lin, to):
            rows = call_ref.at[pl.ds(pl.multiple_of(src_lin * 8, 8), 8), :]
            return pltpu.make_async_remote_copy(src_ref=rows, dst_ref=rows, send_sem=s1.at[k], recv_sem=r1.at[k],
                                                device_id=to, device_id_type=MESH)

        first = [c_copy(k, me, peer) for k, (peer, _) in enumerate(peers)]
        for cp in first:
            cp.start()
        for k, (_, lin) in enumerate(peers):
            c_copy(k, lin, (x, y, c)).wait_recv()
        for cp in first:
            cp.wait_send()

        for j in range(N_DEV):
            cj = call_ref[8 * j:8 * j + 8, :]
            cond = cj * _sigmoid(cj)
            for l in range(2):
                sbuf[j, l] = lax.dot_general(cond, w_ref[l], NN, precision=lax.Precision.HIGHEST,
                                             preferred_element_type=F32) + b_ref[l]

        def m_copy(k, src_slot, dst_slot, to):
            return pltpu.make_async_remote_copy(src_ref=sbuf.at[src_slot], dst_ref=rbuf.at[dst_slot],
                                                send_sem=s2.at[k], recv_sem=r2.at[k], device_id=to,
                                                device_id_type=MESH)

        rbuf[me] = sbuf[me]
        second = [m_copy(k, lin, me, peer) for k, (peer, lin) in enumerate(peers)]
        for cp in second:
            cp.start()
        gather.start()
        for k, (_, lin) in enumerate(peers):
            m_copy(k, lin, lin, (x, y, c)).wait_recv()
        for cp in second:
            cp.wait_send()
        for j in range(N_DEV):
            for l in range(2):
                mod_ref[l, :, w_cols * j:w_cols * (j + 1)] = rbuf[j, l]
        gather.forward()
        gather.end()

    vmem = pl.BlockSpec(memory_space=pltpu.VMEM)
    anyspace = pl.BlockSpec(memory_space=pl.ANY)
    g_shapes, g_sems = _gather_extras(blocks)
    return pl.pallas_call(
        body, name=name, in_specs=[vmem, vmem, vmem] + [anyspace] * n_arr, out_specs=[vmem, vmem] + [anyspace] * n_arr,
        out_shape=[jax.ShapeDtypeStruct((8 * N_DEV, d), F32), jax.ShapeDtypeStruct((2, 8, 3 * d), F32)] + g_shapes,
        scratch_shapes=[pltpu.VMEM((N_DEV, 2, 8, w_cols), F32), pltpu.VMEM((N_DEV, 2, 8, w_cols), F32),
                        pltpu.SemaphoreType.DMA((N_DEV - 1,)), pltpu.SemaphoreType.DMA((N_DEV - 1,)),
                        pltpu.SemaphoreType.DMA((N_DEV - 1,)), pltpu.SemaphoreType.DMA((N_DEV - 1,))] + g_sems,
        compiler_params=pltpu.CompilerParams(vmem_limit_bytes=VMEM_LIMIT),
    )(c8, ada_w, ada_b_cols, *blocks)


class _Gather:
    def __init__(self, x_refs, out_refs, send_sems, recv_sems, local_sems):
        x, y, c, _ = _place()
        self.me, self.sibling, self.c = (x, y, c), (x, y, 1 - c), c
        self.chips = [(1 - x, y), (x, 1 - y), (1 - x, 1 - y)]
        self.n_arr = len(x_refs)
        self.out_refs, self.send_sems, self.recv_sems = out_refs, send_sems, recv_sems
        self.mine = [pltpu.make_async_copy(x_refs[t], out_refs[t].at[4 * x + 2 * y + c], local_sems.at[t])
                     for t in range(self.n_arr)]
        self.first = []
        for t in range(self.n_arr):
            self.first.append(self.copy(t, 0, self.me, self.sibling, src=x_refs[t]))
            self.first += [self.copy(t, 1 + j, self.me, (*chip, c), src=x_refs[t]) for j, chip in enumerate(self.chips)]
        self.passed = [self.copy(t, 4 + j, (*chip, c), self.sibling)
                       for t in range(self.n_arr) for j, chip in enumerate(self.chips)]

    def copy(self, t, k, blk, to, src=None):
        slot = self.out_refs[t].at[4 * blk[0] + 2 * blk[1] + blk[2]]
        return pltpu.make_async_remote_copy(src_ref=slot if src is None else src, dst_ref=slot,
                                            send_sem=self.send_sems.at[7 * t + k], recv_sem=self.recv_sems.at[7 * t + k],
                                            device_id=to, device_id_type=MESH)

    def start(self):
        for cp in self.mine + self.first:
            cp.start()

    def forward(self):
        for t in range(self.n_arr):
            for j, chip in enumerate(self.chips):
                self.copy(t, 1 + j, (*chip, self.c), self.me).wait_recv()
                self.passed[3 * t + j].start()

    def end(self):
        for t in range(self.n_arr):
            self.copy(t, 0, self.sibling, self.me).wait_recv()
            for j, chip in enumerate(self.chips):
                self.copy(t, 4 + j, (*chip, 1 - self.c), self.me).wait_recv()
        for cp in self.first + self.passed:
            cp.wait_send()
        for cp in self.mine:
            cp.wait()


def _gather_extras(blocks):
    n_arr = len(blocks)
    return ([jax.ShapeDtypeStruct((N_DEV,) + b.shape, b.dtype) for b in blocks],
            [pltpu.SemaphoreType.DMA((7 * n_arr,)), pltpu.SemaphoreType.DMA((7 * n_arr,)),
             pltpu.SemaphoreType.DMA((n_arr,))])


def _all_gather(blocks, name):
    n_arr = len(blocks)

    def body(*refs):
        gather = _Gather(refs[:n_arr], refs[n_arr:2 * n_arr], *refs[2 * n_arr:])
        gather.start()
        gather.forward()
        gather.end()

    anyspace = pl.BlockSpec(memory_space=pl.ANY)
    shapes, sems = _gather_extras(blocks)
    return pl.pallas_call(body, name=name, in_specs=[anyspace] * n_arr, out_specs=[anyspace] * n_arr,
                          out_shape=shapes, scratch_shapes=sems)(*blocks)


def _scatter_parts(parts, name):
    n_arr = len(parts)

    def body(*refs):
        copies = _exchange_copies(refs[:n_arr], refs[n_arr:2 * n_arr], *refs[2 * n_arr:], "devices")
        _exchange_start(copies)
        _exchange_wait(copies)

    anyspace = pl.BlockSpec(memory_space=pl.ANY)
    shapes, sems = _exchange_extras(parts, "devices")
    return pl.pallas_call(body, name=name, in_specs=[anyspace] * n_arr, out_specs=[anyspace] * n_arr,
                          out_shape=shapes, scratch_shapes=sems)(*parts)


def _sum_parts(parts, name):
    def body(p_ref, g_ref):
        g = p_ref[0]
        for j in range(1, N_DEV):
            g = g + p_ref[j]
        g_ref[...] = g

    return pl.pallas_call(body, name=name, out_shape=jax.ShapeDtypeStruct(parts.shape[1:], F32),
                          compiler_params=pltpu.CompilerParams(vmem_limit_bytes=VMEM_LIMIT))(parts)


N_CHIPS = N_DEV // 2


def _sibling_swap(part, name):
    def body(g_ref, r_ref, send_sems, recv_sems):
        x, y, c, _ = _place()
        sends = [pltpu.make_async_remote_copy(
            src_ref=g_ref.at[2 * q + 1 - c], dst_ref=r_ref.at[q], send_sem=send_sems.at[q], recv_sem=recv_sems.at[q],
            device_id=(x, y, 1 - c), device_id_type=MESH) for q in range(N_CHIPS)]
        recvs = [pltpu.make_async_remote_copy(
            src_ref=g_ref.at[2 * q + c], dst_ref=r_ref.at[q], send_sem=send_sems.at[q], recv_sem=recv_sems.at[q],
            device_id=(x, y, c), device_id_type=MESH) for q in range(N_CHIPS)]
        for cp in sends:
            cp.start()
        for cp in recvs:
            cp.wait_recv()
        for cp in sends:
            cp.wait_send()

    anyspace = pl.BlockSpec(memory_space=pl.ANY)
    return pl.pallas_call(
        body, name=name, in_specs=[anyspace], out_specs=anyspace,
        out_shape=jax.ShapeDtypeStruct((N_CHIPS,) + part.shape[1:], part.dtype),
        scratch_shapes=[pltpu.SemaphoreType.DMA((N_CHIPS,)), pltpu.SemaphoreType.DMA((N_CHIPS,))])(part)


def _pair_sum(a, b, name):
    n, rows, cols = a.shape
    tr = max(t for t in range(16, 513, 16) if rows % t == 0)

    def body(a_ref, b_ref, o_ref):
        o_ref[...] = (a_ref[...].astype(F32) + b_ref[...].astype(F32)).astype(o_ref.dtype)

    blk = pl.BlockSpec((None, tr, cols), lambda q, i: (q, i, 0))
    return pl.pallas_call(body, name=name, grid=(n, rows // tr), in_specs=[blk, blk], out_specs=blk,
                          out_shape=jax.ShapeDtypeStruct(a.shape, a.dtype), compiler_params=_params(2))(a, b)


def _adamw(w, g, m, v):
    m = ADAM_B1 * m + (1.0 - ADAM_B1) * g
    v = ADAM_B2 * v + (1.0 - ADAM_B2) * (g * g)
    m_hat = m / (1.0 - ADAM_B1 ** ADAM_STEP)
    v_hat = v / (1.0 - ADAM_B2 ** ADAM_STEP)
    return -ADAM_LR * (m_hat / (jnp.sqrt(v_hat) + ADAM_EPS) + ADAM_WD * w), m, v


def _sum_parts_adamw(parts, w, m, v, name):
    n_parts, rows, cols = parts.shape
    tr = max(t for t in range(16, 257, 16) if rows % t == 0)

    def body(p_ref, w_ref, m_ref, v_ref, g_ref, d_ref, mo_ref, vo_ref):
        g = p_ref[0].astype(F32)
        for j in range(1, n_parts):
            g = g + p_ref[j].astype(F32)
        g_ref[...] = g
        d_ref[...], mo_ref[...], vo_ref[...] = _adamw(w_ref[...], g, m_ref[...], v_ref[...])

    row = _rows3(tr, cols)
    out = jax.ShapeDtypeStruct((1, rows, cols), F32)
    return pl.pallas_call(
        body, name=name, grid=(rows // tr,),
        in_specs=[pl.BlockSpec((n_parts, tr, cols), lambda i: (0, i, 0)), row, row, row],
        out_specs=[row, row, row, row], out_shape=[out, out, out, out], compiler_params=_params(1),
    )(parts, w, m, v)


def _sum_parts_adamw_whole(parts, w, m, v, name):
    def body(p_ref, w_ref, m_ref, v_ref, g_ref, d_ref, mo_ref, vo_ref):
        g = p_ref[0:1].astype(F32)
        for j in range(1, N_DEV):
            g = g + p_ref[j:j + 1].astype(F32)
        g_ref[...] = g
        d_ref[...], mo_ref[...], vo_ref[...] = _adamw(w_ref[...], g, m_ref[...], v_ref[...])

    out = jax.ShapeDtypeStruct(w.shape, F32)
    return pl.pallas_call(body, name=name, out_shape=[out] * 4,
                          compiler_params=pltpu.CompilerParams(vmem_limit_bytes=VMEM_LIMIT))(parts, w, m, v)


def _adamw_many(gs, ws, ms, vs, name):
    n = len(gs)

    def body(*refs):
        for k in range(n):
            g_ref, w_ref, m_ref, v_ref = (refs[q * n + k] for q in range(4))
            d_ref, mo_ref, vo_ref = (refs[(4 + q) * n + k] for q in range(3))
            d_ref[...], mo_ref[...], vo_ref[...] = _adamw(w_ref[...], g_ref[...], m_ref[...], v_ref[...])

    out = [jax.ShapeDtypeStruct(w.shape, F32) for w in ws]
    res = pl.pallas_call(body, name=name, out_shape=out * 3,
                         compiler_params=pltpu.CompilerParams(vmem_limit_bytes=VMEM_LIMIT))(*gs, *ws, *ms, *vs)
    return res[:n], res[n:2 * n], res[2 * n:]


def _ada_w_grad_adamw(c_all, dmod_rows, w, m, v, name):
    def body(c_ref, dm_ref, w_ref, m_ref, v_ref, g_ref, d_ref, mo_ref, vo_ref):
        cv = c_ref[...]
        g = lax.dot_general(cv * _sigmoid(cv), dm_ref[...], TN, precision=lax.Precision.HIGHEST,
                            preferred_element_type=F32)
        g_ref[...] = g
        d_ref[...], mo_ref[...], vo_ref[...] = _adamw(w_ref[...], g, m_ref[...], v_ref[...])

    n_layers, d, cols = w.shape
    layer = pl.BlockSpec((None, d, cols), lambda l: (l, 0, 0))
    out = jax.ShapeDtypeStruct(w.shape, F32)
    return pl.pallas_call(
        body, name=name, grid=(n_layers,),
        in_specs=[pl.BlockSpec(c_all.shape, lambda l: (0, 0)),
                  pl.BlockSpec((None,) + dmod_rows.shape[1:], lambda l: (l, 0, 0)), layer, layer, layer],
        out_specs=[layer] * 4, out_shape=[out] * 4, compiler_params=_params(1),
    )(c_all, dmod_rows, w, m, v)


REPLICATED = ("ln_g", "ln_b", "gmlp_norm_g", "gmlp_norm_b", "gmlp_ws", "gmlp_bs", "pool_b", "pool_scale",
              "mla_kv_norm_g", "mla_w_uk", "mla_w_uv")
CHUNK_ROWS, ADA_ROW, QNORM_ROW, LOSS_ROW, REP_ROWS = 73, 73, 74, 75, 80
UQ_ROWS, POOLW_ROWS = 96, 32


def _pad_rows(flat2d, rows):
    n, k = flat2d.shape
    return jnp.pad(flat2d, ((0, 0), (0, rows * LANES - k))).reshape(n, rows, LANES)


def _ada_cols_rows(vec):
    return _pad_rows(vec.reshape(2, N_DEV, -1).transpose(1, 0, 2).reshape(N_DEV, -1), 1)


def _unpack_replicated(rep, shapes):
    chunk = sum(s[1] for s in shapes) // N_DEV
    flat, off, out = rep[:, :CHUNK_ROWS].reshape(N_DEV, -1)[:, :chunk].reshape(-1), 0, {}
    for n, size, shape in shapes:
        out[n] = flat[off:off + size].reshape(shape)
        off += size
    cols = 3 * D_MODEL // N_DEV
    out["ada_b"] = rep[:, ADA_ROW, :2 * cols].reshape(N_DEV, 2, cols).transpose(1, 0, 2).reshape(2, -1)
    return out


def kernel(x, c, positions, ada_w, ada_b, ln_g, ln_b, e_w_in, gmlp_norm_g, gmlp_norm_b, gmlp_ws, gmlp_bs, pool_w, pool_b, pool_scale, e_w_out, o_w_in, mla_q_norm_g, mla_kv_norm_g, mla_w_uq, mla_w_uk, mla_w_uv, o_w_out, loss_target, m_ada_w, m_ada_b, m_ln_g, m_ln_b, m_e_w_in, m_gmlp_norm_g, m_gmlp_norm_b, m_gmlp_ws, m_gmlp_bs, m_pool_w, m_pool_b, m_pool_scale, m_e_w_out, m_o_w_in, m_mla_q_norm_g, m_mla_kv_norm_g, m_mla_w_uq, m_mla_w_uk, m_mla_w_uv, m_o_w_out, v_ada_w, v_ada_b, v_ln_g, v_ln_b, v_e_w_in, v_gmlp_norm_g, v_gmlp_norm_b, v_gmlp_ws, v_gmlp_bs, v_pool_w, v_pool_b, v_pool_scale, v_e_w_out, v_o_w_in, v_mla_q_norm_g, v_mla_kv_norm_g, v_mla_w_uq, v_mla_w_uk, v_mla_w_uv, v_o_w_out):
    w_in = dict(ada_w=ada_w, ada_b=ada_b, ln_g=ln_g, ln_b=ln_b, e_w_in=e_w_in, gmlp_norm_g=gmlp_norm_g,
                gmlp_norm_b=gmlp_norm_b, gmlp_ws=gmlp_ws, gmlp_bs=gmlp_bs, pool_w=pool_w, pool_b=pool_b,
                pool_scale=pool_scale, e_w_out=e_w_out, o_w_in=o_w_in, mla_q_norm_g=mla_q_norm_g,
                mla_kv_norm_g=mla_kv_norm_g, mla_w_uq=mla_w_uq, mla_w_uk=mla_w_uk, mla_w_uv=mla_w_uv, o_w_out=o_w_out)
    m_in = dict(ada_w=m_ada_w, ada_b=m_ada_b, ln_g=m_ln_g, ln_b=m_ln_b, e_w_in=m_e_w_in, gmlp_norm_g=m_gmlp_norm_g,
                gmlp_norm_b=m_gmlp_norm_b, gmlp_ws=m_gmlp_ws, gmlp_bs=m_gmlp_bs, pool_w=m_pool_w, pool_b=m_pool_b,
                pool_scale=m_pool_scale, e_w_out=m_e_w_out, o_w_in=m_o_w_in, mla_q_norm_g=m_mla_q_norm_g,
                mla_kv_norm_g=m_mla_kv_norm_g, mla_w_uq=m_mla_w_uq, mla_w_uk=m_mla_w_uk, mla_w_uv=m_mla_w_uv,
                o_w_out=m_o_w_out)
    v_in = dict(ada_w=v_ada_w, ada_b=v_ada_b, ln_g=v_ln_g, ln_b=v_ln_b, e_w_in=v_e_w_in, gmlp_norm_g=v_gmlp_norm_g,
                gmlp_norm_b=v_gmlp_norm_b, gmlp_ws=v_gmlp_ws, gmlp_bs=v_gmlp_bs, pool_w=v_pool_w, pool_b=v_pool_b,
                pool_scale=v_pool_scale, e_w_out=v_e_w_out, o_w_in=v_o_w_in, mla_q_norm_g=v_mla_q_norm_g,
                mla_kv_norm_g=v_mla_kv_norm_g, mla_w_uq=v_mla_w_uq, mla_w_uk=v_mla_w_uk, mla_w_uv=v_mla_w_uv,
                o_w_out=v_o_w_out)
    names = list(w_in)
    seq = x.shape[1]
    d = D_MODEL
    me = 4 * lax.axis_index("x") + 2 * lax.axis_index("y") + lax.axis_index("c")
    ada_cols = ada_w.shape[2]

    ada_b_cols = lax.dynamic_slice_in_dim(ada_b, me * ada_cols, ada_cols, axis=1)
    slab_row = lax.broadcasted_iota(jnp.int32, (8, d), 0)
    slab = jnp.where(slab_row == 0, c, jnp.where(slab_row == 1, jnp.pad(mla_q_norm_g, ((0, 0), (0, d - 32))), 0.0))
    c_all, mod, w_in_e3, pool_w3 = _adaln_exchange(
        slab, ada_w, jnp.broadcast_to(ada_b_cols[:, None, :], (2, 8, ada_cols)),
        [e_w_in[0].astype(BF16), pool_w.astype(BF16).reshape(POOLW_ROWS, LANES)], "adaln_exchange")
    h0 = _modulate(x, mod[0], "modulate0")
    proj0, o_in3 = _matmul_cols_nn(h0, w_in_e3, BF16, 512, "even_in", side=([o_w_in[0].astype(BF16)], "gather"))
    o_in_full = o_in3.transpose(1, 0, 2).reshape(d, ODD_IN)
    w_in_o = jnp.concatenate([o_in_full[:, :448], jnp.zeros((d, 64), BF16), o_in_full[:, 448:]], axis=1)
    pool_w_full = pool_w3.reshape(N_DEV, 4, 32, 256).transpose(1, 0, 2, 3).reshape(4, 256, 256)
    g_q = c_all.reshape(N_DEV, 8, d)[:, 1, :32].reshape(1, MLA_Q_RANK)

    ws, bs_col = gmlp_ws[0], gmlp_bs[0].reshape(GMLP_HEADS, GMLP_BLOCK, 1)
    wuk2, wuv2 = mla_w_uk[0].reshape(MLA_KV_RANK, -1), mla_w_uv[0].reshape(MLA_KV_RANK, -1)
    inv = 1.0 / (ROPE_THETA ** (jnp.arange(0, MLA_ROPE, 2, dtype=F32) / MLA_ROPE))
    ang = positions[0].astype(F32)[:, None] * inv
    cosp = jnp.tile(jnp.cos(ang), (1, 4))
    sinp = jnp.tile(jnp.concatenate([-jnp.sin(ang), jnp.sin(ang)], axis=1), (1, 2))

    mix0, w_out_e3 = _even_fwd(proj0, ws, bs_col, gmlp_norm_g, gmlp_norm_b, pool_w_full, pool_b, pool_scale, "even_mix",
                               side=([e_w_out[0].astype(BF16)], "gather"))
    w_out_e = w_out_e3.reshape(-1, d)
    y0, uq3 = _matmul([(mix0, w_out_e)], "nn", F32, seq, d, 512, 1024, "even_out",
                      side=([mla_w_uq.astype(BF16).reshape(UQ_ROWS, LANES)], "gather"))
    uq_full = uq3.reshape(MLA_Q_RANK, MLA_HEADS, MLA_NOPE + MLA_ROPE)
    w_uq_n = uq_full[:, :, :MLA_NOPE].reshape(MLA_Q_RANK, -1)
    w_uq_r = uq_full[:, :, MLA_NOPE:].reshape(MLA_Q_RANK, -1)
    w_uq = jnp.concatenate([w_uq_n, w_uq_r], axis=1)
    x1, h1 = _resid_ln(x, y0, mod[0], ln_g[0:1], ln_b[0:1], mod[1], "resid_ln0")

    (proj1,) = _matmul([(h1, w_in_o)], "nn", BF16, seq, ODD_IN_PAD, 512, ODD_IN_PAD, "odd_in")
    qn, kp = _mla_prep(proj1, cosp, sinp, g_q, mla_kv_norm_g, "mla_prep")
    (q_up,) = _matmul([(qn, w_uq)], "nn", BF16, seq, 3072, 512, 3072, "q_up")
    qp = _q_heads(q_up, cosp, sinp, wuk2, "q_heads")
    o_lat, lse, w_out_o3 = _attn_fwd(qp, kp, "attn_fwd", side=([o_w_out[0].astype(BF16)], "gather"))
    w_out_o = w_out_o3.reshape(-1, d)
    gated = _o_gate(o_lat, proj1, wuv2, "o_gate")
    (y1,) = _matmul([(gated, w_out_o)], "nn", F32, seq, d, 512, 1024, "odd_out")

    dy1, dxres1, red2 = _final_ln_loss_bwd(x1, y1, mod[1], ln_g[1:2], ln_b[1:2], loss_target, "final_ln_loss")
    (dgated,) = _matmul([(dy1, w_out_o)], "nt", BF16, seq, MLA_WIDTH, 512, MLA_WIDTH, "odd_out_dx")
    (g_w_out_o,) = _matmul([(gated, dy1)], "tn", BF16, MLA_WIDTH, d, 256, d, "odd_out_dw")
    dproj1_z, do_lat, g_wuv = _o_gate_bwd(dgated, o_lat, proj1, wuv2, "o_gate_bwd")
    dqp, dkp, dvv, r_o_out = _attn_bwd(qp, kp, o_lat, do_lat, lse, "attn_bwd",
                                       side=([g_w_out_o.reshape(N_DEV, -1, d)], "devices"))
    dq_nope, dq_rope, g_wuk = _q_heads_bwd(dqp, q_up, cosp, sinp, wuk2, "q_heads_bwd")
    (dqn,) = _matmul([(dq_nope, w_uq_n), (dq_rope, w_uq_r)], "nt", F32, seq, MLA_Q_RANK, 512, 256, "q_up_dx")
    (g_wuq_n,) = _matmul([(qn, dq_nope)], "tn", F32, MLA_Q_RANK, MLA_WIDTH, 256, MLA_WIDTH, "q_up_dw_nope")
    (g_wuq_r,) = _matmul([(qn, dq_rope)], "tn", F32, MLA_Q_RANK, 1024, 256, 1024, "q_up_dw_rope")
    dproj1, red_mla = _mla_prep_bwd(proj1, dqn, dkp, dvv, cosp, sinp, g_q, mla_kv_norm_g, dproj1_z, "mla_prep_bwd")
    (dh1,) = _matmul([(dproj1, w_in_o)], "nt", F32, seq, d, 512, d, "odd_in_dx")
    part_uq = jnp.concatenate([g_wuq_n.reshape(MLA_Q_RANK, MLA_HEADS, MLA_NOPE),
                               g_wuq_r.reshape(MLA_Q_RANK, MLA_HEADS, MLA_ROPE)], axis=2).astype(BF16).reshape(
                                   (N_DEV,) + mla_w_uq.shape[1:])
    (g_w_in_o,) = _matmul([(h1, dproj1)], "tn", BF16, d, ODD_IN_PAD, 256, ODD_IN_PAD // 2, "odd_in_dw", n_outer=True)
    part_o_in = jnp.concatenate([g_w_in_o[:, :448], g_w_in_o[:, 512:]], axis=1).reshape(d, N_DEV, -1).transpose(1, 0, 2)
    dy0, dxres0, red1 = _mid_bwd(dh1, dxres1, x, y0, mod[0], mod[1], ln_g[0:1], ln_b[0:1], "mid_bwd")
    dmix, r_uq = _matmul([(dy0, w_out_e)], "nt", BF16, seq, 2048, 512, 2048, "even_out_dx", side=([part_uq], "devices"))
    (g_w_out_e,) = _matmul([(mix0, dy0)], "tn", BF16, 2048, d, 256, d, "even_out_dw")
    dproj0, g_ws, g_bs, g_ng, g_nb, g_pw, g_pb, g_ps, r_o_in = _even_bwd(
        proj0, dmix, ws, bs_col, gmlp_norm_g, gmlp_norm_b, pool_w_full, pool_b, pool_scale, "even_mix_bwd",
        side=([part_o_in], "devices"))
    part_pw = g_pw.reshape(4, N_DEV, 32, 256).transpose(1, 0, 2, 3)
    part_e_in, r_e_out, r_pw = _matmul_cols_tn(h0, dproj0, w_in_e3.shape[2], BF16, 512, "even_in_dw",
                                               side=([g_w_out_e.reshape(N_DEV, -1, d), part_pw], "devices"))
    mine = lax.dynamic_index_in_dim(part_e_in.reshape((N_CHIPS, 2) + part_e_in.shape[1:]), lax.axis_index("c"), 1, False)
    chip_e_in = _pair_sum(mine, _sibling_swap(part_e_in, "e_in_sibling_swap"), "e_in_pair_sum")
    grad_x, red0, r_e_in = _first_bwd(dproj0, w_in_e3, dxres0, x, mod[0], "even_in_dx", side=([chip_e_in], "chips"))

    t_mask = lax.broadcasted_iota(jnp.int32, (GMLP_BLOCK, GMLP_BLOCK), 0) // CHUNK
    s_mask = lax.broadcasted_iota(jnp.int32, (GMLP_BLOCK, GMLP_BLOCK), 1) // CHUNK
    part = {
        "ln_g": jnp.stack([red1[2], red2[0]]), "ln_b": jnp.stack([red1[3], red2[1]]),
        "gmlp_norm_g": g_ng, "gmlp_norm_b": g_nb,
        "gmlp_ws": jnp.where(s_mask <= t_mask, g_ws, 0.0), "gmlp_bs": g_bs,
        "pool_b": g_pb, "pool_scale": g_ps, "mla_kv_norm_g": red_mla[1, :MLA_KV_RANK],
        "mla_w_uk": g_wuk, "mla_w_uv": g_wuv,
    }
    dmod = jnp.stack([jnp.concatenate([red0[1], red0[0], red1[4]]),
                      jnp.concatenate([red1[1], red1[0], red2[2]])])

    loss_row = jnp.pad(jnp.broadcast_to((0.5 / d * jnp.sum(red2[3])).reshape(1, 1, 1), (N_DEV, 1, 1)),
                       ((0, 0), (0, 0), (0, LANES - 1)))
    part_small = jnp.concatenate([
        _pad_rows(jnp.concatenate([part[n].reshape(-1) for n in REPLICATED]).reshape(N_DEV, -1), CHUNK_ROWS),
        jnp.pad(jnp.concatenate([_ada_cols_rows(dmod), _pad_rows(red_mla[0].reshape(N_DEV, -1), 1), loss_row], axis=1),
                ((0, 0), (0, REP_ROWS - LOSS_ROW - 1), (0, 0)))], axis=1)
    (r_small,) = _scatter_parts([part_small], "grad_scatter")
    small_sum = _sum_parts(r_small, "small_sum")
    loss = small_sum[LOSS_ROW, 0]
    (rep_sum,) = _all_gather([small_sum], "replicated_gather")

    res = {"e_w_in": _sum_parts_adamw(r_e_in, e_w_in, m_e_w_in, v_e_w_in, "adamw_e_w_in"),
           "o_w_in": _sum_parts_adamw(r_o_in, o_w_in, m_o_w_in, v_o_w_in, "adamw_o_w_in"),
           "e_w_out": _sum_parts_adamw(r_e_out, e_w_out, m_e_w_out, v_e_w_out, "adamw_e_w_out"),
           "o_w_out": _sum_parts_adamw(r_o_out, o_w_out, m_o_w_out, v_o_w_out, "adamw_o_w_out"),
           "mla_w_uq": _sum_parts_adamw_whole(r_uq, mla_w_uq, m_mla_w_uq, v_mla_w_uq, "adamw_w_uq"),
           "pool_w": _sum_parts_adamw_whole(r_pw, pool_w, m_pool_w, v_pool_w, "adamw_pool_w")}
    grads = _unpack_replicated(rep_sum, [(n, w_in[n].size, w_in[n].shape) for n in REPLICATED])
    grads["mla_q_norm_g"] = small_sum[QNORM_ROW:QNORM_ROW + 1, :32]
    small_names = list(grads)
    deltas, new_ms, new_vs = _adamw_many([grads[n] for n in small_names], [w_in[n] for n in small_names],
                                         [m_in[n] for n in small_names], [v_in[n] for n in small_names], "small_adamw")
    for k, n in enumerate(small_names):
        res[n] = [grads[n], deltas[k], new_ms[k], new_vs[k]]
    dmod_all = r_small[:, ADA_ROW, :2 * ada_cols].reshape(N_DEV, 2, ada_cols).transpose(1, 0, 2)
    dmod_rows = jnp.pad(dmod_all[:, :, None, :], ((0, 0), (0, 0), (0, 7), (0, 0))).reshape(2, 8 * N_DEV, ada_cols)
    res["ada_w"] = _ada_w_grad_adamw(c_all, dmod_rows, ada_w, m_ada_w, v_ada_w, "ada_w_adamw")

    return (loss, grad_x, *[res[n][0] for n in names], *[res[n][1] for n in names],
            *[res[n][2] for n in names], *[res[n][3] for n in names])
```

```python
import functools

import jax
import jax.numpy as jnp
from jax import lax
from jax.experimental import pallas as pl
from jax.experimental.pallas import tpu as pltpu

F32 = jnp.float32
BF16 = jnp.bfloat16

D_MODEL = 1024
CHUNK = 64
LN_EPS = 1e-5
GMLP_HEADS = 4
GMLP_HEAD_DIM = 256
GMLP_BLOCK = 128
POOL_WINDOWS = (2, 4, 8, 16)
POOL_GROUP_DIM = 256
POOL_HALO = 16
MLA_HEADS = 16
MLA_NOPE = 128
MLA_ROPE = 64
MLA_Q_RANK = 256
MLA_KV_RANK = 128
MLA_WIDTH = 2048
ODD_IN = 2496
ODD_IN_PAD = 2560
ROPE_THETA = 10000.0
ATTN_SCALE = (MLA_NOPE + MLA_ROPE) ** -0.5
ATTN_SCALE_LOG2 = ATTN_SCALE * 1.4426950408889634
DEEPNORM_ALPHA = 4.0 ** 0.25
ADAM_LR, ADAM_B1, ADAM_B2, ADAM_EPS, ADAM_WD, ADAM_STEP = 0.001, 0.9, 0.999, 1e-8, 0.01, 10
N_DEV = 8
LANES = 1024
VMEM_LIMIT = 56 * 1024 * 1024
MESH = pl.DeviceIdType.MESH

NT = (((1,), (1,)), ((), ()))
NN = (((1,), (0,)), ((), ()))
TN = (((0,), (0,)), ((), ()))


def _params(n_axes):
    return pltpu.CompilerParams(dimension_semantics=("arbitrary",) * n_axes, vmem_limit_bytes=VMEM_LIMIT)


def _dot(a, b, dn):
    return lax.dot_general(a.astype(BF16), b.astype(BF16), dn, preferred_element_type=F32)


def _sigmoid(z):
    return 1.0 / (1.0 + jnp.exp(-z))


def _colsum(t):
    return jnp.sum(t, axis=0, keepdims=True)


EXCHANGE_RELATIONS = {"devices": tuple(range(1, N_DEV)), "chips": (2, 4, 6)}


def _exchange_copies(g_refs, r_refs, send_sems, recv_sems, local_sems, kind):
    x, y, c, me = _place()
    n_arr = len(g_refs)

    def slot(lin):
        return lin // 2 if kind == "chips" else lin

    own = [pltpu.make_async_copy(g_refs[t].at[slot(me)], r_refs[t].at[slot(me)], local_sems.at[t]) for t in range(n_arr)]
    sends, recvs = [], []
    for n, r in enumerate(EXCHANGE_RELATIONS[kind]):
        peer, lin = _flip(x, y, c, r)
        for t in range(n_arr):
            k = n_arr * n + t
            sends.append(pltpu.make_async_remote_copy(
                src_ref=g_refs[t].at[slot(lin)], dst_ref=r_refs[t].at[slot(me)], send_sem=send_sems.at[k],
                recv_sem=recv_sems.at[k], device_id=peer, device_id_type=MESH))
            recvs.append(pltpu.make_async_remote_copy(
                src_ref=g_refs[t].at[slot(lin)], dst_ref=r_refs[t].at[slot(lin)], send_sem=send_sems.at[k],
                recv_sem=recv_sems.at[k], device_id=(x, y, c), device_id_type=MESH))
    return own, sends, recvs


def _exchange_start(copies):
    own, sends, _ = copies
    for cp in own + sends:
        cp.start()


def _exchange_wait(copies):
    own, sends, recvs = copies
    for cp in recvs:
        cp.wait_recv()
    for cp in sends:
        cp.wait_send()
    for cp in own:
        cp.wait()


def _exchange_extras(parts, kind):
    shapes = [jax.ShapeDtypeStruct(p.shape, p.dtype) for p in parts]
    n = len(parts) * len(EXCHANGE_RELATIONS[kind])
    return shapes, [pltpu.SemaphoreType.DMA((n,)), pltpu.SemaphoreType.DMA((n,)), pltpu.SemaphoreType.DMA((len(parts),))]


def _grid_call(body, name, grid, in_specs, out_specs, out_shape, args, scratch=(), side=None):
    if side is None:
        return pl.pallas_call(body, name=name, grid=grid, in_specs=in_specs, out_specs=out_specs,
                              out_shape=out_shape, scratch_shapes=list(scratch),
                              compiler_params=_params(len(grid)))(*args)
    parts, kind = side
    gather = kind == "gather"
    n_in, n_out, n_sc, n_arr = len(args), len(out_shape), len(scratch), len(parts)
    side_shapes, side_sems = _gather_extras(parts) if gather else _exchange_extras(parts, kind)
    mid = tuple(g // 2 for g in grid)

    def wrapped(*refs):
        ins, g_refs = refs[:n_in], refs[n_in:n_in + n_arr]
        outs = refs[n_in + n_arr:n_in + n_arr + n_out]
        r_refs = refs[n_in + n_arr + n_out:n_in + 2 * n_arr + n_out]
        sc = refs[n_in + 2 * n_arr + n_out:n_in + 2 * n_arr + n_out + n_sc]
        ids = [pl.program_id(a) for a in range(len(grid))]

        def at(step):
            return functools.reduce(jnp.logical_and, [i == s for i, s in zip(ids, step)])

        first, last = at((0,) * len(grid)), at(tuple(g - 1 for g in grid))
        if gather:
            exchange = _Gather(g_refs, r_refs, *refs[-3:])
            pl.when(first)(exchange.start)
            if mid != (0,) * len(grid):
                pl.when(at(mid))(exchange.forward)
            body(*ins, *outs, *sc)

            @pl.when(last)
            def _():
                if mid == (0,) * len(grid):
                    exchange.forward()
                exchange.end()
        else:
            copies = _exchange_copies(g_refs, r_refs, *refs[-3:], kind)
            pl.when(first)(lambda: _exchange_start(copies))
            body(*ins, *outs, *sc)
            pl.when(last)(lambda: _exchange_wait(copies))

    anyspace = pl.BlockSpec(memory_space=pl.ANY)
    return pl.pallas_call(
        wrapped, name=name, grid=grid, in_specs=list(in_specs) + [anyspace] * n_arr,
        out_specs=list(out_specs) + [anyspace] * n_arr, out_shape=list(out_shape) + side_shapes,
        scratch_shapes=list(scratch) + side_sems, compiler_params=_params(len(grid)),
    )(*args, *parts)


def _matmul(pairs, mode, out_dtype, m, n, tm, tn, name, side=None, n_outer=False):
    dn = {"nn": NN, "nt": NT, "tn": TN}[mode]
    tm, tn = min(tm, m), min(tn, n)
    n_pairs = len(pairs)
    grid = (n // tn, m // tm) if n_outer else (m // tm, n // tn)

    def ij(f):
        return (lambda j, i: f(i, j)) if n_outer else f

    def body(*refs):
        o_ref = refs[-1]
        acc = None
        for p in range(n_pairs):
            t = _dot(refs[2 * p][...], refs[2 * p + 1][...], dn)
            acc = t if acc is None else acc + t
        o_ref[...] = acc.astype(o_ref.dtype)

    in_specs, args = [], []
    for a, b in pairs:
        if mode == "nn":
            k = a.shape[1]
            in_specs += [pl.BlockSpec((tm, k), ij(lambda i, j: (i, 0))), pl.BlockSpec((k, tn), ij(lambda i, j: (0, j)))]
        elif mode == "nt":
            k = a.shape[1]
            in_specs += [pl.BlockSpec((tm, k), ij(lambda i, j: (i, 0))), pl.BlockSpec((tn, k), ij(lambda i, j: (j, 0)))]
        else:
            k = a.shape[0]
            in_specs += [pl.BlockSpec((k, tm), ij(lambda i, j: (0, i))), pl.BlockSpec((k, tn), ij(lambda i, j: (0, j)))]
        args += [a, b]
    return _grid_call(body, name, grid, in_specs, [pl.BlockSpec((tm, tn), ij(lambda i, j: (i, j)))],
                      [jax.ShapeDtypeStruct((m, n), out_dtype)], args, side=side)


def _matmul_cols_nn(a, w3, out_dtype, tm, name, side=None):
    m, k = a.shape
    _, _, n = w3.shape
    tm = min(tm, m)

    def body(a_ref, w_ref, o_ref):
        av = a_ref[...]
        for j in range(N_DEV):
            o_ref[:, n * j:n * (j + 1)] = _dot(av, w_ref[j], NN).astype(o_ref.dtype)

    return _grid_call(
        body, name, (m // tm,),
        [pl.BlockSpec((tm, k), lambda i: (i, 0)), pl.BlockSpec((N_DEV, k, n), lambda i: (0, 0, 0))],
        [pl.BlockSpec((tm, N_DEV * n), lambda i: (i, 0))], [jax.ShapeDtypeStruct((m, N_DEV * n), out_dtype)], [a, w3],
        side=side)


def _matmul_cols_tn(a, b, n, out_dtype, tk, name, side=None):
    m, k = a.shape
    tk = min(tk, k)

    def body(a_ref, b_ref, o_ref):
        o_ref[...] = _dot(a_ref[...], b_ref[...], TN).astype(o_ref.dtype)

    return _grid_call(
        body, name, (N_DEV, k // tk),
        [pl.BlockSpec((m, tk), lambda j, i: (0, i)), pl.BlockSpec((m, n), lambda j, i: (0, j))],
        [pl.BlockSpec((None, tk, n), lambda j, i: (j, i, 0))], [jax.ShapeDtypeStruct((N_DEV, k, n), out_dtype)], [a, b],
        side=side)


def _rows3(tm, d):
    return pl.BlockSpec((None, tm, d), lambda i: (0, i, 0))


def _modulate(x, mod, name):
    _, s, d = x.shape
    tm = min(s, 512)

    def body(x_ref, m_ref, h_ref):
        shift, scale = m_ref[0:1, 0:d], m_ref[0:1, d:2 * d]
        h_ref[...] = (x_ref[...] * (1.0 + scale) + shift).astype(BF16)

    return pl.pallas_call(
        body, name=name, grid=(s // tm,),
        in_specs=[_rows3(tm, d), pl.BlockSpec((8, 3 * d), lambda i: (0, 0))],
        out_specs=pl.BlockSpec((tm, d), lambda i: (i, 0)),
        out_shape=jax.ShapeDtypeStruct((s, d), BF16), compiler_params=_params(1),
    )(x, mod)


def _ln_stats(r):
    mu = jnp.mean(r, axis=-1, keepdims=True)
    rc = r - mu
    var = jnp.mean(rc * rc, axis=-1, keepdims=True)
    rstd = lax.rsqrt(var + LN_EPS)
    return rc * rstd, rstd


def _ln_bwd(dxhat, xhat, rstd):
    return rstd * (dxhat - jnp.mean(dxhat, axis=-1, keepdims=True)
                   - xhat * jnp.mean(dxhat * xhat, axis=-1, keepdims=True))


def _resid_ln(x, y, mod, g, b, mod_next, name):
    _, s, d = x.shape
    tm = min(s, 512)

    def body(x_ref, y_ref, m_ref, g_ref, b_ref, mn_ref, o_ref, h_ref):
        gate = m_ref[0:1, 2 * d:3 * d]
        xhat, _ = _ln_stats(DEEPNORM_ALPHA * x_ref[...] + (1.0 + gate) * y_ref[...])
        out = xhat * g_ref[...] + b_ref[...]
        o_ref[...] = out
        h_ref[...] = (out * (1.0 + mn_ref[0:1, d:2 * d]) + mn_ref[0:1, 0:d]).astype(BF16)

    row = pl.BlockSpec((tm, d), lambda i: (i, 0))
    vec = pl.BlockSpec((1, d), lambda i: (0, 0))
    modspec = pl.BlockSpec((8, 3 * d), lambda i: (0, 0))
    return pl.pallas_call(
        body, name=name, grid=(s // tm,),
        in_specs=[_rows3(tm, d), row, modspec, vec, vec, modspec],
        out_specs=[_rows3(tm, d), row],
        out_shape=[jax.ShapeDtypeStruct((1, s, d), F32), jax.ShapeDtypeStruct((s, d), BF16)],
        compiler_params=_params(1),
    )(x, y, mod, g, b, mod_next)


def _final_ln_loss_bwd(x, y, mod, g, b, target, name):
    _, s, d = x.shape
    tm = min(s, 512)

    def body(x_ref, y_ref, m_ref, g_ref, b_ref, t_ref, dy_ref, dx_ref, red_ref):
        @pl.when(pl.program_id(0) == 0)
        def _():
            red_ref[...] = jnp.zeros_like(red_ref)

        gate = m_ref[0:1, 2 * d:3 * d]
        yv = y_ref[...]
        xhat, rstd = _ln_stats(DEEPNORM_ALPHA * x_ref[...] + (1.0 + gate) * yv)
        err = xhat * g_ref[...] + b_ref[...] - t_ref[...]
        dout = err * (1.0 / d)
        dr = _ln_bwd(dout * g_ref[...], xhat, rstd)
        dy_ref[...] = ((1.0 + gate) * dr).astype(BF16)
        dx_ref[...] = DEEPNORM_ALPHA * dr
        red_ref[0:1, :] += _colsum(dout * xhat)
        red_ref[1:2, :] += _colsum(dout)
        red_ref[2:3, :] += _colsum(dr * yv)
        red_ref[3:4, :] += _colsum(err * err)

    row = pl.BlockSpec((tm, d), lambda i: (i, 0))
    vec = pl.BlockSpec((1, d), lambda i: (0, 0))
    return pl.pallas_call(
        body, name=name, grid=(s // tm,),
        in_specs=[_rows3(tm, d), row, pl.BlockSpec((8, 3 * d), lambda i: (0, 0)), vec, vec, _rows3(tm, d)],
        out_specs=[row, row, pl.BlockSpec((8, d), lambda i: (0, 0))],
        out_shape=[jax.ShapeDtypeStruct((s, d), BF16), jax.ShapeDtypeStruct((s, d), F32),
                   jax.ShapeDtypeStruct((8, d), F32)],
        compiler_params=_params(1),
    )(x, y, mod, g, b, target)


def _mid_bwd(dh, dxres, x, y, mod_lo, mod_hi, g, b, name):
    _, s, d = x.shape
    tm = min(s, 512)

    def body(dh_ref, dxr_ref, x_ref, y_ref, ml_ref, mh_ref, g_ref, b_ref, dy_ref, dx_ref, red_ref):
        @pl.when(pl.program_id(0) == 0)
        def _():
            red_ref[...] = jnp.zeros_like(red_ref)

        gate = ml_ref[0:1, 2 * d:3 * d]
        scale_hi = mh_ref[0:1, d:2 * d]
        yv, dhv = y_ref[...], dh_ref[...]
        xhat, rstd = _ln_stats(DEEPNORM_ALPHA * x_ref[...] + (1.0 + gate) * yv)
        x_mid = xhat * g_ref[...] + b_ref[...]
        dx_mid = dxr_ref[...] + dhv * (1.0 + scale_hi)
        dr = _ln_bwd(dx_mid * g_ref[...], xhat, rstd)
        dy_ref[...] = ((1.0 + gate) * dr).astype(BF16)
        dx_ref[...] = DEEPNORM_ALPHA * dr
        red_ref[0:1, :] += _colsum(dhv * x_mid)
        red_ref[1:2, :] += _colsum(dhv)
        red_ref[2:3, :] += _colsum(dx_mid * xhat)
        red_ref[3:4, :] += _colsum(dx_mid)
        red_ref[4:5, :] += _colsum(dr * yv)

    row = pl.BlockSpec((tm, d), lambda i: (i, 0))
    vec = pl.BlockSpec((1, d), lambda i: (0, 0))
    modspec = pl.BlockSpec((8, 3 * d), lambda i: (0, 0))
    return pl.pallas_call(
        body, name=name, grid=(s // tm,),
        in_specs=[row, row, _rows3(tm, d), row, modspec, modspec, vec, vec],
        out_specs=[row, row, pl.BlockSpec((8, d), lambda i: (0, 0))],
        out_shape=[jax.ShapeDtypeStruct((s, d), BF16), jax.ShapeDtypeStruct((s, d), F32),
                   jax.ShapeDtypeStruct((8, d), F32)],
        compiler_params=_params(1),
    )(dh, dxres, x, y, mod_lo, mod_hi, g, b)


def _first_bwd(dproj, w3, dxres, x, mod, name, side=None):
    _, s, d = x.shape
    n = w3.shape[2]
    tm = min(s, 256)

    def body(a_ref, w_ref, dxr_ref, x_ref, m_ref, gx_ref, red_ref):
        @pl.when(pl.program_id(0) == 0)
        def _():
            red_ref[...] = jnp.zeros_like(red_ref)

        dhv = _dot(a_ref[:, 0:n], w_ref[0], NT)
        for j in range(1, N_DEV):
            dhv = dhv + _dot(a_ref[:, n * j:n * (j + 1)], w_ref[j], NT)
        gx_ref[...] = dxr_ref[...] + dhv * (1.0 + m_ref[0:1, d:2 * d])
        red_ref[0:1, :] += _colsum(dhv * x_ref[...])
        red_ref[1:2, :] += _colsum(dhv)

    return _grid_call(
        body, name, (s // tm,),
        [pl.BlockSpec((tm, N_DEV * n), lambda i: (i, 0)), pl.BlockSpec((N_DEV, d, n), lambda i: (0, 0, 0)),
         pl.BlockSpec((tm, d), lambda i: (i, 0)), _rows3(tm, d), pl.BlockSpec((8, 3 * d), lambda i: (0, 0))],
        [_rows3(tm, d), pl.BlockSpec((8, d), lambda i: (0, 0))],
        [jax.ShapeDtypeStruct((1, s, d), F32), jax.ShapeDtypeStruct((8, d), F32)],
        [dproj, w3, dxres, x, mod], side=side)


EVEN_TM = 512


def _gmlp_mask():
    t = lax.broadcasted_iota(jnp.int32, (GMLP_BLOCK, GMLP_BLOCK), 0) // CHUNK
    s = lax.broadcasted_iota(jnp.int32, (GMLP_BLOCK, GMLP_BLOCK), 1) // CHUNK
    return s <= t


def _window_sum(ext, win, back):
    n = ext.shape[0]
    k = 1
    while k < win:
        ext = ext + pltpu.roll(ext, k if back else n - k, 0)
        k *= 2
    return ext


def _inv_count(row0, rows, win):
    t = row0 + lax.broadcasted_iota(jnp.int32, (rows, 1), 0)
    return t, 1.0 / jnp.minimum(t + 1, win).astype(F32)


def _pooled(xb, halo, row0, win):
    tm = xb.shape[0]
    sums = _window_sum(jnp.concatenate([halo, xb], axis=0), win, True)[POOL_HALO:]
    _, inv = _inv_count(row0, tm, win)
    return sums * inv - xb


def _even_fwd(proj, ws, bs_col, ng, nb, pw, pb, ps, name, side=None):
    s = proj.shape[0]
    tm = min(s, EVEN_TM)
    hd, gd = GMLP_HEAD_DIM, POOL_GROUP_DIM

    def body(p_ref, halo_ref, ws_ref, bs_ref, ng_ref, nb_ref, pw_ref, pb_ref, ps_ref, m_ref):
        i = pl.program_id(0)
        mask = _gmlp_mask()
        for h in range(GMLP_HEADS):
            wm = jnp.where(mask, ws_ref[h], 0.0).astype(BF16)
            for blk in range(tm // GMLP_BLOCK):
                rows = slice(blk * GMLP_BLOCK, (blk + 1) * GMLP_BLOCK)
                cu, cv, cz = h * hd, 1024 + h * hd, 2048 + h * hd
                vhat, _ = _ln_stats(p_ref[rows, cv:cv + hd].astype(F32))
                vn = vhat * ng_ref[...] + nb_ref[...]
                sv = _dot(wm, vn, NN) + bs_ref[h]
                za = p_ref[rows, cz:cz + hd].astype(F32)
                m_ref[rows, cu:cu + hd] = (p_ref[rows, cu:cu + hd].astype(F32) * sv * (za * _sigmoid(za))).astype(BF16)
        for g, win in enumerate(POOL_WINDOWS):
            cx, cz = 3072 + g * gd, 4096 + g * gd
            halo = jnp.where(i > 0, halo_ref[:, g * gd:(g + 1) * gd].astype(F32), 0.0)
            pooled = _pooled(p_ref[:, cx:cx + gd].astype(F32), halo, i * tm, win)
            yb = _dot(pooled, pw_ref[g], NN) + pb_ref[:, g * gd:(g + 1) * gd]
            zb = p_ref[:, cz:cz + gd].astype(F32)
            m_ref[:, 1024 + g * gd:1024 + (g + 1) * gd] = (
                yb * ps_ref[:, g * gd:(g + 1) * gd] * (zb * _sigmoid(zb))).astype(BF16)

    hb = tm // POOL_HALO
    return _grid_call(
        body, name, (s // tm,),
        [
            pl.BlockSpec((tm, 5120), lambda i: (i, 0)),
            pl.BlockSpec((POOL_HALO, 1024), lambda i: (jnp.maximum(i * hb - 1, 0), 3)),
            pl.BlockSpec((GMLP_HEADS, GMLP_BLOCK, GMLP_BLOCK), lambda i: (0, 0, 0)),
            pl.BlockSpec((GMLP_HEADS, GMLP_BLOCK, 1), lambda i: (0, 0, 0)),
            pl.BlockSpec((1, hd), lambda i: (0, 0)), pl.BlockSpec((1, hd), lambda i: (0, 0)),
            pl.BlockSpec((4, gd, gd), lambda i: (0, 0, 0)),
            pl.BlockSpec((1, 1024), lambda i: (0, 0)), pl.BlockSpec((1, 1024), lambda i: (0, 0)),
        ],
        [pl.BlockSpec((tm, 2048), lambda i: (i, 0))], [jax.ShapeDtypeStruct((s, 2048), BF16)],
        [proj, proj, ws, bs_col, ng, nb, pw, pb, ps], side=side)


def _even_bwd(proj, dm, ws, bs_col, ng, nb, pw, pb, ps, name, side=None):
    s = proj.shape[0]
    tm = min(s, EVEN_TM)
    hd, gd = GMLP_HEAD_DIM, POOL_GROUP_DIM
    n_tiles = s // tm

    def body(p_ref, halo_ref, zbn_ref, dm_ref, dbn_ref, ws_ref, bs_ref, ng_ref, nb_ref, pw_ref, pb_ref, ps_ref,
             dp_ref, dws_ref, dbs_ref, dng_ref, dnb_ref, dpw_ref, dpb_ref, dps_ref):
        i = pl.program_id(0)

        @pl.when(i == 0)
        def _():
            for r in (dws_ref, dbs_ref, dng_ref, dnb_ref, dpw_ref, dpb_ref, dps_ref):
                r[...] = jnp.zeros_like(r)

        mask = _gmlp_mask()
        for h in range(GMLP_HEADS):
            wm = jnp.where(mask, ws_ref[h], 0.0).astype(BF16)
            for blk in range(tm // GMLP_BLOCK):
                rows = slice(blk * GMLP_BLOCK, (blk + 1) * GMLP_BLOCK)
                cu, cv, cz = h * hd, 1024 + h * hd, 2048 + h * hd
                vhat, rstd = _ln_stats(p_ref[rows, cv:cv + hd].astype(F32))
                vn = (vhat * ng_ref[...] + nb_ref[...]).astype(BF16)
                sv = _dot(wm, vn, NN) + bs_ref[h]
                u, za = p_ref[rows, cu:cu + hd].astype(F32), p_ref[rows, cz:cz + hd].astype(F32)
                da = dm_ref[rows, cu:cu + hd].astype(F32)
                sig = _sigmoid(za)
                sa = za * sig
                dau = da * u
                dsv = dau * sa
                dp_ref[rows, cu:cu + hd] = (da * sv * sa).astype(BF16)
                dp_ref[rows, cz:cz + hd] = (dau * sv * (sig * (1.0 + za * (1.0 - sig)))).astype(BF16)
                dsv_b = dsv.astype(BF16)
                dbs_ref[h] += jnp.sum(dsv, axis=1, keepdims=True)
                dws_ref[h] += _dot(dsv_b, vn, NT)
                dvn = _dot(wm, dsv_b, TN)
                dng_ref[...] += _colsum(dvn * vhat)
                dnb_ref[...] += _colsum(dvn)
                dp_ref[rows, cv:cv + hd] = _ln_bwd(dvn * ng_ref[...], vhat, rstd).astype(BF16)

        row0 = i * tm
        for g, win in enumerate(POOL_WINDOWS):
            cx, cz, cd = 3072 + g * gd, 4096 + g * gd, 1024 + g * gd
            gs = slice(g * gd, (g + 1) * gd)
            halo = jnp.where(i > 0, halo_ref[:, gs].astype(F32), 0.0)
            xb = p_ref[:, cx:cx + gd].astype(F32)
            pooled = _pooled(xb, halo, row0, win).astype(BF16)
            scale_g = ps_ref[:, gs]
            yb = _dot(pooled, pw_ref[g], NN) + pb_ref[:, gs]
            zb, db = p_ref[:, cz:cz + gd].astype(F32), dm_ref[:, cd:cd + gd].astype(F32)
            sig = _sigmoid(zb)
            dyp = db * (zb * sig)
            dp_ref[:, cz:cz + gd] = (db * yb * scale_g * (sig * (1.0 + zb * (1.0 - sig)))).astype(BF16)
            dps_ref[:, gs] += _colsum(dyp * yb)
            dpb_ref[:, gs] += _colsum(dyp * scale_g)
            zb_ext = jnp.concatenate([zb, zbn_ref[:, gs].astype(F32)], axis=0)
            db_ext = jnp.concatenate([db, dbn_ref[:, gs].astype(F32)], axis=0)
            dy_ext = (db_ext * (zb_ext * _sigmoid(zb_ext)) * scale_g).astype(BF16)
            dpw_ref[g] += _dot(pooled, dy_ext[:tm], TN)
            dpooled = _dot(dy_ext, pw_ref[g], NT)
            t, inv = _inv_count(row0, tm + POOL_HALO, win)
            w_ext = jnp.where(t < s, dpooled * inv, 0.0)
            dp_ref[:, cx:cx + gd] = (_window_sum(w_ext, win, False)[:tm] - dpooled[:tm]).astype(BF16)

    hb = tm // POOL_HALO
    last = s // POOL_HALO - 1
    small = lambda shape: pl.BlockSpec(shape, lambda i: (0,) * len(shape))
    return _grid_call(
        body, name, (n_tiles,),
        [
            pl.BlockSpec((tm, 5120), lambda i: (i, 0)),
            pl.BlockSpec((POOL_HALO, 1024), lambda i: (jnp.maximum(i * hb - 1, 0), 3)),
            pl.BlockSpec((POOL_HALO, 1024), lambda i: (jnp.minimum((i + 1) * hb, last), 4)),
            pl.BlockSpec((tm, 2048), lambda i: (i, 0)),
            pl.BlockSpec((POOL_HALO, 1024), lambda i: (jnp.minimum((i + 1) * hb, last), 1)),
            small((GMLP_HEADS, GMLP_BLOCK, GMLP_BLOCK)), small((GMLP_HEADS, GMLP_BLOCK, 1)),
            small((1, hd)), small((1, hd)), small((4, gd, gd)), small((1, 1024)), small((1, 1024)),
        ],
        [
            pl.BlockSpec((tm, 5120), lambda i: (i, 0)),
            small((GMLP_HEADS, GMLP_BLOCK, GMLP_BLOCK)), small((GMLP_HEADS, GMLP_BLOCK, 1)),
            small((1, hd)), small((1, hd)), small((4, gd, gd)), small((1, 1024)), small((1, 1024)),
        ],
        [
            jax.ShapeDtypeStruct((s, 5120), BF16),
            jax.ShapeDtypeStruct((GMLP_HEADS, GMLP_BLOCK, GMLP_BLOCK), F32),
            jax.ShapeDtypeStruct((GMLP_HEADS, GMLP_BLOCK, 1), F32),
            jax.ShapeDtypeStruct((1, hd), F32), jax.ShapeDtypeStruct((1, hd), F32),
            jax.ShapeDtypeStruct((4, gd, gd), F32),
            jax.ShapeDtypeStruct((1, 1024), F32), jax.ShapeDtypeStruct((1, 1024), F32),
        ],
        [proj, proj, proj, dm, dm, ws, bs_col, ng, nb, pw, pb, ps], side=side)


def _rope_pair_swap(t):
    lane = lax.broadcasted_iota(jnp.int32, t.shape, 1)
    return jnp.where(lane % 64 < 32, pltpu.roll(t, 96, 1), pltpu.roll(t, 32, 1))


def _rms(x, g):
    r = lax.rsqrt(jnp.mean(x * x, axis=-1, keepdims=True) + LN_EPS)
    return x * r, r


def _rms_bwd(dy, g, xhat, r):
    dyg = dy * g
    return r * (dyg - xhat * jnp.mean(dyg * xhat, axis=-1, keepdims=True))


def _lane_lt(shape, n):
    return lax.broadcasted_iota(jnp.int32, shape, 1) < n


def _mla_prep(proj, cosp, sinp, gq, gkv, name):
    s = proj.shape[0]
    tm = min(s, 512)

    def body(qc_ref, kv_ref, c_ref, s_ref, gq_ref, gkv_ref, qn_ref, kp_ref):
        qhat, _ = _rms(qc_ref[...].astype(F32), None)
        qn_ref[...] = (qhat * gq_ref[...]).astype(BF16)
        khat, _ = _rms(kv_ref[:, 0:128].astype(F32), None)
        kp_ref[:, 0:128] = (khat * gkv_ref[...]).astype(BF16)
        kr = kv_ref[:, 128:256].astype(F32)
        kp_ref[:, 128:256] = (kr * c_ref[...] + _rope_pair_swap(kr) * s_ref[...]).astype(BF16)

    return pl.pallas_call(
        body, name=name, grid=(s // tm,),
        in_specs=[pl.BlockSpec((tm, 256), lambda i: (i, 0)), pl.BlockSpec((tm, 256), lambda i: (i, 1)),
                  pl.BlockSpec((tm, 128), lambda i: (i, 0)), pl.BlockSpec((tm, 128), lambda i: (i, 0)),
                  pl.BlockSpec((1, 256), lambda i: (0, 0)), pl.BlockSpec((1, 128), lambda i: (0, 0))],
        out_specs=[pl.BlockSpec((tm, 256), lambda i: (i, 0)), pl.BlockSpec((tm, 256), lambda i: (i, 0))],
        out_shape=[jax.ShapeDtypeStruct((s, 256), BF16), jax.ShapeDtypeStruct((s, 256), BF16)],
        compiler_params=_params(1),
    )(proj, proj, cosp, sinp, gq, gkv)


def _mla_prep_bwd(proj, dqn, dkp, dv, cosp, sinp, gq, gkv, dproj, name):
    s = proj.shape[0]
    tm = min(s, 512)

    def body(qc_ref, kv_ref, dqn_ref, dkp_ref, dv_ref, c_ref, s_ref, gq_ref, gkv_ref, dproj_ref, o_ref, red_ref):
        @pl.when(pl.program_id(0) == 0)
        def _():
            red_ref[...] = jnp.zeros_like(red_ref)

        qhat, qr = _rms(qc_ref[...].astype(F32), None)
        dq = dqn_ref[...]
        o_ref[:, 0:256] = _rms_bwd(dq, gq_ref[...], qhat, qr).astype(BF16)
        red_ref[0:1, :] += _colsum(dq * qhat)
        khat, kr = _rms(kv_ref[:, 0:128].astype(F32), None)
        dk = dkp_ref[:, 0:128] + dv_ref[...]
        o_ref[:, 256:384] = _rms_bwd(dk, gkv_ref[...], khat, kr).astype(BF16)
        red_ref[1:2, 0:128] += _colsum(dk * khat)
        dr = dkp_ref[:, 128:256]
        o_ref[:, 384:512] = (dr * c_ref[...] - _rope_pair_swap(dr) * s_ref[...]).astype(BF16)

    return pl.pallas_call(
        body, name=name, grid=(s // tm,),
        in_specs=[pl.BlockSpec((tm, 256), lambda i: (i, 0)), pl.BlockSpec((tm, 256), lambda i: (i, 1)),
                  pl.BlockSpec((tm, 256), lambda i: (i, 0)), pl.BlockSpec((tm, 256), lambda i: (i, 0)),
                  pl.BlockSpec((tm, 128), lambda i: (i, 0)),
                  pl.BlockSpec((tm, 128), lambda i: (i, 0)), pl.BlockSpec((tm, 128), lambda i: (i, 0)),
                  pl.BlockSpec((1, 256), lambda i: (0, 0)), pl.BlockSpec((1, 128), lambda i: (0, 0)),
                  pl.BlockSpec(memory_space=pl.ANY)],
        out_specs=[pl.BlockSpec((tm, 512), lambda i: (i, 0)), pl.BlockSpec((8, 256), lambda i: (0, 0))],
        out_shape=[jax.ShapeDtypeStruct(dproj.shape, BF16), jax.ShapeDtypeStruct((8, 256), F32)],
        input_output_aliases={9: 0}, compiler_params=_params(1),
    )(proj, proj, dqn, dkp, dv, cosp, sinp, gq, gkv, dproj)


HEADS_TM = 512
Z_COL0 = ODD_IN_PAD - MLA_WIDTH


def _head_cols(h):
    return slice(128 * h, 128 * h + 128)


def _q_heads(q_up, cosp, sinp, wuk, name):
    s = q_up.shape[0]
    tm = min(s, HEADS_TM)

    def body(q_ref, c_ref, s_ref, w_ref, o_ref):
        for p in range(MLA_HEADS // 2):
            raw = q_ref[:, MLA_WIDTH + 128 * p:MLA_WIDTH + 128 * (p + 1)].astype(F32)
            rot = raw * c_ref[...] + _rope_pair_swap(raw) * s_ref[...]
            low = _lane_lt(rot.shape, 64)
            o_ref[2 * p, :, 128:256] = jnp.where(low, rot, 0.0).astype(BF16)
            o_ref[2 * p + 1, :, 128:256] = jnp.where(low, pltpu.roll(rot, 64, 1), 0.0).astype(BF16)
        for h in range(MLA_HEADS):
            o_ref[h, :, 0:128] = _dot(q_ref[:, _head_cols(h)], w_ref[:, _head_cols(h)], NT).astype(BF16)

    return pl.pallas_call(
        body, name=name, grid=(s // tm,),
        in_specs=[pl.BlockSpec((tm, 3072), lambda i: (i, 0)),
                  pl.BlockSpec((tm, 128), lambda i: (i, 0)), pl.BlockSpec((tm, 128), lambda i: (i, 0)),
                  pl.BlockSpec((128, MLA_WIDTH), lambda i: (0, 0))],
        out_specs=pl.BlockSpec((MLA_HEADS, tm, 256), lambda i: (0, i, 0)),
        out_shape=jax.ShapeDtypeStruct((MLA_HEADS, s, 256), BF16), compiler_params=_params(1),
    )(q_up, cosp, sinp, wuk)


def _q_heads_bwd(dqp, q_up, cosp, sinp, wuk, name):
    s = q_up.shape[0]
    tm = min(s, HEADS_TM)

    def body(dq_ref, qn_ref, c_ref, s_ref, w_ref, dn_ref, dr_ref, dw_ref):
        @pl.when(pl.program_id(0) == 0)
        def _():
            dw_ref[...] = jnp.zeros_like(dw_ref)

        for h in range(MLA_HEADS):
            dlat = dq_ref[h, :, 0:128]
            dn_ref[:, _head_cols(h)] = _dot(dlat, w_ref[:, _head_cols(h)], NN).astype(BF16)
            dw_ref[:, _head_cols(h)] += _dot(dlat, qn_ref[:, _head_cols(h)], TN)
        for p in range(MLA_HEADS // 2):
            drot = dq_ref[2 * p, :, 128:256].astype(F32) + pltpu.roll(dq_ref[2 * p + 1, :, 128:256].astype(F32), 64, 1)
            dr_ref[:, _head_cols(p)] = (drot * c_ref[...] - _rope_pair_swap(drot) * s_ref[...]).astype(BF16)

    return pl.pallas_call(
        body, name=name, grid=(s // tm,),
        in_specs=[pl.BlockSpec((MLA_HEADS, tm, 256), lambda i: (0, i, 0)),
                  pl.BlockSpec((tm, MLA_WIDTH), lambda i: (i, 0)),
                  pl.BlockSpec((tm, 128), lambda i: (i, 0)), pl.BlockSpec((tm, 128), lambda i: (i, 0)),
                  pl.BlockSpec((128, MLA_WIDTH), lambda i: (0, 0))],
        out_specs=[pl.BlockSpec((tm, MLA_WIDTH), lambda i: (i, 0)),
                   pl.BlockSpec((tm, 1024), lambda i: (i, 0)),
                   pl.BlockSpec((128, MLA_WIDTH), lambda i: (0, 0))],
        out_shape=[jax.ShapeDtypeStruct((s, MLA_WIDTH), BF16), jax.ShapeDtypeStruct((s, 1024), BF16),
                   jax.ShapeDtypeStruct((128, MLA_WIDTH), F32)],
        compiler_params=_params(1),
    )(dqp, q_up, cosp, sinp, wuk)


def _o_gate(o_lat, proj, wuv, name):
    s = o_lat.shape[1]
    tm = min(s, HEADS_TM)

    def body(ol_ref, p_ref, w_ref, g_ref):
        for h in range(MLA_HEADS):
            z = p_ref[:, Z_COL0 + 128 * h:Z_COL0 + 128 * (h + 1)].astype(F32)
            g_ref[:, _head_cols(h)] = (_dot(ol_ref[h], w_ref[:, _head_cols(h)], NN) * (z * _sigmoid(z))).astype(BF16)

    return pl.pallas_call(
        body, name=name, grid=(s // tm,),
        in_specs=[pl.BlockSpec((MLA_HEADS, tm, 128), lambda i: (0, i, 0)),
                  pl.BlockSpec((tm, ODD_IN_PAD), lambda i: (i, 0)),
                  pl.BlockSpec((128, MLA_WIDTH), lambda i: (0, 0))],
        out_specs=pl.BlockSpec((tm, MLA_WIDTH), lambda i: (i, 0)),
        out_shape=jax.ShapeDtypeStruct((s, MLA_WIDTH), BF16), compiler_params=_params(1),
    )(o_lat, proj, wuv)


def _o_gate_bwd(dg, o_lat, proj, wuv, name):
    s = o_lat.shape[1]
    tm = min(s, HEADS_TM)

    def body(dg_ref, ol_ref, p_ref, w_ref, dp_ref, dol_ref, dw_ref):
        @pl.when(pl.program_id(0) == 0)
        def _():
            dw_ref[...] = jnp.zeros_like(dw_ref)

        dp_ref[:, 0:Z_COL0] = jnp.zeros((tm, Z_COL0), BF16)
        for h in range(MLA_HEADS):
            zc = slice(Z_COL0 + 128 * h, Z_COL0 + 128 * (h + 1))
            z, dgv, ol = p_ref[:, zc].astype(F32), dg_ref[:, _head_cols(h)].astype(F32), ol_ref[h]
            sig = _sigmoid(z)
            o = _dot(ol, w_ref[:, _head_cols(h)], NN)
            dp_ref[:, zc] = (dgv * o * (sig * (1.0 + z * (1.0 - sig)))).astype(BF16)
            do = (dgv * (z * sig)).astype(BF16)
            dol_ref[h] = _dot(do, w_ref[:, _head_cols(h)], NT).astype(BF16)
            dw_ref[:, _head_cols(h)] += _dot(ol, do, TN)

    return pl.pallas_call(
        body, name=name, grid=(s // tm,),
        in_specs=[pl.BlockSpec((tm, MLA_WIDTH), lambda i: (i, 0)),
                  pl.BlockSpec((MLA_HEADS, tm, 128), lambda i: (0, i, 0)),
                  pl.BlockSpec((tm, ODD_IN_PAD), lambda i: (i, 0)),
                  pl.BlockSpec((128, MLA_WIDTH), lambda i: (0, 0))],
        out_specs=[pl.BlockSpec((tm, ODD_IN_PAD), lambda i: (i, 0)),
                   pl.BlockSpec((MLA_HEADS, tm, 128), lambda i: (0, i, 0)),
                   pl.BlockSpec((128, MLA_WIDTH), lambda i: (0, 0))],
        out_shape=[jax.ShapeDtypeStruct((s, ODD_IN_PAD), BF16), jax.ShapeDtypeStruct((MLA_HEADS, s, 128), BF16),
                   jax.ShapeDtypeStruct((128, MLA_WIDTH), F32)],
        compiler_params=_params(1),
    )(dg, o_lat, proj, wuv)


ATT_TQ = CHUNK
ATT_ROWS = ATT_TQ * MLA_HEADS
ATT_TK = 512
ATT_HEAD_GROUP = 8


def _visible(k0, q_chunk, tk):
    kpos = k0 + lax.broadcasted_iota(jnp.int32, (1, tk), 1)
    return kpos // CHUNK <= q_chunk


def _tile_lanes(t, n):
    return jnp.concatenate([t] * (n // 128), axis=1)


def _key_blocks(i, tk, block, pairs=False):
    visible = i * ATT_TQ + ATT_TQ
    n_full = (visible + tk - 1) // tk - 1

    def full(j):
        block(pl.multiple_of(j * tk, tk), tk, False)

    if pairs:
        def two(jj, carry):
            full(2 * jj)
            full(2 * jj + 1)
            return carry

        lax.fori_loop(0, n_full // 2, two, 0)

        @pl.when(n_full % 2 == 1)
        def _():
            full(n_full - 1)
    else:
        def one(j, carry):
            full(j)
            return carry

        lax.fori_loop(0, n_full, one, 0)
    last0 = pl.multiple_of(n_full * tk, tk)
    half = tk // 2
    if half % 128 == 0:
        @pl.when(visible - n_full * tk <= half)
        def _():
            block(last0, half, True)

        @pl.when(visible - n_full * tk > half)
        def _():
            block(last0, tk, True)
    else:
        block(last0, tk, True)


def _attn_fwd(qp, kp, name, side=None):
    s = kp.shape[0]
    tk = min(ATT_TK, s)

    def body(q_ref, k_ref, o_ref, lse_ref, m_sc, acc_sc):
        i = pl.program_id(0)
        m_sc[...] = jnp.full_like(m_sc, -jnp.inf)
        acc_sc[...] = jnp.zeros_like(acc_sc)

        def block(k0, width, masked):
            k = k_ref[pl.ds(k0, width), :]
            v1 = jnp.where(_lane_lt(k.shape, 128), k, jnp.ones_like(k))
            for h0 in range(0, MLA_HEADS, ATT_HEAD_GROUP):
                rows = slice(h0 * ATT_TQ, (h0 + ATT_HEAD_GROUP) * ATT_TQ)
                q = q_ref[h0:h0 + ATT_HEAD_GROUP].reshape(ATT_HEAD_GROUP * ATT_TQ, 256)
                sc = _dot(q, k, NT) * ATTN_SCALE_LOG2
                if masked:
                    sc = jnp.where(_visible(k0, i, width), sc, -jnp.inf)
                m_prev = m_sc[rows]
                m_new = jnp.maximum(m_prev, jnp.max(sc, axis=1, keepdims=True))
                p = jnp.exp2(sc - _tile_lanes(m_new, width))
                acc_sc[rows] = _tile_lanes(jnp.exp2(m_prev - m_new), 256) * acc_sc[rows] + _dot(p, v1, NN)
                m_sc[rows] = m_new

        _key_blocks(i, tk, block, pairs=True)
        acc = acc_sc[...]
        l = acc[:, 128:256]
        o_ref[...] = (acc[:, 0:128] / l).astype(BF16).reshape(MLA_HEADS, ATT_TQ, 128)
        lse_ref[...] = (m_sc[...] + jnp.log2(l)).reshape(MLA_HEADS, ATT_TQ, 128)

    head128 = pl.BlockSpec((MLA_HEADS, ATT_TQ, 128), lambda i: (0, i, 0))
    return _grid_call(
        body, name, (s // ATT_TQ,),
        [pl.BlockSpec((MLA_HEADS, ATT_TQ, 256), lambda i: (0, i, 0)), pl.BlockSpec((s, 256), lambda i: (0, 0))],
        [head128, head128],
        [jax.ShapeDtypeStruct((MLA_HEADS, s, 128), BF16), jax.ShapeDtypeStruct((MLA_HEADS, s, 128), F32)],
        [qp, kp], scratch=[pltpu.VMEM((ATT_ROWS, 128), F32), pltpu.VMEM((ATT_ROWS, 256), F32)], side=side)


def _attn_bwd(qp, kp, o, do, lse, name, side=None):
    s = kp.shape[0]
    tk = min(ATT_TK, s)

    def body(q_ref, k_ref, o_ref, do_ref, lse_ref, dq_ref, dk_ref, dv_ref, dq_sc):
        i = pl.program_id(0)

        @pl.when(i == 0)
        def _():
            dk_ref[...] = jnp.zeros_like(dk_ref)
            dv_ref[...] = jnp.zeros_like(dv_ref)

        q = q_ref[...].reshape(ATT_ROWS, 256)
        dov = do_ref[...].reshape(ATT_ROWS, 128)
        delta = jnp.sum(dov.astype(F32) * o_ref[...].reshape(ATT_ROWS, 128).astype(F32), axis=1, keepdims=True)
        delta_t = _tile_lanes(jnp.broadcast_to(delta, (ATT_ROWS, 128)), tk)
        lse_t = _tile_lanes(lse_ref[...].reshape(ATT_ROWS, 128), tk)
        dq_sc[...] = jnp.zeros_like(dq_sc)

        def block(k0, width, masked):
            k = k_ref[pl.ds(k0, width), :]
            p = jnp.exp2(_dot(q, k, NT) * ATTN_SCALE_LOG2 - lse_t[:, 0:width])
            if masked:
                p = jnp.where(_visible(k0, i, width), p, 0.0)
            dv_ref[pl.ds(k0, width), :] += _dot(p, dov, TN)
            ds = (p * (_dot(dov, k[:, 0:128], NT) - delta_t[:, 0:width]) * ATTN_SCALE).astype(BF16)
            dq_sc[...] += _dot(ds, k, NN)
            dk_ref[pl.ds(k0, width), :] += _dot(ds, q, TN)

        _key_blocks(i, tk, block, pairs=True)
        dq_ref[...] = dq_sc[...].astype(BF16).reshape(MLA_HEADS, ATT_TQ, 256)

    head128 = pl.BlockSpec((MLA_HEADS, ATT_TQ, 128), lambda i: (0, i, 0))
    head256 = pl.BlockSpec((MLA_HEADS, ATT_TQ, 256), lambda i: (0, i, 0))
    return _grid_call(
        body, name, (s // ATT_TQ,),
        [head256, pl.BlockSpec((s, 256), lambda i: (0, 0)), head128, head128, head128],
        [head256, pl.BlockSpec((s, 256), lambda i: (0, 0)), pl.BlockSpec((s, 128), lambda i: (0, 0))],
        [jax.ShapeDtypeStruct((MLA_HEADS, s, 256), BF16),
         jax.ShapeDtypeStruct((s, 256), F32), jax.ShapeDtypeStruct((s, 128), F32)],
        [qp, kp, o, do, lse], scratch=[pltpu.VMEM((ATT_ROWS, 256), F32)], side=side)


def _place():
    x, y, c = lax.axis_index("x"), lax.axis_index("y"), lax.axis_index("c")
    return x, y, c, 4 * x + 2 * y + c


def _flip(x, y, c, r):
    px = 1 - x if r & 4 else x
    py = 1 - y if r & 2 else y
    pc = 1 - c if r & 1 else c
    return (px, py, pc), 4 * px + 2 * py + pc


def _adaln_exchange(c8, ada_w, ada_b_cols, blocks, name):
    d = c8.shape[1]
    w_cols = ada_w.shape[2]
    n_arr = len(blocks)

    def body(c_ref, w_ref, b_ref, *refs):
        x_refs, (call_ref, mod_ref), out_refs = refs[:n_arr], refs[n_arr:n_arr + 2], refs[n_arr + 2:2 * n_arr + 2]
        sbuf, rbuf, s1, r1, s2, r2 = refs[2 * n_arr + 2:2 * n_arr + 8]
        gather = _Gather(x_refs, out_refs, *refs[2 * n_arr + 8:])
        x, y, c, me = _place()
        call_ref[pl.ds(pl.multiple_of(me * 8, 8), 8), :] = c_ref[...]
        peers = [_flip(x, y, c, r) for r in range(1, N_DEV)]

        def c_copy(k, src_lin, to):
            rows = call_ref.at[pl.ds(pl.multiple_of(src_lin * 8, 8), 8), :]
            return pltpu.make_async_remote_copy(src_ref=rows, dst_ref=rows, send_sem=s1.at[k], recv_sem=r1.at[k],
                                                device_id=to, device_id_type=MESH)

        first = [c_copy(k, me, peer) for k, (peer, _) in enumerate(peers)]
        for cp in first:
            cp.start()
        for k, (_, lin) in enumerate(peers):
            c_copy(k, lin, (x, y, c)).wait_recv()
        for cp in first:
            cp.wait_send()

        for j in range(N_DEV):
            cj = call_ref[8 * j:8 * j + 8, :]
            cond = cj * _sigmoid(cj)
            for l in range(2):
                sbuf[j, l] = lax.dot_general(cond, w_ref[l], NN, precision=lax.Precision.HIGHEST,
                                             preferred_element_type=F32) + b_ref[l]

        def m_copy(k, src_slot, dst_slot, to):
            return pltpu.make_async_remote_copy(src_ref=sbuf.at[src_slot], dst_ref=rbuf.at[dst_slot],
                                                send_sem=s2.at[k], recv_sem=r2.at[k], device_id=to,
                                                device_id_type=MESH)

        rbuf[me] = sbuf[me]
        second = [m_copy(k, lin, me, peer) for k, (peer, lin) in enumerate(peers)]
        for cp in second:
            cp.start()
        gather.start()
        for k, (_, lin) in enumerate(peers):
            m_copy(k, lin, lin, (x, y, c)).wait_recv()
        for cp in second:
            cp.wait_send()
        for j in range(N_DEV):
            for l in range(2):
                mod_ref[l, :, w_cols * j:w_cols * (j + 1)] = rbuf[j, l]
        gather.forward()
        gather.end()

    vmem = pl.BlockSpec(memory_space=pltpu.VMEM)
    anyspace = pl.BlockSpec(memory_space=pl.ANY)
    g_shapes, g_sems = _gather_extras(blocks)
    return pl.pallas_call(
        body, name=name, in_specs=[vmem, vmem, vmem] + [anyspace] * n_arr, out_specs=[vmem, vmem] + [anyspace] * n_arr,
        out_shape=[jax.ShapeDtypeStruct((8 * N_DEV, d), F32), jax.ShapeDtypeStruct((2, 8, 3 * d), F32)] + g_shapes,
        scratch_shapes=[pltpu.VMEM((N_DEV, 2, 8, w_cols), F32), pltpu.VMEM((N_DEV, 2, 8, w_cols), F32),
                        pltpu.SemaphoreType.DMA((N_DEV - 1,)), pltpu.SemaphoreType.DMA((N_DEV - 1,)),
                        pltpu.SemaphoreType.DMA((N_DEV - 1,)), pltpu.SemaphoreType.DMA((N_DEV - 1,))] + g_sems,
        compiler_params=pltpu.CompilerParams(vmem_limit_bytes=VMEM_LIMIT),
    )(c8, ada_w, ada_b_cols, *blocks)


class _Gather:
    def __init__(self, x_refs, out_refs, send_sems, recv_sems, local_sems):
        x, y, c, _ = _place()
        self.me, self.sibling, self.c = (x, y, c), (x, y, 1 - c), c
        self.chips = [(1 - x, y), (x, 1 - y), (1 - x, 1 - y)]
        self.n_arr = len(x_refs)
        self.out_refs, self.send_sems, self.recv_sems = out_refs, send_sems, recv_sems
        self.mine = [pltpu.make_async_copy(x_refs[t], out_refs[t].at[4 * x + 2 * y + c], local_sems.at[t])
                     for t in range(self.n_arr)]
        self.first = []
        for t in range(self.n_arr):
            self.first.append(self.copy(t, 0, self.me, self.sibling, src=x_refs[t]))
            self.first += [self.copy(t, 1 + j, self.me, (*chip, c), src=x_refs[t]) for j, chip in enumerate(self.chips)]
        self.passed = [self.copy(t, 4 + j, (*chip, c), self.sibling)
                       for t in range(self.n_arr) for j, chip in enumerate(self.chips)]

    def copy(self, t, k, blk, to, src=None):
        slot = self.out_refs[t].at[4 * blk[0] + 2 * blk[1] + blk[2]]
        return pltpu.make_async_remote_copy(src_ref=slot if src is None else src, dst_ref=slot,
                                            send_sem=self.send_sems.at[7 * t + k], recv_sem=self.recv_sems.at[7 * t + k],
                                            device_id=to, device_id_type=MESH)

    def start(self):
        for cp in self.mine + self.first:
            cp.start()

    def forward(self):
        for t in range(self.n_arr):
            for j, chip in enumerate(self.chips):
                self.copy(t, 1 + j, (*chip, self.c), self.me).wait_recv()
                self.passed[3 * t + j].start()

    def end(self):
        for t in range(self.n_arr):
            self.copy(t, 0, self.sibling, self.me).wait_recv()
            for j, chip in enumerate(self.chips):
                self.copy(t, 4 + j, (*chip, 1 - self.c), self.me).wait_recv()
        for cp in self.first + self.passed:
            cp.wait_send()
        for cp in self.mine:
            cp.wait()


def _gather_extras(blocks):
    n_arr = len(blocks)
    return ([jax.ShapeDtypeStruct((N_DEV,) + b.shape, b.dtype) for b in blocks],
            [pltpu.SemaphoreType.DMA((7 * n_arr,)), pltpu.SemaphoreType.DMA((7 * n_arr,)),
             pltpu.SemaphoreType.DMA((n_arr,))])


def _all_gather(blocks, name):
    n_arr = len(blocks)

    def body(*refs):
        gather = _Gather(refs[:n_arr], refs[n_arr:2 * n_arr], *refs[2 * n_arr:])
        gather.start()
        gather.forward()
        gather.end()

    anyspace = pl.BlockSpec(memory_space=pl.ANY)
    shapes, sems = _gather_extras(blocks)
    return pl.pallas_call(body, name=name, in_specs=[anyspace] * n_arr, out_specs=[anyspace] * n_arr,
                          out_shape=shapes, scratch_shapes=sems)(*blocks)


def _scatter_parts(parts, name):
    n_arr = len(parts)

    def body(*refs):
        copies = _exchange_copies(refs[:n_arr], refs[n_arr:2 * n_arr], *refs[2 * n_arr:], "devices")
        _exchange_start(copies)
        _exchange_wait(copies)

    anyspace = pl.BlockSpec(memory_space=pl.ANY)
    shapes, sems = _exchange_extras(parts, "devices")
    return pl.pallas_call(body, name=name, in_specs=[anyspace] * n_arr, out_specs=[anyspace] * n_arr,
                          out_shape=shapes, scratch_shapes=sems)(*parts)


def _sum_parts(parts, name):
    def body(p_ref, g_ref):
        g = p_ref[0]
        for j in range(1, N_DEV):
            g = g + p_ref[j]
        g_ref[...] = g

    return pl.pallas_call(body, name=name, out_shape=jax.ShapeDtypeStruct(parts.shape[1:], F32),
                          compiler_params=pltpu.CompilerParams(vmem_limit_bytes=VMEM_LIMIT))(parts)


N_CHIPS = N_DEV // 2


def _sibling_swap(part, name):
    def body(g_ref, r_ref, send_sems, recv_sems):
        x, y, c, _ = _place()
        sends = [pltpu.make_async_remote_copy(
            src_ref=g_ref.at[2 * q + 1 - c], dst_ref=r_ref.at[q], send_sem=send_sems.at[q], recv_sem=recv_sems.at[q],
            device_id=(x, y, 1 - c), device_id_type=MESH) for q in range(N_CHIPS)]
        recvs = [pltpu.make_async_remote_copy(
            src_ref=g_ref.at[2 * q + c], dst_ref=r_ref.at[q], send_sem=send_sems.at[q], recv_sem=recv_sems.at[q],
            device_id=(x, y, c), device_id_type=MESH) for q in range(N_CHIPS)]
        for cp in sends:
            cp.start()
        for cp in recvs:
            cp.wait_recv()
        for cp in sends:
            cp.wait_send()

    anyspace = pl.BlockSpec(memory_space=pl.ANY)
    return pl.pallas_call(
        body, name=name, in_specs=[anyspace], out_specs=anyspace,
        out_shape=jax.ShapeDtypeStruct((N_CHIPS,) + part.shape[1:], part.dtype),
        scratch_shapes=[pltpu.SemaphoreType.DMA((N_CHIPS,)), pltpu.SemaphoreType.DMA((N_CHIPS,))])(part)


def _pair_sum(a, b, name):
    n, rows, cols = a.shape
    tr = max(t for t in range(16, 513, 16) if rows % t == 0)

    def body(a_ref, b_ref, o_ref):
        o_ref[...] = (a_ref[...].astype(F32) + b_ref[...].astype(F32)).astype(o_ref.dtype)

    blk = pl.BlockSpec((None, tr, cols), lambda q, i: (q, i, 0))
    return pl.pallas_call(body, name=name, grid=(n, rows // tr), in_specs=[blk, blk], out_specs=blk,
                          out_shape=jax.ShapeDtypeStruct(a.shape, a.dtype), compiler_params=_params(2))(a, b)


def _adamw(w, g, m, v):
    m = ADAM_B1 * m + (1.0 - ADAM_B1) * g
    v = ADAM_B2 * v + (1.0 - ADAM_B2) * (g * g)
    m_hat = m / (1.0 - ADAM_B1 ** ADAM_STEP)
    v_hat = v / (1.0 - ADAM_B2 ** ADAM_STEP)
    return -ADAM_LR * (m_hat / (jnp.sqrt(v_hat) + ADAM_EPS) + ADAM_WD * w), m, v


def _sum_parts_adamw(parts, w, m, v, name):
    n_parts, rows, cols = parts.shape
    tr = max(t for t in range(16, 257, 16) if rows % t == 0)

    def body(p_ref, w_ref, m_ref, v_ref, g_ref, d_ref, mo_ref, vo_ref):
        g = p_ref[0].astype(F32)
        for j in range(1, n_parts):
            g = g + p_ref[j].astype(F32)
        g_ref[...] = g
        d_ref[...], mo_ref[...], vo_ref[...] = _adamw(w_ref[...], g, m_ref[...], v_ref[...])

    row = _rows3(tr, cols)
    out = jax.ShapeDtypeStruct((1, rows, cols), F32)
    return pl.pallas_call(
        body, name=name, grid=(rows // tr,),
        in_specs=[pl.BlockSpec((n_parts, tr, cols), lambda i: (0, i, 0)), row, row, row],
        out_specs=[row, row, row, row], out_shape=[out, out, out, out], compiler_params=_params(1),
    )(parts, w, m, v)


def _sum_parts_adamw_whole(parts, w, m, v, name):
    def body(p_ref, w_ref, m_ref, v_ref, g_ref, d_ref, mo_ref, vo_ref):
        g = p_ref[0:1].astype(F32)
        for j in range(1, N_DEV):
            g = g + p_ref[j:j + 1].astype(F32)
        g_ref[...] = g
        d_ref[...], mo_ref[...], vo_ref[...] = _adamw(w_ref[...], g, m_ref[...], v_ref[...])

    out = jax.ShapeDtypeStruct(w.shape, F32)
    return pl.pallas_call(body, name=name, out_shape=[out] * 4,
                          compiler_params=pltpu.CompilerParams(vmem_limit_bytes=VMEM_LIMIT))(parts, w, m, v)


def _adamw_many(gs, ws, ms, vs, name):
    n = len(gs)

    def body(*refs):
        for k in range(n):
            g_ref, w_ref, m_ref, v_ref = (refs[q * n + k] for q in range(4))
            d_ref, mo_ref, vo_ref = (refs[(4 + q) * n + k] for q in range(3))
            d_ref[...], mo_ref[...], vo_ref[...] = _adamw(w_ref[...], g_ref[...], m_ref[...], v_ref[...])

    out = [jax.ShapeDtypeStruct(w.shape, F32) for w in ws]
    res = pl.pallas_call(body, name=name, out_shape=out * 3,
                         compiler_params=pltpu.CompilerParams(vmem_limit_bytes=VMEM_LIMIT))(*gs, *ws, *ms, *vs)
    return res[:n], res[n:2 * n], res[2 * n:]


def _ada_w_grad_adamw(c_all, dmod_rows, w, m, v, name):
    def body(c_ref, dm_ref, w_ref, m_ref, v_ref, g_ref, d_ref, mo_ref, vo_ref):
        cv = c_ref[...]
        g = lax.dot_general(cv * _sigmoid(cv), dm_ref[...], TN, precision=lax.Precision.HIGHEST,
                            preferred_element_type=F32)
        g_ref[...] = g
        d_ref[...], mo_ref[...], vo_ref[...] = _adamw(w_ref[...], g, m_ref[...], v_ref[...])

    n_layers, d, cols = w.shape
    layer = pl.BlockSpec((None, d, cols), lambda l: (l, 0, 0))
    out = jax.ShapeDtypeStruct(w.shape, F32)
    return pl.pallas_call(
        body, name=name, grid=(n_layers,),
        in_specs=[pl.BlockSpec(c_all.shape, lambda l: (0, 0)),
                  pl.BlockSpec((None,) + dmod_rows.shape[1:], lambda l: (l, 0, 0)), layer, layer, layer],
        out_specs=[layer] * 4, out_shape=[out] * 4, compiler_params=_params(1),
    )(c_all, dmod_rows, w, m, v)


REPLICATED = ("ln_g", "ln_b", "gmlp_norm_g", "gmlp_norm_b", "gmlp_ws", "gmlp_bs", "pool_b", "pool_scale",
              "mla_kv_norm_g", "mla_w_uk", "mla_w_uv")
CHUNK_ROWS, ADA_ROW, QNORM_ROW, LOSS_ROW, REP_ROWS = 73, 73, 74, 75, 80
UQ_ROWS, POOLW_ROWS = 96, 32


def _pad_rows(flat2d, rows):
    n, k = flat2d.shape
    return jnp.pad(flat2d, ((0, 0), (0, rows * LANES - k))).reshape(n, rows, LANES)


def _ada_cols_rows(vec):
    return _pad_rows(vec.reshape(2, N_DEV, -1).transpose(1, 0, 2).reshape(N_DEV, -1), 1)


def _unpack_replicated(rep, shapes):
    chunk = sum(s[1] for s in shapes) // N_DEV
    flat, off, out = rep[:, :CHUNK_ROWS].reshape(N_DEV, -1)[:, :chunk].reshape(-1), 0, {}
    for n, size, shape in shapes:
        out[n] = flat[off:off + size].reshape(shape)
        off += size
    cols = 3 * D_MODEL // N_DEV
    out["ada_b"] = rep[:, ADA_ROW, :2 * cols].reshape(N_DEV, 2, cols).transpose(1, 0, 2).reshape(2, -1)
    return out


def kernel(x, c, positions, ada_w, ada_b, ln_g, ln_b, e_w_in, gmlp_norm_g, gmlp_norm_b, gmlp_ws, gmlp_bs, pool_w, pool_b, pool_scale, e_w_out, o_w_in, mla_q_norm_g, mla_kv_norm_g, mla_w_uq, mla_w_uk, mla_w_uv, o_w_out, loss_target, m_ada_w, m_ada_b, m_ln_g, m_ln_b, m_e_w_in, m_gmlp_norm_g, m_gmlp_norm_b, m_gmlp_ws, m_gmlp_bs, m_pool_w, m_pool_b, m_pool_scale, m_e_w_out, m_o_w_in, m_mla_q_norm_g, m_mla_kv_norm_g, m_mla_w_uq, m_mla_w_uk, m_mla_w_uv, m_o_w_out, v_ada_w, v_ada_b, v_ln_g, v_ln_b, v_e_w_in, v_gmlp_norm_g, v_gmlp_norm_b, v_gmlp_ws, v_gmlp_bs, v_pool_w, v_pool_b, v_pool_scale, v_e_w_out, v_o_w_in, v_mla_q_norm_g, v_mla_kv_norm_g, v_mla_w_uq, v_mla_w_uk, v_mla_w_uv, v_o_w_out):
    w_in = dict(ada_w=ada_w, ada_b=ada_b, ln_g=ln_g, ln_b=ln_b, e_w_in=e_w_in, gmlp_norm_g=gmlp_norm_g,
                gmlp_norm_b=gmlp_norm_b, gmlp_ws=gmlp_ws, gmlp_bs=gmlp_bs, pool_w=pool_w, pool_b=pool_b,
                pool_scale=pool_scale, e_w_out=e_w_out, o_w_in=o_w_in, mla_q_norm_g=mla_q_norm_g,
                mla_kv_norm_g=mla_kv_norm_g, mla_w_uq=mla_w_uq, mla_w_uk=mla_w_uk, mla_w_uv=mla_w_uv, o_w_out=o_w_out)
    m_in = dict(ada_w=m_ada_w, ada_b=m_ada_b, ln_g=m_ln_g, ln_b=m_ln_b, e_w_in=m_e_w_in, gmlp_norm_g=m_gmlp_norm_g,
                gmlp_norm_b=m_gmlp_norm_b, gmlp_ws=m_gmlp_ws, gmlp_bs=m_gmlp_bs, pool_w=m_pool_w, pool_b=m_pool_b,
                pool_scale=m_pool_scale, e_w_out=m_e_w_out, o_w_in=m_o_w_in, mla_q_norm_g=m_mla_q_norm_g,
                mla_kv_norm_g=m_mla_kv_norm_g, mla_w_uq=m_mla_w_uq, mla_w_uk=m_mla_w_uk, mla_w_uv=m_mla_w_uv,
                o_w_out=m_o_w_out)
    v_in = dict(ada_w=v_ada_w, ada_b=v_ada_b, ln_g=v_ln_g, ln_b=v_ln_b, e_w_in=v_e_w_in, gmlp_norm_g=v_gmlp_norm_g,
                gmlp_norm_b=v_gmlp_norm_b, gmlp_ws=v_gmlp_ws, gmlp_bs=v_gmlp_bs, pool_w=v_pool_w, pool_b=v_pool_b,
                pool_scale=v_pool_scale, e_w_out=v_e_w_out, o_w_in=v_o_w_in, mla_q_norm_g=v_mla_q_norm_g,
                mla_kv_norm_g=v_mla_kv_norm_g, mla_w_uq=v_mla_w_uq, mla_w_uk=v_mla_w_uk, mla_w_uv=v_mla_w_uv,
                o_w_out=v_o_w_out)
    names = list(w_in)
    seq = x.shape[1]
    d = D_MODEL
    me = 4 * lax.axis_index("x") + 2 * lax.axis_index("y") + lax.axis_index("c")
    ada_cols = ada_w.shape[2]

    ada_b_cols = lax.dynamic_slice_in_dim(ada_b, me * ada_cols, ada_cols, axis=1)
    slab_row = lax.broadcasted_iota(jnp.int32, (8, d), 0)
    slab = jnp.where(slab_row == 0, c, jnp.where(slab_row == 1, jnp.pad(mla_q_norm_g, ((0, 0), (0, d - 32))), 0.0))
    c_all, mod, w_in_e3, pool_w3 = _adaln_exchange(
        slab, ada_w, jnp.broadcast_to(ada_b_cols[:, None, :], (2, 8, ada_cols)),
        [e_w_in[0].astype(BF16), pool_w.astype(BF16).reshape(POOLW_ROWS, LANES)], "adaln_exchange")
    h0 = _modulate(x, mod[0], "modulate0")
    proj0, o_in3 = _matmul_cols_nn(h0, w_in_e3, BF16, 512, "even_in", side=([o_w_in[0].astype(BF16)], "gather"))
    o_in_full = o_in3.transpose(1, 0, 2).reshape(d, ODD_IN)
    w_in_o = jnp.concatenate([o_in_full[:, :448], jnp.zeros((d, 64), BF16), o_in_full[:, 448:]], axis=1)
    pool_w_full = pool_w3.reshape(N_DEV, 4, 32, 256).transpose(1, 0, 2, 3).reshape(4, 256, 256)
    g_q = c_all.reshape(N_DEV, 8, d)[:, 1, :32].reshape(1, MLA_Q_RANK)

    ws, bs_col = gmlp_ws[0], gmlp_bs[0].reshape(GMLP_HEADS, GMLP_BLOCK, 1)
    wuk2, wuv2 = mla_w_uk[0].reshape(MLA_KV_RANK, -1), mla_w_uv[0].reshape(MLA_KV_RANK, -1)
    inv = 1.0 / (ROPE_THETA ** (jnp.arange(0, MLA_ROPE, 2, dtype=F32) / MLA_ROPE))
    ang = positions[0].astype(F32)[:, None] * inv
    cosp = jnp.tile(jnp.cos(ang), (1, 4))
    sinp = jnp.tile(jnp.concatenate([-jnp.sin(ang), jnp.sin(ang)], axis=1), (1, 2))

    mix0, w_out_e3, uq3 = _even_fwd(
        proj0, ws, bs_col, gmlp_norm_g, gmlp_norm_b, pool_w_full, pool_b, pool_scale, "even_mix",
        side=([e_w_out[0].astype(BF16), mla_w_uq.astype(BF16).reshape(UQ_ROWS, LANES)], "gather"))
    w_out_e = w_out_e3.reshape(-1, d)
    (y0,) = _matmul([(mix0, w_out_e)], "nn", F32, seq, d, 512, 1024, "even_out")
    uq_full = uq3.reshape(MLA_Q_RANK, MLA_HEADS, MLA_NOPE + MLA_ROPE)
    w_uq_n = uq_full[:, :, :MLA_NOPE].reshape(MLA_Q_RANK, -1)
    w_uq_r = uq_full[:, :, MLA_NOPE:].reshape(MLA_Q_RANK, -1)
    w_uq = jnp.concatenate([w_uq_n, w_uq_r], axis=1)
    x1, h1 = _resid_ln(x, y0, mod[0], ln_g[0:1], ln_b[0:1], mod[1], "resid_ln0")

    (proj1,) = _matmul([(h1, w_in_o)], "nn", BF16, seq, ODD_IN_PAD, 512, ODD_IN_PAD, "odd_in")
    qn, kp = _mla_prep(proj1, cosp, sinp, g_q, mla_kv_norm_g, "mla_prep")
    (q_up,) = _matmul([(qn, w_uq)], "nn", BF16, seq, 3072, 512, 3072, "q_up")
    qp = _q_heads(q_up, cosp, sinp, wuk2, "q_heads")
    o_lat, lse, w_out_o3 = _attn_fwd(qp, kp, "attn_fwd", side=([o_w_out[0].astype(BF16)], "gather"))
    w_out_o = w_out_o3.reshape(-1, d)
    gated = _o_gate(o_lat, proj1, wuv2, "o_gate")
    (y1,) = _matmul([(gated, w_out_o)], "nn", F32, seq, d, 512, 1024, "odd_out")

    dy1, dxres1, red2 = _final_ln_loss_bwd(x1, y1, mod[1], ln_g[1:2], ln_b[1:2], loss_target, "final_ln_loss")
    (dgated,) = _matmul([(dy1, w_out_o)], "nt", BF16, seq, MLA_WIDTH, 512, MLA_WIDTH, "odd_out_dx")
    (g_w_out_o,) = _matmul([(gated, dy1)], "tn", BF16, MLA_WIDTH, d, 256, d, "odd_out_dw")
    dproj1_z, do_lat, g_wuv = _o_gate_bwd(dgated, o_lat, proj1, wuv2, "o_gate_bwd")
    dqp, dkp, dvv, r_o_out = _attn_bwd(qp, kp, o_lat, do_lat, lse, "attn_bwd",
                                       side=([g_w_out_o.reshape(N_DEV, -1, d)], "devices"))
    dq_nope, dq_rope, g_wuk = _q_heads_bwd(dqp, q_up, cosp, sinp, wuk2, "q_heads_bwd")
    (dqn,) = _matmul([(dq_nope, w_uq_n), (dq_rope, w_uq_r)], "nt", F32, seq, MLA_Q_RANK, 512, 256, "q_up_dx")
    (g_wuq_n,) = _matmul([(qn, dq_nope)], "tn", F32, MLA_Q_RANK, MLA_WIDTH, 256, MLA_WIDTH, "q_up_dw_nope")
    (g_wuq_r,) = _matmul([(qn, dq_rope)], "tn", F32, MLA_Q_RANK, 1024, 256, 1024, "q_up_dw_rope")
    dproj1, red_mla = _mla_prep_bwd(proj1, dqn, dkp, dvv, cosp, sinp, g_q, mla_kv_norm_g, dproj1_z, "mla_prep_bwd")
    (dh1,) = _matmul([(dproj1, w_in_o)], "nt", F32, seq, d, 512, d, "odd_in_dx")
    part_uq = jnp.concatenate([g_wuq_n.reshape(MLA_Q_RANK, MLA_HEADS, MLA_NOPE),
                               g_wuq_r.reshape(MLA_Q_RANK, MLA_HEADS, MLA_ROPE)], axis=2).astype(BF16).reshape(
                                   (N_DEV,) + mla_w_uq.shape[1:])
    (g_w_in_o,) = _matmul([(h1, dproj1)], "tn", BF16, d, ODD_IN_PAD, 256, ODD_IN_PAD // 2, "odd_in_dw", n_outer=True)
    part_o_in = jnp.concatenate([g_w_in_o[:, :448], g_w_in_o[:, 512:]], axis=1).reshape(d, N_DEV, -1).transpose(1, 0, 2)
    dy0, dxres0, red1 = _mid_bwd(dh1, dxres1, x, y0, mod[0], mod[1], ln_g[0:1], ln_b[0:1], "mid_bwd")
    (dmix,) = _matmul([(dy0, w_out_e)], "nt", BF16, seq, 2048, 512, 2048, "even_out_dx")
    (g_w_out_e,) = _matmul([(mix0, dy0)], "tn", BF16, 2048, d, 256, d, "even_out_dw")
    dproj0, g_ws, g_bs, g_ng, g_nb, g_pw, g_pb, g_ps, r_o_in, r_uq = _even_bwd(
        proj0, dmix, ws, bs_col, gmlp_norm_g, gmlp_norm_b, pool_w_full, pool_b, pool_scale, "even_mix_bwd",
        side=([part_o_in, part_uq], "devices"))
    part_pw = g_pw.reshape(4, N_DEV, 32, 256).transpose(1, 0, 2, 3)
    part_e_in, r_e_out, r_pw = _matmul_cols_tn(h0, dproj0, w_in_e3.shape[2], BF16, 512, "even_in_dw",
                                               side=([g_w_out_e.reshape(N_DEV, -1, d), part_pw], "devices"))
    mine = lax.dynamic_index_in_dim(part_e_in.reshape((N_CHIPS, 2) + part_e_in.shape[1:]), lax.axis_index("c"), 1, False)
    chip_e_in = _pair_sum(mine, _sibling_swap(part_e_in, "e_in_sibling_swap"), "e_in_pair_sum")
    grad_x, red0, r_e_in = _first_bwd(dproj0, w_in_e3, dxres0, x, mod[0], "even_in_dx", side=([chip_e_in], "chips"))

    t_mask = lax.broadcasted_iota(jnp.int32, (GMLP_BLOCK, GMLP_BLOCK), 0) // CHUNK
    s_mask = lax.broadcasted_iota(jnp.int32, (GMLP_BLOCK, GMLP_BLOCK), 1) // CHUNK
    part = {
        "ln_g": jnp.stack([red1[2], red2[0]]), "ln_b": jnp.stack([red1[3], red2[1]]),
        "gmlp_norm_g": g_ng, "gmlp_norm_b": g_nb,
        "gmlp_ws": jnp.where(s_mask <= t_mask, g_ws, 0.0), "gmlp_bs": g_bs,
        "pool_b": g_pb, "pool_scale": g_ps, "mla_kv_norm_g": red_mla[1, :MLA_KV_RANK],
        "mla_w_uk": g_wuk, "mla_w_uv": g_wuv,
    }
    dmod = jnp.stack([jnp.concatenate([red0[1], red0[0], red1[4]]),
                      jnp.concatenate([red1[1], red1[0], red2[2]])])

    loss_row = jnp.pad(jnp.broadcast_to((0.5 / d * jnp.sum(red2[3])).reshape(1, 1, 1), (N_DEV, 1, 1)),
                       ((0, 0), (0, 0), (0, LANES - 1)))
    part_small = jnp.concatenate([
        _pad_rows(jnp.concatenate([part[n].reshape(-1) for n in REPLICATED]).reshape(N_DEV, -1), CHUNK_ROWS),
        jnp.pad(jnp.concatenate([_ada_cols_rows(dmod), _pad_rows(red_mla[0].reshape(N_DEV, -1), 1), loss_row], axis=1),
                ((0, 0), (0, REP_ROWS - LOSS_ROW - 1), (0, 0)))], axis=1)
    (r_small,) = _scatter_parts([part_small], "grad_scatter")
    small_sum = _sum_parts(r_small, "small_sum")
    loss = small_sum[LOSS_ROW, 0]
    (rep_sum,) = _all_gather([small_sum], "replicated_gather")

    res = {"e_w_in": _sum_parts_adamw(r_e_in, e_w_in, m_e_w_in, v_e_w_in, "adamw_e_w_in"),
           "o_w_in": _sum_parts_adamw(r_o_in, o_w_in, m_o_w_in, v_o_w_in, "adamw_o_w_in"),
           "e_w_out": _sum_parts_adamw(r_e_out, e_w_out, m_e_w_out, v_e_w_out, "adamw_e_w_out"),
           "o_w_out": _sum_parts_adamw(r_o_out, o_w_out, m_o_w_out, v_o_w_out, "adamw_o_w_out"),
           "mla_w_uq": _sum_parts_adamw_whole(r_uq, mla_w_uq, m_mla_w_uq, v_mla_w_uq, "adamw_w_uq"),
           "pool_w": _sum_parts_adamw_whole(r_pw, pool_w, m_pool_w, v_pool_w, "adamw_pool_w")}
    grads = _unpack_replicated(rep_sum, [(n, w_in[n].size, w_in[n].shape) for n in REPLICATED])
    grads["mla_q_norm_g"] = small_sum[QNORM_ROW:QNORM_ROW + 1, :32]
    small_names = list(grads)
    deltas, new_ms, new_vs = _adamw_many([grads[n] for n in small_names], [w_in[n] for n in small_names],
                                         [m_in[n] for n in small_names], [v_in[n] for n in small_names], "small_adamw")
    for k, n in enumerate(small_names):
        res[n] = [grads[n], deltas[k], new_ms[k], new_vs[k]]
    dmod_all = r_small[:, ADA_ROW, :2 * ada_cols].reshape(N_DEV, 2, ada_cols).transpose(1, 0, 2)
    dmod_rows = jnp.pad(dmod_all[:, :, None, :], ((0, 0), (0, 0), (0, 7), (0, 0))).reshape(2, 8 * N_DEV, ada_cols)
    res["ada_w"] = _ada_w_grad_adamw(c_all, dmod_rows, ada_w, m_ada_w, v_ada_w, "ada_w_adamw")

    return (loss, grad_x, *[res[n][0] for n in names], *[res[n][1] for n in names],
            *[res[n][2] for n in names], *[res[n][3] for n in names])
```

```python
import functools

import jax
import jax.numpy as jnp
from jax import lax
from jax.experimental import pallas as pl
from jax.experimental.pallas import tpu as pltpu

F32 = jnp.float32
BF16 = jnp.bfloat16

D_MODEL = 1024
CHUNK = 64
LN_EPS = 1e-5
GMLP_HEADS = 4
GMLP_HEAD_DIM = 256
GMLP_BLOCK = 128
POOL_WINDOWS = (2, 4, 8, 16)
POOL_GROUP_DIM = 256
POOL_HALO = 16
MLA_HEADS = 16
MLA_NOPE = 128
MLA_ROPE = 64
MLA_Q_RANK = 256
MLA_KV_RANK = 128
MLA_WIDTH = 2048
ODD_IN = 2496
ODD_IN_PAD = 2560
ROPE_THETA = 10000.0
ATTN_SCALE = (MLA_NOPE + MLA_ROPE) ** -0.5
ATTN_SCALE_LOG2 = ATTN_SCALE * 1.4426950408889634
DEEPNORM_ALPHA = 4.0 ** 0.25
ADAM_LR, ADAM_B1, ADAM_B2, ADAM_EPS, ADAM_WD, ADAM_STEP = 0.001, 0.9, 0.999, 1e-8, 0.01, 10
N_DEV = 8
LANES = 1024
VMEM_LIMIT = 56 * 1024 * 1024
MESH = pl.DeviceIdType.MESH

NT = (((1,), (1,)), ((), ()))
NN = (((1,), (0,)), ((), ()))
TN = (((0,), (0,)), ((), ()))


def _params(n_axes):
    return pltpu.CompilerParams(dimension_semantics=("arbitrary",) * n_axes, vmem_limit_bytes=VMEM_LIMIT)


def _dot(a, b, dn):
    return lax.dot_general(a.astype(BF16), b.astype(BF16), dn, preferred_element_type=F32)


def _sigmoid(z):
    return 1.0 / (1.0 + jnp.exp(-z))


def _colsum(t):
    return jnp.sum(t, axis=0, keepdims=True)


EXCHANGE_RELATIONS = {"devices": tuple(range(1, N_DEV)), "chips": (2, 4, 6)}


def _exchange_copies(g_refs, r_refs, send_sems, recv_sems, local_sems, kind):
    x, y, c, me = _place()
    n_arr = len(g_refs)

    def slot(lin):
        return lin // 2 if kind == "chips" else lin

    own = [pltpu.make_async_copy(g_refs[t].at[slot(me)], r_refs[t].at[slot(me)], local_sems.at[t]) for t in range(n_arr)]
    sends, recvs = [], []
    for n, r in enumerate(EXCHANGE_RELATIONS[kind]):
        peer, lin = _flip(x, y, c, r)
        for t in range(n_arr):
            k = n_arr * n + t
            sends.append(pltpu.make_async_remote_copy(
                src_ref=g_refs[t].at[slot(lin)], dst_ref=r_refs[t].at[slot(me)], send_sem=send_sems.at[k],
                recv_sem=recv_sems.at[k], device_id=peer, device_id_type=MESH))
            recvs.append(pltpu.make_async_remote_copy(
                src_ref=g_refs[t].at[slot(lin)], dst_ref=r_refs[t].at[slot(lin)], send_sem=send_sems.at[k],
                recv_sem=recv_sems.at[k], device_id=(x, y, c), device_id_type=MESH))
    return own, sends, recvs


def _exchange_start(copies):
    own, sends, _ = copies
    for cp in own + sends:
        cp.start()


def _exchange_wait(copies):
    own, sends, recvs = copies
    for cp in recvs:
        cp.wait_recv()
    for cp in sends:
        cp.wait_send()
    for cp in own:
        cp.wait()


def _exchange_extras(parts, kind):
    shapes = [jax.ShapeDtypeStruct(p.shape, p.dtype) for p in parts]
    n = len(parts) * len(EXCHANGE_RELATIONS[kind])
    return shapes, [pltpu.SemaphoreType.DMA((n,)), pltpu.SemaphoreType.DMA((n,)), pltpu.SemaphoreType.DMA((len(parts),))]


def _grid_call(body, name, grid, in_specs, out_specs, out_shape, args, scratch=(), side=None):
    if side is None:
        return pl.pallas_call(body, name=name, grid=grid, in_specs=in_specs, out_specs=out_specs,
                              out_shape=out_shape, scratch_shapes=list(scratch),
                              compiler_params=_params(len(grid)))(*args)
    parts, kind = side
    gather = kind == "gather"
    n_in, n_out, n_sc, n_arr = len(args), len(out_shape), len(scratch), len(parts)
    side_shapes, side_sems = _gather_extras(parts) if gather else _exchange_extras(parts, kind)
    mid = tuple(g // 2 for g in grid)

    def wrapped(*refs):
        ins, g_refs = refs[:n_in], refs[n_in:n_in + n_arr]
        outs = refs[n_in + n_arr:n_in + n_arr + n_out]
        r_refs = refs[n_in + n_arr + n_out:n_in + 2 * n_arr + n_out]
        sc = refs[n_in + 2 * n_arr + n_out:n_in + 2 * n_arr + n_out + n_sc]
        ids = [pl.program_id(a) for a in range(len(grid))]

        def at(step):
            return functools.reduce(jnp.logical_and, [i == s for i, s in zip(ids, step)])

        first, last = at((0,) * len(grid)), at(tuple(g - 1 for g in grid))
        if gather:
            exchange = _Gather(g_refs, r_refs, *refs[-3:])
            pl.when(first)(exchange.start)
            if mid != (0,) * len(grid):
                pl.when(at(mid))(exchange.forward)
            body(*ins, *outs, *sc)

            @pl.when(last)
            def _():
                if mid == (0,) * len(grid):
                    exchange.forward()
                exchange.end()
        else:
            copies = _exchange_copies(g_refs, r_refs, *refs[-3:], kind)
            pl.when(first)(lambda: _exchange_start(copies))
            body(*ins, *outs, *sc)
            pl.when(last)(lambda: _exchange_wait(copies))

    anyspace = pl.BlockSpec(memory_space=pl.ANY)
    return pl.pallas_call(
        wrapped, name=name, grid=grid, in_specs=list(in_specs) + [anyspace] * n_arr,
        out_specs=list(out_specs) + [anyspace] * n_arr, out_shape=list(out_shape) + side_shapes,
        scratch_shapes=list(scratch) + side_sems, compiler_params=_params(len(grid)),
    )(*args, *parts)


def _matmul(pairs, mode, out_dtype, m, n, tm, tn, name, side=None, n_outer=False):
    dn = {"nn": NN, "nt": NT, "tn": TN}[mode]
    tm, tn = min(tm, m), min(tn, n)
    n_pairs = len(pairs)
    grid = (n // tn, m // tm) if n_outer else (m // tm, n // tn)

    def ij(f):
        return (lambda j, i: f(i, j)) if n_outer else f

    def body(*refs):
        o_ref = refs[-1]
        acc = None
        for p in range(n_pairs):
            t = _dot(refs[2 * p][...], refs[2 * p + 1][...], dn)
            acc = t if acc is None else acc + t
        o_ref[...] = acc.astype(o_ref.dtype)

    in_specs, args = [], []
    for a, b in pairs:
        if mode == "nn":
            k = a.shape[1]
            in_specs += [pl.BlockSpec((tm, k), ij(lambda i, j: (i, 0))), pl.BlockSpec((k, tn), ij(lambda i, j: (0, j)))]
        elif mode == "nt":
            k = a.shape[1]
            in_specs += [pl.BlockSpec((tm, k), ij(lambda i, j: (i, 0))), pl.BlockSpec((tn, k), ij(lambda i, j: (j, 0)))]
        else:
            k = a.shape[0]
            in_specs += [pl.BlockSpec((k, tm), ij(lambda i, j: (0, i))), pl.BlockSpec((k, tn), ij(lambda i, j: (0, j)))]
        args += [a, b]
    return _grid_call(body, name, grid, in_specs, [pl.BlockSpec((tm, tn), ij(lambda i, j: (i, j)))],
                      [jax.ShapeDtypeStruct((m, n), out_dtype)], args, side=side)


def _matmul_cols_nn(a, w3, out_dtype, tm, name, side=None):
    m, k = a.shape
    _, _, n = w3.shape
    tm = min(tm, m)

    def body(a_ref, w_ref, o_ref):
        av = a_ref[...]
        for j in range(N_DEV):
            o_ref[:, n * j:n * (j + 1)] = _dot(av, w_ref[j], NN).astype(o_ref.dtype)

    return _grid_call(
        body, name, (m // tm,),
        [pl.BlockSpec((tm, k), lambda i: (i, 0)), pl.BlockSpec((N_DEV, k, n), lambda i: (0, 0, 0))],
        [pl.BlockSpec((tm, N_DEV * n), lambda i: (i, 0))], [jax.ShapeDtypeStruct((m, N_DEV * n), out_dtype)], [a, w3],
        side=side)


def _matmul_cols_tn(a, b, n, out_dtype, tk, name, side=None):
    m, k = a.shape
    tk = min(tk, k)

    def body(a_ref, b_ref, o_ref):
        o_ref[...] = _dot(a_ref[...], b_ref[...], TN).astype(o_ref.dtype)

    return _grid_call(
        body, name, (N_DEV, k // tk),
        [pl.BlockSpec((m, tk), lambda j, i: (0, i)), pl.BlockSpec((m, n), lambda j, i: (0, j))],
        [pl.BlockSpec((None, tk, n), lambda j, i: (j, i, 0))], [jax.ShapeDtypeStruct((N_DEV, k, n), out_dtype)], [a, b],
        side=side)


def _rows3(tm, d):
    return pl.BlockSpec((None, tm, d), lambda i: (0, i, 0))


def _modulate(x, mod, name):
    _, s, d = x.shape
    tm = min(s, 512)

    def body(x_ref, m_ref, h_ref):
        shift, scale = m_ref[0:1, 0:d], m_ref[0:1, d:2 * d]
        h_ref[...] = (x_ref[...] * (1.0 + scale) + shift).astype(BF16)

    return pl.pallas_call(
        body, name=name, grid=(s // tm,),
        in_specs=[_rows3(tm, d), pl.BlockSpec((8, 3 * d), lambda i: (0, 0))],
        out_specs=pl.BlockSpec((tm, d), lambda i: (i, 0)),
        out_shape=jax.ShapeDtypeStruct((s, d), BF16), compiler_params=_params(1),
    )(x, mod)


def _ln_stats(r):
    mu = jnp.mean(r, axis=-1, keepdims=True)
    rc = r - mu
    var = jnp.mean(rc * rc, axis=-1, keepdims=True)
    rstd = lax.rsqrt(var + LN_EPS)
    return rc * rstd, rstd


def _ln_bwd(dxhat, xhat, rstd):
    return rstd * (dxhat - jnp.mean(dxhat, axis=-1, keepdims=True)
                   - xhat * jnp.mean(dxhat * xhat, axis=-1, keepdims=True))


def _resid_ln(x, y, mod, g, b, mod_next, name):
    _, s, d = x.shape
    tm = min(s, 512)

    def body(x_ref, y_ref, m_ref, g_ref, b_ref, mn_ref, o_ref, h_ref):
        gate = m_ref[0:1, 2 * d:3 * d]
        xhat, _ = _ln_stats(DEEPNORM_ALPHA * x_ref[...] + (1.0 + gate) * y_ref[...])
        out = xhat * g_ref[...] + b_ref[...]
        o_ref[...] = out
        h_ref[...] = (out * (1.0 + mn_ref[0:1, d:2 * d]) + mn_ref[0:1, 0:d]).astype(BF16)

    row = pl.BlockSpec((tm, d), lambda i: (i, 0))
    vec = pl.BlockSpec((1, d), lambda i: (0, 0))
    modspec = pl.BlockSpec((8, 3 * d), lambda i: (0, 0))
    return pl.pallas_call(
        body, name=name, grid=(s // tm,),
        in_specs=[_rows3(tm, d), row, modspec, vec, vec, modspec],
        out_specs=[_rows3(tm, d), row],
        out_shape=[jax.ShapeDtypeStruct((1, s, d), F32), jax.ShapeDtypeStruct((s, d), BF16)],
        compiler_params=_params(1),
    )(x, y, mod, g, b, mod_next)


def _final_ln_loss_bwd(x, y, mod, g, b, target, name):
    _, s, d = x.shape
    tm = min(s, 512)

    def body(x_ref, y_ref, m_ref, g_ref, b_ref, t_ref, dy_ref, dx_ref, red_ref):
        @pl.when(pl.program_id(0) == 0)
        def _():
            red_ref[...] = jnp.zeros_like(red_ref)

        gate = m_ref[0:1, 2 * d:3 * d]
        yv = y_ref[...]
        xhat, rstd = _ln_stats(DEEPNORM_ALPHA * x_ref[...] + (1.0 + gate) * yv)
        err = xhat * g_ref[...] + b_ref[...] - t_ref[...]
        dout = err * (1.0 / d)
        dr = _ln_bwd(dout * g_ref[...], xhat, rstd)
        dy_ref[...] = ((1.0 + gate) * dr).astype(BF16)
        dx_ref[...] = DEEPNORM_ALPHA * dr
        red_ref[0:1, :] += _colsum(dout * xhat)
        red_ref[1:2, :] += _colsum(dout)
        red_ref[2:3, :] += _colsum(dr * yv)
        red_ref[3:4, :] += _colsum(err * err)

    row = pl.BlockSpec((tm, d), lambda i: (i, 0))
    vec = pl.BlockSpec((1, d), lambda i: (0, 0))
    return pl.pallas_call(
        body, name=name, grid=(s // tm,),
        in_specs=[_rows3(tm, d), row, pl.BlockSpec((8, 3 * d), lambda i: (0, 0)), vec, vec, _rows3(tm, d)],
        out_specs=[row, row, pl.BlockSpec((8, d), lambda i: (0, 0))],
        out_shape=[jax.ShapeDtypeStruct((s, d), BF16), jax.ShapeDtypeStruct((s, d), F32),
                   jax.ShapeDtypeStruct((8, d), F32)],
        compiler_params=_params(1),
    )(x, y, mod, g, b, target)


def _mid_bwd(dh, dxres, x, y, mod_lo, mod_hi, g, b, name):
    _, s, d = x.shape
    tm = min(s, 512)

    def body(dh_ref, dxr_ref, x_ref, y_ref, ml_ref, mh_ref, g_ref, b_ref, dy_ref, dx_ref, red_ref):
        @pl.when(pl.program_id(0) == 0)
        def _():
            red_ref[...] = jnp.zeros_like(red_ref)

        gate = ml_ref[0:1, 2 * d:3 * d]
        scale_hi = mh_ref[0:1, d:2 * d]
        yv, dhv = y_ref[...], dh_ref[...]
        xhat, rstd = _ln_stats(DEEPNORM_ALPHA * x_ref[...] + (1.0 + gate) * yv)
        x_mid = xhat * g_ref[...] + b_ref[...]
        dx_mid = dxr_ref[...] + dhv * (1.0 + scale_hi)
        dr = _ln_bwd(dx_mid * g_ref[...], xhat, rstd)
        dy_ref[...] = ((1.0 + gate) * dr).astype(BF16)
        dx_ref[...] = DEEPNORM_ALPHA * dr
        red_ref[0:1, :] += _colsum(dhv * x_mid)
        red_ref[1:2, :] += _colsum(dhv)
        red_ref[2:3, :] += _colsum(dx_mid * xhat)
        red_ref[3:4, :] += _colsum(dx_mid)
        red_ref[4:5, :] += _colsum(dr * yv)

    row = pl.BlockSpec((tm, d), lambda i: (i, 0))
    vec = pl.BlockSpec((1, d), lambda i: (0, 0))
    modspec = pl.BlockSpec((8, 3 * d), lambda i: (0, 0))
    return pl.pallas_call(
        body, name=name, grid=(s // tm,),
        in_specs=[row, row, _rows3(tm, d), row, modspec, modspec, vec, vec],
        out_specs=[row, row, pl.BlockSpec((8, d), lambda i: (0, 0))],
        out_shape=[jax.ShapeDtypeStruct((s, d), BF16), jax.ShapeDtypeStruct((s, d), F32),
                   jax.ShapeDtypeStruct((8, d), F32)],
        compiler_params=_params(1),
    )(dh, dxres, x, y, mod_lo, mod_hi, g, b)


def _first_bwd(dproj, w3, dxres, x, mod, name, side=None):
    _, s, d = x.shape
    n = w3.shape[2]
    tm = min(s, 256)

    def body(a_ref, w_ref, dxr_ref, x_ref, m_ref, gx_ref, red_ref):
        @pl.when(pl.program_id(0) == 0)
        def _():
            red_ref[...] = jnp.zeros_like(red_ref)

        dhv = _dot(a_ref[:, 0:n], w_ref[0], NT)
        for j in range(1, N_DEV):
            dhv = dhv + _dot(a_ref[:, n * j:n * (j + 1)], w_ref[j], NT)
        gx_ref[...] = dxr_ref[...] + dhv * (1.0 + m_ref[0:1, d:2 * d])
        red_ref[0:1, :] += _colsum(dhv * x_ref[...])
        red_ref[1:2, :] += _colsum(dhv)

    return _grid_call(
        body, name, (s // tm,),
        [pl.BlockSpec((tm, N_DEV * n), lambda i: (i, 0)), pl.BlockSpec((N_DEV, d, n), lambda i: (0, 0, 0)),
         pl.BlockSpec((tm, d), lambda i: (i, 0)), _rows3(tm, d), pl.BlockSpec((8, 3 * d), lambda i: (0, 0))],
        [_rows3(tm, d), pl.BlockSpec((8, d), lambda i: (0, 0))],
        [jax.ShapeDtypeStruct((1, s, d), F32), jax.ShapeDtypeStruct((8, d), F32)],
        [dproj, w3, dxres, x, mod], side=side)


EVEN_TM = 512


def _gmlp_mask():
    t = lax.broadcasted_iota(jnp.int32, (GMLP_BLOCK, GMLP_BLOCK), 0) // CHUNK
    s = lax.broadcasted_iota(jnp.int32, (GMLP_BLOCK, GMLP_BLOCK), 1) // CHUNK
    return s <= t


def _window_sum(ext, win, back):
    n = ext.shape[0]
    k = 1
    while k < win:
        ext = ext + pltpu.roll(ext, k if back else n - k, 0)
        k *= 2
    return ext


def _inv_count(row0, rows, win):
    t = row0 + lax.broadcasted_iota(jnp.int32, (rows, 1), 0)
    return t, 1.0 / jnp.minimum(t + 1, win).astype(F32)


def _pooled(xb, halo, row0, win):
    tm = xb.shape[0]
    sums = _window_sum(jnp.concatenate([halo, xb], axis=0), win, True)[POOL_HALO:]
    _, inv = _inv_count(row0, tm, win)
    return sums * inv - xb


def _even_fwd(proj, ws, bs_col, ng, nb, pw, pb, ps, name, side=None):
    s = proj.shape[0]
    tm = min(s, EVEN_TM)
    hd, gd = GMLP_HEAD_DIM, POOL_GROUP_DIM

    def body(p_ref, halo_ref, ws_ref, bs_ref, ng_ref, nb_ref, pw_ref, pb_ref, ps_ref, m_ref):
        i = pl.program_id(0)
        mask = _gmlp_mask()
        for h in range(GMLP_HEADS):
            wm = jnp.where(mask, ws_ref[h], 0.0).astype(BF16)
            for blk in range(tm // GMLP_BLOCK):
                rows = slice(blk * GMLP_BLOCK, (blk + 1) * GMLP_BLOCK)
                cu, cv, cz = h * hd, 1024 + h * hd, 2048 + h * hd
                vhat, _ = _ln_stats(p_ref[rows, cv:cv + hd].astype(F32))
                vn = vhat * ng_ref[...] + nb_ref[...]
                sv = _dot(wm, vn, NN) + bs_ref[h]
                za = p_ref[rows, cz:cz + hd].astype(F32)
                m_ref[rows, cu:cu + hd] = (p_ref[rows, cu:cu + hd].astype(F32) * sv * (za * _sigmoid(za))).astype(BF16)
        for g, win in enumerate(POOL_WINDOWS):
            cx, cz = 3072 + g * gd, 4096 + g * gd
            halo = jnp.where(i > 0, halo_ref[:, g * gd:(g + 1) * gd].astype(F32), 0.0)
            pooled = _pooled(p_ref[:, cx:cx + gd].astype(F32), halo, i * tm, win)
            yb = _dot(pooled, pw_ref[g], NN) + pb_ref[:, g * gd:(g + 1) * gd]
            zb = p_ref[:, cz:cz + gd].astype(F32)
            m_ref[:, 1024 + g * gd:1024 + (g + 1) * gd] = (
                yb * ps_ref[:, g * gd:(g + 1) * gd] * (zb * _sigmoid(zb))).astype(BF16)

    hb = tm // POOL_HALO
    return _grid_call(
        body, name, (s // tm,),
        [
            pl.BlockSpec((tm, 5120), lambda i: (i, 0)),
            pl.BlockSpec((POOL_HALO, 1024), lambda i: (jnp.maximum(i * hb - 1, 0), 3)),
            pl.BlockSpec((GMLP_HEADS, GMLP_BLOCK, GMLP_BLOCK), lambda i: (0, 0, 0)),
            pl.BlockSpec((GMLP_HEADS, GMLP_BLOCK, 1), lambda i: (0, 0, 0)),
            pl.BlockSpec((1, hd), lambda i: (0, 0)), pl.BlockSpec((1, hd), lambda i: (0, 0)),
            pl.BlockSpec((4, gd, gd), lambda i: (0, 0, 0)),
            pl.BlockSpec((1, 1024), lambda i: (0, 0)), pl.BlockSpec((1, 1024), lambda i: (0, 0)),
        ],
        [pl.BlockSpec((tm, 2048), lambda i: (i, 0))], [jax.ShapeDtypeStruct((s, 2048), BF16)],
        [proj, proj, ws, bs_col, ng, nb, pw, pb, ps], side=side)


def _even_bwd(proj, dm, ws, bs_col, ng, nb, pw, pb, ps, name, side=None):
    s = proj.shape[0]
    tm = min(s, EVEN_TM)
    hd, gd = GMLP_HEAD_DIM, POOL_GROUP_DIM
    n_tiles = s // tm

    def body(p_ref, halo_ref, zbn_ref, dm_ref, dbn_ref, ws_ref, bs_ref, ng_ref, nb_ref, pw_ref, pb_ref, ps_ref,
             dp_ref, dws_ref, dbs_ref, dng_ref, dnb_ref, dpw_ref, dpb_ref, dps_ref):
        i = pl.program_id(0)

        @pl.when(i == 0)
        def _():
            for r in (dws_ref, dbs_ref, dng_ref, dnb_ref, dpw_ref, dpb_ref, dps_ref):
                r[...] = jnp.zeros_like(r)

        mask = _gmlp_mask()
        for h in range(GMLP_HEADS):
            wm = jnp.where(mask, ws_ref[h], 0.0).astype(BF16)
            for blk in range(tm // GMLP_BLOCK):
                rows = slice(blk * GMLP_BLOCK, (blk + 1) * GMLP_BLOCK)
                cu, cv, cz = h * hd, 1024 + h * hd, 2048 + h * hd
                vhat, rstd = _ln_stats(p_ref[rows, cv:cv + hd].astype(F32))
                vn = (vhat * ng_ref[...] + nb_ref[...]).astype(BF16)
                sv = _dot(wm, vn, NN) + bs_ref[h]
                u, za = p_ref[rows, cu:cu + hd].astype(F32), p_ref[rows, cz:cz + hd].astype(F32)
                da = dm_ref[rows, cu:cu + hd].astype(F32)
                sig = _sigmoid(za)
                sa = za * sig
                dau = da * u
                dsv = dau * sa
                dp_ref[rows, cu:cu + hd] = (da * sv * sa).astype(BF16)
                dp_ref[rows, cz:cz + hd] = (dau * sv * (sig * (1.0 + za * (1.0 - sig)))).astype(BF16)
                dsv_b = dsv.astype(BF16)
                dbs_ref[h] += jnp.sum(dsv, axis=1, keepdims=True)
                dws_ref[h] += _dot(dsv_b, vn, NT)
                dvn = _dot(wm, dsv_b, TN)
                dng_ref[...] += _colsum(dvn * vhat)
                dnb_ref[...] += _colsum(dvn)
                dp_ref[rows, cv:cv + hd] = _ln_bwd(dvn * ng_ref[...], vhat, rstd).astype(BF16)

        row0 = i * tm
        for g, win in enumerate(POOL_WINDOWS):
            cx, cz, cd = 3072 + g * gd, 4096 + g * gd, 1024 + g * gd
            gs = slice(g * gd, (g + 1) * gd)
            halo = jnp.where(i > 0, halo_ref[:, gs].astype(F32), 0.0)
            xb = p_ref[:, cx:cx + gd].astype(F32)
            pooled = _pooled(xb, halo, row0, win).astype(BF16)
            scale_g = ps_ref[:, gs]
            yb = _dot(pooled, pw_ref[g], NN) + pb_ref[:, gs]
            zb, db = p_ref[:, cz:cz + gd].astype(F32), dm_ref[:, cd:cd + gd].astype(F32)
            sig = _sigmoid(zb)
            dyp = db * (zb * sig)
            dp_ref[:, cz:cz + gd] = (db * yb * scale_g * (sig * (1.0 + zb * (1.0 - sig)))).astype(BF16)
            dps_ref[:, gs] += _colsum(dyp * yb)
            dpb_ref[:, gs] += _colsum(dyp * scale_g)
            zb_ext = jnp.concatenate([zb, zbn_ref[:, gs].astype(F32)], axis=0)
            db_ext = jnp.concatenate([db, dbn_ref[:, gs].astype(F32)], axis=0)
            dy_ext = (db_ext * (zb_ext * _sigmoid(zb_ext)) * scale_g).astype(BF16)
            dpw_ref[g] += _dot(pooled, dy_ext[:tm], TN)
            dpooled = _dot(dy_ext, pw_ref[g], NT)
            t, inv = _inv_count(row0, tm + POOL_HALO, win)
            w_ext = jnp.where(t < s, dpooled * inv, 0.0)
            dp_ref[:, cx:cx + gd] = (_window_sum(w_ext, win, False)[:tm] - dpooled[:tm]).astype(BF16)

    hb = tm // POOL_HALO
    last = s // POOL_HALO - 1
    small = lambda shape: pl.BlockSpec(shape, lambda i: (0,) * len(shape))
    return _grid_call(
        body, name, (n_tiles,),
        [
            pl.BlockSpec((tm, 5120), lambda i: (i, 0)),
            pl.BlockSpec((POOL_HALO, 1024), lambda i: (jnp.maximum(i * hb - 1, 0), 3)),
            pl.BlockSpec((POOL_HALO, 1024), lambda i: (jnp.minimum((i + 1) * hb, last), 4)),
            pl.BlockSpec((tm, 2048), lambda i: (i, 0)),
            pl.BlockSpec((POOL_HALO, 1024), lambda i: (jnp.minimum((i + 1) * hb, last), 1)),
            small((GMLP_HEADS, GMLP_BLOCK, GMLP_BLOCK)), small((GMLP_HEADS, GMLP_BLOCK, 1)),
            small((1, hd)), small((1, hd)), small((4, gd, gd)), small((1, 1024)), small((1, 1024)),
        ],
        [
            pl.BlockSpec((tm, 5120), lambda i: (i, 0)),
            small((GMLP_HEADS, GMLP_BLOCK, GMLP_BLOCK)), small((GMLP_HEADS, GMLP_BLOCK, 1)),
            small((1, hd)), small((1, hd)), small((4, gd, gd)), small((1, 1024)), small((1, 1024)),
        ],
        [
            jax.ShapeDtypeStruct((s, 5120), BF16),
            jax.ShapeDtypeStruct((GMLP_HEADS, GMLP_BLOCK, GMLP_BLOCK), F32),
            jax.ShapeDtypeStruct((GMLP_HEADS, GMLP_BLOCK, 1), F32),
            jax.ShapeDtypeStruct((1, hd), F32), jax.ShapeDtypeStruct((1, hd), F32),
            jax.ShapeDtypeStruct((4, gd, gd), F32),
            jax.ShapeDtypeStruct((1, 1024), F32), jax.ShapeDtypeStruct((1, 1024), F32),
        ],
        [proj, proj, proj, dm, dm, ws, bs_col, ng, nb, pw, pb, ps], side=side)


def _rope_pair_swap(t):
    lane = lax.broadcasted_iota(jnp.int32, t.shape, 1)
    return jnp.where(lane % 64 < 32, pltpu.roll(t, 96, 1), pltpu.roll(t, 32, 1))


def _rms(x, g):
    r = lax.rsqrt(jnp.mean(x * x, axis=-1, keepdims=True) + LN_EPS)
    return x * r, r


def _rms_bwd(dy, g, xhat, r):
    dyg = dy * g
    return r * (dyg - xhat * jnp.mean(dyg * xhat, axis=-1, keepdims=True))


def _lane_lt(shape, n):
    return lax.broadcasted_iota(jnp.int32, shape, 1) < n


def _mla_prep(proj, cosp, sinp, gq, gkv, name):
    s = proj.shape[0]
    tm = min(s, 512)

    def body(qc_ref, kv_ref, c_ref, s_ref, gq_ref, gkv_ref, qn_ref, kp_ref):
        qhat, _ = _rms(qc_ref[...].astype(F32), None)
        qn_ref[...] = (qhat * gq_ref[...]).astype(BF16)
        khat, _ = _rms(kv_ref[:, 0:128].astype(F32), None)
        kp_ref[:, 0:128] = (khat * gkv_ref[...]).astype(BF16)
        kr = kv_ref[:, 128:256].astype(F32)
        kp_ref[:, 128:256] = (kr * c_ref[...] + _rope_pair_swap(kr) * s_ref[...]).astype(BF16)

    return pl.pallas_call(
        body, name=name, grid=(s // tm,),
        in_specs=[pl.BlockSpec((tm, 256), lambda i: (i, 0)), pl.BlockSpec((tm, 256), lambda i: (i, 1)),
                  pl.BlockSpec((tm, 128), lambda i: (i, 0)), pl.BlockSpec((tm, 128), lambda i: (i, 0)),
                  pl.BlockSpec((1, 256), lambda i: (0, 0)), pl.BlockSpec((1, 128), lambda i: (0, 0))],
        out_specs=[pl.BlockSpec((tm, 256), lambda i: (i, 0)), pl.BlockSpec((tm, 256), lambda i: (i, 0))],
        out_shape=[jax.ShapeDtypeStruct((s, 256), BF16), jax.ShapeDtypeStruct((s, 256), BF16)],
        compiler_params=_params(1),
    )(proj, proj, cosp, sinp, gq, gkv)


def _mla_prep_bwd(proj, dqn, dkp, dv, cosp, sinp, gq, gkv, dproj, name):
    s = proj.shape[0]
    tm = min(s, 512)

    def body(qc_ref, kv_ref, dqn_ref, dkp_ref, dv_ref, c_ref, s_ref, gq_ref, gkv_ref, dproj_ref, o_ref, red_ref):
        @pl.when(pl.program_id(0) == 0)
        def _():
            red_ref[...] = jnp.zeros_like(red_ref)

        qhat, qr = _rms(qc_ref[...].astype(F32), None)
        dq = dqn_ref[...]
        o_ref[:, 0:256] = _rms_bwd(dq, gq_ref[...], qhat, qr).astype(BF16)
        red_ref[0:1, :] += _colsum(dq * qhat)
        khat, kr = _rms(kv_ref[:, 0:128].astype(F32), None)
        dk = dkp_ref[:, 0:128] + dv_ref[...]
        o_ref[:, 256:384] = _rms_bwd(dk, gkv_ref[...], khat, kr).astype(BF16)
        red_ref[1:2, 0:128] += _colsum(dk * khat)
        dr = dkp_ref[:, 128:256]
        o_ref[:, 384:512] = (dr * c_ref[...] - _rope_pair_swap(dr) * s_ref[...]).astype(BF16)

    return pl.pallas_call(
        body, name=name, grid=(s // tm,),
        in_specs=[pl.BlockSpec((tm, 256), lambda i: (i, 0)), pl.BlockSpec((tm, 256), lambda i: (i, 1)),
                  pl.BlockSpec((tm, 256), lambda i: (i, 0)), pl.BlockSpec((tm, 256), lambda i: (i, 0)),
                  pl.BlockSpec((tm, 128), lambda i: (i, 0)),
                  pl.BlockSpec((tm, 128), lambda i: (i, 0)), pl.BlockSpec((tm, 128), lambda i: (i, 0)),
                  pl.BlockSpec((1, 256), lambda i: (0, 0)), pl.BlockSpec((1, 128), lambda i: (0, 0)),
                  pl.BlockSpec(memory_space=pl.ANY)],
        out_specs=[pl.BlockSpec((tm, 512), lambda i: (i, 0)), pl.BlockSpec((8, 256), lambda i: (0, 0))],
        out_shape=[jax.ShapeDtypeStruct(dproj.shape, BF16), jax.ShapeDtypeStruct((8, 256), F32)],
        input_output_aliases={9: 0}, compiler_params=_params(1),
    )(proj, proj, dqn, dkp, dv, cosp, sinp, gq, gkv, dproj)


HEADS_TM = 512
Z_COL0 = ODD_IN_PAD - MLA_WIDTH


def _head_cols(h):
    return slice(128 * h, 128 * h + 128)


def _q_heads(q_up, cosp, sinp, wuk, name):
    s = q_up.shape[0]
    tm = min(s, HEADS_TM)

    def body(q_ref, c_ref, s_ref, w_ref, o_ref):
        for p in range(MLA_HEADS // 2):
            raw = q_ref[:, MLA_WIDTH + 128 * p:MLA_WIDTH + 128 * (p + 1)].astype(F32)
            rot = raw * c_ref[...] + _rope_pair_swap(raw) * s_ref[...]
            low = _lane_lt(rot.shape, 64)
            o_ref[2 * p, :, 128:256] = jnp.where(low, rot, 0.0).astype(BF16)
            o_ref[2 * p + 1, :, 128:256] = jnp.where(low, pltpu.roll(rot, 64, 1), 0.0).astype(BF16)
        for h in range(MLA_HEADS):
            o_ref[h, :, 0:128] = _dot(q_ref[:, _head_cols(h)], w_ref[:, _head_cols(h)], NT).astype(BF16)

    return pl.pallas_call(
        body, name=name, grid=(s // tm,),
        in_specs=[pl.BlockSpec((tm, 3072), lambda i: (i, 0)),
                  pl.BlockSpec((tm, 128), lambda i: (i, 0)), pl.BlockSpec((tm, 128), lambda i: (i, 0)),
                  pl.BlockSpec((128, MLA_WIDTH), lambda i: (0, 0))],
        out_specs=pl.BlockSpec((MLA_HEADS, tm, 256), lambda i: (0, i, 0)),
        out_shape=jax.ShapeDtypeStruct((MLA_HEADS, s, 256), BF16), compiler_params=_params(1),
    )(q_up, cosp, sinp, wuk)


def _q_heads_bwd(dqp, q_up, cosp, sinp, wuk, name):
    s = q_up.shape[0]
    tm = min(s, HEADS_TM)

    def body(dq_ref, qn_ref, c_ref, s_ref, w_ref, dn_ref, dr_ref, dw_ref):
        @pl.when(pl.program_id(0) == 0)
        def _():
            dw_ref[...] = jnp.zeros_like(dw_ref)

        for h in range(MLA_HEADS):
            dlat = dq_ref[h, :, 0:128]
            dn_ref[:, _head_cols(h)] = _dot(dlat, w_ref[:, _head_cols(h)], NN).astype(BF16)
            dw_ref[:, _head_cols(h)] += _dot(dlat, qn_ref[:, _head_cols(h)], TN)
        for p in range(MLA_HEADS // 2):
            drot = dq_ref[2 * p, :, 128:256].astype(F32) + pltpu.roll(dq_ref[2 * p + 1, :, 128:256].astype(F32), 64, 1)
            dr_ref[:, _head_cols(p)] = (drot * c_ref[...] - _rope_pair_swap(drot) * s_ref[...]).astype(BF16)

    return pl.pallas_call(
        body, name=name, grid=(s // tm,),
        in_specs=[pl.BlockSpec((MLA_HEADS, tm, 256), lambda i: (0, i, 0)),
                  pl.BlockSpec((tm, MLA_WIDTH), lambda i: (i, 0)),
                  pl.BlockSpec((tm, 128), lambda i: (i, 0)), pl.BlockSpec((tm, 128), lambda i: (i, 0)),
                  pl.BlockSpec((128, MLA_WIDTH), lambda i: (0, 0))],
        out_specs=[pl.BlockSpec((tm, MLA_WIDTH), lambda i: (i, 0)),
                   pl.BlockSpec((tm, 1024), lambda i: (i, 0)),
                   pl.BlockSpec((128, MLA_WIDTH), lambda i: (0, 0))],
        out_shape=[jax.ShapeDtypeStruct((s, MLA_WIDTH), BF16), jax.ShapeDtypeStruct((s, 1024), BF16),
                   jax.ShapeDtypeStruct((128, MLA_WIDTH), F32)],
        compiler_params=_params(1),
    )(dqp, q_up, cosp, sinp, wuk)


def _o_gate(o_lat, proj, wuv, name):
    s = o_lat.shape[1]
    tm = min(s, HEADS_TM)

    def body(ol_ref, p_ref, w_ref, g_ref):
        for h in range(MLA_HEADS):
            z = p_ref[:, Z_COL0 + 128 * h:Z_COL0 + 128 * (h + 1)].astype(F32)
            g_ref[:, _head_cols(h)] = (_dot(ol_ref[h], w_ref[:, _head_cols(h)], NN) * (z * _sigmoid(z))).astype(BF16)

    return pl.pallas_call(
        body, name=name, grid=(s // tm,),
        in_specs=[pl.BlockSpec((MLA_HEADS, tm, 128), lambda i: (0, i, 0)),
                  pl.BlockSpec((tm, ODD_IN_PAD), lambda i: (i, 0)),
                  pl.BlockSpec((128, MLA_WIDTH), lambda i: (0, 0))],
        out_specs=pl.BlockSpec((tm, MLA_WIDTH), lambda i: (i, 0)),
        out_shape=jax.ShapeDtypeStruct((s, MLA_WIDTH), BF16), compiler_params=_params(1),
    )(o_lat, proj, wuv)


def _o_gate_bwd(dg, o_lat, proj, wuv, name):
    s = o_lat.shape[1]
    tm = min(s, HEADS_TM)

    def body(dg_ref, ol_ref, p_ref, w_ref, dp_ref, dol_ref, dw_ref):
        @pl.when(pl.program_id(0) == 0)
        def _():
            dw_ref[...] = jnp.zeros_like(dw_ref)

        dp_ref[:, 0:Z_COL0] = jnp.zeros((tm, Z_COL0), BF16)
        for h in range(MLA_HEADS):
            zc = slice(Z_COL0 + 128 * h, Z_COL0 + 128 * (h + 1))
            z, dgv, ol = p_ref[:, zc].astype(F32), dg_ref[:, _head_cols(h)].astype(F32), ol_ref[h]
            sig = _sigmoid(z)
            o = _dot(ol, w_ref[:, _head_cols(h)], NN)
            dp_ref[:, zc] = (dgv * o * (sig * (1.0 + z * (1.0 - sig)))).astype(BF16)
            do = (dgv * (z * sig)).astype(BF16)
            dol_ref[h] = _dot(do, w_ref[:, _head_cols(h)], NT).astype(BF16)
            dw_ref[:, _head_cols(h)] += _dot(ol, do, TN)

    return pl.pallas_call(
        body, name=name, grid=(s // tm,),
        in_specs=[pl.BlockSpec((tm, MLA_WIDTH), lambda i: (i, 0)),
                  pl.BlockSpec((MLA_HEADS, tm, 128), lambda i: (0, i, 0)),
                  pl.BlockSpec((tm, ODD_IN_PAD), lambda i: (i, 0)),
                  pl.BlockSpec((128, MLA_WIDTH), lambda i: (0, 0))],
        out_specs=[pl.BlockSpec((tm, ODD_IN_PAD), lambda i: (i, 0)),
                   pl.BlockSpec((MLA_HEADS, tm, 128), lambda i: (0, i, 0)),
                   pl.BlockSpec((128, MLA_WIDTH), lambda i: (0, 0))],
        out_shape=[jax.ShapeDtypeStruct((s, ODD_IN_PAD), BF16), jax.ShapeDtypeStruct((MLA_HEADS, s, 128), BF16),
                   jax.ShapeDtypeStruct((128, MLA_WIDTH), F32)],
        compiler_params=_params(1),
    )(dg, o_lat, proj, wuv)


ATT_TQ = CHUNK
ATT_ROWS = ATT_TQ * MLA_HEADS
ATT_TK = 512
ATT_HEAD_GROUP = 8


def _visible(k0, q_chunk, tk):
    kpos = k0 + lax.broadcasted_iota(jnp.int32, (1, tk), 1)
    return kpos // CHUNK <= q_chunk


def _tile_lanes(t, n):
    return jnp.concatenate([t] * (n // 128), axis=1)


def _key_blocks(i, tk, block, pairs=False):
    visible = i * ATT_TQ + ATT_TQ
    n_full = (visible + tk - 1) // tk - 1

    def full(j):
        block(pl.multiple_of(j * tk, tk), tk, False)

    if pairs:
        def two(jj, carry):
            full(2 * jj)
            full(2 * jj + 1)
            return carry

        lax.fori_loop(0, n_full // 2, two, 0)

        @pl.when(n_full % 2 == 1)
        def _():
            full(n_full - 1)
    else:
        def one(j, carry):
            full(j)
            return carry

        lax.fori_loop(0, n_full, one, 0)
    last0 = pl.multiple_of(n_full * tk, tk)
    half = tk // 2
    if half % 128 == 0:
        @pl.when(visible - n_full * tk <= half)
        def _():
            block(last0, half, True)

        @pl.when(visible - n_full * tk > half)
        def _():
            block(last0, tk, True)
    else:
        block(last0, tk, True)


def _attn_fwd(qp, kp, name, side=None):
    s = kp.shape[0]
    tk = min(ATT_TK, s)

    def body(q_ref, k_ref, o_ref, lse_ref, m_sc, acc_sc):
        i = pl.program_id(0)
        m_sc[...] = jnp.full_like(m_sc, -jnp.inf)
        acc_sc[...] = jnp.zeros_like(acc_sc)

        def block(k0, width, masked):
            k = k_ref[pl.ds(k0, width), :]
            v1 = jnp.where(_lane_lt(k.shape, 128), k, jnp.ones_like(k))
            for h0 in range(0, MLA_HEADS, ATT_HEAD_GROUP):
                rows = slice(h0 * ATT_TQ, (h0 + ATT_HEAD_GROUP) * ATT_TQ)
                q = q_ref[h0:h0 + ATT_HEAD_GROUP].reshape(ATT_HEAD_GROUP * ATT_TQ, 256)
                sc = _dot(q, k, NT) * ATTN_SCALE_LOG2
                if masked:
                    sc = jnp.where(_visible(k0, i, width), sc, -jnp.inf)
                m_prev = m_sc[rows]
                m_new = jnp.maximum(m_prev, jnp.max(sc, axis=1, keepdims=True))
                p = jnp.exp2(sc - _tile_lanes(m_new, width))
                acc_sc[rows] = _tile_lanes(jnp.exp2(m_prev - m_new), 256) * acc_sc[rows] + _dot(p, v1, NN)
                m_sc[rows] = m_new

        _key_blocks(i, tk, block, pairs=True)
        acc = acc_sc[...]
        l = acc[:, 128:256]
        o_ref[...] = (acc[:, 0:128] / l).astype(BF16).reshape(MLA_HEADS, ATT_TQ, 128)
        lse_ref[...] = (m_sc[...] + jnp.log2(l)).reshape(MLA_HEADS, ATT_TQ, 128)

    head128 = pl.BlockSpec((MLA_HEADS, ATT_TQ, 128), lambda i: (0, i, 0))
    return _grid_call(
        body, name, (s // ATT_TQ,),
        [pl.BlockSpec((MLA_HEADS, ATT_TQ, 256), lambda i: (0, i, 0)), pl.BlockSpec((s, 256), lambda i: (0, 0))],
        [head128, head128],
        [jax.ShapeDtypeStruct((MLA_HEADS, s, 128), BF16), jax.ShapeDtypeStruct((MLA_HEADS, s, 128), F32)],
        [qp, kp], scratch=[pltpu.VMEM((ATT_ROWS, 128), F32), pltpu.VMEM((ATT_ROWS, 256), F32)], side=side)


def _attn_bwd(qp, kp, o, do, lse, name, side=None):
    s = kp.shape[0]
    tk = min(ATT_TK, s)

    def body(q_ref, k_ref, o_ref, do_ref, lse_ref, dq_ref, dk_ref, dv_ref, dq_sc):
        i = pl.program_id(0)

        @pl.when(i == 0)
        def _():
            dk_ref[...] = jnp.zeros_like(dk_ref)
            dv_ref[...] = jnp.zeros_like(dv_ref)

        q = q_ref[...].reshape(ATT_ROWS, 256)
        dov = do_ref[...].reshape(ATT_ROWS, 128)
        delta = jnp.sum(dov.astype(F32) * o_ref[...].reshape(ATT_ROWS, 128).astype(F32), axis=1, keepdims=True)
        delta_t = _tile_lanes(jnp.broadcast_to(delta, (ATT_ROWS, 128)), tk)
        lse_t = _tile_lanes(lse_ref[...].reshape(ATT_ROWS, 128), tk)
        dq_sc[...] = jnp.zeros_like(dq_sc)

        def block(k0, width, masked):
            k = k_ref[pl.ds(k0, width), :]
            p = jnp.exp2(_dot(q, k, NT) * ATTN_SCALE_LOG2 - lse_t[:, 0:width])
            if masked:
                p = jnp.where(_visible(k0, i, width), p, 0.0)
            dv_ref[pl.ds(k0, width), :] += _dot(p, dov, TN)
            ds = (p * (_dot(dov, k[:, 0:128], NT) - delta_t[:, 0:width]) * ATTN_SCALE).astype(BF16)
            dq_sc[...] += _dot(ds, k, NN)
            dk_ref[pl.ds(k0, width), :] += _dot(ds, q, TN)

        _key_blocks(i, tk, block, pairs=True)
        dq_ref[...] = dq_sc[...].astype(BF16).reshape(MLA_HEADS, ATT_TQ, 256)

    head128 = pl.BlockSpec((MLA_HEADS, ATT_TQ, 128), lambda i: (0, i, 0))
    head256 = pl.BlockSpec((MLA_HEADS, ATT_TQ, 256), lambda i: (0, i, 0))
    return _grid_call(
        body, name, (s // ATT_TQ,),
        [head256, pl.BlockSpec((s, 256), lambda i: (0, 0)), head128, head128, head128],
        [head256, pl.BlockSpec((s, 256), lambda i: (0, 0)), pl.BlockSpec((s, 128), lambda i: (0, 0))],
        [jax.ShapeDtypeStruct((MLA_HEADS, s, 256), BF16),
         jax.ShapeDtypeStruct((s, 256), F32), jax.ShapeDtypeStruct((s, 128), F32)],
        [qp, kp, o, do, lse], scratch=[pltpu.VMEM((ATT_ROWS, 256), F32)], side=side)


def _place():
    x, y, c = lax.axis_index("x"), lax.axis_index("y"), lax.axis_index("c")
    return x, y, c, 4 * x + 2 * y + c


def _flip(x, y, c, r):
    px = 1 - x if r & 4 else x
    py = 1 - y if r & 2 else y
    pc = 1 - c if r & 1 else c
    return (px, py, pc), 4 * px + 2 * py + pc


def _adaln_exchange(c8, ada_w, ada_b_cols, blocks, name):
    d = c8.shape[1]
    w_cols = ada_w.shape[2]
    n_arr = len(blocks)

    def body(c_ref, w_ref, b_ref, *refs):
        x_refs, (call_ref, mod_ref), out_refs = refs[:n_arr], refs[n_arr:n_arr + 2], refs[n_arr + 2:2 * n_arr + 2]
        sbuf, rbuf, s1, r1, s2, r2 = refs[2 * n_arr + 2:2 * n_arr + 8]
        gather = _Gather(x_refs, out_refs, *refs[2 * n_arr + 8:])
        x, y, c, me = _place()
        call_ref[pl.ds(pl.multiple_of(me * 8, 8), 8), :] = c_ref[...]
        peers = [_flip(x, y, c, r) for r in range(1, N_DEV)]

        def c_copy(k, src_lin, to):
            rows = call_ref.at[pl.ds(pl.multiple_of(src_lin * 8, 8), 8), :]
            return pltpu.make_async_remote_copy(src_ref=rows, dst_ref=rows, send_sem=s1.at[k], recv_sem=r1.at[k],
                                                device_id=to, device_id_type=MESH)

        first = [c_copy(k, me, peer) for k, (peer, _) in enumerate(peers)]
        for cp in first:
            cp.start()
        for k, (_, lin) in enumerate(peers):
            c_copy(k, lin, (x, y, c)).wait_recv()
        for cp in first:
            cp.wait_send()

        for j in range(N_DEV):
            cj = call_ref[8 * j:8 * j + 8, :]
            cond = cj * _sigmoid(cj)
            for l in range(2):
                sbuf[j, l] = lax.dot_general(cond, w_ref[l], NN, precision=lax.Precision.HIGHEST,
                                             preferred_element_type=F32) + b_ref[l]

        def m_copy(k, src_slot, dst_slot, to):
            return pltpu.make_async_remote_copy(src_ref=sbuf.at[src_slot], dst_ref=rbuf.at[dst_slot],
                                                send_sem=s2.at[k], recv_sem=r2.at[k], device_id=to,
                                                device_id_type=MESH)

        rbuf[me] = sbuf[me]
        second = [m_copy(k, lin, me, peer) for k, (peer, lin) in enumerate(peers)]
        for cp in second:
            cp.start()
        gather.start()
        for k, (_, lin) in enumerate(peers):
            m_copy(k, lin, lin, (x, y, c)).wait_recv()
        for cp in second:
            cp.wait_send()
        for j in range(N_DEV):
            for l in range(2):
                mod_ref[l, :, w_cols * j:w_cols * (j + 1)] = rbuf[j, l]
        gather.forward()
        gather.end()

    vmem = pl.BlockSpec(memory_space=pltpu.VMEM)
    anyspace = pl.BlockSpec(memory_space=pl.ANY)
    g_shapes, g_sems = _gather_extras(blocks)
    return pl.pallas_call(
        body, name=name, in_specs=[vmem, vmem, vmem] + [anyspace] * n_arr, out_specs=[vmem, vmem] + [anyspace] * n_arr,
        out_shape=[jax.ShapeDtypeStruct((8 * N_DEV, d), F32), jax.ShapeDtypeStruct((2, 8, 3 * d), F32)] + g_shapes,
        scratch_shapes=[pltpu.VMEM((N_DEV, 2, 8, w_cols), F32), pltpu.VMEM((N_DEV, 2, 8, w_cols), F32),
                        pltpu.SemaphoreType.DMA((N_DEV - 1,)), pltpu.SemaphoreType.DMA((N_DEV - 1,)),
                        pltpu.SemaphoreType.DMA((N_DEV - 1,)), pltpu.SemaphoreType.DMA((N_DEV - 1,))] + g_sems,
        compiler_params=pltpu.CompilerParams(vmem_limit_bytes=VMEM_LIMIT),
    )(c8, ada_w, ada_b_cols, *blocks)


class _Gather:
    def __init__(self, x_refs, out_refs, send_sems, recv_sems, local_sems):
        x, y, c, _ = _place()
        self.me, self.sibling, self.c = (x, y, c), (x, y, 1 - c), c
        self.chips = [(1 - x, y), (x, 1 - y), (1 - x, 1 - y)]
        self.n_arr = len(x_refs)
        self.out_refs, self.send_sems, self.recv_sems = out_refs, send_sems, recv_sems
        self.mine = [pltpu.make_async_copy(x_refs[t], out_refs[t].at[4 * x + 2 * y + c], local_sems.at[t])
                     for t in range(self.n_arr)]
        self.first = []
        for t in range(self.n_arr):
            self.first.append(self.copy(t, 0, self.me, self.sibling, src=x_refs[t]))
            self.first += [self.copy(t, 1 + j, self.me, (*chip, c), src=x_refs[t]) for j, chip in enumerate(self.chips)]
        self.passed = [self.copy(t, 4 + j, (*chip, c), self.sibling)
                       for t in range(self.n_arr) for j, chip in enumerate(self.chips)]

    def copy(self, t, k, blk, to, src=None):
        slot = self.out_refs[t].at[4 * blk[0] + 2 * blk[1] + blk[2]]
        return pltpu.make_async_remote_copy(src_ref=slot if src is None else src, dst_ref=slot,
                                            send_sem=self.send_sems.at[7 * t + k], recv_sem=self.recv_sems.at[7 * t + k],
                                            device_id=to, device_id_type=MESH)

    def start(self):
        for cp in self.mine + self.first:
            cp.start()

    def forward(self):
        for t in range(self.n_arr):
            for j, chip in enumerate(self.chips):
                self.copy(t, 1 + j, (*chip, self.c), self.me).wait_recv()
                self.passed[3 * t + j].start()

    def end(self):
        for t in range(self.n_arr):
            self.copy(t, 0, self.sibling, self.me).wait_recv()
            for j, chip in enumerate(self.chips):
                self.copy(t, 4 + j, (*chip, 1 - self.c), self.me).wait_recv()
        for cp in self.first + self.passed:
            cp.wait_send()
        for cp in self.mine:
            cp.wait()


def _gather_extras(blocks):
    n_arr = len(blocks)
    return ([jax.ShapeDtypeStruct((N_DEV,) + b.shape, b.dtype) for b in blocks],
            [pltpu.SemaphoreType.DMA((7 * n_arr,)), pltpu.SemaphoreType.DMA((7 * n_arr,)),
             pltpu.SemaphoreType.DMA((n_arr,))])


def _all_gather(blocks, name):
    n_arr = len(blocks)

    def body(*refs):
        gather = _Gather(refs[:n_arr], refs[n_arr:2 * n_arr], *refs[2 * n_arr:])
        gather.start()
        gather.forward()
        gather.end()

    anyspace = pl.BlockSpec(memory_space=pl.ANY)
    shapes, sems = _gather_extras(blocks)
    return pl.pallas_call(body, name=name, in_specs=[anyspace] * n_arr, out_specs=[anyspace] * n_arr,
                          out_shape=shapes, scratch_shapes=sems)(*blocks)


def _scatter_parts(parts, name):
    n_arr = len(parts)

    def body(*refs):
        copies = _exchange_copies(refs[:n_arr], refs[n_arr:2 * n_arr], *refs[2 * n_arr:], "devices")
        _exchange_start(copies)
        _exchange_wait(copies)

    anyspace = pl.BlockSpec(memory_space=pl.ANY)
    shapes, sems = _exchange_extras(parts, "devices")
    return pl.pallas_call(body, name=name, in_specs=[anyspace] * n_arr, out_specs=[anyspace] * n_arr,
                          out_shape=shapes, scratch_shapes=sems)(*parts)


def _sum_parts(parts, name):
    def body(p_ref, g_ref):
        g = p_ref[0]
        for j in range(1, N_DEV):
            g = g + p_ref[j]
        g_ref[...] = g

    return pl.pallas_call(body, name=name, out_shape=jax.ShapeDtypeStruct(parts.shape[1:], F32),
                          compiler_params=pltpu.CompilerParams(vmem_limit_bytes=VMEM_LIMIT))(parts)


N_CHIPS = N_DEV // 2


def _sibling_swap(part, name):
    def body(g_ref, r_ref, send_sems, recv_sems):
        x, y, c, _ = _place()
        sends = [pltpu.make_async_remote_copy(
            src_ref=g_ref.at[2 * q + 1 - c], dst_ref=r_ref.at[q], send_sem=send_sems.at[q], recv_sem=recv_sems.at[q],
            device_id=(x, y, 1 - c), device_id_type=MESH) for q in range(N_CHIPS)]
        recvs = [pltpu.make_async_remote_copy(
            src_ref=g_ref.at[2 * q + c], dst_ref=r_ref.at[q], send_sem=send_sems.at[q], recv_sem=recv_sems.at[q],
            device_id=(x, y, c), device_id_type=MESH) for q in range(N_CHIPS)]
        for cp in sends:
            cp.start()
        for cp in recvs:
            cp.wait_recv()
        for cp in sends:
            cp.wait_send()

    anyspace = pl.BlockSpec(memory_space=pl.ANY)
    return pl.pallas_call(
        body, name=name, in_specs=[anyspace], out_specs=anyspace,
        out_shape=jax.ShapeDtypeStruct((N_CHIPS,) + part.shape[1:], part.dtype),
        scratch_shapes=[pltpu.SemaphoreType.DMA((N_CHIPS,)), pltpu.SemaphoreType.DMA((N_CHIPS,))])(part)


def _pair_sum(a, b, name):
    n, rows, cols = a.shape
    tr = max(t for t in range(16, 513, 16) if rows % t == 0)

    def body(a_ref, b_ref, o_ref):
        o_ref[...] = (a_ref[...].astype(F32) + b_ref[...].astype(F32)).astype(o_ref.dtype)

    blk = pl.BlockSpec((None, tr, cols), lambda q, i: (q, i, 0))
    return pl.pallas_call(body, name=name, grid=(n, rows // tr), in_specs=[blk, blk], out_specs=blk,
                          out_shape=jax.ShapeDtypeStruct(a.shape, a.dtype), compiler_params=_params(2))(a, b)


def _adamw(w, g, m, v):
    m = ADAM_B1 * m + (1.0 - ADAM_B1) * g
    v = ADAM_B2 * v + (1.0 - ADAM_B2) * (g * g)
    m_hat = m / (1.0 - ADAM_B1 ** ADAM_STEP)
    v_hat = v / (1.0 - ADAM_B2 ** ADAM_STEP)
    return -ADAM_LR * (m_hat / (jnp.sqrt(v_hat) + ADAM_EPS) + ADAM_WD * w), m, v


def _sum_parts_adamw(parts, w, m, v, name):
    n_parts, rows, cols = parts.shape
    tr = max(t for t in range(16, 257, 16) if rows % t == 0)

    def body(p_ref, w_ref, m_ref, v_ref, g_ref, d_ref, mo_ref, vo_ref):
        g = p_ref[0].astype(F32)
        for j in range(1, n_parts):
            g = g + p_ref[j].astype(F32)
        g_ref[...] = g
        d_ref[...], mo_ref[...], vo_ref[...] = _adamw(w_ref[...], g, m_ref[...], v_ref[...])

    row = _rows3(tr, cols)
    out = jax.ShapeDtypeStruct((1, rows, cols), F32)
    return pl.pallas_call(
        body, name=name, grid=(rows // tr,),
        in_specs=[pl.BlockSpec((n_parts, tr, cols), lambda i: (0, i, 0)), row, row, row],
        out_specs=[row, row, row, row], out_shape=[out, out, out, out], compiler_params=_params(1),
    )(parts, w, m, v)


def _sum_parts_adamw_whole(parts, w, m, v, name):
    def body(p_ref, w_ref, m_ref, v_ref, g_ref, d_ref, mo_ref, vo_ref):
        g = p_ref[0:1].astype(F32)
        for j in range(1, N_DEV):
            g = g + p_ref[j:j + 1].astype(F32)
        g_ref[...] = g
        d_ref[...], mo_ref[...], vo_ref[...] = _adamw(w_ref[...], g, m_ref[...], v_ref[...])

    out = jax.ShapeDtypeStruct(w.shape, F32)
    return pl.pallas_call(body, name=name, out_shape=[out] * 4,
                          compiler_params=pltpu.CompilerParams(vmem_limit_bytes=VMEM_LIMIT))(parts, w, m, v)


def _adamw_many(gs, ws, ms, vs, name):
    n = len(gs)

    def body(*refs):
        for k in range(n):
            g_ref, w_ref, m_ref, v_ref = (refs[q * n + k] for q in range(4))
            d_ref, mo_ref, vo_ref = (refs[(4 + q) * n + k] for q in range(3))
            d_ref[...], mo_ref[...], vo_ref[...] = _adamw(w_ref[...], g_ref[...], m_ref[...], v_ref[...])

    out = [jax.ShapeDtypeStruct(w.shape, F32) for w in ws]
    res = pl.pallas_call(body, name=name, out_shape=out * 3,
                         compiler_params=pltpu.CompilerParams(vmem_limit_bytes=VMEM_LIMIT))(*gs, *ws, *ms, *vs)
    return res[:n], res[n:2 * n], res[2 * n:]


def _ada_w_grad_adamw(c_all, dmod_rows, w, m, v, name):
    def body(c_ref, dm_ref, w_ref, m_ref, v_ref, g_ref, d_ref, mo_ref, vo_ref):
        cv = c_ref[...]
        g = lax.dot_general(cv * _sigmoid(cv), dm_ref[...], TN, precision=lax.Precision.HIGHEST,
                            preferred_element_type=F32)
        g_ref[...] = g
        d_ref[...], mo_ref[...], vo_ref[...] = _adamw(w_ref[...], g, m_ref[...], v_ref[...])

    n_layers, d, cols = w.shape
    layer = pl.BlockSpec((None, d, cols), lambda l: (l, 0, 0))
    out = jax.ShapeDtypeStruct(w.shape, F32)
    return pl.pallas_call(
        body, name=name, grid=(n_layers,),
        in_specs=[pl.BlockSpec(c_all.shape, lambda l: (0, 0)),
                  pl.BlockSpec((None,) + dmod_rows.shape[1:], lambda l: (l, 0, 0)), layer, layer, layer],
        out_specs=[layer] * 4, out_shape=[out] * 4, compiler_params=_params(1),
    )(c_all, dmod_rows, w, m, v)


REPLICATED = ("ln_g", "ln_b", "gmlp_norm_g", "gmlp_norm_b", "gmlp_ws", "gmlp_bs", "pool_b", "pool_scale",
              "mla_kv_norm_g", "mla_w_uk", "mla_w_uv")
CHUNK_ROWS, ADA_ROW, QNORM_ROW, LOSS_ROW, REP_ROWS = 73, 73, 74, 75, 80
UQ_ROWS, POOLW_ROWS = 96, 32


def _pad_rows(flat2d, rows):
    n, k = flat2d.shape
    return jnp.pad(flat2d, ((0, 0), (0, rows * LANES - k))).reshape(n, rows, LANES)


def _ada_cols_rows(vec):
    return _pad_rows(vec.reshape(2, N_DEV, -1).transpose(1, 0, 2).reshape(N_DEV, -1), 1)


def _unpack_replicated(rep, shapes):
    chunk = sum(s[1] for s in shapes) // N_DEV
    flat, off, out = rep[:, :CHUNK_ROWS].reshape(N_DEV, -1)[:, :chunk].reshape(-1), 0, {}
    for n, size, shape in shapes:
        out[n] = flat[off:off + size].reshape(shape)
        off += size
    cols = 3 * D_MODEL // N_DEV
    out["ada_b"] = rep[:, ADA_ROW, :2 * cols].reshape(N_DEV, 2, cols).transpose(1, 0, 2).reshape(2, -1)
    return out


def kernel(x, c, positions, ada_w, ada_b, ln_g, ln_b, e_w_in, gmlp_norm_g, gmlp_norm_b, gmlp_ws, gmlp_bs, pool_w, pool_b, pool_scale, e_w_out, o_w_in, mla_q_norm_g, mla_kv_norm_g, mla_w_uq, mla_w_uk, mla_w_uv, o_w_out, loss_target, m_ada_w, m_ada_b, m_ln_g, m_ln_b, m_e_w_in, m_gmlp_norm_g, m_gmlp_norm_b, m_gmlp_ws, m_gmlp_bs, m_pool_w, m_pool_b, m_pool_scale, m_e_w_out, m_o_w_in, m_mla_q_norm_g, m_mla_kv_norm_g, m_mla_w_uq, m_mla_w_uk, m_mla_w_uv, m_o_w_out, v_ada_w, v_ada_b, v_ln_g, v_ln_b, v_e_w_in, v_gmlp_norm_g, v_gmlp_norm_b, v_gmlp_ws, v_gmlp_bs, v_pool_w, v_pool_b, v_pool_scale, v_e_w_out, v_o_w_in, v_mla_q_norm_g, v_mla_kv_norm_g, v_mla_w_uq, v_mla_w_uk, v_mla_w_uv, v_o_w_out):
    w_in = dict(ada_w=ada_w, ada_b=ada_b, ln_g=ln_g, ln_b=ln_b, e_w_in=e_w_in, gmlp_norm_g=gmlp_norm_g,
                gmlp_norm_b=gmlp_norm_b, gmlp_ws=gmlp_ws, gmlp_bs=gmlp_bs, pool_w=pool_w, pool_b=pool_b,
                pool_scale=pool_scale, e_w_out=e_w_out, o_w_in=o_w_in, mla_q_norm_g=mla_q_norm_g,
                mla_kv_norm_g=mla_kv_norm_g, mla_w_uq=mla_w_uq, mla_w_uk=mla_w_uk, mla_w_uv=mla_w_uv, o_w_out=o_w_out)
    m_in = dict(ada_w=m_ada_w, ada_b=m_ada_b, ln_g=m_ln_g, ln_b=m_ln_b, e_w_in=m_e_w_in, gmlp_norm_g=m_gmlp_norm_g,
                gmlp_norm_b=m_gmlp_norm_b, gmlp_ws=m_gmlp_ws, gmlp_bs=m_gmlp_bs, pool_w=m_pool_w, pool_b=m_pool_b,
                pool_scale=m_pool_scale, e_w_out=m_e_w_out, o_w_in=m_o_w_in, mla_q_norm_g=m_mla_q_norm_g,
                mla_kv_norm_g=m_mla_kv_norm_g, mla_w_uq=m_mla_w_uq, mla_w_uk=m_mla_w_uk, mla_w_uv=m_mla_w_uv,
                o_w_out=m_o_w_out)
    v_in = dict(ada_w=v_ada_w, ada_b=v_ada_b, ln_g=v_ln_g, ln_b=v_ln_b, e_w_in=v_e_w_in, gmlp_norm_g=v_gmlp_norm_g,
                gmlp_norm_b=v_gmlp_norm_b, gmlp_ws=v_gmlp_ws, gmlp_bs=v_gmlp_bs, pool_w=v_pool_w, pool_b=v_pool_b,
                pool_scale=v_pool_scale, e_w_out=v_e_w_out, o_w_in=v_o_w_in, mla_q_norm_g=v_mla_q_norm_g,
                mla_kv_norm_g=v_mla_kv_norm_g, mla_w_uq=v_mla_w_uq, mla_w_uk=v_mla_w_uk, mla_w_uv=v_mla_w_uv,
                o_w_out=v_o_w_out)
    names = list(w_in)
    seq = x.shape[1]
    d = D_MODEL
    me = 4 * lax.axis_index("x") + 2 * lax.axis_index("y") + lax.axis_index("c")
    ada_cols = ada_w.shape[2]

    ada_b_cols = lax.dynamic_slice_in_dim(ada_b, me * ada_cols, ada_cols, axis=1)
    slab_row = lax.broadcasted_iota(jnp.int32, (8, d), 0)
    slab = jnp.where(slab_row == 0, c, jnp.where(slab_row == 1, jnp.pad(mla_q_norm_g, ((0, 0), (0, d - 32))), 0.0))
    c_all, mod, w_in_e3, pool_w3 = _adaln_exchange(
        slab, ada_w, jnp.broadcast_to(ada_b_cols[:, None, :], (2, 8, ada_cols)),
        [e_w_in[0].astype(BF16), pool_w.astype(BF16).reshape(POOLW_ROWS, LANES)], "adaln_exchange")
    h0 = _modulate(x, mod[0], "modulate0")
    proj0, o_in3 = _matmul_cols_nn(h0, w_in_e3, BF16, 512, "even_in", side=([o_w_in[0].astype(BF16)], "gather"))
    o_in_full = o_in3.transpose(1, 0, 2).reshape(d, ODD_IN)
    w_in_o = jnp.concatenate([o_in_full[:, :448], jnp.zeros((d, 64), BF16), o_in_full[:, 448:]], axis=1)
    pool_w_full = pool_w3.reshape(N_DEV, 4, 32, 256).transpose(1, 0, 2, 3).reshape(4, 256, 256)
    g_q = c_all.reshape(N_DEV, 8, d)[:, 1, :32].reshape(1, MLA_Q_RANK)

    ws, bs_col = gmlp_ws[0], gmlp_bs[0].reshape(GMLP_HEADS, GMLP_BLOCK, 1)
    wuk2, wuv2 = mla_w_uk[0].reshape(MLA_KV_RANK, -1), mla_w_uv[0].reshape(MLA_KV_RANK, -1)
    inv = 1.0 / (ROPE_THETA ** (jnp.arange(0, MLA_ROPE, 2, dtype=F32) / MLA_ROPE))
    ang = positions[0].astype(F32)[:, None] * inv
    cosp = jnp.tile(jnp.cos(ang), (1, 4))
    sinp = jnp.tile(jnp.concatenate([-jnp.sin(ang), jnp.sin(ang)], axis=1), (1, 2))

    mix0, w_out_e3 = _even_fwd(proj0, ws, bs_col, gmlp_norm_g, gmlp_norm_b, pool_w_full, pool_b, pool_scale, "even_mix",
                               side=([e_w_out[0].astype(BF16)], "gather"))
    w_out_e = w_out_e3.reshape(-1, d)
    y0, uq3 = _matmul([(mix0, w_out_e)], "nn", F32, seq, d, 512, 1024, "even_out",
                      side=([mla_w_uq.astype(BF16).reshape(UQ_ROWS, LANES)], "gather"))
    uq_full = uq3.reshape(MLA_Q_RANK, MLA_HEADS, MLA_NOPE + MLA_ROPE)
    w_uq_n = uq_full[:, :, :MLA_NOPE].reshape(MLA_Q_RANK, -1)
    w_uq_r = uq_full[:, :, MLA_NOPE:].reshape(MLA_Q_RANK, -1)
    w_uq = jnp.concatenate([w_uq_n, w_uq_r], axis=1)
    x1, h1 = _resid_ln(x, y0, mod[0], ln_g[0:1], ln_b[0:1], mod[1], "resid_ln0")

    (proj1,) = _matmul([(h1, w_in_o)], "nn", BF16, seq, ODD_IN_PAD, 512, ODD_IN_PAD, "odd_in")
    qn, kp = _mla_prep(proj1, cosp, sinp, g_q, mla_kv_norm_g, "mla_prep")
    (q_up,) = _matmul([(qn, w_uq)], "nn", BF16, seq, 3072, 512, 3072, "q_up")
    qp = _q_heads(q_up, cosp, sinp, wuk2, "q_heads")
    o_lat, lse, w_out_o3 = _attn_fwd(qp, kp, "attn_fwd", side=([o_w_out[0].astype(BF16)], "gather"))
    w_out_o = w_out_o3.reshape(-1, d)
    gated = _o_gate(o_lat, proj1, wuv2, "o_gate")
    (y1,) = _matmul([(gated, w_out_o)], "nn", F32, seq, d, 512, 1024, "odd_out")

    dy1, dxres1, red2 = _final_ln_loss_bwd(x1, y1, mod[1], ln_g[1:2], ln_b[1:2], loss_target, "final_ln_loss")
    (dgated,) = _matmul([(dy1, w_out_o)], "nt", BF16, seq, MLA_WIDTH, 512, MLA_WIDTH, "odd_out_dx")
    (g_w_out_o,) = _matmul([(gated, dy1)], "tn", BF16, MLA_WIDTH, d, 256, d, "odd_out_dw")
    dproj1_z, do_lat, g_wuv = _o_gate_bwd(dgated, o_lat, proj1, wuv2, "o_gate_bwd")
    dqp, dkp, dvv, r_o_out = _attn_bwd(qp, kp, o_lat, do_lat, lse, "attn_bwd",
                                       side=([g_w_out_o.reshape(N_DEV, -1, d)], "devices"))
    dq_nope, dq_rope, g_wuk = _q_heads_bwd(dqp, q_up, cosp, sinp, wuk2, "q_heads_bwd")
    (dqn,) = _matmul([(dq_nope, w_uq_n), (dq_rope, w_uq_r)], "nt", F32, seq, MLA_Q_RANK, 512, 256, "q_up_dx")
    (g_wuq_n,) = _matmul([(qn, dq_nope)], "tn", F32, MLA_Q_RANK, MLA_WIDTH, 256, 512, "q_up_dw_nope")
    (g_wuq_r,) = _matmul([(qn, dq_rope)], "tn", F32, MLA_Q_RANK, 1024, 256, 512, "q_up_dw_rope")
    dproj1, red_mla = _mla_prep_bwd(proj1, dqn, dkp, dvv, cosp, sinp, g_q, mla_kv_norm_g, dproj1_z, "mla_prep_bwd")
    (dh1,) = _matmul([(dproj1, w_in_o)], "nt", F32, seq, d, 512, d, "odd_in_dx")
    part_uq = jnp.concatenate([g_wuq_n.reshape(MLA_Q_RANK, MLA_HEADS, MLA_NOPE),
                               g_wuq_r.reshape(MLA_Q_RANK, MLA_HEADS, MLA_ROPE)], axis=2).astype(BF16).reshape(
                                   (N_DEV,) + mla_w_uq.shape[1:])
    (g_w_in_o,) = _matmul([(h1, dproj1)], "tn", BF16, d, ODD_IN_PAD, 256, ODD_IN_PAD // 2, "odd_in_dw", n_outer=True)
    part_o_in = jnp.concatenate([g_w_in_o[:, :448], g_w_in_o[:, 512:]], axis=1).reshape(d, N_DEV, -1).transpose(1, 0, 2)
    dy0, dxres0, red1 = _mid_bwd(dh1, dxres1, x, y0, mod[0], mod[1], ln_g[0:1], ln_b[0:1], "mid_bwd")
    dmix, r_uq = _matmul([(dy0, w_out_e)], "nt", BF16, seq, 2048, 512, 2048, "even_out_dx", side=([part_uq], "devices"))
    (g_w_out_e,) = _matmul([(mix0, dy0)], "tn", BF16, 2048, d, 256, d, "even_out_dw")
    dproj0, g_ws, g_bs, g_ng, g_nb, g_pw, g_pb, g_ps, r_o_in = _even_bwd(
        proj0, dmix, ws, bs_col, gmlp_norm_g, gmlp_norm_b, pool_w_full, pool_b, pool_scale, "even_mix_bwd",
        side=([part_o_in], "devices"))
    part_pw = g_pw.reshape(4, N_DEV, 32, 256).transpose(1, 0, 2, 3)
    part_e_in, r_e_out, r_pw = _matmul_cols_tn(h0, dproj0, w_in_e3.shape[2], BF16, 512, "even_in_dw",
                                               side=([g_w_out_e.reshape(N_DEV, -1, d), part_pw], "devices"))
    mine = lax.dynamic_index_in_dim(part_e_in.reshape((N_CHIPS, 2) + part_e_in.shape[1:]), lax.axis_index("c"), 1, False)
    chip_e_in = _pair_sum(mine, _sibling_swap(part_e_in, "e_in_sibling_swap"), "e_in_pair_sum")
    grad_x, red0, r_e_in = _first_bwd(dproj0, w_in_e3, dxres0, x, mod[0], "even_in_dx", side=([chip_e_in], "chips"))

    t_mask = lax.broadcasted_iota(jnp.int32, (GMLP_BLOCK, GMLP_BLOCK), 0) // CHUNK
    s_mask = lax.broadcasted_iota(jnp.int32, (GMLP_BLOCK, GMLP_BLOCK), 1) // CHUNK
    part = {
        "ln_g": jnp.stack([red1[2], red2[0]]), "ln_b": jnp.stack([red1[3], red2[1]]),
        "gmlp_norm_g": g_ng, "gmlp_norm_b": g_nb,
        "gmlp_ws": jnp.where(s_mask <= t_mask, g_ws, 0.0), "gmlp_bs": g_bs,
        "pool_b": g_pb, "pool_scale": g_ps, "mla_kv_norm_g": red_mla[1, :MLA_KV_RANK],
        "mla_w_uk": g_wuk, "mla_w_uv": g_wuv,
    }
    dmod = jnp.stack([jnp.concatenate([red0[1], red0[0], red1[4]]),
                      jnp.concatenate([red1[1], red1[0], red2[2]])])

    loss_row = jnp.pad(jnp.broadcast_to((0.5 / d * jnp.sum(red2[3])).reshape(1, 1, 1), (N_DEV, 1, 1)),
                       ((0, 0), (0, 0), (0, LANES - 1)))
    part_small = jnp.concatenate([
        _pad_rows(jnp.concatenate([part[n].reshape(-1) for n in REPLICATED]).reshape(N_DEV, -1), CHUNK_ROWS),
        jnp.pad(jnp.concatenate([_ada_cols_rows(dmod), _pad_rows(red_mla[0].reshape(N_DEV, -1), 1), loss_row], axis=1),
                ((0, 0), (0, REP_ROWS - LOSS_ROW - 1), (0, 0)))], axis=1)
    (r_small,) = _scatter_parts([part_small], "grad_scatter")
    small_sum = _sum_parts(r_small, "small_sum")
    loss = small_sum[LOSS_ROW, 0]
    (rep_sum,) = _all_gather([small_sum], "replicated_gather")

    res = {"e_w_in": _sum_parts_adamw(r_e_in, e_w_in, m_e_w_in, v_e_w_in, "adamw_e_w_in"),
           "o_w_in": _sum_parts_adamw(r_o_in, o_w_in, m_o_w_in, v_o_w_in, "adamw_o_w_in"),
           "e_w_out": _sum_parts_adamw(r_e_out, e_w_out, m_e_w_out, v_e_w_out, "adamw_e_w_out"),
           "o_w_out": _sum_parts_adamw(r_o_out, o_w_out, m_o_w_out, v_o_w_out, "adamw_o_w_out"),
           "mla_w_uq": _sum_parts_adamw_whole(r_uq, mla_w_uq, m_mla_w_uq, v_mla_w_uq, "adamw_w_uq"),
           "pool_w": _sum_parts_adamw_whole(r_pw, pool_w, m_pool_w, v_pool_w, "adamw_pool_w")}
    grads = _unpack_replicated(rep_sum, [(n, w_in[n].size, w_in[n].shape) for n in REPLICATED])
    grads["mla_q_norm_g"] = small_sum[QNORM_ROW:QNORM_ROW + 1, :32]
    small_names = list(grads)
    deltas, new_ms, new_vs = _adamw_many([grads[n] for n in small_names], [w_in[n] for n in small_names],
                                         [m_in[n] for n in small_names], [v_in[n] for n in small_names], "small_adamw")
    for k, n in enumerate(small_names):
        res[n] = [grads[n], deltas[k], new_ms[k], new_vs[k]]
    dmod_all = r_small[:, ADA_ROW, :2 * ada_cols].reshape(N_DEV, 2, ada_cols).transpose(1, 0, 2)
    dmod_rows = jnp.pad(dmod_all[:, :, None, :], ((0, 0), (0, 0), (0, 7), (0, 0))).reshape(2, 8 * N_DEV, ada_cols)
    res["ada_w"] = _ada_w_grad_adamw(c_all, dmod_rows, ada_w, m_ada_w, v_ada_w, "ada_w_adamw")

    return (loss, grad_x, *[res[n][0] for n in names], *[res[n][1] for n in names],
            *[res[n][2] for n in names], *[res[n][3] for n in names])
```

```python
import functools

import jax
import jax.numpy as jnp
from jax import lax
from jax.experimental import pallas as pl
from jax.experimental.pallas import tpu as pltpu

F32 = jnp.float32
BF16 = jnp.bfloat16

D_MODEL = 1024
CHUNK = 64
LN_EPS = 1e-5
GMLP_HEADS = 4
GMLP_HEAD_DIM = 256
GMLP_BLOCK = 128
POOL_WINDOWS = (2, 4, 8, 16)
POOL_GROUP_DIM = 256
POOL_HALO = 16
MLA_HEADS = 16
MLA_NOPE = 128
MLA_ROPE = 64
MLA_Q_RANK = 256
MLA_KV_RANK = 128
MLA_WIDTH = 2048
ODD_IN = 2496
ODD_IN_PAD = 2560
ROPE_THETA = 10000.0
ATTN_SCALE = (MLA_NOPE + MLA_ROPE) ** -0.5
ATTN_SCALE_LOG2 = ATTN_SCALE * 1.4426950408889634
DEEPNORM_ALPHA = 4.0 ** 0.25
ADAM_LR, ADAM_B1, ADAM_B2, ADAM_EPS, ADAM_WD, ADAM_STEP = 0.001, 0.9, 0.999, 1e-8, 0.01, 10
N_DEV = 8
LANES = 1024
VMEM_LIMIT = 56 * 1024 * 1024
MESH = pl.DeviceIdType.MESH

NT = (((1,), (1,)), ((), ()))
NN = (((1,), (0,)), ((), ()))
TN = (((0,), (0,)), ((), ()))


def _params(n_axes):
    return pltpu.CompilerParams(dimension_semantics=("arbitrary",) * n_axes, vmem_limit_bytes=VMEM_LIMIT)


def _dot(a, b, dn):
    return lax.dot_general(a.astype(BF16), b.astype(BF16), dn, preferred_element_type=F32)


def _sigmoid(z):
    return 1.0 / (1.0 + jnp.exp(-z))


def _colsum(t):
    return jnp.sum(t, axis=0, keepdims=True)


EXCHANGE_RELATIONS = {"devices": tuple(range(1, N_DEV)), "chips": (2, 4, 6)}


def _exchange_copies(g_refs, r_refs, send_sems, recv_sems, local_sems, kind):
    x, y, c, me = _place()
    n_arr = len(g_refs)

    def slot(lin):
        return lin // 2 if kind == "chips" else lin

    own = [pltpu.make_async_copy(g_refs[t].at[slot(me)], r_refs[t].at[slot(me)], local_sems.at[t]) for t in range(n_arr)]
    sends, recvs = [], []
    for n, r in enumerate(EXCHANGE_RELATIONS[kind]):
        peer, lin = _flip(x, y, c, r)
        for t in range(n_arr):
            k = n_arr * n + t
            sends.append(pltpu.make_async_remote_copy(
                src_ref=g_refs[t].at[slot(lin)], dst_ref=r_refs[t].at[slot(me)], send_sem=send_sems.at[k],
                recv_sem=recv_sems.at[k], device_id=peer, device_id_type=MESH))
            recvs.append(pltpu.make_async_remote_copy(
                src_ref=g_refs[t].at[slot(lin)], dst_ref=r_refs[t].at[slot(lin)], send_sem=send_sems.at[k],
                recv_sem=recv_sems.at[k], device_id=(x, y, c), device_id_type=MESH))
    return own, sends, recvs


def _exchange_start(copies):
    own, sends, _ = copies
    for cp in own + sends:
        cp.start()


def _exchange_wait(copies):
    own, sends, recvs = copies
    for cp in recvs:
        cp.wait_recv()
    for cp in sends:
        cp.wait_send()
    for cp in own:
        cp.wait()


def _exchange_extras(parts, kind):
    shapes = [jax.ShapeDtypeStruct(p.shape, p.dtype) for p in parts]
    n = len(parts) * len(EXCHANGE_RELATIONS[kind])
    return shapes, [pltpu.SemaphoreType.DMA((n,)), pltpu.SemaphoreType.DMA((n,)), pltpu.SemaphoreType.DMA((len(parts),))]


def _grid_call(body, name, grid, in_specs, out_specs, out_shape, args, scratch=(), side=None):
    if side is None:
        return pl.pallas_call(body, name=name, grid=grid, in_specs=in_specs, out_specs=out_specs,
                              out_shape=out_shape, scratch_shapes=list(scratch),
                              compiler_params=_params(len(grid)))(*args)
    parts, kind = side
    gather = kind == "gather"
    n_in, n_out, n_sc, n_arr = len(args), len(out_shape), len(scratch), len(parts)
    side_shapes, side_sems = _gather_extras(parts) if gather else _exchange_extras(parts, kind)
    mid = tuple(g // 2 for g in grid)

    def wrapped(*refs):
        ins, g_refs = refs[:n_in], refs[n_in:n_in + n_arr]
        outs = refs[n_in + n_arr:n_in + n_arr + n_out]
        r_refs = refs[n_in + n_arr + n_out:n_in + 2 * n_arr + n_out]
        sc = refs[n_in + 2 * n_arr + n_out:n_in + 2 * n_arr + n_out + n_sc]
        ids = [pl.program_id(a) for a in range(len(grid))]

        def at(step):
            return functools.reduce(jnp.logical_and, [i == s for i, s in zip(ids, step)])

        first, last = at((0,) * len(grid)), at(tuple(g - 1 for g in grid))
        if gather:
            exchange = _Gather(g_refs, r_refs, *refs[-3:])
            pl.when(first)(exchange.start)
            if mid != (0,) * len(grid):
                pl.when(at(mid))(exchange.forward)
            body(*ins, *outs, *sc)

            @pl.when(last)
            def _():
                if mid == (0,) * len(grid):
                    exchange.forward()
                exchange.end()
        else:
            copies = _exchange_copies(g_refs, r_refs, *refs[-3:], kind)
            pl.when(first)(lambda: _exchange_start(copies))
            body(*ins, *outs, *sc)
            pl.when(last)(lambda: _exchange_wait(copies))

    anyspace = pl.BlockSpec(memory_space=pl.ANY)
    return pl.pallas_call(
        wrapped, name=name, grid=grid, in_specs=list(in_specs) + [anyspace] * n_arr,
        out_specs=list(out_specs) + [anyspace] * n_arr, out_shape=list(out_shape) + side_shapes,
        scratch_shapes=list(scratch) + side_sems, compiler_params=_params(len(grid)),
    )(*args, *parts)


def _matmul(pairs, mode, out_dtype, m, n, tm, tn, name, side=None, n_outer=False):
    dn = {"nn": NN, "nt": NT, "tn": TN}[mode]
    tm, tn = min(tm, m), min(tn, n)
    n_pairs = len(pairs)
    grid = (n // tn, m // tm) if n_outer else (m // tm, n // tn)

    def ij(f):
        return (lambda j, i: f(i, j)) if n_outer else f

    def body(*refs):
        o_ref = refs[-1]
        acc = None
        for p in range(n_pairs):
            t = _dot(refs[2 * p][...], refs[2 * p + 1][...], dn)
            acc = t if acc is None else acc + t
        o_ref[...] = acc.astype(o_ref.dtype)

    in_specs, args = [], []
    for a, b in pairs:
        if mode == "nn":
            k = a.shape[1]
            in_specs += [pl.BlockSpec((tm, k), ij(lambda i, j: (i, 0))), pl.BlockSpec((k, tn), ij(lambda i, j: (0, j)))]
        elif mode == "nt":
            k = a.shape[1]
            in_specs += [pl.BlockSpec((tm, k), ij(lambda i, j: (i, 0))), pl.BlockSpec((tn, k), ij(lambda i, j: (j, 0)))]
        else:
            k = a.shape[0]
            in_specs += [pl.BlockSpec((k, tm), ij(lambda i, j: (0, i))), pl.BlockSpec((k, tn), ij(lambda i, j: (0, j)))]
        args += [a, b]
    return _grid_call(body, name, grid, in_specs, [pl.BlockSpec((tm, tn), ij(lambda i, j: (i, j)))],
                      [jax.ShapeDtypeStruct((m, n), out_dtype)], args, side=side)


def _matmul_cols_nn(a, w3, out_dtype, tm, name, side=None):
    m, k = a.shape
    _, _, n = w3.shape
    tm = min(tm, m)

    def body(a_ref, w_ref, o_ref):
        av = a_ref[...]
        for j in range(N_DEV):
            o_ref[:, n * j:n * (j + 1)] = _dot(av, w_ref[j], NN).astype(o_ref.dtype)

    return _grid_call(
        body, name, (m // tm,),
        [pl.BlockSpec((tm, k), lambda i: (i, 0)), pl.BlockSpec((N_DEV, k, n), lambda i: (0, 0, 0))],
        [pl.BlockSpec((tm, N_DEV * n), lambda i: (i, 0))], [jax.ShapeDtypeStruct((m, N_DEV * n), out_dtype)], [a, w3],
        side=side)


def _matmul_cols_tn(a, b, n, out_dtype, tk, name, side=None):
    m, k = a.shape
    tk = min(tk, k)

    def body(a_ref, b_ref, o_ref):
        o_ref[...] = _dot(a_ref[...], b_ref[...], TN).astype(o_ref.dtype)

    return _grid_call(
        body, name, (N_DEV, k // tk),
        [pl.BlockSpec((m, tk), lambda j, i: (0, i)), pl.BlockSpec((m, n), lambda j, i: (0, j))],
        [pl.BlockSpec((None, tk, n), lambda j, i: (j, i, 0))], [jax.ShapeDtypeStruct((N_DEV, k, n), out_dtype)], [a, b],
        side=side)


def _rows3(tm, d):
    return pl.BlockSpec((None, tm, d), lambda i: (0, i, 0))


def _modulate(x, mod, name):
    _, s, d = x.shape
    tm = min(s, 512)

    def body(x_ref, m_ref, h_ref):
        shift, scale = m_ref[0:1, 0:d], m_ref[0:1, d:2 * d]
        h_ref[...] = (x_ref[...] * (1.0 + scale) + shift).astype(BF16)

    return pl.pallas_call(
        body, name=name, grid=(s // tm,),
        in_specs=[_rows3(tm, d), pl.BlockSpec((8, 3 * d), lambda i: (0, 0))],
        out_specs=pl.BlockSpec((tm, d), lambda i: (i, 0)),
        out_shape=jax.ShapeDtypeStruct((s, d), BF16), compiler_params=_params(1),
    )(x, mod)


def _ln_stats(r):
    mu = jnp.mean(r, axis=-1, keepdims=True)
    rc = r - mu
    var = jnp.mean(rc * rc, axis=-1, keepdims=True)
    rstd = lax.rsqrt(var + LN_EPS)
    return rc * rstd, rstd


def _ln_bwd(dxhat, xhat, rstd):
    return rstd * (dxhat - jnp.mean(dxhat, axis=-1, keepdims=True)
                   - xhat * jnp.mean(dxhat * xhat, axis=-1, keepdims=True))


def _resid_ln(x, y, mod, g, b, mod_next, name):
    _, s, d = x.shape
    tm = min(s, 512)

    def body(x_ref, y_ref, m_ref, g_ref, b_ref, mn_ref, o_ref, h_ref):
        gate = m_ref[0:1, 2 * d:3 * d]
        xhat, _ = _ln_stats(DEEPNORM_ALPHA * x_ref[...] + (1.0 + gate) * y_ref[...])
        out = xhat * g_ref[...] + b_ref[...]
        o_ref[...] = out
        h_ref[...] = (out * (1.0 + mn_ref[0:1, d:2 * d]) + mn_ref[0:1, 0:d]).astype(BF16)

    row = pl.BlockSpec((tm, d), lambda i: (i, 0))
    vec = pl.BlockSpec((1, d), lambda i: (0, 0))
    modspec = pl.BlockSpec((8, 3 * d), lambda i: (0, 0))
    return pl.pallas_call(
        body, name=name, grid=(s // tm,),
        in_specs=[_rows3(tm, d), row, modspec, vec, vec, modspec],
        out_specs=[_rows3(tm, d), row],
        out_shape=[jax.ShapeDtypeStruct((1, s, d), F32), jax.ShapeDtypeStruct((s, d), BF16)],
        compiler_params=_params(1),
    )(x, y, mod, g, b, mod_next)


def _final_ln_loss_bwd(x, y, mod, g, b, target, name):
    _, s, d = x.shape
    tm = min(s, 512)

    def body(x_ref, y_ref, m_ref, g_ref, b_ref, t_ref, dy_ref, dx_ref, red_ref):
        @pl.when(pl.program_id(0) == 0)
        def _():
            red_ref[...] = jnp.zeros_like(red_ref)

        gate = m_ref[0:1, 2 * d:3 * d]
        yv = y_ref[...]
        xhat, rstd = _ln_stats(DEEPNORM_ALPHA * x_ref[...] + (1.0 + gate) * yv)
        err = xhat * g_ref[...] + b_ref[...] - t_ref[...]
        dout = err * (1.0 / d)
        dr = _ln_bwd(dout * g_ref[...], xhat, rstd)
        dy_ref[...] = ((1.0 + gate) * dr).astype(BF16)
        dx_ref[...] = DEEPNORM_ALPHA * dr
        red_ref[0:1, :] += _colsum(dout * xhat)
        red_ref[1:2, :] += _colsum(dout)
        red_ref[2:3, :] += _colsum(dr * yv)
        red_ref[3:4, :] += _colsum(err * err)

    row = pl.BlockSpec((tm, d), lambda i: (i, 0))
    vec = pl.BlockSpec((1, d), lambda i: (0, 0))
    return pl.pallas_call(
        body, name=name, grid=(s // tm,),
        in_specs=[_rows3(tm, d), row, pl.BlockSpec((8, 3 * d), lambda i: (0, 0)), vec, vec, _rows3(tm, d)],
        out_specs=[row, row, pl.BlockSpec((8, d), lambda i: (0, 0))],
        out_shape=[jax.ShapeDtypeStruct((s, d), BF16), jax.ShapeDtypeStruct((s, d), F32),
                   jax.ShapeDtypeStruct((8, d), F32)],
        compiler_params=_params(1),
    )(x, y, mod, g, b, target)


def _mid_bwd(dh, dxres, x, y, mod_lo, mod_hi, g, b, name):
    _, s, d = x.shape
    tm = min(s, 512)

    def body(dh_ref, dxr_ref, x_ref, y_ref, ml_ref, mh_ref, g_ref, b_ref, dy_ref, dx_ref, red_ref):
        @pl.when(pl.program_id(0) == 0)
        def _():
            red_ref[...] = jnp.zeros_like(red_ref)

        gate = ml_ref[0:1, 2 * d:3 * d]
        scale_hi = mh_ref[0:1, d:2 * d]
        yv, dhv = y_ref[...], dh_ref[...]
        xhat, rstd = _ln_stats(DEEPNORM_ALPHA * x_ref[...] + (1.0 + gate) * yv)
        x_mid = xhat * g_ref[...] + b_ref[...]
        dx_mid = dxr_ref[...] + dhv * (1.0 + scale_hi)
        dr = _ln_bwd(dx_mid * g_ref[...], xhat, rstd)
        dy_ref[...] = ((1.0 + gate) * dr).astype(BF16)
        dx_ref[...] = DEEPNORM_ALPHA * dr
        red_ref[0:1, :] += _colsum(dhv * x_mid)
        red_ref[1:2, :] += _colsum(dhv)
        red_ref[2:3, :] += _colsum(dx_mid * xhat)
        red_ref[3:4, :] += _colsum(dx_mid)
        red_ref[4:5, :] += _colsum(dr * yv)

    row = pl.BlockSpec((tm, d), lambda i: (i, 0))
    vec = pl.BlockSpec((1, d), lambda i: (0, 0))
    modspec = pl.BlockSpec((8, 3 * d), lambda i: (0, 0))
    return pl.pallas_call(
        body, name=name, grid=(s // tm,),
        in_specs=[row, row, _rows3(tm, d), row, modspec, modspec, vec, vec],
        out_specs=[row, row, pl.BlockSpec((8, d), lambda i: (0, 0))],
        out_shape=[jax.ShapeDtypeStruct((s, d), BF16), jax.ShapeDtypeStruct((s, d), F32),
                   jax.ShapeDtypeStruct((8, d), F32)],
        compiler_params=_params(1),
    )(dh, dxres, x, y, mod_lo, mod_hi, g, b)


def _first_bwd(dproj, w3, dxres, x, mod, name, side=None):
    _, s, d = x.shape
    n = w3.shape[2]
    tm = min(s, 512)

    def body(a_ref, w_ref, dxr_ref, x_ref, m_ref, gx_ref, red_ref):
        @pl.when(pl.program_id(0) == 0)
        def _():
            red_ref[...] = jnp.zeros_like(red_ref)

        dhv = _dot(a_ref[:, 0:n], w_ref[0], NT)
        for j in range(1, N_DEV):
            dhv = dhv + _dot(a_ref[:, n * j:n * (j + 1)], w_ref[j], NT)
        gx_ref[...] = dxr_ref[...] + dhv * (1.0 + m_ref[0:1, d:2 * d])
        red_ref[0:1, :] += _colsum(dhv * x_ref[...])
        red_ref[1:2, :] += _colsum(dhv)

    return _grid_call(
        body, name, (s // tm,),
        [pl.BlockSpec((tm, N_DEV * n), lambda i: (i, 0)), pl.BlockSpec((N_DEV, d, n), lambda i: (0, 0, 0)),
         pl.BlockSpec((tm, d), lambda i: (i, 0)), _rows3(tm, d), pl.BlockSpec((8, 3 * d), lambda i: (0, 0))],
        [_rows3(tm, d), pl.BlockSpec((8, d), lambda i: (0, 0))],
        [jax.ShapeDtypeStruct((1, s, d), F32), jax.ShapeDtypeStruct((8, d), F32)],
        [dproj, w3, dxres, x, mod], side=side)


EVEN_TM = 512


def _gmlp_mask():
    t = lax.broadcasted_iota(jnp.int32, (GMLP_BLOCK, GMLP_BLOCK), 0) // CHUNK
    s = lax.broadcasted_iota(jnp.int32, (GMLP_BLOCK, GMLP_BLOCK), 1) // CHUNK
    return s <= t


def _window_sum(ext, win, back):
    n = ext.shape[0]
    k = 1
    while k < win:
        ext = ext + pltpu.roll(ext, k if back else n - k, 0)
        k *= 2
    return ext


def _inv_count(row0, rows, win):
    t = row0 + lax.broadcasted_iota(jnp.int32, (rows, 1), 0)
    return t, 1.0 / jnp.minimum(t + 1, win).astype(F32)


def _pooled(xb, halo, row0, win):
    tm = xb.shape[0]
    sums = _window_sum(jnp.concatenate([halo, xb], axis=0), win, True)[POOL_HALO:]
    _, inv = _inv_count(row0, tm, win)
    return sums * inv - xb


def _even_fwd(proj, ws, bs_col, ng, nb, pw, pb, ps, name, side=None):
    s = proj.shape[0]
    tm = min(s, EVEN_TM)
    hd, gd = GMLP_HEAD_DIM, POOL_GROUP_DIM

    def body(p_ref, halo_ref, ws_ref, bs_ref, ng_ref, nb_ref, pw_ref, pb_ref, ps_ref, m_ref):
        i = pl.program_id(0)
        mask = _gmlp_mask()
        for h in range(GMLP_HEADS):
            wm = jnp.where(mask, ws_ref[h], 0.0).astype(BF16)
            for blk in range(tm // GMLP_BLOCK):
                rows = slice(blk * GMLP_BLOCK, (blk + 1) * GMLP_BLOCK)
                cu, cv, cz = h * hd, 1024 + h * hd, 2048 + h * hd
                vhat, _ = _ln_stats(p_ref[rows, cv:cv + hd].astype(F32))
                vn = vhat * ng_ref[...] + nb_ref[...]
                sv = _dot(wm, vn, NN) + bs_ref[h]
                za = p_ref[rows, cz:cz + hd].astype(F32)
                m_ref[rows, cu:cu + hd] = (p_ref[rows, cu:cu + hd].astype(F32) * sv * (za * _sigmoid(za))).astype(BF16)
        for g, win in enumerate(POOL_WINDOWS):
            cx, cz = 3072 + g * gd, 4096 + g * gd
            halo = jnp.where(i > 0, halo_ref[:, g * gd:(g + 1) * gd].astype(F32), 0.0)
            pooled = _pooled(p_ref[:, cx:cx + gd].astype(F32), halo, i * tm, win)
            yb = _dot(pooled, pw_ref[g], NN) + pb_ref[:, g * gd:(g + 1) * gd]
            zb = p_ref[:, cz:cz + gd].astype(F32)
            m_ref[:, 1024 + g * gd:1024 + (g + 1) * gd] = (
                yb * ps_ref[:, g * gd:(g + 1) * gd] * (zb * _sigmoid(zb))).astype(BF16)

    hb = tm // POOL_HALO
    return _grid_call(
        body, name, (s // tm,),
        [
            pl.BlockSpec((tm, 5120), lambda i: (i, 0)),
            pl.BlockSpec((POOL_HALO, 1024), lambda i: (jnp.maximum(i * hb - 1, 0), 3)),
            pl.BlockSpec((GMLP_HEADS, GMLP_BLOCK, GMLP_BLOCK), lambda i: (0, 0, 0)),
            pl.BlockSpec((GMLP_HEADS, GMLP_BLOCK, 1), lambda i: (0, 0, 0)),
            pl.BlockSpec((1, hd), lambda i: (0, 0)), pl.BlockSpec((1, hd), lambda i: (0, 0)),
            pl.BlockSpec((4, gd, gd), lambda i: (0, 0, 0)),
            pl.BlockSpec((1, 1024), lambda i: (0, 0)), pl.BlockSpec((1, 1024), lambda i: (0, 0)),
        ],
        [pl.BlockSpec((tm, 2048), lambda i: (i, 0))], [jax.ShapeDtypeStruct((s, 2048), BF16)],
        [proj, proj, ws, bs_col, ng, nb, pw, pb, ps], side=side)


def _even_bwd(proj, dm, ws, bs_col, ng, nb, pw, pb, ps, name, side=None):
    s = proj.shape[0]
    tm = min(s, EVEN_TM)
    hd, gd = GMLP_HEAD_DIM, POOL_GROUP_DIM
    n_tiles = s // tm

    def body(p_ref, halo_ref, zbn_ref, dm_ref, dbn_ref, ws_ref, bs_ref, ng_ref, nb_ref, pw_ref, pb_ref, ps_ref,
             dp_ref, dws_ref, dbs_ref, dng_ref, dnb_ref, dpw_ref, dpb_ref, dps_ref):
        i = pl.program_id(0)

        @pl.when(i == 0)
        def _():
            for r in (dws_ref, dbs_ref, dng_ref, dnb_ref, dpw_ref, dpb_ref, dps_ref):
                r[...] = jnp.zeros_like(r)

        mask = _gmlp_mask()
        for h in range(GMLP_HEADS):
            wm = jnp.where(mask, ws_ref[h], 0.0).astype(BF16)
            for blk in range(tm // GMLP_BLOCK):
                rows = slice(blk * GMLP_BLOCK, (blk + 1) * GMLP_BLOCK)
                cu, cv, cz = h * hd, 1024 + h * hd, 2048 + h * hd
                vhat, rstd = _ln_stats(p_ref[rows, cv:cv + hd].astype(F32))
                vn = (vhat * ng_ref[...] + nb_ref[...]).astype(BF16)
                sv = _dot(wm, vn, NN) + bs_ref[h]
                u, za = p_ref[rows, cu:cu + hd].astype(F32), p_ref[rows, cz:cz + hd].astype(F32)
                da = dm_ref[rows, cu:cu + hd].astype(F32)
                sig = _sigmoid(za)
                sa = za * sig
                dau = da * u
                dsv = dau * sa
                dp_ref[rows, cu:cu + hd] = (da * sv * sa).astype(BF16)
                dp_ref[rows, cz:cz + hd] = (dau * sv * (sig * (1.0 + za * (1.0 - sig)))).astype(BF16)
                dsv_b = dsv.astype(BF16)
                dbs_ref[h] += jnp.sum(dsv, axis=1, keepdims=True)
                dws_ref[h] += _dot(dsv_b, vn, NT)
                dvn = _dot(wm, dsv_b, TN)
                dng_ref[...] += _colsum(dvn * vhat)
                dnb_ref[...] += _colsum(dvn)
                dp_ref[rows, cv:cv + hd] = _ln_bwd(dvn * ng_ref[...], vhat, rstd).astype(BF16)

        row0 = i * tm
        for g, win in enumerate(POOL_WINDOWS):
            cx, cz, cd = 3072 + g * gd, 4096 + g * gd, 1024 + g * gd
            gs = slice(g * gd, (g + 1) * gd)
            halo = jnp.where(i > 0, halo_ref[:, gs].astype(F32), 0.0)
            xb = p_ref[:, cx:cx + gd].astype(F32)
            pooled = _pooled(xb, halo, row0, win).astype(BF16)
            scale_g = ps_ref[:, gs]
            yb = _dot(pooled, pw_ref[g], NN) + pb_ref[:, gs]
            zb, db = p_ref[:, cz:cz + gd].astype(F32), dm_ref[:, cd:cd + gd].astype(F32)
            sig = _sigmoid(zb)
            dyp = db * (zb * sig)
            dp_ref[:, cz:cz + gd] = (db * yb * scale_g * (sig * (1.0 + zb * (1.0 - sig)))).astype(BF16)
            dps_ref[:, gs] += _colsum(dyp * yb)
            dpb_ref[:, gs] += _colsum(dyp * scale_g)
            zb_ext = jnp.concatenate([zb, zbn_ref[:, gs].astype(F32)], axis=0)
            db_ext = jnp.concatenate([db, dbn_ref[:, gs].astype(F32)], axis=0)
            dy_ext = (db_ext * (zb_ext * _sigmoid(zb_ext)) * scale_g).astype(BF16)
            dpw_ref[g] += _dot(pooled, dy_ext[:tm], TN)
            dpooled = _dot(dy_ext, pw_ref[g], NT)
            t, inv = _inv_count(row0, tm + POOL_HALO, win)
            w_ext = jnp.where(t < s, dpooled * inv, 0.0)
            dp_ref[:, cx:cx + gd] = (_window_sum(w_ext, win, False)[:tm] - dpooled[:tm]).astype(BF16)

    hb = tm // POOL_HALO
    last = s // POOL_HALO - 1
    small = lambda shape: pl.BlockSpec(shape, lambda i: (0,) * len(shape))
    return _grid_call(
        body, name, (n_tiles,),
        [
            pl.BlockSpec((tm, 5120), lambda i: (i, 0)),
            pl.BlockSpec((POOL_HALO, 1024), lambda i: (jnp.maximum(i * hb - 1, 0), 3)),
            pl.BlockSpec((POOL_HALO, 1024), lambda i: (jnp.minimum((i + 1) * hb, last), 4)),
            pl.BlockSpec((tm, 2048), lambda i: (i, 0)),
            pl.BlockSpec((POOL_HALO, 1024), lambda i: (jnp.minimum((i + 1) * hb, last), 1)),
            small((GMLP_HEADS, GMLP_BLOCK, GMLP_BLOCK)), small((GMLP_HEADS, GMLP_BLOCK, 1)),
            small((1, hd)), small((1, hd)), small((4, gd, gd)), small((1, 1024)), small((1, 1024)),
        ],
        [
            pl.BlockSpec((tm, 5120), lambda i: (i, 0)),
            small((GMLP_HEADS, GMLP_BLOCK, GMLP_BLOCK)), small((GMLP_HEADS, GMLP_BLOCK, 1)),
            small((1, hd)), small((1, hd)), small((4, gd, gd)), small((1, 1024)), small((1, 1024)),
        ],
        [
            jax.ShapeDtypeStruct((s, 5120), BF16),
            jax.ShapeDtypeStruct((GMLP_HEADS, GMLP_BLOCK, GMLP_BLOCK), F32),
            jax.ShapeDtypeStruct((GMLP_HEADS, GMLP_BLOCK, 1), F32),
            jax.ShapeDtypeStruct((1, hd), F32), jax.ShapeDtypeStruct((1, hd), F32),
            jax.ShapeDtypeStruct((4, gd, gd), F32),
            jax.ShapeDtypeStruct((1, 1024), F32), jax.ShapeDtypeStruct((1, 1024), F32),
        ],
        [proj, proj, proj, dm, dm, ws, bs_col, ng, nb, pw, pb, ps], side=side)


def _rope_pair_swap(t):
    lane = lax.broadcasted_iota(jnp.int32, t.shape, 1)
    return jnp.where(lane % 64 < 32, pltpu.roll(t, 96, 1), pltpu.roll(t, 32, 1))


def _rms(x, g):
    r = lax.rsqrt(jnp.mean(x * x, axis=-1, keepdims=True) + LN_EPS)
    return x * r, r


def _rms_bwd(dy, g, xhat, r):
    dyg = dy * g
    return r * (dyg - xhat * jnp.mean(dyg * xhat, axis=-1, keepdims=True))


def _lane_lt(shape, n):
    return lax.broadcasted_iota(jnp.int32, shape, 1) < n


def _mla_prep(proj, cosp, sinp, gq, gkv, name):
    s = proj.shape[0]
    tm = min(s, 512)

    def body(qc_ref, kv_ref, c_ref, s_ref, gq_ref, gkv_ref, qn_ref, kp_ref):
        qhat, _ = _rms(qc_ref[...].astype(F32), None)
        qn_ref[...] = (qhat * gq_ref[...]).astype(BF16)
        khat, _ = _rms(kv_ref[:, 0:128].astype(F32), None)
        kp_ref[:, 0:128] = (khat * gkv_ref[...]).astype(BF16)
        kr = kv_ref[:, 128:256].astype(F32)
        kp_ref[:, 128:256] = (kr * c_ref[...] + _rope_pair_swap(kr) * s_ref[...]).astype(BF16)

    return pl.pallas_call(
        body, name=name, grid=(s // tm,),
        in_specs=[pl.BlockSpec((tm, 256), lambda i: (i, 0)), pl.BlockSpec((tm, 256), lambda i: (i, 1)),
                  pl.BlockSpec((tm, 128), lambda i: (i, 0)), pl.BlockSpec((tm, 128), lambda i: (i, 0)),
                  pl.BlockSpec((1, 256), lambda i: (0, 0)), pl.BlockSpec((1, 128), lambda i: (0, 0))],
        out_specs=[pl.BlockSpec((tm, 256), lambda i: (i, 0)), pl.BlockSpec((tm, 256), lambda i: (i, 0))],
        out_shape=[jax.ShapeDtypeStruct((s, 256), BF16), jax.ShapeDtypeStruct((s, 256), BF16)],
        compiler_params=_params(1),
    )(proj, proj, cosp, sinp, gq, gkv)


def _mla_prep_bwd(proj, dqn, dkp, dv, cosp, sinp, gq, gkv, dproj, name):
    s = proj.shape[0]
    tm = min(s, 512)

    def body(qc_ref, kv_ref, dqn_ref, dkp_ref, dv_ref, c_ref, s_ref, gq_ref, gkv_ref, dproj_ref, o_ref, red_ref):
        @pl.when(pl.program_id(0) == 0)
        def _():
            red_ref[...] = jnp.zeros_like(red_ref)

        qhat, qr = _rms(qc_ref[...].astype(F32), None)
        dq = dqn_ref[...]
        o_ref[:, 0:256] = _rms_bwd(dq, gq_ref[...], qhat, qr).astype(BF16)
        red_ref[0:1, :] += _colsum(dq * qhat)
        khat, kr = _rms(kv_ref[:, 0:128].astype(F32), None)
        dk = dkp_ref[:, 0:128] + dv_ref[...]
        o_ref[:, 256:384] = _rms_bwd(dk, gkv_ref[...], khat, kr).astype(BF16)
        red_ref[1:2, 0:128] += _colsum(dk * khat)
        dr = dkp_ref[:, 128:256]
        o_ref[:, 384:512] = (dr * c_ref[...] - _rope_pair_swap(dr) * s_ref[...]).astype(BF16)

    return pl.pallas_call(
        body, name=name, grid=(s // tm,),
        in_specs=[pl.BlockSpec((tm, 256), lambda i: (i, 0)), pl.BlockSpec((tm, 256), lambda i: (i, 1)),
                  pl.BlockSpec((tm, 256), lambda i: (i, 0)), pl.BlockSpec((tm, 256), lambda i: (i, 0)),
                  pl.BlockSpec((tm, 128), lambda i: (i, 0)),
                  pl.BlockSpec((tm, 128), lambda i: (i, 0)), pl.BlockSpec((tm, 128), lambda i: (i, 0)),
                  pl.BlockSpec((1, 256), lambda i: (0, 0)), pl.BlockSpec((1, 128), lambda i: (0, 0)),
                  pl.BlockSpec(memory_space=pl.ANY)],
        out_specs=[pl.BlockSpec((tm, 512), lambda i: (i, 0)), pl.BlockSpec((8, 256), lambda i: (0, 0))],
        out_shape=[jax.ShapeDtypeStruct(dproj.shape, BF16), jax.ShapeDtypeStruct((8, 256), F32)],
        input_output_aliases={9: 0}, compiler_params=_params(1),
    )(proj, proj, dqn, dkp, dv, cosp, sinp, gq, gkv, dproj)


HEADS_TM = 512
Z_COL0 = ODD_IN_PAD - MLA_WIDTH


def _head_cols(h):
    return slice(128 * h, 128 * h + 128)


def _q_heads(q_up, cosp, sinp, wuk, name):
    s = q_up.shape[0]
    tm = min(s, HEADS_TM)

    def body(q_ref, c_ref, s_ref, w_ref, o_ref):
        for p in range(MLA_HEADS // 2):
            raw = q_ref[:, MLA_WIDTH + 128 * p:MLA_WIDTH + 128 * (p + 1)].astype(F32)
            rot = raw * c_ref[...] + _rope_pair_swap(raw) * s_ref[...]
            low = _lane_lt(rot.shape, 64)
            o_ref[2 * p, :, 128:256] = jnp.where(low, rot, 0.0).astype(BF16)
            o_ref[2 * p + 1, :, 128:256] = jnp.where(low, pltpu.roll(rot, 64, 1), 0.0).astype(BF16)
        for h in range(MLA_HEADS):
            o_ref[h, :, 0:128] = _dot(q_ref[:, _head_cols(h)], w_ref[:, _head_cols(h)], NT).astype(BF16)

    return pl.pallas_call(
        body, name=name, grid=(s // tm,),
        in_specs=[pl.BlockSpec((tm, 3072), lambda i: (i, 0)),
                  pl.BlockSpec((tm, 128), lambda i: (i, 0)), pl.BlockSpec((tm, 128), lambda i: (i, 0)),
                  pl.BlockSpec((128, MLA_WIDTH), lambda i: (0, 0))],
        out_specs=pl.BlockSpec((MLA_HEADS, tm, 256), lambda i: (0, i, 0)),
        out_shape=jax.ShapeDtypeStruct((MLA_HEADS, s, 256), BF16), compiler_params=_params(1),
    )(q_up, cosp, sinp, wuk)


def _q_heads_bwd(dqp, q_up, cosp, sinp, wuk, name):
    s = q_up.shape[0]
    tm = min(s, HEADS_TM)

    def body(dq_ref, qn_ref, c_ref, s_ref, w_ref, dn_ref, dr_ref, dw_ref):
        @pl.when(pl.program_id(0) == 0)
        def _():
            dw_ref[...] = jnp.zeros_like(dw_ref)

        for h in range(MLA_HEADS):
            dlat = dq_ref[h, :, 0:128]
            dn_ref[:, _head_cols(h)] = _dot(dlat, w_ref[:, _head_cols(h)], NN).astype(BF16)
            dw_ref[:, _head_cols(h)] += _dot(dlat, qn_ref[:, _head_cols(h)], TN)
        for p in range(MLA_HEADS // 2):
            drot = dq_ref[2 * p, :, 128:256].astype(F32) + pltpu.roll(dq_ref[2 * p + 1, :, 128:256].astype(F32), 64, 1)
            dr_ref[:, _head_cols(p)] = (drot * c_ref[...] - _rope_pair_swap(drot) * s_ref[...]).astype(BF16)

    return pl.pallas_call(
        body, name=name, grid=(s // tm,),
        in_specs=[pl.BlockSpec((MLA_HEADS, tm, 256), lambda i: (0, i, 0)),
                  pl.BlockSpec((tm, MLA_WIDTH), lambda i: (i, 0)),
                  pl.BlockSpec((tm, 128), lambda i: (i, 0)), pl.BlockSpec((tm, 128), lambda i: (i, 0)),
                  pl.BlockSpec((128, MLA_WIDTH), lambda i: (0, 0))],
        out_specs=[pl.BlockSpec((tm, MLA_WIDTH), lambda i: (i, 0)),
                   pl.BlockSpec((tm, 1024), lambda i: (i, 0)),
                   pl.BlockSpec((128, MLA_WIDTH), lambda i: (0, 0))],
        out_shape=[jax.ShapeDtypeStruct((s, MLA_WIDTH), BF16), jax.ShapeDtypeStruct((s, 1024), BF16),
                   jax.ShapeDtypeStruct((128, MLA_WIDTH), F32)],
        compiler_params=_params(1),
    )(dqp, q_up, cosp, sinp, wuk)


def _o_gate(o_lat, proj, wuv, name):
    s = o_lat.shape[1]
    tm = min(s, HEADS_TM)

    def body(ol_ref, p_ref, w_ref, g_ref):
        for h in range(MLA_HEADS):
            z = p_ref[:, Z_COL0 + 128 * h:Z_COL0 + 128 * (h + 1)].astype(F32)
            g_ref[:, _head_cols(h)] = (_dot(ol_ref[h], w_ref[:, _head_cols(h)], NN) * (z * _sigmoid(z))).astype(BF16)

    return pl.pallas_call(
        body, name=name, grid=(s // tm,),
        in_specs=[pl.BlockSpec((MLA_HEADS, tm, 128), lambda i: (0, i, 0)),
                  pl.BlockSpec((tm, ODD_IN_PAD), lambda i: (i, 0)),
                  pl.BlockSpec((128, MLA_WIDTH), lambda i: (0, 0))],
        out_specs=pl.BlockSpec((tm, MLA_WIDTH), lambda i: (i, 0)),
        out_shape=jax.ShapeDtypeStruct((s, MLA_WIDTH), BF16), compiler_params=_params(1),
    )(o_lat, proj, wuv)


def _o_gate_bwd(dg, o_lat, proj, wuv, name):
    s = o_lat.shape[1]
    tm = min(s, HEADS_TM)

    def body(dg_ref, ol_ref, p_ref, w_ref, dp_ref, dol_ref, dw_ref):
        @pl.when(pl.program_id(0) == 0)
        def _():
            dw_ref[...] = jnp.zeros_like(dw_ref)

        dp_ref[:, 0:Z_COL0] = jnp.zeros((tm, Z_COL0), BF16)
        for h in range(MLA_HEADS):
            zc = slice(Z_COL0 + 128 * h, Z_COL0 + 128 * (h + 1))
            z, dgv, ol = p_ref[:, zc].astype(F32), dg_ref[:, _head_cols(h)].astype(F32), ol_ref[h]
            sig = _sigmoid(z)
            o = _dot(ol, w_ref[:, _head_cols(h)], NN)
            dp_ref[:, zc] = (dgv * o * (sig * (1.0 + z * (1.0 - sig)))).astype(BF16)
            do = (dgv * (z * sig)).astype(BF16)
            dol_ref[h] = _dot(do, w_ref[:, _head_cols(h)], NT).astype(BF16)
            dw_ref[:, _head_cols(h)] += _dot(ol, do, TN)

    return pl.pallas_call(
        body, name=name, grid=(s // tm,),
        in_specs=[pl.BlockSpec((tm, MLA_WIDTH), lambda i: (i, 0)),
                  pl.BlockSpec((MLA_HEADS, tm, 128), lambda i: (0, i, 0)),
                  pl.BlockSpec((tm, ODD_IN_PAD), lambda i: (i, 0)),
                  pl.BlockSpec((128, MLA_WIDTH), lambda i: (0, 0))],
        out_specs=[pl.BlockSpec((tm, ODD_IN_PAD), lambda i: (i, 0)),
                   pl.BlockSpec((MLA_HEADS, tm, 128), lambda i: (0, i, 0)),
                   pl.BlockSpec((128, MLA_WIDTH), lambda i: (0, 0))],
        out_shape=[jax.ShapeDtypeStruct((s, ODD_IN_PAD), BF16), jax.ShapeDtypeStruct((MLA_HEADS, s, 128), BF16),
                   jax.ShapeDtypeStruct((128, MLA_WIDTH), F32)],
        compiler_params=_params(1),
    )(dg, o_lat, proj, wuv)


ATT_TQ = CHUNK
ATT_ROWS = ATT_TQ * MLA_HEADS
ATT_TK = 512
ATT_HEAD_GROUP = 8


def _visible(k0, q_chunk, tk):
    kpos = k0 + lax.broadcasted_iota(jnp.int32, (1, tk), 1)
    return kpos // CHUNK <= q_chunk


def _tile_lanes(t, n):
    return jnp.concatenate([t] * (n // 128), axis=1)


def _key_blocks(i, tk, block, pairs=False):
    visible = i * ATT_TQ + ATT_TQ
    n_full = (visible + tk - 1) // tk - 1

    def full(j):
        block(pl.multiple_of(j * tk, tk), tk, False)

    if pairs:
        def two(jj, carry):
            full(2 * jj)
            full(2 * jj + 1)
            return carry

        lax.fori_loop(0, n_full // 2, two, 0)

        @pl.when(n_full % 2 == 1)
        def _():
            full(n_full - 1)
    else:
        def one(j, carry):
            full(j)
            return carry

        lax.fori_loop(0, n_full, one, 0)
    last0 = pl.multiple_of(n_full * tk, tk)
    half = tk // 2
    if half % 128 == 0:
        @pl.when(visible - n_full * tk <= half)
        def _():
            block(last0, half, True)

        @pl.when(visible - n_full * tk > half)
        def _():
            block(last0, tk, True)
    else:
        block(last0, tk, True)


def _attn_fwd(qp, kp, name, side=None):
    s = kp.shape[0]
    tk = min(ATT_TK, s)

    def body(q_ref, k_ref, o_ref, lse_ref, m_sc, acc_sc):
        i = pl.program_id(0)
        m_sc[...] = jnp.full_like(m_sc, -jnp.inf)
        acc_sc[...] = jnp.zeros_like(acc_sc)

        def block(k0, width, masked):
            k = k_ref[pl.ds(k0, width), :]
            v1 = jnp.where(_lane_lt(k.shape, 128), k, jnp.ones_like(k))
            for h0 in range(0, MLA_HEADS, ATT_HEAD_GROUP):
                rows = slice(h0 * ATT_TQ, (h0 + ATT_HEAD_GROUP) * ATT_TQ)
                q = q_ref[h0:h0 + ATT_HEAD_GROUP].reshape(ATT_HEAD_GROUP * ATT_TQ, 256)
                sc = _dot(q, k, NT) * ATTN_SCALE_LOG2
                if masked:
                    sc = jnp.where(_visible(k0, i, width), sc, -jnp.inf)
                m_prev = m_sc[rows]
                m_new = jnp.maximum(m_prev, jnp.max(sc, axis=1, keepdims=True))
                p = jnp.exp2(sc - _tile_lanes(m_new, width))
                acc_sc[rows] = _tile_lanes(jnp.exp2(m_prev - m_new), 256) * acc_sc[rows] + _dot(p, v1, NN)
                m_sc[rows] = m_new

        _key_blocks(i, tk, block, pairs=True)
        acc = acc_sc[...]
        l = acc[:, 128:256]
        o_ref[...] = (acc[:, 0:128] / l).astype(BF16).reshape(MLA_HEADS, ATT_TQ, 128)
        lse_ref[...] = (m_sc[...] + jnp.log2(l)).reshape(MLA_HEADS, ATT_TQ, 128)

    head128 = pl.BlockSpec((MLA_HEADS, ATT_TQ, 128), lambda i: (0, i, 0))
    return _grid_call(
        body, name, (s // ATT_TQ,),
        [pl.BlockSpec((MLA_HEADS, ATT_TQ, 256), lambda i: (0, i, 0)), pl.BlockSpec((s, 256), lambda i: (0, 0))],
        [head128, head128],
        [jax.ShapeDtypeStruct((MLA_HEADS, s, 128), BF16), jax.ShapeDtypeStruct((MLA_HEADS, s, 128), F32)],
        [qp, kp], scratch=[pltpu.VMEM((ATT_ROWS, 128), F32), pltpu.VMEM((ATT_ROWS, 256), F32)], side=side)


def _attn_bwd(qp, kp, o, do, lse, name, side=None):
    s = kp.shape[0]
    tk = min(ATT_TK, s)

    def body(q_ref, k_ref, o_ref, do_ref, lse_ref, dq_ref, dk_ref, dv_ref, dq_sc):
        i = pl.program_id(0)

        @pl.when(i == 0)
        def _():
            dk_ref[...] = jnp.zeros_like(dk_ref)
            dv_ref[...] = jnp.zeros_like(dv_ref)

        q = q_ref[...].reshape(ATT_ROWS, 256)
        dov = do_ref[...].reshape(ATT_ROWS, 128)
        delta = jnp.sum(dov.astype(F32) * o_ref[...].reshape(ATT_ROWS, 128).astype(F32), axis=1, keepdims=True)
        delta_t = _tile_lanes(jnp.broadcast_to(delta, (ATT_ROWS, 128)), tk)
        lse_t = _tile_lanes(lse_ref[...].reshape(ATT_ROWS, 128), tk)
        dq_sc[...] = jnp.zeros_like(dq_sc)

        def block(k0, width, masked):
            k = k_ref[pl.ds(k0, width), :]
            p = jnp.exp2(_dot(q, k, NT) * ATTN_SCALE_LOG2 - lse_t[:, 0:width])
            if masked:
                p = jnp.where(_visible(k0, i, width), p, 0.0)
            dv_ref[pl.ds(k0, width), :] += _dot(p, dov, TN)
            ds = (p * (_dot(dov, k[:, 0:128], NT) - delta_t[:, 0:width]) * ATTN_SCALE).astype(BF16)
            dq_sc[...] += _dot(ds, k, NN)
            dk_ref[pl.ds(k0, width), :] += _dot(ds, q, TN)

        _key_blocks(i, tk, block, pairs=True)
        dq_ref[...] = dq_sc[...].astype(BF16).reshape(MLA_HEADS, ATT_TQ, 256)

    head128 = pl.BlockSpec((MLA_HEADS, ATT_TQ, 128), lambda i: (0, i, 0))
    head256 = pl.BlockSpec((MLA_HEADS, ATT_TQ, 256), lambda i: (0, i, 0))
    return _grid_call(
        body, name, (s // ATT_TQ,),
        [head256, pl.BlockSpec((s, 256), lambda i: (0, 0)), head128, head128, head128],
        [head256, pl.BlockSpec((s, 256), lambda i: (0, 0)), pl.BlockSpec((s, 128), lambda i: (0, 0))],
        [jax.ShapeDtypeStruct((MLA_HEADS, s, 256), BF16),
         jax.ShapeDtypeStruct((s, 256), F32), jax.ShapeDtypeStruct((s, 128), F32)],
        [qp, kp, o, do, lse], scratch=[pltpu.VMEM((ATT_ROWS, 256), F32)], side=side)


def _place():
    x, y, c = lax.axis_index("x"), lax.axis_index("y"), lax.axis_index("c")
    return x, y, c, 4 * x + 2 * y + c


def _flip(x, y, c, r):
    px = 1 - x if r & 4 else x
    py = 1 - y if r & 2 else y
    pc = 1 - c if r & 1 else c
    return (px, py, pc), 4 * px + 2 * py + pc


def _adaln_exchange(c8, ada_w, ada_b_cols, blocks, name):
    d = c8.shape[1]
    w_cols = ada_w.shape[2]
    n_arr = len(blocks)

    def body(c_ref, w_ref, b_ref, *refs):
        x_refs, (call_ref, mod_ref), out_refs = refs[:n_arr], refs[n_arr:n_arr + 2], refs[n_arr + 2:2 * n_arr + 2]
        sbuf, rbuf, s1, r1, s2, r2 = refs[2 * n_arr + 2:2 * n_arr + 8]
        gather = _Gather(x_refs, out_refs, *refs[2 * n_arr + 8:])
        x, y, c, me = _place()
        call_ref[pl.ds(pl.multiple_of(me * 8, 8), 8), :] = c_ref[...]
        peers = [_flip(x, y, c, r) for r in range(1, N_DEV)]

        def c_copy(k, src_lin, to):
            rows = call_ref.at[pl.ds(pl.multiple_of(src_lin * 8, 8), 8), :]
            return pltpu.make_async_remote_copy(src_ref=rows, dst_ref=rows, send_sem=s1.at[k], recv_sem=r1.at[k],
                                                device_id=to, device_id_type=MESH)

        first = [c_copy(k, me, peer) for k, (peer, _) in enumerate(peers)]
        for cp in first:
            cp.start()
        for k, (_, lin) in enumerate(peers):
            c_copy(k, lin, (x, y, c)).wait_recv()
        for cp in first:
            cp.wait_send()

        for j in range(N_DEV):
            cj = call_ref[8 * j:8 * j + 8, :]
            cond = cj * _sigmoid(cj)
            for l in range(2):
                sbuf[j, l] = lax.dot_general(cond, w_ref[l], NN, precision=lax.Precision.HIGHEST,
                                             preferred_element_type=F32) + b_ref[l]

        def m_copy(k, src_slot, dst_slot, to):
            return pltpu.make_async_remote_copy(src_ref=sbuf.at[src_slot], dst_ref=rbuf.at[dst_slot],
                                                send_sem=s2.at[k], recv_sem=r2.at[k], device_id=to,
                                                device_id_type=MESH)

        rbuf[me] = sbuf[me]
        second = [m_copy(k, lin, me, peer) for k, (peer, lin) in enumerate(peers)]
        for cp in second:
            cp.start()
        gather.start()
        for k, (_, lin) in enumerate(peers):
            m_copy(k, lin, lin, (x, y, c)).wait_recv()
        for cp in second:
            cp.wait_send()
        for j in range(N_DEV):
            for l in range(2):
                mod_ref[l, :, w_cols * j:w_cols * (j + 1)] = rbuf[j, l]
        gather.forward()
        gather.end()

    vmem = pl.BlockSpec(memory_space=pltpu.VMEM)
    anyspace = pl.BlockSpec(memory_space=pl.ANY)
    g_shapes, g_sems = _gather_extras(blocks)
    return pl.pallas_call(
        body, name=name, in_specs=[vmem, vmem, vmem] + [anyspace] * n_arr, out_specs=[vmem, vmem] + [anyspace] * n_arr,
        out_shape=[jax.ShapeDtypeStruct((8 * N_DEV, d), F32), jax.ShapeDtypeStruct((2, 8, 3 * d), F32)] + g_shapes,
        scratch_shapes=[pltpu.VMEM((N_DEV, 2, 8, w_cols), F32), pltpu.VMEM((N_DEV, 2, 8, w_cols), F32),
                        pltpu.SemaphoreType.DMA((N_DEV - 1,)), pltpu.SemaphoreType.DMA((N_DEV - 1,)),
                        pltpu.SemaphoreType.DMA((N_DEV - 1,)), pltpu.SemaphoreType.DMA((N_DEV - 1,))] + g_sems,
        compiler_params=pltpu.CompilerParams(vmem_limit_bytes=VMEM_LIMIT),
    )(c8, ada_w, ada_b_cols, *blocks)


class _Gather:
    def __init__(self, x_refs, out_refs, send_sems, recv_sems, local_sems):
        x, y, c, _ = _place()
        self.me, self.sibling, self.c = (x, y, c), (x, y, 1 - c), c
        self.chips = [(1 - x, y), (x, 1 - y), (1 - x, 1 - y)]
        self.n_arr = len(x_refs)
        self.out_refs, self.send_sems, self.recv_sems = out_refs, send_sems, recv_sems
        self.mine = [pltpu.make_async_copy(x_refs[t], out_refs[t].at[4 * x + 2 * y + c], local_sems.at[t])
                     for t in range(self.n_arr)]
        self.first = []
        for t in range(self.n_arr):
            self.first.append(self.copy(t, 0, self.me, self.sibling, src=x_refs[t]))
            self.first += [self.copy(t, 1 + j, self.me, (*chip, c), src=x_refs[t]) for j, chip in enumerate(self.chips)]
        self.passed = [self.copy(t, 4 + j, (*chip, c), self.sibling)
                       for t in range(self.n_arr) for j, chip in enumerate(self.chips)]

    def copy(self, t, k, blk, to, src=None):
        slot = self.out_refs[t].at[4 * blk[0] + 2 * blk[1] + blk[2]]
        return pltpu.make_async_remote_copy(src_ref=slot if src is None else src, dst_ref=slot,
                                            send_sem=self.send_sems.at[7 * t + k], recv_sem=self.recv_sems.at[7 * t + k],
                                            device_id=to, device_id_type=MESH)

    def start(self):
        for cp in self.mine + self.first:
            cp.start()

    def forward(self):
        for t in range(self.n_arr):
            for j, chip in enumerate(self.chips):
                self.copy(t, 1 + j, (*chip, self.c), self.me).wait_recv()
                self.passed[3 * t + j].start()

    def end(self):
        for t in range(self.n_arr):
            self.copy(t, 0, self.sibling, self.me).wait_recv()
            for j, chip in enumerate(self.chips):
                self.copy(t, 4 + j, (*chip, 1 - self.c), self.me).wait_recv()
        for cp in self.first + self.passed:
            cp.wait_send()
        for cp in self.mine:
            cp.wait()


def _gather_extras(blocks):
    n_arr = len(blocks)
    return ([jax.ShapeDtypeStruct((N_DEV,) + b.shape, b.dtype) for b in blocks],
            [pltpu.SemaphoreType.DMA((7 * n_arr,)), pltpu.SemaphoreType.DMA((7 * n_arr,)),
             pltpu.SemaphoreType.DMA((n_arr,))])


def _all_gather(blocks, name):
    n_arr = len(blocks)

    def body(*refs):
        gather = _Gather(refs[:n_arr], refs[n_arr:2 * n_arr], *refs[2 * n_arr:])
        gather.start()
        gather.forward()
        gather.end()

    anyspace = pl.BlockSpec(memory_space=pl.ANY)
    shapes, sems = _gather_extras(blocks)
    return pl.pallas_call(body, name=name, in_specs=[anyspace] * n_arr, out_specs=[anyspace] * n_arr,
                          out_shape=shapes, scratch_shapes=sems)(*blocks)


def _scatter_parts(parts, name):
    n_arr = len(parts)

    def body(*refs):
        copies = _exchange_copies(refs[:n_arr], refs[n_arr:2 * n_arr], *refs[2 * n_arr:], "devices")
        _exchange_start(copies)
        _exchange_wait(copies)

    anyspace = pl.BlockSpec(memory_space=pl.ANY)
    shapes, sems = _exchange_extras(parts, "devices")
    return pl.pallas_call(body, name=name, in_specs=[anyspace] * n_arr, out_specs=[anyspace] * n_arr,
                          out_shape=shapes, scratch_shapes=sems)(*parts)


def _sum_parts(parts, name):
    def body(p_ref, g_ref):
        g = p_ref[0]
        for j in range(1, N_DEV):
            g = g + p_ref[j]
        g_ref[...] = g

    return pl.pallas_call(body, name=name, out_shape=jax.ShapeDtypeStruct(parts.shape[1:], F32),
                          compiler_params=pltpu.CompilerParams(vmem_limit_bytes=VMEM_LIMIT))(parts)


N_CHIPS = N_DEV // 2


def _sibling_swap(part, name):
    def body(g_ref, r_ref, send_sems, recv_sems):
        x, y, c, _ = _place()
        sends = [pltpu.make_async_remote_copy(
            src_ref=g_ref.at[2 * q + 1 - c], dst_ref=r_ref.at[q], send_sem=send_sems.at[q], recv_sem=recv_sems.at[q],
            device_id=(x, y, 1 - c), device_id_type=MESH) for q in range(N_CHIPS)]
        recvs = [pltpu.make_async_remote_copy(
            src_ref=g_ref.at[2 * q + c], dst_ref=r_ref.at[q], send_sem=send_sems.at[q], recv_sem=recv_sems.at[q],
            device_id=(x, y, c), device_id_type=MESH) for q in range(N_CHIPS)]
        for cp in sends:
            cp.start()
        for cp in recvs:
            cp.wait_recv()
        for cp in sends:
            cp.wait_send()

    anyspace = pl.BlockSpec(memory_space=pl.ANY)
    return pl.pallas_call(
        body, name=name, in_specs=[anyspace], out_specs=anyspace,
        out_shape=jax.ShapeDtypeStruct((N_CHIPS,) + part.shape[1:], part.dtype),
        scratch_shapes=[pltpu.SemaphoreType.DMA((N_CHIPS,)), pltpu.SemaphoreType.DMA((N_CHIPS,))])(part)


def _pair_sum(a, b, name):
    n, rows, cols = a.shape
    tr = max(t for t in range(16, 513, 16) if rows % t == 0)

    def body(a_ref, b_ref, o_ref):
        o_ref[...] = (a_ref[...].astype(F32) + b_ref[...].astype(F32)).astype(o_ref.dtype)

    blk = pl.BlockSpec((None, tr, cols), lambda q, i: (q, i, 0))
    return pl.pallas_call(body, name=name, grid=(n, rows // tr), in_specs=[blk, blk], out_specs=blk,
                          out_shape=jax.ShapeDtypeStruct(a.shape, a.dtype), compiler_params=_params(2))(a, b)


def _adamw(w, g, m, v):
    m = ADAM_B1 * m + (1.0 - ADAM_B1) * g
    v = ADAM_B2 * v + (1.0 - ADAM_B2) * (g * g)
    m_hat = m / (1.0 - ADAM_B1 ** ADAM_STEP)
    v_hat = v / (1.0 - ADAM_B2 ** ADAM_STEP)
    return -ADAM_LR * (m_hat / (jnp.sqrt(v_hat) + ADAM_EPS) + ADAM_WD * w), m, v


def _sum_parts_adamw(parts, w, m, v, name):
    n_parts, rows, cols = parts.shape
    tr = max(t for t in range(16, 257, 16) if rows % t == 0)

    def body(p_ref, w_ref, m_ref, v_ref, g_ref, d_ref, mo_ref, vo_ref):
        g = p_ref[0].astype(F32)
        for j in range(1, n_parts):
            g = g + p_ref[j].astype(F32)
        g_ref[...] = g
        d_ref[...], mo_ref[...], vo_ref[...] = _adamw(w_ref[...], g, m_ref[...], v_ref[...])

    row = _rows3(tr, cols)
    out = jax.ShapeDtypeStruct((1, rows, cols), F32)
    return pl.pallas_call(
        body, name=name, grid=(rows // tr,),
        in_specs=[pl.BlockSpec((n_parts, tr, cols), lambda i: (0, i, 0)), row, row, row],
        out_specs=[row, row, row, row], out_shape=[out, out, out, out], compiler_params=_params(1),
    )(parts, w, m, v)


def _sum_parts_adamw_whole(parts, w, m, v, name):
    def body(p_ref, w_ref, m_ref, v_ref, g_ref, d_ref, mo_ref, vo_ref):
        g = p_ref[0:1].astype(F32)
        for j in range(1, N_DEV):
            g = g + p_ref[j:j + 1].astype(F32)
        g_ref[...] = g
        d_ref[...], mo_ref[...], vo_ref[...] = _adamw(w_ref[...], g, m_ref[...], v_ref[...])

    out = jax.ShapeDtypeStruct(w.shape, F32)
    return pl.pallas_call(body, name=name, out_shape=[out] * 4,
                          compiler_params=pltpu.CompilerParams(vmem_limit_bytes=VMEM_LIMIT))(parts, w, m, v)


def _adamw_many(gs, ws, ms, vs, name):
    n = len(gs)

    def body(*refs):
        for k in range(n):
            g_ref, w_ref, m_ref, v_ref = (refs[q * n + k] for q in range(4))
            d_ref, mo_ref, vo_ref = (refs[(4 + q) * n + k] for q in range(3))
            d_ref[...], mo_ref[...], vo_ref[...] = _adamw(w_ref[...], g_ref[...], m_ref[...], v_ref[...])

    out = [jax.ShapeDtypeStruct(w.shape, F32) for w in ws]
    res = pl.pallas_call(body, name=name, out_shape=out * 3,
                         compiler_params=pltpu.CompilerParams(vmem_limit_bytes=VMEM_LIMIT))(*gs, *ws, *ms, *vs)
    return res[:n], res[n:2 * n], res[2 * n:]


def _ada_w_grad_adamw(c_all, dmod_rows, w, m, v, name):
    def body(c_ref, dm_ref, w_ref, m_ref, v_ref, g_ref, d_ref, mo_ref, vo_ref):
        cv = c_ref[...]
        g = lax.dot_general(cv * _sigmoid(cv), dm_ref[...], TN, precision=lax.Precision.HIGHEST,
                            preferred_element_type=F32)
        g_ref[...] = g
        d_ref[...], mo_ref[...], vo_ref[...] = _adamw(w_ref[...], g, m_ref[...], v_ref[...])

    n_layers, d, cols = w.shape
    layer = pl.BlockSpec((None, d, cols), lambda l: (l, 0, 0))
    out = jax.ShapeDtypeStruct(w.shape, F32)
    return pl.pallas_call(
        body, name=name, grid=(n_layers,),
        in_specs=[pl.BlockSpec(c_all.shape, lambda l: (0, 0)),
                  pl.BlockSpec((None,) + dmod_rows.shape[1:], lambda l: (l, 0, 0)), layer, layer, layer],
        out_specs=[layer] * 4, out_shape=[out] * 4, compiler_params=_params(1),
    )(c_all, dmod_rows, w, m, v)


REPLICATED = ("ln_g", "ln_b", "gmlp_norm_g", "gmlp_norm_b", "gmlp_ws", "gmlp_bs", "pool_b", "pool_scale",
              "mla_kv_norm_g", "mla_w_uk", "mla_w_uv")
CHUNK_ROWS, ADA_ROW, QNORM_ROW, LOSS_ROW, REP_ROWS = 73, 73, 74, 75, 80
UQ_ROWS, POOLW_ROWS = 96, 32


def _pad_rows(flat2d, rows):
    n, k = flat2d.shape
    return jnp.pad(flat2d, ((0, 0), (0, rows * LANES - k))).reshape(n, rows, LANES)


def _ada_cols_rows(vec):
    return _pad_rows(vec.reshape(2, N_DEV, -1).transpose(1, 0, 2).reshape(N_DEV, -1), 1)


def _unpack_replicated(rep, shapes):
    chunk = sum(s[1] for s in shapes) // N_DEV
    flat, off, out = rep[:, :CHUNK_ROWS].reshape(N_DEV, -1)[:, :chunk].reshape(-1), 0, {}
    for n, size, shape in shapes:
        out[n] = flat[off:off + size].reshape(shape)
        off += size
    cols = 3 * D_MODEL // N_DEV
    out["ada_b"] = rep[:, ADA_ROW, :2 * cols].reshape(N_DEV, 2, cols).transpose(1, 0, 2).reshape(2, -1)
    return out


def kernel(x, c, positions, ada_w, ada_b, ln_g, ln_b, e_w_in, gmlp_norm_g, gmlp_norm_b, gmlp_ws, gmlp_bs, pool_w, pool_b, pool_scale, e_w_out, o_w_in, mla_q_norm_g, mla_kv_norm_g, mla_w_uq, mla_w_uk, mla_w_uv, o_w_out, loss_target, m_ada_w, m_ada_b, m_ln_g, m_ln_b, m_e_w_in, m_gmlp_norm_g, m_gmlp_norm_b, m_gmlp_ws, m_gmlp_bs, m_pool_w, m_pool_b, m_pool_scale, m_e_w_out, m_o_w_in, m_mla_q_norm_g, m_mla_kv_norm_g, m_mla_w_uq, m_mla_w_uk, m_mla_w_uv, m_o_w_out, v_ada_w, v_ada_b, v_ln_g, v_ln_b, v_e_w_in, v_gmlp_norm_g, v_gmlp_norm_b, v_gmlp_ws, v_gmlp_bs, v_pool_w, v_pool_b, v_pool_scale, v_e_w_out, v_o_w_in, v_mla_q_norm_g, v_mla_kv_norm_g, v_mla_w_uq, v_mla_w_uk, v_mla_w_uv, v_o_w_out):
    w_in = dict(ada_w=ada_w, ada_b=ada_b, ln_g=ln_g, ln_b=ln_b, e_w_in=e_w_in, gmlp_norm_g=gmlp_norm_g,
                gmlp_norm_b=gmlp_norm_b, gmlp_ws=gmlp_ws, gmlp_bs=gmlp_bs, pool_w=pool_w, pool_b=pool_b,
                pool_scale=pool_scale, e_w_out=e_w_out, o_w_in=o_w_in, mla_q_norm_g=mla_q_norm_g,
                mla_kv_norm_g=mla_kv_norm_g, mla_w_uq=mla_w_uq, mla_w_uk=mla_w_uk, mla_w_uv=mla_w_uv, o_w_out=o_w_out)
    m_in = dict(ada_w=m_ada_w, ada_b=m_ada_b, ln_g=m_ln_g, ln_b=m_ln_b, e_w_in=m_e_w_in, gmlp_norm_g=m_gmlp_norm_g,
                gmlp_norm_b=m_gmlp_norm_b, gmlp_ws=m_gmlp_ws, gmlp_bs=m_gmlp_bs, pool_w=m_pool_w, pool_b=m_pool_b,
                pool_scale=m_pool_scale, e_w_out=m_e_w_out, o_w_in=m_o_w_in, mla_q_norm_g=m_mla_q_norm_g,
                mla_kv_norm_g=m_mla_kv_norm_g, mla_w_uq=m_mla_w_uq, mla_w_uk=m_mla_w_uk, mla_w_uv=m_mla_w_uv,
                o_w_out=m_o_w_out)
    v_in = dict(ada_w=v_ada_w, ada_b=v_ada_b, ln_g=v_ln_g, ln_b=v_ln_b, e_w_in=v_e_w_in, gmlp_norm_g=v_gmlp_norm_g,
                gmlp_norm_b=v_gmlp_norm_b, gmlp_ws=v_gmlp_ws, gmlp_bs=v_gmlp_bs, pool_w=v_pool_w, pool_b=v_pool_b,
                pool_scale=v_pool_scale, e_w_out=v_e_w_out, o_w_in=v_o_w_in, mla_q_norm_g=v_mla_q_norm_g,
                mla_kv_norm_g=v_mla_kv_norm_g, mla_w_uq=v_mla_w_uq, mla_w_uk=v_mla_w_uk, mla_w_uv=v_mla_w_uv,
                o_w_out=v_o_w_out)
    names = list(w_in)
    seq = x.shape[1]
    d = D_MODEL
    me = 4 * lax.axis_index("x") + 2 * lax.axis_index("y") + lax.axis_index("c")
    ada_cols = ada_w.shape[2]

    ada_b_cols = lax.dynamic_slice_in_dim(ada_b, me * ada_cols, ada_cols, axis=1)
    slab_row = lax.broadcasted_iota(jnp.int32, (8, d), 0)
    slab = jnp.where(slab_row == 0, c, jnp.where(slab_row == 1, jnp.pad(mla_q_norm_g, ((0, 0), (0, d - 32))), 0.0))
    c_all, mod, w_in_e3, pool_w3 = _adaln_exchange(
        slab, ada_w, jnp.broadcast_to(ada_b_cols[:, None, :], (2, 8, ada_cols)),
        [e_w_in[0].astype(BF16), pool_w.astype(BF16).reshape(POOLW_ROWS, LANES)], "adaln_exchange")
    h0 = _modulate(x, mod[0], "modulate0")
    proj0, o_in3 = _matmul_cols_nn(h0, w_in_e3, BF16, 512, "even_in", side=([o_w_in[0].astype(BF16)], "gather"))
    o_in_full = o_in3.transpose(1, 0, 2).reshape(d, ODD_IN)
    w_in_o = jnp.concatenate([o_in_full[:, :448], jnp.zeros((d, 64), BF16), o_in_full[:, 448:]], axis=1)
    pool_w_full = pool_w3.reshape(N_DEV, 4, 32, 256).transpose(1, 0, 2, 3).reshape(4, 256, 256)
    g_q = c_all.reshape(N_DEV, 8, d)[:, 1, :32].reshape(1, MLA_Q_RANK)

    ws, bs_col = gmlp_ws[0], gmlp_bs[0].reshape(GMLP_HEADS, GMLP_BLOCK, 1)
    wuk2, wuv2 = mla_w_uk[0].reshape(MLA_KV_RANK, -1), mla_w_uv[0].reshape(MLA_KV_RANK, -1)
    inv = 1.0 / (ROPE_THETA ** (jnp.arange(0, MLA_ROPE, 2, dtype=F32) / MLA_ROPE))
    ang = positions[0].astype(F32)[:, None] * inv
    cosp = jnp.tile(jnp.cos(ang), (1, 4))
    sinp = jnp.tile(jnp.concatenate([-jnp.sin(ang), jnp.sin(ang)], axis=1), (1, 2))

    mix0, w_out_e3 = _even_fwd(proj0, ws, bs_col, gmlp_norm_g, gmlp_norm_b, pool_w_full, pool_b, pool_scale, "even_mix",
                               side=([e_w_out[0].astype(BF16)], "gather"))
    w_out_e = w_out_e3.reshape(-1, d)
    y0, uq3 = _matmul([(mix0, w_out_e)], "nn", F32, seq, d, 512, 1024, "even_out",
                      side=([mla_w_uq.astype(BF16).reshape(UQ_ROWS, LANES)], "gather"))
    uq_full = uq3.reshape(MLA_Q_RANK, MLA_HEADS, MLA_NOPE + MLA_ROPE)
    w_uq_n = uq_full[:, :, :MLA_NOPE].reshape(MLA_Q_RANK, -1)
    w_uq_r = uq_full[:, :, MLA_NOPE:].reshape(MLA_Q_RANK, -1)
    w_uq = jnp.concatenate([w_uq_n, w_uq_r], axis=1)
    x1, h1 = _resid_ln(x, y0, mod[0], ln_g[0:1], ln_b[0:1], mod[1], "resid_ln0")

    (proj1,) = _matmul([(h1, w_in_o)], "nn", BF16, seq, ODD_IN_PAD, 512, ODD_IN_PAD, "odd_in")
    qn, kp = _mla_prep(proj1, cosp, sinp, g_q, mla_kv_norm_g, "mla_prep")
    (q_up,) = _matmul([(qn, w_uq)], "nn", BF16, seq, 3072, 512, 3072, "q_up")
    qp = _q_heads(q_up, cosp, sinp, wuk2, "q_heads")
    o_lat, lse, w_out_o3 = _attn_fwd(qp, kp, "attn_fwd", side=([o_w_out[0].astype(BF16)], "gather"))
    w_out_o = w_out_o3.reshape(-1, d)
    gated = _o_gate(o_lat, proj1, wuv2, "o_gate")
    (y1,) = _matmul([(gated, w_out_o)], "nn", F32, seq, d, 512, 1024, "odd_out")

    dy1, dxres1, red2 = _final_ln_loss_bwd(x1, y1, mod[1], ln_g[1:2], ln_b[1:2], loss_target, "final_ln_loss")
    (dgated,) = _matmul([(dy1, w_out_o)], "nt", BF16, seq, MLA_WIDTH, 512, MLA_WIDTH, "odd_out_dx")
    (g_w_out_o,) = _matmul([(gated, dy1)], "tn", BF16, MLA_WIDTH, d, 256, d, "odd_out_dw")
    dproj1_z, do_lat, g_wuv = _o_gate_bwd(dgated, o_lat, proj1, wuv2, "o_gate_bwd")
    dqp, dkp, dvv, r_o_out = _attn_bwd(qp, kp, o_lat, do_lat, lse, "attn_bwd",
                                       side=([g_w_out_o.reshape(N_DEV, -1, d)], "devices"))
    dq_nope, dq_rope, g_wuk = _q_heads_bwd(dqp, q_up, cosp, sinp, wuk2, "q_heads_bwd")
    (dqn,) = _matmul([(dq_nope, w_uq_n), (dq_rope, w_uq_r)], "nt", F32, seq, MLA_Q_RANK, 512, 256, "q_up_dx")
    (g_wuq_n,) = _matmul([(qn, dq_nope)], "tn", F32, MLA_Q_RANK, MLA_WIDTH, 256, 512, "q_up_dw_nope")
    (g_wuq_r,) = _matmul([(qn, dq_rope)], "tn", F32, MLA_Q_RANK, 1024, 256, 512, "q_up_dw_rope")
    dproj1, red_mla = _mla_prep_bwd(proj1, dqn, dkp, dvv, cosp, sinp, g_q, mla_kv_norm_g, dproj1_z, "mla_prep_bwd")
    (dh1,) = _matmul([(dproj1, w_in_o)], "nt", F32, seq, d, 512, d, "odd_in_dx")
    part_uq = jnp.concatenate([g_wuq_n.reshape(MLA_Q_RANK, MLA_HEADS, MLA_NOPE),
                               g_wuq_r.reshape(MLA_Q_RANK, MLA_HEADS, MLA_ROPE)], axis=2).astype(BF16).reshape(
                                   (N_DEV,) + mla_w_uq.shape[1:])
    (g_w_in_o,) = _matmul([(h1, dproj1)], "tn", BF16, d, ODD_IN_PAD, 256, ODD_IN_PAD // 2, "odd_in_dw", n_outer=True)
    part_o_in = jnp.concatenate([g_w_in_o[:, :448], g_w_in_o[:, 512:]], axis=1).reshape(d, N_DEV, -1).transpose(1, 0, 2)
    dy0, dxres0, red1 = _mid_bwd(dh1, dxres1, x, y0, mod[0], mod[1], ln_g[0:1], ln_b[0:1], "mid_bwd")
    dmix, r_uq = _matmul([(dy0, w_out_e)], "nt", BF16, seq, 2048, 512, 2048, "even_out_dx", side=([part_uq], "devices"))
    (g_w_out_e,) = _matmul([(mix0, dy0)], "tn", BF16, 2048, d, 256, d, "even_out_dw")
    dproj0, g_ws, g_bs, g_ng, g_nb, g_pw, g_pb, g_ps, r_o_in = _even_bwd(
        proj0, dmix, ws, bs_col, gmlp_norm_g, gmlp_norm_b, pool_w_full, pool_b, pool_scale, "even_mix_bwd",
        side=([part_o_in], "devices"))
    part_pw = g_pw.reshape(4, N_DEV, 32, 256).transpose(1, 0, 2, 3)
    part_e_in, r_e_out, r_pw = _matmul_cols_tn(h0, dproj0, w_in_e3.shape[2], BF16, 512, "even_in_dw",
                                               side=([g_w_out_e.reshape(N_DEV, -1, d), part_pw], "devices"))
    mine = lax.dynamic_index_in_dim(part_e_in.reshape((N_CHIPS, 2) + part_e_in.shape[1:]), lax.axis_index("c"), 1, False)
    chip_e_in = _pair_sum(mine, _sibling_swap(part_e_in, "e_in_sibling_swap"), "e_in_pair_sum")
    grad_x, red0, r_e_in = _first_bwd(dproj0, w_in_e3, dxres0, x, mod[0], "even_in_dx", side=([chip_e_in], "chips"))

    t_mask = lax.broadcasted_iota(jnp.int32, (GMLP_BLOCK, GMLP_BLOCK), 0) // CHUNK
    s_mask = lax.broadcasted_iota(jnp.int32, (GMLP_BLOCK, GMLP_BLOCK), 1) // CHUNK
    part = {
        "ln_g": jnp.stack([red1[2], red2[0]]), "ln_b": jnp.stack([red1[3], red2[1]]),
        "gmlp_norm_g": g_ng, "gmlp_norm_b": g_nb,
        "gmlp_ws": jnp.where(s_mask <= t_mask, g_ws, 0.0), "gmlp_bs": g_bs,
        "pool_b": g_pb, "pool_scale": g_ps, "mla_kv_norm_g": red_mla[1, :MLA_KV_RANK],
        "mla_w_uk": g_wuk, "mla_w_uv": g_wuv,
    }
    dmod = jnp.stack([jnp.concatenate([red0[1], red0[0], red1[4]]),
                      jnp.concatenate([red1[1], red1[0], red2[2]])])

    loss_row = jnp.pad(jnp.broadcast_to((0.5 / d * jnp.sum(red2[3])).reshape(1, 1, 1), (N_DEV, 1, 1)),
                       ((0, 0), (0, 0), (0, LANES - 1)))
    part_small = jnp.concatenate([
        _pad_rows(jnp.concatenate([part[n].reshape(-1) for n in REPLICATED]).reshape(N_DEV, -1), CHUNK_ROWS),
        jnp.pad(jnp.concatenate([_ada_cols_rows(dmod), _pad_rows(red_mla[0].reshape(N_DEV, -1), 1), loss_row], axis=1),
                ((0, 0), (0, REP_ROWS - LOSS_ROW - 1), (0, 0)))], axis=1)
    (r_small,) = _scatter_parts([part_small], "grad_scatter")
    small_sum = _sum_parts(r_small, "small_sum")
    loss = small_sum[LOSS_ROW, 0]
    (rep_sum,) = _all_gather([small_sum], "replicated_gather")

    res = {"e_w_in": _sum_parts_adamw(r_e_in, e_w_in, m_e_w_in, v_e_w_in, "adamw_e_w_in"),
           "o_w_in": _sum_parts_adamw(r_o_in, o_w_in, m_o_w_in, v_o_w_in, "adamw_o_w_in"),
           "e_w_out": _sum_parts_adamw(r_e_out, e_w_out, m_e_w_out, v_e_w_out, "adamw_e_w_out"),
           "o_w_out": _sum_parts_adamw(r_o_out, o_w_out, m_o_w_out, v_o_w_out, "adamw_o_w_out"),
           "mla_w_uq": _sum_parts_adamw_whole(r_uq, mla_w_uq, m_mla_w_uq, v_mla_w_uq, "adamw_w_uq"),
           "pool_w": _sum_parts_adamw_whole(r_pw, pool_w, m_pool_w, v_pool_w, "adamw_pool_w")}
    grads = _unpack_replicated(rep_sum, [(n, w_in[n].size, w_in[n].shape) for n in REPLICATED])
    grads["mla_q_norm_g"] = small_sum[QNORM_ROW:QNORM_ROW + 1, :32]
    small_names = list(grads)
    deltas, new_ms, new_vs = _adamw_many([grads[n] for n in small_names], [w_in[n] for n in small_names],
                                         [m_in[n] for n in small_names], [v_in[n] for n in small_names], "small_adamw")
    for k, n in enumerate(small_names):
        res[n] = [grads[n], deltas[k], new_ms[k], new_vs[k]]
    dmod_all = r_small[:, ADA_ROW, :2 * ada_cols].reshape(N_DEV, 2, ada_cols).transpose(1, 0, 2)
    dmod_rows = jnp.pad(dmod_all[:, :, None, :], ((0, 0), (0, 0), (0, 7), (0, 0))).reshape(2, 8 * N_DEV, ada_cols)
    res["ada_w"] = _ada_w_grad_adamw(c_all, dmod_rows, ada_w, m_ada_w, v_ada_w, "ada_w_adamw")

    return (loss, grad_x, *[res[n][0] for n in names], *[res[n][1] for n in names],
            *[res[n][2] for n in names], *[res[n][3] for n in names])
```

```python
import functools

import jax
import jax.numpy as jnp
from jax import lax
from jax.experimental import pallas as pl
from jax.experimental.pallas import tpu as pltpu

F32 = jnp.float32
BF16 = jnp.bfloat16

D_MODEL = 1024
CHUNK = 64
LN_EPS = 1e-5
GMLP_HEADS = 4
GMLP_HEAD_DIM = 256
GMLP_BLOCK = 128
POOL_WINDOWS = (2, 4, 8, 16)
POOL_GROUP_DIM = 256
POOL_HALO = 16
MLA_HEADS = 16
MLA_NOPE = 128
MLA_ROPE = 64
MLA_Q_RANK = 256
MLA_KV_RANK = 128
MLA_WIDTH = 2048
ODD_IN = 2496
ODD_IN_PAD = 2560
ROPE_THETA = 10000.0
ATTN_SCALE = (MLA_NOPE + MLA_ROPE) ** -0.5
ATTN_SCALE_LOG2 = ATTN_SCALE * 1.4426950408889634
DEEPNORM_ALPHA = 4.0 ** 0.25
ADAM_LR, ADAM_B1, ADAM_B2, ADAM_EPS, ADAM_WD, ADAM_STEP = 0.001, 0.9, 0.999, 1e-8, 0.01, 10
N_DEV = 8
LANES = 1024
VMEM_LIMIT = 56 * 1024 * 1024
MESH = pl.DeviceIdType.MESH

NT = (((1,), (1,)), ((), ()))
NN = (((1,), (0,)), ((), ()))
TN = (((0,), (0,)), ((), ()))


def _params(n_axes):
    return pltpu.CompilerParams(dimension_semantics=("arbitrary",) * n_axes, vmem_limit_bytes=VMEM_LIMIT)


def _dot(a, b, dn):
    return lax.dot_general(a.astype(BF16), b.astype(BF16), dn, preferred_element_type=F32)


def _sigmoid(z):
    return 1.0 / (1.0 + jnp.exp(-z))


def _colsum(t):
    return jnp.sum(t, axis=0, keepdims=True)


EXCHANGE_RELATIONS = {"devices": tuple(range(1, N_DEV)), "chips": (2, 4, 6)}


def _exchange_copies(g_refs, r_refs, send_sems, recv_sems, local_sems, kind):
    x, y, c, me = _place()
    n_arr = len(g_refs)

    def slot(lin):
        return lin // 2 if kind == "chips" else lin

    own = [pltpu.make_async_copy(g_refs[t].at[slot(me)], r_refs[t].at[slot(me)], local_sems.at[t]) for t in range(n_arr)]
    sends, recvs = [], []
    for n, r in enumerate(EXCHANGE_RELATIONS[kind]):
        peer, lin = _flip(x, y, c, r)
        for t in range(n_arr):
            k = n_arr * n + t
            sends.append(pltpu.make_async_remote_copy(
                src_ref=g_refs[t].at[slot(lin)], dst_ref=r_refs[t].at[slot(me)], send_sem=send_sems.at[k],
                recv_sem=recv_sems.at[k], device_id=peer, device_id_type=MESH))
            recvs.append(pltpu.make_async_remote_copy(
                src_ref=g_refs[t].at[slot(lin)], dst_ref=r_refs[t].at[slot(lin)], send_sem=send_sems.at[k],
                recv_sem=recv_sems.at[k], device_id=(x, y, c), device_id_type=MESH))
    return own, sends, recvs


def _exchange_start(copies):
    own, sends, _ = copies
    for cp in own + sends:
        cp.start()


def _exchange_wait(copies):
    own, sends, recvs = copies
    for cp in recvs:
        cp.wait_recv()
    for cp in sends:
        cp.wait_send()
    for cp in own:
        cp.wait()


def _exchange_extras(parts, kind):
    shapes = [jax.ShapeDtypeStruct(p.shape, p.dtype) for p in parts]
    n = len(parts) * len(EXCHANGE_RELATIONS[kind])
    return shapes, [pltpu.SemaphoreType.DMA((n,)), pltpu.SemaphoreType.DMA((n,)), pltpu.SemaphoreType.DMA((len(parts),))]


def _grid_call(body, name, grid, in_specs, out_specs, out_shape, args, scratch=(), side=None):
    if side is None:
        return pl.pallas_call(body, name=name, grid=grid, in_specs=in_specs, out_specs=out_specs,
                              out_shape=out_shape, scratch_shapes=list(scratch),
                              compiler_params=_params(len(grid)))(*args)
    parts, kind = side
    gather = kind == "gather"
    n_in, n_out, n_sc, n_arr = len(args), len(out_shape), len(scratch), len(parts)
    side_shapes, side_sems = _gather_extras(parts) if gather else _exchange_extras(parts, kind)
    mid = tuple(g // 2 for g in grid)

    def wrapped(*refs):
        ins, g_refs = refs[:n_in], refs[n_in:n_in + n_arr]
        outs = refs[n_in + n_arr:n_in + n_arr + n_out]
        r_refs = refs[n_in + n_arr + n_out:n_in + 2 * n_arr + n_out]
        sc = refs[n_in + 2 * n_arr + n_out:n_in + 2 * n_arr + n_out + n_sc]
        ids = [pl.program_id(a) for a in range(len(grid))]

        def at(step):
            return functools.reduce(jnp.logical_and, [i == s for i, s in zip(ids, step)])

        first, last = at((0,) * len(grid)), at(tuple(g - 1 for g in grid))
        if gather:
            exchange = _Gather(g_refs, r_refs, *refs[-3:])
            pl.when(first)(exchange.start)
            if mid != (0,) * len(grid):
                pl.when(at(mid))(exchange.forward)
            body(*ins, *outs, *sc)

            @pl.when(last)
            def _():
                if mid == (0,) * len(grid):
                    exchange.forward()
                exchange.end()
        else:
            copies = _exchange_copies(g_refs, r_refs, *refs[-3:], kind)
            pl.when(first)(lambda: _exchange_start(copies))
            body(*ins, *outs, *sc)
            pl.when(last)(lambda: _exchange_wait(copies))

    anyspace = pl.BlockSpec(memory_space=pl.ANY)
    return pl.pallas_call(
        wrapped, name=name, grid=grid, in_specs=list(in_specs) + [anyspace] * n_arr,
        out_specs=list(out_specs) + [anyspace] * n_arr, out_shape=list(out_shape) + side_shapes,
        scratch_shapes=list(scratch) + side_sems, compiler_params=_params(len(grid)),
    )(*args, *parts)


def _matmul(pairs, mode, out_dtype, m, n, tm, tn, name, side=None, n_outer=False):
    dn = {"nn": NN, "nt": NT, "tn": TN}[mode]
    tm, tn = min(tm, m), min(tn, n)
    n_pairs = len(pairs)
    grid = (n // tn, m // tm) if n_outer else (m // tm, n // tn)

    def ij(f):
        return (lambda j, i: f(i, j)) if n_outer else f

    def body(*refs):
        o_ref = refs[-1]
        acc = None
        for p in range(n_pairs):
            t = _dot(refs[2 * p][...], refs[2 * p + 1][...], dn)
            acc = t if acc is None else acc + t
        o_ref[...] = acc.astype(o_ref.dtype)

    in_specs, args = [], []
    for a, b in pairs:
        if mode == "nn":
            k = a.shape[1]
            in_specs += [pl.BlockSpec((tm, k), ij(lambda i, j: (i, 0))), pl.BlockSpec((k, tn), ij(lambda i, j: (0, j)))]
        elif mode == "nt":
            k = a.shape[1]
            in_specs += [pl.BlockSpec((tm, k), ij(lambda i, j: (i, 0))), pl.BlockSpec((tn, k), ij(lambda i, j: (j, 0)))]
        else:
            k = a.shape[0]
            in_specs += [pl.BlockSpec((k, tm), ij(lambda i, j: (0, i))), pl.BlockSpec((k, tn), ij(lambda i, j: (0, j)))]
        args += [a, b]
    return _grid_call(body, name, grid, in_specs, [pl.BlockSpec((tm, tn), ij(lambda i, j: (i, j)))],
                      [jax.ShapeDtypeStruct((m, n), out_dtype)], args, side=side)


def _matmul_cols_nn(a, w3, out_dtype, tm, name, side=None):
    m, k = a.shape
    _, _, n = w3.shape
    tm = min(tm, m)

    def body(a_ref, w_ref, o_ref):
        av = a_ref[...]
        for j in range(N_DEV):
            o_ref[:, n * j:n * (j + 1)] = _dot(av, w_ref[j], NN).astype(o_ref.dtype)

    return _grid_call(
        body, name, (m // tm,),
        [pl.BlockSpec((tm, k), lambda i: (i, 0)), pl.BlockSpec((N_DEV, k, n), lambda i: (0, 0, 0))],
        [pl.BlockSpec((tm, N_DEV * n), lambda i: (i, 0))], [jax.ShapeDtypeStruct((m, N_DEV * n), out_dtype)], [a, w3],
        side=side)


def _matmul_cols_tn(a, b, n, out_dtype, tk, name, side=None):
    m, k = a.shape
    tk = min(tk, k)

    def body(a_ref, b_ref, o_ref):
        o_ref[...] = _dot(a_ref[...], b_ref[...], TN).astype(o_ref.dtype)

    return _grid_call(
        body, name, (N_DEV, k // tk),
        [pl.BlockSpec((m, tk), lambda j, i: (0, i)), pl.BlockSpec((m, n), lambda j, i: (0, j))],
        [pl.BlockSpec((None, tk, n), lambda j, i: (j, i, 0))], [jax.ShapeDtypeStruct((N_DEV, k, n), out_dtype)], [a, b],
        side=side)


def _rows3(tm, d):
    return pl.BlockSpec((None, tm, d), lambda i: (0, i, 0))


def _modulate(x, mod, name):
    _, s, d = x.shape
    tm = min(s, 512)

    def body(x_ref, m_ref, h_ref):
        shift, scale = m_ref[0:1, 0:d], m_ref[0:1, d:2 * d]
        h_ref[...] = (x_ref[...] * (1.0 + scale) + shift).astype(BF16)

    return pl.pallas_call(
        body, name=name, grid=(s // tm,),
        in_specs=[_rows3(tm, d), pl.BlockSpec((8, 3 * d), lambda i: (0, 0))],
        out_specs=pl.BlockSpec((tm, d), lambda i: (i, 0)),
        out_shape=jax.ShapeDtypeStruct((s, d), BF16), compiler_params=_params(1),
    )(x, mod)


def _ln_stats(r):
    mu = jnp.mean(r, axis=-1, keepdims=True)
    rc = r - mu
    var = jnp.mean(rc * rc, axis=-1, keepdims=True)
    rstd = lax.rsqrt(var + LN_EPS)
    return rc * rstd, rstd


def _ln_bwd(dxhat, xhat, rstd):
    return rstd * (dxhat - jnp.mean(dxhat, axis=-1, keepdims=True)
                   - xhat * jnp.mean(dxhat * xhat, axis=-1, keepdims=True))


def _resid_ln(x, y, mod, g, b, mod_next, name):
    _, s, d = x.shape
    tm = min(s, 512)

    def body(x_ref, y_ref, m_ref, g_ref, b_ref, mn_ref, o_ref, h_ref):
        gate = m_ref[0:1, 2 * d:3 * d]
        xhat, _ = _ln_stats(DEEPNORM_ALPHA * x_ref[...] + (1.0 + gate) * y_ref[...])
        out = xhat * g_ref[...] + b_ref[...]
        o_ref[...] = out
        h_ref[...] = (out * (1.0 + mn_ref[0:1, d:2 * d]) + mn_ref[0:1, 0:d]).astype(BF16)

    row = pl.BlockSpec((tm, d), lambda i: (i, 0))
    vec = pl.BlockSpec((1, d), lambda i: (0, 0))
    modspec = pl.BlockSpec((8, 3 * d), lambda i: (0, 0))
    return pl.pallas_call(
        body, name=name, grid=(s // tm,),
        in_specs=[_rows3(tm, d), row, modspec, vec, vec, modspec],
        out_specs=[_rows3(tm, d), row],
        out_shape=[jax.ShapeDtypeStruct((1, s, d), F32), jax.ShapeDtypeStruct((s, d), BF16)],
        compiler_params=_params(1),
    )(x, y, mod, g, b, mod_next)


def _final_ln_loss_bwd(x, y, mod, g, b, target, name):
    _, s, d = x.shape
    tm = min(s, 512)

    def body(x_ref, y_ref, m_ref, g_ref, b_ref, t_ref, dy_ref, dx_ref, red_ref):
        @pl.when(pl.program_id(0) == 0)
        def _():
            red_ref[...] = jnp.zeros_like(red_ref)

        gate = m_ref[0:1, 2 * d:3 * d]
        yv = y_ref[...]
        xhat, rstd = _ln_stats(DEEPNORM_ALPHA * x_ref[...] + (1.0 + gate) * yv)
        err = xhat * g_ref[...] + b_ref[...] - t_ref[...]
        dout = err * (1.0 / d)
        dr = _ln_bwd(dout * g_ref[...], xhat, rstd)
        dy_ref[...] = ((1.0 + gate) * dr).astype(BF16)
        dx_ref[...] = DEEPNORM_ALPHA * dr
        red_ref[0:1, :] += _colsum(dout * xhat)
        red_ref[1:2, :] += _colsum(dout)
        red_ref[2:3, :] += _colsum(dr * yv)
        red_ref[3:4, :] += _colsum(err * err)

    row = pl.BlockSpec((tm, d), lambda i: (i, 0))
    vec = pl.BlockSpec((1, d), lambda i: (0, 0))
    return pl.pallas_call(
        body, name=name, grid=(s // tm,),
        in_specs=[_rows3(tm, d), row, pl.BlockSpec((8, 3 * d), lambda i: (0, 0)), vec, vec, _rows3(tm, d)],
        out_specs=[row, row, pl.BlockSpec((8, d), lambda i: (0, 0))],
        out_shape=[jax.ShapeDtypeStruct((s, d), BF16), jax.ShapeDtypeStruct((s, d), F32),
                   jax.ShapeDtypeStruct((8, d), F32)],
        compiler_params=_params(1),
    )(x, y, mod, g, b, target)


def _mid_bwd(dh, dxres, x, y, mod_lo, mod_hi, g, b, name):
    _, s, d = x.shape
    tm = min(s, 512)

    def body(dh_ref, dxr_ref, x_ref, y_ref, ml_ref, mh_ref, g_ref, b_ref, dy_ref, dx_ref, red_ref):
        @pl.when(pl.program_id(0) == 0)
        def _():
            red_ref[...] = jnp.zeros_like(red_ref)

        gate = ml_ref[0:1, 2 * d:3 * d]
        scale_hi = mh_ref[0:1, d:2 * d]
        yv, dhv = y_ref[...], dh_ref[...]
        xhat, rstd = _ln_stats(DEEPNORM_ALPHA * x_ref[...] + (1.0 + gate) * yv)
        x_mid = xhat * g_ref[...] + b_ref[...]
        dx_mid = dxr_ref[...] + dhv * (1.0 + scale_hi)
        dr = _ln_bwd(dx_mid * g_ref[...], xhat, rstd)
        dy_ref[...] = ((1.0 + gate) * dr).astype(BF16)
        dx_ref[...] = DEEPNORM_ALPHA * dr
        red_ref[0:1, :] += _colsum(dhv * x_mid)
        red_ref[1:2, :] += _colsum(dhv)
        red_ref[2:3, :] += _colsum(dx_mid * xhat)
        red_ref[3:4, :] += _colsum(dx_mid)
        red_ref[4:5, :] += _colsum(dr * yv)

    row = pl.BlockSpec((tm, d), lambda i: (i, 0))
    vec = pl.BlockSpec((1, d), lambda i: (0, 0))
    modspec = pl.BlockSpec((8, 3 * d), lambda i: (0, 0))
    return pl.pallas_call(
        body, name=name, grid=(s // tm,),
        in_specs=[row, row, _rows3(tm, d), row, modspec, modspec, vec, vec],
        out_specs=[row, row, pl.BlockSpec((8, d), lambda i: (0, 0))],
        out_shape=[jax.ShapeDtypeStruct((s, d), BF16), jax.ShapeDtypeStruct((s, d), F32),
                   jax.ShapeDtypeStruct((8, d), F32)],
        compiler_params=_params(1),
    )(dh, dxres, x, y, mod_lo, mod_hi, g, b)


def _first_bwd(dproj, w3, dxres, x, mod, name, side=None):
    _, s, d = x.shape
    n = w3.shape[2]
    tm = min(s, 256)

    def body(a_ref, w_ref, dxr_ref, x_ref, m_ref, gx_ref, red_ref):
        @pl.when(pl.program_id(0) == 0)
        def _():
            red_ref[...] = jnp.zeros_like(red_ref)

        dhv = _dot(a_ref[:, 0:n], w_ref[0], NT)
        for j in range(1, N_DEV):
            dhv = dhv + _dot(a_ref[:, n * j:n * (j + 1)], w_ref[j], NT)
        gx_ref[...] = dxr_ref[...] + dhv * (1.0 + m_ref[0:1, d:2 * d])
        red_ref[0:1, :] += _colsum(dhv * x_ref[...])
        red_ref[1:2, :] += _colsum(dhv)

    return _grid_call(
        body, name, (s // tm,),
        [pl.BlockSpec((tm, N_DEV * n), lambda i: (i, 0)), pl.BlockSpec((N_DEV, d, n), lambda i: (0, 0, 0)),
         pl.BlockSpec((tm, d), lambda i: (i, 0)), _rows3(tm, d), pl.BlockSpec((8, 3 * d), lambda i: (0, 0))],
        [_rows3(tm, d), pl.BlockSpec((8, d), lambda i: (0, 0))],
        [jax.ShapeDtypeStruct((1, s, d), F32), jax.ShapeDtypeStruct((8, d), F32)],
        [dproj, w3, dxres, x, mod], side=side)


EVEN_TM = 512


def _gmlp_mask():
    t = lax.broadcasted_iota(jnp.int32, (GMLP_BLOCK, GMLP_BLOCK), 0) // CHUNK
    s = lax.broadcasted_iota(jnp.int32, (GMLP_BLOCK, GMLP_BLOCK), 1) // CHUNK
    return s <= t


def _window_sum(ext, win, back):
    n = ext.shape[0]
    k = 1
    while k < win:
        ext = ext + pltpu.roll(ext, k if back else n - k, 0)
        k *= 2
    return ext


def _inv_count(row0, rows, win):
    t = row0 + lax.broadcasted_iota(jnp.int32, (rows, 1), 0)
    return t, 1.0 / jnp.minimum(t + 1, win).astype(F32)


def _pooled(xb, halo, row0, win):
    tm = xb.shape[0]
    sums = _window_sum(jnp.concatenate([halo, xb], axis=0), win, True)[POOL_HALO:]
    _, inv = _inv_count(row0, tm, win)
    return sums * inv - xb


def _even_fwd(proj, ws, bs_col, ng, nb, pw, pb, ps, name, side=None):
    s = proj.shape[0]
    tm = min(s, EVEN_TM)
    hd, gd = GMLP_HEAD_DIM, POOL_GROUP_DIM

    def body(p_ref, halo_ref, ws_ref, bs_ref, ng_ref, nb_ref, pw_ref, pb_ref, ps_ref, m_ref):
        i = pl.program_id(0)
        mask = _gmlp_mask()
        for h in range(GMLP_HEADS):
            wm = jnp.where(mask, ws_ref[h], 0.0).astype(BF16)
            for blk in range(tm // GMLP_BLOCK):
                rows = slice(blk * GMLP_BLOCK, (blk + 1) * GMLP_BLOCK)
                cu, cv, cz = h * hd, 1024 + h * hd, 2048 + h * hd
                vhat, _ = _ln_stats(p_ref[rows, cv:cv + hd].astype(F32))
                vn = vhat * ng_ref[...] + nb_ref[...]
                sv = _dot(wm, vn, NN) + bs_ref[h]
                za = p_ref[rows, cz:cz + hd].astype(F32)
                m_ref[rows, cu:cu + hd] = (p_ref[rows, cu:cu + hd].astype(F32) * sv * (za * _sigmoid(za))).astype(BF16)
        for g, win in enumerate(POOL_WINDOWS):
            cx, cz = 3072 + g * gd, 4096 + g * gd
            halo = jnp.where(i > 0, halo_ref[:, g * gd:(g + 1) * gd].astype(F32), 0.0)
            pooled = _pooled(p_ref[:, cx:cx + gd].astype(F32), halo, i * tm, win)
            yb = _dot(pooled, pw_ref[g], NN) + pb_ref[:, g * gd:(g + 1) * gd]
            zb = p_ref[:, cz:cz + gd].astype(F32)
            m_ref[:, 1024 + g * gd:1024 + (g + 1) * gd] = (
                yb * ps_ref[:, g * gd:(g + 1) * gd] * (zb * _sigmoid(zb))).astype(BF16)

    hb = tm // POOL_HALO
    return _grid_call(
        body, name, (s // tm,),
        [
            pl.BlockSpec((tm, 5120), lambda i: (i, 0)),
            pl.BlockSpec((POOL_HALO, 1024), lambda i: (jnp.maximum(i * hb - 1, 0), 3)),
            pl.BlockSpec((GMLP_HEADS, GMLP_BLOCK, GMLP_BLOCK), lambda i: (0, 0, 0)),
            pl.BlockSpec((GMLP_HEADS, GMLP_BLOCK, 1), lambda i: (0, 0, 0)),
            pl.BlockSpec((1, hd), lambda i: (0, 0)), pl.BlockSpec((1, hd), lambda i: (0, 0)),
            pl.BlockSpec((4, gd, gd), lambda i: (0, 0, 0)),
            pl.BlockSpec((1, 1024), lambda i: (0, 0)), pl.BlockSpec((1, 1024), lambda i: (0, 0)),
        ],
        [pl.BlockSpec((tm, 2048), lambda i: (i, 0))], [jax.ShapeDtypeStruct((s, 2048), BF16)],
        [proj, proj, ws, bs_col, ng, nb, pw, pb, ps], side=side)


def _even_bwd(proj, dm, ws, bs_col, ng, nb, pw, pb, ps, name, side=None):
    s = proj.shape[0]
    tm = min(s, EVEN_TM)
    hd, gd = GMLP_HEAD_DIM, POOL_GROUP_DIM
    n_tiles = s // tm

    def body(p_ref, halo_ref, zbn_ref, dm_ref, dbn_ref, ws_ref, bs_ref, ng_ref, nb_ref, pw_ref, pb_ref, ps_ref,
             dp_ref, dws_ref, dbs_ref, dng_ref, dnb_ref, dpw_ref, dpb_ref, dps_ref):
        i = pl.program_id(0)

        @pl.when(i == 0)
        def _():
            for r in (dws_ref, dbs_ref, dng_ref, dnb_ref, dpw_ref, dpb_ref, dps_ref):
                r[...] = jnp.zeros_like(r)

        mask = _gmlp_mask()
        for h in range(GMLP_HEADS):
            wm = jnp.where(mask, ws_ref[h], 0.0).astype(BF16)
            for blk in range(tm // GMLP_BLOCK):
                rows = slice(blk * GMLP_BLOCK, (blk + 1) * GMLP_BLOCK)
                cu, cv, cz = h * hd, 1024 + h * hd, 2048 + h * hd
                vhat, rstd = _ln_stats(p_ref[rows, cv:cv + hd].astype(F32))
                vn = (vhat * ng_ref[...] + nb_ref[...]).astype(BF16)
                sv = _dot(wm, vn, NN) + bs_ref[h]
                u, za = p_ref[rows, cu:cu + hd].astype(F32), p_ref[rows, cz:cz + hd].astype(F32)
                da = dm_ref[rows, cu:cu + hd].astype(F32)
                sig = _sigmoid(za)
                sa = za * sig
                dau = da * u
                dsv = dau * sa
                dp_ref[rows, cu:cu + hd] = (da * sv * sa).astype(BF16)
                dp_ref[rows, cz:cz + hd] = (dau * sv * (sig * (1.0 + za * (1.0 - sig)))).astype(BF16)
                dsv_b = dsv.astype(BF16)
                dbs_ref[h] += jnp.sum(dsv, axis=1, keepdims=True)
                dws_ref[h] += _dot(dsv_b, vn, NT)
                dvn = _dot(wm, dsv_b, TN)
                dng_ref[...] += _colsum(dvn * vhat)
                dnb_ref[...] += _colsum(dvn)
                dp_ref[rows, cv:cv + hd] = _ln_bwd(dvn * ng_ref[...], vhat, rstd).astype(BF16)

        row0 = i * tm
        for g, win in enumerate(POOL_WINDOWS):
            cx, cz, cd = 3072 + g * gd, 4096 + g * gd, 1024 + g * gd
            gs = slice(g * gd, (g + 1) * gd)
            halo = jnp.where(i > 0, halo_ref[:, gs].astype(F32), 0.0)
            xb = p_ref[:, cx:cx + gd].astype(F32)
            pooled = _pooled(xb, halo, row0, win).astype(BF16)
            scale_g = ps_ref[:, gs]
            yb = _dot(pooled, pw_ref[g], NN) + pb_ref[:, gs]
            zb, db = p_ref[:, cz:cz + gd].astype(F32), dm_ref[:, cd:cd + gd].astype(F32)
            sig = _sigmoid(zb)
            dyp = db * (zb * sig)
            dp_ref[:, cz:cz + gd] = (db * yb * scale_g * (sig * (1.0 + zb * (1.0 - sig)))).astype(BF16)
            dps_ref[:, gs] += _colsum(dyp * yb)
            dpb_ref[:, gs] += _colsum(dyp * scale_g)
            zb_ext = jnp.concatenate([zb, zbn_ref[:, gs].astype(F32)], axis=0)
            db_ext = jnp.concatenate([db, dbn_ref[:, gs].astype(F32)], axis=0)
            dy_ext = (db_ext * (zb_ext * _sigmoid(zb_ext)) * scale_g).astype(BF16)
            dpw_ref[g] += _dot(pooled, dy_ext[:tm], TN)
            dpooled = _dot(dy_ext, pw_ref[g], NT)
            t, inv = _inv_count(row0, tm + POOL_HALO, win)
            w_ext = jnp.where(t < s, dpooled * inv, 0.0)
            dp_ref[:, cx:cx + gd] = (_window_sum(w_ext, win, False)[:tm] - dpooled[:tm]).astype(BF16)

    hb = tm // POOL_HALO
    last = s // POOL_HALO - 1
    small = lambda shape: pl.BlockSpec(shape, lambda i: (0,) * len(shape))
    return _grid_call(
        body, name, (n_tiles,),
        [
            pl.BlockSpec((tm, 5120), lambda i: (i, 0)),
            pl.BlockSpec((POOL_HALO, 1024), lambda i: (jnp.maximum(i * hb - 1, 0), 3)),
            pl.BlockSpec((POOL_HALO, 1024), lambda i: (jnp.minimum((i + 1) * hb, last), 4)),
            pl.BlockSpec((tm, 2048), lambda i: (i, 0)),
            pl.BlockSpec((POOL_HALO, 1024), lambda i: (jnp.minimum((i + 1) * hb, last), 1)),
            small((GMLP_HEADS, GMLP_BLOCK, GMLP_BLOCK)), small((GMLP_HEADS, GMLP_BLOCK, 1)),
            small((1, hd)), small((1, hd)), small((4, gd, gd)), small((1, 1024)), small((1, 1024)),
        ],
        [
            pl.BlockSpec((tm, 5120), lambda i: (i, 0)),
            small((GMLP_HEADS, GMLP_BLOCK, GMLP_BLOCK)), small((GMLP_HEADS, GMLP_BLOCK, 1)),
            small((1, hd)), small((1, hd)), small((4, gd, gd)), small((1, 1024)), small((1, 1024)),
        ],
        [
            jax.ShapeDtypeStruct((s, 5120), BF16),
            jax.ShapeDtypeStruct((GMLP_HEADS, GMLP_BLOCK, GMLP_BLOCK), F32),
            jax.ShapeDtypeStruct((GMLP_HEADS, GMLP_BLOCK, 1), F32),
            jax.ShapeDtypeStruct((1, hd), F32), jax.ShapeDtypeStruct((1, hd), F32),
            jax.ShapeDtypeStruct((4, gd, gd), F32),
            jax.ShapeDtypeStruct((1, 1024), F32), jax.ShapeDtypeStruct((1, 1024), F32),
        ],
        [proj, proj, proj, dm, dm, ws, bs_col, ng, nb, pw, pb, ps], side=side)


def _rope_pair_swap(t):
    lane = lax.broadcasted_iota(jnp.int32, t.shape, 1)
    return jnp.where(lane % 64 < 32, pltpu.roll(t, 96, 1), pltpu.roll(t, 32, 1))


def _rms(x, g):
    r = lax.rsqrt(jnp.mean(x * x, axis=-1, keepdims=True) + LN_EPS)
    return x * r, r


def _rms_bwd(dy, g, xhat, r):
    dyg = dy * g
    return r * (dyg - xhat * jnp.mean(dyg * xhat, axis=-1, keepdims=True))


def _lane_lt(shape, n):
    return lax.broadcasted_iota(jnp.int32, shape, 1) < n


def _mla_prep(proj, cosp, sinp, gq, gkv, name):
    s = proj.shape[0]
    tm = min(s, 512)

    def body(qc_ref, kv_ref, c_ref, s_ref, gq_ref, gkv_ref, qn_ref, kp_ref):
        qhat, _ = _rms(qc_ref[...].astype(F32), None)
        qn_ref[...] = (qhat * gq_ref[...]).astype(BF16)
        khat, _ = _rms(kv_ref[:, 0:128].astype(F32), None)
        kp_ref[:, 0:128] = (khat * gkv_ref[...]).astype(BF16)
        kr = kv_ref[:, 128:256].astype(F32)
        kp_ref[:, 128:256] = (kr * c_ref[...] + _rope_pair_swap(kr) * s_ref[...]).astype(BF16)

    return pl.pallas_call(
        body, name=name, grid=(s // tm,),
        in_specs=[pl.BlockSpec((tm, 256), lambda i: (i, 0)), pl.BlockSpec((tm, 256), lambda i: (i, 1)),
                  pl.BlockSpec((tm, 128), lambda i: (i, 0)), pl.BlockSpec((tm, 128), lambda i: (i, 0)),
                  pl.BlockSpec((1, 256), lambda i: (0, 0)), pl.BlockSpec((1, 128), lambda i: (0, 0))],
        out_specs=[pl.BlockSpec((tm, 256), lambda i: (i, 0)), pl.BlockSpec((tm, 256), lambda i: (i, 0))],
        out_shape=[jax.ShapeDtypeStruct((s, 256), BF16), jax.ShapeDtypeStruct((s, 256), BF16)],
        compiler_params=_params(1),
    )(proj, proj, cosp, sinp, gq, gkv)


def _mla_prep_bwd(proj, dqn, dkp, dv, cosp, sinp, gq, gkv, dproj, name):
    s = proj.shape[0]
    tm = min(s, 512)

    def body(qc_ref, kv_ref, dqn_ref, dkp_ref, dv_ref, c_ref, s_ref, gq_ref, gkv_ref, dproj_ref, o_ref, red_ref):
        @pl.when(pl.program_id(0) == 0)
        def _():
            red_ref[...] = jnp.zeros_like(red_ref)

        qhat, qr = _rms(qc_ref[...].astype(F32), None)
        dq = dqn_ref[...]
        o_ref[:, 0:256] = _rms_bwd(dq, gq_ref[...], qhat, qr).astype(BF16)
        red_ref[0:1, :] += _colsum(dq * qhat)
        khat, kr = _rms(kv_ref[:, 0:128].astype(F32), None)
        dk = dkp_ref[:, 0:128] + dv_ref[...]
        o_ref[:, 256:384] = _rms_bwd(dk, gkv_ref[...], khat, kr).astype(BF16)
        red_ref[1:2, 0:128] += _colsum(dk * khat)
        dr = dkp_ref[:, 128:256]
        o_ref[:, 384:512] = (dr * c_ref[...] - _rope_pair_swap(dr) * s_ref[...]).astype(BF16)

    return pl.pallas_call(
        body, name=name, grid=(s // tm,),
        in_specs=[pl.BlockSpec((tm, 256), lambda i: (i, 0)), pl.BlockSpec((tm, 256), lambda i: (i, 1)),
                  pl.BlockSpec((tm, 256), lambda i: (i, 0)), pl.BlockSpec((tm, 256), lambda i: (i, 0)),
                  pl.BlockSpec((tm, 128), lambda i: (i, 0)),
                  pl.BlockSpec((tm, 128), lambda i: (i, 0)), pl.BlockSpec((tm, 128), lambda i: (i, 0)),
                  pl.BlockSpec((1, 256), lambda i: (0, 0)), pl.BlockSpec((1, 128), lambda i: (0, 0)),
                  pl.BlockSpec(memory_space=pl.ANY)],
        out_specs=[pl.BlockSpec((tm, 512), lambda i: (i, 0)), pl.BlockSpec((8, 256), lambda i: (0, 0))],
        out_shape=[jax.ShapeDtypeStruct(dproj.shape, BF16), jax.ShapeDtypeStruct((8, 256), F32)],
        input_output_aliases={9: 0}, compiler_params=_params(1),
    )(proj, proj, dqn, dkp, dv, cosp, sinp, gq, gkv, dproj)


HEADS_TM = 512
Z_COL0 = ODD_IN_PAD - MLA_WIDTH


def _head_cols(h):
    return slice(128 * h, 128 * h + 128)


def _q_heads(q_up, cosp, sinp, wuk, name):
    s = q_up.shape[0]
    tm = min(s, HEADS_TM)

    def body(q_ref, c_ref, s_ref, w_ref, o_ref):
        for p in range(MLA_HEADS // 2):
            raw = q_ref[:, MLA_WIDTH + 128 * p:MLA_WIDTH + 128 * (p + 1)].astype(F32)
            rot = raw * c_ref[...] + _rope_pair_swap(raw) * s_ref[...]
            low = _lane_lt(rot.shape, 64)
            o_ref[2 * p, :, 128:256] = jnp.where(low, rot, 0.0).astype(BF16)
            o_ref[2 * p + 1, :, 128:256] = jnp.where(low, pltpu.roll(rot, 64, 1), 0.0).astype(BF16)
        for h in range(MLA_HEADS):
            o_ref[h, :, 0:128] = _dot(q_ref[:, _head_cols(h)], w_ref[:, _head_cols(h)], NT).astype(BF16)

    return pl.pallas_call(
        body, name=name, grid=(s // tm,),
        in_specs=[pl.BlockSpec((tm, 3072), lambda i: (i, 0)),
                  pl.BlockSpec((tm, 128), lambda i: (i, 0)), pl.BlockSpec((tm, 128), lambda i: (i, 0)),
                  pl.BlockSpec((128, MLA_WIDTH), lambda i: (0, 0))],
        out_specs=pl.BlockSpec((MLA_HEADS, tm, 256), lambda i: (0, i, 0)),
        out_shape=jax.ShapeDtypeStruct((MLA_HEADS, s, 256), BF16), compiler_params=_params(1),
    )(q_up, cosp, sinp, wuk)


def _q_heads_bwd(dqp, q_up, cosp, sinp, wuk, name):
    s = q_up.shape[0]
    tm = min(s, HEADS_TM)

    def body(dq_ref, qn_ref, c_ref, s_ref, w_ref, dn_ref, dr_ref, dw_ref):
        @pl.when(pl.program_id(0) == 0)
        def _():
            dw_ref[...] = jnp.zeros_like(dw_ref)

        for h in range(MLA_HEADS):
            dlat = dq_ref[h, :, 0:128]
            dn_ref[:, _head_cols(h)] = _dot(dlat, w_ref[:, _head_cols(h)], NN).astype(BF16)
            dw_ref[:, _head_cols(h)] += _dot(dlat, qn_ref[:, _head_cols(h)], TN)
        for p in range(MLA_HEADS // 2):
            drot = dq_ref[2 * p, :, 128:256].astype(F32) + pltpu.roll(dq_ref[2 * p + 1, :, 128:256].astype(F32), 64, 1)
            dr_ref[:, _head_cols(p)] = (drot * c_ref[...] - _rope_pair_swap(drot) * s_ref[...]).astype(BF16)

    return pl.pallas_call(
        body, name=name, grid=(s // tm,),
        in_specs=[pl.BlockSpec((MLA_HEADS, tm, 256), lambda i: (0, i, 0)),
                  pl.BlockSpec((tm, MLA_WIDTH), lambda i: (i, 0)),
                  pl.BlockSpec((tm, 128), lambda i: (i, 0)), pl.BlockSpec((tm, 128), lambda i: (i, 0)),
                  pl.BlockSpec((128, MLA_WIDTH), lambda i: (0, 0))],
        out_specs=[pl.BlockSpec((tm, MLA_WIDTH), lambda i: (i, 0)),
                   pl.BlockSpec((tm, 1024), lambda i: (i, 0)),
                   pl.BlockSpec((128, MLA_WIDTH), lambda i: (0, 0))],
        out_shape=[jax.ShapeDtypeStruct((s, MLA_WIDTH), BF16), jax.ShapeDtypeStruct((s, 1024), BF16),
                   jax.ShapeDtypeStruct((128, MLA_WIDTH), F32)],
        compiler_params=_params(1),
    )(dqp, q_up, cosp, sinp, wuk)


def _o_gate(o_lat, proj, wuv, name):
    s = o_lat.shape[1]
    tm = min(s, HEADS_TM)

    def body(ol_ref, p_ref, w_ref, g_ref):
        for h in range(MLA_HEADS):
            z = p_ref[:, Z_COL0 + 128 * h:Z_COL0 + 128 * (h + 1)].astype(F32)
            g_ref[:, _head_cols(h)] = (_dot(ol_ref[h], w_ref[:, _head_cols(h)], NN) * (z * _sigmoid(z))).astype(BF16)

    return pl.pallas_call(
        body, name=name, grid=(s // tm,),
        in_specs=[pl.BlockSpec((MLA_HEADS, tm, 128), lambda i: (0, i, 0)),
                  pl.BlockSpec((tm, ODD_IN_PAD), lambda i: (i, 0)),
                  pl.BlockSpec((128, MLA_WIDTH), lambda i: (0, 0))],
        out_specs=pl.BlockSpec((tm, MLA_WIDTH), lambda i: (i, 0)),
        out_shape=jax.ShapeDtypeStruct((s, MLA_WIDTH), BF16), compiler_params=_params(1),
    )(o_lat, proj, wuv)


def _o_gate_bwd(dg, o_lat, proj, wuv, name):
    s = o_lat.shape[1]
    tm = min(s, HEADS_TM)

    def body(dg_ref, ol_ref, p_ref, w_ref, dp_ref, dol_ref, dw_ref):
        @pl.when(pl.program_id(0) == 0)
        def _():
            dw_ref[...] = jnp.zeros_like(dw_ref)

        dp_ref[:, 0:Z_COL0] = jnp.zeros((tm, Z_COL0), BF16)
        for h in range(MLA_HEADS):
            zc = slice(Z_COL0 + 128 * h, Z_COL0 + 128 * (h + 1))
            z, dgv, ol = p_ref[:, zc].astype(F32), dg_ref[:, _head_cols(h)].astype(F32), ol_ref[h]
            sig = _sigmoid(z)
            o = _dot(ol, w_ref[:, _head_cols(h)], NN)
            dp_ref[:, zc] = (dgv * o * (sig * (1.0 + z * (1.0 - sig)))).astype(BF16)
            do = (dgv * (z * sig)).astype(BF16)
            dol_ref[h] = _dot(do, w_ref[:, _head_cols(h)], NT).astype(BF16)
            dw_ref[:, _head_cols(h)] += _dot(ol, do, TN)

    return pl.pallas_call(
        body, name=name, grid=(s // tm,),
        in_specs=[pl.BlockSpec((tm, MLA_WIDTH), lambda i: (i, 0)),
                  pl.BlockSpec((MLA_HEADS, tm, 128), lambda i: (0, i, 0)),
                  pl.BlockSpec((tm, ODD_IN_PAD), lambda i: (i, 0)),
                  pl.BlockSpec((128, MLA_WIDTH), lambda i: (0, 0))],
        out_specs=[pl.BlockSpec((tm, ODD_IN_PAD), lambda i: (i, 0)),
                   pl.BlockSpec((MLA_HEADS, tm, 128), lambda i: (0, i, 0)),
                   pl.BlockSpec((128, MLA_WIDTH), lambda i: (0, 0))],
        out_shape=[jax.ShapeDtypeStruct((s, ODD_IN_PAD), BF16), jax.ShapeDtypeStruct((MLA_HEADS, s, 128), BF16),
                   jax.ShapeDtypeStruct((128, MLA_WIDTH), F32)],
        compiler_params=_params(1),
    )(dg, o_lat, proj, wuv)


ATT_TQ = CHUNK
ATT_ROWS = ATT_TQ * MLA_HEADS
ATT_TK = 512
ATT_HEAD_GROUP = 8


def _visible(k0, q_chunk, tk):
    kpos = k0 + lax.broadcasted_iota(jnp.int32, (1, tk), 1)
    return kpos // CHUNK <= q_chunk


def _tile_lanes(t, n):
    return jnp.concatenate([t] * (n // 128), axis=1)


def _key_blocks(i, tk, block, pairs=False):
    visible = i * ATT_TQ + ATT_TQ
    n_full = (visible + tk - 1) // tk - 1

    def full(j):
        block(pl.multiple_of(j * tk, tk), tk, False)

    if pairs:
        def two(jj, carry):
            full(2 * jj)
            full(2 * jj + 1)
            return carry

        lax.fori_loop(0, n_full // 2, two, 0)

        @pl.when(n_full % 2 == 1)
        def _():
            full(n_full - 1)
    else:
        def one(j, carry):
            full(j)
            return carry

        lax.fori_loop(0, n_full, one, 0)
    last0 = pl.multiple_of(n_full * tk, tk)
    half = tk // 2
    if half % 128 == 0:
        @pl.when(visible - n_full * tk <= half)
        def _():
            block(last0, half, True)

        @pl.when(visible - n_full * tk > half)
        def _():
            block(last0, tk, True)
    else:
        block(last0, tk, True)


def _attn_fwd(qp, kp, name, side=None):
    s = kp.shape[0]
    tk = min(ATT_TK, s)

    def body(q_ref, k_ref, o_ref, lse_ref, m_sc, acc_sc):
        i = pl.program_id(0)
        m_sc[...] = jnp.full_like(m_sc, -jnp.inf)
        acc_sc[...] = jnp.zeros_like(acc_sc)

        def block(k0, width, masked):
            k = k_ref[pl.ds(k0, width), :]
            v1 = jnp.where(_lane_lt(k.shape, 128), k, jnp.ones_like(k))
            for h0 in range(0, MLA_HEADS, ATT_HEAD_GROUP):
                rows = slice(h0 * ATT_TQ, (h0 + ATT_HEAD_GROUP) * ATT_TQ)
                q = q_ref[h0:h0 + ATT_HEAD_GROUP].reshape(ATT_HEAD_GROUP * ATT_TQ, 256)
                sc = _dot(q, k, NT) * ATTN_SCALE_LOG2
                if masked:
                    sc = jnp.where(_visible(k0, i, width), sc, -jnp.inf)
                m_prev = m_sc[rows]
                m_new = jnp.maximum(m_prev, jnp.max(sc, axis=1, keepdims=True))
                p = jnp.exp2(sc - _tile_lanes(m_new, width))
                acc_sc[rows] = _tile_lanes(jnp.exp2(m_prev - m_new), 256) * acc_sc[rows] + _dot(p, v1, NN)
                m_sc[rows] = m_new

        _key_blocks(i, tk, block, pairs=True)
        acc = acc_sc[...]
        l = acc[:, 128:256]
        o_ref[...] = (acc[:, 0:128] / l).astype(BF16).reshape(MLA_HEADS, ATT_TQ, 128)
        lse_ref[...] = (m_sc[...] + jnp.log2(l)).reshape(MLA_HEADS, ATT_TQ, 128)

    head128 = pl.BlockSpec((MLA_HEADS, ATT_TQ, 128), lambda i: (0, i, 0))
    return _grid_call(
        body, name, (s // ATT_TQ,),
        [pl.BlockSpec((MLA_HEADS, ATT_TQ, 256), lambda i: (0, i, 0)), pl.BlockSpec((s, 256), lambda i: (0, 0))],
        [head128, head128],
        [jax.ShapeDtypeStruct((MLA_HEADS, s, 128), BF16), jax.ShapeDtypeStruct((MLA_HEADS, s, 128), F32)],
        [qp, kp], scratch=[pltpu.VMEM((ATT_ROWS, 128), F32), pltpu.VMEM((ATT_ROWS, 256), F32)], side=side)


def _attn_bwd(qp, kp, o, do, lse, name, side=None):
    s = kp.shape[0]
    tk = min(ATT_TK, s)

    def body(q_ref, k_ref, o_ref, do_ref, lse_ref, dq_ref, dk_ref, dv_ref, dq_sc):
        i = pl.program_id(0)

        @pl.when(i == 0)
        def _():
            dk_ref[...] = jnp.zeros_like(dk_ref)
            dv_ref[...] = jnp.zeros_like(dv_ref)

        delta = jnp.sum(do_ref[...].reshape(ATT_ROWS, 128).astype(F32) * o_ref[...].reshape(ATT_ROWS, 128).astype(F32),
                        axis=1, keepdims=True)
        delta_t = _tile_lanes(jnp.broadcast_to(delta, (ATT_ROWS, 128)), tk)
        lse_t = _tile_lanes(lse_ref[...].reshape(ATT_ROWS, 128), tk)
        dq_sc[...] = jnp.zeros_like(dq_sc)

        def block(k0, width, masked):
            k = k_ref[pl.ds(k0, width), :]
            for h0 in range(0, MLA_HEADS, ATT_HEAD_GROUP):
                rows = slice(h0 * ATT_TQ, (h0 + ATT_HEAD_GROUP) * ATT_TQ)
                q = q_ref[h0:h0 + ATT_HEAD_GROUP].reshape(ATT_HEAD_GROUP * ATT_TQ, 256)
                dov = do_ref[h0:h0 + ATT_HEAD_GROUP].reshape(ATT_HEAD_GROUP * ATT_TQ, 128)
                p = jnp.exp2(_dot(q, k, NT) * ATTN_SCALE_LOG2 - lse_t[rows, 0:width])
                if masked:
                    p = jnp.where(_visible(k0, i, width), p, 0.0)
                dv_ref[pl.ds(k0, width), :] += _dot(p, dov, TN)
                ds = (p * (_dot(dov, k[:, 0:128], NT) - delta_t[rows, 0:width]) * ATTN_SCALE).astype(BF16)
                dq_sc[rows] += _dot(ds, k, NN)
                dk_ref[pl.ds(k0, width), :] += _dot(ds, q, TN)

        _key_blocks(i, tk, block, pairs=True)
        dq_ref[...] = dq_sc[...].astype(BF16).reshape(MLA_HEADS, ATT_TQ, 256)

    head128 = pl.BlockSpec((MLA_HEADS, ATT_TQ, 128), lambda i: (0, i, 0))
    head256 = pl.BlockSpec((MLA_HEADS, ATT_TQ, 256), lambda i: (0, i, 0))
    return _grid_call(
        body, name, (s // ATT_TQ,),
        [head256, pl.BlockSpec((s, 256), lambda i: (0, 0)), head128, head128, head128],
        [head256, pl.BlockSpec((s, 256), lambda i: (0, 0)), pl.BlockSpec((s, 128), lambda i: (0, 0))],
        [jax.ShapeDtypeStruct((MLA_HEADS, s, 256), BF16),
         jax.ShapeDtypeStruct((s, 256), F32), jax.ShapeDtypeStruct((s, 128), F32)],
        [qp, kp, o, do, lse], scratch=[pltpu.VMEM((ATT_ROWS, 256), F32)], side=side)


def _place():
    x, y, c = lax.axis_index("x"), lax.axis_index("y"), lax.axis_index("c")
    return x, y, c, 4 * x + 2 * y + c


def _flip(x, y, c, r):
    px = 1 - x if r & 4 else x
    py = 1 - y if r & 2 else y
    pc = 1 - c if r & 1 else c
    return (px, py, pc), 4 * px + 2 * py + pc


def _adaln_exchange(c8, ada_w, ada_b_cols, blocks, name):
    d = c8.shape[1]
    w_cols = ada_w.shape[2]
    n_arr = len(blocks)

    def body(c_ref, w_ref, b_ref, *refs):
        x_refs, (call_ref, mod_ref), out_refs = refs[:n_arr], refs[n_arr:n_arr + 2], refs[n_arr + 2:2 * n_arr + 2]
        sbuf, rbuf, s1, r1, s2, r2 = refs[2 * n_arr + 2:2 * n_arr + 8]
        gather = _Gather(x_refs, out_refs, *refs[2 * n_arr + 8:])
        x, y, c, me = _place()
        call_ref[pl.ds(pl.multiple_of(me * 8, 8), 8), :] = c_ref[...]
        peers = [_flip(x, y, c, r) for r in range(1, N_DEV)]

        def c_copy(k, src_lin, to):
            rows = call_ref.at[pl.ds(pl.multiple_of(src_lin * 8, 8), 8), :]
            return pltpu.make_async_remote_copy(src_ref=rows, dst_ref=rows, send_sem=s1.at[k], recv_sem=r1.at[k],
                                                device_id=to, device_id_type=MESH)

        first = [c_copy(k, me, peer) for k, (peer, _) in enumerate(peers)]
        for cp in first:
            cp.start()
        for k, (_, lin) in enumerate(peers):
            c_copy(k, lin, (x, y, c)).wait_recv()
        for cp in first:
            cp.wait_send()

        for j in range(N_DEV):
            cj = call_ref[8 * j:8 * j + 8, :]
            cond = cj * _sigmoid(cj)
            for l in range(2):
                sbuf[j, l] = lax.dot_general(cond, w_ref[l], NN, precision=lax.Precision.HIGHEST,
                                             preferred_element_type=F32) + b_ref[l]

        def m_copy(k, src_slot, dst_slot, to):
            return pltpu.make_async_remote_copy(src_ref=sbuf.at[src_slot], dst_ref=rbuf.at[dst_slot],
                                                send_sem=s2.at[k], recv_sem=r2.at[k], device_id=to,
                                                device_id_type=MESH)

        rbuf[me] = sbuf[me]
        second = [m_copy(k, lin, me, peer) for k, (peer, lin) in enumerate(peers)]
        for cp in second:
            cp.start()
        gather.start()
        for k, (_, lin) in enumerate(peers):
            m_copy(k, lin, lin, (x, y, c)).wait_recv()
        for cp in second:
            cp.wait_send()
        for j in range(N_DEV):
            for l in range(2):
                mod_ref[l, :, w_cols * j:w_cols * (j + 1)] = rbuf[j, l]
        gather.forward()
        gather.end()

    vmem = pl.BlockSpec(memory_space=pltpu.VMEM)
    anyspace = pl.BlockSpec(memory_space=pl.ANY)
    g_shapes, g_sems = _gather_extras(blocks)
    return pl.pallas_call(
        body, name=name, in_specs=[vmem, vmem, vmem] + [anyspace] * n_arr, out_specs=[vmem, vmem] + [anyspace] * n_arr,
        out_shape=[jax.ShapeDtypeStruct((8 * N_DEV, d), F32), jax.ShapeDtypeStruct((2, 8, 3 * d), F32)] + g_shapes,
        scratch_shapes=[pltpu.VMEM((N_DEV, 2, 8, w_cols), F32), pltpu.VMEM((N_DEV, 2, 8, w_cols), F32),
                        pltpu.SemaphoreType.DMA((N_DEV - 1,)), pltpu.SemaphoreType.DMA((N_DEV - 1,)),
                        pltpu.SemaphoreType.DMA((N_DEV - 1,)), pltpu.SemaphoreType.DMA((N_DEV - 1,))] + g_sems,
        compiler_params=pltpu.CompilerParams(vmem_limit_bytes=VMEM_LIMIT),
    )(c8, ada_w, ada_b_cols, *blocks)


class _Gather:
    def __init__(self, x_refs, out_refs, send_sems, recv_sems, local_sems):
        x, y, c, _ = _place()
        self.me, self.sibling, self.c = (x, y, c), (x, y, 1 - c), c
        self.chips = [(1 - x, y), (x, 1 - y), (1 - x, 1 - y)]
        self.n_arr = len(x_refs)
        self.out_refs, self.send_sems, self.recv_sems = out_refs, send_sems, recv_sems
        self.mine = [pltpu.make_async_copy(x_refs[t], out_refs[t].at[4 * x + 2 * y + c], local_sems.at[t])
                     for t in range(self.n_arr)]
        self.first = []
        for t in range(self.n_arr):
            self.first.append(self.copy(t, 0, self.me, self.sibling, src=x_refs[t]))
            self.first += [self.copy(t, 1 + j, self.me, (*chip, c), src=x_refs[t]) for j, chip in enumerate(self.chips)]
        self.passed = [self.copy(t, 4 + j, (*chip, c), self.sibling)
                       for t in range(self.n_arr) for j, chip in enumerate(self.chips)]

    def copy(self, t, k, blk, to, src=None):
        slot = self.out_refs[t].at[4 * blk[0] + 2 * blk[1] + blk[2]]
        return pltpu.make_async_remote_copy(src_ref=slot if src is None else src, dst_ref=slot,
                                            send_sem=self.send_sems.at[7 * t + k], recv_sem=self.recv_sems.at[7 * t + k],
                                            device_id=to, device_id_type=MESH)

    def start(self):
        for cp in self.mine + self.first:
            cp.start()

    def forward(self):
        for t in range(self.n_arr):
            for j, chip in enumerate(self.chips):
                self.copy(t, 1 + j, (*chip, self.c), self.me).wait_recv()
                self.passed[3 * t + j].start()

    def end(self):
        for t in range(self.n_arr):
            self.copy(t, 0, self.sibling, self.me).wait_recv()
            for j, chip in enumerate(self.chips):
                self.copy(t, 4 + j, (*chip, 1 - self.c), self.me).wait_recv()
        for cp in self.first + self.passed:
            cp.wait_send()
        for cp in self.mine:
            cp.wait()


def _gather_extras(blocks):
    n_arr = len(blocks)
    return ([jax.ShapeDtypeStruct((N_DEV,) + b.shape, b.dtype) for b in blocks],
            [pltpu.SemaphoreType.DMA((7 * n_arr,)), pltpu.SemaphoreType.DMA((7 * n_arr,)),
             pltpu.SemaphoreType.DMA((n_arr,))])


def _all_gather(blocks, name):
    n_arr = len(blocks)

    def body(*refs):
        gather = _Gather(refs[:n_arr], refs[n_arr:2 * n_arr], *refs[2 * n_arr:])
        gather.start()
        gather.forward()
        gather.end()

    anyspace = pl.BlockSpec(memory_space=pl.ANY)
    shapes, sems = _gather_extras(blocks)
    return pl.pallas_call(body, name=name, in_specs=[anyspace] * n_arr, out_specs=[anyspace] * n_arr,
                          out_shape=shapes, scratch_shapes=sems)(*blocks)


def _scatter_parts(parts, name):
    n_arr = len(parts)

    def body(*refs):
        copies = _exchange_copies(refs[:n_arr], refs[n_arr:2 * n_arr], *refs[2 * n_arr:], "devices")
        _exchange_start(copies)
        _exchange_wait(copies)

    anyspace = pl.BlockSpec(memory_space=pl.ANY)
    shapes, sems = _exchange_extras(parts, "devices")
    return pl.pallas_call(body, name=name, in_specs=[anyspace] * n_arr, out_specs=[anyspace] * n_arr,
                          out_shape=shapes, scratch_shapes=sems)(*parts)


def _sum_parts(parts, name):
    def body(p_ref, g_ref):
        g = p_ref[0]
        for j in range(1, N_DEV):
            g = g + p_ref[j]
        g_ref[...] = g

    return pl.pallas_call(body, name=name, out_shape=jax.ShapeDtypeStruct(parts.shape[1:], F32),
                          compiler_params=pltpu.CompilerParams(vmem_limit_bytes=VMEM_LIMIT))(parts)


N_CHIPS = N_DEV // 2


def _sibling_swap(part, name):
    def body(g_ref, r_ref, send_sems, recv_sems):
        x, y, c, _ = _place()
        sends = [pltpu.make_async_remote_copy(
            src_ref=g_ref.at[2 * q + 1 - c], dst_ref=r_ref.at[q], send_sem=send_sems.at[q], recv_sem=recv_sems.at[q],
            device_id=(x, y, 1 - c), device_id_type=MESH) for q in range(N_CHIPS)]
        recvs = [pltpu.make_async_remote_copy(
            src_ref=g_ref.at[2 * q + c], dst_ref=r_ref.at[q], send_sem=send_sems.at[q], recv_sem=recv_sems.at[q],
            device_id=(x, y, c), device_id_type=MESH) for q in range(N_CHIPS)]
        for cp in sends:
            cp.start()
        for cp in recvs:
            cp.wait_recv()
        for cp in sends:
            cp.wait_send()

    anyspace = pl.BlockSpec(memory_space=pl.ANY)
    return pl.pallas_call(
        body, name=name, in_specs=[anyspace], out_specs=anyspace,
        out_shape=jax.ShapeDtypeStruct((N_CHIPS,) + part.shape[1:], part.dtype),
        scratch_shapes=[pltpu.SemaphoreType.DMA((N_CHIPS,)), pltpu.SemaphoreType.DMA((N_CHIPS,))])(part)


def _pair_sum(a, b, name):
    n, rows, cols = a.shape
    tr = max(t for t in range(16, 513, 16) if rows % t == 0)

    def body(a_ref, b_ref, o_ref):
        o_ref[...] = (a_ref[...].astype(F32) + b_ref[...].astype(F32)).astype(o_ref.dtype)

    blk = pl.BlockSpec((None, tr, cols), lambda q, i: (q, i, 0))
    return pl.pallas_call(body, name=name, grid=(n, rows // tr), in_specs=[blk, blk], out_specs=blk,
                          out_shape=jax.ShapeDtypeStruct(a.shape, a.dtype), compiler_params=_params(2))(a, b)


def _adamw(w, g, m, v):
    m = ADAM_B1 * m + (1.0 - ADAM_B1) * g
    v = ADAM_B2 * v + (1.0 - ADAM_B2) * (g * g)
    m_hat = m / (1.0 - ADAM_B1 ** ADAM_STEP)
    v_hat = v / (1.0 - ADAM_B2 ** ADAM_STEP)
    return -ADAM_LR * (m_hat / (jnp.sqrt(v_hat) + ADAM_EPS) + ADAM_WD * w), m, v


def _sum_parts_adamw(parts, w, m, v, name):
    n_parts, rows, cols = parts.shape
    tr = max(t for t in range(16, 257, 16) if rows % t == 0)

    def body(p_ref, w_ref, m_ref, v_ref, g_ref, d_ref, mo_ref, vo_ref):
        g = p_ref[0].astype(F32)
        for j in range(1, n_parts):
            g = g + p_ref[j].astype(F32)
        g_ref[...] = g
        d_ref[...], mo_ref[...], vo_ref[...] = _adamw(w_ref[...], g, m_ref[...], v_ref[...])

    row = _rows3(tr, cols)
    out = jax.ShapeDtypeStruct((1, rows, cols), F32)
    return pl.pallas_call(
        body, name=name, grid=(rows // tr,),
        in_specs=[pl.BlockSpec((n_parts, tr, cols), lambda i: (0, i, 0)), row, row, row],
        out_specs=[row, row, row, row], out_shape=[out, out, out, out], compiler_params=_params(1),
    )(parts, w, m, v)


def _sum_parts_adamw_whole(parts, w, m, v, name):
    def body(p_ref, w_ref, m_ref, v_ref, g_ref, d_ref, mo_ref, vo_ref):
        g = p_ref[0:1].astype(F32)
        for j in range(1, N_DEV):
            g = g + p_ref[j:j + 1].astype(F32)
        g_ref[...] = g
        d_ref[...], mo_ref[...], vo_ref[...] = _adamw(w_ref[...], g, m_ref[...], v_ref[...])

    out = jax.ShapeDtypeStruct(w.shape, F32)
    return pl.pallas_call(body, name=name, out_shape=[out] * 4,
                          compiler_params=pltpu.CompilerParams(vmem_limit_bytes=VMEM_LIMIT))(parts, w, m, v)


def _adamw_many(gs, ws, ms, vs, name):
    n = len(gs)

    def body(*refs):
        for k in range(n):
            g_ref, w_ref, m_ref, v_ref = (refs[q * n + k] for q in range(4))
            d_ref, mo_ref, vo_ref = (refs[(4 + q) * n + k] for q in range(3))
            d_ref[...], mo_ref[...], vo_ref[...] = _adamw(w_ref[...], g_ref[...], m_ref[...], v_ref[...])

    out = [jax.ShapeDtypeStruct(w.shape, F32) for w in ws]
    res = pl.pallas_call(body, name=name, out_shape=out * 3,
                         compiler_params=pltpu.CompilerParams(vmem_limit_bytes=VMEM_LIMIT))(*gs, *ws, *ms, *vs)
    return res[:n], res[n:2 * n], res[2 * n:]


def _ada_w_grad_adamw(c_all, dmod_rows, w, m, v, name):
    def body(c_ref, dm_ref, w_ref, m_ref, v_ref, g_ref, d_ref, mo_ref, vo_ref):
        cv = c_ref[...]
        g = lax.dot_general(cv * _sigmoid(cv), dm_ref[...], TN, precision=lax.Precision.HIGHEST,
                            preferred_element_type=F32)
        g_ref[...] = g
        d_ref[...], mo_ref[...], vo_ref[...] = _adamw(w_ref[...], g, m_ref[...], v_ref[...])

    n_layers, d, cols = w.shape
    layer = pl.BlockSpec((None, d, cols), lambda l: (l, 0, 0))
    out = jax.ShapeDtypeStruct(w.shape, F32)
    return pl.pallas_call(
        body, name=name, grid=(n_layers,),
        in_specs=[pl.BlockSpec(c_all.shape, lambda l: (0, 0)),
                  pl.BlockSpec((None,) + dmod_rows.shape[1:], lambda l: (l, 0, 0)), layer, layer, layer],
        out_specs=[layer] * 4, out_shape=[out] * 4, compiler_params=_params(1),
    )(c_all, dmod_rows, w, m, v)


REPLICATED = ("ln_g", "ln_b", "gmlp_norm_g", "gmlp_norm_b", "gmlp_ws", "gmlp_bs", "pool_b", "pool_scale",
              "mla_kv_norm_g", "mla_w_uk", "mla_w_uv")
CHUNK_ROWS, ADA_ROW, QNORM_ROW, LOSS_ROW, REP_ROWS = 73, 73, 74, 75, 80
UQ_ROWS, POOLW_ROWS = 96, 32


def _pad_rows(flat2d, rows):
    n, k = flat2d.shape
    return jnp.pad(flat2d, ((0, 0), (0, rows * LANES - k))).reshape(n, rows, LANES)


def _ada_cols_rows(vec):
    return _pad_rows(vec.reshape(2, N_DEV, -1).transpose(1, 0, 2).reshape(N_DEV, -1), 1)


def _unpack_replicated(rep, shapes):
    chunk = sum(s[1] for s in shapes) // N_DEV
    flat, off, out = rep[:, :CHUNK_ROWS].reshape(N_DEV, -1)[:, :chunk].reshape(-1), 0, {}
    for n, size, shape in shapes:
        out[n] = flat[off:off + size].reshape(shape)
        off += size
    cols = 3 * D_MODEL // N_DEV
    out["ada_b"] = rep[:, ADA_ROW, :2 * cols].reshape(N_DEV, 2, cols).transpose(1, 0, 2).reshape(2, -1)
    return out


def kernel(x, c, positions, ada_w, ada_b, ln_g, ln_b, e_w_in, gmlp_norm_g, gmlp_norm_b, gmlp_ws, gmlp_bs, pool_w, pool_b, pool_scale, e_w_out, o_w_in, mla_q_norm_g, mla_kv_norm_g, mla_w_uq, mla_w_uk, mla_w_uv, o_w_out, loss_target, m_ada_w, m_ada_b, m_ln_g, m_ln_b, m_e_w_in, m_gmlp_norm_g, m_gmlp_norm_b, m_gmlp_ws, m_gmlp_bs, m_pool_w, m_pool_b, m_pool_scale, m_e_w_out, m_o_w_in, m_mla_q_norm_g, m_mla_kv_norm_g, m_mla_w_uq, m_mla_w_uk, m_mla_w_uv, m_o_w_out, v_ada_w, v_ada_b, v_ln_g, v_ln_b, v_e_w_in, v_gmlp_norm_g, v_gmlp_norm_b, v_gmlp_ws, v_gmlp_bs, v_pool_w, v_pool_b, v_pool_scale, v_e_w_out, v_o_w_in, v_mla_q_norm_g, v_mla_kv_norm_g, v_mla_w_uq, v_mla_w_uk, v_mla_w_uv, v_o_w_out):
    w_in = dict(ada_w=ada_w, ada_b=ada_b, ln_g=ln_g, ln_b=ln_b, e_w_in=e_w_in, gmlp_norm_g=gmlp_norm_g,
                gmlp_norm_b=gmlp_norm_b, gmlp_ws=gmlp_ws, gmlp_bs=gmlp_bs, pool_w=pool_w, pool_b=pool_b,
                pool_scale=pool_scale, e_w_out=e_w_out, o_w_in=o_w_in, mla_q_norm_g=mla_q_norm_g,
                mla_kv_norm_g=mla_kv_norm_g, mla_w_uq=mla_w_uq, mla_w_uk=mla_w_uk, mla_w_uv=mla_w_uv, o_w_out=o_w_out)
    m_in = dict(ada_w=m_ada_w, ada_b=m_ada_b, ln_g=m_ln_g, ln_b=m_ln_b, e_w_in=m_e_w_in, gmlp_norm_g=m_gmlp_norm_g,
                gmlp_norm_b=m_gmlp_norm_b, gmlp_ws=m_gmlp_ws, gmlp_bs=m_gmlp_bs, pool_w=m_pool_w, pool_b=m_pool_b,
                pool_scale=m_pool_scale, e_w_out=m_e_w_out, o_w_in=m_o_w_in, mla_q_norm_g=m_mla_q_norm_g,
                mla_kv_norm_g=m_mla_kv_norm_g, mla_w_uq=m_mla_w_uq, mla_w_uk=m_mla_w_uk, mla_w_uv=m_mla_w_uv,
                o_w_out=m_o_w_out)
    v_in = dict(ada_w=v_ada_w, ada_b=v_ada_b, ln_g=v_ln_g, ln_b=v_ln_b, e_w_in=v_e_w_in, gmlp_norm_g=v_gmlp_norm_g,
                gmlp_norm_b=v_gmlp_norm_b, gmlp_ws=v_gmlp_ws, gmlp_bs=v_gmlp_bs, pool_w=v_pool_w, pool_b=v_pool_b,
                pool_scale=v_pool_scale, e_w_out=v_e_w_out, o_w_in=v_o_w_in, mla_q_norm_g=v_mla_q_norm_g,
                mla_kv_norm_g=v_mla_kv_norm_g, mla_w_uq=v_mla_w_uq, mla_w_uk=v_mla_w_uk, mla_w_uv=v_mla_w_uv,
                o_w_out=v_o_w_out)
    names = list(w_in)
    seq = x.shape[1]
    d = D_MODEL
    me = 4 * lax.axis_index("x") + 2 * lax.axis_index("y") + lax.axis_index("c")
    ada_cols = ada_w.shape[2]

    ada_b_cols = lax.dynamic_slice_in_dim(ada_b, me * ada_cols, ada_cols, axis=1)
    slab_row = lax.broadcasted_iota(jnp.int32, (8, d), 0)
    slab = jnp.where(slab_row == 0, c, jnp.where(slab_row == 1, jnp.pad(mla_q_norm_g, ((0, 0), (0, d - 32))), 0.0))
    c_all, mod, w_in_e3, pool_w3 = _adaln_exchange(
        slab, ada_w, jnp.broadcast_to(ada_b_cols[:, None, :], (2, 8, ada_cols)),
        [e_w_in[0].astype(BF16), pool_w.astype(BF16).reshape(POOLW_ROWS, LANES)], "adaln_exchange")
    h0 = _modulate(x, mod[0], "modulate0")
    proj0, o_in3 = _matmul_cols_nn(h0, w_in_e3, BF16, 512, "even_in", side=([o_w_in[0].astype(BF16)], "gather"))
    o_in_full = o_in3.transpose(1, 0, 2).reshape(d, ODD_IN)
    w_in_o = jnp.concatenate([o_in_full[:, :448], jnp.zeros((d, 64), BF16), o_in_full[:, 448:]], axis=1)
    pool_w_full = pool_w3.reshape(N_DEV, 4, 32, 256).transpose(1, 0, 2, 3).reshape(4, 256, 256)
    g_q = c_all.reshape(N_DEV, 8, d)[:, 1, :32].reshape(1, MLA_Q_RANK)

    ws, bs_col = gmlp_ws[0], gmlp_bs[0].reshape(GMLP_HEADS, GMLP_BLOCK, 1)
    wuk2, wuv2 = mla_w_uk[0].reshape(MLA_KV_RANK, -1), mla_w_uv[0].reshape(MLA_KV_RANK, -1)
    inv = 1.0 / (ROPE_THETA ** (jnp.arange(0, MLA_ROPE, 2, dtype=F32) / MLA_ROPE))
    ang = positions[0].astype(F32)[:, None] * inv
    cosp = jnp.tile(jnp.cos(ang), (1, 4))
    sinp = jnp.tile(jnp.concatenate([-jnp.sin(ang), jnp.sin(ang)], axis=1), (1, 2))

    mix0, w_out_e3 = _even_fwd(proj0, ws, bs_col, gmlp_norm_g, gmlp_norm_b, pool_w_full, pool_b, pool_scale, "even_mix",
                               side=([e_w_out[0].astype(BF16)], "gather"))
    w_out_e = w_out_e3.reshape(-1, d)
    y0, uq3 = _matmul([(mix0, w_out_e)], "nn", F32, seq, d, 512, 1024, "even_out",
                      side=([mla_w_uq.astype(BF16).reshape(UQ_ROWS, LANES)], "gather"))
    uq_full = uq3.reshape(MLA_Q_RANK, MLA_HEADS, MLA_NOPE + MLA_ROPE)
    w_uq_n = uq_full[:, :, :MLA_NOPE].reshape(MLA_Q_RANK, -1)
    w_uq_r = uq_full[:, :, MLA_NOPE:].reshape(MLA_Q_RANK, -1)
    w_uq = jnp.concatenate([w_uq_n, w_uq_r], axis=1)
    x1, h1 = _resid_ln(x, y0, mod[0], ln_g[0:1], ln_b[0:1], mod[1], "resid_ln0")

    (proj1,) = _matmul([(h1, w_in_o)], "nn", BF16, seq, ODD_IN_PAD, 512, ODD_IN_PAD, "odd_in")
    qn, kp = _mla_prep(proj1, cosp, sinp, g_q, mla_kv_norm_g, "mla_prep")
    (q_up,) = _matmul([(qn, w_uq)], "nn", BF16, seq, 3072, 512, 3072, "q_up")
    qp = _q_heads(q_up, cosp, sinp, wuk2, "q_heads")
    o_lat, lse, w_out_o3 = _attn_fwd(qp, kp, "attn_fwd", side=([o_w_out[0].astype(BF16)], "gather"))
    w_out_o = w_out_o3.reshape(-1, d)
    gated = _o_gate(o_lat, proj1, wuv2, "o_gate")
    (y1,) = _matmul([(gated, w_out_o)], "nn", F32, seq, d, 512, 1024, "odd_out")

    dy1, dxres1, red2 = _final_ln_loss_bwd(x1, y1, mod[1], ln_g[1:2], ln_b[1:2], loss_target, "final_ln_loss")
    (dgated,) = _matmul([(dy1, w_out_o)], "nt", BF16, seq, MLA_WIDTH, 512, MLA_WIDTH, "odd_out_dx")
    (g_w_out_o,) = _matmul([(gated, dy1)], "tn", BF16, MLA_WIDTH, d, 256, d, "odd_out_dw")
    dproj1_z, do_lat, g_wuv = _o_gate_bwd(dgated, o_lat, proj1, wuv2, "o_gate_bwd")
    dqp, dkp, dvv, r_o_out = _attn_bwd(qp, kp, o_lat, do_lat, lse, "attn_bwd",
                                       side=([g_w_out_o.reshape(N_DEV, -1, d)], "devices"))
    dq_nope, dq_rope, g_wuk = _q_heads_bwd(dqp, q_up, cosp, sinp, wuk2, "q_heads_bwd")
    (dqn,) = _matmul([(dq_nope, w_uq_n), (dq_rope, w_uq_r)], "nt", F32, seq, MLA_Q_RANK, 512, 256, "q_up_dx")
    (g_wuq_n,) = _matmul([(qn, dq_nope)], "tn", F32, MLA_Q_RANK, MLA_WIDTH, 256, 512, "q_up_dw_nope")
    (g_wuq_r,) = _matmul([(qn, dq_rope)], "tn", F32, MLA_Q_RANK, 1024, 256, 512, "q_up_dw_rope")
    dproj1, red_mla = _mla_prep_bwd(proj1, dqn, dkp, dvv, cosp, sinp, g_q, mla_kv_norm_g, dproj1_z, "mla_prep_bwd")
    (dh1,) = _matmul([(dproj1, w_in_o)], "nt", F32, seq, d, 512, d, "odd_in_dx")
    part_uq = jnp.concatenate([g_wuq_n.reshape(MLA_Q_RANK, MLA_HEADS, MLA_NOPE),
                               g_wuq_r.reshape(MLA_Q_RANK, MLA_HEADS, MLA_ROPE)], axis=2).astype(BF16).reshape(
                                   (N_DEV,) + mla_w_uq.shape[1:])
    (g_w_in_o,) = _matmul([(h1, dproj1)], "tn", BF16, d, ODD_IN_PAD, 256, ODD_IN_PAD // 2, "odd_in_dw", n_outer=True)
    part_o_in = jnp.concatenate([g_w_in_o[:, :448], g_w_in_o[:, 512:]], axis=1).reshape(d, N_DEV, -1).transpose(1, 0, 2)
    dy0, dxres0, red1 = _mid_bwd(dh1, dxres1, x, y0, mod[0], mod[1], ln_g[0:1], ln_b[0:1], "mid_bwd")
    dmix, r_uq = _matmul([(dy0, w_out_e)], "nt", BF16, seq, 2048, 512, 2048, "even_out_dx", side=([part_uq], "devices"))
    (g_w_out_e,) = _matmul([(mix0, dy0)], "tn", BF16, 2048, d, 256, d, "even_out_dw")
    dproj0, g_ws, g_bs, g_ng, g_nb, g_pw, g_pb, g_ps, r_o_in = _even_bwd(
        proj0, dmix, ws, bs_col, gmlp_norm_g, gmlp_norm_b, pool_w_full, pool_b, pool_scale, "even_mix_bwd",
        side=([part_o_in], "devices"))
    part_pw = g_pw.reshape(4, N_DEV, 32, 256).transpose(1, 0, 2, 3)
    part_e_in, r_e_out, r_pw = _matmul_cols_tn(h0, dproj0, w_in_e3.shape[2], BF16, 512, "even_in_dw",
                                               side=([g_w_out_e.reshape(N_DEV, -1, d), part_pw], "devices"))
    mine = lax.dynamic_index_in_dim(part_e_in.reshape((N_CHIPS, 2) + part_e_in.shape[1:]), lax.axis_index("c"), 1, False)
    chip_e_in = _pair_sum(mine, _sibling_swap(part_e_in, "e_in_sibling_swap"), "e_in_pair_sum")
    grad_x, red0, r_e_in = _first_bwd(dproj0, w_in_e3, dxres0, x, mod[0], "even_in_dx", side=([chip_e_in], "chips"))

    t_mask = lax.broadcasted_iota(jnp.int32, (GMLP_BLOCK, GMLP_BLOCK), 0) // CHUNK
    s_mask = lax.broadcasted_iota(jnp.int32, (GMLP_BLOCK, GMLP_BLOCK), 1) // CHUNK
    part = {
        "ln_g": jnp.stack([red1[2], red2[0]]), "ln_b": jnp.stack([red1[3], red2[1]]),
        "gmlp_norm_g": g_ng, "gmlp_norm_b": g_nb,
        "gmlp_ws": jnp.where(s_mask <= t_mask, g_ws, 0.0), "gmlp_bs": g_bs,
        "pool_b": g_pb, "pool_scale": g_ps, "mla_kv_norm_g": red_mla[1, :MLA_KV_RANK],
        "mla_w_uk": g_wuk, "mla_w_uv": g_wuv,
    }
    dmod = jnp.stack([jnp.concatenate([red0[1], red0[0], red1[4]]),
                      jnp.concatenate([red1[1], red1[0], red2[2]])])

    loss_row = jnp.pad(jnp.broadcast_to((0.5 / d * jnp.sum(red2[3])).reshape(1, 1, 1), (N_DEV, 1, 1)),
                       ((0, 0), (0, 0), (0, LANES - 1)))
    part_small = jnp.concatenate([
        _pad_rows(jnp.concatenate([part[n].reshape(-1) for n in REPLICATED]).reshape(N_DEV, -1), CHUNK_ROWS),
        jnp.pad(jnp.concatenate([_ada_cols_rows(dmod), _pad_rows(red_mla[0].reshape(N_DEV, -1), 1), loss_row], axis=1),
                ((0, 0), (0, REP_ROWS - LOSS_ROW - 1), (0, 0)))], axis=1)
    (r_small,) = _scatter_parts([part_small], "grad_scatter")
    small_sum = _sum_parts(r_small, "small_sum")
    loss = small_sum[LOSS_ROW, 0]
    (rep_sum,) = _all_gather([small_sum], "replicated_gather")

    res = {"e_w_in": _sum_parts_adamw(r_e_in, e_w_in, m_e_w_in, v_e_w_in, "adamw_e_w_in"),
           "o_w_in": _sum_parts_adamw(r_o_in, o_w_in, m_o_w_in, v_o_w_in, "adamw_o_w_in"),
           "e_w_out": _sum_parts_adamw(r_e_out, e_w_out, m_e_w_out, v_e_w_out, "adamw_e_w_out"),
           "o_w_out": _sum_parts_adamw(r_o_out, o_w_out, m_o_w_out, v_o_w_out, "adamw_o_w_out"),
           "mla_w_uq": _sum_parts_adamw_whole(r_uq, mla_w_uq, m_mla_w_uq, v_mla_w_uq, "adamw_w_uq"),
           "pool_w": _sum_parts_adamw_whole(r_pw, pool_w, m_pool_w, v_pool_w, "adamw_pool_w")}
    grads = _unpack_replicated(rep_sum, [(n, w_in[n].size, w_in[n].shape) for n in REPLICATED])
    grads["mla_q_norm_g"] = small_sum[QNORM_ROW:QNORM_ROW + 1, :32]
    small_names = list(grads)
    deltas, new_ms, new_vs = _adamw_many([grads[n] for n in small_names], [w_in[n] for n in small_names],
                                         [m_in[n] for n in small_names], [v_in[n] for n in small_names], "small_adamw")
    for k, n in enumerate(small_names):
        res[n] = [grads[n], deltas[k], new_ms[k], new_vs[k]]
    dmod_all = r_small[:, ADA_ROW, :2 * ada_cols].reshape(N_DEV, 2, ada_cols).transpose(1, 0, 2)
    dmod_rows = jnp.pad(dmod_all[:, :, None, :], ((0, 0), (0, 0), (0, 7), (0, 0))).reshape(2, 8 * N_DEV, ada_cols)
    res["ada_w"] = _ada_w_grad_adamw(c_all, dmod_rows, ada_w, m_ada_w, v_ada_w, "ada_w_adamw")

    return (loss, grad_x, *[res[n][0] for n in names], *[res[n][1] for n in names],
            *[res[n][2] for n in names], *[res[n][3] for n in names])
```
